```python
import jax, jax.numpy as jnp
from jax import lax
import numpy as np

D_MODEL = 1024
BATCH = 8
SEQ = 8192
DEPTH = 1

HEAD_DIM = 64
DIL_GROUPS = ((128, 1), (512, 4), (2048, 16))
N_DIL = len(DIL_GROUPS)
DIL_HEADS = 8
SWA_WINDOW = 128
SWA_Q_HEADS = 8
SWA_KV_HEADS = 2
SWA_GROUP = SWA_Q_HEADS // SWA_KV_HEADS
BLOCK = 128
ROPE_THETA = 10000.0
MEM_LEN = 256
X_HEADS = 4
X_HEAD_DIM = D_MODEL // X_HEADS
D_FF = 4 * D_MODEL
EPS = 1e-6
N_BRANCH = 2

DIL_WIDTH = N_DIL * DIL_HEADS * HEAD_DIM
DIL_OUT = DIL_HEADS * HEAD_DIM
SWA_Q_WIDTH = SWA_Q_HEADS * HEAD_DIM
SWA_KV_WIDTH = SWA_KV_HEADS * HEAD_DIM
GATE_WIDTH = N_BRANCH * D_MODEL
IN_WIDTH = 3 * DIL_WIDTH + SWA_Q_WIDTH + 2 * SWA_KV_WIDTH + GATE_WIDTH

kernel_name = "hybrid_dilated_swa_sink_gated_block"


def rmsnorm(x, g):
    xf = x.astype(jnp.float32)
    y = xf * lax.rsqrt(jnp.mean(xf * xf, axis=-1, keepdims=True) + EPS)
    return (y * g.astype(jnp.float32)).astype(x.dtype)


def rope(x, pos):
    half = x.shape[-1] // 2
    inv = ROPE_THETA ** (-jnp.arange(half, dtype=jnp.float32) / half)
    ang = pos.astype(jnp.float32)[..., None] * inv
    cos = jnp.cos(ang)[:, :, None, :]
    sin = jnp.sin(ang)[:, :, None, :]
    xf = x.astype(jnp.float32)
    x1, x2 = xf[..., :half], xf[..., half:]
    out = jnp.concatenate([x1 * cos - x2 * sin, x2 * cos + x1 * sin], axis=-1)
    return out.astype(x.dtype)


def banded_attention(q, k, v, window, sink=None):
    assert window <= BLOCK
    N, L, Hkv, G, D = q.shape
    nb = -(-L // BLOCK)
    Lp = nb * BLOCK
    pad = Lp - L
    if pad:
        q = jnp.pad(q, ((0, 0), (0, pad), (0, 0), (0, 0), (0, 0)))
        k = jnp.pad(k, ((0, 0), (0, pad), (0, 0), (0, 0)))
        v = jnp.pad(v, ((0, 0), (0, pad), (0, 0), (0, 0)))
    qb = q.reshape(N, nb, BLOCK, Hkv, G, D)
    kb = k.reshape(N, nb, BLOCK, Hkv, D)
    vb = v.reshape(N, nb, BLOCK, Hkv, D)
    shift = ((0, 0), (1, 0), (0, 0), (0, 0), (0, 0))
    kcat = jnp.concatenate([jnp.pad(kb, shift)[:, :-1], kb], axis=2)
    vcat = jnp.concatenate([jnp.pad(vb, shift)[:, :-1], vb], axis=2)
    s = jnp.einsum('nbqhgd,nbkhd->nbhgqk', qb, kcat,
                   preferred_element_type=jnp.float32) * (D ** -0.5)
    qi = jnp.arange(BLOCK)[:, None] + BLOCK
    ki = jnp.arange(2 * BLOCK)[None, :]
    dist = qi - ki
    key_glob = jnp.arange(nb)[:, None, None] * BLOCK - BLOCK + ki[None]
    mask = ((dist >= 0) & (dist <= window))[None] & (key_glob >= 0)
    s = jnp.where(mask[None, :, None, None], s, -jnp.inf)
    m = jnp.max(s, axis=-1, keepdims=True)
    if sink is not None:
        sink_b = sink.astype(jnp.float32)[None, None, :, :, None, None]
        m = jnp.maximum(m, sink_b)
    p = jnp.exp(s - m)
    den = jnp.sum(p, axis=-1, keepdims=True)
    if sink is not None:
        den = den + jnp.exp(sink_b - m)
    lse = (m + jnp.log(den))[..., 0]
    o = jnp.einsum('nbhgqk,nbkhd->nbqhgd', p / den, vcat.astype(jnp.float32))
    o = o.reshape(N, Lp, Hkv, G, D)[:, :L].astype(v.dtype)
    lse = lse.transpose(0, 1, 4, 2, 3).reshape(N, Lp, Hkv, G)[:, :L]
    return o, lse


def dilated_attention(q, k, v):
    B, S, _, H, D = q.shape
    outs, lses = [], []
    for gi, (w, d) in enumerate(DIL_GROUPS):
        Ls = S // d
        def to_res(a):
            return a.reshape(B, Ls, d, H, D).swapaxes(1, 2).reshape(B * d, Ls, H, D)
        o, lse = banded_attention(to_res(q[:, :, gi])[:, :, :, None, :],
                                  to_res(k[:, :, gi]), to_res(v[:, :, gi]), w // d)
        o = o[:, :, :, 0].reshape(B, d, Ls, H, D).swapaxes(1, 2).reshape(B, S, H, D)
        lse = lse[..., 0].reshape(B, d, Ls, H).swapaxes(1, 2).reshape(B, S, H)
        outs.append(o)
        lses.append(lse)
    O = jnp.stack(outs, axis=2).astype(jnp.float32)
    alpha = jax.nn.softmax(jnp.stack(lses, axis=2), axis=2)
    return jnp.sum(alpha[..., None] * O, axis=2).astype(q.dtype)


def mixer_sublayer(x, positions, g_mix, w_in, b_gate, sink, w_branch_a, w_branch_b, w_out):
    B, S, _ = x.shape
    h = rmsnorm(x, g_mix)
    proj = jnp.einsum('bsd,de->bse', h, w_in)
    c = np.cumsum([DIL_WIDTH, DIL_WIDTH, DIL_WIDTH, SWA_Q_WIDTH, SWA_KV_WIDTH, SWA_KV_WIDTH])
    qa, ka, va, qb, kb, vb, gpre = jnp.split(proj, [int(i) for i in c], axis=-1)
    na = N_DIL * DIL_HEADS
    qa = rope(qa.reshape(B, S, na, HEAD_DIM), positions).reshape(B, S, N_DIL, DIL_HEADS, HEAD_DIM)
    ka = rope(ka.reshape(B, S, na, HEAD_DIM), positions).reshape(B, S, N_DIL, DIL_HEADS, HEAD_DIM)
    va = va.reshape(B, S, N_DIL, DIL_HEADS, HEAD_DIM)
    oa = dilated_attention(qa, ka, va).reshape(B, S, DIL_OUT)
    qb = rope(qb.reshape(B, S, SWA_Q_HEADS, HEAD_DIM), positions)
    kb = rope(kb.reshape(B, S, SWA_KV_HEADS, HEAD_DIM), positions)
    vb = vb.reshape(B, S, SWA_KV_HEADS, HEAD_DIM)
    ob, _ = banded_attention(qb.reshape(B, S, SWA_KV_HEADS, SWA_GROUP, HEAD_DIM), kb, vb,
                             SWA_WINDOW, sink=sink.reshape(SWA_KV_HEADS, SWA_GROUP))
    ob = ob.reshape(B, S, SWA_Q_WIDTH)
    gates = jax.nn.sigmoid((gpre + b_gate.reshape(GATE_WIDTH)).astype(jnp.float32))
    gates = gates.reshape(B, S, N_BRANCH, D_MODEL).astype(x.dtype)
    ya = jnp.einsum('bse,ed->bsd', oa, w_branch_a)
    yb = jnp.einsum('bse,ed->bsd', ob, w_branch_b)
    merged = gates[:, :, 0] * ya + gates[:, :, 1] * yb
    return x + jnp.einsum('bsd,de->bse', merged, w_out)


def cross_sublayer(x, mem, g_cross, g_mem, w_cq, w_ckv, w_co):
    B, S, _ = x.shape
    M = mem.shape[1]
    hc = rmsnorm(x, g_cross)
    mn = rmsnorm(mem, g_mem)
    q = jnp.einsum('bsd,de->bse', hc, w_cq).reshape(B, S, X_HEADS, X_HEAD_DIM)
    k, v = jnp.split(jnp.einsum('bmd,de->bme', mn, w_ckv), 2, axis=-1)
    k = k.reshape(B, M, X_HEADS, X_HEAD_DIM)
    v = v.reshape(B, M, X_HEADS, X_HEAD_DIM)
    s = jnp.einsum('bshd,bmhd->bhsm', q, k, preferred_element_type=jnp.float32) * (X_HEAD_DIM ** -0.5)
    p = jax.nn.softmax(s, axis=-1)
    o = jnp.einsum('bhsm,bmhd->bshd', p, v.astype(jnp.float32)).astype(x.dtype).reshape(B, S, D_MODEL)
    return x + jnp.einsum('bse,ed->bsd', o, w_co)


def mlp_sublayer(x, g_mlp, w_1, w_2):
    h = rmsnorm(x, g_mlp)
    a = jax.nn.relu(jnp.einsum('bsd,df->bsf', h, w_1))
    return x + jnp.einsum('bsf,fd->bsd', a * a, w_2)


def _fwd_setup_inputs(seed: int = 0) -> dict:
    key = jax.random.key(seed)
    ks = jax.random.split(key, 20)
    f32 = jnp.float32
    def nrm(k, shape, fan_in):
        return jax.random.normal(k, shape, f32) * (fan_in ** -0.5)
    def gain(k, shape):
        return 1.0 + 0.01 * jax.random.normal(k, shape, f32)
    x = jax.random.normal(ks[0], (BATCH, SEQ, D_MODEL), f32)
    mem = jax.random.normal(ks[1], (BATCH, MEM_LEN, D_MODEL), f32)
    start = jax.random.randint(ks[2], (BATCH,), 0, 4096, dtype=jnp.int32)
    positions = start[:, None] + jnp.arange(SEQ, dtype=jnp.int32)[None, :]
    return {
        "x": x,
        "mem": mem,
        "positions": positions,
        "g_mix": gain(ks[3], (DEPTH, D_MODEL)),
        "w_in": nrm(ks[4], (DEPTH, D_MODEL, IN_WIDTH), D_MODEL),
        "b_gate": 0.01 * jax.random.normal(ks[5], (DEPTH, N_BRANCH, D_MODEL), f32),
        "sink": 0.5 * jax.random.normal(ks[6], (DEPTH, SWA_Q_HEADS), f32),
        "w_branch_a": nrm(ks[7], (DEPTH, DIL_OUT, D_MODEL), DIL_OUT),
        "w_branch_b": nrm(ks[8], (DEPTH, SWA_Q_WIDTH, D_MODEL), SWA_Q_WIDTH),
        "w_out": nrm(ks[9], (DEPTH, D_MODEL, D_MODEL), D_MODEL),
        "g_cross": gain(ks[10], (DEPTH, D_MODEL)),
        "g_mem": gain(ks[11], (DEPTH, D_MODEL)),
        "w_cq": nrm(ks[12], (DEPTH, D_MODEL, D_MODEL), D_MODEL),
        "w_ckv": nrm(ks[13], (DEPTH, D_MODEL, 2 * D_MODEL), D_MODEL),
        "w_co": nrm(ks[14], (DEPTH, D_MODEL, D_MODEL), D_MODEL),
        "g_mlp": gain(ks[15], (DEPTH, D_MODEL)),
        "w_1": nrm(ks[16], (DEPTH, D_MODEL, D_FF), D_MODEL),
        "w_2": nrm(ks[17], (DEPTH, D_FF, D_MODEL), D_FF),
        "g_final": gain(ks[18], (D_MODEL,)),
    }


def _fwd_reference(x, mem, positions, g_mix, w_in, b_gate, sink, w_branch_a, w_branch_b, w_out,
              g_cross, g_mem, w_cq, w_ckv, w_co, g_mlp, w_1, w_2, g_final):
    for l in range(DEPTH):
        x = mixer_sublayer(x, positions, g_mix[l], w_in[l], b_gate[l], sink[l],
                           w_branch_a[l], w_branch_b[l], w_out[l])
        x = cross_sublayer(x, mem, g_cross[l], g_mem[l], w_cq[l], w_ckv[l], w_co[l])
        x = mlp_sublayer(x, g_mlp[l], w_1[l], w_2[l])
    return rmsnorm(x, g_final)


import jax as _jax
import jax.numpy as _jnp

TWIN_FORMAT = 'train_step'
FWD_PARAMS = ['x', 'mem', 'positions', 'g_mix', 'w_in', 'b_gate', 'sink', 'w_branch_a', 'w_branch_b', 'w_out', 'g_cross', 'g_mem', 'w_cq', 'w_ckv', 'w_co', 'g_mlp', 'w_1', 'w_2', 'g_final']
TWIN_WEIGHTS = ['g_mix', 'w_in', 'b_gate', 'sink', 'w_branch_a', 'w_branch_b', 'w_out', 'g_cross', 'g_mem', 'w_cq', 'w_ckv', 'w_co', 'g_mlp', 'w_1', 'w_2', 'g_final']
TWIN_DIFF_INPUT = 'x'
TWIN_INPUTS = ['x', 'mem', 'positions', 'g_mix', 'w_in', 'b_gate', 'sink', 'w_branch_a', 'w_branch_b', 'w_out', 'g_cross', 'g_mem', 'w_cq', 'w_ckv', 'w_co', 'g_mlp', 'w_1', 'w_2', 'g_final', 'loss_target', 'm_g_mix', 'm_w_in', 'm_b_gate', 'm_sink', 'm_w_branch_a', 'm_w_branch_b', 'm_w_out', 'm_g_cross', 'm_g_mem', 'm_w_cq', 'm_w_ckv', 'm_w_co', 'm_g_mlp', 'm_w_1', 'm_w_2', 'm_g_final', 'v_g_mix', 'v_w_in', 'v_b_gate', 'v_sink', 'v_w_branch_a', 'v_w_branch_b', 'v_w_out', 'v_g_cross', 'v_g_mem', 'v_w_cq', 'v_w_ckv', 'v_w_co', 'v_g_mlp', 'v_w_1', 'v_w_2', 'v_g_final']
TWIN_OUTPUTS = ['loss', 'grad_x', 'grad_g_mix', 'grad_w_in', 'grad_b_gate', 'grad_sink', 'grad_w_branch_a', 'grad_w_branch_b', 'grad_w_out', 'grad_g_cross', 'grad_g_mem', 'grad_w_cq', 'grad_w_ckv', 'grad_w_co', 'grad_g_mlp', 'grad_w_1', 'grad_w_2', 'grad_g_final', 'delta_g_mix', 'delta_w_in', 'delta_b_gate', 'delta_sink', 'delta_w_branch_a', 'delta_w_branch_b', 'delta_w_out', 'delta_g_cross', 'delta_g_mem', 'delta_w_cq', 'delta_w_ckv', 'delta_w_co', 'delta_g_mlp', 'delta_w_1', 'delta_w_2', 'delta_g_final', 'new_m_g_mix', 'new_m_w_in', 'new_m_b_gate', 'new_m_sink', 'new_m_w_branch_a', 'new_m_w_branch_b', 'new_m_w_out', 'new_m_g_cross', 'new_m_g_mem', 'new_m_w_cq', 'new_m_w_ckv', 'new_m_w_co', 'new_m_g_mlp', 'new_m_w_1', 'new_m_w_2', 'new_m_g_final', 'new_v_g_mix', 'new_v_w_in', 'new_v_b_gate', 'new_v_sink', 'new_v_w_branch_a', 'new_v_w_branch_b', 'new_v_w_out', 'new_v_g_cross', 'new_v_g_mem', 'new_v_w_cq', 'new_v_w_ckv', 'new_v_w_co', 'new_v_g_mlp', 'new_v_w_1', 'new_v_w_2', 'new_v_g_final']
TWIN_LEAF_KINDS = {'loss': 'loss', 'grad_x': 'grad_x', 'grad_g_mix': 'grad_w', 'grad_w_in': 'grad_w', 'grad_b_gate': 'grad_w', 'grad_sink': 'grad_w', 'grad_w_branch_a': 'grad_w', 'grad_w_branch_b': 'grad_w', 'grad_w_out': 'grad_w', 'grad_g_cross': 'grad_w', 'grad_g_mem': 'grad_w', 'grad_w_cq': 'grad_w', 'grad_w_ckv': 'grad_w', 'grad_w_co': 'grad_w', 'grad_g_mlp': 'grad_w', 'grad_w_1': 'grad_w', 'grad_w_2': 'grad_w', 'grad_g_final': 'grad_w', 'delta_g_mix': 'delta_w', 'delta_w_in': 'delta_w', 'delta_b_gate': 'delta_w', 'delta_sink': 'delta_w', 'delta_w_branch_a': 'delta_w', 'delta_w_branch_b': 'delta_w', 'delta_w_out': 'delta_w', 'delta_g_cross': 'delta_w', 'delta_g_mem': 'delta_w', 'delta_w_cq': 'delta_w', 'delta_w_ckv': 'delta_w', 'delta_w_co': 'delta_w', 'delta_g_mlp': 'delta_w', 'delta_w_1': 'delta_w', 'delta_w_2': 'delta_w', 'delta_g_final': 'delta_w', 'new_m_g_mix': 'new_m', 'new_m_w_in': 'new_m', 'new_m_b_gate': 'new_m', 'new_m_sink': 'new_m', 'new_m_w_branch_a': 'new_m', 'new_m_w_branch_b': 'new_m', 'new_m_w_out': 'new_m', 'new_m_g_cross': 'new_m', 'new_m_g_mem': 'new_m', 'new_m_w_cq': 'new_m', 'new_m_w_ckv': 'new_m', 'new_m_w_co': 'new_m', 'new_m_g_mlp': 'new_m', 'new_m_w_1': 'new_m', 'new_m_w_2': 'new_m', 'new_m_g_final': 'new_m', 'new_v_g_mix': 'new_v', 'new_v_w_in': 'new_v', 'new_v_b_gate': 'new_v', 'new_v_sink': 'new_v', 'new_v_w_branch_a': 'new_v', 'new_v_w_branch_b': 'new_v', 'new_v_w_out': 'new_v', 'new_v_g_cross': 'new_v', 'new_v_g_mem': 'new_v', 'new_v_w_cq': 'new_v', 'new_v_w_ckv': 'new_v', 'new_v_w_co': 'new_v', 'new_v_g_mlp': 'new_v', 'new_v_w_1': 'new_v', 'new_v_w_2': 'new_v', 'new_v_g_final': 'new_v'}


def _forward(args):
    return _fwd_reference(*[args[k] for k in FWD_PARAMS])


def _output_shape():
    out = _jax.eval_shape(lambda: _forward(_fwd_setup_inputs(0)))
    return out.shape, out.dtype

N_MICROBATCH = 1
ADAM_LR = 0.001
ADAM_B1 = 0.9
ADAM_B2 = 0.999
ADAM_EPS = 1e-08
ADAM_WD = 0.01
ADAM_STEP = 10
PER_EXAMPLE_BATCH_AXIS = {'x': 0, 'mem': 0, 'positions': 0, 'loss_target': 0}
SHARED_INPUTS = []
_WEIGHT_DTYPES = {'g_mix': _jnp.float32, 'w_in': _jnp.float32, 'b_gate': _jnp.float32, 'sink': _jnp.float32, 'w_branch_a': _jnp.float32, 'w_branch_b': _jnp.float32, 'w_out': _jnp.float32, 'g_cross': _jnp.float32, 'g_mem': _jnp.float32, 'w_cq': _jnp.float32, 'w_ckv': _jnp.float32, 'w_co': _jnp.float32, 'g_mlp': _jnp.float32, 'w_1': _jnp.float32, 'w_2': _jnp.float32, 'g_final': _jnp.float32}
MOMENT_SCALE = {'g_mix': 6.696864e-02, 'w_in': 2.458807e-02, 'b_gate': 1.100465e-02, 'sink': 3.879867e-02, 'w_branch_a': 2.710450e-02, 'w_branch_b': 2.915823e-02, 'w_out': 3.969830e-02, 'g_cross': 2.884125e-02, 'g_mem': 4.194306e-02, 'w_cq': 2.822795e-02, 'w_ckv': 2.872240e-02, 'w_co': 2.916136e-02, 'g_mlp': 2.236850e-01, 'w_1': 1.118744e-01, 'w_2': 1.979823e-01, 'g_final': 6.458380e+01}


def _to_microbatches(a, axis):
    t = _jnp.moveaxis(a, axis, 0)
    t = t.reshape((N_MICROBATCH, t.shape[0] // N_MICROBATCH) + t.shape[1:])
    return _jnp.moveaxis(t, 1, axis + 1)


def setup_inputs(seed: int = 0) -> dict:
    inp = _fwd_setup_inputs(seed)
    key = _jax.random.fold_in(_jax.random.key(seed), 7919)
    shape, _ = _output_shape()
    out = dict(inp)
    out["loss_target"] = _jax.random.normal(_jax.random.fold_in(key, 0), shape, _jnp.float32)
    for i, name in enumerate(TWIN_WEIGHTS):
        w = inp[name].astype(_jnp.float32)
        if MOMENT_SCALE is None:
            s = _jnp.sqrt(_jnp.mean(_jnp.square(w)) + 1e-30)
        else:
            s = MOMENT_SCALE[name]
        km, kv = _jax.random.split(_jax.random.fold_in(key, i + 1))
        out[name] = w
        out["m_" + name] = s * _jax.random.normal(km, w.shape, _jnp.float32)
        out["v_" + name] = (s * s) * _jax.random.uniform(kv, w.shape, _jnp.float32, 0.5, 1.5)
    if N_MICROBATCH > 1:
        for name, axis in PER_EXAMPLE_BATCH_AXIS.items():
            out[name] = _to_microbatches(out[name], axis)
    return {'x': out['x'], 'mem': out['mem'], 'positions': out['positions'], 'g_mix': out['g_mix'], 'w_in': out['w_in'], 'b_gate': out['b_gate'], 'sink': out['sink'], 'w_branch_a': out['w_branch_a'], 'w_branch_b': out['w_branch_b'], 'w_out': out['w_out'], 'g_cross': out['g_cross'], 'g_mem': out['g_mem'], 'w_cq': out['w_cq'], 'w_ckv': out['w_ckv'], 'w_co': out['w_co'], 'g_mlp': out['g_mlp'], 'w_1': out['w_1'], 'w_2': out['w_2'], 'g_final': out['g_final'], 'loss_target': out['loss_target'], 'm_g_mix': out['m_g_mix'], 'm_w_in': out['m_w_in'], 'm_b_gate': out['m_b_gate'], 'm_sink': out['m_sink'], 'm_w_branch_a': out['m_w_branch_a'], 'm_w_branch_b': out['m_w_branch_b'], 'm_w_out': out['m_w_out'], 'm_g_cross': out['m_g_cross'], 'm_g_mem': out['m_g_mem'], 'm_w_cq': out['m_w_cq'], 'm_w_ckv': out['m_w_ckv'], 'm_w_co': out['m_w_co'], 'm_g_mlp': out['m_g_mlp'], 'm_w_1': out['m_w_1'], 'm_w_2': out['m_w_2'], 'm_g_final': out['m_g_final'], 'v_g_mix': out['v_g_mix'], 'v_w_in': out['v_w_in'], 'v_b_gate': out['v_b_gate'], 'v_sink': out['v_sink'], 'v_w_branch_a': out['v_w_branch_a'], 'v_w_branch_b': out['v_w_branch_b'], 'v_w_out': out['v_w_out'], 'v_g_cross': out['v_g_cross'], 'v_g_mem': out['v_g_mem'], 'v_w_cq': out['v_w_cq'], 'v_w_ckv': out['v_w_ckv'], 'v_w_co': out['v_w_co'], 'v_g_mlp': out['v_g_mlp'], 'v_w_1': out['v_w_1'], 'v_w_2': out['v_w_2'], 'v_g_final': out['v_g_final']}


def _loss(weights, diff, rest, loss_target):
    with _jax.named_scope("forward"):
        args = {**rest, TWIN_DIFF_INPUT: diff, **{k: w.astype(_WEIGHT_DTYPES[k]) for k, w in weights.items()}}
        y = _forward(args)
    with _jax.named_scope("loss_head"):
        err = _jnp.square(y.astype(_jnp.float32) - loss_target)
        return 0.5 * _jnp.sum(_jnp.mean(err, axis=-1)) if err.ndim else 0.5 * err


def _adamw(w, g, m, v):
    m = ADAM_B1 * m + (1.0 - ADAM_B1) * g
    v = ADAM_B2 * v + (1.0 - ADAM_B2) * _jnp.square(g)
    m_hat = m / (1.0 - ADAM_B1 ** ADAM_STEP)
    v_hat = v / (1.0 - ADAM_B2 ** ADAM_STEP)
    delta = -ADAM_LR * (m_hat / (_jnp.sqrt(v_hat) + ADAM_EPS) + ADAM_WD * w)
    return delta, m, v


def reference(x, mem, positions, g_mix, w_in, b_gate, sink, w_branch_a, w_branch_b, w_out, g_cross, g_mem, w_cq, w_ckv, w_co, g_mlp, w_1, w_2, g_final, loss_target, m_g_mix, m_w_in, m_b_gate, m_sink, m_w_branch_a, m_w_branch_b, m_w_out, m_g_cross, m_g_mem, m_w_cq, m_w_ckv, m_w_co, m_g_mlp, m_w_1, m_w_2, m_g_final, v_g_mix, v_w_in, v_b_gate, v_sink, v_w_branch_a, v_w_branch_b, v_w_out, v_g_cross, v_g_mem, v_w_cq, v_w_ckv, v_w_co, v_g_mlp, v_w_1, v_w_2, v_g_final):
    given = dict(x=x, mem=mem, positions=positions, g_mix=g_mix, w_in=w_in, b_gate=b_gate, sink=sink, w_branch_a=w_branch_a, w_branch_b=w_branch_b, w_out=w_out, g_cross=g_cross, g_mem=g_mem, w_cq=w_cq, w_ckv=w_ckv, w_co=w_co, g_mlp=g_mlp, w_1=w_1, w_2=w_2, g_final=g_final, loss_target=loss_target, m_g_mix=m_g_mix, m_w_in=m_w_in, m_b_gate=m_b_gate, m_sink=m_sink, m_w_branch_a=m_w_branch_a, m_w_branch_b=m_w_branch_b, m_w_out=m_w_out, m_g_cross=m_g_cross, m_g_mem=m_g_mem, m_w_cq=m_w_cq, m_w_ckv=m_w_ckv, m_w_co=m_w_co, m_g_mlp=m_g_mlp, m_w_1=m_w_1, m_w_2=m_w_2, m_g_final=m_g_final, v_g_mix=v_g_mix, v_w_in=v_w_in, v_b_gate=v_b_gate, v_sink=v_sink, v_w_branch_a=v_w_branch_a, v_w_branch_b=v_w_branch_b, v_w_out=v_w_out, v_g_cross=v_g_cross, v_g_mem=v_g_mem, v_w_cq=v_w_cq, v_w_ckv=v_w_ckv, v_w_co=v_w_co, v_g_mlp=v_g_mlp, v_w_1=v_w_1, v_w_2=v_w_2, v_g_final=v_g_final)
    weights = {n: given[n] for n in TWIN_WEIGHTS}
    shared = {n: given[n] for n in SHARED_INPUTS}
    per_example = {n: given[n] for n in ['x', 'mem', 'positions']}
    grad_fn = _jax.value_and_grad(_loss, argnums=(0, 1))

    def one_microbatch(ex, loss_target):
        ex = dict(ex)
        diff = ex.pop(TWIN_DIFF_INPUT)
        return grad_fn(weights, diff, {**shared, **ex}, loss_target)

    if N_MICROBATCH == 1:
        loss, (grad_w, grad_x) = one_microbatch(per_example, given["loss_target"])
    else:
        def body(carry, xs):
            loss_sum, grad_sum = carry
            l_k, (gw_k, gx_k) = one_microbatch(xs[0], xs[1])
            with _jax.named_scope("update"):
                return (loss_sum + l_k, _jax.tree.map(_jnp.add, grad_sum, gw_k)), gx_k

        init = (_jnp.zeros((), _jnp.float32), _jax.tree.map(_jnp.zeros_like, weights))
        (loss, grad_w), grad_x = _jax.lax.scan(body, init, (per_example, given["loss_target"]))
    with _jax.named_scope("update"):
        delta_w, new_m, new_v = {}, {}, {}
        for n in TWIN_WEIGHTS:
            delta_w[n], new_m[n], new_v[n] = _adamw(weights[n], grad_w[n], given["m_" + n], given["v_" + n])
    return (loss, grad_x, *[grad_w[n] for n in TWIN_WEIGHTS], *[delta_w[n] for n in TWIN_WEIGHTS],
            *[new_m[n] for n in TWIN_WEIGHTS], *[new_v[n] for n in TWIN_WEIGHTS])
```

```python
import functools
import math

import jax
import jax.numpy as jnp
from jax import lax
from jax.experimental import pallas as pl
from jax.experimental.pallas import tpu as pltpu

F32 = jnp.float32
BF16 = jnp.bfloat16

D_MODEL = 1024
HEAD_DIM = 64
DIL_GROUPS = ((128, 1), (512, 4), (2048, 16))
ROPE_THETA = 10000.0
X_HEADS = 4
X_HEAD_DIM = D_MODEL // X_HEADS
D_FF = 4 * D_MODEL
EPS = 1e-6
DIL_WIDTH = 1536
SWA_Q_WIDTH = 512
SWA_KV_WIDTH = 128
P_WIDTH = 3 * DIL_WIDTH + SWA_Q_WIDTH + 2 * SWA_KV_WIDTH
GATE_WIDTH = 2 * D_MODEL
IN_WIDTH = P_WIDTH + GATE_WIDTH
BAND = 128
PBLK = 768
Q_SCALE = HEAD_DIM ** -0.5
X_SCALE = X_HEAD_DIM ** -0.5

ADAM_LR = 0.001
ADAM_B1 = 0.9
ADAM_B2 = 0.999
ADAM_EPS = 1e-08
ADAM_WD = 0.01
ADAM_STEP = 10

N_DEV = 8
LANES = 1024
VMEM_LIMIT = 52 * 1024 * 1024

NT = (((1,), (1,)), ((), ()))
TN = (((0,), (0,)), ((), ()))

_PACK = (("w_in", 928), ("w_branch_a", 64), ("w_branch_b", 64), ("w_out", 128), ("w_cq", 128),
         ("w_ckv", 256), ("w_co", 128), ("w_1", 512), ("w_2", 512), ("b_gate", 16))
PACK_ROWS = 2816
BIAS_ROW = 2720
SMALL = ("g_mix", "g_cross", "g_mem", "g_mlp", "g_final", "sink")


def _params(sem=None):
    return pltpu.CompilerParams(dimension_semantics=sem, vmem_limit_bytes=VMEM_LIMIT)


def _dot(a, b):
    return jnp.dot(a, b, preferred_element_type=F32)


def _dot_nt(a, b):
    return lax.dot_general(a, b, NT, preferred_element_type=F32)


def _dot_tn(a, b):
    return lax.dot_general(a, b, TN, preferred_element_type=F32)


def _rms(xt):
    return lax.rsqrt(jnp.mean(xt * xt, axis=-1, keepdims=True) + EPS)


def _rms_bwd(dh, xt, r, g):
    xn = xt * r
    dxn = dh * g
    dx = r * (dxn - xn * jnp.mean(dxn * xn, axis=-1, keepdims=True))
    return dx, jnp.sum(dh * xn, axis=0, keepdims=True)


def _partner(x):
    n = x.shape[-1]
    lane = lax.broadcasted_iota(jnp.int32, x.shape, 1)
    return jnp.where((lane % HEAD_DIM) < HEAD_DIM // 2, pltpu.roll(x, n - 32, 1), pltpu.roll(x, 32, 1))


def _rope_pattern(c, s, swa):
    one, zero = jnp.ones_like(c), jnp.zeros_like(c)
    cq, sq = c * Q_SCALE, s * Q_SCALE
    if swa:
        cs, ss = [cq] * 4 + [c, one], [sq] * 4 + [s, zero]
    else:
        cs, ss = [cq, c, one] * 2, [sq, s, zero] * 2
    return jnp.concatenate(cs, axis=1), jnp.concatenate(ss, axis=1)


def _inproj(x, g, w_p, cos, sin, tm):
    t = x.shape[0]
    nj = P_WIDTH // PBLK

    def body(x_ref, g_ref, w_ref, c_ref, s_ref, h_ref, p_ref, hs_ref):
        j = pl.program_id(1)

        @pl.when(j == 0)
        def _():
            xt = x_ref[...]
            hb = (xt * _rms(xt) * g_ref[...]).astype(BF16)
            hs_ref[...] = hb
            h_ref[...] = hb

        acc = _dot(hs_ref[...], w_ref[...])

        def emit(swa):
            cp, sp = _rope_pattern(c_ref[...], s_ref[...], swa)
            p_ref[...] = (acc * cp + _partner(acc) * sp).astype(BF16)

        pl.when(j < nj - 1)(lambda: emit(False))
        pl.when(j == nj - 1)(lambda: emit(True))

    return pl.pallas_call(
        body, name="inproj", grid=(t // tm, nj),
        in_specs=[pl.BlockSpec((tm, D_MODEL), lambda i, j: (i, 0)),
                  pl.BlockSpec((1, D_MODEL), lambda i, j: (0, 0)),
                  pl.BlockSpec((D_MODEL, PBLK), lambda i, j: (0, j)),
                  pl.BlockSpec((tm, 128), lambda i, j: (i, 0)),
                  pl.BlockSpec((tm, 128), lambda i, j: (i, 0))],
        out_specs=[pl.BlockSpec((tm, D_MODEL), lambda i, j: (i, 0)),
                   pl.BlockSpec((tm, PBLK), lambda i, j: (i, j))],
        out_shape=[jax.ShapeDtypeStruct((t, D_MODEL), BF16), jax.ShapeDtypeStruct((t, P_WIDTH), BF16)],
        scratch_shapes=[pltpu.VMEM((tm, D_MODEL), BF16)],
        compiler_params=_params(("arbitrary", "arbitrary")),
    )(x, g, w_p, cos, sin)


def _gates(h, w_g, b, tm, tn):
    t = h.shape[0]

    def body(h_ref, w_ref, b_ref, o_ref):
        z = _dot(h_ref[...], w_ref[...]) + b_ref[...]
        o_ref[...] = jax.nn.sigmoid(z).astype(BF16)

    return pl.pallas_call(
        body, name="gates", grid=(t // tm, GATE_WIDTH // tn),
        in_specs=[pl.BlockSpec((tm, D_MODEL), lambda i, j: (i, 0)),
                  pl.BlockSpec((D_MODEL, tn), lambda i, j: (0, j)),
                  pl.BlockSpec((1, tn), lambda i, j: (0, j))],
        out_specs=pl.BlockSpec((tm, tn), lambda i, j: (i, j)),
        out_shape=jax.ShapeDtypeStruct((t, GATE_WIDTH), BF16),
        compiler_params=_params(("arbitrary", "arbitrary")),
    )(h, w_g, b)


def _band_mask(i, s):
    row = lax.broadcasted_iota(jnp.int32, (BAND, 2 * BAND), 0)
    col = lax.broadcasted_iota(jnp.int32, (BAND, 2 * BAND), 1)
    band = (col >= row) & (col <= row + BAND)
    if s == 0:
        band = band & ((col >= BAND) | (i > 0))
    return band


def _kv_rows(cur_ref, tail_ref, s, off):
    if s == 0:
        return jnp.concatenate([tail_ref[:, off:off + 128], cur_ref[0:BAND, off:off + 128]], axis=0)
    return cur_ref[(s - 1) * BAND:(s + 1) * BAND, off:off + 128]


def _attn_layout(swa):
    if swa:
        return [(128 * j, 512, 640) for j in range(4)]
    return [(0, 128, 256), (384, 512, 640)]


def _attn_fwd(name, pv, d, col0, swa, sink_row, tq):
    ls = pv.shape[0]
    n, nsb = ls // tq, tq // BAND
    pairs = _attn_layout(swa)
    ncol = 1 if swa else 2
    ow = 128 * len(pairs)
    npb = P_WIDTH // PBLK

    def body(cur_ref, tail_ref, sink_ref, o_ref, lse_ref):
        i = pl.program_id(2)
        lane = lax.broadcasted_iota(jnp.int32, (BAND, 128), 1)
        lo = lane < HEAD_DIM
        for s in range(nsb):
            mask = _band_mask(i, s)
            rows = slice(s * BAND, (s + 1) * BAND)
            lse_tile = jnp.zeros((BAND, 128), F32)
            for j, (qo, ko, vo) in enumerate(pairs):
                q = cur_ref[rows, qo:qo + 128]
                kk = _kv_rows(cur_ref, tail_ref, s, ko)
                vv = _kv_rows(cur_ref, tail_ref, s, vo)
                halves = []
                for hf in (0, 1):
                    sel = lo if hf == 0 else jnp.logical_not(lo)
                    sc = _dot_nt(jnp.where(sel, q, jnp.zeros_like(q)), kk)
                    sc = jnp.where(mask, sc, -jnp.inf)
                    m = jnp.max(sc, axis=-1, keepdims=True)
                    if swa:
                        sk = sink_ref[:, 2 * j + hf:2 * j + hf + 1]
                        m = jnp.maximum(m, sk)
                    p = jnp.exp(sc - m)
                    den = jnp.sum(p, axis=-1, keepdims=True)
                    if swa:
                        den = den + jnp.exp(sk - m)
                    lse_tile = jnp.where(lane == 2 * j + hf, m + jnp.log(den), lse_tile)
                    halves.append(_dot((p / den).astype(BF16), vv))
                o_ref[rows, j * 128:(j + 1) * 128] = jnp.where(lo, halves[0], halves[1]).astype(BF16)
            lse_ref[rows, :] = lse_tile

    return pl.pallas_call(
        body, name=name, grid=(d, ncol, n),
        in_specs=[pl.BlockSpec((tq, PBLK), lambda r, cb, i: (i, r * npb + col0 + cb)),
                  pl.BlockSpec((BAND, PBLK), lambda r, cb, i: (jnp.maximum(i * nsb - 1, 0), r * npb + col0 + cb)),
                  pl.BlockSpec((1, 128), lambda r, cb, i: (0, 0))],
        out_specs=[pl.BlockSpec((tq, ow), lambda r, cb, i: (i, r * ncol + cb)),
                   pl.BlockSpec((tq, 128), lambda r, cb, i: (i, r * ncol + cb))],
        out_shape=[jax.ShapeDtypeStruct((ls, d * 512), BF16), jax.ShapeDtypeStruct((ls, d * 128 * ncol), F32)],
        compiler_params=_params(("arbitrary", "arbitrary", "arbitrary")),
    )(pv, pv, sink_row)


def _lse_lane(h):
    return (h // 4) * 128 + h % 4


def _head_scale(x, tile, lanes):
    lane = lax.broadcasted_iota(jnp.int32, (x.shape[0], 128), 1)
    lo = lane < HEAD_DIM
    out = []
    for c in range(x.shape[1] // 128):
        a0 = tile[:, lanes[2 * c]:lanes[2 * c] + 1]
        a1 = tile[:, lanes[2 * c + 1]:lanes[2 * c + 1] + 1]
        out.append(x[:, c * 128:(c + 1) * 128] * jnp.where(lo, a0, a1))
    return jnp.concatenate(out, axis=1)


def _head_sums(x, lanes, width):
    lane = lax.broadcasted_iota(jnp.int32, (x.shape[0], width), 1)
    out = jnp.zeros((x.shape[0], width), F32)
    for h in range(x.shape[1] // HEAD_DIM):
        sm = jnp.sum(x[:, h * HEAD_DIM:(h + 1) * HEAD_DIM], axis=-1, keepdims=True)
        out = jnp.where(lane == lanes[h], sm, out)
    return out


def _alphas(l0, l1, l2):
    m = jnp.maximum(jnp.maximum(l0, l1), l2)
    e0, e1, e2 = jnp.exp(l0 - m), jnp.exp(l1 - m), jnp.exp(l2 - m)
    den = e0 + e1 + e2
    return e0 / den, e1 / den, e2 / den


DIL_LANES = [_lse_lane(h) for h in range(8)]
SWA_LANES = list(range(8))


def _mix(o0, o1, o2, l0, l1, l2, ob, gts, x, w_a, w_b, w_out, g_cross, tm):
    t = x.shape[0]

    def body(o0_ref, o1_ref, o2_ref, l0_ref, l1_ref, l2_ref, ob_ref, g_ref, x_ref, wa_ref, wb_ref, wo_ref,
             gc_ref, oa_ref, ya_ref, yb_ref, mg_ref, x1_ref, hc_ref):
        a0, a1, a2 = _alphas(l0_ref[...], l1_ref[...], l2_ref[...])
        oa = (_head_scale(o0_ref[...].astype(F32), a0, DIL_LANES)
              + _head_scale(o1_ref[...].astype(F32), a1, DIL_LANES)
              + _head_scale(o2_ref[...].astype(F32), a2, DIL_LANES))
        oab = oa.astype(BF16)
        oa_ref[...] = oab
        ya = _dot(oab, wa_ref[...])
        yb = _dot(ob_ref[...], wb_ref[...])
        ya_ref[...] = ya.astype(BF16)
        yb_ref[...] = yb.astype(BF16)
        merged = (g_ref[:, :D_MODEL].astype(F32) * ya + g_ref[:, D_MODEL:].astype(F32) * yb).astype(BF16)
        mg_ref[...] = merged
        x1 = x_ref[...] + _dot(merged, wo_ref[...])
        x1_ref[...] = x1
        hc_ref[...] = (x1 * _rms(x1) * gc_ref[...]).astype(BF16)

    row = lambda w: pl.BlockSpec((tm, w), lambda i: (i, 0))
    full = lambda a, b: pl.BlockSpec((a, b), lambda i: (0, 0))
    return pl.pallas_call(
        body, name="mix", grid=(t // tm,),
        in_specs=[row(512), row(512), row(512), row(256), row(256), row(256), row(512), row(GATE_WIDTH),
                  row(D_MODEL), full(512, D_MODEL), full(512, D_MODEL), full(D_MODEL, D_MODEL), full(1, D_MODEL)],
        out_specs=[row(512), row(D_MODEL), row(D_MODEL), row(D_MODEL), row(D_MODEL), row(D_MODEL)],
        out_shape=[jax.ShapeDtypeStruct((t, 512), BF16), jax.ShapeDtypeStruct((t, D_MODEL), BF16),
                   jax.ShapeDtypeStruct((t, D_MODEL), BF16), jax.ShapeDtypeStruct((t, D_MODEL), BF16),
                   jax.ShapeDtypeStruct((t, D_MODEL), F32), jax.ShapeDtypeStruct((t, D_MODEL), BF16)],
        compiler_params=_params(("arbitrary",)),
    )(o0, o1, o2, l0, l1, l2, ob, gts, x, w_a, w_b, w_out, g_cross)


def _memkv(mem, g_mem, w_ckv):
    m = mem.shape[0]

    def body(mem_ref, g_ref, w_ref, mn_ref, kv_ref):
        xt = mem_ref[...]
        mn = (xt * _rms(xt) * g_ref[...]).astype(BF16)
        mn_ref[...] = mn
        kv_ref[...] = _dot(mn, w_ref[...]).astype(BF16)

    return pl.pallas_call(
        body, name="memkv",
        out_shape=[jax.ShapeDtypeStruct((m, D_MODEL), BF16), jax.ShapeDtypeStruct((m, 2 * D_MODEL), BF16)],
        compiler_params=_params(),
    )(mem, g_mem, w_ckv)


def _cross_probs(q, kv_ref, h):
    k = kv_ref[:, h * X_HEAD_DIM:(h + 1) * X_HEAD_DIM]
    sc = _dot_nt(q[:, h * X_HEAD_DIM:(h + 1) * X_HEAD_DIM], k)
    m = jnp.max(sc, axis=-1, keepdims=True)
    p = jnp.exp(sc - m)
    return p / jnp.sum(p, axis=-1, keepdims=True)


def _cross(hc, x1, kv, w_cq, w_co, g_mlp, tm):
    t = x1.shape[0]
    m = kv.shape[0]

    def body(hc_ref, x1_ref, kv_ref, wq_ref, wo_ref, g_ref, q_ref, o_ref, x2_ref, hm_ref):
        q = (_dot(hc_ref[...], wq_ref[...]) * X_SCALE).astype(BF16)
        q_ref[...] = q
        outs = []
        for h in range(X_HEADS):
            p = _cross_probs(q, kv_ref, h)
            v = kv_ref[:, D_MODEL + h * X_HEAD_DIM:D_MODEL + (h + 1) * X_HEAD_DIM]
            outs.append(_dot(p.astype(BF16), v))
        o = jnp.concatenate(outs, axis=1).astype(BF16)
        o_ref[...] = o
        x2 = x1_ref[...] + _dot(o, wo_ref[...])
        x2_ref[...] = x2
        hm_ref[...] = (x2 * _rms(x2) * g_ref[...]).astype(BF16)

    row = lambda w: pl.BlockSpec((tm, w), lambda i: (i, 0))
    full = lambda a, b: pl.BlockSpec((a, b), lambda i: (0, 0))
    return pl.pallas_call(
        body, name="cross", grid=(t // tm,),
        in_specs=[row(D_MODEL), row(D_MODEL), full(m, 2 * D_MODEL), full(D_MODEL, D_MODEL),
                  full(D_MODEL, D_MODEL), full(1, D_MODEL)],
        out_specs=[row(D_MODEL)] * 4,
        out_shape=[jax.ShapeDtypeStruct((t, D_MODEL), BF16), jax.ShapeDtypeStruct((t, D_MODEL), BF16),
                   jax.ShapeDtypeStruct((t, D_MODEL), F32), jax.ShapeDtypeStruct((t, D_MODEL), BF16)],
        compiler_params=_params(("arbitrary",)),
    )(hc, x1, kv, w_cq, w_co, g_mlp)


def _mlp(hm, x2, w_1, w_2, g_final, target, tm, tf):
    t = x2.shape[0]
    nf = D_FF // tf

    def body(hm_ref, x2_ref, w1_ref, w2_ref, g_ref, tg_ref, a_ref, dx3_ref, loss_ref, dg_ref, acc_ref):
        i, f = pl.program_id(0), pl.program_id(1)
        a = jnp.maximum(_dot(hm_ref[...], w1_ref[...]), 0.0)
        a_ref[...] = a.astype(BF16)
        part = _dot((a * a).astype(BF16), w2_ref[...])

        @pl.when(f == 0)
        def _():
            acc_ref[...] = part

        @pl.when(f > 0)
        def _():
            acc_ref[...] += part

        @pl.when((i == 0) & (f == 0))
        def _():
            loss_ref[...] = jnp.zeros_like(loss_ref)
            dg_ref[...] = jnp.zeros_like(dg_ref)

        @pl.when(f == nf - 1)
        def _():
            x3 = x2_ref[...] + acc_ref[...]
            r = _rms(x3)
            g = g_ref[...]
            diff = x3 * r * g - tg_ref[...]
            loss_ref[...] += 0.5 * jnp.sum(jnp.mean(diff * diff, axis=-1, keepdims=True))
            dx3, dg = _rms_bwd(diff / D_MODEL, x3, r, g)
            dx3_ref[...] = dx3
            dg_ref[...] += dg

    return pl.pallas_call(
        body, name="mlp", grid=(t // tm, nf),
        in_specs=[pl.BlockSpec((tm, D_MODEL), lambda i, f: (i, 0)),
                  pl.BlockSpec((tm, D_MODEL), lambda i, f: (i, 0)),
                  pl.BlockSpec((D_MODEL, tf), lambda i, f: (0, f)),
                  pl.BlockSpec((tf, D_MODEL), lambda i, f: (f, 0)),
                  pl.BlockSpec((1, D_MODEL), lambda i, f: (0, 0)),
                  pl.BlockSpec((tm, D_MODEL), lambda i, f: (i, 0))],
        out_specs=[pl.BlockSpec((tm, tf), lambda i, f: (i, f)),
                   pl.BlockSpec((tm, D_MODEL), lambda i, f: (i, 0)),
                   pl.BlockSpec((1, 128), lambda i, f: (0, 0)),
                   pl.BlockSpec((1, D_MODEL), lambda i, f: (0, 0))],
        out_shape=[jax.ShapeDtypeStruct((t, D_FF), BF16), jax.ShapeDtypeStruct((t, D_MODEL), F32),
                   jax.ShapeDtypeStruct((1, 128), F32), jax.ShapeDtypeStruct((1, D_MODEL), F32)],
        scratch_shapes=[pltpu.VMEM((tm, D_MODEL), F32)],
        compiler_params=_params(("arbitrary", "arbitrary")),
    )(hm, x2, w_1, w_2, g_final, target)


def _mlp_bwd(dx3, a, w_1, w_2, x2, g_mlp, tm, tf):
    t = x2.shape[0]
    nf = D_FF // tf

    def body(dx3_ref, a_ref, w1_ref, w2_ref, x2_ref, g_ref, dz_ref, dx2_ref, dg_ref, acc_ref):
        i, f = pl.program_id(0), pl.program_id(1)
        da2 = _dot_nt(dx3_ref[...].astype(BF16), w2_ref[...])
        dz = (2.0 * a_ref[...].astype(F32) * da2).astype(BF16)
        dz_ref[...] = dz
        part = _dot_nt(dz, w1_ref[...])

        @pl.when(f == 0)
        def _():
            acc_ref[...] = part

        @pl.when(f > 0)
        def _():
            acc_ref[...] += part

        @pl.when((i == 0) & (f == 0))
        def _():
            dg_ref[...] = jnp.zeros_like(dg_ref)

        @pl.when(f == nf - 1)
        def _():
            xt = x2_ref[...]
            dx, dg = _rms_bwd(acc_ref[...], xt, _rms(xt), g_ref[...])
            dx2_ref[...] = dx3_ref[...] + dx
            dg_ref[...] += dg

    return pl.pallas_call(
        body, name="mlp_bwd", grid=(t // tm, nf),
        in_specs=[pl.BlockSpec((tm, D_MODEL), lambda i, f: (i, 0)),
                  pl.BlockSpec((tm, tf), lambda i, f: (i, f)),
                  pl.BlockSpec((D_MODEL, tf), lambda i, f: (0, f)),
                  pl.BlockSpec((tf, D_MODEL), lambda i, f: (f, 0)),
                  pl.BlockSpec((tm, D_MODEL), lambda i, f: (i, 0)),
                  pl.BlockSpec((1, D_MODEL), lambda i, f: (0, 0))],
        out_specs=[pl.BlockSpec((tm, tf), lambda i, f: (i, f)),
                   pl.BlockSpec((tm, D_MODEL), lambda i, f: (i, 0)),
                   pl.BlockSpec((1, D_MODEL), lambda i, f: (0, 0))],
        out_shape=[jax.ShapeDtypeStruct((t, D_FF), BF16), jax.ShapeDtypeStruct((t, D_MODEL), F32),
                   jax.ShapeDtypeStruct((1, D_MODEL), F32)],
        scratch_shapes=[pltpu.VMEM((tm, D_MODEL), F32)],
        compiler_params=_params(("arbitrary", "arbitrary")),
    )(dx3, a, w_1, w_2, x2, g_mlp)


def _wgrad(name, a, b, tka, tn, tm, square=False):
    t, ka = a.shape
    n = b.shape[1]

    def body(a_ref, b_ref, o_ref):
        at = a_ref[...]
        if square:
            af = at.astype(F32)
            at = af * af
        part = _dot_tn(at.astype(BF16), b_ref[...].astype(BF16))

        @pl.when(pl.program_id(2) == 0)
        def _():
            o_ref[...] = part

        @pl.when(pl.program_id(2) > 0)
        def _():
            o_ref[...] += part

    return pl.pallas_call(
        body, name=name, grid=(ka // tka, n // tn, t // tm),
        in_specs=[pl.BlockSpec((tm, tka), lambda p, q, k: (k, p)),
                  pl.BlockSpec((tm, tn), lambda p, q, k: (k, q))],
        out_specs=pl.BlockSpec((tka, tn), lambda p, q, k: (p, q)),
        out_shape=jax.ShapeDtypeStruct((ka, n), F32),
        compiler_params=_params(("arbitrary", "arbitrary", "arbitrary")),
    )(a, b)


def _cross_bwd(dx2, x1, q, kv, w_cq, w_co, g_cross, tm):
    t = x1.shape[0]
    m = kv.shape[0]

    def body(dx2_ref, x1_ref, q_ref, kv_ref, wq_ref, wo_ref, g_ref, dq_ref, dx1_ref, dkv_ref, dg_ref):
        @pl.when(pl.program_id(0) == 0)
        def _():
            dkv_ref[...] = jnp.zeros_like(dkv_ref)
            dg_ref[...] = jnp.zeros_like(dg_ref)

        do = _dot_nt(dx2_ref[...].astype(BF16), wo_ref[...]).astype(BF16)
        q = q_ref[...]
        dqs = []
        for h in range(X_HEADS):
            hs = slice(h * X_HEAD_DIM, (h + 1) * X_HEAD_DIM)
            vs = slice(D_MODEL + h * X_HEAD_DIM, D_MODEL + (h + 1) * X_HEAD_DIM)
            p = _cross_probs(q, kv_ref, h)
            dp = _dot_nt(do[:, hs], kv_ref[:, vs])
            ds = (p * (dp - jnp.sum(dp * p, axis=-1, keepdims=True))).astype(BF16)
            dqs.append(_dot(ds, kv_ref[:, hs]))
            dkv_ref[:, hs] += _dot_tn(ds, q[:, hs])
            dkv_ref[:, vs] += _dot_tn(p.astype(BF16), do[:, hs])
        dq = (jnp.concatenate(dqs, axis=1) * X_SCALE).astype(BF16)
        dq_ref[...] = dq
        xt = x1_ref[...]
        dx, dg = _rms_bwd(_dot_nt(dq, wq_ref[...]), xt, _rms(xt), g_ref[...])
        dx1_ref[...] = dx2_ref[...] + dx
        dg_ref[...] += dg

    row = lambda w: pl.BlockSpec((tm, w), lambda i: (i, 0))
    full = lambda a, b: pl.BlockSpec((a, b), lambda i: (0, 0))
    return pl.pallas_call(
        body, name="cross_bwd", grid=(t // tm,),
        in_specs=[row(D_MODEL), row(D_MODEL), row(D_MODEL), full(m, 2 * D_MODEL), full(D_MODEL, D_MODEL),
                  full(D_MODEL, D_MODEL), full(1, D_MODEL)],
        out_specs=[row(D_MODEL), row(D_MODEL), full(m, 2 * D_MODEL), full(1, D_MODEL)],
        out_shape=[jax.ShapeDtypeStruct((t, D_MODEL), BF16), jax.ShapeDtypeStruct((t, D_MODEL), F32),
                   jax.ShapeDtypeStruct((m, 2 * D_MODEL), F32), jax.ShapeDtypeStruct((1, D_MODEL), F32)],
        compiler_params=_params(("arbitrary",)),
    )(dx2, x1, q, kv, w_cq, w_co, g_cross)


def _memkv_bwd(dkv, mn, mem, w_ckv, g_mem):
    def body(dkv_ref, mn_ref, mem_ref, w_ref, g_ref, dw_ref, dg_ref):
        dkvb = dkv_ref[...].astype(BF16)
        dw_ref[...] = _dot_tn(mn_ref[...], dkvb)
        dmn = _dot_nt(dkvb, w_ref[...])
        xt = mem_ref[...]
        dg_ref[...] = jnp.sum(dmn * xt * _rms(xt), axis=0, keepdims=True)

    return pl.pallas_call(
        body, name="memkv_bwd",
        out_shape=[jax.ShapeDtypeStruct((D_MODEL, 2 * D_MODEL), F32), jax.ShapeDtypeStruct((1, D_MODEL), F32)],
        compiler_params=_params(),
    )(dkv, mn, mem, w_ckv, g_mem)


def _mix_bwd(dx1, ya, yb, gts, oa, ob, l0, l1, l2, lb, sink_row, w_out, w_a, w_b, w_g, tm):
    t = dx1.shape[0]

    def body(dx1_ref, ya_ref, yb_ref, g_ref, oa_ref, ob_ref, l0_ref, l1_ref, l2_ref, lb_ref, sk_ref,
             wo_ref, wa_ref, wb_ref, wg_ref,
             dg_ref, dhp_ref, dya_ref, dyb_ref, do0_ref, do1_ref, do2_ref, c0_ref, c1_ref, c2_ref,
             dob_ref, cb_ref, db_ref, dsk_ref):
        @pl.when(pl.program_id(0) == 0)
        def _():
            db_ref[...] = jnp.zeros_like(db_ref)
            dsk_ref[...] = jnp.zeros_like(dsk_ref)

        dm = _dot_nt(dx1_ref[...].astype(BF16), wo_ref[...])
        ga = g_ref[:, :D_MODEL].astype(F32)
        gb = g_ref[:, D_MODEL:].astype(F32)
        dya = (dm * ga).astype(BF16)
        dyb = (dm * gb).astype(BF16)
        dya_ref[...] = dya
        dyb_ref[...] = dyb
        dpa = dm * ya_ref[...].astype(F32) * ga * (1.0 - ga)
        dpb = dm * yb_ref[...].astype(F32) * gb * (1.0 - gb)
        dpre = jnp.concatenate([dpa, dpb], axis=1)
        db_ref[...] += jnp.sum(dpre, axis=0, keepdims=True)
        dpreb = dpre.astype(BF16)
        dg_ref[...] = dpreb
        dhp_ref[...] = _dot_nt(dpreb, wg_ref[...])

        doa = _dot_nt(dya, wa_ref[...])
        dob = _dot_nt(dyb, wb_ref[...])
        dsum = _head_sums(doa * oa_ref[...].astype(F32), DIL_LANES, 256)
        alphas = _alphas(l0_ref[...], l1_ref[...], l2_ref[...])
        for al, do_ref, c_ref in zip(alphas, (do0_ref, do1_ref, do2_ref), (c0_ref, c1_ref, c2_ref)):
            c_ref[...] = al * dsum
            do_ref[...] = _head_scale(doa, al, DIL_LANES).astype(BF16)
        dob_ref[...] = dob.astype(BF16)
        cb = _head_sums(dob * ob_ref[...].astype(F32), SWA_LANES, 128)
        cb_ref[...] = cb
        lane = lax.broadcasted_iota(jnp.int32, cb.shape, 1)
        psink = jnp.where(lane < 8, jnp.exp(sk_ref[...] - lb_ref[...]), 0.0)
        dsk_ref[...] += jnp.sum(-psink * cb, axis=0, keepdims=True)

    row = lambda w: pl.BlockSpec((tm, w), lambda i: (i, 0))
    full = lambda a, b: pl.BlockSpec((a, b), lambda i: (0, 0))
    sds = jax.ShapeDtypeStruct
    return pl.pallas_call(
        body, name="mix_bwd", grid=(t // tm,),
        in_specs=[row(D_MODEL), row(D_MODEL), row(D_MODEL), row(GATE_WIDTH), row(512), row(512),
                  row(256), row(256), row(256), row(128), full(1, 128),
                  full(D_MODEL, D_MODEL), full(512, D_MODEL), full(512, D_MODEL), full(D_MODEL, GATE_WIDTH)],
        out_specs=[row(GATE_WIDTH), row(D_MODEL), row(D_MODEL), row(D_MODEL), row(512), row(512), row(512),
                   row(256), row(256), row(256), row(512), row(128), full(1, GATE_WIDTH), full(1, 128)],
        out_shape=[sds((t, GATE_WIDTH), BF16), sds((t, D_MODEL), F32), sds((t, D_MODEL), BF16),
                   sds((t, D_MODEL), BF16), sds((t, 512), BF16), sds((t, 512), BF16), sds((t, 512), BF16),
                   sds((t, 256), F32), sds((t, 256), F32), sds((t, 256), F32), sds((t, 512), BF16),
                   sds((t, 128), F32), sds((1, GATE_WIDTH), F32), sds((1, 128), F32)],
        compiler_params=_params(("arbitrary",)),
    )(dx1, ya, yb, gts, oa, ob, l0, l1, l2, lb, sink_row, w_out, w_a, w_b, w_g)


def _attn_bwd(name, pv, dov, lsev, cv, cosv, sinv, dp_prev, d, col0, swa, tq):
    ls = pv.shape[0]
    n, nsb = ls // tq, tq // BAND
    pairs = _attn_layout(swa)
    ncol = 1 if swa else 2
    ow = 128 * len(pairs)
    npb = P_WIDTH // PBLK

    def body(*refs):
        cur_ref, tail_ref, do_ref, lse_ref, c_ref, cos_ref, sin_ref = refs[:7]
        out_ref, acc_ref, carry_ref = refs[-3:]
        i = pl.program_id(2)
        acc_ref[...] = jnp.zeros_like(acc_ref)

        @pl.when(i < n)
        def _():
            lo = lax.broadcasted_iota(jnp.int32, (BAND, 128), 1) < HEAD_DIM
            lo2 = lax.broadcasted_iota(jnp.int32, (2 * BAND, 128), 1) < HEAD_DIM
            for s in range(nsb):
                mask = _band_mask(i, s)
                rows = slice(s * BAND, (s + 1) * BAND)
                krows = slice(s * BAND, (s + 2) * BAND)
                for j, (qo, ko, vo) in enumerate(pairs):
                    q = cur_ref[rows, qo:qo + 128]
                    kk = _kv_rows(cur_ref, tail_ref, s, ko)
                    vv = _kv_rows(cur_ref, tail_ref, s, vo)
                    do = do_ref[rows, j * 128:(j + 1) * 128]
                    dqh, dkh, dvh = [], [], []
                    for hf in (0, 1):
                        sel = lo if hf == 0 else jnp.logical_not(lo)
                        idx = 2 * j + hf
                        sc = _dot_nt(jnp.where(sel, q, jnp.zeros_like(q)), kk)
                        p = jnp.exp(jnp.where(mask, sc, -jnp.inf) - lse_ref[rows, idx:idx + 1])
                        dp = _dot_nt(jnp.where(sel, do, jnp.zeros_like(do)), vv)
                        ds = (p * (dp - c_ref[rows, idx:idx + 1])).astype(BF16)
                        dqh.append(_dot(ds, kk))
                        dkh.append(_dot_tn(ds, q))
                        dvh.append(_dot_tn(p.astype(BF16), do))
                    acc_ref[BAND + s * BAND:BAND + (s + 1) * BAND, qo:qo + 128] += jnp.where(lo, dqh[0], dqh[1])
                    acc_ref[krows, ko:ko + 128] += jnp.where(lo2, dkh[0], dkh[1])
                    acc_ref[krows, vo:vo + 128] += jnp.where(lo2, dvh[0], dvh[1])

        @pl.when(i >= 1)
        def _():
            if tq > BAND:
                fin = jnp.concatenate([carry_ref[0:tq - BAND, :], carry_ref[tq - BAND:, :] + acc_ref[0:BAND, :]], axis=0)
            else:
                fin = carry_ref[...] + acc_ref[0:BAND, :]
            cp, sp = _rope_pattern(cos_ref[...], sin_ref[...], swa)
            out_ref[...] = (fin * cp - _partner(fin) * sp).astype(BF16)

        carry_ref[...] = acc_ref[BAND:, :]

    qi = lambda i: jnp.minimum(i, n - 1)
    pi = lambda i: jnp.maximum(i - 1, 0)
    in_specs = [pl.BlockSpec((tq, PBLK), lambda r, cb, i: (qi(i), r * npb + col0 + cb)),
                pl.BlockSpec((BAND, PBLK), lambda r, cb, i: (jnp.maximum(qi(i) * nsb - 1, 0), r * npb + col0 + cb)),
                pl.BlockSpec((tq, ow), lambda r, cb, i: (qi(i), r * ncol + cb)),
                pl.BlockSpec((tq, 128), lambda r, cb, i: (qi(i), r * ncol + cb)),
                pl.BlockSpec((tq, 128), lambda r, cb, i: (qi(i), r * ncol + cb)),
                pl.BlockSpec((tq, 128), lambda r, cb, i: (pi(i), r)),
                pl.BlockSpec((tq, 128), lambda r, cb, i: (pi(i), r))]
    args = [pv, pv, dov, lsev, cv, cosv, sinv]
    aliases = {}
    if dp_prev is not None:
        in_specs.append(pl.BlockSpec(memory_space=pl.ANY))
        args.append(dp_prev)
        aliases = {7: 0}
    return pl.pallas_call(
        body, name=name, grid=(d, ncol, n + 1),
        in_specs=in_specs,
        out_specs=pl.BlockSpec((tq, PBLK), lambda r, cb, i: (pi(i), r * npb + col0 + cb)),
        out_shape=jax.ShapeDtypeStruct((ls, d * P_WIDTH), BF16),
        scratch_shapes=[pltpu.VMEM((tq + BAND, PBLK), F32), pltpu.VMEM((tq, PBLK), F32)],
        input_output_aliases=aliases,
        compiler_params=_params(("arbitrary", "arbitrary", "arbitrary")),
    )(*args)


def _dx(dp, w_p, dh_part, dx1, x, g_mix, tm):
    t = x.shape[0]
    nk = P_WIDTH // PBLK

    def body(dp_ref, w_ref, dhp_ref, dx1_ref, x_ref, g_ref, gx_ref, dg_ref, acc_ref):
        i, k = pl.program_id(0), pl.program_id(1)
        part = _dot_nt(dp_ref[...], w_ref[...])

        @pl.when(k == 0)
        def _():
            acc_ref[...] = part + dhp_ref[...]

        @pl.when(k > 0)
        def _():
            acc_ref[...] += part

        @pl.when((i == 0) & (k == 0))
        def _():
            dg_ref[...] = jnp.zeros_like(dg_ref)

        @pl.when(k == nk - 1)
        def _():
            xt = x_ref[...]
            dx, dg = _rms_bwd(acc_ref[...], xt, _rms(xt), g_ref[...])
            gx_ref[...] = dx1_ref[...] + dx
            dg_ref[...] += dg

    row = pl.BlockSpec((tm, D_MODEL), lambda i, k: (i, 0))
    return pl.pallas_call(
        body, name="dx", grid=(t // tm, nk),
        in_specs=[pl.BlockSpec((tm, PBLK), lambda i, k: (i, k)),
                  pl.BlockSpec((D_MODEL, PBLK), lambda i, k: (0, k)),
                  row, row, row, pl.BlockSpec((1, D_MODEL), lambda i, k: (0, 0))],
        out_specs=[row, pl.BlockSpec((1, D_MODEL), lambda i, k: (0, 0))],
        out_shape=[jax.ShapeDtypeStruct((t, D_MODEL), F32), jax.ShapeDtypeStruct((1, D_MODEL), F32)],
        scratch_shapes=[pltpu.VMEM((tm, D_MODEL), F32)],
        compiler_params=_params(("arbitrary", "arbitrary")),
    )(dp, w_p, dh_part, dx1, x, g_mix)


MESH = pl.DeviceIdType.MESH
HBM_SPEC = pl.BlockSpec(memory_space=pltpu.HBM)
VMEM_SPEC = pl.BlockSpec(memory_space=pltpu.VMEM)


def _all_gather(xp):
    r = xp.shape[0]

    def body(x_ref, out_ref, send_sems, recv_sems, local_sem):
        x, y, c = lax.axis_index("x"), lax.axis_index("y"), lax.axis_index("c")
        me, sibling = (x, y, c), (x, y, 1 - c)
        chips = [(1 - x, y), (x, 1 - y), (1 - x, 1 - y)]

        def rows(px, py, pc):
            return out_ref.at[4 * px + 2 * py + pc]

        def copy(k, block, to, src=None):
            return pltpu.make_async_remote_copy(
                src_ref=rows(*block) if src is None else src, dst_ref=rows(*block),
                send_sem=send_sems.at[k], recv_sem=recv_sems.at[k], device_id=to, device_id_type=MESH)

        mine = pltpu.make_async_copy(x_ref, rows(*me), local_sem)
        mine.start()
        first = [copy(0, me, sibling, src=x_ref)]
        first += [copy(1 + j, me, (*chip, c), src=x_ref) for j, chip in enumerate(chips)]
        for cp in first:
            cp.start()
        passed = [copy(4 + j, (*chip, c), sibling) for j, chip in enumerate(chips)]
        for j, chip in enumerate(chips):
            copy(1 + j, (*chip, c), me).wait_recv()
            passed[j].start()
        copy(0, sibling, me).wait_recv()
        for j, chip in enumerate(chips):
            copy(4 + j, (*chip, 1 - c), me).wait_recv()
        for cp in first + passed:
            cp.wait_send()
        mine.wait()

    return pl.pallas_call(
        body, name="all_gather",
        out_shape=jax.ShapeDtypeStruct((N_DEV, r, LANES), xp.dtype),
        in_specs=[HBM_SPEC], out_specs=HBM_SPEC,
        scratch_shapes=[pltpu.SemaphoreType.DMA((7,)), pltpu.SemaphoreType.DMA((7,)), pltpu.SemaphoreType.DMA],
    )(xp)


def _exchange(gp, sp):
    r = gp.shape[1]

    def body(g_ref, s_ref, recv_ref, srecv_ref, send_sems, recv_sems, local_sem):
        x, y, c = lax.axis_index("x"), lax.axis_index("y"), lax.axis_index("c")
        me_idx = 4 * x + 2 * y + c
        mine = pltpu.make_async_copy(g_ref.at[me_idx], recv_ref.at[0], local_sem)
        mine.start()
        srecv_ref[pl.ds(me_idx, 1)] = s_ref[...][None]
        copies = []
        for k in range(1, N_DEV):
            px = 1 - x if k & 4 else x
            py = 1 - y if k & 2 else y
            pc = 1 - c if k & 1 else c
            peer = (px, py, pc)
            copies.append(pltpu.make_async_remote_copy(
                src_ref=g_ref.at[4 * px + 2 * py + pc], dst_ref=recv_ref.at[k],
                send_sem=send_sems.at[k - 1], recv_sem=recv_sems.at[k - 1], device_id=peer, device_id_type=MESH))
            copies.append(pltpu.make_async_remote_copy(
                src_ref=s_ref, dst_ref=srecv_ref.at[me_idx],
                send_sem=send_sems.at[6 + k], recv_sem=recv_sems.at[6 + k], device_id=peer, device_id_type=MESH))
        for cp in copies:
            cp.start()
        for cp in copies:
            cp.wait_recv()
        for cp in copies:
            cp.wait_send()
        mine.wait()

    return pl.pallas_call(
        body, name="grad_exchange",
        out_shape=[jax.ShapeDtypeStruct((N_DEV, r, LANES), gp.dtype), jax.ShapeDtypeStruct((N_DEV, 8, LANES), F32)],
        in_specs=[HBM_SPEC, VMEM_SPEC], out_specs=[HBM_SPEC, VMEM_SPEC],
        scratch_shapes=[pltpu.SemaphoreType.DMA((14,)), pltpu.SemaphoreType.DMA((14,)), pltpu.SemaphoreType.DMA],
    )(gp, sp)


def _adamw(name, parts, w, m, v, tr):
    rows = w.shape[0]

    def body(p_ref, w_ref, m_ref, v_ref, g_ref, d_ref, nm_ref, nv_ref):
        g = p_ref[0].astype(F32)
        for k in range(1, N_DEV):
            g = g + p_ref[k].astype(F32)
        g_ref[...] = g
        nm = ADAM_B1 * m_ref[...] + (1.0 - ADAM_B1) * g
        nv = ADAM_B2 * v_ref[...] + (1.0 - ADAM_B2) * (g * g)
        nm_ref[...] = nm
        nv_ref[...] = nv
        m_hat = nm / (1.0 - ADAM_B1 ** ADAM_STEP)
        v_hat = nv / (1.0 - ADAM_B2 ** ADAM_STEP)
        d_ref[...] = -ADAM_LR * (m_hat / (jnp.sqrt(v_hat) + ADAM_EPS) + ADAM_WD * w_ref[...])

    blk = pl.BlockSpec((tr, LANES), lambda i: (i, 0))
    return pl.pallas_call(
        body, name=name, grid=(rows // tr,),
        in_specs=[pl.BlockSpec((N_DEV, tr, LANES), lambda i: (0, i, 0)), blk, blk, blk],
        out_specs=[blk] * 4,
        out_shape=[jax.ShapeDtypeStruct((rows, LANES), F32)] * 4,
        compiler_params=_params(("arbitrary",)),
    )(parts, w, m, v)


def _pack(shards):
    rows = []
    for name, n in _PACK:
        a = shards[name].reshape(-1)
        a = jnp.pad(a, (0, n * LANES - a.shape[0]))
        rows.append(a.reshape(n, LANES))
    used = sum(n for _, n in _PACK)
    rows.append(jnp.zeros((PACK_ROWS - used, LANES), rows[0].dtype))
    return jnp.concatenate(rows, axis=0)


def _unpack(buf, shapes):
    out, r0 = {}, 0
    for name, n in _PACK:
        size = math.prod(shapes[name])
        out[name] = buf[r0:r0 + n].reshape(-1)[:size].reshape(shapes[name])
        r0 += n
    return out


_SHARD_SHAPES = {"w_in": (1024, 928), "w_branch_a": (512, 128), "w_branch_b": (512, 128), "w_out": (128, 1024),
                 "w_cq": (128, 1024), "w_ckv": (1024, 256), "w_co": (128, 1024), "w_1": (1024, 512),
                 "w_2": (512, 1024), "b_gate": (2, 128)}
_COL_SHARDED = ("w_in", "w_branch_a", "w_branch_b", "w_ckv", "w_1", "b_gate")


def _full_from_gathered(gathered):
    out, r0 = {}, 0
    for name, n in _PACK:
        shp = _SHARD_SHAPES[name]
        size = math.prod(shp)
        if name == "b_gate":
            a = lax.bitcast_convert_type(gathered[:, r0, :2 * size].reshape(N_DEV, size, 2), F32)
            a = a.reshape((N_DEV,) + shp)
        else:
            a = gathered[:, r0:r0 + n].reshape(N_DEV, -1)[:, :size].reshape((N_DEV,) + shp)
        if name in _COL_SHARDED:
            a = jnp.swapaxes(a, 0, 1).reshape(shp[0], N_DEV * shp[1])
        else:
            a = a.reshape(N_DEV * shp[0], shp[1])
        out[name] = a
        r0 += n
    return out


def _shards_from_full(full):
    rows = []
    for name, n in _PACK:
        shp = _SHARD_SHAPES[name]
        a = full[name]
        if name in _COL_SHARDED:
            a = jnp.swapaxes(a.reshape(shp[0], N_DEV, shp[1]), 0, 1)
        a = a.reshape(N_DEV, -1).astype(BF16)
        a = jnp.pad(a, ((0, 0), (0, n * LANES - a.shape[1])))
        rows.append(a.reshape(N_DEV, n, LANES))
    used = sum(n for _, n in _PACK)
    rows.append(jnp.zeros((N_DEV, PACK_ROWS - used, LANES), BF16))
    return jnp.concatenate(rows, axis=1)


def _split_w_in(w_in):
    rows = w_in.shape[0]
    dil = w_in[:, :3 * DIL_WIDTH].reshape(rows, 3, 3, 4, 128)
    dil = dil.transpose(0, 2, 3, 1, 4).reshape(rows, 3 * DIL_WIDTH)
    o = 3 * DIL_WIDTH
    qb = w_in[:, o:o + SWA_Q_WIDTH].reshape(rows, 2, 4, HEAD_DIM).transpose(0, 2, 1, 3).reshape(rows, SWA_Q_WIDTH)
    kvb = w_in[:, o + SWA_Q_WIDTH:P_WIDTH]
    return jnp.concatenate([dil, qb, kvb], axis=1), w_in[:, P_WIDTH:]


def _merge_w_in(dw_p, dw_g):
    rows = dw_p.shape[0]
    dil = dw_p[:, :3 * DIL_WIDTH].reshape(rows, 3, 4, 3, 128).transpose(0, 3, 1, 2, 4).reshape(rows, 3 * DIL_WIDTH)
    o = 3 * DIL_WIDTH
    qb = dw_p[:, o:o + SWA_Q_WIDTH].reshape(rows, 4, 2, HEAD_DIM).transpose(0, 2, 1, 3).reshape(rows, SWA_Q_WIDTH)
    return jnp.concatenate([dil, qb, dw_p[:, o + SWA_Q_WIDTH:], dw_g], axis=1)


def _swa_rows(w_b):
    return w_b.reshape(2, 4, HEAD_DIM, -1).transpose(1, 0, 2, 3).reshape(SWA_Q_WIDTH, -1)


def _swa_rows_inv(dw_b):
    return dw_b.reshape(4, 2, HEAD_DIM, -1).transpose(1, 0, 2, 3).reshape(SWA_Q_WIDTH, -1)


def _rope_tables(pos):
    half = HEAD_DIM // 2
    inv = ROPE_THETA ** (-jnp.arange(half, dtype=F32) / half)
    ang = pos.astype(F32)[:, None] * inv
    c, s = jnp.cos(ang), jnp.sin(ang)
    return jnp.concatenate([c, c, c, c], axis=1), jnp.concatenate([-s, s, -s, s], axis=1)


def _local_step(x, mem, pos, target, wts, g_mix, g_cross, g_mem, g_mlp, g_final, sink):
    t = x.shape[0]
    tm = min(512, t)
    tq = min(256, t // 16)
    w_p, w_g = _split_w_in(wts["w_in"])
    w_b = _swa_rows(wts["w_branch_b"])
    b_gate = wts["b_gate"].astype(F32).reshape(1, GATE_WIDTH)
    cos, sin = _rope_tables(pos)
    sink_row = jnp.pad(sink.reshape(2, 4).T.reshape(1, 8), ((0, 0), (0, 120)))
    views = [(1, 0, False), (4, 2, False), (16, 4, False), (1, 6, True)]

    h, p = _inproj(x, g_mix, w_p, cos, sin, tm)
    gts = _gates(h, w_g, b_gate, tm, 1024)
    outs, lses = [], []
    for gi, (d, col0, swa) in enumerate(views):
        o, l = _attn_fwd(f"attn_fwd{gi}", p.reshape(t // d, d * P_WIDTH), d, col0, swa, sink_row, tq)
        outs.append(o.reshape(t, 512))
        lses.append(l.reshape(t, -1))
    oa, ya, yb, merged, x1, hc = _mix(outs[0], outs[1], outs[2], lses[0], lses[1], lses[2], outs[3], gts, x,
                                      wts["w_branch_a"], w_b, wts["w_out"], g_cross, tm)
    mn, kv = _memkv(mem, g_mem, wts["w_ckv"])
    q, o, x2, hm = _cross(hc, x1, kv, wts["w_cq"], wts["w_co"], g_mlp, tm)
    a, dx3, loss, dg_final = _mlp(hm, x2, wts["w_1"], wts["w_2"], g_final.reshape(1, D_MODEL), target, tm, 1024)

    grads = {}
    dz, dx2, dg_mlp = _mlp_bwd(dx3, a, wts["w_1"], wts["w_2"], x2, g_mlp, tm, 1024)
    grads["w_2"] = _wgrad("dw_2", a, dx3, 1024, 1024, tm, square=True)
    grads["w_1"] = _wgrad("dw_1", hm, dz, 1024, 1024, tm)
    dq, dx1, dkv, dg_cross = _cross_bwd(dx2, x1, q, kv, wts["w_cq"], wts["w_co"], g_cross, tm)
    grads["w_co"] = _wgrad("dw_co", o, dx2, 1024, 1024, tm)
    grads["w_cq"] = _wgrad("dw_cq", hc, dq, 1024, 1024, tm)
    grads["w_ckv"], dg_mem = _memkv_bwd(dkv, mn, mem, wts["w_ckv"], g_mem)
    (dgt, dh_part, dya, dyb, do0, do1, do2, c0, c1, c2, dob, cb, db_gate, dsink) = _mix_bwd(
        dx1, ya, yb, gts, oa, outs[3], lses[0], lses[1], lses[2], lses[3], sink_row,
        wts["w_out"], wts["w_branch_a"], w_b, w_g, min(256, tm))
    grads["w_out"] = _wgrad("dw_out", merged, dx1, 1024, 1024, tm)
    grads["w_branch_a"] = _wgrad("dw_a", oa, dya, 512, 1024, tm)
    grads["w_branch_b"] = _swa_rows_inv(_wgrad("dw_b", outs[3], dyb, 512, 1024, tm))
    dw_g = _wgrad("dw_g", h, dgt, 1024, 1024, tm)
    dp = None
    for gi, ((d, col0, swa), do_g, c_g) in enumerate(zip(views, (do0, do1, do2, dob), (c0, c1, c2, cb))):
        ls = t // d
        dp = _attn_bwd(f"attn_bwd{gi}", p.reshape(ls, d * P_WIDTH), do_g.reshape(ls, d * 512),
                       lses[gi].reshape(ls, -1), c_g.reshape(ls, -1), cos.reshape(ls, d * 128),
                       sin.reshape(ls, d * 128), None if dp is None else dp.reshape(ls, d * P_WIDTH),
                       d, col0, swa, tq).reshape(t, P_WIDTH)
    dw_p = _wgrad("dw_p", h, dp, 1024, PBLK, tm)
    grads["w_in"] = _merge_w_in(dw_p, dw_g)
    grads["b_gate"] = db_gate.reshape(2, D_MODEL)
    grad_x, dg_mix = _dx(dp, w_p, dh_part, dx1, x, g_mix, tm)
    dsink_heads = dsink[0, :8].reshape(4, 2).T.reshape(8)
    small = {"g_mix": dg_mix[0], "g_cross": dg_cross[0], "g_mem": dg_mem[0], "g_mlp": dg_mlp[0],
             "g_final": dg_final[0], "sink": dsink_heads}
    return loss[0, 0], grad_x, grads, small


def kernel(x, mem, positions, g_mix, w_in, b_gate, sink, w_branch_a, w_branch_b, w_out, g_cross, g_mem, w_cq, w_ckv, w_co, g_mlp, w_1, w_2, g_final, loss_target, m_g_mix, m_w_in, m_b_gate, m_sink, m_w_branch_a, m_w_branch_b, m_w_out, m_g_cross, m_g_mem, m_w_cq, m_w_ckv, m_w_co, m_g_mlp, m_w_1, m_w_2, m_g_final, v_g_mix, v_w_in, v_b_gate, v_sink, v_w_branch_a, v_w_branch_b, v_w_out, v_g_cross, v_g_mem, v_w_cq, v_w_ckv, v_w_co, v_g_mlp, v_w_1, v_w_2, v_g_final):
    local = dict(locals())
    big = [n for n, _ in _PACK]
    w_sh = {n: local[n][0] for n in big}
    m_sh = {n: local["m_" + n][0] for n in big}
    v_sh = {n: local["v_" + n][0] for n in big}

    w_pack = _pack(w_sh)
    bias_words = lax.bitcast_convert_type(w_sh["b_gate"].reshape(-1), BF16).reshape(1, -1)
    bias_rows = jnp.pad(bias_words, ((0, 15), (0, LANES - bias_words.shape[1])))
    w_pack16 = w_pack.astype(BF16)
    gathered = _all_gather(jnp.concatenate([w_pack16[:BIAS_ROW], bias_rows, w_pack16[BIAS_ROW + 16:]], axis=0))
    wts = _full_from_gathered(gathered)

    loss, grad_x, grads, small = _local_step(
        x[0], mem[0], positions[0], loss_target[0], wts, g_mix, g_cross, g_mem, g_mlp,
        g_final, sink[0])

    gp = _shards_from_full(grads)
    sp = jnp.stack([small[n] if n != "sink" else jnp.pad(small[n], (0, LANES - 8)) for n in SMALL]
                   + [jnp.zeros((LANES,), F32)] * 2)
    recv, srecv = _exchange(gp, sp)

    g_big, d_big, nm_big, nv_big = _adamw("adamw", recv, w_pack, _pack(m_sh), _pack(v_sh), 256)

    def small_pack(prefix):
        rows = []
        for n in SMALL:
            a = local[prefix + n].reshape(-1)
            rows.append(jnp.pad(a, (0, LANES - a.shape[0])))
        return jnp.stack(rows + [jnp.zeros((LANES,), F32)] * 2)

    g_sm, d_sm, nm_sm, nv_sm = _adamw("adamw_small", srecv, small_pack(""), small_pack("m_"), small_pack("v_"), 8)

    def collect(buf_big, buf_small):
        b = _unpack(buf_big, _SHARD_SHAPES)
        out = {n: b[n][None] for n in big}
        for i, n in enumerate(SMALL):
            shp = local[n].shape
            out[n] = buf_small[i, :math.prod(shp)].reshape(shp)
        return out

    names = ["g_mix", "w_in", "b_gate", "sink", "w_branch_a", "w_branch_b", "w_out", "g_cross", "g_mem", "w_cq",
             "w_ckv", "w_co", "g_mlp", "w_1", "w_2", "g_final"]
    res = [lax.psum(loss, ("x", "y", "c")), grad_x[None]]
    for bb, bs in ((g_big, g_sm), (d_big, d_sm), (nm_big, nm_sm), (nv_big, nv_sm)):
        c = collect(bb, bs)
        res += [c[n] for n in names]
    return tuple(res)
```

```python
import functools
import math

import jax
import jax.numpy as jnp
from jax import lax
from jax.experimental import pallas as pl
from jax.experimental.pallas import tpu as pltpu

F32 = jnp.float32
BF16 = jnp.bfloat16

D_MODEL = 1024
HEAD_DIM = 64
DIL_GROUPS = ((128, 1), (512, 4), (2048, 16))
ROPE_THETA = 10000.0
X_HEADS = 4
X_HEAD_DIM = D_MODEL // X_HEADS
D_FF = 4 * D_MODEL
EPS = 1e-6
DIL_WIDTH = 1536
SWA_Q_WIDTH = 512
SWA_KV_WIDTH = 128
P_WIDTH = 3 * DIL_WIDTH + SWA_Q_WIDTH + 2 * SWA_KV_WIDTH
GATE_WIDTH = 2 * D_MODEL
IN_WIDTH = P_WIDTH + GATE_WIDTH
BAND = 128
PBLK = 768
Q_SCALE = HEAD_DIM ** -0.5
X_SCALE = X_HEAD_DIM ** -0.5

ADAM_LR = 0.001
ADAM_B1 = 0.9
ADAM_B2 = 0.999
ADAM_EPS = 1e-08
ADAM_WD = 0.01
ADAM_STEP = 10

N_DEV = 8
LANES = 1024
VMEM_LIMIT = 52 * 1024 * 1024

NT = (((1,), (1,)), ((), ()))
TN = (((0,), (0,)), ((), ()))

_PACK = (("w_in", 928), ("w_branch_a", 64), ("w_branch_b", 64), ("w_out", 128), ("w_cq", 128),
         ("w_ckv", 256), ("w_co", 128), ("w_1", 512), ("w_2", 512), ("b_gate", 16))
PACK_ROWS = 2816
BIAS_ROW = 2720
SMALL = ("g_mix", "g_cross", "g_mem", "g_mlp", "g_final", "sink")


def _params(sem=None):
    return pltpu.CompilerParams(dimension_semantics=sem, vmem_limit_bytes=VMEM_LIMIT)


def _dot(a, b):
    return jnp.dot(a, b, preferred_element_type=F32)


def _dot_nt(a, b):
    return lax.dot_general(a, b, NT, preferred_element_type=F32)


def _dot_tn(a, b):
    return lax.dot_general(a, b, TN, preferred_element_type=F32)


def _rms(xt):
    return lax.rsqrt(jnp.mean(xt * xt, axis=-1, keepdims=True) + EPS)


def _rms_bwd(dh, xt, r, g):
    xn = xt * r
    dxn = dh * g
    dx = r * (dxn - xn * jnp.mean(dxn * xn, axis=-1, keepdims=True))
    return dx, jnp.sum(dh * xn, axis=0, keepdims=True)


def _partner(x):
    n = x.shape[-1]
    lane = lax.broadcasted_iota(jnp.int32, x.shape, 1)
    return jnp.where((lane % HEAD_DIM) < HEAD_DIM // 2, pltpu.roll(x, n - 32, 1), pltpu.roll(x, 32, 1))


def _rope_pattern(c, s, swa):
    one, zero = jnp.ones_like(c), jnp.zeros_like(c)
    cq, sq = c * Q_SCALE, s * Q_SCALE
    if swa:
        cs, ss = [cq] * 4 + [c, one], [sq] * 4 + [s, zero]
    else:
        cs, ss = [cq, c, one] * 2, [sq, s, zero] * 2
    return jnp.concatenate(cs, axis=1), jnp.concatenate(ss, axis=1)


def _lane_scratch(rows, w):
    return pltpu.VMEM((w // 128, rows, 128), F32)


def _deinterleave(val, scr_ref, dst_ref, dtype):
    d, n = dst_ref.shape[0], dst_ref.shape[1]
    nc = val.shape[1] // 128
    for c in range(nc):
        scr_ref[c] = val[:, c * 128:(c + 1) * 128]
    for r in range(d):
        rows = [scr_ref.at[c][pl.ds(r, n, stride=d), :] for c in range(nc)]
        dst_ref[r] = jnp.concatenate(rows, axis=1).astype(dtype)


def _res_spec(a, tm):
    d, w = a.shape[0], a.shape[2]
    return pl.BlockSpec((d, tm // d, w), lambda i: (0, i, 0))


def _interleave(src_ref, scr_ref):
    d, n = src_ref.shape[0], src_ref.shape[1]
    nc = src_ref.shape[2] // 128
    for r in range(d):
        v = src_ref[r].astype(F32)
        for c in range(nc):
            scr_ref.at[c][pl.ds(r, n, stride=d), :] = v[:, c * 128:(c + 1) * 128]
    return jnp.concatenate([scr_ref[c] for c in range(nc)], axis=1)


def _inproj(x, g, w_p, cos, sin, tm):
    t = x.shape[0]
    nj = P_WIDTH // PBLK

    def body(x_ref, g_ref, w_ref, c_ref, s_ref, h_ref, h1_ref, h2_ref, p0_ref, p1_ref, p2_ref, pb_ref,
             hs_ref, hf_ref, pf_ref):
        j = pl.program_id(1)

        @pl.when(j == 0)
        def _():
            xt = x_ref[...]
            hf = xt * _rms(xt) * g_ref[...]
            hb = hf.astype(BF16)
            hs_ref[...] = hb
            h_ref[...] = hb
            _deinterleave(hf, hf_ref, h1_ref, BF16)
            _deinterleave(hf, hf_ref, h2_ref, BF16)

        acc = _dot(hs_ref[...], w_ref[...])

        def rope(swa):
            cp, sp = _rope_pattern(c_ref[...], s_ref[...], swa)
            return acc * cp + _partner(acc) * sp

        @pl.when(j < 2)
        def _():
            p0_ref[...] = rope(False).astype(BF16)

        @pl.when((j >= 2) & (j < 4))
        def _():
            _deinterleave(rope(False), pf_ref, p1_ref, BF16)

        @pl.when((j >= 4) & (j < 6))
        def _():
            _deinterleave(rope(False), pf_ref, p2_ref, BF16)

        @pl.when(j == nj - 1)
        def _():
            pb_ref[...] = rope(True).astype(BF16)

    d1, d2 = DIL_GROUPS[1][1], DIL_GROUPS[2][1]
    col = lambda lo: (lambda i, j: (0, i, jnp.clip(j - lo, 0, 1)))
    sds = jax.ShapeDtypeStruct
    return pl.pallas_call(
        body, name="inproj", grid=(t // tm, nj),
        in_specs=[pl.BlockSpec((tm, D_MODEL), lambda i, j: (i, 0)),
                  pl.BlockSpec((1, D_MODEL), lambda i, j: (0, 0)),
                  pl.BlockSpec((D_MODEL, PBLK), lambda i, j: (0, j)),
                  pl.BlockSpec((tm, 128), lambda i, j: (i, 0)),
                  pl.BlockSpec((tm, 128), lambda i, j: (i, 0))],
        out_specs=[pl.BlockSpec((tm, D_MODEL), lambda i, j: (i, 0)),
                   pl.BlockSpec((d1, tm // d1, D_MODEL), lambda i, j: (0, i, 0)),
                   pl.BlockSpec((d2, tm // d2, D_MODEL), lambda i, j: (0, i, 0)),
                   pl.BlockSpec((tm, PBLK), lambda i, j: (i, jnp.minimum(j, 1))),
                   pl.BlockSpec((d1, tm // d1, PBLK), col(2)),
                   pl.BlockSpec((d2, tm // d2, PBLK), col(4)),
                   pl.BlockSpec((tm, PBLK), lambda i, j: (i, 0))],
        out_shape=[sds((t, D_MODEL), BF16), sds((d1, t // d1, D_MODEL), BF16), sds((d2, t // d2, D_MODEL), BF16),
                   sds((t, 2 * PBLK), BF16), sds((d1, t // d1, 2 * PBLK), BF16), sds((d2, t // d2, 2 * PBLK), BF16),
                   sds((t, PBLK), BF16)],
        scratch_shapes=[pltpu.VMEM((tm, D_MODEL), BF16), _lane_scratch(tm, D_MODEL), _lane_scratch(tm, PBLK)],
        compiler_params=_params(("arbitrary", "arbitrary")),
    )(x, g, w_p, cos, sin)


def _gates(h, w_g, b, tm, tn):
    t = h.shape[0]

    def body(h_ref, w_ref, b_ref, o_ref):
        z = _dot(h_ref[...], w_ref[...]) + b_ref[...]
        o_ref[...] = jax.nn.sigmoid(z).astype(BF16)

    return pl.pallas_call(
        body, name="gates", grid=(t // tm, GATE_WIDTH // tn),
        in_specs=[pl.BlockSpec((tm, D_MODEL), lambda i, j: (i, 0)),
                  pl.BlockSpec((D_MODEL, tn), lambda i, j: (0, j)),
                  pl.BlockSpec((1, tn), lambda i, j: (0, j))],
        out_specs=pl.BlockSpec((tm, tn), lambda i, j: (i, j)),
        out_shape=jax.ShapeDtypeStruct((t, GATE_WIDTH), BF16),
        compiler_params=_params(("arbitrary", "arbitrary")),
    )(h, w_g, b)


def _band_mask(i, s):
    row = lax.broadcasted_iota(jnp.int32, (BAND, 2 * BAND), 0)
    col = lax.broadcasted_iota(jnp.int32, (BAND, 2 * BAND), 1)
    band = (col >= row) & (col <= row + BAND)
    if s == 0:
        band = band & ((col >= BAND) | (i > 0))
    return band


def _kv_rows(cur_ref, tail_ref, s, off):
    if s == 0:
        return jnp.concatenate([tail_ref[:, off:off + 128], cur_ref[0:BAND, off:off + 128]], axis=0)
    return cur_ref[(s - 1) * BAND:(s + 1) * BAND, off:off + 128]


def _attn_layout(swa):
    if swa:
        return [(128 * j, 512, 640) for j in range(4)]
    return [(0, 128, 256), (384, 512, 640)]


def _attn_fwd(name, pv, swa, sink_row, tq):
    d, ls = pv.shape[0], pv.shape[1]
    n, nsb = ls // tq, tq // BAND
    pairs = _attn_layout(swa)
    ncol = 1 if swa else 2
    ow = 128 * len(pairs)

    def body(cur_ref, tail_ref, sink_ref, o_ref, lse_ref):
        i = pl.program_id(2)
        lane = lax.broadcasted_iota(jnp.int32, (BAND, 128), 1)
        lo = lane < HEAD_DIM
        for s in range(nsb):
            mask = _band_mask(i, s)
            rows = slice(s * BAND, (s + 1) * BAND)
            lse_tile = jnp.zeros((BAND, 128), F32)
            for j, (qo, ko, vo) in enumerate(pairs):
                q = cur_ref[rows, qo:qo + 128]
                kk = _kv_rows(cur_ref, tail_ref, s, ko)
                vv = _kv_rows(cur_ref, tail_ref, s, vo)
                halves = []
                for hf in (0, 1):
                    sel = lo if hf == 0 else jnp.logical_not(lo)
                    sc = _dot_nt(jnp.where(sel, q, jnp.zeros_like(q)), kk)
                    sc = jnp.where(mask, sc, -jnp.inf)
                    m = jnp.max(sc, axis=-1, keepdims=True)
                    if swa:
                        sk = sink_ref[:, 2 * j + hf:2 * j + hf + 1]
                        m = jnp.maximum(m, sk)
                    p = jnp.exp(sc - m)
                    den = jnp.sum(p, axis=-1, keepdims=True)
                    if swa:
                        den = den + jnp.exp(sk - m)
                    lse_tile = jnp.where(lane == 2 * j + hf, m + jnp.log(den), lse_tile)
                    halves.append(_dot((p / den).astype(BF16), vv))
                o_ref[rows, j * 128:(j + 1) * 128] = jnp.where(lo, halves[0], halves[1]).astype(BF16)
            lse_ref[rows, :] = lse_tile

    return pl.pallas_call(
        body, name=name, grid=(d, ncol, n),
        in_specs=[pl.BlockSpec((None, tq, PBLK), lambda r, cb, i: (r, i, cb)),
                  pl.BlockSpec((None, BAND, PBLK), lambda r, cb, i: (r, jnp.maximum(i * nsb - 1, 0), cb)),
                  pl.BlockSpec((1, 128), lambda r, cb, i: (0, 0))],
        out_specs=[pl.BlockSpec((None, tq, ow), lambda r, cb, i: (r, i, cb)),
                   pl.BlockSpec((None, tq, 128), lambda r, cb, i: (r, i, cb))],
        out_shape=[jax.ShapeDtypeStruct((d, ls, 512), BF16), jax.ShapeDtypeStruct((d, ls, 128 * ncol), F32)],
        compiler_params=_params(("arbitrary", "arbitrary", "arbitrary")),
    )(pv, pv, sink_row)


def _lse_lane(h):
    return (h // 4) * 128 + h % 4


def _head_scale(x, tile, lanes):
    lane = lax.broadcasted_iota(jnp.int32, (x.shape[0], 128), 1)
    lo = lane < HEAD_DIM
    out = []
    for c in range(x.shape[1] // 128):
        a0 = tile[:, lanes[2 * c]:lanes[2 * c] + 1]
        a1 = tile[:, lanes[2 * c + 1]:lanes[2 * c + 1] + 1]
        out.append(x[:, c * 128:(c + 1) * 128] * jnp.where(lo, a0, a1))
    return jnp.concatenate(out, axis=1)


def _head_sums(x, lanes, width):
    lane = lax.broadcasted_iota(jnp.int32, (x.shape[0], width), 1)
    out = jnp.zeros((x.shape[0], width), F32)
    for h in range(x.shape[1] // HEAD_DIM):
        sm = jnp.sum(x[:, h * HEAD_DIM:(h + 1) * HEAD_DIM], axis=-1, keepdims=True)
        out = jnp.where(lane == lanes[h], sm, out)
    return out


def _alphas(l0, l1, l2):
    m = jnp.maximum(jnp.maximum(l0, l1), l2)
    e0, e1, e2 = jnp.exp(l0 - m), jnp.exp(l1 - m), jnp.exp(l2 - m)
    den = e0 + e1 + e2
    return e0 / den, e1 / den, e2 / den


DIL_LANES = [_lse_lane(h) for h in range(8)]
SWA_LANES = list(range(8))


def _mix(o0, o1, o2, l0, l1, l2, ob, gts, x, w_a, w_b, w_out, g_cross, tm):
    t = x.shape[0]

    def body(o0_ref, o1_ref, o2_ref, l0_ref, l1_ref, l2_ref, ob_ref, g_ref, x_ref, wa_ref, wb_ref, wo_ref,
             gc_ref, oa_ref, ya_ref, yb_ref, mg_ref, x1_ref, hc_ref, so_ref, sl_ref):
        a0, a1, a2 = _alphas(l0_ref[...], _interleave(l1_ref, sl_ref), _interleave(l2_ref, sl_ref))
        oa = (_head_scale(o0_ref[...].astype(F32), a0, DIL_LANES)
              + _head_scale(_interleave(o1_ref, so_ref), a1, DIL_LANES)
              + _head_scale(_interleave(o2_ref, so_ref), a2, DIL_LANES))
        oab = oa.astype(BF16)
        oa_ref[...] = oab
        ya = _dot(oab, wa_ref[...])
        yb = _dot(ob_ref[...], wb_ref[...])
        ya_ref[...] = ya.astype(BF16)
        yb_ref[...] = yb.astype(BF16)
        merged = (g_ref[:, :D_MODEL].astype(F32) * ya + g_ref[:, D_MODEL:].astype(F32) * yb).astype(BF16)
        mg_ref[...] = merged
        x1 = x_ref[...] + _dot(merged, wo_ref[...])
        x1_ref[...] = x1
        hc_ref[...] = (x1 * _rms(x1) * gc_ref[...]).astype(BF16)

    row = lambda w: pl.BlockSpec((tm, w), lambda i: (i, 0))
    full = lambda a, b: pl.BlockSpec((a, b), lambda i: (0, 0))
    return pl.pallas_call(
        body, name="mix", grid=(t // tm,),
        in_specs=[row(512), _res_spec(o1, tm), _res_spec(o2, tm), row(256), _res_spec(l1, tm), _res_spec(l2, tm),
                  row(512), row(GATE_WIDTH),
                  row(D_MODEL), full(512, D_MODEL), full(512, D_MODEL), full(D_MODEL, D_MODEL), full(1, D_MODEL)],
        out_specs=[row(512), row(D_MODEL), row(D_MODEL), row(D_MODEL), row(D_MODEL), row(D_MODEL)],
        out_shape=[jax.ShapeDtypeStruct((t, 512), BF16), jax.ShapeDtypeStruct((t, D_MODEL), BF16),
                   jax.ShapeDtypeStruct((t, D_MODEL), BF16), jax.ShapeDtypeStruct((t, D_MODEL), BF16),
                   jax.ShapeDtypeStruct((t, D_MODEL), F32), jax.ShapeDtypeStruct((t, D_MODEL), BF16)],
        scratch_shapes=[_lane_scratch(tm, 512), _lane_scratch(tm, 256)],
        compiler_params=_params(("arbitrary",)),
    )(o0, o1, o2, l0, l1, l2, ob, gts, x, w_a, w_b, w_out, g_cross)


def _memkv(mem, g_mem, w_ckv):
    m = mem.shape[0]

    def body(mem_ref, g_ref, w_ref, mn_ref, kv_ref):
        xt = mem_ref[...]
        mn = (xt * _rms(xt) * g_ref[...]).astype(BF16)
        mn_ref[...] = mn
        kv_ref[...] = _dot(mn, w_ref[...]).astype(BF16)

    return pl.pallas_call(
        body, name="memkv",
        out_shape=[jax.ShapeDtypeStruct((m, D_MODEL), BF16), jax.ShapeDtypeStruct((m, 2 * D_MODEL), BF16)],
        compiler_params=_params(),
    )(mem, g_mem, w_ckv)


def _cross_probs(q, kv_ref, h):
    k = kv_ref[:, h * X_HEAD_DIM:(h + 1) * X_HEAD_DIM]
    sc = _dot_nt(q[:, h * X_HEAD_DIM:(h + 1) * X_HEAD_DIM], k)
    m = jnp.max(sc, axis=-1, keepdims=True)
    p = jnp.exp(sc - m)
    return p / jnp.sum(p, axis=-1, keepdims=True)


def _cross(hc, x1, kv, w_cq, w_co, g_mlp, tm):
    t = x1.shape[0]
    m = kv.shape[0]

    def body(hc_ref, x1_ref, kv_ref, wq_ref, wo_ref, g_ref, q_ref, o_ref, x2_ref, hm_ref):
        q = (_dot(hc_ref[...], wq_ref[...]) * X_SCALE).astype(BF16)
        q_ref[...] = q
        outs = []
        for h in range(X_HEADS):
            p = _cross_probs(q, kv_ref, h)
            v = kv_ref[:, D_MODEL + h * X_HEAD_DIM:D_MODEL + (h + 1) * X_HEAD_DIM]
            outs.append(_dot(p.astype(BF16), v))
        o = jnp.concatenate(outs, axis=1).astype(BF16)
        o_ref[...] = o
        x2 = x1_ref[...] + _dot(o, wo_ref[...])
        x2_ref[...] = x2
        hm_ref[...] = (x2 * _rms(x2) * g_ref[...]).astype(BF16)

    row = lambda w: pl.BlockSpec((tm, w), lambda i: (i, 0))
    full = lambda a, b: pl.BlockSpec((a, b), lambda i: (0, 0))
    return pl.pallas_call(
        body, name="cross", grid=(t // tm,),
        in_specs=[row(D_MODEL), row(D_MODEL), full(m, 2 * D_MODEL), full(D_MODEL, D_MODEL),
                  full(D_MODEL, D_MODEL), full(1, D_MODEL)],
        out_specs=[row(D_MODEL)] * 4,
        out_shape=[jax.ShapeDtypeStruct((t, D_MODEL), BF16), jax.ShapeDtypeStruct((t, D_MODEL), BF16),
                   jax.ShapeDtypeStruct((t, D_MODEL), F32), jax.ShapeDtypeStruct((t, D_MODEL), BF16)],
        compiler_params=_params(("arbitrary",)),
    )(hc, x1, kv, w_cq, w_co, g_mlp)


def _mlp(hm, x2, w_1, w_2, g_final, target, tm, tf):
    t = x2.shape[0]
    nf = D_FF // tf

    def body(hm_ref, x2_ref, w1_ref, w2_ref, g_ref, tg_ref, a_ref, dx3_ref, loss_ref, dg_ref, acc_ref):
        i, f = pl.program_id(0), pl.program_id(1)
        a = jnp.maximum(_dot(hm_ref[...], w1_ref[...]), 0.0)
        a_ref[...] = a.astype(BF16)
        part = _dot((a * a).astype(BF16), w2_ref[...])

        @pl.when(f == 0)
        def _():
            acc_ref[...] = part

        @pl.when(f > 0)
        def _():
            acc_ref[...] += part

        @pl.when((i == 0) & (f == 0))
        def _():
            loss_ref[...] = jnp.zeros_like(loss_ref)
            dg_ref[...] = jnp.zeros_like(dg_ref)

        @pl.when(f == nf - 1)
        def _():
            x3 = x2_ref[...] + acc_ref[...]
            r = _rms(x3)
            g = g_ref[...]
            diff = x3 * r * g - tg_ref[...]
            loss_ref[...] += 0.5 * jnp.sum(jnp.mean(diff * diff, axis=-1, keepdims=True))
            dx3, dg = _rms_bwd(diff / D_MODEL, x3, r, g)
            dx3_ref[...] = dx3
            dg_ref[...] += dg

    return pl.pallas_call(
        body, name="mlp", grid=(t // tm, nf),
        in_specs=[pl.BlockSpec((tm, D_MODEL), lambda i, f: (i, 0)),
                  pl.BlockSpec((tm, D_MODEL), lambda i, f: (i, 0)),
                  pl.BlockSpec((D_MODEL, tf), lambda i, f: (0, f)),
                  pl.BlockSpec((tf, D_MODEL), lambda i, f: (f, 0)),
                  pl.BlockSpec((1, D_MODEL), lambda i, f: (0, 0)),
                  pl.BlockSpec((tm, D_MODEL), lambda i, f: (i, 0))],
        out_specs=[pl.BlockSpec((tm, tf), lambda i, f: (i, f)),
                   pl.BlockSpec((tm, D_MODEL), lambda i, f: (i, 0)),
                   pl.BlockSpec((1, 128), lambda i, f: (0, 0)),
                   pl.BlockSpec((1, D_MODEL), lambda i, f: (0, 0))],
        out_shape=[jax.ShapeDtypeStruct((t, D_FF), BF16), jax.ShapeDtypeStruct((t, D_MODEL), F32),
                   jax.ShapeDtypeStruct((1, 128), F32), jax.ShapeDtypeStruct((1, D_MODEL), F32)],
        scratch_shapes=[pltpu.VMEM((tm, D_MODEL), F32)],
        compiler_params=_params(("arbitrary", "arbitrary")),
    )(hm, x2, w_1, w_2, g_final, target)


def _mlp_bwd(dx3, a, w_1, w_2, x2, g_mlp, tm, tf):
    t = x2.shape[0]
    nf = D_FF // tf

    def body(dx3_ref, a_ref, w1_ref, w2_ref, x2_ref, g_ref, dz_ref, dx2_ref, dg_ref, acc_ref):
        i, f = pl.program_id(0), pl.program_id(1)
        da2 = _dot_nt(dx3_ref[...].astype(BF16), w2_ref[...])
        dz = (2.0 * a_ref[...].astype(F32) * da2).astype(BF16)
        dz_ref[...] = dz
        part = _dot_nt(dz, w1_ref[...])

        @pl.when(f == 0)
        def _():
            acc_ref[...] = part

        @pl.when(f > 0)
        def _():
            acc_ref[...] += part

        @pl.when((i == 0) & (f == 0))
        def _():
            dg_ref[...] = jnp.zeros_like(dg_ref)

        @pl.when(f == nf - 1)
        def _():
            xt = x2_ref[...]
            dx, dg = _rms_bwd(acc_ref[...], xt, _rms(xt), g_ref[...])
            dx2_ref[...] = dx3_ref[...] + dx
            dg_ref[...] += dg

    return pl.pallas_call(
        body, name="mlp_bwd", grid=(t // tm, nf),
        in_specs=[pl.BlockSpec((tm, D_MODEL), lambda i, f: (i, 0)),
                  pl.BlockSpec((tm, tf), lambda i, f: (i, f)),
                  pl.BlockSpec((D_MODEL, tf), lambda i, f: (0, f)),
                  pl.BlockSpec((tf, D_MODEL), lambda i, f: (f, 0)),
                  pl.BlockSpec((tm, D_MODEL), lambda i, f: (i, 0)),
                  pl.BlockSpec((1, D_MODEL), lambda i, f: (0, 0))],
        out_specs=[pl.BlockSpec((tm, tf), lambda i, f: (i, f)),
                   pl.BlockSpec((tm, D_MODEL), lambda i, f: (i, 0)),
                   pl.BlockSpec((1, D_MODEL), lambda i, f: (0, 0))],
        out_shape=[jax.ShapeDtypeStruct((t, D_FF), BF16), jax.ShapeDtypeStruct((t, D_MODEL), F32),
                   jax.ShapeDtypeStruct((1, D_MODEL), F32)],
        scratch_shapes=[pltpu.VMEM((tm, D_MODEL), F32)],
        compiler_params=_params(("arbitrary", "arbitrary")),
    )(dx3, a, w_1, w_2, x2, g_mlp)


def _wgrad(name, a, b, tka, tn, tm, square=False):
    t, ka = a.shape
    n = b.shape[1]

    def body(a_ref, b_ref, o_ref):
        at = a_ref[...]
        if square:
            af = at.astype(F32)
            at = af * af
        part = _dot_tn(at.astype(BF16), b_ref[...].astype(BF16))

        @pl.when(pl.program_id(2) == 0)
        def _():
            o_ref[...] = part

        @pl.when(pl.program_id(2) > 0)
        def _():
            o_ref[...] += part

    return pl.pallas_call(
        body, name=name, grid=(ka // tka, n // tn, t // tm),
        in_specs=[pl.BlockSpec((tm, tka), lambda p, q, k: (k, p)),
                  pl.BlockSpec((tm, tn), lambda p, q, k: (k, q))],
        out_specs=pl.BlockSpec((tka, tn), lambda p, q, k: (p, q)),
        out_shape=jax.ShapeDtypeStruct((ka, n), F32),
        compiler_params=_params(("arbitrary", "arbitrary", "arbitrary")),
    )(a, b)


def _cross_bwd(dx2, x1, q, kv, w_cq, w_co, g_cross, tm):
    t = x1.shape[0]
    m = kv.shape[0]

    def body(dx2_ref, x1_ref, q_ref, kv_ref, wq_ref, wo_ref, g_ref, dq_ref, dx1_ref, dkv_ref, dg_ref):
        @pl.when(pl.program_id(0) == 0)
        def _():
            dkv_ref[...] = jnp.zeros_like(dkv_ref)
            dg_ref[...] = jnp.zeros_like(dg_ref)

        do = _dot_nt(dx2_ref[...].astype(BF16), wo_ref[...]).astype(BF16)
        q = q_ref[...]
        dqs = []
        for h in range(X_HEADS):
            hs = slice(h * X_HEAD_DIM, (h + 1) * X_HEAD_DIM)
            vs = slice(D_MODEL + h * X_HEAD_DIM, D_MODEL + (h + 1) * X_HEAD_DIM)
            p = _cross_probs(q, kv_ref, h)
            dp = _dot_nt(do[:, hs], kv_ref[:, vs])
            ds = (p * (dp - jnp.sum(dp * p, axis=-1, keepdims=True))).astype(BF16)
            dqs.append(_dot(ds, kv_ref[:, hs]))
            dkv_ref[:, hs] += _dot_tn(ds, q[:, hs])
            dkv_ref[:, vs] += _dot_tn(p.astype(BF16), do[:, hs])
        dq = (jnp.concatenate(dqs, axis=1) * X_SCALE).astype(BF16)
        dq_ref[...] = dq
        xt = x1_ref[...]
        dx, dg = _rms_bwd(_dot_nt(dq, wq_ref[...]), xt, _rms(xt), g_ref[...])
        dx1_ref[...] = dx2_ref[...] + dx
        dg_ref[...] += dg

    row = lambda w: pl.BlockSpec((tm, w), lambda i: (i, 0))
    full = lambda a, b: pl.BlockSpec((a, b), lambda i: (0, 0))
    return pl.pallas_call(
        body, name="cross_bwd", grid=(t // tm,),
        in_specs=[row(D_MODEL), row(D_MODEL), row(D_MODEL), full(m, 2 * D_MODEL), full(D_MODEL, D_MODEL),
                  full(D_MODEL, D_MODEL), full(1, D_MODEL)],
        out_specs=[row(D_MODEL), row(D_MODEL), full(m, 2 * D_MODEL), full(1, D_MODEL)],
        out_shape=[jax.ShapeDtypeStruct((t, D_MODEL), BF16), jax.ShapeDtypeStruct((t, D_MODEL), F32),
                   jax.ShapeDtypeStruct((m, 2 * D_MODEL), F32), jax.ShapeDtypeStruct((1, D_MODEL), F32)],
        compiler_params=_params(("arbitrary",)),
    )(dx2, x1, q, kv, w_cq, w_co, g_cross)


def _memkv_bwd(dkv, mn, mem, w_ckv, g_mem):
    def body(dkv_ref, mn_ref, mem_ref, w_ref, g_ref, dw_ref, dg_ref):
        dkvb = dkv_ref[...].astype(BF16)
        dw_ref[...] = _dot_tn(mn_ref[...], dkvb)
        dmn = _dot_nt(dkvb, w_ref[...])
        xt = mem_ref[...]
        dg_ref[...] = jnp.sum(dmn * xt * _rms(xt), axis=0, keepdims=True)

    return pl.pallas_call(
        body, name="memkv_bwd",
        out_shape=[jax.ShapeDtypeStruct((D_MODEL, 2 * D_MODEL), F32), jax.ShapeDtypeStruct((1, D_MODEL), F32)],
        compiler_params=_params(),
    )(dkv, mn, mem, w_ckv, g_mem)


def _mix_bwd(dx1, ya, yb, gts, oa, ob, l0, l1, l2, lb, sink_row, w_out, w_a, w_b, w_g, tm):
    t = dx1.shape[0]

    def body(dx1_ref, ya_ref, yb_ref, g_ref, oa_ref, ob_ref, l0_ref, l1_ref, l2_ref, lb_ref, sk_ref,
             wo_ref, wa_ref, wb_ref, wg_ref,
             dg_ref, dhp_ref, dya_ref, dyb_ref, do0_ref, do1_ref, do2_ref, c0_ref, c1_ref, c2_ref,
             dob_ref, cb_ref, db_ref, dsk_ref, so_ref, sl_ref):
        @pl.when(pl.program_id(0) == 0)
        def _():
            db_ref[...] = jnp.zeros_like(db_ref)
            dsk_ref[...] = jnp.zeros_like(dsk_ref)

        dm = _dot_nt(dx1_ref[...].astype(BF16), wo_ref[...])
        ga = g_ref[:, :D_MODEL].astype(F32)
        gb = g_ref[:, D_MODEL:].astype(F32)
        dya = (dm * ga).astype(BF16)
        dyb = (dm * gb).astype(BF16)
        dya_ref[...] = dya
        dyb_ref[...] = dyb
        dpa = dm * ya_ref[...].astype(F32) * ga * (1.0 - ga)
        dpb = dm * yb_ref[...].astype(F32) * gb * (1.0 - gb)
        dpre = jnp.concatenate([dpa, dpb], axis=1)
        db_ref[...] += jnp.sum(dpre, axis=0, keepdims=True)
        dpreb = dpre.astype(BF16)
        dg_ref[...] = dpreb
        dhp_ref[...] = _dot_nt(dpreb, wg_ref[...])

        doa = _dot_nt(dya, wa_ref[...])
        dob = _dot_nt(dyb, wb_ref[...])
        dsum = _head_sums(doa * oa_ref[...].astype(F32), DIL_LANES, 256)
        a0, a1, a2 = _alphas(l0_ref[...], _interleave(l1_ref, sl_ref), _interleave(l2_ref, sl_ref))
        c0_ref[...] = a0 * dsum
        do0_ref[...] = _head_scale(doa, a0, DIL_LANES).astype(BF16)
        for al, do_ref, c_ref in ((a1, do1_ref, c1_ref), (a2, do2_ref, c2_ref)):
            _deinterleave(al * dsum, sl_ref, c_ref, F32)
            _deinterleave(_head_scale(doa, al, DIL_LANES), so_ref, do_ref, BF16)
        dob_ref[...] = dob.astype(BF16)
        cb = _head_sums(dob * ob_ref[...].astype(F32), SWA_LANES, 128)
        cb_ref[...] = cb
        lane = lax.broadcasted_iota(jnp.int32, cb.shape, 1)
        psink = jnp.where(lane < 8, jnp.exp(sk_ref[...] - lb_ref[...]), 0.0)
        dsk_ref[...] += jnp.sum(-psink * cb, axis=0, keepdims=True)

    row = lambda w: pl.BlockSpec((tm, w), lambda i: (i, 0))
    full = lambda a, b: pl.BlockSpec((a, b), lambda i: (0, 0))
    sds = jax.ShapeDtypeStruct
    d1, d2 = l1.shape[0], l2.shape[0]
    res = lambda d, w: pl.BlockSpec((d, tm // d, w), lambda i: (0, i, 0))
    return pl.pallas_call(
        body, name="mix_bwd", grid=(t // tm,),
        in_specs=[row(D_MODEL), row(D_MODEL), row(D_MODEL), row(GATE_WIDTH), row(512), row(512),
                  row(256), _res_spec(l1, tm), _res_spec(l2, tm), row(128), full(1, 128),
                  full(D_MODEL, D_MODEL), full(512, D_MODEL), full(512, D_MODEL), full(D_MODEL, GATE_WIDTH)],
        out_specs=[row(GATE_WIDTH), row(D_MODEL), row(D_MODEL), row(D_MODEL),
                   row(512), res(d1, 512), res(d2, 512), row(256), res(d1, 256), res(d2, 256),
                   row(512), row(128), full(1, GATE_WIDTH), full(1, 128)],
        out_shape=[sds((t, GATE_WIDTH), BF16), sds((t, D_MODEL), F32), sds((t, D_MODEL), BF16),
                   sds((t, D_MODEL), BF16), sds((t, 512), BF16), sds((d1, t // d1, 512), BF16),
                   sds((d2, t // d2, 512), BF16), sds((t, 256), F32), sds((d1, t // d1, 256), F32),
                   sds((d2, t // d2, 256), F32), sds((t, 512), BF16),
                   sds((t, 128), F32), sds((1, GATE_WIDTH), F32), sds((1, 128), F32)],
        scratch_shapes=[_lane_scratch(tm, 512), _lane_scratch(tm, 256)],
        compiler_params=_params(("arbitrary",)),
    )(dx1, ya, yb, gts, oa, ob, l0, l1, l2, lb, sink_row, w_out, w_a, w_b, w_g)


def _attn_bwd(name, pv, dov, lsev, cv, cosv, sinv, swa, tq):
    d, ls = pv.shape[0], pv.shape[1]
    n, nsb = ls // tq, tq // BAND
    pairs = _attn_layout(swa)
    ncol = 1 if swa else 2
    ow = 128 * len(pairs)

    def body(cur_ref, tail_ref, do_ref, lse_ref, c_ref, cos_ref, sin_ref, out_ref, acc_ref, carry_ref):
        i = pl.program_id(2)
        acc_ref[...] = jnp.zeros_like(acc_ref)

        @pl.when(i < n)
        def _():
            lo = lax.broadcasted_iota(jnp.int32, (BAND, 128), 1) < HEAD_DIM
            lo2 = lax.broadcasted_iota(jnp.int32, (2 * BAND, 128), 1) < HEAD_DIM
            for s in range(nsb):
                mask = _band_mask(i, s)
                rows = slice(s * BAND, (s + 1) * BAND)
                krows = slice(s * BAND, (s + 2) * BAND)
                for j, (qo, ko, vo) in enumerate(pairs):
                    q = cur_ref[rows, qo:qo + 128]
                    kk = _kv_rows(cur_ref, tail_ref, s, ko)
                    vv = _kv_rows(cur_ref, tail_ref, s, vo)
                    do = do_ref[rows, j * 128:(j + 1) * 128]
                    dqh, dkh, dvh = [], [], []
                    for hf in (0, 1):
                        sel = lo if hf == 0 else jnp.logical_not(lo)
                        idx = 2 * j + hf
                        sc = _dot_nt(jnp.where(sel, q, jnp.zeros_like(q)), kk)
                        p = jnp.exp(jnp.where(mask, sc, -jnp.inf) - lse_ref[rows, idx:idx + 1])
                        dp = _dot_nt(jnp.where(sel, do, jnp.zeros_like(do)), vv)
                        ds = (p * (dp - c_ref[rows, idx:idx + 1])).astype(BF16)
                        dqh.append(_dot(ds, kk))
                        dkh.append(_dot_tn(ds, q))
                        dvh.append(_dot_tn(p.astype(BF16), do))
                    acc_ref[BAND + s * BAND:BAND + (s + 1) * BAND, qo:qo + 128] += jnp.where(lo, dqh[0], dqh[1])
                    acc_ref[krows, ko:ko + 128] += jnp.where(lo2, dkh[0], dkh[1])
                    acc_ref[krows, vo:vo + 128] += jnp.where(lo2, dvh[0], dvh[1])

        @pl.when(i >= 1)
        def _():
            if tq > BAND:
                fin = jnp.concatenate([carry_ref[0:tq - BAND, :], carry_ref[tq - BAND:, :] + acc_ref[0:BAND, :]], axis=0)
            else:
                fin = carry_ref[...] + acc_ref[0:BAND, :]
            cp, sp = _rope_pattern(cos_ref[...], sin_ref[...], swa)
            out_ref[...] = (fin * cp - _partner(fin) * sp).astype(BF16)

        carry_ref[...] = acc_ref[BAND:, :]

    qi = lambda i: jnp.minimum(i, n - 1)
    pi = lambda i: jnp.maximum(i - 1, 0)
    blk = lambda rows, w, row_of: pl.BlockSpec((None, rows, w), lambda r, cb, i: (r, row_of(i), cb))
    return pl.pallas_call(
        body, name=name, grid=(d, ncol, n + 1),
        in_specs=[blk(tq, PBLK, qi), blk(BAND, PBLK, lambda i: jnp.maximum(qi(i) * nsb - 1, 0)),
                  blk(tq, ow, qi), blk(tq, 128, qi), blk(tq, 128, qi),
                  pl.BlockSpec((None, tq, 128), lambda r, cb, i: (r, pi(i), 0)),
                  pl.BlockSpec((None, tq, 128), lambda r, cb, i: (r, pi(i), 0))],
        out_specs=blk(tq, PBLK, pi),
        out_shape=jax.ShapeDtypeStruct((d, ls, ncol * PBLK), BF16),
        scratch_shapes=[pltpu.VMEM((tq + BAND, PBLK), F32), pltpu.VMEM((tq, PBLK), F32)],
        compiler_params=_params(("arbitrary", "arbitrary", "arbitrary")),
    )(pv, pv, dov, lsev, cv, cosv, sinv)


def _dx(dp0, dp1, dp2, dpb, w_p, dh_part, dx1, x, g_mix, tm):
    t = x.shape[0]
    nk = P_WIDTH // PBLK

    def body(dp0_ref, dp1_ref, dp2_ref, dpb_ref, w_ref, dhp_ref, dx1_ref, x_ref, g_ref, gx_ref, dg_ref,
             acc_ref, dpt_ref, scr_ref):
        i, k = pl.program_id(0), pl.program_id(1)

        @pl.when(k < 2)
        def _():
            dpt_ref[...] = dp0_ref[...]

        @pl.when((k >= 2) & (k < 4))
        def _():
            dpt_ref[...] = _interleave(dp1_ref, scr_ref).astype(BF16)

        @pl.when((k >= 4) & (k < 6))
        def _():
            dpt_ref[...] = _interleave(dp2_ref, scr_ref).astype(BF16)

        @pl.when(k == nk - 1)
        def _():
            dpt_ref[...] = dpb_ref[...]

        part = _dot_nt(dpt_ref[...], w_ref[...])

        @pl.when(k == 0)
        def _():
            acc_ref[...] = part + dhp_ref[...]

        @pl.when(k > 0)
        def _():
            acc_ref[...] += part

        @pl.when((i == 0) & (k == 0))
        def _():
            dg_ref[...] = jnp.zeros_like(dg_ref)

        @pl.when(k == nk - 1)
        def _():
            xt = x_ref[...]
            dx, dg = _rms_bwd(acc_ref[...], xt, _rms(xt), g_ref[...])
            gx_ref[...] = dx1_ref[...] + dx
            dg_ref[...] += dg

    row = pl.BlockSpec((tm, D_MODEL), lambda i, k: (i, 0))
    d1, d2 = dp1.shape[0], dp2.shape[0]
    res = lambda d, lo: pl.BlockSpec((d, tm // d, PBLK), lambda i, k: (0, i, jnp.clip(k - lo, 0, 1)))
    return pl.pallas_call(
        body, name="dx", grid=(t // tm, nk),
        in_specs=[pl.BlockSpec((tm, PBLK), lambda i, k: (i, jnp.minimum(k, 1))), res(d1, 2), res(d2, 4),
                  pl.BlockSpec((tm, PBLK), lambda i, k: (i, 0)),
                  pl.BlockSpec((D_MODEL, PBLK), lambda i, k: (0, k)),
                  row, row, row, pl.BlockSpec((1, D_MODEL), lambda i, k: (0, 0))],
        out_specs=[row, pl.BlockSpec((1, D_MODEL), lambda i, k: (0, 0))],
        out_shape=[jax.ShapeDtypeStruct((t, D_MODEL), F32), jax.ShapeDtypeStruct((1, D_MODEL), F32)],
        scratch_shapes=[pltpu.VMEM((tm, D_MODEL), F32), pltpu.VMEM((tm, PBLK), BF16), _lane_scratch(tm, PBLK)],
        compiler_params=_params(("arbitrary", "arbitrary")),
    )(dp0, dp1, dp2, dpb, w_p, dh_part, dx1, x, g_mix)


MESH = pl.DeviceIdType.MESH
HBM_SPEC = pl.BlockSpec(memory_space=pltpu.HBM)
VMEM_SPEC = pl.BlockSpec(memory_space=pltpu.VMEM)


def _all_gather(xp):
    r = xp.shape[0]

    def body(x_ref, out_ref, send_sems, recv_sems, local_sem):
        x, y, c = lax.axis_index("x"), lax.axis_index("y"), lax.axis_index("c")
        me, sibling = (x, y, c), (x, y, 1 - c)
        chips = [(1 - x, y), (x, 1 - y), (1 - x, 1 - y)]

        def rows(px, py, pc):
            return out_ref.at[4 * px + 2 * py + pc]

        def copy(k, block, to, src=None):
            return pltpu.make_async_remote_copy(
                src_ref=rows(*block) if src is None else src, dst_ref=rows(*block),
                send_sem=send_sems.at[k], recv_sem=recv_sems.at[k], device_id=to, device_id_type=MESH)

        mine = pltpu.make_async_copy(x_ref, rows(*me), local_sem)
        mine.start()
        first = [copy(0, me, sibling, src=x_ref)]
        first += [copy(1 + j, me, (*chip, c), src=x_ref) for j, chip in enumerate(chips)]
        for cp in first:
            cp.start()
        passed = [copy(4 + j, (*chip, c), sibling) for j, chip in enumerate(chips)]
        for j, chip in enumerate(chips):
            copy(1 + j, (*chip, c), me).wait_recv()
            passed[j].start()
        copy(0, sibling, me).wait_recv()
        for j, chip in enumerate(chips):
            copy(4 + j, (*chip, 1 - c), me).wait_recv()
        for cp in first + passed:
            cp.wait_send()
        mine.wait()

    return pl.pallas_call(
        body, name="all_gather",
        out_shape=jax.ShapeDtypeStruct((N_DEV, r, LANES), xp.dtype),
        in_specs=[HBM_SPEC], out_specs=HBM_SPEC,
        scratch_shapes=[pltpu.SemaphoreType.DMA((7,)), pltpu.SemaphoreType.DMA((7,)), pltpu.SemaphoreType.DMA],
    )(xp)


def _exchange(gp, sp):
    r = gp.shape[1]

    def body(g_ref, s_ref, recv_ref, srecv_ref, send_sems, recv_sems, local_sem):
        x, y, c = lax.axis_index("x"), lax.axis_index("y"), lax.axis_index("c")
        me_idx = 4 * x + 2 * y + c
        mine = pltpu.make_async_copy(g_ref.at[me_idx], recv_ref.at[0], local_sem)
        mine.start()
        srecv_ref[pl.ds(me_idx, 1)] = s_ref[...][None]
        copies = []
        for k in range(1, N_DEV):
            px = 1 - x if k & 4 else x
            py = 1 - y if k & 2 else y
            pc = 1 - c if k & 1 else c
            peer = (px, py, pc)
            copies.append(pltpu.make_async_remote_copy(
                src_ref=g_ref.at[4 * px + 2 * py + pc], dst_ref=recv_ref.at[k],
                send_sem=send_sems.at[k - 1], recv_sem=recv_sems.at[k - 1], device_id=peer, device_id_type=MESH))
            copies.append(pltpu.make_async_remote_copy(
                src_ref=s_ref, dst_ref=srecv_ref.at[me_idx],
                send_sem=send_sems.at[6 + k], recv_sem=recv_sems.at[6 + k], device_id=peer, device_id_type=MESH))
        for cp in copies:
            cp.start()
        for cp in copies:
            cp.wait_recv()
        for cp in copies:
            cp.wait_send()
        mine.wait()

    return pl.pallas_call(
        body, name="grad_exchange",
        out_shape=[jax.ShapeDtypeStruct((N_DEV, r, LANES), gp.dtype), jax.ShapeDtypeStruct((N_DEV, 8, LANES), F32)],
        in_specs=[HBM_SPEC, VMEM_SPEC], out_specs=[HBM_SPEC, VMEM_SPEC],
        scratch_shapes=[pltpu.SemaphoreType.DMA((14,)), pltpu.SemaphoreType.DMA((14,)), pltpu.SemaphoreType.DMA],
    )(gp, sp)


def _adamw(name, parts, w, m, v, tr):
    rows = w.shape[0]

    def body(p_ref, w_ref, m_ref, v_ref, g_ref, d_ref, nm_ref, nv_ref):
        g = p_ref[0].astype(F32)
        for k in range(1, N_DEV):
            g = g + p_ref[k].astype(F32)
        g_ref[...] = g
        nm = ADAM_B1 * m_ref[...] + (1.0 - ADAM_B1) * g
        nv = ADAM_B2 * v_ref[...] + (1.0 - ADAM_B2) * (g * g)
        nm_ref[...] = nm
        nv_ref[...] = nv
        m_hat = nm / (1.0 - ADAM_B1 ** ADAM_STEP)
        v_hat = nv / (1.0 - ADAM_B2 ** ADAM_STEP)
        d_ref[...] = -ADAM_LR * (m_hat / (jnp.sqrt(v_hat) + ADAM_EPS) + ADAM_WD * w_ref[...])

    blk = pl.BlockSpec((tr, LANES), lambda i: (i, 0))
    return pl.pallas_call(
        body, name=name, grid=(rows // tr,),
        in_specs=[pl.BlockSpec((N_DEV, tr, LANES), lambda i: (0, i, 0)), blk, blk, blk],
        out_specs=[blk] * 4,
        out_shape=[jax.ShapeDtypeStruct((rows, LANES), F32)] * 4,
        compiler_params=_params(("arbitrary",)),
    )(parts, w, m, v)


def _pack(shards):
    rows = []
    for name, n in _PACK:
        a = shards[name].reshape(-1)
        a = jnp.pad(a, (0, n * LANES - a.shape[0]))
        rows.append(a.reshape(n, LANES))
    used = sum(n for _, n in _PACK)
    rows.append(jnp.zeros((PACK_ROWS - used, LANES), rows[0].dtype))
    return jnp.concatenate(rows, axis=0)


def _unpack(buf, shapes):
    out, r0 = {}, 0
    for name, n in _PACK:
        size = math.prod(shapes[name])
        out[name] = buf[r0:r0 + n].reshape(-1)[:size].reshape(shapes[name])
        r0 += n
    return out


_SHARD_SHAPES = {"w_in": (1024, 928), "w_branch_a": (512, 128), "w_branch_b": (512, 128), "w_out": (128, 1024),
                 "w_cq": (128, 1024), "w_ckv": (1024, 256), "w_co": (128, 1024), "w_1": (1024, 512),
                 "w_2": (512, 1024), "b_gate": (2, 128)}
_COL_SHARDED = ("w_in", "w_branch_a", "w_branch_b", "w_ckv", "w_1", "b_gate")


def _full_from_gathered(gathered):
    out, r0 = {}, 0
    for name, n in _PACK:
        shp = _SHARD_SHAPES[name]
        size = math.prod(shp)
        if name == "b_gate":
            a = lax.bitcast_convert_type(gathered[:, r0, :2 * size].reshape(N_DEV, size, 2), F32)
            a = a.reshape((N_DEV,) + shp)
        else:
            a = gathered[:, r0:r0 + n].reshape(N_DEV, -1)[:, :size].reshape((N_DEV,) + shp)
        if name in _COL_SHARDED:
            a = jnp.swapaxes(a, 0, 1).reshape(shp[0], N_DEV * shp[1])
        else:
            a = a.reshape(N_DEV * shp[0], shp[1])
        out[name] = a
        r0 += n
    return out


def _shards_from_full(full):
    rows = []
    for name, n in _PACK:
        shp = _SHARD_SHAPES[name]
        a = full[name]
        if name in _COL_SHARDED:
            a = jnp.swapaxes(a.reshape(shp[0], N_DEV, shp[1]), 0, 1)
        a = a.reshape(N_DEV, -1).astype(BF16)
        a = jnp.pad(a, ((0, 0), (0, n * LANES - a.shape[1])))
        rows.append(a.reshape(N_DEV, n, LANES))
    used = sum(n for _, n in _PACK)
    rows.append(jnp.zeros((N_DEV, PACK_ROWS - used, LANES), BF16))
    return jnp.concatenate(rows, axis=1)


def _split_w_in(w_in):
    rows = w_in.shape[0]
    dil = w_in[:, :3 * DIL_WIDTH].reshape(rows, 3, 3, 4, 128)
    dil = dil.transpose(0, 2, 3, 1, 4).reshape(rows, 3 * DIL_WIDTH)
    o = 3 * DIL_WIDTH
    qb = w_in[:, o:o + SWA_Q_WIDTH].reshape(rows, 2, 4, HEAD_DIM).transpose(0, 2, 1, 3).reshape(rows, SWA_Q_WIDTH)
    kvb = w_in[:, o + SWA_Q_WIDTH:P_WIDTH]
    return jnp.concatenate([dil, qb, kvb], axis=1), w_in[:, P_WIDTH:]


def _merge_w_in(dw_p, dw_g):
    rows = dw_p.shape[0]
    dil = dw_p[:, :3 * DIL_WIDTH].reshape(rows, 3, 4, 3, 128).transpose(0, 3, 1, 2, 4).reshape(rows, 3 * DIL_WIDTH)
    o = 3 * DIL_WIDTH
    qb = dw_p[:, o:o + SWA_Q_WIDTH].reshape(rows, 4, 2, HEAD_DIM).transpose(0, 2, 1, 3).reshape(rows, SWA_Q_WIDTH)
    return jnp.concatenate([dil, qb, dw_p[:, o + SWA_Q_WIDTH:], dw_g], axis=1)


def _swa_rows(w_b):
    return w_b.reshape(2, 4, HEAD_DIM, -1).transpose(1, 0, 2, 3).reshape(SWA_Q_WIDTH, -1)


def _swa_rows_inv(dw_b):
    return dw_b.reshape(4, 2, HEAD_DIM, -1).transpose(1, 0, 2, 3).reshape(SWA_Q_WIDTH, -1)


def _rope_tables(pos):
    half = HEAD_DIM // 2
    inv = ROPE_THETA ** (-jnp.arange(half, dtype=F32) / half)
    ang = pos.astype(F32)[:, None] * inv
    c, s = jnp.cos(ang), jnp.sin(ang)
    return jnp.concatenate([c, c, c, c], axis=1), jnp.concatenate([-s, s, -s, s], axis=1)


def _local_step(x, mem, pos, target, wts, g_mix, g_cross, g_mem, g_mlp, g_final, sink):
    t = x.shape[0]
    tm = min(512, t)
    tq = min(256, t // 16)
    w_p, w_g = _split_w_in(wts["w_in"])
    w_b = _swa_rows(wts["w_branch_b"])
    b_gate = wts["b_gate"].astype(F32).reshape(1, GATE_WIDTH)
    cos, sin = _rope_tables(pos)
    sink_row = jnp.pad(sink.reshape(2, 4).T.reshape(1, 8), ((0, 0), (0, 120)))
    tabs = [(cos[None], sin[None])]
    for _, d in DIL_GROUPS[1:]:
        c_d, s_d = _rope_tables(pos.reshape(t // d, d).T.reshape(-1))
        tabs.append((c_d.reshape(d, t // d, 128), s_d.reshape(d, t // d, 128)))
    tabs.append(tabs[0])

    h, h1, h2, p0, p1, p2, pb = _inproj(x, g_mix, w_p, cos, sin, tm)
    gts = _gates(h, w_g, b_gate, tm, 1024)
    ps = [p0[None], p1, p2, pb[None]]
    outs, lses = [], []
    for gi, pv in enumerate(ps):
        o, l = _attn_fwd(f"attn_fwd{gi}", pv, gi == 3, sink_row, tq)
        outs.append(o)
        lses.append(l)
    o0, l0, ob, lb = outs[0][0], lses[0][0], outs[3][0], lses[3][0]
    oa, ya, yb, merged, x1, hc = _mix(o0, outs[1], outs[2], l0, lses[1], lses[2], ob, gts, x,
                                      wts["w_branch_a"], w_b, wts["w_out"], g_cross, tm)
    mn, kv = _memkv(mem, g_mem, wts["w_ckv"])
    q, o, x2, hm = _cross(hc, x1, kv, wts["w_cq"], wts["w_co"], g_mlp, tm)
    a, dx3, loss, dg_final = _mlp(hm, x2, wts["w_1"], wts["w_2"], g_final.reshape(1, D_MODEL), target, tm, 1024)

    grads = {}
    dz, dx2, dg_mlp = _mlp_bwd(dx3, a, wts["w_1"], wts["w_2"], x2, g_mlp, tm, 1024)
    grads["w_2"] = _wgrad("dw_2", a, dx3, 1024, 1024, tm, square=True)
    grads["w_1"] = _wgrad("dw_1", hm, dz, 1024, 1024, tm)
    dq, dx1, dkv, dg_cross = _cross_bwd(dx2, x1, q, kv, wts["w_cq"], wts["w_co"], g_cross, tm)
    grads["w_co"] = _wgrad("dw_co", o, dx2, 1024, 1024, tm)
    grads["w_cq"] = _wgrad("dw_cq", hc, dq, 1024, 1024, tm)
    grads["w_ckv"], dg_mem = _memkv_bwd(dkv, mn, mem, wts["w_ckv"], g_mem)
    (dgt, dh_part, dya, dyb, do0, do1, do2, c0, c1, c2, dob, cb, db_gate, dsink) = _mix_bwd(
        dx1, ya, yb, gts, oa, ob, l0, lses[1], lses[2], lb, sink_row,
        wts["w_out"], wts["w_branch_a"], w_b, w_g, min(256, tm))
    grads["w_out"] = _wgrad("dw_out", merged, dx1, 1024, 1024, tm)
    grads["w_branch_a"] = _wgrad("dw_a", oa, dya, 512, 1024, tm)
    grads["w_branch_b"] = _swa_rows_inv(_wgrad("dw_b", ob, dyb, 512, 1024, tm))
    dw_g = _wgrad("dw_g", h, dgt, 1024, 1024, tm)
    dps = []
    for gi, (pv, do_g, c_g) in enumerate(zip(ps, (do0[None], do1, do2, dob[None]), (c0[None], c1, c2, cb[None]))):
        dps.append(_attn_bwd(f"attn_bwd{gi}", pv, do_g, lses[gi], c_g, tabs[gi][0], tabs[gi][1], gi == 3, tq))
    dw_p = jnp.concatenate(
        [_wgrad(f"dw_p{gi}", hh.reshape(t, D_MODEL), dpg.reshape(t, -1), 1024, PBLK, tm)
         for gi, (hh, dpg) in enumerate(zip((h, h1, h2, h), dps))], axis=1)
    grads["w_in"] = _merge_w_in(dw_p, dw_g)
    grads["b_gate"] = db_gate.reshape(2, D_MODEL)
    grad_x, dg_mix = _dx(dps[0][0], dps[1], dps[2], dps[3][0], w_p, dh_part, dx1, x, g_mix, tm)
    dsink_heads = dsink[0, :8].reshape(4, 2).T.reshape(8)
    small = {"g_mix": dg_mix[0], "g_cross": dg_cross[0], "g_mem": dg_mem[0], "g_mlp": dg_mlp[0],
             "g_final": dg_final[0], "sink": dsink_heads}
    return loss[0, 0], grad_x, grads, small


def kernel(x, mem, positions, g_mix, w_in, b_gate, sink, w_branch_a, w_branch_b, w_out, g_cross, g_mem, w_cq, w_ckv, w_co, g_mlp, w_1, w_2, g_final, loss_target, m_g_mix, m_w_in, m_b_gate, m_sink, m_w_branch_a, m_w_branch_b, m_w_out, m_g_cross, m_g_mem, m_w_cq, m_w_ckv, m_w_co, m_g_mlp, m_w_1, m_w_2, m_g_final, v_g_mix, v_w_in, v_b_gate, v_sink, v_w_branch_a, v_w_branch_b, v_w_out, v_g_cross, v_g_mem, v_w_cq, v_w_ckv, v_w_co, v_g_mlp, v_w_1, v_w_2, v_g_final):
    local = dict(locals())
    big = [n for n, _ in _PACK]
    w_sh = {n: local[n][0] for n in big}
    m_sh = {n: local["m_" + n][0] for n in big}
    v_sh = {n: local["v_" + n][0] for n in big}

    w_pack = _pack(w_sh)
    bias_words = lax.bitcast_convert_type(w_sh["b_gate"].reshape(-1), BF16).reshape(1, -1)
    bias_rows = jnp.pad(bias_words, ((0, 15), (0, LANES - bias_words.shape[1])))
    w_pack16 = w_pack.astype(BF16)
    gathered = _all_gather(jnp.concatenate([w_pack16[:BIAS_ROW], bias_rows, w_pack16[BIAS_ROW + 16:]], axis=0))
    wts = _full_from_gathered(gathered)

    loss, grad_x, grads, small = _local_step(
        x[0], mem[0], positions[0], loss_target[0], wts, g_mix, g_cross, g_mem, g_mlp,
        g_final, sink[0])

    gp = _shards_from_full(grads)
    sp = jnp.stack([small[n] if n != "sink" else jnp.pad(small[n], (0, LANES - 8)) for n in SMALL]
                   + [jnp.zeros((LANES,), F32)] * 2)
    recv, srecv = _exchange(gp, sp)

    g_big, d_big, nm_big, nv_big = _adamw("adamw", recv, w_pack, _pack(m_sh), _pack(v_sh), 256)

    def small_pack(prefix):
        rows = []
        for n in SMALL:
            a = local[prefix + n].reshape(-1)
            rows.append(jnp.pad(a, (0, LANES - a.shape[0])))
        return jnp.stack(rows + [jnp.zeros((LANES,), F32)] * 2)

    g_sm, d_sm, nm_sm, nv_sm = _adamw("adamw_small", srecv, small_pack(""), small_pack("m_"), small_pack("v_"), 8)

    def collect(buf_big, buf_small):
        b = _unpack(buf_big, _SHARD_SHAPES)
        out = {n: b[n][None] for n in big}
        for i, n in enumerate(SMALL):
            shp = local[n].shape
            out[n] = buf_small[i, :math.prod(shp)].reshape(shp)
        return out

    names = ["g_mix", "w_in", "b_gate", "sink", "w_branch_a", "w_branch_b", "w_out", "g_cross", "g_mem", "w_cq",
             "w_ckv", "w_co", "g_mlp", "w_1", "w_2", "g_final"]
    res = [lax.psum(loss, ("x", "y", "c")), grad_x[None]]
    for bb, bs in ((g_big, g_sm), (d_big, d_sm), (nm_big, nm_sm), (nv_big, nv_sm)):
        c = collect(bb, bs)
        res += [c[n] for n in names]
    return tuple(res)
```

```python
import functools
import math

import jax
import jax.numpy as jnp
from jax import lax
from jax.experimental import pallas as pl
from jax.experimental.pallas import tpu as pltpu

F32 = jnp.float32
BF16 = jnp.bfloat16

D_MODEL = 1024
HEAD_DIM = 64
DIL_GROUPS = ((128, 1), (512, 4), (2048, 16))
ROPE_THETA = 10000.0
X_HEADS = 4
X_HEAD_DIM = D_MODEL // X_HEADS
D_FF = 4 * D_MODEL
EPS = 1e-6
DIL_WIDTH = 1536
SWA_Q_WIDTH = 512
SWA_KV_WIDTH = 128
P_WIDTH = 3 * DIL_WIDTH + SWA_Q_WIDTH + 2 * SWA_KV_WIDTH
GATE_WIDTH = 2 * D_MODEL
IN_WIDTH = P_WIDTH + GATE_WIDTH
BAND = 128
PBLK = 768
Q_SCALE = HEAD_DIM ** -0.5
X_SCALE = X_HEAD_DIM ** -0.5

ADAM_LR = 0.001
ADAM_B1 = 0.9
ADAM_B2 = 0.999
ADAM_EPS = 1e-08
ADAM_WD = 0.01
ADAM_STEP = 10

N_DEV = 8
LANES = 1024
VMEM_LIMIT = 52 * 1024 * 1024

NT = (((1,), (1,)), ((), ()))
TN = (((0,), (0,)), ((), ()))

_PACK = (("w_in", 928), ("w_branch_a", 64), ("w_branch_b", 64), ("w_out", 128), ("w_cq", 128),
         ("w_ckv", 256), ("w_co", 128), ("w_1", 512), ("w_2", 512), ("b_gate", 16))
PACK_ROWS = 2816
BIAS_ROW = 2720
SMALL = ("g_mix", "g_cross", "g_mem", "g_mlp", "g_final", "sink")


def _params(sem=None):
    return pltpu.CompilerParams(dimension_semantics=sem, vmem_limit_bytes=VMEM_LIMIT)


def _dot(a, b):
    return jnp.dot(a, b, preferred_element_type=F32)


def _dot_nt(a, b):
    return lax.dot_general(a, b, NT, preferred_element_type=F32)


def _dot_tn(a, b):
    return lax.dot_general(a, b, TN, preferred_element_type=F32)


def _rms(xt):
    return lax.rsqrt(jnp.mean(xt * xt, axis=-1, keepdims=True) + EPS)


def _rms_bwd(dh, xt, r, g):
    xn = xt * r
    dxn = dh * g
    dx = r * (dxn - xn * jnp.mean(dxn * xn, axis=-1, keepdims=True))
    return dx, jnp.sum(dh * xn, axis=0, keepdims=True)


def _rope(x, c, s, swa, sign):
    kinds = "qqqqkv" if swa else "qkvqkv"
    cq, sq = c * Q_SCALE, s * (sign * Q_SCALE)
    sk = s * sign if sign != 1 else s
    out = []
    for ci, kind in enumerate(kinds):
        xc = x[:, ci * 128:(ci + 1) * 128]
        if kind == "v":
            out.append(xc)
        elif kind == "q":
            out.append(xc * cq + pltpu.roll(xc, 64, 1) * sq)
        else:
            out.append(xc * c + pltpu.roll(xc, 64, 1) * sk)
    return jnp.concatenate(out, axis=1)


def _lane_scratch(rows, w):
    return pltpu.VMEM((w // 128, rows, 128), F32)


def _deinterleave(val, scr_ref, dst_ref, dtype):
    d, n = dst_ref.shape[0], dst_ref.shape[1]
    nc = val.shape[1] // 128
    for c in range(nc):
        scr_ref[c] = val[:, c * 128:(c + 1) * 128]
    for r in range(d):
        rows = [scr_ref.at[c][pl.ds(r, n, stride=d), :] for c in range(nc)]
        dst_ref[r] = jnp.concatenate(rows, axis=1).astype(dtype)


def _res_spec(a, tm):
    d, w = a.shape[0], a.shape[2]
    return pl.BlockSpec((d, tm // d, w), lambda i: (0, i, 0))


def _interleave(src_ref, scr_ref):
    d, n = src_ref.shape[0], src_ref.shape[1]
    nc = src_ref.shape[2] // 128
    for r in range(d):
        v = src_ref[r].astype(F32)
        for c in range(nc):
            scr_ref.at[c][pl.ds(r, n, stride=d), :] = v[:, c * 128:(c + 1) * 128]
    return jnp.concatenate([scr_ref[c] for c in range(nc)], axis=1)


def _inproj(x, g, w_p, cos, sin, tm):
    t = x.shape[0]
    nj = P_WIDTH // PBLK

    def body(x_ref, g_ref, w_ref, c_ref, s_ref, h_ref, h1_ref, h2_ref, p0_ref, p1_ref, p2_ref, pb_ref,
             hs_ref, hf_ref, pf_ref):
        j = pl.program_id(1)

        @pl.when(j == 0)
        def _():
            xt = x_ref[...]
            hf = xt * _rms(xt) * g_ref[...]
            hb = hf.astype(BF16)
            hs_ref[...] = hb
            h_ref[...] = hb
            _deinterleave(hf, hf_ref, h1_ref, BF16)
            _deinterleave(hf, hf_ref, h2_ref, BF16)

        acc = _dot(hs_ref[...], w_ref[...])

        def rope(swa):
            return _rope(acc, c_ref[...], s_ref[...], swa, 1)

        @pl.when(j < 2)
        def _():
            p0_ref[...] = rope(False).astype(BF16)

        @pl.when((j >= 2) & (j < 4))
        def _():
            _deinterleave(rope(False), pf_ref, p1_ref, BF16)

        @pl.when((j >= 4) & (j < 6))
        def _():
            _deinterleave(rope(False), pf_ref, p2_ref, BF16)

        @pl.when(j == nj - 1)
        def _():
            pb_ref[...] = rope(True).astype(BF16)

    d1, d2 = DIL_GROUPS[1][1], DIL_GROUPS[2][1]
    col = lambda lo: (lambda i, j: (0, i, jnp.clip(j - lo, 0, 1)))
    sds = jax.ShapeDtypeStruct
    return pl.pallas_call(
        body, name="inproj", grid=(t // tm, nj),
        in_specs=[pl.BlockSpec((tm, D_MODEL), lambda i, j: (i, 0)),
                  pl.BlockSpec((1, D_MODEL), lambda i, j: (0, 0)),
                  pl.BlockSpec((D_MODEL, PBLK), lambda i, j: (0, j)),
                  pl.BlockSpec((tm, 128), lambda i, j: (i, 0)),
                  pl.BlockSpec((tm, 128), lambda i, j: (i, 0))],
        out_specs=[pl.BlockSpec((tm, D_MODEL), lambda i, j: (i, 0)),
                   pl.BlockSpec((d1, tm // d1, D_MODEL), lambda i, j: (0, i, 0)),
                   pl.BlockSpec((d2, tm // d2, D_MODEL), lambda i, j: (0, i, 0)),
                   pl.BlockSpec((tm, PBLK), lambda i, j: (i, jnp.minimum(j, 1))),
                   pl.BlockSpec((d1, tm // d1, PBLK), col(2)),
                   pl.BlockSpec((d2, tm // d2, PBLK), col(4)),
                   pl.BlockSpec((tm, PBLK), lambda i, j: (i, 0))],
        out_shape=[sds((t, D_MODEL), BF16), sds((d1, t // d1, D_MODEL), BF16), sds((d2, t // d2, D_MODEL), BF16),
                   sds((t, 2 * PBLK), BF16), sds((d1, t // d1, 2 * PBLK), BF16), sds((d2, t // d2, 2 * PBLK), BF16),
                   sds((t, PBLK), BF16)],
        scratch_shapes=[pltpu.VMEM((tm, D_MODEL), BF16), _lane_scratch(tm, D_MODEL), _lane_scratch(tm, PBLK)],
        compiler_params=_params(("arbitrary", "arbitrary")),
    )(x, g, w_p, cos, sin)


def _gates(h, w_g, b, tm, tn):
    t = h.shape[0]

    def body(h_ref, w_ref, b_ref, o_ref):
        z = _dot(h_ref[...], w_ref[...]) + b_ref[...]
        o_ref[...] = jax.nn.sigmoid(z).astype(BF16)

    return pl.pallas_call(
        body, name="gates", grid=(t // tm, GATE_WIDTH // tn),
        in_specs=[pl.BlockSpec((tm, D_MODEL), lambda i, j: (i, 0)),
                  pl.BlockSpec((D_MODEL, tn), lambda i, j: (0, j)),
                  pl.BlockSpec((1, tn), lambda i, j: (0, j))],
        out_specs=pl.BlockSpec((tm, tn), lambda i, j: (i, j)),
        out_shape=jax.ShapeDtypeStruct((t, GATE_WIDTH), BF16),
        compiler_params=_params(("arbitrary", "arbitrary")),
    )(h, w_g, b)


def _band_mask(i, s):
    row = lax.broadcasted_iota(jnp.int32, (BAND, 2 * BAND), 0)
    col = lax.broadcasted_iota(jnp.int32, (BAND, 2 * BAND), 1)
    band = (col >= row) & (col <= row + BAND)
    if s == 0:
        band = band & ((col >= BAND) | (i > 0))
    return band


def _head_a_masks(rows):
    lane = lax.broadcasted_iota(jnp.int32, (rows, 128), 1)
    return (lane % HEAD_DIM) < HEAD_DIM // 2, lane < HEAD_DIM


def _one_head(x, head_a, hf):
    zero = jnp.zeros_like(x)
    return jnp.where(head_a, x, zero) if hf == 0 else jnp.where(head_a, zero, x)


def _kv_rows(cur_ref, tail_ref, s, off):
    if s == 0:
        return jnp.concatenate([tail_ref[:, off:off + 128], cur_ref[0:BAND, off:off + 128]], axis=0)
    return cur_ref[(s - 1) * BAND:(s + 1) * BAND, off:off + 128]


def _attn_layout(swa):
    if swa:
        return [(128 * j, 512, 640) for j in range(4)]
    return [(0, 128, 256), (384, 512, 640)]


def _attn_fwd(name, pv, swa, sinks, tq):
    d, ls = pv.shape[0], pv.shape[1]
    n, nsb = ls // tq, tq // BAND
    pairs = _attn_layout(swa)
    ncol = 1 if swa else 2
    ow = 128 * len(pairs)

    def body(cur_ref, tail_ref, *rest):
        sink_ref = rest[0] if swa else None
        o_ref, lse_ref = rest[-2:]
        i = pl.program_id(2)
        lane = lax.broadcasted_iota(jnp.int32, (BAND, 128), 1)
        head_a = _head_a_masks(BAND)
        for s in range(nsb):
            mask = _band_mask(i, s)
            rows = slice(s * BAND, (s + 1) * BAND)
            lse_tile = jnp.zeros((BAND, 128), F32)
            for j, (qo, ko, vo) in enumerate(pairs):
                q = cur_ref[rows, qo:qo + 128]
                kk = _kv_rows(cur_ref, tail_ref, s, ko)
                vv = _kv_rows(cur_ref, tail_ref, s, vo)
                halves = []
                for hf in (0, 1):
                    sc = _dot_nt(_one_head(q, head_a[0], hf), kk)
                    sc = jnp.where(mask, sc, -jnp.inf)
                    m = jnp.max(sc, axis=-1, keepdims=True)
                    if swa:
                        sk = sink_ref[2 * j + hf]
                        m = jnp.maximum(m, sk)
                    p = jnp.exp(sc - m)
                    den = jnp.sum(p, axis=-1, keepdims=True)
                    if swa:
                        den = den + jnp.exp(sk - m)
                    lse_tile = jnp.where(lane == 2 * j + hf, m + jnp.log(den), lse_tile)
                    halves.append(_dot((p * (1.0 / den)).astype(BF16), vv))
                o_ref[rows, j * 128:(j + 1) * 128] = jnp.where(head_a[1], halves[0], halves[1]).astype(BF16)
            lse_ref[rows, :] = lse_tile

    in_specs = [pl.BlockSpec((None, tq, PBLK), lambda r, cb, i: (r, i, cb)),
                pl.BlockSpec((None, BAND, PBLK), lambda r, cb, i: (r, jnp.maximum(i * nsb - 1, 0), cb))]
    args = [pv, pv]
    if swa:
        in_specs.append(pl.BlockSpec(memory_space=pltpu.SMEM))
        args.append(sinks)
    return pl.pallas_call(
        body, name=name, grid=(d, ncol, n),
        in_specs=in_specs,
        out_specs=[pl.BlockSpec((None, tq, ow), lambda r, cb, i: (r, i, cb)),
                   pl.BlockSpec((None, tq, 128), lambda r, cb, i: (r, i, cb))],
        out_shape=[jax.ShapeDtypeStruct((d, ls, 512), BF16), jax.ShapeDtypeStruct((d, ls, 128 * ncol), F32)],
        compiler_params=_params(("arbitrary", "arbitrary", "arbitrary")),
    )(*args)


def _lse_lane(h):
    return (h // 4) * 128 + h % 4


def _head_scale(x, tile, lanes):
    lane = lax.broadcasted_iota(jnp.int32, (x.shape[0], 128), 1)
    lo = lane < HEAD_DIM
    out = []
    for c in range(x.shape[1] // 128):
        a0 = tile[:, lanes[2 * c]:lanes[2 * c] + 1]
        a1 = tile[:, lanes[2 * c + 1]:lanes[2 * c + 1] + 1]
        out.append(x[:, c * 128:(c + 1) * 128] * jnp.where(lo, a0, a1))
    return jnp.concatenate(out, axis=1)


def _head_sums(x, lanes, width):
    lane = lax.broadcasted_iota(jnp.int32, (x.shape[0], width), 1)
    out = jnp.zeros((x.shape[0], width), F32)
    for h in range(x.shape[1] // HEAD_DIM):
        sm = jnp.sum(x[:, h * HEAD_DIM:(h + 1) * HEAD_DIM], axis=-1, keepdims=True)
        out = jnp.where(lane == lanes[h], sm, out)
    return out


def _alphas(l0, l1, l2):
    m = jnp.maximum(jnp.maximum(l0, l1), l2)
    e0, e1, e2 = jnp.exp(l0 - m), jnp.exp(l1 - m), jnp.exp(l2 - m)
    den = e0 + e1 + e2
    return e0 / den, e1 / den, e2 / den


DIL_LANES = [_lse_lane(h) for h in range(8)]
SWA_LANES = list(range(8))


def _mix(o0, o1, o2, l0, l1, l2, ob, gts, x, w_a, w_b, w_out, g_cross, tm):
    t = x.shape[0]

    def body(o0_ref, o1_ref, o2_ref, l0_ref, l1_ref, l2_ref, ob_ref, g_ref, x_ref, wa_ref, wb_ref, wo_ref,
             gc_ref, oa_ref, ya_ref, yb_ref, mg_ref, x1_ref, hc_ref, so_ref, sl_ref):
        a0, a1, a2 = _alphas(l0_ref[...], _interleave(l1_ref, sl_ref), _interleave(l2_ref, sl_ref))
        oa = (_head_scale(o0_ref[...].astype(F32), a0, DIL_LANES)
              + _head_scale(_interleave(o1_ref, so_ref), a1, DIL_LANES)
              + _head_scale(_interleave(o2_ref, so_ref), a2, DIL_LANES))
        oab = oa.astype(BF16)
        oa_ref[...] = oab
        ya = _dot(oab, wa_ref[...])
        yb = _dot(ob_ref[...], wb_ref[...])
        ya_ref[...] = ya.astype(BF16)
        yb_ref[...] = yb.astype(BF16)
        merged = (g_ref[:, :D_MODEL].astype(F32) * ya + g_ref[:, D_MODEL:].astype(F32) * yb).astype(BF16)
        mg_ref[...] = merged
        x1 = x_ref[...] + _dot(merged, wo_ref[...])
        x1_ref[...] = x1
        hc_ref[...] = (x1 * _rms(x1) * gc_ref[...]).astype(BF16)

    row = lambda w: pl.BlockSpec((tm, w), lambda i: (i, 0))
    full = lambda a, b: pl.BlockSpec((a, b), lambda i: (0, 0))
    return pl.pallas_call(
        body, name="mix", grid=(t // tm,),
        in_specs=[row(512), _res_spec(o1, tm), _res_spec(o2, tm), row(256), _res_spec(l1, tm), _res_spec(l2, tm),
                  row(512), row(GATE_WIDTH),
                  row(D_MODEL), full(512, D_MODEL), full(512, D_MODEL), full(D_MODEL, D_MODEL), full(1, D_MODEL)],
        out_specs=[row(512), row(D_MODEL), row(D_MODEL), row(D_MODEL), row(D_MODEL), row(D_MODEL)],
        out_shape=[jax.ShapeDtypeStruct((t, 512), BF16), jax.ShapeDtypeStruct((t, D_MODEL), BF16),
                   jax.ShapeDtypeStruct((t, D_MODEL), BF16), jax.ShapeDtypeStruct((t, D_MODEL), BF16),
                   jax.ShapeDtypeStruct((t, D_MODEL), F32), jax.ShapeDtypeStruct((t, D_MODEL), BF16)],
        scratch_shapes=[_lane_scratch(tm, 512), _lane_scratch(tm, 256)],
        compiler_params=_params(("arbitrary",)),
    )(o0, o1, o2, l0, l1, l2, ob, gts, x, w_a, w_b, w_out, g_cross)


def _memkv(mem, g_mem, w_ckv):
    m = mem.shape[0]

    def body(mem_ref, g_ref, w_ref, mn_ref, kv_ref):
        xt = mem_ref[...]
        mn = (xt * _rms(xt) * g_ref[...]).astype(BF16)
        mn_ref[...] = mn
        kv_ref[...] = _dot(mn, w_ref[...]).astype(BF16)

    return pl.pallas_call(
        body, name="memkv",
        out_shape=[jax.ShapeDtypeStruct((m, D_MODEL), BF16), jax.ShapeDtypeStruct((m, 2 * D_MODEL), BF16)],
        compiler_params=_params(),
    )(mem, g_mem, w_ckv)


def _cross_probs(q, kv_ref, h):
    k = kv_ref[:, h * X_HEAD_DIM:(h + 1) * X_HEAD_DIM]
    sc = _dot_nt(q[:, h * X_HEAD_DIM:(h + 1) * X_HEAD_DIM], k)
    m = jnp.max(sc, axis=-1, keepdims=True)
    p = jnp.exp(sc - m)
    return p / jnp.sum(p, axis=-1, keepdims=True)


def _cross(hc, x1, kv, w_cq, w_co, g_mlp, tm):
    t = x1.shape[0]
    m = kv.shape[0]

    def body(hc_ref, x1_ref, kv_ref, wq_ref, wo_ref, g_ref, q_ref, o_ref, x2_ref, hm_ref):
        q = (_dot(hc_ref[...], wq_ref[...]) * X_SCALE).astype(BF16)
        q_ref[...] = q
        outs = []
        for h in range(X_HEADS):
            p = _cross_probs(q, kv_ref, h)
            v = kv_ref[:, D_MODEL + h * X_HEAD_DIM:D_MODEL + (h + 1) * X_HEAD_DIM]
            outs.append(_dot(p.astype(BF16), v))
        o = jnp.concatenate(outs, axis=1).astype(BF16)
        o_ref[...] = o
        x2 = x1_ref[...] + _dot(o, wo_ref[...])
        x2_ref[...] = x2
        hm_ref[...] = (x2 * _rms(x2) * g_ref[...]).astype(BF16)

    row = lambda w: pl.BlockSpec((tm, w), lambda i: (i, 0))
    full = lambda a, b: pl.BlockSpec((a, b), lambda i: (0, 0))
    return pl.pallas_call(
        body, name="cross", grid=(t // tm,),
        in_specs=[row(D_MODEL), row(D_MODEL), full(m, 2 * D_MODEL), full(D_MODEL, D_MODEL),
                  full(D_MODEL, D_MODEL), full(1, D_MODEL)],
        out_specs=[row(D_MODEL)] * 4,
        out_shape=[jax.ShapeDtypeStruct((t, D_MODEL), BF16), jax.ShapeDtypeStruct((t, D_MODEL), BF16),
                   jax.ShapeDtypeStruct((t, D_MODEL), F32), jax.ShapeDtypeStruct((t, D_MODEL), BF16)],
        compiler_params=_params(("arbitrary",)),
    )(hc, x1, kv, w_cq, w_co, g_mlp)


def _mlp(hm, x2, w_1, w_2, g_final, target, tm, tf):
    t = x2.shape[0]
    nf = D_FF // tf

    def body(hm_ref, x2_ref, w1_ref, w2_ref, g_ref, tg_ref, a_ref, dx3_ref, loss_ref, dg_ref, acc_ref):
        i, f = pl.program_id(0), pl.program_id(1)
        a = jnp.maximum(_dot(hm_ref[...], w1_ref[...]), 0.0)
        a_ref[...] = a.astype(BF16)
        part = _dot((a * a).astype(BF16), w2_ref[...])

        @pl.when(f == 0)
        def _():
            acc_ref[...] = part

        @pl.when(f > 0)
        def _():
            acc_ref[...] += part

        @pl.when((i == 0) & (f == 0))
        def _():
            loss_ref[...] = jnp.zeros_like(loss_ref)
            dg_ref[...] = jnp.zeros_like(dg_ref)

        @pl.when(f == nf - 1)
        def _():
            x3 = x2_ref[...] + acc_ref[...]
            r = _rms(x3)
            g = g_ref[...]
            diff = x3 * r * g - tg_ref[...]
            loss_ref[...] += 0.5 * jnp.sum(jnp.mean(diff * diff, axis=-1, keepdims=True))
            dx3, dg = _rms_bwd(diff / D_MODEL, x3, r, g)
            dx3_ref[...] = dx3
            dg_ref[...] += dg

    return pl.pallas_call(
        body, name="mlp", grid=(t // tm, nf),
        in_specs=[pl.BlockSpec((tm, D_MODEL), lambda i, f: (i, 0)),
                  pl.BlockSpec((tm, D_MODEL), lambda i, f: (i, 0)),
                  pl.BlockSpec((D_MODEL, tf), lambda i, f: (0, f)),
                  pl.BlockSpec((tf, D_MODEL), lambda i, f: (f, 0)),
                  pl.BlockSpec((1, D_MODEL), lambda i, f: (0, 0)),
                  pl.BlockSpec((tm, D_MODEL), lambda i, f: (i, 0))],
        out_specs=[pl.BlockSpec((tm, tf), lambda i, f: (i, f)),
                   pl.BlockSpec((tm, D_MODEL), lambda i, f: (i, 0)),
                   pl.BlockSpec((1, 128), lambda i, f: (0, 0)),
                   pl.BlockSpec((1, D_MODEL), lambda i, f: (0, 0))],
        out_shape=[jax.ShapeDtypeStruct((t, D_FF), BF16), jax.ShapeDtypeStruct((t, D_MODEL), F32),
                   jax.ShapeDtypeStruct((1, 128), F32), jax.ShapeDtypeStruct((1, D_MODEL), F32)],
        scratch_shapes=[pltpu.VMEM((tm, D_MODEL), F32)],
        compiler_params=_params(("arbitrary", "arbitrary")),
    )(hm, x2, w_1, w_2, g_final, target)


def _mlp_bwd(dx3, a, w_1, w_2, x2, g_mlp, tm, tf):
    t = x2.shape[0]
    nf = D_FF // tf

    def body(dx3_ref, a_ref, w1_ref, w2_ref, x2_ref, g_ref, dz_ref, dx2_ref, dg_ref, acc_ref):
        i, f = pl.program_id(0), pl.program_id(1)
        da2 = _dot_nt(dx3_ref[...].astype(BF16), w2_ref[...])
        dz = (2.0 * a_ref[...].astype(F32) * da2).astype(BF16)
        dz_ref[...] = dz
        part = _dot_nt(dz, w1_ref[...])

        @pl.when(f == 0)
        def _():
            acc_ref[...] = part

        @pl.when(f > 0)
        def _():
            acc_ref[...] += part

        @pl.when((i == 0) & (f == 0))
        def _():
            dg_ref[...] = jnp.zeros_like(dg_ref)

        @pl.when(f == nf - 1)
        def _():
            xt = x2_ref[...]
            dx, dg = _rms_bwd(acc_ref[...], xt, _rms(xt), g_ref[...])
            dx2_ref[...] = dx3_ref[...] + dx
            dg_ref[...] += dg

    return pl.pallas_call(
        body, name="mlp_bwd", grid=(t // tm, nf),
        in_specs=[pl.BlockSpec((tm, D_MODEL), lambda i, f: (i, 0)),
                  pl.BlockSpec((tm, tf), lambda i, f: (i, f)),
                  pl.BlockSpec((D_MODEL, tf), lambda i, f: (0, f)),
                  pl.BlockSpec((tf, D_MODEL), lambda i, f: (f, 0)),
                  pl.BlockSpec((tm, D_MODEL), lambda i, f: (i, 0)),
                  pl.BlockSpec((1, D_MODEL), lambda i, f: (0, 0))],
        out_specs=[pl.BlockSpec((tm, tf), lambda i, f: (i, f)),
                   pl.BlockSpec((tm, D_MODEL), lambda i, f: (i, 0)),
                   pl.BlockSpec((1, D_MODEL), lambda i, f: (0, 0))],
        out_shape=[jax.ShapeDtypeStruct((t, D_FF), BF16), jax.ShapeDtypeStruct((t, D_MODEL), F32),
                   jax.ShapeDtypeStruct((1, D_MODEL), F32)],
        scratch_shapes=[pltpu.VMEM((tm, D_MODEL), F32)],
        compiler_params=_params(("arbitrary", "arbitrary")),
    )(dx3, a, w_1, w_2, x2, g_mlp)


def _wgrad(name, a, b, tka, tn, tm, square=False):
    t, ka = a.shape
    n = b.shape[1]

    def body(a_ref, b_ref, o_ref):
        at = a_ref[...]
        if square:
            af = at.astype(F32)
            at = af * af
        part = _dot_tn(at.astype(BF16), b_ref[...].astype(BF16))

        @pl.when(pl.program_id(2) == 0)
        def _():
            o_ref[...] = part

        @pl.when(pl.program_id(2) > 0)
        def _():
            o_ref[...] += part

    return pl.pallas_call(
        body, name=name, grid=(ka // tka, n // tn, t // tm),
        in_specs=[pl.BlockSpec((tm, tka), lambda p, q, k: (k, p)),
                  pl.BlockSpec((tm, tn), lambda p, q, k: (k, q))],
        out_specs=pl.BlockSpec((tka, tn), lambda p, q, k: (p, q)),
        out_shape=jax.ShapeDtypeStruct((ka, n), F32),
        compiler_params=_params(("arbitrary", "arbitrary", "arbitrary")),
    )(a, b)


def _cross_bwd(dx2, x1, q, kv, w_cq, w_co, g_cross, tm):
    t = x1.shape[0]
    m = kv.shape[0]

    def body(dx2_ref, x1_ref, q_ref, kv_ref, wq_ref, wo_ref, g_ref, dq_ref, dx1_ref, dkv_ref, dg_ref):
        @pl.when(pl.program_id(0) == 0)
        def _():
            dkv_ref[...] = jnp.zeros_like(dkv_ref)
            dg_ref[...] = jnp.zeros_like(dg_ref)

        do = _dot_nt(dx2_ref[...].astype(BF16), wo_ref[...]).astype(BF16)
        q = q_ref[...]
        dqs = []
        for h in range(X_HEADS):
            hs = slice(h * X_HEAD_DIM, (h + 1) * X_HEAD_DIM)
            vs = slice(D_MODEL + h * X_HEAD_DIM, D_MODEL + (h + 1) * X_HEAD_DIM)
            p = _cross_probs(q, kv_ref, h)
            dp = _dot_nt(do[:, hs], kv_ref[:, vs])
            ds = (p * (dp - jnp.sum(dp * p, axis=-1, keepdims=True))).astype(BF16)
            dqs.append(_dot(ds, kv_ref[:, hs]))
            dkv_ref[:, hs] += _dot_tn(ds, q[:, hs])
            dkv_ref[:, vs] += _dot_tn(p.astype(BF16), do[:, hs])
        dq = (jnp.concatenate(dqs, axis=1) * X_SCALE).astype(BF16)
        dq_ref[...] = dq
        xt = x1_ref[...]
        dx, dg = _rms_bwd(_dot_nt(dq, wq_ref[...]), xt, _rms(xt), g_ref[...])
        dx1_ref[...] = dx2_ref[...] + dx
        dg_ref[...] += dg

    row = lambda w: pl.BlockSpec((tm, w), lambda i: (i, 0))
    full = lambda a, b: pl.BlockSpec((a, b), lambda i: (0, 0))
    return pl.pallas_call(
        body, name="cross_bwd", grid=(t // tm,),
        in_specs=[row(D_MODEL), row(D_MODEL), row(D_MODEL), full(m, 2 * D_MODEL), full(D_MODEL, D_MODEL),
                  full(D_MODEL, D_MODEL), full(1, D_MODEL)],
        out_specs=[row(D_MODEL), row(D_MODEL), full(m, 2 * D_MODEL), full(1, D_MODEL)],
        out_shape=[jax.ShapeDtypeStruct((t, D_MODEL), BF16), jax.ShapeDtypeStruct((t, D_MODEL), F32),
                   jax.ShapeDtypeStruct((m, 2 * D_MODEL), F32), jax.ShapeDtypeStruct((1, D_MODEL), F32)],
        compiler_params=_params(("arbitrary",)),
    )(dx2, x1, q, kv, w_cq, w_co, g_cross)


def _memkv_bwd(dkv, mn, mem, w_ckv, g_mem):
    def body(dkv_ref, mn_ref, mem_ref, w_ref, g_ref, dw_ref, dg_ref):
        dkvb = dkv_ref[...].astype(BF16)
        dw_ref[...] = _dot_tn(mn_ref[...], dkvb)
        dmn = _dot_nt(dkvb, w_ref[...])
        xt = mem_ref[...]
        dg_ref[...] = jnp.sum(dmn * xt * _rms(xt), axis=0, keepdims=True)

    return pl.pallas_call(
        body, name="memkv_bwd",
        out_shape=[jax.ShapeDtypeStruct((D_MODEL, 2 * D_MODEL), F32), jax.ShapeDtypeStruct((1, D_MODEL), F32)],
        compiler_params=_params(),
    )(dkv, mn, mem, w_ckv, g_mem)


def _mix_bwd(dx1, ya, yb, gts, oa, ob, l0, l1, l2, lb, sink_row, w_out, w_a, w_b, w_g, tm):
    t = dx1.shape[0]

    def body(dx1_ref, ya_ref, yb_ref, g_ref, oa_ref, ob_ref, l0_ref, l1_ref, l2_ref, lb_ref, sk_ref,
             wo_ref, wa_ref, wb_ref, wg_ref,
             dg_ref, dhp_ref, dya_ref, dyb_ref, do0_ref, do1_ref, do2_ref, c0_ref, c1_ref, c2_ref,
             dob_ref, cb_ref, db_ref, dsk_ref, so_ref, sl_ref):
        @pl.when(pl.program_id(0) == 0)
        def _():
            db_ref[...] = jnp.zeros_like(db_ref)
            dsk_ref[...] = jnp.zeros_like(dsk_ref)

        dm = _dot_nt(dx1_ref[...].astype(BF16), wo_ref[...])
        ga = g_ref[:, :D_MODEL].astype(F32)
        gb = g_ref[:, D_MODEL:].astype(F32)
        dya = (dm * ga).astype(BF16)
        dyb = (dm * gb).astype(BF16)
        dya_ref[...] = dya
        dyb_ref[...] = dyb
        dpa = dm * ya_ref[...].astype(F32) * ga * (1.0 - ga)
        dpb = dm * yb_ref[...].astype(F32) * gb * (1.0 - gb)
        dpre = jnp.concatenate([dpa, dpb], axis=1)
        db_ref[...] += jnp.sum(dpre, axis=0, keepdims=True)
        dpreb = dpre.astype(BF16)
        dg_ref[...] = dpreb
        dhp_ref[...] = _dot_nt(dpreb, wg_ref[...])

        doa = _dot_nt(dya, wa_ref[...])
        dob = _dot_nt(dyb, wb_ref[...])
        dsum = _head_sums(doa * oa_ref[...].astype(F32), DIL_LANES, 256)
        a0, a1, a2 = _alphas(l0_ref[...], _interleave(l1_ref, sl_ref), _interleave(l2_ref, sl_ref))
        c0_ref[...] = a0 * dsum
        do0_ref[...] = _head_scale(doa, a0, DIL_LANES).astype(BF16)
        for al, do_ref, c_ref in ((a1, do1_ref, c1_ref), (a2, do2_ref, c2_ref)):
            _deinterleave(al * dsum, sl_ref, c_ref, F32)
            _deinterleave(_head_scale(doa, al, DIL_LANES), so_ref, do_ref, BF16)
        dob_ref[...] = dob.astype(BF16)
        cb = _head_sums(dob * ob_ref[...].astype(F32), SWA_LANES, 128)
        cb_ref[...] = cb
        lane = lax.broadcasted_iota(jnp.int32, cb.shape, 1)
        psink = jnp.where(lane < 8, jnp.exp(sk_ref[...] - lb_ref[...]), 0.0)
        dsk_ref[...] += jnp.sum(-psink * cb, axis=0, keepdims=True)

    row = lambda w: pl.BlockSpec((tm, w), lambda i: (i, 0))
    full = lambda a, b: pl.BlockSpec((a, b), lambda i: (0, 0))
    sds = jax.ShapeDtypeStruct
    d1, d2 = l1.shape[0], l2.shape[0]
    res = lambda d, w: pl.BlockSpec((d, tm // d, w), lambda i: (0, i, 0))
    return pl.pallas_call(
        body, name="mix_bwd", grid=(t // tm,),
        in_specs=[row(D_MODEL), row(D_MODEL), row(D_MODEL), row(GATE_WIDTH), row(512), row(512),
                  row(256), _res_spec(l1, tm), _res_spec(l2, tm), row(128), full(1, 128),
                  full(D_MODEL, D_MODEL), full(512, D_MODEL), full(512, D_MODEL), full(D_MODEL, GATE_WIDTH)],
        out_specs=[row(GATE_WIDTH), row(D_MODEL), row(D_MODEL), row(D_MODEL),
                   row(512), res(d1, 512), res(d2, 512), row(256), res(d1, 256), res(d2, 256),
                   row(512), row(128), full(1, GATE_WIDTH), full(1, 128)],
        out_shape=[sds((t, GATE_WIDTH), BF16), sds((t, D_MODEL), F32), sds((t, D_MODEL), BF16),
                   sds((t, D_MODEL), BF16), sds((t, 512), BF16), sds((d1, t // d1, 512), BF16),
                   sds((d2, t // d2, 512), BF16), sds((t, 256), F32), sds((d1, t // d1, 256), F32),
                   sds((d2, t // d2, 256), F32), sds((t, 512), BF16),
                   sds((t, 128), F32), sds((1, GATE_WIDTH), F32), sds((1, 128), F32)],
        scratch_shapes=[_lane_scratch(tm, 512), _lane_scratch(tm, 256)],
        compiler_params=_params(("arbitrary",)),
    )(dx1, ya, yb, gts, oa, ob, l0, l1, l2, lb, sink_row, w_out, w_a, w_b, w_g)


def _attn_bwd(name, pv, dov, lsev, cv, cosv, sinv, swa, tq):
    d, ls = pv.shape[0], pv.shape[1]
    n, nsb = ls // tq, tq // BAND
    pairs = _attn_layout(swa)
    ncol = 1 if swa else 2
    ow = 128 * len(pairs)

    def body(cur_ref, tail_ref, do_ref, lse_ref, c_ref, cos_ref, sin_ref, out_ref, acc_ref, carry_ref):
        i = pl.program_id(2)
        acc_ref[...] = jnp.zeros_like(acc_ref)

        @pl.when(i < n)
        def _():
            qk_a, v_a = _head_a_masks(BAND)
            qk_a2, v_a2 = _head_a_masks(2 * BAND)
            for s in range(nsb):
                mask = _band_mask(i, s)
                rows = slice(s * BAND, (s + 1) * BAND)
                krows = slice(s * BAND, (s + 2) * BAND)
                for j, (qo, ko, vo) in enumerate(pairs):
                    q = cur_ref[rows, qo:qo + 128]
                    kk = _kv_rows(cur_ref, tail_ref, s, ko)
                    vv = _kv_rows(cur_ref, tail_ref, s, vo)
                    do = do_ref[rows, j * 128:(j + 1) * 128]
                    dqh, dkh, dvh = [], [], []
                    for hf in (0, 1):
                        idx = 2 * j + hf
                        sc = _dot_nt(_one_head(q, qk_a, hf), kk)
                        p = jnp.exp(jnp.where(mask, sc, -jnp.inf) - lse_ref[rows, idx:idx + 1])
                        dp = _dot_nt(_one_head(do, v_a, hf), vv)
                        ds = (p * (dp - c_ref[rows, idx:idx + 1])).astype(BF16)
                        dqh.append(_dot(ds, kk))
                        dkh.append(_dot_tn(ds, q))
                        dvh.append(_dot_tn(p.astype(BF16), do))
                    acc_ref[BAND + s * BAND:BAND + (s + 1) * BAND, qo:qo + 128] += jnp.where(qk_a, dqh[0], dqh[1])
                    acc_ref[krows, ko:ko + 128] += jnp.where(qk_a2, dkh[0], dkh[1])
                    acc_ref[krows, vo:vo + 128] += jnp.where(v_a2, dvh[0], dvh[1])

        @pl.when(i >= 1)
        def _():
            if tq > BAND:
                fin = jnp.concatenate([carry_ref[0:tq - BAND, :], carry_ref[tq - BAND:, :] + acc_ref[0:BAND, :]], axis=0)
            else:
                fin = carry_ref[...] + acc_ref[0:BAND, :]
            out_ref[...] = _rope(fin, cos_ref[...], sin_ref[...], swa, -1).astype(BF16)

        carry_ref[...] = acc_ref[BAND:, :]

    qi = lambda i: jnp.minimum(i, n - 1)
    pi = lambda i: jnp.maximum(i - 1, 0)
    blk = lambda rows, w, row_of: pl.BlockSpec((None, rows, w), lambda r, cb, i: (r, row_of(i), cb))
    return pl.pallas_call(
        body, name=name, grid=(d, ncol, n + 1),
        in_specs=[blk(tq, PBLK, qi), blk(BAND, PBLK, lambda i: jnp.maximum(qi(i) * nsb - 1, 0)),
                  blk(tq, ow, qi), blk(tq, 128, qi), blk(tq, 128, qi),
                  pl.BlockSpec((None, tq, 128), lambda r, cb, i: (r, pi(i), 0)),
                  pl.BlockSpec((None, tq, 128), lambda r, cb, i: (r, pi(i), 0))],
        out_specs=blk(tq, PBLK, pi),
        out_shape=jax.ShapeDtypeStruct((d, ls, ncol * PBLK), BF16),
        scratch_shapes=[pltpu.VMEM((tq + BAND, PBLK), F32), pltpu.VMEM((tq, PBLK), F32)],
        compiler_params=_params(("arbitrary", "arbitrary", "arbitrary")),
    )(pv, pv, dov, lsev, cv, cosv, sinv)


def _dx(dp0, dp1, dp2, dpb, w_p, dh_part, dx1, x, g_mix, tm):
    t = x.shape[0]
    gw = 2 * PBLK

    def body(dp0_ref, dp1_ref, dp2_ref, dpb_ref, w_ref, dhp_ref, dx1_ref, x_ref, g_ref, gx_ref, dg_ref,
             dpt_ref, scr_ref):
        @pl.when(pl.program_id(0) == 0)
        def _():
            dg_ref[...] = jnp.zeros_like(dg_ref)

        dpt_ref[:, 0:gw] = dp0_ref[...]
        dpt_ref[:, gw:2 * gw] = _interleave(dp1_ref, scr_ref).astype(BF16)
        dpt_ref[:, 2 * gw:3 * gw] = _interleave(dp2_ref, scr_ref).astype(BF16)
        dpt_ref[:, 3 * gw:] = dpb_ref[...]
        dh = _dot_nt(dpt_ref[...], w_ref[...]) + dhp_ref[...]
        xt = x_ref[...]
        dx, dg = _rms_bwd(dh, xt, _rms(xt), g_ref[...])
        gx_ref[...] = dx1_ref[...] + dx
        dg_ref[...] += dg

    row = lambda w: pl.BlockSpec((tm, w), lambda i: (i, 0))
    full = lambda a, b: pl.BlockSpec((a, b), lambda i: (0, 0))
    return pl.pallas_call(
        body, name="dx", grid=(t // tm,),
        in_specs=[row(gw), _res_spec(dp1, tm), _res_spec(dp2, tm), row(PBLK), full(D_MODEL, P_WIDTH),
                  row(D_MODEL), row(D_MODEL), row(D_MODEL), full(1, D_MODEL)],
        out_specs=[row(D_MODEL), full(1, D_MODEL)],
        out_shape=[jax.ShapeDtypeStruct((t, D_MODEL), F32), jax.ShapeDtypeStruct((1, D_MODEL), F32)],
        scratch_shapes=[pltpu.VMEM((tm, P_WIDTH), BF16), _lane_scratch(tm, gw)],
        compiler_params=_params(("arbitrary",)),
    )(dp0, dp1, dp2, dpb, w_p, dh_part, dx1, x, g_mix)


MESH = pl.DeviceIdType.MESH
HBM_SPEC = pl.BlockSpec(memory_space=pltpu.HBM)
VMEM_SPEC = pl.BlockSpec(memory_space=pltpu.VMEM)


def _all_gather(xp):
    r = xp.shape[0]

    def body(x_ref, out_ref, send_sems, recv_sems, local_sem):
        x, y, c = lax.axis_index("x"), lax.axis_index("y"), lax.axis_index("c")
        me, sibling = (x, y, c), (x, y, 1 - c)
        chips = [(1 - x, y), (x, 1 - y), (1 - x, 1 - y)]

        def rows(px, py, pc):
            return out_ref.at[4 * px + 2 * py + pc]

        def copy(k, block, to, src=None):
            return pltpu.make_async_remote_copy(
                src_ref=rows(*block) if src is None else src, dst_ref=rows(*block),
                send_sem=send_sems.at[k], recv_sem=recv_sems.at[k], device_id=to, device_id_type=MESH)

        mine = pltpu.make_async_copy(x_ref, rows(*me), local_sem)
        mine.start()
        first = [copy(0, me, sibling, src=x_ref)]
        first += [copy(1 + j, me, (*chip, c), src=x_ref) for j, chip in enumerate(chips)]
        for cp in first:
            cp.start()
        passed = [copy(4 + j, (*chip, c), sibling) for j, chip in enumerate(chips)]
        for j, chip in enumerate(chips):
            copy(1 + j, (*chip, c), me).wait_recv()
            passed[j].start()
        copy(0, sibling, me).wait_recv()
        for j, chip in enumerate(chips):
            copy(4 + j, (*chip, 1 - c), me).wait_recv()
        for cp in first + passed:
            cp.wait_send()
        mine.wait()

    return pl.pallas_call(
        body, name="all_gather",
        out_shape=jax.ShapeDtypeStruct((N_DEV, r, LANES), xp.dtype),
        in_specs=[HBM_SPEC], out_specs=HBM_SPEC,
        scratch_shapes=[pltpu.SemaphoreType.DMA((7,)), pltpu.SemaphoreType.DMA((7,)), pltpu.SemaphoreType.DMA],
    )(xp)


def _exchange(gp, sp):
    r = gp.shape[1]

    def body(g_ref, s_ref, recv_ref, srecv_ref, send_sems, recv_sems, local_sem):
        x, y, c = lax.axis_index("x"), lax.axis_index("y"), lax.axis_index("c")
        me_idx = 4 * x + 2 * y + c
        mine = pltpu.make_async_copy(g_ref.at[me_idx], recv_ref.at[0], local_sem)
        mine.start()
        srecv_ref[pl.ds(me_idx, 1)] = s_ref[...][None]
        copies = []
        for k in range(1, N_DEV):
            px = 1 - x if k & 4 else x
            py = 1 - y if k & 2 else y
            pc = 1 - c if k & 1 else c
            peer = (px, py, pc)
            copies.append(pltpu.make_async_remote_copy(
                src_ref=g_ref.at[4 * px + 2 * py + pc], dst_ref=recv_ref.at[k],
                send_sem=send_sems.at[k - 1], recv_sem=recv_sems.at[k - 1], device_id=peer, device_id_type=MESH))
            copies.append(pltpu.make_async_remote_copy(
                src_ref=s_ref, dst_ref=srecv_ref.at[me_idx],
                send_sem=send_sems.at[6 + k], recv_sem=recv_sems.at[6 + k], device_id=peer, device_id_type=MESH))
        for cp in copies:
            cp.start()
        for cp in copies:
            cp.wait_recv()
        for cp in copies:
            cp.wait_send()
        mine.wait()

    return pl.pallas_call(
        body, name="grad_exchange",
        out_shape=[jax.ShapeDtypeStruct((N_DEV, r, LANES), gp.dtype), jax.ShapeDtypeStruct((N_DEV, 8, LANES), F32)],
        in_specs=[HBM_SPEC, VMEM_SPEC], out_specs=[HBM_SPEC, VMEM_SPEC],
        scratch_shapes=[pltpu.SemaphoreType.DMA((14,)), pltpu.SemaphoreType.DMA((14,)), pltpu.SemaphoreType.DMA],
    )(gp, sp)


def _adamw(name, parts, w, m, v, tr):
    rows = w.shape[0]

    def body(p_ref, w_ref, m_ref, v_ref, g_ref, d_ref, nm_ref, nv_ref):
        g = p_ref[0].astype(F32)
        for k in range(1, N_DEV):
            g = g + p_ref[k].astype(F32)
        g_ref[...] = g
        nm = ADAM_B1 * m_ref[...] + (1.0 - ADAM_B1) * g
        nv = ADAM_B2 * v_ref[...] + (1.0 - ADAM_B2) * (g * g)
        nm_ref[...] = nm
        nv_ref[...] = nv
        m_hat = nm / (1.0 - ADAM_B1 ** ADAM_STEP)
        v_hat = nv / (1.0 - ADAM_B2 ** ADAM_STEP)
        d_ref[...] = -ADAM_LR * (m_hat / (jnp.sqrt(v_hat) + ADAM_EPS) + ADAM_WD * w_ref[...])

    blk = pl.BlockSpec((tr, LANES), lambda i: (i, 0))
    return pl.pallas_call(
        body, name=name, grid=(rows // tr,),
        in_specs=[pl.BlockSpec((N_DEV, tr, LANES), lambda i: (0, i, 0)), blk, blk, blk],
        out_specs=[blk] * 4,
        out_shape=[jax.ShapeDtypeStruct((rows, LANES), F32)] * 4,
        compiler_params=_params(("arbitrary",)),
    )(parts, w, m, v)


def _pack(shards):
    rows = []
    for name, n in _PACK:
        a = shards[name].reshape(-1)
        a = jnp.pad(a, (0, n * LANES - a.shape[0]))
        rows.append(a.reshape(n, LANES))
    used = sum(n for _, n in _PACK)
    rows.append(jnp.zeros((PACK_ROWS - used, LANES), rows[0].dtype))
    return jnp.concatenate(rows, axis=0)


def _unpack(buf, shapes):
    out, r0 = {}, 0
    for name, n in _PACK:
        size = math.prod(shapes[name])
        out[name] = buf[r0:r0 + n].reshape(-1)[:size].reshape(shapes[name])
        r0 += n
    return out


_SHARD_SHAPES = {"w_in": (1024, 928), "w_branch_a": (512, 128), "w_branch_b": (512, 128), "w_out": (128, 1024),
                 "w_cq": (128, 1024), "w_ckv": (1024, 256), "w_co": (128, 1024), "w_1": (1024, 512),
                 "w_2": (512, 1024), "b_gate": (2, 128)}
_COL_SHARDED = ("w_in", "w_branch_a", "w_branch_b", "w_ckv", "w_1", "b_gate")


def _full_from_gathered(gathered):
    out, r0 = {}, 0
    for name, n in _PACK:
        shp = _SHARD_SHAPES[name]
        size = math.prod(shp)
        if name == "b_gate":
            a = lax.bitcast_convert_type(gathered[:, r0, :2 * size].reshape(N_DEV, size, 2), F32)
            a = a.reshape((N_DEV,) + shp)
        else:
            a = gathered[:, r0:r0 + n].reshape(N_DEV, -1)[:, :size].reshape((N_DEV,) + shp)
        if name in _COL_SHARDED:
            a = jnp.swapaxes(a, 0, 1).reshape(shp[0], N_DEV * shp[1])
        else:
            a = a.reshape(N_DEV * shp[0], shp[1])
        out[name] = a
        r0 += n
    return out


def _shards_from_full(full):
    rows = []
    for name, n in _PACK:
        shp = _SHARD_SHAPES[name]
        a = full[name]
        if name in _COL_SHARDED:
            a = jnp.swapaxes(a.reshape(shp[0], N_DEV, shp[1]), 0, 1)
        a = a.reshape(N_DEV, -1).astype(BF16)
        a = jnp.pad(a, ((0, 0), (0, n * LANES - a.shape[1])))
        rows.append(a.reshape(N_DEV, n, LANES))
    used = sum(n for _, n in _PACK)
    rows.append(jnp.zeros((N_DEV, PACK_ROWS - used, LANES), BF16))
    return jnp.concatenate(rows, axis=1)


def _pair_lanes(a):
    lead = a.shape[:-1]
    return a.reshape(lead + (2, 2, HEAD_DIM // 2)).swapaxes(-3, -2).reshape(lead + (128,))


def _split_w_in(w_in):
    rows = w_in.shape[0]
    dil = w_in[:, :3 * DIL_WIDTH].reshape(rows, 3, 3, 4, 128)
    dil = jnp.concatenate([_pair_lanes(dil[:, :2]), dil[:, 2:]], axis=1)
    dil = dil.transpose(0, 2, 3, 1, 4).reshape(rows, 3 * DIL_WIDTH)
    o = 3 * DIL_WIDTH
    qb = w_in[:, o:o + SWA_Q_WIDTH].reshape(rows, 2, 4, HEAD_DIM).transpose(0, 2, 1, 3).reshape(rows, 4, 128)
    qb = _pair_lanes(qb).reshape(rows, SWA_Q_WIDTH)
    kb = _pair_lanes(w_in[:, o + SWA_Q_WIDTH:o + SWA_Q_WIDTH + SWA_KV_WIDTH])
    vb = w_in[:, o + SWA_Q_WIDTH + SWA_KV_WIDTH:P_WIDTH]
    return jnp.concatenate([dil, qb, kb, vb], axis=1), w_in[:, P_WIDTH:]


def _merge_w_in(dw_p, dw_g):
    rows = dw_p.shape[0]
    dil = dw_p[:, :3 * DIL_WIDTH].reshape(rows, 3, 4, 3, 128).transpose(0, 3, 1, 2, 4)
    dil = jnp.concatenate([_pair_lanes(dil[:, :2]), dil[:, 2:]], axis=1).reshape(rows, 3 * DIL_WIDTH)
    o = 3 * DIL_WIDTH
    qb = _pair_lanes(dw_p[:, o:o + SWA_Q_WIDTH].reshape(rows, 4, 128))
    qb = qb.reshape(rows, 4, 2, HEAD_DIM).transpose(0, 2, 1, 3).reshape(rows, SWA_Q_WIDTH)
    kb = _pair_lanes(dw_p[:, o + SWA_Q_WIDTH:o + SWA_Q_WIDTH + SWA_KV_WIDTH])
    vb = dw_p[:, o + SWA_Q_WIDTH + SWA_KV_WIDTH:]
    return jnp.concatenate([dil, qb, kb, vb, dw_g], axis=1)


def _swa_rows(w_b):
    return w_b.reshape(2, 4, HEAD_DIM, -1).transpose(1, 0, 2, 3).reshape(SWA_Q_WIDTH, -1)


def _swa_rows_inv(dw_b):
    return dw_b.reshape(4, 2, HEAD_DIM, -1).transpose(1, 0, 2, 3).reshape(SWA_Q_WIDTH, -1)


def _rope_tables(pos):
    half = HEAD_DIM // 2
    inv = ROPE_THETA ** (-jnp.arange(half, dtype=F32) / half)
    ang = pos.astype(F32)[:, None] * inv
    c, s = jnp.cos(ang), jnp.sin(ang)
    return jnp.concatenate([c, c, c, c], axis=1), jnp.concatenate([-s, -s, s, s], axis=1)


def _local_step(x, mem, pos, target, wts, g_mix, g_cross, g_mem, g_mlp, g_final, sink):
    t = x.shape[0]
    tm = min(512, t)
    tq = min(256, t // 16)
    w_p, w_g = _split_w_in(wts["w_in"])
    w_b = _swa_rows(wts["w_branch_b"])
    b_gate = wts["b_gate"].astype(F32).reshape(1, GATE_WIDTH)
    cos, sin = _rope_tables(pos)
    sink_row = jnp.pad(sink.reshape(2, 4).T.reshape(1, 8), ((0, 0), (0, 120)))
    tabs = [(cos[None], sin[None])]
    for _, d in DIL_GROUPS[1:]:
        c_d, s_d = _rope_tables(pos.reshape(t // d, d).T.reshape(-1))
        tabs.append((c_d.reshape(d, t // d, 128), s_d.reshape(d, t // d, 128)))
    tabs.append(tabs[0])

    h, h1, h2, p0, p1, p2, pb = _inproj(x, g_mix, w_p, cos, sin, tm)
    gts = _gates(h, w_g, b_gate, tm, 1024)
    ps = [p0[None], p1, p2, pb[None]]
    outs, lses = [], []
    for gi, pv in enumerate(ps):
        o, l = _attn_fwd(f"attn_fwd{gi}", pv, gi == 3, sink_row[0, :8], tq)
        outs.append(o)
        lses.append(l)
    o0, l0, ob, lb = outs[0][0], lses[0][0], outs[3][0], lses[3][0]
    oa, ya, yb, merged, x1, hc = _mix(o0, outs[1], outs[2], l0, lses[1], lses[2], ob, gts, x,
                                      wts["w_branch_a"], w_b, wts["w_out"], g_cross, tm)
    mn, kv = _memkv(mem, g_mem, wts["w_ckv"])
    q, o, x2, hm = _cross(hc, x1, kv, wts["w_cq"], wts["w_co"], g_mlp, tm)
    a, dx3, loss, dg_final = _mlp(hm, x2, wts["w_1"], wts["w_2"], g_final.reshape(1, D_MODEL), target, tm, 1024)

    grads = {}
    dz, dx2, dg_mlp = _mlp_bwd(dx3, a, wts["w_1"], wts["w_2"], x2, g_mlp, tm, 1024)
    grads["w_2"] = _wgrad("dw_2", a, dx3, 1024, 1024, tm, square=True)
    grads["w_1"] = _wgrad("dw_1", hm, dz, 1024, 1024, tm)
    dq, dx1, dkv, dg_cross = _cross_bwd(dx2, x1, q, kv, wts["w_cq"], wts["w_co"], g_cross, tm)
    grads["w_co"] = _wgrad("dw_co", o, dx2, 1024, 1024, tm)
    grads["w_cq"] = _wgrad("dw_cq", hc, dq, 1024, 1024, tm)
    grads["w_ckv"], dg_mem = _memkv_bwd(dkv, mn, mem, wts["w_ckv"], g_mem)
    (dgt, dh_part, dya, dyb, do0, do1, do2, c0, c1, c2, dob, cb, db_gate, dsink) = _mix_bwd(
        dx1, ya, yb, gts, oa, ob, l0, lses[1], lses[2], lb, sink_row,
        wts["w_out"], wts["w_branch_a"], w_b, w_g, min(256, tm))
    grads["w_out"] = _wgrad("dw_out", merged, dx1, 1024, 1024, tm)
    grads["w_branch_a"] = _wgrad("dw_a", oa, dya, 512, 1024, tm)
    grads["w_branch_b"] = _swa_rows_inv(_wgrad("dw_b", ob, dyb, 512, 1024, tm))
    dw_g = _wgrad("dw_g", h, dgt, 1024, 1024, tm)
    dps = []
    for gi, (pv, do_g, c_g) in enumerate(zip(ps, (do0[None], do1, do2, dob[None]), (c0[None], c1, c2, cb[None]))):
        dps.append(_attn_bwd(f"attn_bwd{gi}", pv, do_g, lses[gi], c_g, tabs[gi][0], tabs[gi][1], gi == 3, tq))
    dw_p = jnp.concatenate(
        [_wgrad(f"dw_p{gi}", hh.reshape(t, D_MODEL), dpg.reshape(t, -1), 1024, PBLK, tm)
         for gi, (hh, dpg) in enumerate(zip((h, h1, h2, h), dps))], axis=1)
    grads["w_in"] = _merge_w_in(dw_p, dw_g)
    grads["b_gate"] = db_gate.reshape(2, D_MODEL)
    grad_x, dg_mix = _dx(dps[0][0], dps[1], dps[2], dps[3][0], w_p, dh_part, dx1, x, g_mix, min(256, tm))
    dsink_heads = dsink[0, :8].reshape(4, 2).T.reshape(8)
    small = {"g_mix": dg_mix[0], "g_cross": dg_cross[0], "g_mem": dg_mem[0], "g_mlp": dg_mlp[0],
             "g_final": dg_final[0], "sink": dsink_heads}
    return loss[0, 0], grad_x, grads, small


def kernel(x, mem, positions, g_mix, w_in, b_gate, sink, w_branch_a, w_branch_b, w_out, g_cross, g_mem, w_cq, w_ckv, w_co, g_mlp, w_1, w_2, g_final, loss_target, m_g_mix, m_w_in, m_b_gate, m_sink, m_w_branch_a, m_w_branch_b, m_w_out, m_g_cross, m_g_mem, m_w_cq, m_w_ckv, m_w_co, m_g_mlp, m_w_1, m_w_2, m_g_final, v_g_mix, v_w_in, v_b_gate, v_sink, v_w_branch_a, v_w_branch_b, v_w_out, v_g_cross, v_g_mem, v_w_cq, v_w_ckv, v_w_co, v_g_mlp, v_w_1, v_w_2, v_g_final):
    local = dict(locals())
    big = [n for n, _ in _PACK]
    w_sh = {n: local[n][0] for n in big}
    m_sh = {n: local["m_" + n][0] for n in big}
    v_sh = {n: local["v_" + n][0] for n in big}

    w_pack = _pack(w_sh)
    bias_words = lax.bitcast_convert_type(w_sh["b_gate"].reshape(-1), BF16).reshape(1, -1)
    bias_rows = jnp.pad(bias_words, ((0, 15), (0, LANES - bias_words.shape[1])))
    w_pack16 = w_pack.astype(BF16)
    gathered = _all_gather(jnp.concatenate([w_pack16[:BIAS_ROW], bias_rows, w_pack16[BIAS_ROW + 16:]], axis=0))
    wts = _full_from_gathered(gathered)

    loss, grad_x, grads, small = _local_step(
        x[0], mem[0], positions[0], loss_target[0], wts, g_mix, g_cross, g_mem, g_mlp,
        g_final, sink[0])

    gp = _shards_from_full(grads)
    sp = jnp.stack([small[n] if n != "sink" else jnp.pad(small[n], (0, LANES - 8)) for n in SMALL]
                   + [jnp.zeros((LANES,), F32)] * 2)
    recv, srecv = _exchange(gp, sp)

    g_big, d_big, nm_big, nv_big = _adamw("adamw", recv, w_pack, _pack(m_sh), _pack(v_sh), 256)

    def small_pack(prefix):
        rows = []
        for n in SMALL:
            a = local[prefix + n].reshape(-1)
            rows.append(jnp.pad(a, (0, LANES - a.shape[0])))
        return jnp.stack(rows + [jnp.zeros((LANES,), F32)] * 2)

    g_sm, d_sm, nm_sm, nv_sm = _adamw("adamw_small", srecv, small_pack(""), small_pack("m_"), small_pack("v_"), 8)

    def collect(buf_big, buf_small):
        b = _unpack(buf_big, _SHARD_SHAPES)
        out = {n: b[n][None] for n in big}
        for i, n in enumerate(SMALL):
            shp = local[n].shape
            out[n] = buf_small[i, :math.prod(shp)].reshape(shp)
        return out

    names = ["g_mix", "w_in", "b_gate", "sink", "w_branch_a", "w_branch_b", "w_out", "g_cross", "g_mem", "w_cq",
             "w_ckv", "w_co", "g_mlp", "w_1", "w_2", "g_final"]
    res = [lax.psum(loss, ("x", "y", "c")), grad_x[None]]
    for bb, bs in ((g_big, g_sm), (d_big, d_sm), (nm_big, nm_sm), (nv_big, nv_sm)):
        c = collect(bb, bs)
        res += [c[n] for n in names]
    return tuple(res)
```

```python
import functools
import math

import jax
import jax.numpy as jnp
from jax import lax
from jax.experimental import pallas as pl
from jax.experimental.pallas import tpu as pltpu

F32 = jnp.float32
BF16 = jnp.bfloat16

D_MODEL = 1024
HEAD_DIM = 64
DIL_GROUPS = ((128, 1), (512, 4), (2048, 16))
ROPE_THETA = 10000.0
X_HEADS = 4
X_HEAD_DIM = D_MODEL // X_HEADS
D_FF = 4 * D_MODEL
EPS = 1e-6
DIL_WIDTH = 1536
SWA_Q_WIDTH = 512
SWA_KV_WIDTH = 128
P_WIDTH = 3 * DIL_WIDTH + SWA_Q_WIDTH + 2 * SWA_KV_WIDTH
GATE_WIDTH = 2 * D_MODEL
IN_WIDTH = P_WIDTH + GATE_WIDTH
BAND = 128
PBLK = 768
Q_SCALE = HEAD_DIM ** -0.5
X_SCALE = X_HEAD_DIM ** -0.5

ADAM_LR = 0.001
ADAM_B1 = 0.9
ADAM_B2 = 0.999
ADAM_EPS = 1e-08
ADAM_WD = 0.01
ADAM_STEP = 10

N_DEV = 8
LANES = 1024
VMEM_LIMIT = 52 * 1024 * 1024

NT = (((1,), (1,)), ((), ()))
TN = (((0,), (0,)), ((), ()))

_ROWS = {"w_in": 928, "w_branch_a": 64, "w_branch_b": 64, "w_out": 128, "w_cq": 128,
         "w_ckv": 256, "w_co": 128, "w_1": 512, "w_2": 512, "b_gate": 16}
GROUP_A = ("w_1", "w_2")
GROUP_B = ("w_branch_a", "w_branch_b", "w_out", "w_cq", "w_ckv", "w_co", "b_gate")
GROUP_C = ("w_in",)
ADAM_TILE = {GROUP_A: 256, GROUP_B: 112, GROUP_C: 464}
SMALL = ("g_mix", "g_cross", "g_mem", "g_mlp", "g_final", "sink")


def _params(sem=None):
    return pltpu.CompilerParams(dimension_semantics=sem, vmem_limit_bytes=VMEM_LIMIT)


def _dot(a, b):
    return jnp.dot(a, b, preferred_element_type=F32)


def _dot_nt(a, b):
    return lax.dot_general(a, b, NT, preferred_element_type=F32)


def _dot_tn(a, b):
    return lax.dot_general(a, b, TN, preferred_element_type=F32)


def _rms(xt):
    return lax.rsqrt(jnp.mean(xt * xt, axis=-1, keepdims=True) + EPS)


def _rms_bwd(dh, xt, r, g):
    xn = xt * r
    dxn = dh * g
    dx = r * (dxn - xn * jnp.mean(dxn * xn, axis=-1, keepdims=True))
    return dx, jnp.sum(dh * xn, axis=0, keepdims=True)


def _rope(x, c, s, swa, sign):
    kinds = "qqqqkv" if swa else "qkvqkv"
    cq, sq = c * Q_SCALE, s * (sign * Q_SCALE)
    sk = s * sign if sign != 1 else s
    out = []
    for ci, kind in enumerate(kinds):
        xc = x[:, ci * 128:(ci + 1) * 128]
        if kind == "v":
            out.append(xc)
        elif kind == "q":
            out.append(xc * cq + pltpu.roll(xc, 64, 1) * sq)
        else:
            out.append(xc * c + pltpu.roll(xc, 64, 1) * sk)
    return jnp.concatenate(out, axis=1)


def _lane_scratch(rows, w):
    return pltpu.VMEM((w // 128, rows, 128), F32)


def _deinterleave(val, scr_ref, dst_ref, dtype):
    d, n = dst_ref.shape[0], dst_ref.shape[1]
    nc = val.shape[1] // 128
    for c in range(nc):
        scr_ref[c] = val[:, c * 128:(c + 1) * 128]
    for r in range(d):
        rows = [scr_ref.at[c][pl.ds(r, n, stride=d), :] for c in range(nc)]
        dst_ref[r] = jnp.concatenate(rows, axis=1).astype(dtype)


def _res_spec(a, tm):
    d, w = a.shape[0], a.shape[2]
    return pl.BlockSpec((d, tm // d, w), lambda i: (0, i, 0))


def _interleave(src_ref, scr_ref):
    d, n = src_ref.shape[0], src_ref.shape[1]
    nc = src_ref.shape[2] // 128
    for r in range(d):
        v = src_ref[r].astype(F32)
        for c in range(nc):
            scr_ref.at[c][pl.ds(r, n, stride=d), :] = v[:, c * 128:(c + 1) * 128]
    return jnp.concatenate([scr_ref[c] for c in range(nc)], axis=1)


def _with_dep(body, n_in, dep):
    if dep is None:
        return body
    return lambda *refs: body(*refs[:n_in], *refs[n_in + 1:])


def _dep_spec(dep):
    return [] if dep is None else [pl.BlockSpec(memory_space=pl.ANY)]


def _dep_arg(dep):
    return [] if dep is None else [dep]


def _inproj(x, g, w_p, cos, sin, tm, dep=None):
    t = x.shape[0]
    nj = P_WIDTH // PBLK

    def body(x_ref, g_ref, w_ref, c_ref, s_ref, h_ref, h1_ref, h2_ref, p0_ref, p1_ref, p2_ref, pb_ref,
             hs_ref, hf_ref, pf_ref):
        j = pl.program_id(1)

        @pl.when(j == 0)
        def _():
            xt = x_ref[...]
            hf = xt * _rms(xt) * g_ref[...]
            hb = hf.astype(BF16)
            hs_ref[...] = hb
            h_ref[...] = hb
            _deinterleave(hf, hf_ref, h1_ref, BF16)
            _deinterleave(hf, hf_ref, h2_ref, BF16)

        acc = _dot(hs_ref[...], w_ref[...])

        def rope(swa):
            return _rope(acc, c_ref[...], s_ref[...], swa, 1)

        @pl.when(j < 2)
        def _():
            p0_ref[...] = rope(False).astype(BF16)

        @pl.when((j >= 2) & (j < 4))
        def _():
            _deinterleave(rope(False), pf_ref, p1_ref, BF16)

        @pl.when((j >= 4) & (j < 6))
        def _():
            _deinterleave(rope(False), pf_ref, p2_ref, BF16)

        @pl.when(j == nj - 1)
        def _():
            pb_ref[...] = rope(True).astype(BF16)

    d1, d2 = DIL_GROUPS[1][1], DIL_GROUPS[2][1]
    col = lambda lo: (lambda i, j: (0, i, jnp.clip(j - lo, 0, 1)))
    sds = jax.ShapeDtypeStruct
    return pl.pallas_call(
        _with_dep(body, 5, dep), name="inproj", grid=(t // tm, nj),
        in_specs=[pl.BlockSpec((tm, D_MODEL), lambda i, j: (i, 0)),
                  pl.BlockSpec((1, D_MODEL), lambda i, j: (0, 0)),
                  pl.BlockSpec((D_MODEL, PBLK), lambda i, j: (0, j)),
                  pl.BlockSpec((tm, 128), lambda i, j: (i, 0)),
                  pl.BlockSpec((tm, 128), lambda i, j: (i, 0))] + _dep_spec(dep),
        out_specs=[pl.BlockSpec((tm, D_MODEL), lambda i, j: (i, 0)),
                   pl.BlockSpec((d1, tm // d1, D_MODEL), lambda i, j: (0, i, 0)),
                   pl.BlockSpec((d2, tm // d2, D_MODEL), lambda i, j: (0, i, 0)),
                   pl.BlockSpec((tm, PBLK), lambda i, j: (i, jnp.minimum(j, 1))),
                   pl.BlockSpec((d1, tm // d1, PBLK), col(2)),
                   pl.BlockSpec((d2, tm // d2, PBLK), col(4)),
                   pl.BlockSpec((tm, PBLK), lambda i, j: (i, 0))],
        out_shape=[sds((t, D_MODEL), BF16), sds((d1, t // d1, D_MODEL), BF16), sds((d2, t // d2, D_MODEL), BF16),
                   sds((t, 2 * PBLK), BF16), sds((d1, t // d1, 2 * PBLK), BF16), sds((d2, t // d2, 2 * PBLK), BF16),
                   sds((t, PBLK), BF16)],
        scratch_shapes=[pltpu.VMEM((tm, D_MODEL), BF16), _lane_scratch(tm, D_MODEL), _lane_scratch(tm, PBLK)],
        compiler_params=_params(("arbitrary", "arbitrary")),
    )(x, g, w_p, cos, sin, *_dep_arg(dep))


def _gates(h, w_g, b, tm, tn):
    t = h.shape[0]

    def body(h_ref, w_ref, b_ref, o_ref):
        z = _dot(h_ref[...], w_ref[...]) + b_ref[...]
        o_ref[...] = jax.nn.sigmoid(z).astype(BF16)

    return pl.pallas_call(
        body, name="gates", grid=(t // tm, GATE_WIDTH // tn),
        in_specs=[pl.BlockSpec((tm, D_MODEL), lambda i, j: (i, 0)),
                  pl.BlockSpec((D_MODEL, tn), lambda i, j: (0, j)),
                  pl.BlockSpec((1, tn), lambda i, j: (0, j))],
        out_specs=pl.BlockSpec((tm, tn), lambda i, j: (i, j)),
        out_shape=jax.ShapeDtypeStruct((t, GATE_WIDTH), BF16),
        compiler_params=_params(("arbitrary", "arbitrary")),
    )(h, w_g, b)


def _band_mask(i, s):
    row = lax.broadcasted_iota(jnp.int32, (BAND, 2 * BAND), 0)
    col = lax.broadcasted_iota(jnp.int32, (BAND, 2 * BAND), 1)
    band = (col >= row) & (col <= row + BAND)
    if s == 0:
        band = band & ((col >= BAND) | (i > 0))
    return band


def _head_a_masks(rows):
    lane = lax.broadcasted_iota(jnp.int32, (rows, 128), 1)
    return (lane % HEAD_DIM) < HEAD_DIM // 2, lane < HEAD_DIM


def _one_head(x, head_a, hf):
    zero = jnp.zeros_like(x)
    return jnp.where(head_a, x, zero) if hf == 0 else jnp.where(head_a, zero, x)


def _kv_rows(cur_ref, tail_ref, s, off):
    if s == 0:
        return jnp.concatenate([tail_ref[:, off:off + 128], cur_ref[0:BAND, off:off + 128]], axis=0)
    return cur_ref[(s - 1) * BAND:(s + 1) * BAND, off:off + 128]


def _attn_layout(swa):
    if swa:
        return [(128 * j, 512, 640) for j in range(4)]
    return [(0, 128, 256), (384, 512, 640)]


def _attn_fwd(name, pv, swa, sinks, tq):
    d, ls = pv.shape[0], pv.shape[1]
    n, nsb = ls // tq, tq // BAND
    pairs = _attn_layout(swa)
    ncol = 1 if swa else 2
    ow = 128 * len(pairs)

    def body(cur_ref, tail_ref, *rest):
        sink_ref = rest[0] if swa else None
        o_ref, lse_ref = rest[-2:]
        i = pl.program_id(2)
        lane = lax.broadcasted_iota(jnp.int32, (BAND, 128), 1)
        head_a = _head_a_masks(BAND)
        for s in range(nsb):
            mask = _band_mask(i, s)
            rows = slice(s * BAND, (s + 1) * BAND)
            lse_tile = jnp.zeros((BAND, 128), F32)
            for j, (qo, ko, vo) in enumerate(pairs):
                q = cur_ref[rows, qo:qo + 128]
                kk = _kv_rows(cur_ref, tail_ref, s, ko)
                vv = _kv_rows(cur_ref, tail_ref, s, vo)
                halves = []
                for hf in (0, 1):
                    sc = _dot_nt(_one_head(q, head_a[0], hf), kk)
                    sc = jnp.where(mask, sc, -jnp.inf)
                    m = jnp.max(sc, axis=-1, keepdims=True)
                    if swa:
                        sk = sink_ref[2 * j + hf]
                        m = jnp.maximum(m, sk)
                    p = jnp.exp(sc - m)
                    den = jnp.sum(p, axis=-1, keepdims=True)
                    if swa:
                        den = den + jnp.exp(sk - m)
                    lse_tile = jnp.where(lane == 2 * j + hf, m + jnp.log(den), lse_tile)
                    halves.append(_dot((p * (1.0 / den)).astype(BF16), vv))
                o_ref[rows, j * 128:(j + 1) * 128] = jnp.where(head_a[1], halves[0], halves[1]).astype(BF16)
            lse_ref[rows, :] = lse_tile

    in_specs = [pl.BlockSpec((None, tq, PBLK), lambda r, cb, i: (r, i, cb)),
                pl.BlockSpec((None, BAND, PBLK), lambda r, cb, i: (r, jnp.maximum(i * nsb - 1, 0), cb))]
    args = [pv, pv]
    if swa:
        in_specs.append(pl.BlockSpec(memory_space=pltpu.SMEM))
        args.append(sinks)
    return pl.pallas_call(
        body, name=name, grid=(d, ncol, n),
        in_specs=in_specs,
        out_specs=[pl.BlockSpec((None, tq, ow), lambda r, cb, i: (r, i, cb)),
                   pl.BlockSpec((None, tq, 128), lambda r, cb, i: (r, i, cb))],
        out_shape=[jax.ShapeDtypeStruct((d, ls, 512), BF16), jax.ShapeDtypeStruct((d, ls, 128 * ncol), F32)],
        compiler_params=_params(("arbitrary", "arbitrary", "arbitrary")),
    )(*args)


def _lse_lane(h):
    return (h // 4) * 128 + h % 4


def _head_scale(x, tile, lanes):
    lane = lax.broadcasted_iota(jnp.int32, (x.shape[0], 128), 1)
    lo = lane < HEAD_DIM
    out = []
    for c in range(x.shape[1] // 128):
        a0 = tile[:, lanes[2 * c]:lanes[2 * c] + 1]
        a1 = tile[:, lanes[2 * c + 1]:lanes[2 * c + 1] + 1]
        out.append(x[:, c * 128:(c + 1) * 128] * jnp.where(lo, a0, a1))
    return jnp.concatenate(out, axis=1)


def _head_sums(x, lanes, width):
    lane = lax.broadcasted_iota(jnp.int32, (x.shape[0], width), 1)
    out = jnp.zeros((x.shape[0], width), F32)
    for h in range(x.shape[1] // HEAD_DIM):
        sm = jnp.sum(x[:, h * HEAD_DIM:(h + 1) * HEAD_DIM], axis=-1, keepdims=True)
        out = jnp.where(lane == lanes[h], sm, out)
    return out


def _alphas(l0, l1, l2):
    m = jnp.maximum(jnp.maximum(l0, l1), l2)
    e0, e1, e2 = jnp.exp(l0 - m), jnp.exp(l1 - m), jnp.exp(l2 - m)
    den = e0 + e1 + e2
    return e0 / den, e1 / den, e2 / den


DIL_LANES = [_lse_lane(h) for h in range(8)]
SWA_LANES = list(range(8))


def _mix(o0, o1, o2, l0, l1, l2, ob, gts, x, w_a, w_b, w_out, g_cross, tm):
    t = x.shape[0]

    def body(o0_ref, o1_ref, o2_ref, l0_ref, l1_ref, l2_ref, ob_ref, g_ref, x_ref, wa_ref, wb_ref, wo_ref,
             gc_ref, oa_ref, ya_ref, yb_ref, mg_ref, x1_ref, hc_ref, so_ref, sl_ref):
        a0, a1, a2 = _alphas(l0_ref[...], _interleave(l1_ref, sl_ref), _interleave(l2_ref, sl_ref))
        oa = (_head_scale(o0_ref[...].astype(F32), a0, DIL_LANES)
              + _head_scale(_interleave(o1_ref, so_ref), a1, DIL_LANES)
              + _head_scale(_interleave(o2_ref, so_ref), a2, DIL_LANES))
        oab = oa.astype(BF16)
        oa_ref[...] = oab
        ya = _dot(oab, wa_ref[...])
        yb = _dot(ob_ref[...], wb_ref[...])
        ya_ref[...] = ya.astype(BF16)
        yb_ref[...] = yb.astype(BF16)
        merged = (g_ref[:, :D_MODEL].astype(F32) * ya + g_ref[:, D_MODEL:].astype(F32) * yb).astype(BF16)
        mg_ref[...] = merged
        x1 = x_ref[...] + _dot(merged, wo_ref[...])
        x1_ref[...] = x1
        hc_ref[...] = (x1 * _rms(x1) * gc_ref[...]).astype(BF16)

    row = lambda w: pl.BlockSpec((tm, w), lambda i: (i, 0))
    full = lambda a, b: pl.BlockSpec((a, b), lambda i: (0, 0))
    return pl.pallas_call(
        body, name="mix", grid=(t // tm,),
        in_specs=[row(512), _res_spec(o1, tm), _res_spec(o2, tm), row(256), _res_spec(l1, tm), _res_spec(l2, tm),
                  row(512), row(GATE_WIDTH),
                  row(D_MODEL), full(512, D_MODEL), full(512, D_MODEL), full(D_MODEL, D_MODEL), full(1, D_MODEL)],
        out_specs=[row(512), row(D_MODEL), row(D_MODEL), row(D_MODEL), row(D_MODEL), row(D_MODEL)],
        out_shape=[jax.ShapeDtypeStruct((t, 512), BF16), jax.ShapeDtypeStruct((t, D_MODEL), BF16),
                   jax.ShapeDtypeStruct((t, D_MODEL), BF16), jax.ShapeDtypeStruct((t, D_MODEL), BF16),
                   jax.ShapeDtypeStruct((t, D_MODEL), F32), jax.ShapeDtypeStruct((t, D_MODEL), BF16)],
        scratch_shapes=[_lane_scratch(tm, 512), _lane_scratch(tm, 256)],
        compiler_params=_params(("arbitrary",)),
    )(o0, o1, o2, l0, l1, l2, ob, gts, x, w_a, w_b, w_out, g_cross)


def _memkv(mem, g_mem, w_ckv):
    m = mem.shape[0]

    def body(mem_ref, g_ref, w_ref, mn_ref, kv_ref):
        xt = mem_ref[...]
        mn = (xt * _rms(xt) * g_ref[...]).astype(BF16)
        mn_ref[...] = mn
        kv_ref[...] = _dot(mn, w_ref[...]).astype(BF16)

    return pl.pallas_call(
        body, name="memkv",
        out_shape=[jax.ShapeDtypeStruct((m, D_MODEL), BF16), jax.ShapeDtypeStruct((m, 2 * D_MODEL), BF16)],
        compiler_params=_params(),
    )(mem, g_mem, w_ckv)


def _cross_probs(q, kv_ref, h):
    k = kv_ref[:, h * X_HEAD_DIM:(h + 1) * X_HEAD_DIM]
    sc = _dot_nt(q[:, h * X_HEAD_DIM:(h + 1) * X_HEAD_DIM], k)
    m = jnp.max(sc, axis=-1, keepdims=True)
    p = jnp.exp(sc - m)
    return p / jnp.sum(p, axis=-1, keepdims=True)


def _cross(hc, x1, kv, w_cq, w_co, g_mlp, tm):
    t = x1.shape[0]
    m = kv.shape[0]

    def body(hc_ref, x1_ref, kv_ref, wq_ref, wo_ref, g_ref, q_ref, o_ref, x2_ref, hm_ref):
        q = (_dot(hc_ref[...], wq_ref[...]) * X_SCALE).astype(BF16)
        q_ref[...] = q
        outs = []
        for h in range(X_HEADS):
            p = _cross_probs(q, kv_ref, h)
            v = kv_ref[:, D_MODEL + h * X_HEAD_DIM:D_MODEL + (h + 1) * X_HEAD_DIM]
            outs.append(_dot(p.astype(BF16), v))
        o = jnp.concatenate(outs, axis=1).astype(BF16)
        o_ref[...] = o
        x2 = x1_ref[...] + _dot(o, wo_ref[...])
        x2_ref[...] = x2
        hm_ref[...] = (x2 * _rms(x2) * g_ref[...]).astype(BF16)

    row = lambda w: pl.BlockSpec((tm, w), lambda i: (i, 0))
    full = lambda a, b: pl.BlockSpec((a, b), lambda i: (0, 0))
    return pl.pallas_call(
        body, name="cross", grid=(t // tm,),
        in_specs=[row(D_MODEL), row(D_MODEL), full(m, 2 * D_MODEL), full(D_MODEL, D_MODEL),
                  full(D_MODEL, D_MODEL), full(1, D_MODEL)],
        out_specs=[row(D_MODEL)] * 4,
        out_shape=[jax.ShapeDtypeStruct((t, D_MODEL), BF16), jax.ShapeDtypeStruct((t, D_MODEL), BF16),
                   jax.ShapeDtypeStruct((t, D_MODEL), F32), jax.ShapeDtypeStruct((t, D_MODEL), BF16)],
        compiler_params=_params(("arbitrary",)),
    )(hc, x1, kv, w_cq, w_co, g_mlp)


def _mlp(hm, x2, w_1, w_2, g_final, target, tm, tf):
    t = x2.shape[0]
    nf = D_FF // tf

    def body(hm_ref, x2_ref, w1_ref, w2_ref, g_ref, tg_ref, a_ref, dx3_ref, loss_ref, dg_ref, acc_ref):
        i, f = pl.program_id(0), pl.program_id(1)
        a = jnp.maximum(_dot(hm_ref[...], w1_ref[...]), 0.0)
        a_ref[...] = a.astype(BF16)
        part = _dot((a * a).astype(BF16), w2_ref[...])

        @pl.when(f == 0)
        def _():
            acc_ref[...] = part

        @pl.when(f > 0)
        def _():
            acc_ref[...] += part

        @pl.when((i == 0) & (f == 0))
        def _():
            loss_ref[...] = jnp.zeros_like(loss_ref)
            dg_ref[...] = jnp.zeros_like(dg_ref)

        @pl.when(f == nf - 1)
        def _():
            x3 = x2_ref[...] + acc_ref[...]
            r = _rms(x3)
            g = g_ref[...]
            diff = x3 * r * g - tg_ref[...]
            loss_ref[...] += 0.5 * jnp.sum(jnp.mean(diff * diff, axis=-1, keepdims=True))
            dx3, dg = _rms_bwd(diff / D_MODEL, x3, r, g)
            dx3_ref[...] = dx3
            dg_ref[...] += dg

    return pl.pallas_call(
        body, name="mlp", grid=(t // tm, nf),
        in_specs=[pl.BlockSpec((tm, D_MODEL), lambda i, f: (i, 0)),
                  pl.BlockSpec((tm, D_MODEL), lambda i, f: (i, 0)),
                  pl.BlockSpec((D_MODEL, tf), lambda i, f: (0, f)),
                  pl.BlockSpec((tf, D_MODEL), lambda i, f: (f, 0)),
                  pl.BlockSpec((1, D_MODEL), lambda i, f: (0, 0)),
                  pl.BlockSpec((tm, D_MODEL), lambda i, f: (i, 0))],
        out_specs=[pl.BlockSpec((tm, tf), lambda i, f: (i, f)),
                   pl.BlockSpec((tm, D_MODEL), lambda i, f: (i, 0)),
                   pl.BlockSpec((1, 128), lambda i, f: (0, 0)),
                   pl.BlockSpec((1, D_MODEL), lambda i, f: (0, 0))],
        out_shape=[jax.ShapeDtypeStruct((t, D_FF), BF16), jax.ShapeDtypeStruct((t, D_MODEL), F32),
                   jax.ShapeDtypeStruct((1, 128), F32), jax.ShapeDtypeStruct((1, D_MODEL), F32)],
        scratch_shapes=[pltpu.VMEM((tm, D_MODEL), F32)],
        compiler_params=_params(("arbitrary", "arbitrary")),
    )(hm, x2, w_1, w_2, g_final, target)


def _mlp_bwd(dx3, a, w_1, w_2, x2, g_mlp, tm, tf):
    t = x2.shape[0]
    nf = D_FF // tf

    def body(dx3_ref, a_ref, w1_ref, w2_ref, x2_ref, g_ref, dz_ref, dx2_ref, dg_ref, acc_ref):
        i, f = pl.program_id(0), pl.program_id(1)
        da2 = _dot_nt(dx3_ref[...].astype(BF16), w2_ref[...])
        dz = (2.0 * a_ref[...].astype(F32) * da2).astype(BF16)
        dz_ref[...] = dz
        part = _dot_nt(dz, w1_ref[...])

        @pl.when(f == 0)
        def _():
            acc_ref[...] = part

        @pl.when(f > 0)
        def _():
            acc_ref[...] += part

        @pl.when((i == 0) & (f == 0))
        def _():
            dg_ref[...] = jnp.zeros_like(dg_ref)

        @pl.when(f == nf - 1)
        def _():
            xt = x2_ref[...]
            dx, dg = _rms_bwd(acc_ref[...], xt, _rms(xt), g_ref[...])
            dx2_ref[...] = dx3_ref[...] + dx
            dg_ref[...] += dg

    return pl.pallas_call(
        body, name="mlp_bwd", grid=(t // tm, nf),
        in_specs=[pl.BlockSpec((tm, D_MODEL), lambda i, f: (i, 0)),
                  pl.BlockSpec((tm, tf), lambda i, f: (i, f)),
                  pl.BlockSpec((D_MODEL, tf), lambda i, f: (0, f)),
                  pl.BlockSpec((tf, D_MODEL), lambda i, f: (f, 0)),
                  pl.BlockSpec((tm, D_MODEL), lambda i, f: (i, 0)),
                  pl.BlockSpec((1, D_MODEL), lambda i, f: (0, 0))],
        out_specs=[pl.BlockSpec((tm, tf), lambda i, f: (i, f)),
                   pl.BlockSpec((tm, D_MODEL), lambda i, f: (i, 0)),
                   pl.BlockSpec((1, D_MODEL), lambda i, f: (0, 0))],
        out_shape=[jax.ShapeDtypeStruct((t, D_FF), BF16), jax.ShapeDtypeStruct((t, D_MODEL), F32),
                   jax.ShapeDtypeStruct((1, D_MODEL), F32)],
        scratch_shapes=[pltpu.VMEM((tm, D_MODEL), F32)],
        compiler_params=_params(("arbitrary", "arbitrary")),
    )(dx3, a, w_1, w_2, x2, g_mlp)


def _wgrad(name, a, b, tka, tn, tm, square=False):
    t, ka = a.shape
    n = b.shape[1]

    def body(a_ref, b_ref, o_ref):
        at = a_ref[...]
        if square:
            af = at.astype(F32)
            at = af * af
        part = _dot_tn(at.astype(BF16), b_ref[...].astype(BF16))

        @pl.when(pl.program_id(2) == 0)
        def _():
            o_ref[...] = part

        @pl.when(pl.program_id(2) > 0)
        def _():
            o_ref[...] += part

    return pl.pallas_call(
        body, name=name, grid=(ka // tka, n // tn, t // tm),
        in_specs=[pl.BlockSpec((tm, tka), lambda p, q, k: (k, p)),
                  pl.BlockSpec((tm, tn), lambda p, q, k: (k, q))],
        out_specs=pl.BlockSpec((tka, tn), lambda p, q, k: (p, q)),
        out_shape=jax.ShapeDtypeStruct((ka, n), F32),
        compiler_params=_params(("arbitrary", "arbitrary", "arbitrary")),
    )(a, b)


def _cross_bwd(dx2, x1, q, kv, w_cq, w_co, g_cross, tm, dep=None):
    t = x1.shape[0]
    m = kv.shape[0]

    def body(dx2_ref, x1_ref, q_ref, kv_ref, wq_ref, wo_ref, g_ref, dq_ref, dx1_ref, dkv_ref, dg_ref):
        @pl.when(pl.program_id(0) == 0)
        def _():
            dkv_ref[...] = jnp.zeros_like(dkv_ref)
            dg_ref[...] = jnp.zeros_like(dg_ref)

        do = _dot_nt(dx2_ref[...].astype(BF16), wo_ref[...]).astype(BF16)
        q = q_ref[...]
        dqs = []
        for h in range(X_HEADS):
            hs = slice(h * X_HEAD_DIM, (h + 1) * X_HEAD_DIM)
            vs = slice(D_MODEL + h * X_HEAD_DIM, D_MODEL + (h + 1) * X_HEAD_DIM)
            p = _cross_probs(q, kv_ref, h)
            dp = _dot_nt(do[:, hs], kv_ref[:, vs])
            ds = (p * (dp - jnp.sum(dp * p, axis=-1, keepdims=True))).astype(BF16)
            dqs.append(_dot(ds, kv_ref[:, hs]))
            dkv_ref[:, hs] += _dot_tn(ds, q[:, hs])
            dkv_ref[:, vs] += _dot_tn(p.astype(BF16), do[:, hs])
        dq = (jnp.concatenate(dqs, axis=1) * X_SCALE).astype(BF16)
        dq_ref[...] = dq
        xt = x1_ref[...]
        dx, dg = _rms_bwd(_dot_nt(dq, wq_ref[...]), xt, _rms(xt), g_ref[...])
        dx1_ref[...] = dx2_ref[...] + dx
        dg_ref[...] += dg

    row = lambda w: pl.BlockSpec((tm, w), lambda i: (i, 0))
    full = lambda a, b: pl.BlockSpec((a, b), lambda i: (0, 0))
    return pl.pallas_call(
        _with_dep(body, 7, dep), name="cross_bwd", grid=(t // tm,),
        in_specs=[row(D_MODEL), row(D_MODEL), row(D_MODEL), full(m, 2 * D_MODEL), full(D_MODEL, D_MODEL),
                  full(D_MODEL, D_MODEL), full(1, D_MODEL)] + _dep_spec(dep),
        out_specs=[row(D_MODEL), row(D_MODEL), full(m, 2 * D_MODEL), full(1, D_MODEL)],
        out_shape=[jax.ShapeDtypeStruct((t, D_MODEL), BF16), jax.ShapeDtypeStruct((t, D_MODEL), F32),
                   jax.ShapeDtypeStruct((m, 2 * D_MODEL), F32), jax.ShapeDtypeStruct((1, D_MODEL), F32)],
        compiler_params=_params(("arbitrary",)),
    )(dx2, x1, q, kv, w_cq, w_co, g_cross, *_dep_arg(dep))


def _memkv_bwd(dkv, mn, mem, w_ckv, g_mem):
    def body(dkv_ref, mn_ref, mem_ref, w_ref, g_ref, dw_ref, dg_ref):
        dkvb = dkv_ref[...].astype(BF16)
        dw_ref[...] = _dot_tn(mn_ref[...], dkvb)
        dmn = _dot_nt(dkvb, w_ref[...])
        xt = mem_ref[...]
        dg_ref[...] = jnp.sum(dmn * xt * _rms(xt), axis=0, keepdims=True)

    return pl.pallas_call(
        body, name="memkv_bwd",
        out_shape=[jax.ShapeDtypeStruct((D_MODEL, 2 * D_MODEL), F32), jax.ShapeDtypeStruct((1, D_MODEL), F32)],
        compiler_params=_params(),
    )(dkv, mn, mem, w_ckv, g_mem)


def _mix_bwd(dx1, ya, yb, gts, oa, ob, l0, l1, l2, lb, sink_row, w_out, w_a, w_b, w_g, tm):
    t = dx1.shape[0]

    def body(dx1_ref, ya_ref, yb_ref, g_ref, oa_ref, ob_ref, l0_ref, l1_ref, l2_ref, lb_ref, sk_ref,
             wo_ref, wa_ref, wb_ref, wg_ref,
             dg_ref, dhp_ref, dya_ref, dyb_ref, do0_ref, do1_ref, do2_ref, c0_ref, c1_ref, c2_ref,
             dob_ref, cb_ref, db_ref, dsk_ref, so_ref, sl_ref):
        @pl.when(pl.program_id(0) == 0)
        def _():
            db_ref[...] = jnp.zeros_like(db_ref)
            dsk_ref[...] = jnp.zeros_like(dsk_ref)

        dm = _dot_nt(dx1_ref[...].astype(BF16), wo_ref[...])
        ga = g_ref[:, :D_MODEL].astype(F32)
        gb = g_ref[:, D_MODEL:].astype(F32)
        dya = (dm * ga).astype(BF16)
        dyb = (dm * gb).astype(BF16)
        dya_ref[...] = dya
        dyb_ref[...] = dyb
        dpa = dm * ya_ref[...].astype(F32) * ga * (1.0 - ga)
        dpb = dm * yb_ref[...].astype(F32) * gb * (1.0 - gb)
        dpre = jnp.concatenate([dpa, dpb], axis=1)
        db_ref[...] += jnp.sum(dpre, axis=0, keepdims=True)
        dpreb = dpre.astype(BF16)
        dg_ref[...] = dpreb
        dhp_ref[...] = _dot_nt(dpreb, wg_ref[...])

        doa = _dot_nt(dya, wa_ref[...])
        dob = _dot_nt(dyb, wb_ref[...])
        dsum = _head_sums(doa * oa_ref[...].astype(F32), DIL_LANES, 256)
        a0, a1, a2 = _alphas(l0_ref[...], _interleave(l1_ref, sl_ref), _interleave(l2_ref, sl_ref))
        c0_ref[...] = a0 * dsum
        do0_ref[...] = _head_scale(doa, a0, DIL_LANES).astype(BF16)
        for al, do_ref, c_ref in ((a1, do1_ref, c1_ref), (a2, do2_ref, c2_ref)):
            _deinterleave(al * dsum, sl_ref, c_ref, F32)
            _deinterleave(_head_scale(doa, al, DIL_LANES), so_ref, do_ref, BF16)
        dob_ref[...] = dob.astype(BF16)
        cb = _head_sums(dob * ob_ref[...].astype(F32), SWA_LANES, 128)
        cb_ref[...] = cb
        lane = lax.broadcasted_iota(jnp.int32, cb.shape, 1)
        psink = jnp.where(lane < 8, jnp.exp(sk_ref[...] - lb_ref[...]), 0.0)
        dsk_ref[...] += jnp.sum(-psink * cb, axis=0, keepdims=True)

    row = lambda w: pl.BlockSpec((tm, w), lambda i: (i, 0))
    full = lambda a, b: pl.BlockSpec((a, b), lambda i: (0, 0))
    sds = jax.ShapeDtypeStruct
    d1, d2 = l1.shape[0], l2.shape[0]
    res = lambda d, w: pl.BlockSpec((d, tm // d, w), lambda i: (0, i, 0))
    return pl.pallas_call(
        body, name="mix_bwd", grid=(t // tm,),
        in_specs=[row(D_MODEL), row(D_MODEL), row(D_MODEL), row(GATE_WIDTH), row(512), row(512),
                  row(256), _res_spec(l1, tm), _res_spec(l2, tm), row(128), full(1, 128),
                  full(D_MODEL, D_MODEL), full(512, D_MODEL), full(512, D_MODEL), full(D_MODEL, GATE_WIDTH)],
        out_specs=[row(GATE_WIDTH), row(D_MODEL), row(D_MODEL), row(D_MODEL),
                   row(512), res(d1, 512), res(d2, 512), row(256), res(d1, 256), res(d2, 256),
                   row(512), row(128), full(1, GATE_WIDTH), full(1, 128)],
        out_shape=[sds((t, GATE_WIDTH), BF16), sds((t, D_MODEL), F32), sds((t, D_MODEL), BF16),
                   sds((t, D_MODEL), BF16), sds((t, 512), BF16), sds((d1, t // d1, 512), BF16),
                   sds((d2, t // d2, 512), BF16), sds((t, 256), F32), sds((d1, t // d1, 256), F32),
                   sds((d2, t // d2, 256), F32), sds((t, 512), BF16),
                   sds((t, 128), F32), sds((1, GATE_WIDTH), F32), sds((1, 128), F32)],
        scratch_shapes=[_lane_scratch(tm, 512), _lane_scratch(tm, 256)],
        compiler_params=_params(("arbitrary",)),
    )(dx1, ya, yb, gts, oa, ob, l0, l1, l2, lb, sink_row, w_out, w_a, w_b, w_g)


def _attn_bwd(name, pv, dov, lsev, cv, cosv, sinv, swa, tq, dep=None):
    d, ls = pv.shape[0], pv.shape[1]
    n, nsb = ls // tq, tq // BAND
    pairs = _attn_layout(swa)
    ncol = 1 if swa else 2
    ow = 128 * len(pairs)

    def body(cur_ref, tail_ref, do_ref, lse_ref, c_ref, cos_ref, sin_ref, out_ref, acc_ref, carry_ref):
        i = pl.program_id(2)
        acc_ref[...] = jnp.zeros_like(acc_ref)

        @pl.when(i < n)
        def _():
            qk_a, v_a = _head_a_masks(BAND)
            qk_a2, v_a2 = _head_a_masks(2 * BAND)
            for s in range(nsb):
                mask = _band_mask(i, s)
                rows = slice(s * BAND, (s + 1) * BAND)
                krows = slice(s * BAND, (s + 2) * BAND)
                for j, (qo, ko, vo) in enumerate(pairs):
                    q = cur_ref[rows, qo:qo + 128]
                    kk = _kv_rows(cur_ref, tail_ref, s, ko)
                    vv = _kv_rows(cur_ref, tail_ref, s, vo)
                    do = do_ref[rows, j * 128:(j + 1) * 128]
                    dqh, dkh, dvh = [], [], []
                    for hf in (0, 1):
                        idx = 2 * j + hf
                        sc = _dot_nt(_one_head(q, qk_a, hf), kk)
                        p = jnp.exp(jnp.where(mask, sc, -jnp.inf) - lse_ref[rows, idx:idx + 1])
                        dp = _dot_nt(_one_head(do, v_a, hf), vv)
                        ds = (p * (dp - c_ref[rows, idx:idx + 1])).astype(BF16)
                        dqh.append(_dot(ds, kk))
                        dkh.append(_dot_tn(ds, q))
                        dvh.append(_dot_tn(p.astype(BF16), do))
                    acc_ref[BAND + s * BAND:BAND + (s + 1) * BAND, qo:qo + 128] += jnp.where(qk_a, dqh[0], dqh[1])
                    acc_ref[krows, ko:ko + 128] += jnp.where(qk_a2, dkh[0], dkh[1])
                    acc_ref[krows, vo:vo + 128] += jnp.where(v_a2, dvh[0], dvh[1])

        @pl.when(i >= 1)
        def _():
            if tq > BAND:
                fin = jnp.concatenate([carry_ref[0:tq - BAND, :], carry_ref[tq - BAND:, :] + acc_ref[0:BAND, :]], axis=0)
            else:
                fin = carry_ref[...] + acc_ref[0:BAND, :]
            out_ref[...] = _rope(fin, cos_ref[...], sin_ref[...], swa, -1).astype(BF16)

        carry_ref[...] = acc_ref[BAND:, :]

    qi = lambda i: jnp.minimum(i, n - 1)
    pi = lambda i: jnp.maximum(i - 1, 0)
    blk = lambda rows, w, row_of: pl.BlockSpec((None, rows, w), lambda r, cb, i: (r, row_of(i), cb))
    return pl.pallas_call(
        _with_dep(body, 7, dep), name=name, grid=(d, ncol, n + 1),
        in_specs=[blk(tq, PBLK, qi), blk(BAND, PBLK, lambda i: jnp.maximum(qi(i) * nsb - 1, 0)),
                  blk(tq, ow, qi), blk(tq, 128, qi), blk(tq, 128, qi),
                  pl.BlockSpec((None, tq, 128), lambda r, cb, i: (r, pi(i), 0)),
                  pl.BlockSpec((None, tq, 128), lambda r, cb, i: (r, pi(i), 0))] + _dep_spec(dep),
        out_specs=blk(tq, PBLK, pi),
        out_shape=jax.ShapeDtypeStruct((d, ls, ncol * PBLK), BF16),
        scratch_shapes=[pltpu.VMEM((tq + BAND, PBLK), F32), pltpu.VMEM((tq, PBLK), F32)],
        compiler_params=_params(("arbitrary", "arbitrary", "arbitrary")),
    )(pv, pv, dov, lsev, cv, cosv, sinv, *_dep_arg(dep))


def _dx(dp0, dp1, dp2, dpb, w_p, dh_part, dx1, x, g_mix, tm, dep=None):
    t = x.shape[0]
    gw = 2 * PBLK

    def body(dp0_ref, dp1_ref, dp2_ref, dpb_ref, w_ref, dhp_ref, dx1_ref, x_ref, g_ref, gx_ref, dg_ref,
             dpt_ref, scr_ref):
        @pl.when(pl.program_id(0) == 0)
        def _():
            dg_ref[...] = jnp.zeros_like(dg_ref)

        dpt_ref[:, 0:gw] = dp0_ref[...]
        dpt_ref[:, gw:2 * gw] = _interleave(dp1_ref, scr_ref).astype(BF16)
        dpt_ref[:, 2 * gw:3 * gw] = _interleave(dp2_ref, scr_ref).astype(BF16)
        dpt_ref[:, 3 * gw:] = dpb_ref[...]
        dh = _dot_nt(dpt_ref[...], w_ref[...]) + dhp_ref[...]
        xt = x_ref[...]
        dx, dg = _rms_bwd(dh, xt, _rms(xt), g_ref[...])
        gx_ref[...] = dx1_ref[...] + dx
        dg_ref[...] += dg

    row = lambda w: pl.BlockSpec((tm, w), lambda i: (i, 0))
    full = lambda a, b: pl.BlockSpec((a, b), lambda i: (0, 0))
    return pl.pallas_call(
        _with_dep(body, 9, dep), name="dx", grid=(t // tm,),
        in_specs=[row(gw), _res_spec(dp1, tm), _res_spec(dp2, tm), row(PBLK), full(D_MODEL, P_WIDTH),
                  row(D_MODEL), row(D_MODEL), row(D_MODEL), full(1, D_MODEL)] + _dep_spec(dep),
        out_specs=[row(D_MODEL), full(1, D_MODEL)],
        out_shape=[jax.ShapeDtypeStruct((t, D_MODEL), F32), jax.ShapeDtypeStruct((1, D_MODEL), F32)],
        scratch_shapes=[pltpu.VMEM((tm, P_WIDTH), BF16), _lane_scratch(tm, gw)],
        compiler_params=_params(("arbitrary",)),
    )(dp0, dp1, dp2, dpb, w_p, dh_part, dx1, x, g_mix, *_dep_arg(dep))


MESH = pl.DeviceIdType.MESH
HBM_SPEC = pl.BlockSpec(memory_space=pltpu.HBM)
VMEM_SPEC = pl.BlockSpec(memory_space=pltpu.VMEM)


def _all_gather(xp):
    r = xp.shape[0]

    def body(x_ref, out_ref, send_sems, recv_sems, local_sem):
        x, y, c = lax.axis_index("x"), lax.axis_index("y"), lax.axis_index("c")
        me, sibling = (x, y, c), (x, y, 1 - c)
        chips = [(1 - x, y), (x, 1 - y), (1 - x, 1 - y)]

        def rows(px, py, pc):
            return out_ref.at[4 * px + 2 * py + pc]

        def copy(k, block, to, src=None):
            return pltpu.make_async_remote_copy(
                src_ref=rows(*block) if src is None else src, dst_ref=rows(*block),
                send_sem=send_sems.at[k], recv_sem=recv_sems.at[k], device_id=to, device_id_type=MESH)

        mine = pltpu.make_async_copy(x_ref, rows(*me), local_sem)
        mine.start()
        first = [copy(0, me, sibling, src=x_ref)]
        first += [copy(1 + j, me, (*chip, c), src=x_ref) for j, chip in enumerate(chips)]
        for cp in first:
            cp.start()
        passed = [copy(4 + j, (*chip, c), sibling) for j, chip in enumerate(chips)]
        for j, chip in enumerate(chips):
            copy(1 + j, (*chip, c), me).wait_recv()
            passed[j].start()
        copy(0, sibling, me).wait_recv()
        for j, chip in enumerate(chips):
            copy(4 + j, (*chip, 1 - c), me).wait_recv()
        for cp in first + passed:
            cp.wait_send()
        mine.wait()

    return pl.pallas_call(
        body, name="all_gather",
        out_shape=jax.ShapeDtypeStruct((N_DEV, r, LANES), xp.dtype),
        in_specs=[HBM_SPEC], out_specs=HBM_SPEC,
        scratch_shapes=[pltpu.SemaphoreType.DMA((7,)), pltpu.SemaphoreType.DMA((7,)), pltpu.SemaphoreType.DMA],
    )(xp)


def _peers():
    x, y, c = lax.axis_index("x"), lax.axis_index("y"), lax.axis_index("c")
    out = []
    for k in range(1, N_DEV):
        px = 1 - x if k & 4 else x
        py = 1 - y if k & 2 else y
        pc = 1 - c if k & 1 else c
        out.append((k, (px, py, pc), 4 * px + 2 * py + pc))
    return out


def _my_index():
    return 4 * lax.axis_index("x") + 2 * lax.axis_index("y") + lax.axis_index("c")


def _exchange_small(sp):
    def body(s_ref, srecv_ref, send_sems, recv_sems):
        me_idx = _my_index()
        srecv_ref[pl.ds(me_idx, 1)] = s_ref[...][None]
        copies = [pltpu.make_async_remote_copy(
            src_ref=s_ref, dst_ref=srecv_ref.at[me_idx], send_sem=send_sems.at[k - 1], recv_sem=recv_sems.at[k - 1],
            device_id=peer, device_id_type=MESH) for k, peer, _ in _peers()]
        for cp in copies:
            cp.start()
        for cp in copies:
            cp.wait_recv()
        for cp in copies:
            cp.wait_send()

    return pl.pallas_call(
        body, name="small_exchange",
        out_shape=jax.ShapeDtypeStruct((N_DEV, 8, LANES), F32),
        in_specs=[VMEM_SPEC], out_specs=VMEM_SPEC,
        scratch_shapes=[pltpu.SemaphoreType.DMA((7,)), pltpu.SemaphoreType.DMA((7,))],
    )(sp)


SEM_SPEC = pl.BlockSpec(memory_space=pltpu.SEMAPHORE)
ANY_SPEC = pl.BlockSpec(memory_space=pl.ANY)
_SPLIT_PARAMS = pltpu.CompilerParams(has_side_effects=pltpu.SideEffectType.DATAFLOW_SIDE_EFFECTING)


def _split_copies(gather, src_ref, land_ref, send_sems, recv_sems):
    me_idx = _my_index()
    out = []
    for k, peer, peer_idx in _peers():
        if gather:
            src, dst = src_ref, land_ref.at[me_idx]
        else:
            src, dst = src_ref.at[peer_idx], land_ref.at[k - 1]
        out.append(pltpu.make_async_remote_copy(
            src_ref=src, dst_ref=dst, send_sem=send_sems.at[k - 1], recv_sem=recv_sems.at[k - 1],
            device_id=peer, device_id_type=MESH))
    return out


def _split_start(name, gather, src, land):
    def body(src_ref, land_ref, send_sems, recv_sems, src_thru, land_thru, token):
        for cp in _split_copies(gather, src_ref, land_ref, send_sems, recv_sems):
            cp.start()
        token[...] = jnp.zeros_like(token)

    hbm = lambda a: pltpu.HBM(a.shape, a.dtype)
    return pl.pallas_call(
        body, name=name,
        out_shape=(pltpu.SemaphoreType.DMA((7,)), pltpu.SemaphoreType.DMA((7,)), hbm(src), hbm(land),
                   jax.ShapeDtypeStruct((8, 128), F32)),
        in_specs=(HBM_SPEC, HBM_SPEC), out_specs=(SEM_SPEC, SEM_SPEC, HBM_SPEC, HBM_SPEC, VMEM_SPEC),
        input_output_aliases={0: 2, 1: 3}, compiler_params=_SPLIT_PARAMS,
    )(pltpu.with_memory_space_constraint(src, pltpu.HBM), pltpu.with_memory_space_constraint(land, pltpu.HBM))


def _split_wait(name, gather, started, after):
    send_sems, recv_sems, src, land, _ = started

    def body(src_ref, land_ref, send_sems, recv_sems, after_ref, src_out, land_out):
        for cp in _split_copies(gather, src_ref, land_ref, send_sems, recv_sems):
            cp.wait_send()
            cp.wait_recv()

    hbm = lambda a: pltpu.HBM(a.shape, a.dtype)
    return pl.pallas_call(
        body, name=name, out_shape=(hbm(src), hbm(land)),
        in_specs=(HBM_SPEC, HBM_SPEC, SEM_SPEC, SEM_SPEC, ANY_SPEC), out_specs=(HBM_SPEC, HBM_SPEC),
        input_output_aliases={0: 0, 1: 1}, compiler_params=_SPLIT_PARAMS,
    )(src, land, send_sems, recv_sems, after)


def _adamw(name, own, parts, w, m, v, tr):
    rows = w.shape[0]
    nparts = parts.shape[0]

    def body(*refs):
        p_ref, w_ref, m_ref, v_ref, g_ref, d_ref, nm_ref, nv_ref = refs[-8:]
        g = p_ref[0].astype(F32) if own is None else refs[0][...].astype(F32) + p_ref[0].astype(F32)
        for k in range(1, nparts):
            g = g + p_ref[k].astype(F32)
        g_ref[...] = g
        nm = ADAM_B1 * m_ref[...] + (1.0 - ADAM_B1) * g
        nv = ADAM_B2 * v_ref[...] + (1.0 - ADAM_B2) * (g * g)
        nm_ref[...] = nm
        nv_ref[...] = nv
        m_hat = nm / (1.0 - ADAM_B1 ** ADAM_STEP)
        v_hat = nv / (1.0 - ADAM_B2 ** ADAM_STEP)
        d_ref[...] = -ADAM_LR * (m_hat / (jnp.sqrt(v_hat) + ADAM_EPS) + ADAM_WD * w_ref[...])

    blk = pl.BlockSpec((tr, LANES), lambda i: (i, 0))
    in_specs = [pl.BlockSpec((nparts, tr, LANES), lambda i: (0, i, 0)), blk, blk, blk]
    args = [parts, w, m, v]
    if own is not None:
        in_specs, args = [blk] + in_specs, [own] + args
    return pl.pallas_call(
        body, name=name, grid=(rows // tr,),
        in_specs=in_specs,
        out_specs=[blk] * 4,
        out_shape=[jax.ShapeDtypeStruct((rows, LANES), F32)] * 4,
        compiler_params=_params(("arbitrary",)),
    )(*args)


def _pack(shards, names, dtype=None):
    rows = []
    for name in names:
        a = shards[name].reshape(-1)
        if dtype is not None:
            a = lax.bitcast_convert_type(a, dtype).reshape(-1) if name == "b_gate" else a.astype(dtype)
        n = _ROWS[name]
        rows.append(jnp.pad(a, (0, n * LANES - a.shape[0])).reshape(n, LANES))
    return jnp.concatenate(rows, axis=0)


def _unpack(buf, names):
    out, r0 = {}, 0
    for name in names:
        shp = _SHARD_SHAPES[name]
        out[name] = buf[r0:r0 + _ROWS[name]].reshape(-1)[:math.prod(shp)].reshape(shp)
        r0 += _ROWS[name]
    return out


_SHARD_SHAPES = {"w_in": (1024, 928), "w_branch_a": (512, 128), "w_branch_b": (512, 128), "w_out": (128, 1024),
                 "w_cq": (128, 1024), "w_ckv": (1024, 256), "w_co": (128, 1024), "w_1": (1024, 512),
                 "w_2": (512, 1024), "b_gate": (2, 128)}
_COL_SHARDED = ("w_in", "w_branch_a", "w_branch_b", "w_ckv", "w_1", "b_gate")


def _full_from_gathered(gathered, names):
    out, r0 = {}, 0
    for name in names:
        n = _ROWS[name]
        shp = _SHARD_SHAPES[name]
        size = math.prod(shp)
        if name == "b_gate":
            a = lax.bitcast_convert_type(gathered[:, r0, :2 * size].reshape(N_DEV, size, 2), F32)
            a = a.reshape((N_DEV,) + shp)
        else:
            a = gathered[:, r0:r0 + n].reshape(N_DEV, -1)[:, :size].reshape((N_DEV,) + shp)
        if name in _COL_SHARDED:
            a = jnp.swapaxes(a, 0, 1).reshape(shp[0], N_DEV * shp[1])
        else:
            a = a.reshape(N_DEV * shp[0], shp[1])
        out[name] = a
        r0 += n
    return out


def _shards_from_full(full, names):
    rows = []
    for name in names:
        n = _ROWS[name]
        shp = _SHARD_SHAPES[name]
        a = full[name]
        if name in _COL_SHARDED:
            a = jnp.swapaxes(a.reshape(shp[0], N_DEV, shp[1]), 0, 1)
        a = a.reshape(N_DEV, -1).astype(BF16)
        a = jnp.pad(a, ((0, 0), (0, n * LANES - a.shape[1])))
        rows.append(a.reshape(N_DEV, n, LANES))
    return jnp.concatenate(rows, axis=1)


def _pair_lanes(a):
    lead = a.shape[:-1]
    return a.reshape(lead + (2, 2, HEAD_DIM // 2)).swapaxes(-3, -2).reshape(lead + (128,))


def _split_w_in(w_in):
    rows = w_in.shape[0]
    dil = w_in[:, :3 * DIL_WIDTH].reshape(rows, 3, 3, 4, 128)
    dil = jnp.concatenate([_pair_lanes(dil[:, :2]), dil[:, 2:]], axis=1)
    dil = dil.transpose(0, 2, 3, 1, 4).reshape(rows, 3 * DIL_WIDTH)
    o = 3 * DIL_WIDTH
    qb = w_in[:, o:o + SWA_Q_WIDTH].reshape(rows, 2, 4, HEAD_DIM).transpose(0, 2, 1, 3).reshape(rows, 4, 128)
    qb = _pair_lanes(qb).reshape(rows, SWA_Q_WIDTH)
    kb = _pair_lanes(w_in[:, o + SWA_Q_WIDTH:o + SWA_Q_WIDTH + SWA_KV_WIDTH])
    vb = w_in[:, o + SWA_Q_WIDTH + SWA_KV_WIDTH:P_WIDTH]
    return jnp.concatenate([dil, qb, kb, vb], axis=1), w_in[:, P_WIDTH:]


def _merge_w_in(dw_p, dw_g):
    rows = dw_p.shape[0]
    dil = dw_p[:, :3 * DIL_WIDTH].reshape(rows, 3, 4, 3, 128).transpose(0, 3, 1, 2, 4)
    dil = jnp.concatenate([_pair_lanes(dil[:, :2]), dil[:, 2:]], axis=1).reshape(rows, 3 * DIL_WIDTH)
    o = 3 * DIL_WIDTH
    qb = _pair_lanes(dw_p[:, o:o + SWA_Q_WIDTH].reshape(rows, 4, 128))
    qb = qb.reshape(rows, 4, 2, HEAD_DIM).transpose(0, 2, 1, 3).reshape(rows, SWA_Q_WIDTH)
    kb = _pair_lanes(dw_p[:, o + SWA_Q_WIDTH:o + SWA_Q_WIDTH + SWA_KV_WIDTH])
    vb = dw_p[:, o + SWA_Q_WIDTH + SWA_KV_WIDTH:]
    return jnp.concatenate([dil, qb, kb, vb, dw_g], axis=1)


def _swa_rows(w_b):
    return w_b.reshape(2, 4, HEAD_DIM, -1).transpose(1, 0, 2, 3).reshape(SWA_Q_WIDTH, -1)


def _swa_rows_inv(dw_b):
    return dw_b.reshape(4, 2, HEAD_DIM, -1).transpose(1, 0, 2, 3).reshape(SWA_Q_WIDTH, -1)


def _rope_tables(pos):
    half = HEAD_DIM // 2
    inv = ROPE_THETA ** (-jnp.arange(half, dtype=F32) / half)
    ang = pos.astype(F32)[:, None] * inv
    c, s = jnp.cos(ang), jnp.sin(ang)
    return jnp.concatenate([c, c, c, c], axis=1), jnp.concatenate([-s, -s, s, s], axis=1)


def _local_step(x, mem, pos, target, w_in, dep, rest_weights, on_grads, g_mix, g_cross, g_mem, g_mlp, g_final, sink):
    t = x.shape[0]
    tm = min(512, t)
    tq = min(256, t // 16)
    w_p, w_g = _split_w_in(w_in)
    cos, sin = _rope_tables(pos)
    sink_row = jnp.pad(sink.reshape(2, 4).T.reshape(1, 8), ((0, 0), (0, 120)))
    tabs = [(cos[None], sin[None])]
    for _, d in DIL_GROUPS[1:]:
        c_d, s_d = _rope_tables(pos.reshape(t // d, d).T.reshape(-1))
        tabs.append((c_d.reshape(d, t // d, 128), s_d.reshape(d, t // d, 128)))
    tabs.append(tabs[0])

    h, h1, h2, p0, p1, p2, pb = _inproj(x, g_mix, w_p, cos, sin, tm, dep)
    ps = [p0[None], p1, p2, pb[None]]
    outs, lses = [], []
    for gi, pv in enumerate(ps):
        o, l = _attn_fwd(f"attn_fwd{gi}", pv, gi == 3, sink_row[0, :8], tq)
        outs.append(o)
        lses.append(l)
    o0, l0, ob, lb = outs[0][0], lses[0][0], outs[3][0], lses[3][0]
    wts = rest_weights(lb)
    w_b = _swa_rows(wts["w_branch_b"])
    gts = _gates(h, w_g, wts["b_gate"].astype(F32).reshape(1, GATE_WIDTH), tm, 1024)
    oa, ya, yb, merged, x1, hc = _mix(o0, outs[1], outs[2], l0, lses[1], lses[2], ob, gts, x,
                                      wts["w_branch_a"], w_b, wts["w_out"], g_cross, tm)
    mn, kv = _memkv(mem, g_mem, wts["w_ckv"])
    q, o, x2, hm = _cross(hc, x1, kv, wts["w_cq"], wts["w_co"], g_mlp, tm)
    a, dx3, loss, dg_final = _mlp(hm, x2, wts["w_1"], wts["w_2"], g_final.reshape(1, D_MODEL), target, tm, 1024)

    grads = {}
    dz, dx2, dg_mlp = _mlp_bwd(dx3, a, wts["w_1"], wts["w_2"], x2, g_mlp, tm, 1024)
    grads["w_2"] = _wgrad("dw_2", a, dx3, 1024, 1024, tm, square=True)
    grads["w_1"] = _wgrad("dw_1", hm, dz, 1024, 1024, tm)
    dep = on_grads(GROUP_A, grads)
    dq, dx1, dkv, dg_cross = _cross_bwd(dx2, x1, q, kv, wts["w_cq"], wts["w_co"], g_cross, tm, dep)
    grads["w_co"] = _wgrad("dw_co", o, dx2, 1024, 1024, tm)
    grads["w_cq"] = _wgrad("dw_cq", hc, dq, 1024, 1024, tm)
    grads["w_ckv"], dg_mem = _memkv_bwd(dkv, mn, mem, wts["w_ckv"], g_mem)
    (dgt, dh_part, dya, dyb, do0, do1, do2, c0, c1, c2, dob, cb, db_gate, dsink) = _mix_bwd(
        dx1, ya, yb, gts, oa, ob, l0, lses[1], lses[2], lb, sink_row,
        wts["w_out"], wts["w_branch_a"], w_b, w_g, min(256, tm))
    grads["w_out"] = _wgrad("dw_out", merged, dx1, 1024, 1024, tm)
    grads["w_branch_a"] = _wgrad("dw_a", oa, dya, 512, 1024, tm)
    grads["w_branch_b"] = _swa_rows_inv(_wgrad("dw_b", ob, dyb, 512, 1024, tm))
    grads["b_gate"] = db_gate.reshape(2, D_MODEL)
    dep = on_grads(GROUP_B, grads)
    dw_g = _wgrad("dw_g", h, dgt, 1024, 1024, tm)
    dps = []
    for gi, (pv, do_g, c_g) in enumerate(zip(ps, (do0[None], do1, do2, dob[None]), (c0[None], c1, c2, cb[None]))):
        dps.append(_attn_bwd(f"attn_bwd{gi}", pv, do_g, lses[gi], c_g, tabs[gi][0], tabs[gi][1], gi == 3, tq,
                             dep if gi == 0 else None))
    dw_p = jnp.concatenate(
        [_wgrad(f"dw_p{gi}", hh.reshape(t, D_MODEL), dpg.reshape(t, -1), 1024, PBLK, tm)
         for gi, (hh, dpg) in enumerate(zip((h, h1, h2, h), dps))], axis=1)
    grads["w_in"] = _merge_w_in(dw_p, dw_g)
    dep = on_grads(GROUP_C, grads)
    grad_x, dg_mix = _dx(dps[0][0], dps[1], dps[2], dps[3][0], w_p, dh_part, dx1, x, g_mix, min(256, tm), dep)
    dsink_heads = dsink[0, :8].reshape(4, 2).T.reshape(8)
    small = {"g_mix": dg_mix[0], "g_cross": dg_cross[0], "g_mem": dg_mem[0], "g_mlp": dg_mlp[0],
             "g_final": dg_final[0], "sink": dsink_heads}
    return loss[0, 0], grad_x, small


def kernel(x, mem, positions, g_mix, w_in, b_gate, sink, w_branch_a, w_branch_b, w_out, g_cross, g_mem, w_cq, w_ckv, w_co, g_mlp, w_1, w_2, g_final, loss_target, m_g_mix, m_w_in, m_b_gate, m_sink, m_w_branch_a, m_w_branch_b, m_w_out, m_g_cross, m_g_mem, m_w_cq, m_w_ckv, m_w_co, m_g_mlp, m_w_1, m_w_2, m_g_final, v_g_mix, v_w_in, v_b_gate, v_sink, v_w_branch_a, v_w_branch_b, v_w_out, v_g_cross, v_g_mem, v_w_cq, v_w_ckv, v_w_co, v_g_mlp, v_w_1, v_w_2, v_g_final):
    local = dict(locals())
    big = GROUP_A + GROUP_B + GROUP_C
    w_sh = {n: local[n][0] for n in big}
    m_sh = {n: local["m_" + n][0] for n in big}
    v_sh = {n: local["v_" + n][0] for n in big}
    me = _my_index()
    tags = {GROUP_A: "a", GROUP_B: "b", GROUP_C: "c"}

    w_in_full = _full_from_gathered(_all_gather(_pack(w_sh, GROUP_C, BF16)), GROUP_C)["w_in"]
    rest = GROUP_A + GROUP_B
    own_rest = _pack(w_sh, rest, BF16)
    land = lax.dynamic_update_slice(lax.empty((N_DEV,) + own_rest.shape, BF16), own_rest[None], (me, 0, 0))
    gather = _split_start("gather_start", True, own_rest, land)

    def rest_weights(after):
        return _full_from_gathered(_split_wait("gather_wait", True, gather, after)[1], rest)

    scatters = {}

    def on_grads(names, grads):
        gp = _shards_from_full(grads, names)
        scatters[names] = _split_start("scatter_start_" + tags[names], False, gp,
                                       lax.empty((N_DEV - 1,) + gp.shape[1:], BF16))
        return scatters[names][4]

    loss, grad_x, small = _local_step(
        x[0], mem[0], positions[0], loss_target[0], w_in_full, gather[4], rest_weights, on_grads,
        g_mix, g_cross, g_mem, g_mlp, g_final, sink[0])

    after, updated = grad_x, {}
    for names in (GROUP_A, GROUP_B, GROUP_C):
        src, got = _split_wait("scatter_wait_" + tags[names], False, scatters[names], after)
        own = lax.dynamic_index_in_dim(src, me, 0, keepdims=False)
        outs = _adamw("adamw_" + tags[names], own, got, _pack(w_sh, names), _pack(m_sh, names),
                      _pack(v_sh, names), ADAM_TILE[names])
        updated[names] = outs
        after = outs[3]

    sp = jnp.stack([small[n] if n != "sink" else jnp.pad(small[n], (0, LANES - 8)) for n in SMALL]
                   + [jnp.zeros((LANES,), F32)] * 2)
    srecv = _exchange_small(sp)

    def small_pack(prefix):
        rows = []
        for n in SMALL:
            a = local[prefix + n].reshape(-1)
            rows.append(jnp.pad(a, (0, LANES - a.shape[0])))
        return jnp.stack(rows + [jnp.zeros((LANES,), F32)] * 2)

    updated_small = _adamw("adamw_small", None, srecv, small_pack(""), small_pack("m_"), small_pack("v_"), 8)

    def collect(which):
        out = {}
        for names, outs in updated.items():
            out.update({n: a[None] for n, a in _unpack(outs[which], names).items()})
        for i, n in enumerate(SMALL):
            shp = local[n].shape
            out[n] = updated_small[which][i, :math.prod(shp)].reshape(shp)
        return out

    order = ["g_mix", "w_in", "b_gate", "sink", "w_branch_a", "w_branch_b", "w_out", "g_cross", "g_mem", "w_cq",
             "w_ckv", "w_co", "g_mlp", "w_1", "w_2", "g_final"]
    res = [lax.psum(loss, ("x", "y", "c")), grad_x[None]]
    for which in range(4):
        c = collect(which)
        res += [c[n] for n in order]
    return tuple(res)
```

```python
import functools
import math

import jax
import jax.numpy as jnp
from jax import lax
from jax.experimental import pallas as pl
from jax.experimental.pallas import tpu as pltpu

F32 = jnp.float32
BF16 = jnp.bfloat16

D_MODEL = 1024
HEAD_DIM = 64
DIL_GROUPS = ((128, 1), (512, 4), (2048, 16))
ROPE_THETA = 10000.0
X_HEADS = 4
X_HEAD_DIM = D_MODEL // X_HEADS
D_FF = 4 * D_MODEL
EPS = 1e-6
DIL_WIDTH = 1536
SWA_Q_WIDTH = 512
SWA_KV_WIDTH = 128
P_WIDTH = 3 * DIL_WIDTH + SWA_Q_WIDTH + 2 * SWA_KV_WIDTH
GATE_WIDTH = 2 * D_MODEL
IN_WIDTH = P_WIDTH + GATE_WIDTH
BAND = 128
PBLK = 768
Q_SCALE = HEAD_DIM ** -0.5
X_SCALE = X_HEAD_DIM ** -0.5

ADAM_LR = 0.001
ADAM_B1 = 0.9
ADAM_B2 = 0.999
ADAM_EPS = 1e-08
ADAM_WD = 0.01
ADAM_STEP = 10

N_DEV = 8
LANES = 1024
VMEM_LIMIT = 52 * 1024 * 1024

NT = (((1,), (1,)), ((), ()))
TN = (((0,), (0,)), ((), ()))

GROUP_A = ("w_1", "w_2")
GROUP_B = ("w_branch_a", "w_branch_b", "w_out", "w_cq", "w_ckv", "w_co", "b_gate")
GROUP_C = ("w_in",)
_COL_SHARDED = ("w_in", "w_branch_a", "w_branch_b", "w_ckv", "w_1", "b_gate")
ADAM_ROWS = {"w_in": 256, "w_branch_a": 512, "w_branch_b": 512, "w_out": 128, "w_cq": 128, "w_ckv": 512,
             "w_co": 128, "w_1": 256, "w_2": 256, "b_gate": 2}
SMALL = ("g_mix", "g_cross", "g_mem", "g_mlp", "g_final", "sink")


def _params(sem=None):
    return pltpu.CompilerParams(dimension_semantics=sem, vmem_limit_bytes=VMEM_LIMIT)


def _dot(a, b):
    return jnp.dot(a, b, preferred_element_type=F32)


def _dot_nt(a, b):
    return lax.dot_general(a, b, NT, preferred_element_type=F32)


def _dot_tn(a, b):
    return lax.dot_general(a, b, TN, preferred_element_type=F32)


def _rms(xt):
    return lax.rsqrt(jnp.mean(xt * xt, axis=-1, keepdims=True) + EPS)


def _rms_bwd(dh, xt, r, g):
    xn = xt * r
    dxn = dh * g
    dx = r * (dxn - xn * jnp.mean(dxn * xn, axis=-1, keepdims=True))
    return dx, jnp.sum(dh * xn, axis=0, keepdims=True)


def _rope(x, c, s, swa, sign):
    kinds = "qqqqkv" if swa else "qkvqkv"
    cq, sq = c * Q_SCALE, s * (sign * Q_SCALE)
    sk = s * sign if sign != 1 else s
    out = []
    for ci, kind in enumerate(kinds):
        xc = x[:, ci * 128:(ci + 1) * 128]
        if kind == "v":
            out.append(xc)
        elif kind == "q":
            out.append(xc * cq + pltpu.roll(xc, 64, 1) * sq)
        else:
            out.append(xc * c + pltpu.roll(xc, 64, 1) * sk)
    return jnp.concatenate(out, axis=1)


def _lane_scratch(rows, w):
    return pltpu.VMEM((w // 128, rows, 128), F32)


def _deinterleave(val, scr_ref, dst_ref, dtype):
    d, n = dst_ref.shape[0], dst_ref.shape[1]
    nc = val.shape[1] // 128
    for c in range(nc):
        scr_ref[c] = val[:, c * 128:(c + 1) * 128]
    for r in range(d):
        rows = [scr_ref.at[c][pl.ds(r, n, stride=d), :] for c in range(nc)]
        dst_ref[r] = jnp.concatenate(rows, axis=1).astype(dtype)


def _res_spec(a, tm):
    d, w = a.shape[0], a.shape[2]
    return pl.BlockSpec((d, tm // d, w), lambda i: (0, i, 0))


def _interleave(src_ref, scr_ref):
    d, n = src_ref.shape[0], src_ref.shape[1]
    nc = src_ref.shape[2] // 128
    for r in range(d):
        v = src_ref[r].astype(F32)
        for c in range(nc):
            scr_ref.at[c][pl.ds(r, n, stride=d), :] = v[:, c * 128:(c + 1) * 128]
    return jnp.concatenate([scr_ref[c] for c in range(nc)], axis=1)


def _with_dep(body, n_in, dep):
    if dep is None:
        return body
    return lambda *refs: body(*refs[:n_in], *refs[n_in + 1:])


def _dep_spec(dep):
    return [] if dep is None else [pl.BlockSpec(memory_space=pl.ANY)]


def _dep_arg(dep):
    return [] if dep is None else [dep]


def _inproj(x, g, w_p, cos, sin, tm, dep=None):
    t = x.shape[0]
    nj = P_WIDTH // PBLK

    def body(x_ref, g_ref, w_ref, c_ref, s_ref, h_ref, h1_ref, h2_ref, p0_ref, p1_ref, p2_ref, pb_ref,
             hs_ref, hf_ref, pf_ref):
        j = pl.program_id(1)

        @pl.when(j == 0)
        def _():
            xt = x_ref[...]
            hf = xt * _rms(xt) * g_ref[...]
            hb = hf.astype(BF16)
            hs_ref[...] = hb
            h_ref[...] = hb
            _deinterleave(hf, hf_ref, h1_ref, BF16)
            _deinterleave(hf, hf_ref, h2_ref, BF16)

        acc = _dot(hs_ref[...], w_ref[...])

        def rope(swa):
            return _rope(acc, c_ref[...], s_ref[...], swa, 1)

        @pl.when(j < 2)
        def _():
            p0_ref[...] = rope(False).astype(BF16)

        @pl.when((j >= 2) & (j < 4))
        def _():
            _deinterleave(rope(False), pf_ref, p1_ref, BF16)

        @pl.when((j >= 4) & (j < 6))
        def _():
            _deinterleave(rope(False), pf_ref, p2_ref, BF16)

        @pl.when(j == nj - 1)
        def _():
            pb_ref[...] = rope(True).astype(BF16)

    d1, d2 = DIL_GROUPS[1][1], DIL_GROUPS[2][1]
    col = lambda lo: (lambda i, j: (0, i, jnp.clip(j - lo, 0, 1)))
    sds = jax.ShapeDtypeStruct
    return pl.pallas_call(
        _with_dep(body, 5, dep), name="inproj", grid=(t // tm, nj),
        in_specs=[pl.BlockSpec((tm, D_MODEL), lambda i, j: (i, 0)),
                  pl.BlockSpec((1, D_MODEL), lambda i, j: (0, 0)),
                  pl.BlockSpec((D_MODEL, PBLK), lambda i, j: (0, j)),
                  pl.BlockSpec((tm, 128), lambda i, j: (i, 0)),
                  pl.BlockSpec((tm, 128), lambda i, j: (i, 0))] + _dep_spec(dep),
        out_specs=[pl.BlockSpec((tm, D_MODEL), lambda i, j: (i, 0)),
                   pl.BlockSpec((d1, tm // d1, D_MODEL), lambda i, j: (0, i, 0)),
                   pl.BlockSpec((d2, tm // d2, D_MODEL), lambda i, j: (0, i, 0)),
                   pl.BlockSpec((tm, PBLK), lambda i, j: (i, jnp.minimum(j, 1))),
                   pl.BlockSpec((d1, tm // d1, PBLK), col(2)),
                   pl.BlockSpec((d2, tm // d2, PBLK), col(4)),
                   pl.BlockSpec((tm, PBLK), lambda i, j: (i, 0))],
        out_shape=[sds((t, D_MODEL), BF16), sds((d1, t // d1, D_MODEL), BF16), sds((d2, t // d2, D_MODEL), BF16),
                   sds((t, 2 * PBLK), BF16), sds((d1, t // d1, 2 * PBLK), BF16), sds((d2, t // d2, 2 * PBLK), BF16),
                   sds((t, PBLK), BF16)],
        scratch_shapes=[pltpu.VMEM((tm, D_MODEL), BF16), _lane_scratch(tm, D_MODEL), _lane_scratch(tm, PBLK)],
        compiler_params=_params(("arbitrary", "arbitrary")),
    )(x, g, w_p, cos, sin, *_dep_arg(dep))


def _gates(h, w_g, b, tm, tn):
    t = h.shape[0]

    def body(h_ref, w_ref, b_ref, o_ref):
        z = _dot(h_ref[...], w_ref[...]) + b_ref[...]
        o_ref[...] = jax.nn.sigmoid(z).astype(BF16)

    return pl.pallas_call(
        body, name="gates", grid=(t // tm, GATE_WIDTH // tn),
        in_specs=[pl.BlockSpec((tm, D_MODEL), lambda i, j: (i, 0)),
                  pl.BlockSpec((D_MODEL, tn), lambda i, j: (0, j)),
                  pl.BlockSpec((1, tn), lambda i, j: (0, j))],
        out_specs=pl.BlockSpec((tm, tn), lambda i, j: (i, j)),
        out_shape=jax.ShapeDtypeStruct((t, GATE_WIDTH), BF16),
        compiler_params=_params(("arbitrary", "arbitrary")),
    )(h, w_g, b)


def _band_mask(i, s):
    row = lax.broadcasted_iota(jnp.int32, (BAND, 2 * BAND), 0)
    col = lax.broadcasted_iota(jnp.int32, (BAND, 2 * BAND), 1)
    band = (col >= row) & (col <= row + BAND)
    if s == 0:
        band = band & ((col >= BAND) | (i > 0))
    return band


def _head_a_masks(rows):
    lane = lax.broadcasted_iota(jnp.int32, (rows, 128), 1)
    return (lane % HEAD_DIM) < HEAD_DIM // 2, lane < HEAD_DIM


def _one_head(x, head_a, hf):
    zero = jnp.zeros_like(x)
    return jnp.where(head_a, x, zero) if hf == 0 else jnp.where(head_a, zero, x)


def _kv_rows(cur_ref, tail_ref, s, off):
    if s == 0:
        return jnp.concatenate([tail_ref[:, off:off + 128], cur_ref[0:BAND, off:off + 128]], axis=0)
    return cur_ref[(s - 1) * BAND:(s + 1) * BAND, off:off + 128]


def _attn_layout(swa):
    if swa:
        return [(128 * j, 512, 640) for j in range(4)]
    return [(0, 128, 256), (384, 512, 640)]


def _attn_fwd(name, pv, swa, sinks, tq):
    d, ls = pv.shape[0], pv.shape[1]
    n, nsb = ls // tq, tq // BAND
    pairs = _attn_layout(swa)
    ncol = 1 if swa else 2
    ow = 128 * len(pairs)

    def body(cur_ref, tail_ref, *rest):
        sink_ref = rest[0] if swa else None
        o_ref, lse_ref = rest[-2:]
        i = pl.program_id(2)
        lane = lax.broadcasted_iota(jnp.int32, (BAND, 128), 1)
        head_a = _head_a_masks(BAND)
        for s in range(nsb):
            mask = _band_mask(i, s)
            rows = slice(s * BAND, (s + 1) * BAND)
            lse_tile = jnp.zeros((BAND, 128), F32)
            for j, (qo, ko, vo) in enumerate(pairs):
                q = cur_ref[rows, qo:qo + 128]
                kk = _kv_rows(cur_ref, tail_ref, s, ko)
                vv = _kv_rows(cur_ref, tail_ref, s, vo)
                halves = []
                for hf in (0, 1):
                    sc = _dot_nt(_one_head(q, head_a[0], hf), kk)
                    sc = jnp.where(mask, sc, -jnp.inf)
                    m = jnp.max(sc, axis=-1, keepdims=True)
                    if swa:
                        sk = sink_ref[2 * j + hf]
                        m = jnp.maximum(m, sk)
                    p = jnp.exp(sc - m)
                    den = jnp.sum(p, axis=-1, keepdims=True)
                    if swa:
                        den = den + jnp.exp(sk - m)
                    lse_tile = jnp.where(lane == 2 * j + hf, m + jnp.log(den), lse_tile)
                    halves.append(_dot((p * (1.0 / den)).astype(BF16), vv))
                o_ref[rows, j * 128:(j + 1) * 128] = jnp.where(head_a[1], halves[0], halves[1]).astype(BF16)
            lse_ref[rows, :] = lse_tile

    in_specs = [pl.BlockSpec((None, tq, PBLK), lambda r, cb, i: (r, i, cb)),
                pl.BlockSpec((None, BAND, PBLK), lambda r, cb, i: (r, jnp.maximum(i * nsb - 1, 0), cb))]
    args = [pv, pv]
    if swa:
        in_specs.append(pl.BlockSpec(memory_space=pltpu.SMEM))
        args.append(sinks)
    return pl.pallas_call(
        body, name=name, grid=(d, ncol, n),
        in_specs=in_specs,
        out_specs=[pl.BlockSpec((None, tq, ow), lambda r, cb, i: (r, i, cb)),
                   pl.BlockSpec((None, tq, 128), lambda r, cb, i: (r, i, cb))],
        out_shape=[jax.ShapeDtypeStruct((d, ls, 512), BF16), jax.ShapeDtypeStruct((d, ls, 128 * ncol), F32)],
        compiler_params=_params(("arbitrary", "arbitrary", "arbitrary")),
    )(*args)


def _lse_lane(h):
    return (h // 4) * 128 + h % 4


def _head_scale(x, tile, lanes):
    lane = lax.broadcasted_iota(jnp.int32, (x.shape[0], 128), 1)
    lo = lane < HEAD_DIM
    out = []
    for c in range(x.shape[1] // 128):
        a0 = tile[:, lanes[2 * c]:lanes[2 * c] + 1]
        a1 = tile[:, lanes[2 * c + 1]:lanes[2 * c + 1] + 1]
        out.append(x[:, c * 128:(c + 1) * 128] * jnp.where(lo, a0, a1))
    return jnp.concatenate(out, axis=1)


def _head_sums(x, lanes, width):
    lane = lax.broadcasted_iota(jnp.int32, (x.shape[0], width), 1)
    out = jnp.zeros((x.shape[0], width), F32)
    for h in range(x.shape[1] // HEAD_DIM):
        sm = jnp.sum(x[:, h * HEAD_DIM:(h + 1) * HEAD_DIM], axis=-1, keepdims=True)
        out = jnp.where(lane == lanes[h], sm, out)
    return out


def _alphas(l0, l1, l2):
    m = jnp.maximum(jnp.maximum(l0, l1), l2)
    e0, e1, e2 = jnp.exp(l0 - m), jnp.exp(l1 - m), jnp.exp(l2 - m)
    den = e0 + e1 + e2
    return e0 / den, e1 / den, e2 / den


DIL_LANES = [_lse_lane(h) for h in range(8)]
SWA_LANES = list(range(8))


def _mix(o0, o1, o2, l0, l1, l2, ob, gts, x, w_a, w_b, w_out, g_cross, tm):
    t = x.shape[0]

    def body(o0_ref, o1_ref, o2_ref, l0_ref, l1_ref, l2_ref, ob_ref, g_ref, x_ref, wa_ref, wb_ref, wo_ref,
             gc_ref, oa_ref, ya_ref, yb_ref, mg_ref, x1_ref, hc_ref, so_ref, sl_ref):
        a0, a1, a2 = _alphas(l0_ref[...], _interleave(l1_ref, sl_ref), _interleave(l2_ref, sl_ref))
        oa = (_head_scale(o0_ref[...].astype(F32), a0, DIL_LANES)
              + _head_scale(_interleave(o1_ref, so_ref), a1, DIL_LANES)
              + _head_scale(_interleave(o2_ref, so_ref), a2, DIL_LANES))
        oab = oa.astype(BF16)
        oa_ref[...] = oab
        ya = _dot(oab, wa_ref[...])
        yb = _dot(ob_ref[...], wb_ref[...])
        ya_ref[...] = ya.astype(BF16)
        yb_ref[...] = yb.astype(BF16)
        merged = (g_ref[:, :D_MODEL].astype(F32) * ya + g_ref[:, D_MODEL:].astype(F32) * yb).astype(BF16)
        mg_ref[...] = merged
        x1 = x_ref[...] + _dot(merged, wo_ref[...])
        x1_ref[...] = x1
        hc_ref[...] = (x1 * _rms(x1) * gc_ref[...]).astype(BF16)

    row = lambda w: pl.BlockSpec((tm, w), lambda i: (i, 0))
    full = lambda a, b: pl.BlockSpec((a, b), lambda i: (0, 0))
    return pl.pallas_call(
        body, name="mix", grid=(t // tm,),
        in_specs=[row(512), _res_spec(o1, tm), _res_spec(o2, tm), row(256), _res_spec(l1, tm), _res_spec(l2, tm),
                  row(512), row(GATE_WIDTH),
                  row(D_MODEL), full(512, D_MODEL), full(512, D_MODEL), full(D_MODEL, D_MODEL), full(1, D_MODEL)],
        out_specs=[row(512), row(D_MODEL), row(D_MODEL), row(D_MODEL), row(D_MODEL), row(D_MODEL)],
        out_shape=[jax.ShapeDtypeStruct((t, 512), BF16), jax.ShapeDtypeStruct((t, D_MODEL), BF16),
                   jax.ShapeDtypeStruct((t, D_MODEL), BF16), jax.ShapeDtypeStruct((t, D_MODEL), BF16),
                   jax.ShapeDtypeStruct((t, D_MODEL), F32), jax.ShapeDtypeStruct((t, D_MODEL), BF16)],
        scratch_shapes=[_lane_scratch(tm, 512), _lane_scratch(tm, 256)],
        compiler_params=_params(("arbitrary",)),
    )(o0, o1, o2, l0, l1, l2, ob, gts, x, w_a, w_b, w_out, g_cross)


def _memkv(mem, g_mem, w_ckv):
    m = mem.shape[0]
    ws = w_ckv.shape[2]

    def body(mem_ref, g_ref, w_ref, mn_ref, kv_ref):
        xt = mem_ref[...]
        mn = (xt * _rms(xt) * g_ref[...]).astype(BF16)
        mn_ref[...] = mn
        for j in range(N_DEV):
            kv_ref[:, j * ws:(j + 1) * ws] = _dot(mn, w_ref[j]).astype(BF16)

    return pl.pallas_call(
        body, name="memkv",
        out_shape=[jax.ShapeDtypeStruct((m, D_MODEL), BF16), jax.ShapeDtypeStruct((m, 2 * D_MODEL), BF16)],
        compiler_params=_params(),
    )(mem, g_mem, w_ckv)


def _cross_probs(q, kv_ref, h):
    k = kv_ref[:, h * X_HEAD_DIM:(h + 1) * X_HEAD_DIM]
    sc = _dot_nt(q[:, h * X_HEAD_DIM:(h + 1) * X_HEAD_DIM], k)
    m = jnp.max(sc, axis=-1, keepdims=True)
    p = jnp.exp(sc - m)
    return p / jnp.sum(p, axis=-1, keepdims=True)


def _cross(hc, x1, kv, w_cq, w_co, g_mlp, tm):
    t = x1.shape[0]
    m = kv.shape[0]

    def body(hc_ref, x1_ref, kv_ref, wq_ref, wo_ref, g_ref, q_ref, o_ref, x2_ref, hm_ref):
        q = (_dot(hc_ref[...], wq_ref[...]) * X_SCALE).astype(BF16)
        q_ref[...] = q
        outs = []
        for h in range(X_HEADS):
            p = _cross_probs(q, kv_ref, h)
            v = kv_ref[:, D_MODEL + h * X_HEAD_DIM:D_MODEL + (h + 1) * X_HEAD_DIM]
            outs.append(_dot(p.astype(BF16), v))
        o = jnp.concatenate(outs, axis=1).astype(BF16)
        o_ref[...] = o
        x2 = x1_ref[...] + _dot(o, wo_ref[...])
        x2_ref[...] = x2
        hm_ref[...] = (x2 * _rms(x2) * g_ref[...]).astype(BF16)

    row = lambda w: pl.BlockSpec((tm, w), lambda i: (i, 0))
    full = lambda a, b: pl.BlockSpec((a, b), lambda i: (0, 0))
    return pl.pallas_call(
        body, name="cross", grid=(t // tm,),
        in_specs=[row(D_MODEL), row(D_MODEL), full(m, 2 * D_MODEL), full(D_MODEL, D_MODEL),
                  full(D_MODEL, D_MODEL), full(1, D_MODEL)],
        out_specs=[row(D_MODEL)] * 4,
        out_shape=[jax.ShapeDtypeStruct((t, D_MODEL), BF16), jax.ShapeDtypeStruct((t, D_MODEL), BF16),
                   jax.ShapeDtypeStruct((t, D_MODEL), F32), jax.ShapeDtypeStruct((t, D_MODEL), BF16)],
        compiler_params=_params(("arbitrary",)),
    )(hc, x1, kv, w_cq, w_co, g_mlp)


def _mlp(hm, x2, w_1, w_2, g_final, target, tm, tf):
    t = x2.shape[0]
    nf = D_FF // tf

    def body(hm_ref, x2_ref, w1_ref, w2_ref, g_ref, tg_ref, a_ref, dx3_ref, loss_ref, dg_ref, acc_ref):
        i, f = pl.program_id(0), pl.program_id(1)
        a = jnp.maximum(_dot(hm_ref[...], w1_ref[...]), 0.0)
        a_ref[...] = a.astype(BF16)
        part = _dot((a * a).astype(BF16), w2_ref[...])

        @pl.when(f == 0)
        def _():
            acc_ref[...] = part

        @pl.when(f > 0)
        def _():
            acc_ref[...] += part

        @pl.when((i == 0) & (f == 0))
        def _():
            loss_ref[...] = jnp.zeros_like(loss_ref)
            dg_ref[...] = jnp.zeros_like(dg_ref)

        @pl.when(f == nf - 1)
        def _():
            x3 = x2_ref[...] + acc_ref[...]
            r = _rms(x3)
            g = g_ref[...]
            diff = x3 * r * g - tg_ref[...]
            loss_ref[...] += 0.5 * jnp.sum(jnp.mean(diff * diff, axis=-1, keepdims=True))
            dx3, dg = _rms_bwd(diff / D_MODEL, x3, r, g)
            dx3_ref[...] = dx3
            dg_ref[...] += dg

    return pl.pallas_call(
        body, name="mlp", grid=(t // tm, nf),
        in_specs=[pl.BlockSpec((tm, D_MODEL), lambda i, f: (i, 0)),
                  pl.BlockSpec((tm, D_MODEL), lambda i, f: (i, 0)),
                  pl.BlockSpec((None, D_MODEL, tf), lambda i, f: (f, 0, 0)),
                  pl.BlockSpec((tf, D_MODEL), lambda i, f: (f, 0)),
                  pl.BlockSpec((1, D_MODEL), lambda i, f: (0, 0)),
                  pl.BlockSpec((tm, D_MODEL), lambda i, f: (i, 0))],
        out_specs=[pl.BlockSpec((tm, tf), lambda i, f: (i, f)),
                   pl.BlockSpec((tm, D_MODEL), lambda i, f: (i, 0)),
                   pl.BlockSpec((1, 128), lambda i, f: (0, 0)),
                   pl.BlockSpec((1, D_MODEL), lambda i, f: (0, 0))],
        out_shape=[jax.ShapeDtypeStruct((t, D_FF), BF16), jax.ShapeDtypeStruct((t, D_MODEL), F32),
                   jax.ShapeDtypeStruct((1, 128), F32), jax.ShapeDtypeStruct((1, D_MODEL), F32)],
        scratch_shapes=[pltpu.VMEM((tm, D_MODEL), F32)],
        compiler_params=_params(("arbitrary", "arbitrary")),
    )(hm, x2, w_1, w_2, g_final, target)


def _mlp_bwd(dx3, a, w_1, w_2, x2, g_mlp, tm, tf):
    t = x2.shape[0]
    nf = D_FF // tf

    def body(dx3_ref, a_ref, w1_ref, w2_ref, x2_ref, g_ref, dz_ref, dx2_ref, dg_ref, acc_ref):
        i, f = pl.program_id(0), pl.program_id(1)
        da2 = _dot_nt(dx3_ref[...].astype(BF16), w2_ref[...])
        dz = (2.0 * a_ref[...].astype(F32) * da2).astype(BF16)
        dz_ref[...] = dz
        part = _dot_nt(dz, w1_ref[...])

        @pl.when(f == 0)
        def _():
            acc_ref[...] = part

        @pl.when(f > 0)
        def _():
            acc_ref[...] += part

        @pl.when((i == 0) & (f == 0))
        def _():
            dg_ref[...] = jnp.zeros_like(dg_ref)

        @pl.when(f == nf - 1)
        def _():
            xt = x2_ref[...]
            dx, dg = _rms_bwd(acc_ref[...], xt, _rms(xt), g_ref[...])
            dx2_ref[...] = dx3_ref[...] + dx
            dg_ref[...] += dg

    return pl.pallas_call(
        body, name="mlp_bwd", grid=(t // tm, nf),
        in_specs=[pl.BlockSpec((tm, D_MODEL), lambda i, f: (i, 0)),
                  pl.BlockSpec((tm, tf), lambda i, f: (i, f)),
                  pl.BlockSpec((None, D_MODEL, tf), lambda i, f: (f, 0, 0)),
                  pl.BlockSpec((tf, D_MODEL), lambda i, f: (f, 0)),
                  pl.BlockSpec((tm, D_MODEL), lambda i, f: (i, 0)),
                  pl.BlockSpec((1, D_MODEL), lambda i, f: (0, 0))],
        out_specs=[pl.BlockSpec((tm, tf), lambda i, f: (i, f)),
                   pl.BlockSpec((tm, D_MODEL), lambda i, f: (i, 0)),
                   pl.BlockSpec((1, D_MODEL), lambda i, f: (0, 0))],
        out_shape=[jax.ShapeDtypeStruct((t, D_FF), BF16), jax.ShapeDtypeStruct((t, D_MODEL), F32),
                   jax.ShapeDtypeStruct((1, D_MODEL), F32)],
        scratch_shapes=[pltpu.VMEM((tm, D_MODEL), F32)],
        compiler_params=_params(("arbitrary", "arbitrary")),
    )(dx3, a, w_1, w_2, x2, g_mlp)


def _wgrad(name, a, b, tka, tn, tm, square=False, col_shards=False):
    t, ka = a.shape
    n = b.shape[1]
    nk = t // tm

    def body(a_ref, b_ref, o_ref, acc_ref):
        at = a_ref[...]
        if square:
            af = at.astype(F32)
            at = af * af
        part = _dot_tn(at.astype(BF16), b_ref[...].astype(BF16))
        k = pl.program_id(2)

        @pl.when(k == 0)
        def _():
            acc_ref[...] = part

        @pl.when(k > 0)
        def _():
            acc_ref[...] += part

        @pl.when(k == nk - 1)
        def _():
            o_ref[...] = acc_ref[...].astype(BF16)

    if col_shards:
        assert n == N_DEV * tn
        out_spec = pl.BlockSpec((None, tka, tn), lambda p, q, k: (q, p, 0))
        out_shape = jax.ShapeDtypeStruct((N_DEV, ka, tn), BF16)
    else:
        out_spec = pl.BlockSpec((tka, tn), lambda p, q, k: (p, q))
        out_shape = jax.ShapeDtypeStruct((ka, n), BF16)
    return pl.pallas_call(
        body, name=name, grid=(ka // tka, n // tn, nk),
        in_specs=[pl.BlockSpec((tm, tka), lambda p, q, k: (k, p)),
                  pl.BlockSpec((tm, tn), lambda p, q, k: (k, q))],
        out_specs=out_spec, out_shape=out_shape,
        scratch_shapes=[pltpu.VMEM((tka, tn), F32)],
        compiler_params=_params(("arbitrary", "arbitrary", "arbitrary")),
    )(a, b)


def _cross_bwd(dx2, x1, q, kv, w_cq, w_co, g_cross, tm, dep=None):
    t = x1.shape[0]
    m = kv.shape[0]

    def body(dx2_ref, x1_ref, q_ref, kv_ref, wq_ref, wo_ref, g_ref, dq_ref, dx1_ref, dkv_ref, dg_ref):
        @pl.when(pl.program_id(0) == 0)
        def _():
            dkv_ref[...] = jnp.zeros_like(dkv_ref)
            dg_ref[...] = jnp.zeros_like(dg_ref)

        do = _dot_nt(dx2_ref[...].astype(BF16), wo_ref[...]).astype(BF16)
        q = q_ref[...]
        dqs = []
        for h in range(X_HEADS):
            hs = slice(h * X_HEAD_DIM, (h + 1) * X_HEAD_DIM)
            vs = slice(D_MODEL + h * X_HEAD_DIM, D_MODEL + (h + 1) * X_HEAD_DIM)
            p = _cross_probs(q, kv_ref, h)
            dp = _dot_nt(do[:, hs], kv_ref[:, vs])
            ds = (p * (dp - jnp.sum(dp * p, axis=-1, keepdims=True))).astype(BF16)
            dqs.append(_dot(ds, kv_ref[:, hs]))
            dkv_ref[:, hs] += _dot_tn(ds, q[:, hs])
            dkv_ref[:, vs] += _dot_tn(p.astype(BF16), do[:, hs])
        dq = (jnp.concatenate(dqs, axis=1) * X_SCALE).astype(BF16)
        dq_ref[...] = dq
        xt = x1_ref[...]
        dx, dg = _rms_bwd(_dot_nt(dq, wq_ref[...]), xt, _rms(xt), g_ref[...])
        dx1_ref[...] = dx2_ref[...] + dx
        dg_ref[...] += dg

    row = lambda w: pl.BlockSpec((tm, w), lambda i: (i, 0))
    full = lambda a, b: pl.BlockSpec((a, b), lambda i: (0, 0))
    return pl.pallas_call(
        _with_dep(body, 7, dep), name="cross_bwd", grid=(t // tm,),
        in_specs=[row(D_MODEL), row(D_MODEL), row(D_MODEL), full(m, 2 * D_MODEL), full(D_MODEL, D_MODEL),
                  full(D_MODEL, D_MODEL), full(1, D_MODEL)] + _dep_spec(dep),
        out_specs=[row(D_MODEL), row(D_MODEL), full(m, 2 * D_MODEL), full(1, D_MODEL)],
        out_shape=[jax.ShapeDtypeStruct((t, D_MODEL), BF16), jax.ShapeDtypeStruct((t, D_MODEL), F32),
                   jax.ShapeDtypeStruct((m, 2 * D_MODEL), F32), jax.ShapeDtypeStruct((1, D_MODEL), F32)],
        compiler_params=_params(("arbitrary",)),
    )(dx2, x1, q, kv, w_cq, w_co, g_cross, *_dep_arg(dep))


def _memkv_bwd(dkv, mn, mem, w_ckv, g_mem):
    ws = w_ckv.shape[2]

    def body(dkv_ref, mn_ref, mem_ref, w_ref, g_ref, dw_ref, dg_ref):
        mn = mn_ref[...]
        dmn = jnp.zeros(mn.shape, F32)
        for j in range(N_DEV):
            dkvb = dkv_ref[:, j * ws:(j + 1) * ws].astype(BF16)
            dw_ref[j] = _dot_tn(mn, dkvb).astype(BF16)
            dmn = dmn + _dot_nt(dkvb, w_ref[j])
        xt = mem_ref[...]
        dg_ref[...] = jnp.sum(dmn * xt * _rms(xt), axis=0, keepdims=True)

    return pl.pallas_call(
        body, name="memkv_bwd",
        out_shape=[jax.ShapeDtypeStruct(w_ckv.shape, BF16), jax.ShapeDtypeStruct((1, D_MODEL), F32)],
        compiler_params=_params(),
    )(dkv, mn, mem, w_ckv, g_mem)


def _mix_bwd(dx1, ya, yb, gts, oa, ob, l0, l1, l2, lb, sink_row, w_out, w_a, w_b, w_g, tm):
    t = dx1.shape[0]

    def body(dx1_ref, ya_ref, yb_ref, g_ref, oa_ref, ob_ref, l0_ref, l1_ref, l2_ref, lb_ref, sk_ref,
             wo_ref, wa_ref, wb_ref, wg_ref,
             dg_ref, dhp_ref, dya_ref, dyb_ref, do0_ref, do1_ref, do2_ref, c0_ref, c1_ref, c2_ref,
             dob_ref, cb_ref, db_ref, dsk_ref, so_ref, sl_ref):
        @pl.when(pl.program_id(0) == 0)
        def _():
            db_ref[...] = jnp.zeros_like(db_ref)
            dsk_ref[...] = jnp.zeros_like(dsk_ref)

        dm = _dot_nt(dx1_ref[...].astype(BF16), wo_ref[...])
        ga = g_ref[:, :D_MODEL].astype(F32)
        gb = g_ref[:, D_MODEL:].astype(F32)
        dya = (dm * ga).astype(BF16)
        dyb = (dm * gb).astype(BF16)
        dya_ref[...] = dya
        dyb_ref[...] = dyb
        dpa = dm * ya_ref[...].astype(F32) * ga * (1.0 - ga)
        dpb = dm * yb_ref[...].astype(F32) * gb * (1.0 - gb)
        dpre = jnp.concatenate([dpa, dpb], axis=1)
        db_ref[...] += jnp.sum(dpre, axis=0, keepdims=True)
        dpreb = dpre.astype(BF16)
        dg_ref[...] = dpreb
        dhp_ref[...] = _dot_nt(dpreb, wg_ref[...])

        doa = _dot_nt(dya, wa_ref[...])
        dob = _dot_nt(dyb, wb_ref[...])
        dsum = _head_sums(doa * oa_ref[...].astype(F32), DIL_LANES, 256)
        a0, a1, a2 = _alphas(l0_ref[...], _interleave(l1_ref, sl_ref), _interleave(l2_ref, sl_ref))
        c0_ref[...] = a0 * dsum
        do0_ref[...] = _head_scale(doa, a0, DIL_LANES).astype(BF16)
        for al, do_ref, c_ref in ((a1, do1_ref, c1_ref), (a2, do2_ref, c2_ref)):
            _deinterleave(al * dsum, sl_ref, c_ref, F32)
            _deinterleave(_head_scale(doa, al, DIL_LANES), so_ref, do_ref, BF16)
        dob_ref[...] = dob.astype(BF16)
        cb = _head_sums(dob * ob_ref[...].astype(F32), SWA_LANES, 128)
        cb_ref[...] = cb
        lane = lax.broadcasted_iota(jnp.int32, cb.shape, 1)
        psink = jnp.where(lane < 8, jnp.exp(sk_ref[...] - lb_ref[...]), 0.0)
        dsk_ref[...] += jnp.sum(-psink * cb, axis=0, keepdims=True)

    row = lambda w: pl.BlockSpec((tm, w), lambda i: (i, 0))
    full = lambda a, b: pl.BlockSpec((a, b), lambda i: (0, 0))
    sds = jax.ShapeDtypeStruct
    d1, d2 = l1.shape[0], l2.shape[0]
    res = lambda d, w: pl.BlockSpec((d, tm // d, w), lambda i: (0, i, 0))
    return pl.pallas_call(
        body, name="mix_bwd", grid=(t // tm,),
        in_specs=[row(D_MODEL), row(D_MODEL), row(D_MODEL), row(GATE_WIDTH), row(512), row(512),
                  row(256), _res_spec(l1, tm), _res_spec(l2, tm), row(128), full(1, 128),
                  full(D_MODEL, D_MODEL), full(512, D_MODEL), full(512, D_MODEL), full(D_MODEL, GATE_WIDTH)],
        out_specs=[row(GATE_WIDTH), row(D_MODEL), row(D_MODEL), row(D_MODEL),
                   row(512), res(d1, 512), res(d2, 512), row(256), res(d1, 256), res(d2, 256),
                   row(512), row(128), full(1, GATE_WIDTH), full(1, 128)],
        out_shape=[sds((t, GATE_WIDTH), BF16), sds((t, D_MODEL), F32), sds((t, D_MODEL), BF16),
                   sds((t, D_MODEL), BF16), sds((t, 512), BF16), sds((d1, t // d1, 512), BF16),
                   sds((d2, t // d2, 512), BF16), sds((t, 256), F32), sds((d1, t // d1, 256), F32),
                   sds((d2, t // d2, 256), F32), sds((t, 512), BF16),
                   sds((t, 128), F32), sds((1, GATE_WIDTH), F32), sds((1, 128), F32)],
        scratch_shapes=[_lane_scratch(tm, 512), _lane_scratch(tm, 256)],
        compiler_params=_params(("arbitrary",)),
    )(dx1, ya, yb, gts, oa, ob, l0, l1, l2, lb, sink_row, w_out, w_a, w_b, w_g)


def _attn_bwd(name, pv, dov, lsev, cv, cosv, sinv, swa, tq, dep=None):
    d, ls = pv.shape[0], pv.shape[1]
    n, nsb = ls // tq, tq // BAND
    pairs = _attn_layout(swa)
    ncol = 1 if swa else 2
    ow = 128 * len(pairs)

    def body(cur_ref, tail_ref, do_ref, lse_ref, c_ref, cos_ref, sin_ref, out_ref, acc_ref, carry_ref):
        i = pl.program_id(2)
        acc_ref[...] = jnp.zeros_like(acc_ref)

        @pl.when(i < n)
        def _():
            qk_a, v_a = _head_a_masks(BAND)
            qk_a2, v_a2 = _head_a_masks(2 * BAND)
            for s in range(nsb):
                mask = _band_mask(i, s)
                rows = slice(s * BAND, (s + 1) * BAND)
                krows = slice(s * BAND, (s + 2) * BAND)
                for j, (qo, ko, vo) in enumerate(pairs):
                    q = cur_ref[rows, qo:qo + 128]
                    kk = _kv_rows(cur_ref, tail_ref, s, ko)
                    vv = _kv_rows(cur_ref, tail_ref, s, vo)
                    do = do_ref[rows, j * 128:(j + 1) * 128]
                    dqh, dkh, dvh = [], [], []
                    for hf in (0, 1):
                        idx = 2 * j + hf
                        sc = _dot_nt(_one_head(q, qk_a, hf), kk)
                        p = jnp.exp(jnp.where(mask, sc, -jnp.inf) - lse_ref[rows, idx:idx + 1])
                        dp = _dot_nt(_one_head(do, v_a, hf), vv)
                        ds = (p * (dp - c_ref[rows, idx:idx + 1])).astype(BF16)
                        dqh.append(_dot(ds, kk))
                        dkh.append(_dot_tn(ds, q))
                        dvh.append(_dot_tn(p.astype(BF16), do))
                    acc_ref[BAND + s * BAND:BAND + (s + 1) * BAND, qo:qo + 128] += jnp.where(qk_a, dqh[0], dqh[1])
                    acc_ref[krows, ko:ko + 128] += jnp.where(qk_a2, dkh[0], dkh[1])
                    acc_ref[krows, vo:vo + 128] += jnp.where(v_a2, dvh[0], dvh[1])

        @pl.when(i >= 1)
        def _():
            if tq > BAND:
                fin = jnp.concatenate([carry_ref[0:tq - BAND, :], carry_ref[tq - BAND:, :] + acc_ref[0:BAND, :]], axis=0)
            else:
                fin = carry_ref[...] + acc_ref[0:BAND, :]
            out_ref[...] = _rope(fin, cos_ref[...], sin_ref[...], swa, -1).astype(BF16)

        carry_ref[...] = acc_ref[BAND:, :]

    qi = lambda i: jnp.minimum(i, n - 1)
    pi = lambda i: jnp.maximum(i - 1, 0)
    blk = lambda rows, w, row_of: pl.BlockSpec((None, rows, w), lambda r, cb, i: (r, row_of(i), cb))
    return pl.pallas_call(
        _with_dep(body, 7, dep), name=name, grid=(d, ncol, n + 1),
        in_specs=[blk(tq, PBLK, qi), blk(BAND, PBLK, lambda i: jnp.maximum(qi(i) * nsb - 1, 0)),
                  blk(tq, ow, qi), blk(tq, 128, qi), blk(tq, 128, qi),
                  pl.BlockSpec((None, tq, 128), lambda r, cb, i: (r, pi(i), 0)),
                  pl.BlockSpec((None, tq, 128), lambda r, cb, i: (r, pi(i), 0))] + _dep_spec(dep),
        out_specs=blk(tq, PBLK, pi),
        out_shape=jax.ShapeDtypeStruct((d, ls, ncol * PBLK), BF16),
        scratch_shapes=[pltpu.VMEM((tq + BAND, PBLK), F32), pltpu.VMEM((tq, PBLK), F32)],
        compiler_params=_params(("arbitrary", "arbitrary", "arbitrary")),
    )(pv, pv, dov, lsev, cv, cosv, sinv, *_dep_arg(dep))


def _dx(dp0, dp1, dp2, dpb, w_p, dh_part, dx1, x, g_mix, tm, dep=None):
    t = x.shape[0]
    gw = 2 * PBLK

    def body(dp0_ref, dp1_ref, dp2_ref, dpb_ref, w_ref, dhp_ref, dx1_ref, x_ref, g_ref, gx_ref, dg_ref,
             dpt_ref, scr_ref):
        @pl.when(pl.program_id(0) == 0)
        def _():
            dg_ref[...] = jnp.zeros_like(dg_ref)

        dpt_ref[:, 0:gw] = dp0_ref[...]
        dpt_ref[:, gw:2 * gw] = _interleave(dp1_ref, scr_ref).astype(BF16)
        dpt_ref[:, 2 * gw:3 * gw] = _interleave(dp2_ref, scr_ref).astype(BF16)
        dpt_ref[:, 3 * gw:] = dpb_ref[...]
        dh = _dot_nt(dpt_ref[...], w_ref[...]) + dhp_ref[...]
        xt = x_ref[...]
        dx, dg = _rms_bwd(dh, xt, _rms(xt), g_ref[...])
        gx_ref[...] = dx1_ref[...] + dx
        dg_ref[...] += dg

    row = lambda w: pl.BlockSpec((tm, w), lambda i: (i, 0))
    full = lambda a, b: pl.BlockSpec((a, b), lambda i: (0, 0))
    return pl.pallas_call(
        _with_dep(body, 9, dep), name="dx", grid=(t // tm,),
        in_specs=[row(gw), _res_spec(dp1, tm), _res_spec(dp2, tm), row(PBLK), full(D_MODEL, P_WIDTH),
                  row(D_MODEL), row(D_MODEL), row(D_MODEL), full(1, D_MODEL)] + _dep_spec(dep),
        out_specs=[row(D_MODEL), full(1, D_MODEL)],
        out_shape=[jax.ShapeDtypeStruct((t, D_MODEL), F32), jax.ShapeDtypeStruct((1, D_MODEL), F32)],
        scratch_shapes=[pltpu.VMEM((tm, P_WIDTH), BF16), _lane_scratch(tm, gw)],
        compiler_params=_params(("arbitrary",)),
    )(dp0, dp1, dp2, dpb, w_p, dh_part, dx1, x, g_mix, *_dep_arg(dep))


MESH = pl.DeviceIdType.MESH
HBM_SPEC = pl.BlockSpec(memory_space=pltpu.HBM)
VMEM_SPEC = pl.BlockSpec(memory_space=pltpu.VMEM)


def _all_gather(xp):
    def body(x_ref, out_ref, send_sems, recv_sems, local_sem):
        x, y, c = lax.axis_index("x"), lax.axis_index("y"), lax.axis_index("c")
        me, sibling = (x, y, c), (x, y, 1 - c)
        chips = [(1 - x, y), (x, 1 - y), (1 - x, 1 - y)]

        def rows(px, py, pc):
            return out_ref.at[4 * px + 2 * py + pc]

        def copy(k, block, to, src=None):
            return pltpu.make_async_remote_copy(
                src_ref=rows(*block) if src is None else src, dst_ref=rows(*block),
                send_sem=send_sems.at[k], recv_sem=recv_sems.at[k], device_id=to, device_id_type=MESH)

        mine = pltpu.make_async_copy(x_ref, rows(*me), local_sem)
        mine.start()
        first = [copy(0, me, sibling, src=x_ref)]
        first += [copy(1 + j, me, (*chip, c), src=x_ref) for j, chip in enumerate(chips)]
        for cp in first:
            cp.start()
        passed = [copy(4 + j, (*chip, c), sibling) for j, chip in enumerate(chips)]
        for j, chip in enumerate(chips):
            copy(1 + j, (*chip, c), me).wait_recv()
            passed[j].start()
        copy(0, sibling, me).wait_recv()
        for j, chip in enumerate(chips):
            copy(4 + j, (*chip, 1 - c), me).wait_recv()
        for cp in first + passed:
            cp.wait_send()
        mine.wait()

    return pl.pallas_call(
        body, name="all_gather",
        out_shape=jax.ShapeDtypeStruct((N_DEV,) + xp.shape, xp.dtype),
        in_specs=[HBM_SPEC], out_specs=HBM_SPEC,
        scratch_shapes=[pltpu.SemaphoreType.DMA((7,)), pltpu.SemaphoreType.DMA((7,)), pltpu.SemaphoreType.DMA],
    )(xp)


def _peers():
    x, y, c = lax.axis_index("x"), lax.axis_index("y"), lax.axis_index("c")
    out = []
    for k in range(1, N_DEV):
        px = 1 - x if k & 4 else x
        py = 1 - y if k & 2 else y
        pc = 1 - c if k & 1 else c
        out.append((k, (px, py, pc), 4 * px + 2 * py + pc))
    return out


def _my_index():
    return 4 * lax.axis_index("x") + 2 * lax.axis_index("y") + lax.axis_index("c")


def _exchange_small(sp):
    def body(s_ref, srecv_ref, send_sems, recv_sems):
        me_idx = _my_index()
        srecv_ref[pl.ds(me_idx, 1)] = s_ref[...][None]
        copies = [pltpu.make_async_remote_copy(
            src_ref=s_ref, dst_ref=srecv_ref.at[me_idx], send_sem=send_sems.at[k - 1], recv_sem=recv_sems.at[k - 1],
            device_id=peer, device_id_type=MESH) for k, peer, _ in _peers()]
        for cp in copies:
            cp.start()
        for cp in copies:
            cp.wait_recv()
        for cp in copies:
            cp.wait_send()

    return pl.pallas_call(
        body, name="small_exchange",
        out_shape=jax.ShapeDtypeStruct((N_DEV, 8, LANES), F32),
        in_specs=[VMEM_SPEC], out_specs=VMEM_SPEC,
        scratch_shapes=[pltpu.SemaphoreType.DMA((7,)), pltpu.SemaphoreType.DMA((7,))],
    )(sp)


SEM_SPEC = pl.BlockSpec(memory_space=pltpu.SEMAPHORE)
ANY_SPEC = pl.BlockSpec(memory_space=pl.ANY)
_SPLIT_PARAMS = pltpu.CompilerParams(has_side_effects=pltpu.SideEffectType.DATAFLOW_SIDE_EFFECTING)


def _split_copies(gather, buf_refs, send_sems, recv_sems):
    me_idx = _my_index()
    n = len(buf_refs) if gather else len(buf_refs) // 2
    out = []
    for a in range(n):
        for k, peer, peer_idx in _peers():
            if gather:
                src, dst = buf_refs[a].at[me_idx], buf_refs[a].at[me_idx]
            else:
                src, dst = buf_refs[a].at[peer_idx], buf_refs[n + a].at[k - 1]
            out.append(pltpu.make_async_remote_copy(
                src_ref=src, dst_ref=dst, send_sem=send_sems.at[7 * a + k - 1], recv_sem=recv_sems.at[7 * a + k - 1],
                device_id=peer, device_id_type=MESH))
    return out


def _split_start(name, gather, bufs):
    nb = len(bufs)
    nsem = 7 * (nb if gather else nb // 2)

    def body(*refs):
        send_sems, recv_sems = refs[nb], refs[nb + 1]
        for cp in _split_copies(gather, refs[:nb], send_sems, recv_sems):
            cp.start()
        token = refs[-1]
        token[...] = jnp.zeros_like(token)

    hbm = [pltpu.HBM(a.shape, a.dtype) for a in bufs]
    return pl.pallas_call(
        body, name=name,
        out_shape=(pltpu.SemaphoreType.DMA((nsem,)), pltpu.SemaphoreType.DMA((nsem,)), *hbm,
                   jax.ShapeDtypeStruct((8, 128), F32)),
        in_specs=(HBM_SPEC,) * nb, out_specs=(SEM_SPEC, SEM_SPEC) + (HBM_SPEC,) * nb + (VMEM_SPEC,),
        input_output_aliases={i: 2 + i for i in range(nb)}, compiler_params=_SPLIT_PARAMS,
    )(*[pltpu.with_memory_space_constraint(a, pltpu.HBM) for a in bufs])


def _split_wait(name, gather, started, after):
    send_sems, recv_sems, bufs = started[0], started[1], started[2:-1]
    nb = len(bufs)

    def body(*refs):
        for cp in _split_copies(gather, refs[:nb], refs[nb], refs[nb + 1]):
            cp.wait_send()
            cp.wait_recv()

    return pl.pallas_call(
        body, name=name, out_shape=tuple(pltpu.HBM(a.shape, a.dtype) for a in bufs),
        in_specs=(HBM_SPEC,) * nb + (SEM_SPEC, SEM_SPEC, ANY_SPEC), out_specs=(HBM_SPEC,) * nb,
        input_output_aliases={i: i for i in range(nb)}, compiler_params=_SPLIT_PARAMS,
    )(*bufs, send_sems, recv_sems, after)


def _adam_update(g, w, m, v):
    nm = ADAM_B1 * m + (1.0 - ADAM_B1) * g
    nv = ADAM_B2 * v + (1.0 - ADAM_B2) * (g * g)
    m_hat = nm / (1.0 - ADAM_B1 ** ADAM_STEP)
    v_hat = nv / (1.0 - ADAM_B2 ** ADAM_STEP)
    return -ADAM_LR * (m_hat / (jnp.sqrt(v_hat) + ADAM_EPS) + ADAM_WD * w), nm, nv


def _adamw(name, me, sent, got, w, m, v, tr):
    r, c = w.shape

    def body(me_ref, own_ref, got_ref, w_ref, m_ref, v_ref, g_ref, d_ref, nm_ref, nv_ref):
        g = own_ref[...].astype(F32)
        for k in range(N_DEV - 1):
            g = g + got_ref[k].astype(F32)
        g_ref[...] = g
        d_ref[...], nm_ref[...], nv_ref[...] = _adam_update(g, w_ref[...], m_ref[...], v_ref[...])

    blk = pl.BlockSpec((tr, c), lambda i, me_ref: (i, 0))
    return pl.pallas_call(
        body, name=name,
        grid_spec=pltpu.PrefetchScalarGridSpec(
            num_scalar_prefetch=1, grid=(r // tr,),
            in_specs=[pl.BlockSpec((None, tr, c), lambda i, me_ref: (me_ref[0], i, 0)),
                      pl.BlockSpec((N_DEV - 1, tr, c), lambda i, me_ref: (0, i, 0)), blk, blk, blk],
            out_specs=[blk] * 4),
        out_shape=[jax.ShapeDtypeStruct((r, c), F32)] * 4,
        compiler_params=_params(("arbitrary",)),
    )(me, sent, got, w, m, v)


def _adamw_small(srecv, ws, ms, vs):
    nv_ = len(ws)

    def body(*refs):
        s_ref = refs[0]
        ins, outs = refs[1:1 + 3 * nv_], refs[1 + 3 * nv_:]
        g_all = s_ref[0]
        for k in range(1, N_DEV):
            g_all = g_all + s_ref[k]
        for i in range(nv_):
            n = ins[i].shape[1]
            g = g_all[i:i + 1, :n]
            d, nm, nv = _adam_update(g, ins[i][...], ins[nv_ + i][...], ins[2 * nv_ + i][...])
            outs[i][...], outs[nv_ + i][...], outs[2 * nv_ + i][...], outs[3 * nv_ + i][...] = g, d, nm, nv

    shapes = [jax.ShapeDtypeStruct(a.shape, F32) for a in ws]
    res = pl.pallas_call(body, name="adamw_small", out_shape=shapes * 4, compiler_params=_params())(
        srecv, *ws, *ms, *vs)
    return [res[k * nv_:(k + 1) * nv_] for k in range(4)]


def _cols_from_shards(a):
    return jnp.swapaxes(a, 0, 1).reshape(a.shape[1], N_DEV * a.shape[2])


def _shards_from_cols(a):
    return jnp.swapaxes(a.reshape(a.shape[0], N_DEV, a.shape[1] // N_DEV), 0, 1)


def _shards_from_rows(a):
    return a.reshape(N_DEV, a.shape[0] // N_DEV, a.shape[1])


def _pair_lanes(a):
    lead = a.shape[:-1]
    return a.reshape(lead + (2, 2, HEAD_DIM // 2)).swapaxes(-3, -2).reshape(lead + (128,))


def _split_w_in(w_in):
    rows = w_in.shape[0]
    dil = w_in[:, :3 * DIL_WIDTH].reshape(rows, 3, 3, 4, 128)
    dil = jnp.concatenate([_pair_lanes(dil[:, :2]), dil[:, 2:]], axis=1)
    dil = dil.transpose(0, 2, 3, 1, 4).reshape(rows, 3 * DIL_WIDTH)
    o = 3 * DIL_WIDTH
    qb = w_in[:, o:o + SWA_Q_WIDTH].reshape(rows, 2, 4, HEAD_DIM).transpose(0, 2, 1, 3).reshape(rows, 4, 128)
    qb = _pair_lanes(qb).reshape(rows, SWA_Q_WIDTH)
    kb = _pair_lanes(w_in[:, o + SWA_Q_WIDTH:o + SWA_Q_WIDTH + SWA_KV_WIDTH])
    vb = w_in[:, o + SWA_Q_WIDTH + SWA_KV_WIDTH:P_WIDTH]
    return jnp.concatenate([dil, qb, kb, vb], axis=1), w_in[:, P_WIDTH:]


def _merge_w_in(dw_p, dw_g):
    rows = dw_p.shape[0]
    dil = dw_p[:, :3 * DIL_WIDTH].reshape(rows, 3, 4, 3, 128).transpose(0, 3, 1, 2, 4)
    dil = jnp.concatenate([_pair_lanes(dil[:, :2]), dil[:, 2:]], axis=1).reshape(rows, 3 * DIL_WIDTH)
    o = 3 * DIL_WIDTH
    qb = _pair_lanes(dw_p[:, o:o + SWA_Q_WIDTH].reshape(rows, 4, 128))
    qb = qb.reshape(rows, 4, 2, HEAD_DIM).transpose(0, 2, 1, 3).reshape(rows, SWA_Q_WIDTH)
    kb = _pair_lanes(dw_p[:, o + SWA_Q_WIDTH:o + SWA_Q_WIDTH + SWA_KV_WIDTH])
    vb = dw_p[:, o + SWA_Q_WIDTH + SWA_KV_WIDTH:]
    return jnp.concatenate([dil, qb, kb, vb, dw_g], axis=1)


def _swa_rows(w_b):
    return w_b.reshape(2, 4, HEAD_DIM, -1).transpose(1, 0, 2, 3).reshape(SWA_Q_WIDTH, -1)


def _swa_rows_inv(dw_b):
    return dw_b.reshape(4, 2, HEAD_DIM, -1).transpose(1, 0, 2, 3).reshape(SWA_Q_WIDTH, -1)


def _rope_tables(pos):
    half = HEAD_DIM // 2
    inv = ROPE_THETA ** (-jnp.arange(half, dtype=F32) / half)
    ang = pos.astype(F32)[:, None] * inv
    c, s = jnp.cos(ang), jnp.sin(ang)
    return jnp.concatenate([c, c, c, c], axis=1), jnp.concatenate([-s, -s, s, s], axis=1)


def _local_step(x, mem, pos, target, w_in, dep, rest_weights, on_grads, g_mix, g_cross, g_mem, g_mlp, g_final, sink):
    t = x.shape[0]
    tm = min(512, t)
    tq = min(256, t // 16)
    w_p, w_g = _split_w_in(w_in)
    cos, sin = _rope_tables(pos)
    sink_row = jnp.pad(sink.reshape(2, 4).T.reshape(1, 8), ((0, 0), (0, 120)))
    tabs = [(cos[None], sin[None])]
    for _, d in DIL_GROUPS[1:]:
        c_d, s_d = _rope_tables(pos.reshape(t // d, d).T.reshape(-1))
        tabs.append((c_d.reshape(d, t // d, 128), s_d.reshape(d, t // d, 128)))
    tabs.append(tabs[0])

    h, h1, h2, p0, p1, p2, pb = _inproj(x, g_mix, w_p, cos, sin, tm, dep)
    ps = [p0[None], p1, p2, pb[None]]
    outs, lses = [], []
    for gi, pv in enumerate(ps):
        o, l = _attn_fwd(f"attn_fwd{gi}", pv, gi == 3, sink_row[0, :8], tq)
        outs.append(o)
        lses.append(l)
    o0, l0, ob, lb = outs[0][0], lses[0][0], outs[3][0], lses[3][0]
    wts = rest_weights(lb)
    w_b = _swa_rows(wts["w_branch_b"])
    tf = wts["w_1"].shape[2]
    gts = _gates(h, w_g, wts["b_gate"].reshape(1, GATE_WIDTH), tm, 1024)
    oa, ya, yb, merged, x1, hc = _mix(o0, outs[1], outs[2], l0, lses[1], lses[2], ob, gts, x,
                                      wts["w_branch_a"], w_b, wts["w_out"], g_cross, tm)
    mn, kv = _memkv(mem, g_mem, wts["w_ckv"])
    q, o, x2, hm = _cross(hc, x1, kv, wts["w_cq"], wts["w_co"], g_mlp, tm)
    a, dx3, loss, dg_final = _mlp(hm, x2, wts["w_1"], wts["w_2"], g_final.reshape(1, D_MODEL), target, tm, tf)

    grads = {}
    dz, dx2, dg_mlp = _mlp_bwd(dx3, a, wts["w_1"], wts["w_2"], x2, g_mlp, tm, tf)
    grads["w_2"] = _shards_from_rows(_wgrad("dw_2", a, dx3, 1024, 1024, tm, square=True))
    grads["w_1"] = _wgrad("dw_1", hm, dz, 1024, tf, tm, col_shards=True)
    dep = on_grads(GROUP_A, grads)
    dq, dx1, dkv, dg_cross = _cross_bwd(dx2, x1, q, kv, wts["w_cq"], wts["w_co"], g_cross, tm, dep)
    grads["w_co"] = _shards_from_rows(_wgrad("dw_co", o, dx2, 1024, 1024, tm))
    grads["w_cq"] = _shards_from_rows(_wgrad("dw_cq", hc, dq, 1024, 1024, tm))
    grads["w_ckv"], dg_mem = _memkv_bwd(dkv, mn, mem, wts["w_ckv"], g_mem)
    (dgt, dh_part, dya, dyb, do0, do1, do2, c0, c1, c2, dob, cb, db_gate, dsink) = _mix_bwd(
        dx1, ya, yb, gts, oa, ob, l0, lses[1], lses[2], lb, sink_row,
        wts["w_out"], wts["w_branch_a"], w_b, w_g, min(256, tm))
    grads["w_out"] = _shards_from_rows(_wgrad("dw_out", merged, dx1, 1024, 1024, tm))
    grads["w_branch_a"] = _shards_from_cols(_wgrad("dw_a", oa, dya, 512, 1024, tm))
    grads["w_branch_b"] = _shards_from_cols(_swa_rows_inv(_wgrad("dw_b", ob, dyb, 512, 1024, tm)))
    grads["b_gate"] = _shards_from_cols(db_gate.reshape(2, D_MODEL)).astype(BF16)
    dep = on_grads(GROUP_B, grads)
    dw_g = _wgrad("dw_g", h, dgt, 1024, 1024, tm)
    dps = []
    for gi, (pv, do_g, c_g) in enumerate(zip(ps, (do0[None], do1, do2, dob[None]), (c0[None], c1, c2, cb[None]))):
        dps.append(_attn_bwd(f"attn_bwd{gi}", pv, do_g, lses[gi], c_g, tabs[gi][0], tabs[gi][1], gi == 3, tq,
                             dep if gi == 0 else None))
    dw_p = jnp.concatenate(
        [_wgrad(f"dw_p{gi}", hh.reshape(t, D_MODEL), dpg.reshape(t, -1), 1024, PBLK, tm)
         for gi, (hh, dpg) in enumerate(zip((h, h1, h2, h), dps))], axis=1)
    grads["w_in"] = _shards_from_cols(_merge_w_in(dw_p, dw_g))
    dep = on_grads(GROUP_C, grads)
    grad_x, dg_mix = _dx(dps[0][0], dps[1], dps[2], dps[3][0], w_p, dh_part, dx1, x, g_mix, min(256, tm), dep)
    dsink_heads = dsink[0, :8].reshape(4, 2).T.reshape(8)
    small = {"g_mix": dg_mix[0], "g_cross": dg_cross[0], "g_mem": dg_mem[0], "g_mlp": dg_mlp[0],
             "g_final": dg_final[0], "sink": dsink_heads}
    return loss[0, 0], grad_x, small


def kernel(x, mem, positions, g_mix, w_in, b_gate, sink, w_branch_a, w_branch_b, w_out, g_cross, g_mem, w_cq, w_ckv, w_co, g_mlp, w_1, w_2, g_final, loss_target, m_g_mix, m_w_in, m_b_gate, m_sink, m_w_branch_a, m_w_branch_b, m_w_out, m_g_cross, m_g_mem, m_w_cq, m_w_ckv, m_w_co, m_g_mlp, m_w_1, m_w_2, m_g_final, v_g_mix, v_w_in, v_b_gate, v_sink, v_w_branch_a, v_w_branch_b, v_w_out, v_g_cross, v_g_mem, v_w_cq, v_w_ckv, v_w_co, v_g_mlp, v_w_1, v_w_2, v_g_final):
    local = dict(locals())
    shard = {n: local[n][0] for n in GROUP_A + GROUP_B + GROUP_C}
    me = _my_index()
    me_arr = me.reshape(1).astype(jnp.int32)
    tags = {GROUP_A: "a", GROUP_B: "b", GROUP_C: "c"}

    w_in_full = _cols_from_shards(_all_gather(shard["w_in"].astype(BF16)))
    rest = GROUP_A + GROUP_B

    def placed(name):
        a = shard[name] if name == "b_gate" else shard[name].astype(BF16)
        return lax.dynamic_update_slice(lax.empty((N_DEV,) + a.shape, a.dtype), a[None], (me, 0, 0))

    gather = _split_start("gather_start", True, [placed(n) for n in rest])

    def rest_weights(after):
        full = {}
        for name, a in zip(rest, _split_wait("gather_wait", True, gather, after)):
            if name in ("w_1", "w_ckv"):
                full[name] = a
            elif name in _COL_SHARDED:
                full[name] = _cols_from_shards(a)
            else:
                full[name] = a.reshape(N_DEV * a.shape[1], a.shape[2])
        return full

    scatters = {}

    def on_grads(names, grads):
        srcs = [grads[n] for n in names]
        lands = [lax.empty((N_DEV - 1,) + g.shape[1:], g.dtype) for g in srcs]
        scatters[names] = _split_start("scatter_start_" + tags[names], False, srcs + lands)
        return scatters[names][-1]

    loss, grad_x, small = _local_step(
        x[0], mem[0], positions[0], loss_target[0], w_in_full, gather[-1], rest_weights, on_grads,
        g_mix, g_cross, g_mem, g_mlp, g_final, sink[0])

    after, updated = grad_x, {}
    for names in (GROUP_A, GROUP_B, GROUP_C):
        bufs = _split_wait("scatter_wait_" + tags[names], False, scatters[names], after)
        for i, name in enumerate(names):
            outs = _adamw("adamw_" + name, me_arr, bufs[i], bufs[len(names) + i], shard[name],
                          local["m_" + name][0], local["v_" + name][0], ADAM_ROWS[name])
            updated[name] = [a[None] for a in outs]
            after = outs[3]

    sp = jnp.stack([small[n] if n != "sink" else jnp.pad(small[n], (0, LANES - 8)) for n in SMALL]
                   + [jnp.zeros((LANES,), F32)] * 2)
    flat = lambda prefix: [local[prefix + n].reshape(1, -1) for n in SMALL]
    outs = _adamw_small(_exchange_small(sp), flat(""), flat("m_"), flat("v_"))
    for i, name in enumerate(SMALL):
        updated[name] = [outs[which][i].reshape(local[name].shape) for which in range(4)]

    order = ["g_mix", "w_in", "b_gate", "sink", "w_branch_a", "w_branch_b", "w_out", "g_cross", "g_mem", "w_cq",
             "w_ckv", "w_co", "g_mlp", "w_1", "w_2", "g_final"]
    res = [lax.psum(loss, ("x", "y", "c")), grad_x[None]]
    for which in range(4):
        res += [updated[n][which] for n in order]
    return tuple(res)
```

```python
import functools
import math

import jax
import jax.numpy as jnp
from jax import lax
from jax.experimental import pallas as pl
from jax.experimental.pallas import tpu as pltpu

F32 = jnp.float32
BF16 = jnp.bfloat16

D_MODEL = 1024
HEAD_DIM = 64
DIL_GROUPS = ((128, 1), (512, 4), (2048, 16))
ROPE_THETA = 10000.0
X_HEADS = 4
X_HEAD_DIM = D_MODEL // X_HEADS
D_FF = 4 * D_MODEL
EPS = 1e-6
DIL_WIDTH = 1536
SWA_Q_WIDTH = 512
SWA_KV_WIDTH = 128
P_WIDTH = 3 * DIL_WIDTH + SWA_Q_WIDTH + 2 * SWA_KV_WIDTH
GATE_WIDTH = 2 * D_MODEL
IN_WIDTH = P_WIDTH + GATE_WIDTH
BAND = 128
PBLK = 768
Q_SCALE = HEAD_DIM ** -0.5
X_SCALE = X_HEAD_DIM ** -0.5

ADAM_LR = 0.001
ADAM_B1 = 0.9
ADAM_B2 = 0.999
ADAM_EPS = 1e-08
ADAM_WD = 0.01
ADAM_STEP = 10

N_DEV = 8
LANES = 1024
VMEM_LIMIT = 52 * 1024 * 1024

NT = (((1,), (1,)), ((), ()))
TN = (((0,), (0,)), ((), ()))

GROUP_A = ("w_1", "w_2")
GROUP_B = ("w_branch_a", "w_branch_b", "w_out", "w_cq", "w_ckv", "w_co", "b_gate")
GROUP_C = ("w_in",)
_COL_SHARDED = ("w_in", "w_branch_a", "w_branch_b", "w_ckv", "w_1", "b_gate")
ADAM_ROWS = {"w_in": 256, "w_branch_a": 512, "w_branch_b": 512, "w_out": 128, "w_cq": 128, "w_ckv": 512,
             "w_co": 128, "w_1": 256, "w_2": 256, "b_gate": 2}
SMALL = ("g_mix", "g_cross", "g_mem", "g_mlp", "g_final", "sink")


def _params(sem=None):
    return pltpu.CompilerParams(dimension_semantics=sem, vmem_limit_bytes=VMEM_LIMIT)


def _dot(a, b):
    return jnp.dot(a, b, preferred_element_type=F32)


def _dot_nt(a, b):
    return lax.dot_general(a, b, NT, preferred_element_type=F32)


def _dot_tn(a, b):
    return lax.dot_general(a, b, TN, preferred_element_type=F32)


def _rms(xt):
    return lax.rsqrt(jnp.mean(xt * xt, axis=-1, keepdims=True) + EPS)


def _rms_bwd(dh, xt, r, g):
    xn = xt * r
    dxn = dh * g
    dx = r * (dxn - xn * jnp.mean(dxn * xn, axis=-1, keepdims=True))
    return dx, jnp.sum(dh * xn, axis=0, keepdims=True)


def _rope(x, c, s, swa, sign):
    kinds = "qqqqkv" if swa else "qkvqkv"
    cq, sq = c * Q_SCALE, s * (sign * Q_SCALE)
    sk = s * sign if sign != 1 else s
    out = []
    for ci, kind in enumerate(kinds):
        xc = x[:, ci * 128:(ci + 1) * 128]
        if kind == "v":
            out.append(xc)
        elif kind == "q":
            out.append(xc * cq + pltpu.roll(xc, 64, 1) * sq)
        else:
            out.append(xc * c + pltpu.roll(xc, 64, 1) * sk)
    return jnp.concatenate(out, axis=1)


def _lane_scratch(rows, w):
    return pltpu.VMEM((w // 128, rows, 128), F32)


def _deinterleave(val, scr_ref, dst_ref, dtype):
    d, n = dst_ref.shape[0], dst_ref.shape[1]
    nc = val.shape[1] // 128
    for c in range(nc):
        scr_ref[c] = val[:, c * 128:(c + 1) * 128]
    for r in range(d):
        rows = [scr_ref.at[c][pl.ds(r, n, stride=d), :] for c in range(nc)]
        dst_ref[r] = jnp.concatenate(rows, axis=1).astype(dtype)


def _res_spec(a, tm):
    d, w = a.shape[0], a.shape[2]
    return pl.BlockSpec((d, tm // d, w), lambda i: (0, i, 0))


def _interleave(src_ref, scr_ref):
    d, n = src_ref.shape[0], src_ref.shape[1]
    nc = src_ref.shape[2] // 128
    for r in range(d):
        v = src_ref[r].astype(F32)
        for c in range(nc):
            scr_ref.at[c][pl.ds(r, n, stride=d), :] = v[:, c * 128:(c + 1) * 128]
    return jnp.concatenate([scr_ref[c] for c in range(nc)], axis=1)


def _with_dep(body, n_in, dep):
    if dep is None:
        return body
    return lambda *refs: body(*refs[:n_in], *refs[n_in + 1:])


def _dep_spec(dep):
    return [] if dep is None else [pl.BlockSpec(memory_space=pl.ANY)]


def _dep_arg(dep):
    return [] if dep is None else [dep]


def _inproj(x, g, w_p, tabs, tm, dep=None):
    t = x.shape[0]
    nj = P_WIDTH // PBLK
    (cos, sin), (cos1, sin1), (cos2, sin2) = tabs[0], tabs[1], tabs[2]

    def body(x_ref, g_ref, w_ref, c_ref, s_ref, c1_ref, s1_ref, c2_ref, s2_ref,
             h_ref, h1_ref, h2_ref, p0_ref, p1_ref, p2_ref, pb_ref, hf_ref):
        j = pl.program_id(1)

        @pl.when(j == 0)
        def _():
            xt = x_ref[...]
            hf = xt * _rms(xt) * g_ref[...]
            h_ref[...] = hf.astype(BF16)
            _deinterleave(hf, hf_ref, h1_ref, BF16)
            _deinterleave(hf, hf_ref, h2_ref, BF16)

        def emit(lhs_ref, cc_ref, ss_ref, out_ref, swa):
            rows = lambda ref: ref[...].reshape(tm, ref.shape[-1])
            acc = _dot(rows(lhs_ref), w_ref[...])
            out_ref[...] = _rope(acc, rows(cc_ref), rows(ss_ref), swa, 1).astype(BF16).reshape(out_ref.shape)

        pl.when(j < 2)(lambda: emit(h_ref, c_ref, s_ref, p0_ref, False))
        pl.when((j >= 2) & (j < 4))(lambda: emit(h1_ref, c1_ref, s1_ref, p1_ref, False))
        pl.when((j >= 4) & (j < 6))(lambda: emit(h2_ref, c2_ref, s2_ref, p2_ref, False))
        pl.when(j == nj - 1)(lambda: emit(h_ref, c_ref, s_ref, pb_ref, True))

    d1, d2 = DIL_GROUPS[1][1], DIL_GROUPS[2][1]
    col = lambda lo: (lambda i, j: (0, i, jnp.clip(j - lo, 0, 1)))
    tab = lambda d: pl.BlockSpec((d, tm // d, 128), lambda i, j: (0, i, 0))
    sds = jax.ShapeDtypeStruct
    return pl.pallas_call(
        _with_dep(body, 9, dep), name="inproj", grid=(t // tm, nj),
        in_specs=[pl.BlockSpec((tm, D_MODEL), lambda i, j: (i, 0)),
                  pl.BlockSpec((1, D_MODEL), lambda i, j: (0, 0)),
                  pl.BlockSpec((D_MODEL, PBLK), lambda i, j: (0, j)),
                  pl.BlockSpec((tm, 128), lambda i, j: (i, 0)),
                  pl.BlockSpec((tm, 128), lambda i, j: (i, 0)),
                  tab(d1), tab(d1), tab(d2), tab(d2)] + _dep_spec(dep),
        out_specs=[pl.BlockSpec((tm, D_MODEL), lambda i, j: (i, 0)),
                   pl.BlockSpec((d1, tm // d1, D_MODEL), lambda i, j: (0, i, 0)),
                   pl.BlockSpec((d2, tm // d2, D_MODEL), lambda i, j: (0, i, 0)),
                   pl.BlockSpec((tm, PBLK), lambda i, j: (i, jnp.minimum(j, 1))),
                   pl.BlockSpec((d1, tm // d1, PBLK), col(2)),
                   pl.BlockSpec((d2, tm // d2, PBLK), col(4)),
                   pl.BlockSpec((tm, PBLK), lambda i, j: (i, 0))],
        out_shape=[sds((t, D_MODEL), BF16), sds((d1, t // d1, D_MODEL), BF16), sds((d2, t // d2, D_MODEL), BF16),
                   sds((t, 2 * PBLK), BF16), sds((d1, t // d1, 2 * PBLK), BF16), sds((d2, t // d2, 2 * PBLK), BF16),
                   sds((t, PBLK), BF16)],
        scratch_shapes=[_lane_scratch(tm, D_MODEL)],
        compiler_params=_params(("arbitrary", "arbitrary")),
    )(x, g, w_p, cos, sin, cos1, sin1, cos2, sin2, *_dep_arg(dep))


def _gates(h, w_g, b, tm, tn):
    t = h.shape[0]

    def body(h_ref, w_ref, b_ref, o_ref):
        z = _dot(h_ref[...], w_ref[...]) + b_ref[...]
        o_ref[...] = jax.nn.sigmoid(z).astype(BF16)

    return pl.pallas_call(
        body, name="gates", grid=(t // tm, GATE_WIDTH // tn),
        in_specs=[pl.BlockSpec((tm, D_MODEL), lambda i, j: (i, 0)),
                  pl.BlockSpec((D_MODEL, tn), lambda i, j: (0, j)),
                  pl.BlockSpec((1, tn), lambda i, j: (0, j))],
        out_specs=pl.BlockSpec((tm, tn), lambda i, j: (i, j)),
        out_shape=jax.ShapeDtypeStruct((t, GATE_WIDTH), BF16),
        compiler_params=_params(("arbitrary", "arbitrary")),
    )(h, w_g, b)


def _band_mask(i, s):
    row = lax.broadcasted_iota(jnp.int32, (BAND, 2 * BAND), 0)
    col = lax.broadcasted_iota(jnp.int32, (BAND, 2 * BAND), 1)
    band = (col >= row) & (col <= row + BAND)
    if s == 0:
        band = band & ((col >= BAND) | (i > 0))
    return band


def _head_a_masks(rows):
    lane = lax.broadcasted_iota(jnp.int32, (rows, 128), 1)
    return (lane % HEAD_DIM) < HEAD_DIM // 2, lane < HEAD_DIM


def _stack_heads(x, head_a):
    zero = jnp.zeros_like(x)
    return jnp.concatenate([jnp.where(head_a, x, zero), jnp.where(head_a, zero, x)], axis=0)


def _kv_rows(cur_ref, tail_ref, s, off):
    if s == 0:
        return jnp.concatenate([tail_ref[:, off:off + 128], cur_ref[0:BAND, off:off + 128]], axis=0)
    return cur_ref[(s - 1) * BAND:(s + 1) * BAND, off:off + 128]


def _attn_layout(swa):
    if swa:
        return [(128 * j, 512, 640) for j in range(4)]
    return [(0, 128, 256), (384, 512, 640)]


def _attn_fwd(name, pv, swa, sinks, tq):
    d, ls = pv.shape[0], pv.shape[1]
    n, nsb = ls // tq, tq // BAND
    pairs = _attn_layout(swa)
    ncol = 1 if swa else 2
    ow = 128 * len(pairs)

    def body(cur_ref, tail_ref, *rest):
        sink_ref = rest[0] if swa else None
        o_ref, lse_ref = rest[-2:]
        i = pl.program_id(2)
        lane = lax.broadcasted_iota(jnp.int32, (BAND, 128), 1)
        qk_a, v_a = _head_a_masks(BAND)
        first = lax.broadcasted_iota(jnp.int32, (2 * BAND, 1), 0) < BAND
        for s in range(nsb):
            mask = _band_mask(i, s)
            mask2 = jnp.concatenate([mask, mask], axis=0)
            rows = slice(s * BAND, (s + 1) * BAND)
            lse_tile = jnp.zeros((BAND, 128), F32)
            for j, (qo, ko, vo) in enumerate(pairs):
                q = cur_ref[rows, qo:qo + 128]
                kk = _kv_rows(cur_ref, tail_ref, s, ko)
                vv = _kv_rows(cur_ref, tail_ref, s, vo)
                sc = _dot_nt(_stack_heads(q, qk_a), kk)
                sc = jnp.where(mask2, sc, -jnp.inf)
                m = jnp.max(sc, axis=-1, keepdims=True)
                if swa:
                    sk = jnp.where(first, sink_ref[2 * j], sink_ref[2 * j + 1])
                    m = jnp.maximum(m, sk)
                p = jnp.exp(sc - m)
                den = jnp.sum(p, axis=-1, keepdims=True)
                if swa:
                    den = den + jnp.exp(sk - m)
                lse = m + jnp.log(den)
                lse_tile = jnp.where(lane == 2 * j, lse[:BAND], jnp.where(lane == 2 * j + 1, lse[BAND:], lse_tile))
                o2 = _dot((p * (1.0 / den)).astype(BF16), vv)
                o_ref[rows, j * 128:(j + 1) * 128] = jnp.where(v_a, o2[:BAND], o2[BAND:]).astype(BF16)
            lse_ref[rows, :] = lse_tile

    in_specs = [pl.BlockSpec((None, tq, PBLK), lambda r, cb, i: (r, i, cb)),
                pl.BlockSpec((None, BAND, PBLK), lambda r, cb, i: (r, jnp.maximum(i * nsb - 1, 0), cb))]
    args = [pv, pv]
    if swa:
        in_specs.append(pl.BlockSpec(memory_space=pltpu.SMEM))
        args.append(sinks)
    return pl.pallas_call(
        body, name=name, grid=(d, ncol, n),
        in_specs=in_specs,
        out_specs=[pl.BlockSpec((None, tq, ow), lambda r, cb, i: (r, i, cb)),
                   pl.BlockSpec((None, tq, 128), lambda r, cb, i: (r, i, cb))],
        out_shape=[jax.ShapeDtypeStruct((d, ls, 512), BF16), jax.ShapeDtypeStruct((d, ls, 128 * ncol), F32)],
        compiler_params=_params(("arbitrary", "arbitrary", "arbitrary")),
    )(*args)


def _lse_lane(h):
    return (h // 4) * 128 + h % 4


def _head_scale(x, tile, lanes):
    lane = lax.broadcasted_iota(jnp.int32, (x.shape[0], 128), 1)
    lo = lane < HEAD_DIM
    out = []
    for c in range(x.shape[1] // 128):
        a0 = tile[:, lanes[2 * c]:lanes[2 * c] + 1]
        a1 = tile[:, lanes[2 * c + 1]:lanes[2 * c + 1] + 1]
        out.append(x[:, c * 128:(c + 1) * 128] * jnp.where(lo, a0, a1))
    return jnp.concatenate(out, axis=1)


def _head_sums(x, lanes, width):
    lane = lax.broadcasted_iota(jnp.int32, (x.shape[0], width), 1)
    out = jnp.zeros((x.shape[0], width), F32)
    for h in range(x.shape[1] // HEAD_DIM):
        sm = jnp.sum(x[:, h * HEAD_DIM:(h + 1) * HEAD_DIM], axis=-1, keepdims=True)
        out = jnp.where(lane == lanes[h], sm, out)
    return out


def _alphas(l0, l1, l2):
    m = jnp.maximum(jnp.maximum(l0, l1), l2)
    e0, e1, e2 = jnp.exp(l0 - m), jnp.exp(l1 - m), jnp.exp(l2 - m)
    den = e0 + e1 + e2
    return e0 / den, e1 / den, e2 / den


DIL_LANES = [_lse_lane(h) for h in range(8)]
SWA_LANES = list(range(8))


def _mix(o0, o1, o2, l0, l1, l2, ob, gts, x, w_a, w_b, w_out, g_cross, tm):
    t = x.shape[0]

    def body(o0_ref, o1_ref, o2_ref, l0_ref, l1_ref, l2_ref, ob_ref, g_ref, x_ref, wa_ref, wb_ref, wo_ref,
             gc_ref, oa_ref, ya_ref, yb_ref, mg_ref, x1_ref, hc_ref, so_ref, sl_ref):
        a0, a1, a2 = _alphas(l0_ref[...], _interleave(l1_ref, sl_ref), _interleave(l2_ref, sl_ref))
        oa = (_head_scale(o0_ref[...].astype(F32), a0, DIL_LANES)
              + _head_scale(_interleave(o1_ref, so_ref), a1, DIL_LANES)
              + _head_scale(_interleave(o2_ref, so_ref), a2, DIL_LANES))
        oab = oa.astype(BF16)
        oa_ref[...] = oab
        ya = _dot(oab, wa_ref[...])
        yb = _dot(ob_ref[...], wb_ref[...])
        ya_ref[...] = ya.astype(BF16)
        yb_ref[...] = yb.astype(BF16)
        merged = (g_ref[:, :D_MODEL].astype(F32) * ya + g_ref[:, D_MODEL:].astype(F32) * yb).astype(BF16)
        mg_ref[...] = merged
        x1 = x_ref[...] + _dot(merged, wo_ref[...])
        x1_ref[...] = x1
        hc_ref[...] = (x1 * _rms(x1) * gc_ref[...]).astype(BF16)

    row = lambda w: pl.BlockSpec((tm, w), lambda i: (i, 0))
    full = lambda a, b: pl.BlockSpec((a, b), lambda i: (0, 0))
    return pl.pallas_call(
        body, name="mix", grid=(t // tm,),
        in_specs=[row(512), _res_spec(o1, tm), _res_spec(o2, tm), row(256), _res_spec(l1, tm), _res_spec(l2, tm),
                  row(512), row(GATE_WIDTH),
                  row(D_MODEL), full(512, D_MODEL), full(512, D_MODEL), full(D_MODEL, D_MODEL), full(1, D_MODEL)],
        out_specs=[row(512), row(D_MODEL), row(D_MODEL), row(D_MODEL), row(D_MODEL), row(D_MODEL)],
        out_shape=[jax.ShapeDtypeStruct((t, 512), BF16), jax.ShapeDtypeStruct((t, D_MODEL), BF16),
                   jax.ShapeDtypeStruct((t, D_MODEL), BF16), jax.ShapeDtypeStruct((t, D_MODEL), BF16),
                   jax.ShapeDtypeStruct((t, D_MODEL), F32), jax.ShapeDtypeStruct((t, D_MODEL), BF16)],
        scratch_shapes=[_lane_scratch(tm, 512), _lane_scratch(tm, 256)],
        compiler_params=_params(("arbitrary",)),
    )(o0, o1, o2, l0, l1, l2, ob, gts, x, w_a, w_b, w_out, g_cross)


def _memkv(mem, g_mem, w_ckv):
    m = mem.shape[0]
    ws = w_ckv.shape[2]

    def body(mem_ref, g_ref, w_ref, mn_ref, kv_ref):
        xt = mem_ref[...]
        mn = (xt * _rms(xt) * g_ref[...]).astype(BF16)
        mn_ref[...] = mn
        for j in range(N_DEV):
            kv_ref[:, j * ws:(j + 1) * ws] = _dot(mn, w_ref[j]).astype(BF16)

    return pl.pallas_call(
        body, name="memkv",
        out_shape=[jax.ShapeDtypeStruct((m, D_MODEL), BF16), jax.ShapeDtypeStruct((m, 2 * D_MODEL), BF16)],
        compiler_params=_params(),
    )(mem, g_mem, w_ckv)


def _cross_probs(q, kv_ref, h):
    k = kv_ref[:, h * X_HEAD_DIM:(h + 1) * X_HEAD_DIM]
    sc = _dot_nt(q[:, h * X_HEAD_DIM:(h + 1) * X_HEAD_DIM], k)
    m = jnp.max(sc, axis=-1, keepdims=True)
    p = jnp.exp(sc - m)
    return p / jnp.sum(p, axis=-1, keepdims=True)


def _cross(hc, x1, kv, w_cq, w_co, g_mlp, tm):
    t = x1.shape[0]
    m = kv.shape[0]

    def body(hc_ref, x1_ref, kv_ref, wq_ref, wo_ref, g_ref, q_ref, o_ref, x2_ref, hm_ref):
        q = (_dot(hc_ref[...], wq_ref[...]) * X_SCALE).astype(BF16)
        q_ref[...] = q
        outs = []
        for h in range(X_HEADS):
            p = _cross_probs(q, kv_ref, h)
            v = kv_ref[:, D_MODEL + h * X_HEAD_DIM:D_MODEL + (h + 1) * X_HEAD_DIM]
            outs.append(_dot(p.astype(BF16), v))
        o = jnp.concatenate(outs, axis=1).astype(BF16)
        o_ref[...] = o
        x2 = x1_ref[...] + _dot(o, wo_ref[...])
        x2_ref[...] = x2
        hm_ref[...] = (x2 * _rms(x2) * g_ref[...]).astype(BF16)

    row = lambda w: pl.BlockSpec((tm, w), lambda i: (i, 0))
    full = lambda a, b: pl.BlockSpec((a, b), lambda i: (0, 0))
    return pl.pallas_call(
        body, name="cross", grid=(t // tm,),
        in_specs=[row(D_MODEL), row(D_MODEL), full(m, 2 * D_MODEL), full(D_MODEL, D_MODEL),
                  full(D_MODEL, D_MODEL), full(1, D_MODEL)],
        out_specs=[row(D_MODEL)] * 4,
        out_shape=[jax.ShapeDtypeStruct((t, D_MODEL), BF16), jax.ShapeDtypeStruct((t, D_MODEL), BF16),
                   jax.ShapeDtypeStruct((t, D_MODEL), F32), jax.ShapeDtypeStruct((t, D_MODEL), BF16)],
        compiler_params=_params(("arbitrary",)),
    )(hc, x1, kv, w_cq, w_co, g_mlp)


def _mlp(hm, x2, w_1, w_2, g_final, target, tm, tf):
    t = x2.shape[0]
    nf = D_FF // tf

    def body(hm_ref, x2_ref, w1_ref, w2_ref, g_ref, tg_ref, a_ref, dx3_ref, loss_ref, dg_ref, acc_ref):
        i, f = pl.program_id(0), pl.program_id(1)
        hm_t = hm_ref[...]
        a = jnp.concatenate([jnp.maximum(_dot(hm_t, w1_ref[s]), 0.0) for s in range(w1_ref.shape[0])], axis=1)
        a_ref[...] = a.astype(BF16)
        part = _dot((a * a).astype(BF16), w2_ref[...])

        @pl.when(f == 0)
        def _():
            acc_ref[...] = part

        @pl.when(f > 0)
        def _():
            acc_ref[...] += part

        @pl.when((i == 0) & (f == 0))
        def _():
            loss_ref[...] = jnp.zeros_like(loss_ref)
            dg_ref[...] = jnp.zeros_like(dg_ref)

        @pl.when(f == nf - 1)
        def _():
            x3 = x2_ref[...] + acc_ref[...]
            r = _rms(x3)
            g = g_ref[...]
            diff = x3 * r * g - tg_ref[...]
            loss_ref[...] += 0.5 * jnp.sum(jnp.mean(diff * diff, axis=-1, keepdims=True))
            dx3, dg = _rms_bwd(diff / D_MODEL, x3, r, g)
            dx3_ref[...] = dx3
            dg_ref[...] += dg

    return pl.pallas_call(
        body, name="mlp", grid=(t // tm, nf),
        in_specs=[pl.BlockSpec((tm, D_MODEL), lambda i, f: (i, 0)),
                  pl.BlockSpec((tm, D_MODEL), lambda i, f: (i, 0)),
                  pl.BlockSpec((tf // w_1.shape[2], D_MODEL, w_1.shape[2]), lambda i, f: (f, 0, 0)),
                  pl.BlockSpec((tf, D_MODEL), lambda i, f: (f, 0)),
                  pl.BlockSpec((1, D_MODEL), lambda i, f: (0, 0)),
                  pl.BlockSpec((tm, D_MODEL), lambda i, f: (i, 0))],
        out_specs=[pl.BlockSpec((tm, tf), lambda i, f: (i, f)),
                   pl.BlockSpec((tm, D_MODEL), lambda i, f: (i, 0)),
                   pl.BlockSpec((1, 128), lambda i, f: (0, 0)),
                   pl.BlockSpec((1, D_MODEL), lambda i, f: (0, 0))],
        out_shape=[jax.ShapeDtypeStruct((t, D_FF), BF16), jax.ShapeDtypeStruct((t, D_MODEL), F32),
                   jax.ShapeDtypeStruct((1, 128), F32), jax.ShapeDtypeStruct((1, D_MODEL), F32)],
        scratch_shapes=[pltpu.VMEM((tm, D_MODEL), F32)],
        compiler_params=_params(("arbitrary", "arbitrary")),
    )(hm, x2, w_1, w_2, g_final, target)


def _mlp_bwd(dx3, a, w_1, w_2, x2, g_mlp, tm, tf):
    t = x2.shape[0]
    nf = D_FF // tf

    def body(dx3_ref, a_ref, w1_ref, w2_ref, x2_ref, g_ref, dz_ref, dx2_ref, dg_ref, acc_ref):
        i, f = pl.program_id(0), pl.program_id(1)
        da2 = _dot_nt(dx3_ref[...].astype(BF16), w2_ref[...])
        dz = (2.0 * a_ref[...].astype(F32) * da2).astype(BF16)
        dz_ref[...] = dz
        sw = w1_ref.shape[2]
        part = _dot_nt(dz[:, 0:sw], w1_ref[0])
        for s in range(1, w1_ref.shape[0]):
            part = part + _dot_nt(dz[:, s * sw:(s + 1) * sw], w1_ref[s])

        @pl.when(f == 0)
        def _():
            acc_ref[...] = part

        @pl.when(f > 0)
        def _():
            acc_ref[...] += part

        @pl.when((i == 0) & (f == 0))
        def _():
            dg_ref[...] = jnp.zeros_like(dg_ref)

        @pl.when(f == nf - 1)
        def _():
            xt = x2_ref[...]
            dx, dg = _rms_bwd(acc_ref[...], xt, _rms(xt), g_ref[...])
            dx2_ref[...] = dx3_ref[...] + dx
            dg_ref[...] += dg

    return pl.pallas_call(
        body, name="mlp_bwd", grid=(t // tm, nf),
        in_specs=[pl.BlockSpec((tm, D_MODEL), lambda i, f: (i, 0)),
                  pl.BlockSpec((tm, tf), lambda i, f: (i, f)),
                  pl.BlockSpec((tf // w_1.shape[2], D_MODEL, w_1.shape[2]), lambda i, f: (f, 0, 0)),
                  pl.BlockSpec((tf, D_MODEL), lambda i, f: (f, 0)),
                  pl.BlockSpec((tm, D_MODEL), lambda i, f: (i, 0)),
                  pl.BlockSpec((1, D_MODEL), lambda i, f: (0, 0))],
        out_specs=[pl.BlockSpec((tm, tf), lambda i, f: (i, f)),
                   pl.BlockSpec((tm, D_MODEL), lambda i, f: (i, 0)),
                   pl.BlockSpec((1, D_MODEL), lambda i, f: (0, 0))],
        out_shape=[jax.ShapeDtypeStruct((t, D_FF), BF16), jax.ShapeDtypeStruct((t, D_MODEL), F32),
                   jax.ShapeDtypeStruct((1, D_MODEL), F32)],
        scratch_shapes=[pltpu.VMEM((tm, D_MODEL), F32)],
        compiler_params=_params(("arbitrary", "arbitrary")),
    )(dx3, a, w_1, w_2, x2, g_mlp)


def _wgrad(name, a, b, tka, tn, tm, square=False, col_shards=False):
    t, ka = a.shape
    n = b.shape[1]
    nk = t // tm

    def body(a_ref, b_ref, o_ref, acc_ref):
        at = a_ref[...].astype(BF16)
        if square:
            at = at * at
        part = _dot_tn(at, b_ref[...].astype(BF16))
        k = pl.program_id(2)

        @pl.when(k == 0)
        def _():
            acc_ref[...] = part

        @pl.when(k > 0)
        def _():
            acc_ref[...] += part

        @pl.when(k == nk - 1)
        def _():
            if col_shards:
                for s in range(tn // sw):
                    o_ref[s] = acc_ref[:, s * sw:(s + 1) * sw].astype(BF16)
            else:
                o_ref[...] = acc_ref[...].astype(BF16)

    if col_shards:
        sw = n // N_DEV
        out_spec = pl.BlockSpec((tn // sw, tka, sw), lambda p, q, k: (q, p, 0))
        out_shape = jax.ShapeDtypeStruct((N_DEV, ka, sw), BF16)
    else:
        out_spec = pl.BlockSpec((tka, tn), lambda p, q, k: (p, q))
        out_shape = jax.ShapeDtypeStruct((ka, n), BF16)
    return pl.pallas_call(
        body, name=name, grid=(ka // tka, n // tn, nk),
        in_specs=[pl.BlockSpec((tm, tka), lambda p, q, k: (k, p)),
                  pl.BlockSpec((tm, tn), lambda p, q, k: (k, q))],
        out_specs=out_spec, out_shape=out_shape,
        scratch_shapes=[pltpu.VMEM((tka, tn), F32)],
        compiler_params=_params(("arbitrary", "arbitrary", "arbitrary")),
    )(a, b)


def _cross_bwd(dx2, x1, q, kv, w_cq, w_co, g_cross, tm, dep=None):
    t = x1.shape[0]
    m = kv.shape[0]

    def body(dx2_ref, x1_ref, q_ref, kv_ref, wq_ref, wo_ref, g_ref, dq_ref, dx1_ref, dkv_ref, dg_ref):
        @pl.when(pl.program_id(0) == 0)
        def _():
            dkv_ref[...] = jnp.zeros_like(dkv_ref)
            dg_ref[...] = jnp.zeros_like(dg_ref)

        do = _dot_nt(dx2_ref[...].astype(BF16), wo_ref[...]).astype(BF16)
        q = q_ref[...]
        dqs = []
        for h in range(X_HEADS):
            hs = slice(h * X_HEAD_DIM, (h + 1) * X_HEAD_DIM)
            vs = slice(D_MODEL + h * X_HEAD_DIM, D_MODEL + (h + 1) * X_HEAD_DIM)
            p = _cross_probs(q, kv_ref, h)
            dp = _dot_nt(do[:, hs], kv_ref[:, vs])
            ds = (p * (dp - jnp.sum(dp * p, axis=-1, keepdims=True))).astype(BF16)
            dqs.append(_dot(ds, kv_ref[:, hs]))
            dkv_ref[:, hs] += _dot_tn(ds, q[:, hs])
            dkv_ref[:, vs] += _dot_tn(p.astype(BF16), do[:, hs])
        dq = (jnp.concatenate(dqs, axis=1) * X_SCALE).astype(BF16)
        dq_ref[...] = dq
        xt = x1_ref[...]
        dx, dg = _rms_bwd(_dot_nt(dq, wq_ref[...]), xt, _rms(xt), g_ref[...])
        dx1_ref[...] = dx2_ref[...] + dx
        dg_ref[...] += dg

    row = lambda w: pl.BlockSpec((tm, w), lambda i: (i, 0))
    full = lambda a, b: pl.BlockSpec((a, b), lambda i: (0, 0))
    return pl.pallas_call(
        _with_dep(body, 7, dep), name="cross_bwd", grid=(t // tm,),
        in_specs=[row(D_MODEL), row(D_MODEL), row(D_MODEL), full(m, 2 * D_MODEL), full(D_MODEL, D_MODEL),
                  full(D_MODEL, D_MODEL), full(1, D_MODEL)] + _dep_spec(dep),
        out_specs=[row(D_MODEL), row(D_MODEL), full(m, 2 * D_MODEL), full(1, D_MODEL)],
        out_shape=[jax.ShapeDtypeStruct((t, D_MODEL), BF16), jax.ShapeDtypeStruct((t, D_MODEL), F32),
                   jax.ShapeDtypeStruct((m, 2 * D_MODEL), F32), jax.ShapeDtypeStruct((1, D_MODEL), F32)],
        compiler_params=_params(("arbitrary",)),
    )(dx2, x1, q, kv, w_cq, w_co, g_cross, *_dep_arg(dep))


def _memkv_bwd(dkv, mn, mem, w_ckv, g_mem):
    ws = w_ckv.shape[2]

    def body(dkv_ref, mn_ref, mem_ref, w_ref, g_ref, dw_ref, dg_ref):
        mn = mn_ref[...]
        dmn = jnp.zeros(mn.shape, F32)
        for j in range(N_DEV):
            dkvb = dkv_ref[:, j * ws:(j + 1) * ws].astype(BF16)
            dw_ref[j] = _dot_tn(mn, dkvb).astype(BF16)
            dmn = dmn + _dot_nt(dkvb, w_ref[j])
        xt = mem_ref[...]
        dg_ref[...] = jnp.sum(dmn * xt * _rms(xt), axis=0, keepdims=True)

    return pl.pallas_call(
        body, name="memkv_bwd",
        out_shape=[jax.ShapeDtypeStruct(w_ckv.shape, BF16), jax.ShapeDtypeStruct((1, D_MODEL), F32)],
        compiler_params=_params(),
    )(dkv, mn, mem, w_ckv, g_mem)


def _mix_bwd(dx1, ya, yb, gts, oa, ob, l0, l1, l2, lb, sink_row, w_out, w_a, w_b, w_g, tm):
    t = dx1.shape[0]

    def body(dx1_ref, ya_ref, yb_ref, g_ref, oa_ref, ob_ref, l0_ref, l1_ref, l2_ref, lb_ref, sk_ref,
             wo_ref, wa_ref, wb_ref, wg_ref,
             dg_ref, dhp_ref, dya_ref, dyb_ref, do0_ref, do1_ref, do2_ref, c0_ref, c1_ref, c2_ref,
             dob_ref, cb_ref, db_ref, dsk_ref, so_ref, sl_ref):
        @pl.when(pl.program_id(0) == 0)
        def _():
            db_ref[...] = jnp.zeros_like(db_ref)
            dsk_ref[...] = jnp.zeros_like(dsk_ref)

        dm = _dot_nt(dx1_ref[...].astype(BF16), wo_ref[...])
        ga = g_ref[:, :D_MODEL].astype(F32)
        gb = g_ref[:, D_MODEL:].astype(F32)
        dya = (dm * ga).astype(BF16)
        dyb = (dm * gb).astype(BF16)
        dya_ref[...] = dya
        dyb_ref[...] = dyb
        dpa = dm * ya_ref[...].astype(F32) * ga * (1.0 - ga)
        dpb = dm * yb_ref[...].astype(F32) * gb * (1.0 - gb)
        dpre = jnp.concatenate([dpa, dpb], axis=1)
        db_ref[...] += jnp.sum(dpre, axis=0, keepdims=True)
        dpreb = dpre.astype(BF16)
        dg_ref[...] = dpreb
        dhp_ref[...] = _dot_nt(dpreb, wg_ref[...])

        doa = _dot_nt(dya, wa_ref[...])
        dob = _dot_nt(dyb, wb_ref[...])
        dsum = _head_sums(doa * oa_ref[...].astype(F32), DIL_LANES, 256)
        a0, a1, a2 = _alphas(l0_ref[...], _interleave(l1_ref, sl_ref), _interleave(l2_ref, sl_ref))
        c0_ref[...] = a0 * dsum
        do0_ref[...] = _head_scale(doa, a0, DIL_LANES).astype(BF16)
        for al, do_ref, c_ref in ((a1, do1_ref, c1_ref), (a2, do2_ref, c2_ref)):
            _deinterleave(al * dsum, sl_ref, c_ref, F32)
            _deinterleave(_head_scale(doa, al, DIL_LANES), so_ref, do_ref, BF16)
        dob_ref[...] = dob.astype(BF16)
        cb = _head_sums(dob * ob_ref[...].astype(F32), SWA_LANES, 128)
        cb_ref[...] = cb
        lane = lax.broadcasted_iota(jnp.int32, cb.shape, 1)
        psink = jnp.where(lane < 8, jnp.exp(sk_ref[...] - lb_ref[...]), 0.0)
        dsk_ref[...] += jnp.sum(-psink * cb, axis=0, keepdims=True)

    row = lambda w: pl.BlockSpec((tm, w), lambda i: (i, 0))
    full = lambda a, b: pl.BlockSpec((a, b), lambda i: (0, 0))
    sds = jax.ShapeDtypeStruct
    d1, d2 = l1.shape[0], l2.shape[0]
    res = lambda d, w: pl.BlockSpec((d, tm // d, w), lambda i: (0, i, 0))
    return pl.pallas_call(
        body, name="mix_bwd", grid=(t // tm,),
        in_specs=[row(D_MODEL), row(D_MODEL), row(D_MODEL), row(GATE_WIDTH), row(512), row(512),
                  row(256), _res_spec(l1, tm), _res_spec(l2, tm), row(128), full(1, 128),
                  full(D_MODEL, D_MODEL), full(512, D_MODEL), full(512, D_MODEL), full(D_MODEL, GATE_WIDTH)],
        out_specs=[row(GATE_WIDTH), row(D_MODEL), row(D_MODEL), row(D_MODEL),
                   row(512), res(d1, 512), res(d2, 512), row(256), res(d1, 256), res(d2, 256),
                   row(512), row(128), full(1, GATE_WIDTH), full(1, 128)],
        out_shape=[sds((t, GATE_WIDTH), BF16), sds((t, D_MODEL), F32), sds((t, D_MODEL), BF16),
                   sds((t, D_MODEL), BF16), sds((t, 512), BF16), sds((d1, t // d1, 512), BF16),
                   sds((d2, t // d2, 512), BF16), sds((t, 256), F32), sds((d1, t // d1, 256), F32),
                   sds((d2, t // d2, 256), F32), sds((t, 512), BF16),
                   sds((t, 128), F32), sds((1, GATE_WIDTH), F32), sds((1, 128), F32)],
        scratch_shapes=[_lane_scratch(tm, 512), _lane_scratch(tm, 256)],
        compiler_params=_params(("arbitrary",)),
    )(dx1, ya, yb, gts, oa, ob, l0, l1, l2, lb, sink_row, w_out, w_a, w_b, w_g)


def _attn_bwd(name, pv, dov, lsev, cv, cosv, sinv, swa, tq, dep=None):
    d, ls = pv.shape[0], pv.shape[1]
    n, nsb = ls // tq, tq // BAND
    pairs = _attn_layout(swa)
    ncol = 1 if swa else 2
    ow = 128 * len(pairs)

    def body(cur_ref, tail_ref, do_ref, lse_ref, c_ref, cos_ref, sin_ref, out_ref, acc_ref, carry_ref):
        i = pl.program_id(2)
        acc_ref[...] = jnp.zeros_like(acc_ref)

        @pl.when(i < n)
        def _():
            qk_a, v_a = _head_a_masks(BAND)
            for s in range(nsb):
                mask = _band_mask(i, s)
                mask2 = jnp.concatenate([mask, mask], axis=0)
                rows = slice(s * BAND, (s + 1) * BAND)
                krows = slice(s * BAND, (s + 2) * BAND)
                for j, (qo, ko, vo) in enumerate(pairs):
                    kk = _kv_rows(cur_ref, tail_ref, s, ko)
                    vv = _kv_rows(cur_ref, tail_ref, s, vo)
                    q2 = _stack_heads(cur_ref[rows, qo:qo + 128], qk_a)
                    do2 = _stack_heads(do_ref[rows, j * 128:(j + 1) * 128], v_a)
                    col2 = lambda ref: jnp.concatenate([ref[rows, 2 * j:2 * j + 1], ref[rows, 2 * j + 1:2 * j + 2]], axis=0)
                    sc = _dot_nt(q2, kk)
                    p = jnp.exp(jnp.where(mask2, sc, -jnp.inf) - col2(lse_ref))
                    dp = _dot_nt(do2, vv)
                    ds = (p * (dp - col2(c_ref))).astype(BF16)
                    dq2 = _dot(ds, kk)
                    acc_ref[BAND + s * BAND:BAND + (s + 1) * BAND, qo:qo + 128] += jnp.where(qk_a, dq2[:BAND], dq2[BAND:])
                    acc_ref[krows, ko:ko + 128] += _dot_tn(ds, q2)
                    acc_ref[krows, vo:vo + 128] += _dot_tn(p.astype(BF16), do2)

        @pl.when(i >= 1)
        def _():
            if tq > BAND:
                fin = jnp.concatenate([carry_ref[0:tq - BAND, :], carry_ref[tq - BAND:, :] + acc_ref[0:BAND, :]], axis=0)
            else:
                fin = carry_ref[...] + acc_ref[0:BAND, :]
            out_ref[...] = _rope(fin, cos_ref[...], sin_ref[...], swa, -1).astype(BF16)

        carry_ref[...] = acc_ref[BAND:, :]

    qi = lambda i: jnp.minimum(i, n - 1)
    pi = lambda i: jnp.maximum(i - 1, 0)
    blk = lambda rows, w, row_of: pl.BlockSpec((None, rows, w), lambda r, cb, i: (r, row_of(i), cb))
    return pl.pallas_call(
        _with_dep(body, 7, dep), name=name, grid=(d, ncol, n + 1),
        in_specs=[blk(tq, PBLK, qi), blk(BAND, PBLK, lambda i: jnp.maximum(qi(i) * nsb - 1, 0)),
                  blk(tq, ow, qi), blk(tq, 128, qi), blk(tq, 128, qi),
                  pl.BlockSpec((None, tq, 128), lambda r, cb, i: (r, pi(i), 0)),
                  pl.BlockSpec((None, tq, 128), lambda r, cb, i: (r, pi(i), 0))] + _dep_spec(dep),
        out_specs=blk(tq, PBLK, pi),
        out_shape=jax.ShapeDtypeStruct((d, ls, ncol * PBLK), BF16),
        scratch_shapes=[pltpu.VMEM((tq + BAND, PBLK), F32), pltpu.VMEM((tq, PBLK), F32)],
        compiler_params=_params(("arbitrary", "arbitrary", "arbitrary")),
    )(pv, pv, dov, lsev, cv, cosv, sinv, *_dep_arg(dep))


def _dx(dp0, dp1, dp2, dpb, w_p, dh_part, dx1, x, g_mix, tm, dep=None):
    t = x.shape[0]
    gw = 2 * PBLK

    def body(dp0_ref, dp1_ref, dp2_ref, dpb_ref, w_ref, dhp_ref, dx1_ref, x_ref, g_ref, gx_ref, dg_ref,
             dpt_ref, scr_ref):
        @pl.when(pl.program_id(0) == 0)
        def _():
            dg_ref[...] = jnp.zeros_like(dg_ref)

        dpt_ref[:, 0:gw] = dp0_ref[...]
        dpt_ref[:, gw:2 * gw] = _interleave(dp1_ref, scr_ref).astype(BF16)
        dpt_ref[:, 2 * gw:3 * gw] = _interleave(dp2_ref, scr_ref).astype(BF16)
        dpt_ref[:, 3 * gw:] = dpb_ref[...]
        dh = _dot_nt(dpt_ref[...], w_ref[...]) + dhp_ref[...]
        xt = x_ref[...]
        dx, dg = _rms_bwd(dh, xt, _rms(xt), g_ref[...])
        gx_ref[...] = dx1_ref[...] + dx
        dg_ref[...] += dg

    row = lambda w: pl.BlockSpec((tm, w), lambda i: (i, 0))
    full = lambda a, b: pl.BlockSpec((a, b), lambda i: (0, 0))
    return pl.pallas_call(
        _with_dep(body, 9, dep), name="dx", grid=(t // tm,),
        in_specs=[row(gw), _res_spec(dp1, tm), _res_spec(dp2, tm), row(PBLK), full(D_MODEL, P_WIDTH),
                  row(D_MODEL), row(D_MODEL), row(D_MODEL), full(1, D_MODEL)] + _dep_spec(dep),
        out_specs=[row(D_MODEL), full(1, D_MODEL)],
        out_shape=[jax.ShapeDtypeStruct((t, D_MODEL), F32), jax.ShapeDtypeStruct((1, D_MODEL), F32)],
        scratch_shapes=[pltpu.VMEM((tm, P_WIDTH), BF16), _lane_scratch(tm, gw)],
        compiler_params=_params(("arbitrary",)),
    )(dp0, dp1, dp2, dpb, w_p, dh_part, dx1, x, g_mix, *_dep_arg(dep))


MESH = pl.DeviceIdType.MESH
HBM_SPEC = pl.BlockSpec(memory_space=pltpu.HBM)
VMEM_SPEC = pl.BlockSpec(memory_space=pltpu.VMEM)


def _all_gather(xp):
    def body(x_ref, out_ref, send_sems, recv_sems, local_sem):
        x, y, c = lax.axis_index("x"), lax.axis_index("y"), lax.axis_index("c")
        me, sibling = (x, y, c), (x, y, 1 - c)
        chips = [(1 - x, y), (x, 1 - y), (1 - x, 1 - y)]

        def rows(px, py, pc):
            return out_ref.at[4 * px + 2 * py + pc]

        def copy(k, block, to, src=None):
            return pltpu.make_async_remote_copy(
                src_ref=rows(*block) if src is None else src, dst_ref=rows(*block),
                send_sem=send_sems.at[k], recv_sem=recv_sems.at[k], device_id=to, device_id_type=MESH)

        mine = pltpu.make_async_copy(x_ref, rows(*me), local_sem)
        mine.start()
        first = [copy(0, me, sibling, src=x_ref)]
        first += [copy(1 + j, me, (*chip, c), src=x_ref) for j, chip in enumerate(chips)]
        for cp in first:
            cp.start()
        passed = [copy(4 + j, (*chip, c), sibling) for j, chip in enumerate(chips)]
        for j, chip in enumerate(chips):
            copy(1 + j, (*chip, c), me).wait_recv()
            passed[j].start()
        copy(0, sibling, me).wait_recv()
        for j, chip in enumerate(chips):
            copy(4 + j, (*chip, 1 - c), me).wait_recv()
        for cp in first + passed:
            cp.wait_send()
        mine.wait()

    return pl.pallas_call(
        body, name="all_gather",
        out_shape=jax.ShapeDtypeStruct((N_DEV,) + xp.shape, xp.dtype),
        in_specs=[HBM_SPEC], out_specs=HBM_SPEC,
        scratch_shapes=[pltpu.SemaphoreType.DMA((7,)), pltpu.SemaphoreType.DMA((7,)), pltpu.SemaphoreType.DMA],
    )(xp)


def _peers():
    x, y, c = lax.axis_index("x"), lax.axis_index("y"), lax.axis_index("c")
    out = []
    for k in range(1, N_DEV):
        px = 1 - x if k & 4 else x
        py = 1 - y if k & 2 else y
        pc = 1 - c if k & 1 else c
        out.append((k, (px, py, pc), 4 * px + 2 * py + pc))
    return out


def _my_index():
    return 4 * lax.axis_index("x") + 2 * lax.axis_index("y") + lax.axis_index("c")


def _exchange_small(sp):
    def body(s_ref, srecv_ref, send_sems, recv_sems):
        me_idx = _my_index()
        srecv_ref[pl.ds(me_idx, 1)] = s_ref[...][None]
        copies = [pltpu.make_async_remote_copy(
            src_ref=s_ref, dst_ref=srecv_ref.at[me_idx], send_sem=send_sems.at[k - 1], recv_sem=recv_sems.at[k - 1],
            device_id=peer, device_id_type=MESH) for k, peer, _ in _peers()]
        for cp in copies:
            cp.start()
        for cp in copies:
            cp.wait_recv()
        for cp in copies:
            cp.wait_send()

    return pl.pallas_call(
        body, name="small_exchange",
        out_shape=jax.ShapeDtypeStruct((N_DEV, 8, LANES), F32),
        in_specs=[VMEM_SPEC], out_specs=VMEM_SPEC,
        scratch_shapes=[pltpu.SemaphoreType.DMA((7,)), pltpu.SemaphoreType.DMA((7,))],
    )(sp)


SEM_SPEC = pl.BlockSpec(memory_space=pltpu.SEMAPHORE)
ANY_SPEC = pl.BlockSpec(memory_space=pl.ANY)
_SPLIT_PARAMS = pltpu.CompilerParams(has_side_effects=pltpu.SideEffectType.DATAFLOW_SIDE_EFFECTING)


def _split_copies(gather, buf_refs, send_sems, recv_sems):
    me_idx = _my_index()
    n = len(buf_refs) if gather else len(buf_refs) // 2
    out = []
    for a in range(n):
        for k, peer, peer_idx in _peers():
            if gather:
                src, dst = buf_refs[a].at[me_idx], buf_refs[a].at[me_idx]
            else:
                src, dst = buf_refs[a].at[peer_idx], buf_refs[n + a].at[k - 1]
            out.append(pltpu.make_async_remote_copy(
                src_ref=src, dst_ref=dst, send_sem=send_sems.at[7 * a + k - 1], recv_sem=recv_sems.at[7 * a + k - 1],
                device_id=peer, device_id_type=MESH))
    return out


def _split_start(name, gather, bufs):
    nb = len(bufs)
    nsem = 7 * (nb if gather else nb // 2)

    def body(*refs):
        send_sems, recv_sems = refs[nb], refs[nb + 1]
        for cp in _split_copies(gather, refs[:nb], send_sems, recv_sems):
            cp.start()
        token = refs[-1]
        token[...] = jnp.zeros_like(token)

    hbm = [pltpu.HBM(a.shape, a.dtype) for a in bufs]
    return pl.pallas_call(
        body, name=name,
        out_shape=(pltpu.SemaphoreType.DMA((nsem,)), pltpu.SemaphoreType.DMA((nsem,)), *hbm,
                   jax.ShapeDtypeStruct((8, 128), F32)),
        in_specs=(HBM_SPEC,) * nb, out_specs=(SEM_SPEC, SEM_SPEC) + (HBM_SPEC,) * nb + (VMEM_SPEC,),
        input_output_aliases={i: 2 + i for i in range(nb)}, compiler_params=_SPLIT_PARAMS,
    )(*[pltpu.with_memory_space_constraint(a, pltpu.HBM) for a in bufs])


def _split_wait(name, gather, started, after):
    send_sems, recv_sems, bufs = started[0], started[1], started[2:-1]
    nb = len(bufs)

    def body(*refs):
        for cp in _split_copies(gather, refs[:nb], refs[nb], refs[nb + 1]):
            cp.wait_send()
            cp.wait_recv()

    return pl.pallas_call(
        body, name=name, out_shape=tuple(pltpu.HBM(a.shape, a.dtype) for a in bufs),
        in_specs=(HBM_SPEC,) * nb + (SEM_SPEC, SEM_SPEC, ANY_SPEC), out_specs=(HBM_SPEC,) * nb,
        input_output_aliases={i: i for i in range(nb)}, compiler_params=_SPLIT_PARAMS,
    )(*bufs, send_sems, recv_sems, after)


def _adam_update(g, w, m, v):
    nm = ADAM_B1 * m + (1.0 - ADAM_B1) * g
    nv = ADAM_B2 * v + (1.0 - ADAM_B2) * (g * g)
    m_hat = nm / (1.0 - ADAM_B1 ** ADAM_STEP)
    v_hat = nv / (1.0 - ADAM_B2 ** ADAM_STEP)
    return -ADAM_LR * (m_hat / (jnp.sqrt(v_hat) + ADAM_EPS) + ADAM_WD * w), nm, nv


def _adamw(name, me, sent, got, w, m, v, tr):
    r, c = w.shape

    def body(me_ref, own_ref, got_ref, w_ref, m_ref, v_ref, g_ref, d_ref, nm_ref, nv_ref):
        g = own_ref[...].astype(F32)
        for k in range(N_DEV - 1):
            g = g + got_ref[k].astype(F32)
        g_ref[...] = g
        d_ref[...], nm_ref[...], nv_ref[...] = _adam_update(g, w_ref[...], m_ref[...], v_ref[...])

    blk = pl.BlockSpec((tr, c), lambda i, me_ref: (i, 0))
    return pl.pallas_call(
        body, name=name,
        grid_spec=pltpu.PrefetchScalarGridSpec(
            num_scalar_prefetch=1, grid=(r // tr,),
            in_specs=[pl.BlockSpec((None, tr, c), lambda i, me_ref: (me_ref[0], i, 0)),
                      pl.BlockSpec((N_DEV - 1, tr, c), lambda i, me_ref: (0, i, 0)), blk, blk, blk],
            out_specs=[blk] * 4),
        out_shape=[jax.ShapeDtypeStruct((r, c), F32)] * 4,
        compiler_params=_params(("arbitrary",)),
    )(me, sent, got, w, m, v)


def _adamw_small(srecv, ws, ms, vs):
    nv_ = len(ws)

    def body(*refs):
        s_ref = refs[0]
        ins, outs = refs[1:1 + 3 * nv_], refs[1 + 3 * nv_:]
        g_all = s_ref[0]
        for k in range(1, N_DEV):
            g_all = g_all + s_ref[k]
        for i in range(nv_):
            n = ins[i].shape[1]
            g = g_all[i:i + 1, :n]
            d, nm, nv = _adam_update(g, ins[i][...], ins[nv_ + i][...], ins[2 * nv_ + i][...])
            outs[i][...], outs[nv_ + i][...], outs[2 * nv_ + i][...], outs[3 * nv_ + i][...] = g, d, nm, nv

    shapes = [jax.ShapeDtypeStruct(a.shape, F32) for a in ws]
    res = pl.pallas_call(body, name="adamw_small", out_shape=shapes * 4, compiler_params=_params())(
        srecv, *ws, *ms, *vs)
    return [res[k * nv_:(k + 1) * nv_] for k in range(4)]


def _cols_from_shards(a):
    return jnp.swapaxes(a, 0, 1).reshape(a.shape[1], N_DEV * a.shape[2])


def _shards_from_cols(a):
    return jnp.swapaxes(a.reshape(a.shape[0], N_DEV, a.shape[1] // N_DEV), 0, 1)


def _shards_from_rows(a):
    return a.reshape(N_DEV, a.shape[0] // N_DEV, a.shape[1])


def _pair_lanes(a):
    lead = a.shape[:-1]
    return a.reshape(lead + (2, 2, HEAD_DIM // 2)).swapaxes(-3, -2).reshape(lead + (128,))


def _split_w_in(w_in):
    rows = w_in.shape[0]
    dil = w_in[:, :3 * DIL_WIDTH].reshape(rows, 3, 3, 4, 128)
    dil = jnp.concatenate([_pair_lanes(dil[:, :2]), dil[:, 2:]], axis=1)
    dil = dil.transpose(0, 2, 3, 1, 4).reshape(rows, 3 * DIL_WIDTH)
    o = 3 * DIL_WIDTH
    qb = w_in[:, o:o + SWA_Q_WIDTH].reshape(rows, 2, 4, HEAD_DIM).transpose(0, 2, 1, 3).reshape(rows, 4, 128)
    qb = _pair_lanes(qb).reshape(rows, SWA_Q_WIDTH)
    kb = _pair_lanes(w_in[:, o + SWA_Q_WIDTH:o + SWA_Q_WIDTH + SWA_KV_WIDTH])
    vb = w_in[:, o + SWA_Q_WIDTH + SWA_KV_WIDTH:P_WIDTH]
    return jnp.concatenate([dil, qb, kb, vb], axis=1), w_in[:, P_WIDTH:]


def _merge_w_in(dw_p, dw_g):
    rows = dw_p.shape[0]
    dil = dw_p[:, :3 * DIL_WIDTH].reshape(rows, 3, 4, 3, 128).transpose(0, 3, 1, 2, 4)
    dil = jnp.concatenate([_pair_lanes(dil[:, :2]), dil[:, 2:]], axis=1).reshape(rows, 3 * DIL_WIDTH)
    o = 3 * DIL_WIDTH
    qb = _pair_lanes(dw_p[:, o:o + SWA_Q_WIDTH].reshape(rows, 4, 128))
    qb = qb.reshape(rows, 4, 2, HEAD_DIM).transpose(0, 2, 1, 3).reshape(rows, SWA_Q_WIDTH)
    kb = _pair_lanes(dw_p[:, o + SWA_Q_WIDTH:o + SWA_Q_WIDTH + SWA_KV_WIDTH])
    vb = dw_p[:, o + SWA_Q_WIDTH + SWA_KV_WIDTH:]
    return jnp.concatenate([dil, qb, kb, vb, dw_g], axis=1)


def _swa_rows(w_b):
    return w_b.reshape(2, 4, HEAD_DIM, -1).transpose(1, 0, 2, 3).reshape(SWA_Q_WIDTH, -1)


def _swa_rows_inv(dw_b):
    return dw_b.reshape(4, 2, HEAD_DIM, -1).transpose(1, 0, 2, 3).reshape(SWA_Q_WIDTH, -1)


def _rope_tables(pos):
    half = HEAD_DIM // 2
    inv = ROPE_THETA ** (-jnp.arange(half, dtype=F32) / half)
    ang = pos.astype(F32)[:, None] * inv
    c, s = jnp.cos(ang), jnp.sin(ang)
    return jnp.concatenate([c, c, c, c], axis=1), jnp.concatenate([-s, -s, s, s], axis=1)


def _local_step(x, mem, pos, target, w_in, dep, rest_weights, on_grads, g_mix, g_cross, g_mem, g_mlp, g_final, sink):
    t = x.shape[0]
    tm = min(512, t)
    tq = min(256, t // 16)
    tw = min(2048, t)
    w_p, w_g = _split_w_in(w_in)
    cos, sin = _rope_tables(pos)
    sink_row = jnp.pad(sink.reshape(2, 4).T.reshape(1, 8), ((0, 0), (0, 120)))
    tabs = [(cos[None], sin[None])]
    for _, d in DIL_GROUPS[1:]:
        c_d, s_d = _rope_tables(pos.reshape(t // d, d).T.reshape(-1))
        tabs.append((c_d.reshape(d, t // d, 128), s_d.reshape(d, t // d, 128)))
    tabs.append(tabs[0])

    h, h1, h2, p0, p1, p2, pb = _inproj(x, g_mix, w_p, [(cos, sin), tabs[1], tabs[2]], tm, dep)
    ps = [p0[None], p1, p2, pb[None]]
    outs, lses = [], []
    for gi, pv in enumerate(ps):
        o, l = _attn_fwd(f"attn_fwd{gi}", pv, gi == 3, sink_row[0, :8], tq)
        outs.append(o)
        lses.append(l)
    o0, l0, ob, lb = outs[0][0], lses[0][0], outs[3][0], lses[3][0]
    wts = rest_weights(lb)
    w_b = _swa_rows(wts["w_branch_b"])
    tf = 2048
    gts = _gates(h, w_g, wts["b_gate"].reshape(1, GATE_WIDTH), tm, 1024)
    oa, ya, yb, merged, x1, hc = _mix(o0, outs[1], outs[2], l0, lses[1], lses[2], ob, gts, x,
                                      wts["w_branch_a"], w_b, wts["w_out"], g_cross, tm)
    mn, kv = _memkv(mem, g_mem, wts["w_ckv"])
    q, o, x2, hm = _cross(hc, x1, kv, wts["w_cq"], wts["w_co"], g_mlp, tm)
    a, dx3, loss, dg_final = _mlp(hm, x2, wts["w_1"], wts["w_2"], g_final.reshape(1, D_MODEL), target, tm, tf)

    grads = {}
    dz, dx2, dg_mlp = _mlp_bwd(dx3, a, wts["w_1"], wts["w_2"], x2, g_mlp, tm, tf)
    grads["w_2"] = _shards_from_rows(_wgrad("dw_2", a, dx3, 1024, 1024, tw, square=True))
    grads["w_1"] = _wgrad("dw_1", hm, dz, 1024, 1024, tw, col_shards=True)
    dep = on_grads(GROUP_A, grads)
    dq, dx1, dkv, dg_cross = _cross_bwd(dx2, x1, q, kv, wts["w_cq"], wts["w_co"], g_cross, tm, dep)
    grads["w_co"] = _shards_from_rows(_wgrad("dw_co", o, dx2, 1024, 1024, tw))
    grads["w_cq"] = _shards_from_rows(_wgrad("dw_cq", hc, dq, 1024, 1024, tw))
    grads["w_ckv"], dg_mem = _memkv_bwd(dkv, mn, mem, wts["w_ckv"], g_mem)
    (dgt, dh_part, dya, dyb, do0, do1, do2, c0, c1, c2, dob, cb, db_gate, dsink) = _mix_bwd(
        dx1, ya, yb, gts, oa, ob, l0, lses[1], lses[2], lb, sink_row,
        wts["w_out"], wts["w_branch_a"], w_b, w_g, min(256, tm))
    grads["w_out"] = _shards_from_rows(_wgrad("dw_out", merged, dx1, 1024, 1024, tw))
    grads["w_branch_a"] = _shards_from_cols(_wgrad("dw_a", oa, dya, 512, 1024, tw))
    grads["w_branch_b"] = _shards_from_cols(_swa_rows_inv(_wgrad("dw_b", ob, dyb, 512, 1024, tw)))
    grads["b_gate"] = _shards_from_cols(db_gate.reshape(2, D_MODEL)).astype(BF16)
    dep = on_grads(GROUP_B, grads)
    dw_g = _wgrad("dw_g", h, dgt, 1024, 1024, tw)
    dps = []
    for gi, (pv, do_g, c_g) in enumerate(zip(ps, (do0[None], do1, do2, dob[None]), (c0[None], c1, c2, cb[None]))):
        dps.append(_attn_bwd(f"attn_bwd{gi}", pv, do_g, lses[gi], c_g, tabs[gi][0], tabs[gi][1], gi == 3, tq,
                             dep if gi == 0 else None))
    dw_p = jnp.concatenate(
        [_wgrad(f"dw_p{gi}", hh.reshape(t, D_MODEL), dpg.reshape(t, -1), 1024, PBLK, tw)
         for gi, (hh, dpg) in enumerate(zip((h, h1, h2, h), dps))], axis=1)
    grads["w_in"] = _shards_from_cols(_merge_w_in(dw_p, dw_g))
    dep = on_grads(GROUP_C, grads)
    grad_x, dg_mix = _dx(dps[0][0], dps[1], dps[2], dps[3][0], w_p, dh_part, dx1, x, g_mix, min(256, tm), dep)
    dsink_heads = dsink[0, :8].reshape(4, 2).T.reshape(8)
    small = {"g_mix": dg_mix[0], "g_cross": dg_cross[0], "g_mem": dg_mem[0], "g_mlp": dg_mlp[0],
             "g_final": dg_final[0], "sink": dsink_heads}
    return loss[0, 0], grad_x, small


def kernel(x, mem, positions, g_mix, w_in, b_gate, sink, w_branch_a, w_branch_b, w_out, g_cross, g_mem, w_cq, w_ckv, w_co, g_mlp, w_1, w_2, g_final, loss_target, m_g_mix, m_w_in, m_b_gate, m_sink, m_w_branch_a, m_w_branch_b, m_w_out, m_g_cross, m_g_mem, m_w_cq, m_w_ckv, m_w_co, m_g_mlp, m_w_1, m_w_2, m_g_final, v_g_mix, v_w_in, v_b_gate, v_sink, v_w_branch_a, v_w_branch_b, v_w_out, v_g_cross, v_g_mem, v_w_cq, v_w_ckv, v_w_co, v_g_mlp, v_w_1, v_w_2, v_g_final):
    local = dict(locals())
    shard = {n: local[n][0] for n in GROUP_A + GROUP_B + GROUP_C}
    me = _my_index()
    me_arr = me.reshape(1).astype(jnp.int32)
    tags = {GROUP_A: "a", GROUP_B: "b", GROUP_C: "c"}

    w_in_full = _cols_from_shards(_all_gather(shard["w_in"].astype(BF16)))
    rest = GROUP_A + GROUP_B

    def placed(name):
        a = shard[name] if name == "b_gate" else shard[name].astype(BF16)
        return lax.dynamic_update_slice(lax.empty((N_DEV,) + a.shape, a.dtype), a[None], (me, 0, 0))

    gather = _split_start("gather_start", True, [placed(n) for n in rest])

    def rest_weights(after):
        full = {}
        for name, a in zip(rest, _split_wait("gather_wait", True, gather, after)):
            if name in ("w_1", "w_ckv"):
                full[name] = a
            elif name in _COL_SHARDED:
                full[name] = _cols_from_shards(a)
            else:
                full[name] = a.reshape(N_DEV * a.shape[1], a.shape[2])
        return full

    scatters = {}

    def on_grads(names, grads):
        srcs = [grads[n] for n in names]
        lands = [lax.empty((N_DEV - 1,) + g.shape[1:], g.dtype) for g in srcs]
        scatters[names] = _split_start("scatter_start_" + tags[names], False, srcs + lands)
        return scatters[names][-1]

    loss, grad_x, small = _local_step(
        x[0], mem[0], positions[0], loss_target[0], w_in_full, gather[-1], rest_weights, on_grads,
        g_mix, g_cross, g_mem, g_mlp, g_final, sink[0])

    after, updated = grad_x, {}
    for names in (GROUP_A, GROUP_B, GROUP_C):
        bufs = _split_wait("scatter_wait_" + tags[names], False, scatters[names], after)
        for i, name in enumerate(names):
            outs = _adamw("adamw_" + name, me_arr, bufs[i], bufs[len(names) + i], shard[name],
                          local["m_" + name][0], local["v_" + name][0], ADAM_ROWS[name])
            updated[name] = [a[None] for a in outs]
            after = outs[3]

    sp = jnp.stack([small[n] if n != "sink" else jnp.pad(small[n], (0, LANES - 8)) for n in SMALL]
                   + [jnp.zeros((LANES,), F32)] * 2)
    flat = lambda prefix: [local[prefix + n].reshape(1, -1) for n in SMALL]
    outs = _adamw_small(_exchange_small(sp), flat(""), flat("m_"), flat("v_"))
    for i, name in enumerate(SMALL):
        updated[name] = [outs[which][i].reshape(local[name].shape) for which in range(4)]

    order = ["g_mix", "w_in", "b_gate", "sink", "w_branch_a", "w_branch_b", "w_out", "g_cross", "g_mem", "w_cq",
             "w_ckv", "w_co", "g_mlp", "w_1", "w_2", "g_final"]
    res = [lax.psum(loss, ("x", "y", "c")), grad_x[None]]
    for which in range(4):
        res += [updated[n][which] for n in order]
    return tuple(res)
```

```python
import functools
import math

import jax
import jax.numpy as jnp
from jax import lax
from jax.experimental import pallas as pl
from jax.experimental.pallas import tpu as pltpu

F32 = jnp.float32
BF16 = jnp.bfloat16

D_MODEL = 1024
HEAD_DIM = 64
DIL_GROUPS = ((128, 1), (512, 4), (2048, 16))
ROPE_THETA = 10000.0
X_HEADS = 4
X_HEAD_DIM = D_MODEL // X_HEADS
D_FF = 4 * D_MODEL
EPS = 1e-6
DIL_WIDTH = 1536
SWA_Q_WIDTH = 512
SWA_KV_WIDTH = 128
P_WIDTH = 3 * DIL_WIDTH + SWA_Q_WIDTH + 2 * SWA_KV_WIDTH
GATE_WIDTH = 2 * D_MODEL
IN_WIDTH = P_WIDTH + GATE_WIDTH
BAND = 128
PBLK = 768
Q_SCALE = HEAD_DIM ** -0.5
X_SCALE = X_HEAD_DIM ** -0.5

ADAM_LR = 0.001
ADAM_B1 = 0.9
ADAM_B2 = 0.999
ADAM_EPS = 1e-08
ADAM_WD = 0.01
ADAM_STEP = 10

N_DEV = 8
LANES = 1024
VMEM_LIMIT = 52 * 1024 * 1024

NT = (((1,), (1,)), ((), ()))
TN = (((0,), (0,)), ((), ()))

GROUP_A = ("w_1", "w_2")
GROUP_B = ("w_branch_a", "w_branch_b", "w_out", "w_cq", "w_ckv", "w_co", "b_gate")
GROUP_C = ("w_in",)
_COL_SHARDED = ("w_in", "w_branch_a", "w_branch_b", "w_ckv", "w_1", "b_gate")
ADAM_ROWS = {"w_in": 256, "w_branch_a": 512, "w_branch_b": 512, "w_out": 128, "w_cq": 128, "w_ckv": 512,
             "w_co": 128, "w_1": 256, "w_2": 256, "b_gate": 2}
SMALL = ("g_mix", "g_cross", "g_mem", "g_mlp", "g_final", "sink")


def _params(sem=None):
    return pltpu.CompilerParams(dimension_semantics=sem, vmem_limit_bytes=VMEM_LIMIT)


def _dot(a, b):
    return jnp.dot(a, b, preferred_element_type=F32)


def _dot_nt(a, b):
    return lax.dot_general(a, b, NT, preferred_element_type=F32)


def _dot_tn(a, b):
    return lax.dot_general(a, b, TN, preferred_element_type=F32)


def _rms(xt):
    return lax.rsqrt(jnp.mean(xt * xt, axis=-1, keepdims=True) + EPS)


def _rms_bwd(dh, xt, r, g):
    xn = xt * r
    dxn = dh * g
    dx = r * (dxn - xn * jnp.mean(dxn * xn, axis=-1, keepdims=True))
    return dx, jnp.sum(dh * xn, axis=0, keepdims=True)


def _rope(x, c, s, swa, sign):
    kinds = "qqqqkv" if swa else "qkvqkv"
    cq, sq = c * Q_SCALE, s * (sign * Q_SCALE)
    sk = s * sign if sign != 1 else s
    out = []
    for ci, kind in enumerate(kinds):
        xc = x[:, ci * 128:(ci + 1) * 128]
        if kind == "v":
            out.append(xc)
        elif kind == "q":
            out.append(xc * cq + pltpu.roll(xc, 64, 1) * sq)
        else:
            out.append(xc * c + pltpu.roll(xc, 64, 1) * sk)
    return jnp.concatenate(out, axis=1)


def _lane_scratch(rows, w):
    return pltpu.VMEM((w // 128, rows, 128), F32)


def _deinterleave(val, scr_ref, dst_ref, dtype):
    d, n = dst_ref.shape[0], dst_ref.shape[1]
    nc = val.shape[1] // 128
    for c in range(nc):
        scr_ref[c] = val[:, c * 128:(c + 1) * 128]
    for r in range(d):
        rows = [scr_ref.at[c][pl.ds(r, n, stride=d), :] for c in range(nc)]
        dst_ref[r] = jnp.concatenate(rows, axis=1).astype(dtype)


def _res_spec(a, tm):
    d, w = a.shape[0], a.shape[2]
    return pl.BlockSpec((d, tm // d, w), lambda i: (0, i, 0))


def _interleave(src_ref, scr_ref):
    d, n = src_ref.shape[0], src_ref.shape[1]
    nc = src_ref.shape[2] // 128
    for r in range(d):
        v = src_ref[r].astype(F32)
        for c in range(nc):
            scr_ref.at[c][pl.ds(r, n, stride=d), :] = v[:, c * 128:(c + 1) * 128]
    return jnp.concatenate([scr_ref[c] for c in range(nc)], axis=1)


def _with_dep(body, n_in, dep):
    if dep is None:
        return body
    return lambda *refs: body(*refs[:n_in], *refs[n_in + 1:])


def _dep_spec(dep):
    return [] if dep is None else [pl.BlockSpec(memory_space=pl.ANY)]


def _dep_arg(dep):
    return [] if dep is None else [dep]


def _inproj(x, g, w_p, tabs, tm, dep=None):
    t = x.shape[0]
    gw = 2 * PBLK
    (cos, sin), (cos1, sin1), (cos2, sin2) = tabs[0], tabs[1], tabs[2]

    def body(x_ref, g_ref, w_ref, c_ref, s_ref, c1_ref, s1_ref, c2_ref, s2_ref,
             h_ref, h1_ref, h2_ref, p0_ref, p1_ref, p2_ref, pb_ref, hf_ref):
        xt = x_ref[...]
        hf = xt * _rms(xt) * g_ref[...]
        h_ref[...] = hf.astype(BF16)
        _deinterleave(hf, hf_ref, h1_ref, BF16)
        _deinterleave(hf, hf_ref, h2_ref, BF16)
        rows = lambda ref: ref[...].reshape(tm, ref.shape[-1])
        groups = ((h_ref, c_ref, s_ref, p0_ref), (h1_ref, c1_ref, s1_ref, p1_ref), (h2_ref, c2_ref, s2_ref, p2_ref))
        for gi, (lhs_ref, cc_ref, ss_ref, out_ref) in enumerate(groups):
            lhs, cc, ss = rows(lhs_ref), rows(cc_ref), rows(ss_ref)
            for half in range(2):
                col = gi * gw + half * PBLK
                val = _rope(_dot(lhs, w_ref[:, col:col + PBLK]), cc, ss, False, 1).astype(BF16)
                if out_ref.ndim == 3:
                    out_ref[:, :, half * PBLK:(half + 1) * PBLK] = val.reshape(out_ref.shape[:2] + (PBLK,))
                else:
                    out_ref[:, half * PBLK:(half + 1) * PBLK] = val
        pb_ref[...] = _rope(_dot(h_ref[...], w_ref[:, 3 * gw:]), c_ref[...], s_ref[...], True, 1).astype(BF16)

    d1, d2 = DIL_GROUPS[1][1], DIL_GROUPS[2][1]
    row = lambda w: pl.BlockSpec((tm, w), lambda i: (i, 0))
    res = lambda d, w: pl.BlockSpec((d, tm // d, w), lambda i: (0, i, 0))
    sds = jax.ShapeDtypeStruct
    return pl.pallas_call(
        _with_dep(body, 9, dep), name="inproj", grid=(t // tm,),
        in_specs=[row(D_MODEL), pl.BlockSpec((1, D_MODEL), lambda i: (0, 0)),
                  pl.BlockSpec((D_MODEL, P_WIDTH), lambda i: (0, 0), pipeline_mode=pl.Buffered(1)),
                  row(128), row(128), res(d1, 128), res(d1, 128), res(d2, 128), res(d2, 128)] + _dep_spec(dep),
        out_specs=[row(D_MODEL), res(d1, D_MODEL), res(d2, D_MODEL), row(gw), res(d1, gw), res(d2, gw), row(PBLK)],
        out_shape=[sds((t, D_MODEL), BF16), sds((d1, t // d1, D_MODEL), BF16), sds((d2, t // d2, D_MODEL), BF16),
                   sds((t, gw), BF16), sds((d1, t // d1, gw), BF16), sds((d2, t // d2, gw), BF16),
                   sds((t, PBLK), BF16)],
        scratch_shapes=[_lane_scratch(tm, D_MODEL)],
        compiler_params=_params(("arbitrary",)),
    )(x, g, w_p, cos, sin, cos1, sin1, cos2, sin2, *_dep_arg(dep))


def _gates(h, w_g, b, tm, tn):
    t = h.shape[0]

    def body(h_ref, w_ref, b_ref, o_ref):
        z = _dot(h_ref[...], w_ref[...]) + b_ref[...]
        o_ref[...] = jax.nn.sigmoid(z).astype(BF16)

    return pl.pallas_call(
        body, name="gates", grid=(t // tm, GATE_WIDTH // tn),
        in_specs=[pl.BlockSpec((tm, D_MODEL), lambda i, j: (i, 0)),
                  pl.BlockSpec((D_MODEL, tn), lambda i, j: (0, j)),
                  pl.BlockSpec((1, tn), lambda i, j: (0, j))],
        out_specs=pl.BlockSpec((tm, tn), lambda i, j: (i, j)),
        out_shape=jax.ShapeDtypeStruct((t, GATE_WIDTH), BF16),
        compiler_params=_params(("arbitrary", "arbitrary")),
    )(h, w_g, b)


def _band_mask(i, s):
    row = lax.broadcasted_iota(jnp.int32, (BAND, 2 * BAND), 0)
    col = lax.broadcasted_iota(jnp.int32, (BAND, 2 * BAND), 1)
    band = (col >= row) & (col <= row + BAND)
    if s == 0:
        band = band & ((col >= BAND) | (i > 0))
    return band


def _head_a_masks(rows):
    lane = lax.broadcasted_iota(jnp.int32, (rows, 128), 1)
    return (lane % HEAD_DIM) < HEAD_DIM // 2, lane < HEAD_DIM


def _stack_heads(x, head_a):
    zero = jnp.zeros_like(x)
    return jnp.concatenate([jnp.where(head_a, x, zero), jnp.where(head_a, zero, x)], axis=0)


def _kv_rows(cur_ref, tail_ref, s, off):
    if s == 0:
        return jnp.concatenate([tail_ref[:, off:off + 128], cur_ref[0:BAND, off:off + 128]], axis=0)
    return cur_ref[(s - 1) * BAND:(s + 1) * BAND, off:off + 128]


def _attn_layout(swa):
    if swa:
        return [(128 * j, 512, 640) for j in range(4)]
    return [(0, 128, 256), (384, 512, 640)]


def _attn_fwd(name, pv, swa, sinks, tq):
    d, ls = pv.shape[0], pv.shape[1]
    n, nsb = ls // tq, tq // BAND
    pairs = _attn_layout(swa)
    ncol = 1 if swa else 2
    ow = 128 * len(pairs)

    def body(cur_ref, tail_ref, *rest):
        sink_ref = rest[0] if swa else None
        o_ref, lse_ref = rest[-2:]
        i = pl.program_id(2)
        lane = lax.broadcasted_iota(jnp.int32, (BAND, 128), 1)
        qk_a, v_a = _head_a_masks(BAND)
        first = lax.broadcasted_iota(jnp.int32, (2 * BAND, 1), 0) < BAND
        for s in range(nsb):
            mask = _band_mask(i, s)
            mask2 = jnp.concatenate([mask, mask], axis=0)
            rows = slice(s * BAND, (s + 1) * BAND)
            lse_tile = jnp.zeros((BAND, 128), F32)
            for j, (qo, ko, vo) in enumerate(pairs):
                q = cur_ref[rows, qo:qo + 128]
                kk = _kv_rows(cur_ref, tail_ref, s, ko)
                vv = _kv_rows(cur_ref, tail_ref, s, vo)
                sc = _dot_nt(_stack_heads(q, qk_a), kk)
                sc = jnp.where(mask2, sc, -jnp.inf)
                m = jnp.max(sc, axis=-1, keepdims=True)
                if swa:
                    sk = jnp.where(first, sink_ref[2 * j], sink_ref[2 * j + 1])
                    m = jnp.maximum(m, sk)
                p = jnp.exp(sc - m)
                den = jnp.sum(p, axis=-1, keepdims=True)
                if swa:
                    den = den + jnp.exp(sk - m)
                lse = m + jnp.log(den)
                lse_tile = jnp.where(lane == 2 * j, lse[:BAND], jnp.where(lane == 2 * j + 1, lse[BAND:], lse_tile))
                o2 = _dot((p * (1.0 / den)).astype(BF16), vv)
                o_ref[rows, j * 128:(j + 1) * 128] = jnp.where(v_a, o2[:BAND], o2[BAND:]).astype(BF16)
            lse_ref[rows, :] = lse_tile

    in_specs = [pl.BlockSpec((None, tq, PBLK), lambda r, cb, i: (r, i, cb)),
                pl.BlockSpec((None, BAND, PBLK), lambda r, cb, i: (r, jnp.maximum(i * nsb - 1, 0), cb))]
    args = [pv, pv]
    if swa:
        in_specs.append(pl.BlockSpec(memory_space=pltpu.SMEM))
        args.append(sinks)
    return pl.pallas_call(
        body, name=name, grid=(d, ncol, n),
        in_specs=in_specs,
        out_specs=[pl.BlockSpec((None, tq, ow), lambda r, cb, i: (r, i, cb)),
                   pl.BlockSpec((None, tq, 128), lambda r, cb, i: (r, i, cb))],
        out_shape=[jax.ShapeDtypeStruct((d, ls, 512), BF16), jax.ShapeDtypeStruct((d, ls, 128 * ncol), F32)],
        compiler_params=_params(("arbitrary", "arbitrary", "arbitrary")),
    )(*args)


def _lse_lane(h):
    return (h // 4) * 128 + h % 4


def _head_scale(x, tile, lanes):
    lane = lax.broadcasted_iota(jnp.int32, (x.shape[0], 128), 1)
    lo = lane < HEAD_DIM
    out = []
    for c in range(x.shape[1] // 128):
        a0 = tile[:, lanes[2 * c]:lanes[2 * c] + 1]
        a1 = tile[:, lanes[2 * c + 1]:lanes[2 * c + 1] + 1]
        out.append(x[:, c * 128:(c + 1) * 128] * jnp.where(lo, a0, a1))
    return jnp.concatenate(out, axis=1)


def _head_sums(x, lanes, width):
    lane = lax.broadcasted_iota(jnp.int32, (x.shape[0], width), 1)
    out = jnp.zeros((x.shape[0], width), F32)
    for h in range(x.shape[1] // HEAD_DIM):
        sm = jnp.sum(x[:, h * HEAD_DIM:(h + 1) * HEAD_DIM], axis=-1, keepdims=True)
        out = jnp.where(lane == lanes[h], sm, out)
    return out


def _alphas(l0, l1, l2):
    m = jnp.maximum(jnp.maximum(l0, l1), l2)
    e0, e1, e2 = jnp.exp(l0 - m), jnp.exp(l1 - m), jnp.exp(l2 - m)
    den = e0 + e1 + e2
    return e0 / den, e1 / den, e2 / den


DIL_LANES = [_lse_lane(h) for h in range(8)]
SWA_LANES = list(range(8))


def _mix(o0, o1, o2, l0, l1, l2, ob, gts, x, w_a, w_b, w_out, g_cross, tm):
    t = x.shape[0]

    def body(o0_ref, o1_ref, o2_ref, l0_ref, l1_ref, l2_ref, ob_ref, g_ref, x_ref, wa_ref, wb_ref, wo_ref,
             gc_ref, oa_ref, ya_ref, yb_ref, mg_ref, x1_ref, hc_ref, so_ref, sl_ref):
        a0, a1, a2 = _alphas(l0_ref[...], _interleave(l1_ref, sl_ref), _interleave(l2_ref, sl_ref))
        oa = (_head_scale(o0_ref[...].astype(F32), a0, DIL_LANES)
              + _head_scale(_interleave(o1_ref, so_ref), a1, DIL_LANES)
              + _head_scale(_interleave(o2_ref, so_ref), a2, DIL_LANES))
        oab = oa.astype(BF16)
        oa_ref[...] = oab
        ya = _dot(oab, wa_ref[...])
        yb = _dot(ob_ref[...], wb_ref[...])
        ya_ref[...] = ya.astype(BF16)
        yb_ref[...] = yb.astype(BF16)
        merged = (g_ref[:, :D_MODEL].astype(F32) * ya + g_ref[:, D_MODEL:].astype(F32) * yb).astype(BF16)
        mg_ref[...] = merged
        x1 = x_ref[...] + _dot(merged, wo_ref[...])
        x1_ref[...] = x1
        hc_ref[...] = (x1 * _rms(x1) * gc_ref[...]).astype(BF16)

    row = lambda w: pl.BlockSpec((tm, w), lambda i: (i, 0))
    full = lambda a, b: pl.BlockSpec((a, b), lambda i: (0, 0))
    return pl.pallas_call(
        body, name="mix", grid=(t // tm,),
        in_specs=[row(512), _res_spec(o1, tm), _res_spec(o2, tm), row(256), _res_spec(l1, tm), _res_spec(l2, tm),
                  row(512), row(GATE_WIDTH),
                  row(D_MODEL), full(512, D_MODEL), full(512, D_MODEL), full(D_MODEL, D_MODEL), full(1, D_MODEL)],
        out_specs=[row(512), row(D_MODEL), row(D_MODEL), row(D_MODEL), row(D_MODEL), row(D_MODEL)],
        out_shape=[jax.ShapeDtypeStruct((t, 512), BF16), jax.ShapeDtypeStruct((t, D_MODEL), BF16),
                   jax.ShapeDtypeStruct((t, D_MODEL), BF16), jax.ShapeDtypeStruct((t, D_MODEL), BF16),
                   jax.ShapeDtypeStruct((t, D_MODEL), F32), jax.ShapeDtypeStruct((t, D_MODEL), BF16)],
        scratch_shapes=[_lane_scratch(tm, 512), _lane_scratch(tm, 256)],
        compiler_params=_params(("arbitrary",)),
    )(o0, o1, o2, l0, l1, l2, ob, gts, x, w_a, w_b, w_out, g_cross)


def _memkv(mem, g_mem, w_ckv):
    m = mem.shape[0]
    ws = w_ckv.shape[2]

    def body(mem_ref, g_ref, w_ref, mn_ref, kv_ref):
        xt = mem_ref[...]
        mn = (xt * _rms(xt) * g_ref[...]).astype(BF16)
        mn_ref[...] = mn
        for j in range(N_DEV):
            kv_ref[:, j * ws:(j + 1) * ws] = _dot(mn, w_ref[j]).astype(BF16)

    return pl.pallas_call(
        body, name="memkv",
        out_shape=[jax.ShapeDtypeStruct((m, D_MODEL), BF16), jax.ShapeDtypeStruct((m, 2 * D_MODEL), BF16)],
        compiler_params=_params(),
    )(mem, g_mem, w_ckv)


def _cross_probs(q, kv_ref, h):
    k = kv_ref[:, h * X_HEAD_DIM:(h + 1) * X_HEAD_DIM]
    sc = _dot_nt(q[:, h * X_HEAD_DIM:(h + 1) * X_HEAD_DIM], k)
    m = jnp.max(sc, axis=-1, keepdims=True)
    p = jnp.exp(sc - m)
    return p / jnp.sum(p, axis=-1, keepdims=True)


def _cross(hc, x1, kv, w_cq, w_co, g_mlp, tm):
    t = x1.shape[0]
    m = kv.shape[0]

    def body(hc_ref, x1_ref, kv_ref, wq_ref, wo_ref, g_ref, q_ref, o_ref, x2_ref, hm_ref):
        q = (_dot(hc_ref[...], wq_ref[...]) * X_SCALE).astype(BF16)
        q_ref[...] = q
        outs = []
        for h in range(X_HEADS):
            p = _cross_probs(q, kv_ref, h)
            v = kv_ref[:, D_MODEL + h * X_HEAD_DIM:D_MODEL + (h + 1) * X_HEAD_DIM]
            outs.append(_dot(p.astype(BF16), v))
        o = jnp.concatenate(outs, axis=1).astype(BF16)
        o_ref[...] = o
        x2 = x1_ref[...] + _dot(o, wo_ref[...])
        x2_ref[...] = x2
        hm_ref[...] = (x2 * _rms(x2) * g_ref[...]).astype(BF16)

    row = lambda w: pl.BlockSpec((tm, w), lambda i: (i, 0))
    full = lambda a, b: pl.BlockSpec((a, b), lambda i: (0, 0))
    return pl.pallas_call(
        body, name="cross", grid=(t // tm,),
        in_specs=[row(D_MODEL), row(D_MODEL), full(m, 2 * D_MODEL), full(D_MODEL, D_MODEL),
                  full(D_MODEL, D_MODEL), full(1, D_MODEL)],
        out_specs=[row(D_MODEL)] * 4,
        out_shape=[jax.ShapeDtypeStruct((t, D_MODEL), BF16), jax.ShapeDtypeStruct((t, D_MODEL), BF16),
                   jax.ShapeDtypeStruct((t, D_MODEL), F32), jax.ShapeDtypeStruct((t, D_MODEL), BF16)],
        compiler_params=_params(("arbitrary",)),
    )(hc, x1, kv, w_cq, w_co, g_mlp)


def _mlp(hm, x2, w_1, w_2, g_final, target, tm, tf):
    t = x2.shape[0]
    nf = D_FF // tf

    def body(hm_ref, x2_ref, w1_ref, w2_ref, g_ref, tg_ref, a_ref, dx3_ref, loss_ref, dg_ref, acc_ref):
        i, f = pl.program_id(0), pl.program_id(1)
        hm_t = hm_ref[...]
        a = jnp.concatenate([jnp.maximum(_dot(hm_t, w1_ref[s]), 0.0) for s in range(w1_ref.shape[0])], axis=1)
        a_ref[...] = a.astype(BF16)
        part = _dot((a * a).astype(BF16), w2_ref[...])

        @pl.when(f == 0)
        def _():
            acc_ref[...] = part

        @pl.when(f > 0)
        def _():
            acc_ref[...] += part

        @pl.when((i == 0) & (f == 0))
        def _():
            loss_ref[...] = jnp.zeros_like(loss_ref)
            dg_ref[...] = jnp.zeros_like(dg_ref)

        @pl.when(f == nf - 1)
        def _():
            x3 = x2_ref[...] + acc_ref[...]
            r = _rms(x3)
            g = g_ref[...]
            diff = x3 * r * g - tg_ref[...]
            loss_ref[...] += 0.5 * jnp.sum(jnp.mean(diff * diff, axis=-1, keepdims=True))
            dx3, dg = _rms_bwd(diff / D_MODEL, x3, r, g)
            dx3_ref[...] = dx3
            dg_ref[...] += dg

    return pl.pallas_call(
        body, name="mlp", grid=(t // tm, nf),
        in_specs=[pl.BlockSpec((tm, D_MODEL), lambda i, f: (i, 0)),
                  pl.BlockSpec((tm, D_MODEL), lambda i, f: (i, 0)),
                  pl.BlockSpec((tf // w_1.shape[2], D_MODEL, w_1.shape[2]), lambda i, f: (f, 0, 0)),
                  pl.BlockSpec((tf, D_MODEL), lambda i, f: (f, 0)),
                  pl.BlockSpec((1, D_MODEL), lambda i, f: (0, 0)),
                  pl.BlockSpec((tm, D_MODEL), lambda i, f: (i, 0))],
        out_specs=[pl.BlockSpec((tm, tf), lambda i, f: (i, f)),
                   pl.BlockSpec((tm, D_MODEL), lambda i, f: (i, 0)),
                   pl.BlockSpec((1, 128), lambda i, f: (0, 0)),
                   pl.BlockSpec((1, D_MODEL), lambda i, f: (0, 0))],
        out_shape=[jax.ShapeDtypeStruct((t, D_FF), BF16), jax.ShapeDtypeStruct((t, D_MODEL), F32),
                   jax.ShapeDtypeStruct((1, 128), F32), jax.ShapeDtypeStruct((1, D_MODEL), F32)],
        scratch_shapes=[pltpu.VMEM((tm, D_MODEL), F32)],
        compiler_params=_params(("arbitrary", "arbitrary")),
    )(hm, x2, w_1, w_2, g_final, target)


def _mlp_bwd(dx3, a, w_1, w_2, x2, g_mlp, tm, tf):
    t = x2.shape[0]
    nf = D_FF // tf

    def body(dx3_ref, a_ref, w1_ref, w2_ref, x2_ref, g_ref, dz_ref, dx2_ref, dg_ref, acc_ref):
        i, f = pl.program_id(0), pl.program_id(1)
        da2 = _dot_nt(dx3_ref[...].astype(BF16), w2_ref[...])
        dz = (2.0 * a_ref[...].astype(F32) * da2).astype(BF16)
        dz_ref[...] = dz
        sw = w1_ref.shape[2]
        part = _dot_nt(dz[:, 0:sw], w1_ref[0])
        for s in range(1, w1_ref.shape[0]):
            part = part + _dot_nt(dz[:, s * sw:(s + 1) * sw], w1_ref[s])

        @pl.when(f == 0)
        def _():
            acc_ref[...] = part

        @pl.when(f > 0)
        def _():
            acc_ref[...] += part

        @pl.when((i == 0) & (f == 0))
        def _():
            dg_ref[...] = jnp.zeros_like(dg_ref)

        @pl.when(f == nf - 1)
        def _():
            xt = x2_ref[...]
            dx, dg = _rms_bwd(acc_ref[...], xt, _rms(xt), g_ref[...])
            dx2_ref[...] = dx3_ref[...] + dx
            dg_ref[...] += dg

    return pl.pallas_call(
        body, name="mlp_bwd", grid=(t // tm, nf),
        in_specs=[pl.BlockSpec((tm, D_MODEL), lambda i, f: (i, 0)),
                  pl.BlockSpec((tm, tf), lambda i, f: (i, f)),
                  pl.BlockSpec((tf // w_1.shape[2], D_MODEL, w_1.shape[2]), lambda i, f: (f, 0, 0)),
                  pl.BlockSpec((tf, D_MODEL), lambda i, f: (f, 0)),
                  pl.BlockSpec((tm, D_MODEL), lambda i, f: (i, 0)),
                  pl.BlockSpec((1, D_MODEL), lambda i, f: (0, 0))],
        out_specs=[pl.BlockSpec((tm, tf), lambda i, f: (i, f)),
                   pl.BlockSpec((tm, D_MODEL), lambda i, f: (i, 0)),
                   pl.BlockSpec((1, D_MODEL), lambda i, f: (0, 0))],
        out_shape=[jax.ShapeDtypeStruct((t, D_FF), BF16), jax.ShapeDtypeStruct((t, D_MODEL), F32),
                   jax.ShapeDtypeStruct((1, D_MODEL), F32)],
        scratch_shapes=[pltpu.VMEM((tm, D_MODEL), F32)],
        compiler_params=_params(("arbitrary", "arbitrary")),
    )(dx3, a, w_1, w_2, x2, g_mlp)


def _wgrad(name, a, b, tka, tn, tm, square=False, col_shards=False):
    t, ka = a.shape
    n = b.shape[1]
    nk = t // tm

    def body(a_ref, b_ref, o_ref, acc_ref):
        at = a_ref[...].astype(BF16)
        if square:
            at = at * at
        part = _dot_tn(at, b_ref[...].astype(BF16))
        k = pl.program_id(2)

        @pl.when(k == 0)
        def _():
            acc_ref[...] = part

        @pl.when(k > 0)
        def _():
            acc_ref[...] += part

        @pl.when(k == nk - 1)
        def _():
            if col_shards:
                for s in range(tn // sw):
                    o_ref[s] = acc_ref[:, s * sw:(s + 1) * sw].astype(BF16)
            else:
                o_ref[...] = acc_ref[...].astype(BF16)

    if col_shards:
        sw = n // N_DEV
        out_spec = pl.BlockSpec((tn // sw, tka, sw), lambda p, q, k: (q, p, 0))
        out_shape = jax.ShapeDtypeStruct((N_DEV, ka, sw), BF16)
    else:
        out_spec = pl.BlockSpec((tka, tn), lambda p, q, k: (p, q))
        out_shape = jax.ShapeDtypeStruct((ka, n), BF16)
    return pl.pallas_call(
        body, name=name, grid=(ka // tka, n // tn, nk),
        in_specs=[pl.BlockSpec((tm, tka), lambda p, q, k: (k, p)),
                  pl.BlockSpec((tm, tn), lambda p, q, k: (k, q))],
        out_specs=out_spec, out_shape=out_shape,
        scratch_shapes=[pltpu.VMEM((tka, tn), F32)],
        compiler_params=_params(("arbitrary", "arbitrary", "arbitrary")),
    )(a, b)


def _cross_bwd(dx2, x1, q, kv, w_cq, w_co, g_cross, tm, dep=None):
    t = x1.shape[0]
    m = kv.shape[0]

    def body(dx2_ref, x1_ref, q_ref, kv_ref, wq_ref, wo_ref, g_ref, dq_ref, dx1_ref, dkv_ref, dg_ref):
        @pl.when(pl.program_id(0) == 0)
        def _():
            dkv_ref[...] = jnp.zeros_like(dkv_ref)
            dg_ref[...] = jnp.zeros_like(dg_ref)

        do = _dot_nt(dx2_ref[...].astype(BF16), wo_ref[...]).astype(BF16)
        q = q_ref[...]
        dqs = []
        for h in range(X_HEADS):
            hs = slice(h * X_HEAD_DIM, (h + 1) * X_HEAD_DIM)
            vs = slice(D_MODEL + h * X_HEAD_DIM, D_MODEL + (h + 1) * X_HEAD_DIM)
            p = _cross_probs(q, kv_ref, h)
            dp = _dot_nt(do[:, hs], kv_ref[:, vs])
            ds = (p * (dp - jnp.sum(dp * p, axis=-1, keepdims=True))).astype(BF16)
            dqs.append(_dot(ds, kv_ref[:, hs]))
            dkv_ref[:, hs] += _dot_tn(ds, q[:, hs])
            dkv_ref[:, vs] += _dot_tn(p.astype(BF16), do[:, hs])
        dq = (jnp.concatenate(dqs, axis=1) * X_SCALE).astype(BF16)
        dq_ref[...] = dq
        xt = x1_ref[...]
        dx, dg = _rms_bwd(_dot_nt(dq, wq_ref[...]), xt, _rms(xt), g_ref[...])
        dx1_ref[...] = dx2_ref[...] + dx
        dg_ref[...] += dg

    row = lambda w: pl.BlockSpec((tm, w), lambda i: (i, 0))
    full = lambda a, b: pl.BlockSpec((a, b), lambda i: (0, 0))
    return pl.pallas_call(
        _with_dep(body, 7, dep), name="cross_bwd", grid=(t // tm,),
        in_specs=[row(D_MODEL), row(D_MODEL), row(D_MODEL), full(m, 2 * D_MODEL), full(D_MODEL, D_MODEL),
                  full(D_MODEL, D_MODEL), full(1, D_MODEL)] + _dep_spec(dep),
        out_specs=[row(D_MODEL), row(D_MODEL), full(m, 2 * D_MODEL), full(1, D_MODEL)],
        out_shape=[jax.ShapeDtypeStruct((t, D_MODEL), BF16), jax.ShapeDtypeStruct((t, D_MODEL), F32),
                   jax.ShapeDtypeStruct((m, 2 * D_MODEL), F32), jax.ShapeDtypeStruct((1, D_MODEL), F32)],
        compiler_params=_params(("arbitrary",)),
    )(dx2, x1, q, kv, w_cq, w_co, g_cross, *_dep_arg(dep))


def _memkv_bwd(dkv, mn, mem, w_ckv, g_mem):
    ws = w_ckv.shape[2]

    def body(dkv_ref, mn_ref, mem_ref, w_ref, g_ref, dw_ref, dg_ref):
        mn = mn_ref[...]
        dmn = jnp.zeros(mn.shape, F32)
        for j in range(N_DEV):
            dkvb = dkv_ref[:, j * ws:(j + 1) * ws].astype(BF16)
            dw_ref[j] = _dot_tn(mn, dkvb).astype(BF16)
            dmn = dmn + _dot_nt(dkvb, w_ref[j])
        xt = mem_ref[...]
        dg_ref[...] = jnp.sum(dmn * xt * _rms(xt), axis=0, keepdims=True)

    return pl.pallas_call(
        body, name="memkv_bwd",
        out_shape=[jax.ShapeDtypeStruct(w_ckv.shape, BF16), jax.ShapeDtypeStruct((1, D_MODEL), F32)],
        compiler_params=_params(),
    )(dkv, mn, mem, w_ckv, g_mem)


def _mix_bwd(dx1, ya, yb, gts, oa, ob, l0, l1, l2, lb, sink_row, w_out, w_a, w_b, w_g, tm):
    t = dx1.shape[0]

    def body(dx1_ref, ya_ref, yb_ref, g_ref, oa_ref, ob_ref, l0_ref, l1_ref, l2_ref, lb_ref, sk_ref,
             wo_ref, wa_ref, wb_ref, wg_ref,
             dg_ref, dhp_ref, dya_ref, dyb_ref, do0_ref, do1_ref, do2_ref, c0_ref, c1_ref, c2_ref,
             dob_ref, cb_ref, db_ref, dsk_ref, so_ref, sl_ref):
        @pl.when(pl.program_id(0) == 0)
        def _():
            db_ref[...] = jnp.zeros_like(db_ref)
            dsk_ref[...] = jnp.zeros_like(dsk_ref)

        dm = _dot_nt(dx1_ref[...].astype(BF16), wo_ref[...])
        ga = g_ref[:, :D_MODEL].astype(F32)
        gb = g_ref[:, D_MODEL:].astype(F32)
        dya = (dm * ga).astype(BF16)
        dyb = (dm * gb).astype(BF16)
        dya_ref[...] = dya
        dyb_ref[...] = dyb
        dpa = dm * ya_ref[...].astype(F32) * ga * (1.0 - ga)
        dpb = dm * yb_ref[...].astype(F32) * gb * (1.0 - gb)
        dpre = jnp.concatenate([dpa, dpb], axis=1)
        db_ref[...] += jnp.sum(dpre, axis=0, keepdims=True)
        dpreb = dpre.astype(BF16)
        dg_ref[...] = dpreb
        dhp_ref[...] = _dot_nt(dpreb, wg_ref[...])

        doa = _dot_nt(dya, wa_ref[...])
        dob = _dot_nt(dyb, wb_ref[...])
        dsum = _head_sums(doa * oa_ref[...].astype(F32), DIL_LANES, 256)
        a0, a1, a2 = _alphas(l0_ref[...], _interleave(l1_ref, sl_ref), _interleave(l2_ref, sl_ref))
        c0_ref[...] = a0 * dsum
        do0_ref[...] = _head_scale(doa, a0, DIL_LANES).astype(BF16)
        for al, do_ref, c_ref in ((a1, do1_ref, c1_ref), (a2, do2_ref, c2_ref)):
            _deinterleave(al * dsum, sl_ref, c_ref, F32)
            _deinterleave(_head_scale(doa, al, DIL_LANES), so_ref, do_ref, BF16)
        dob_ref[...] = dob.astype(BF16)
        cb = _head_sums(dob * ob_ref[...].astype(F32), SWA_LANES, 128)
        cb_ref[...] = cb
        lane = lax.broadcasted_iota(jnp.int32, cb.shape, 1)
        psink = jnp.where(lane < 8, jnp.exp(sk_ref[...] - lb_ref[...]), 0.0)
        dsk_ref[...] += jnp.sum(-psink * cb, axis=0, keepdims=True)

    row = lambda w: pl.BlockSpec((tm, w), lambda i: (i, 0))
    full = lambda a, b: pl.BlockSpec((a, b), lambda i: (0, 0))
    sds = jax.ShapeDtypeStruct
    d1, d2 = l1.shape[0], l2.shape[0]
    res = lambda d, w: pl.BlockSpec((d, tm // d, w), lambda i: (0, i, 0))
    return pl.pallas_call(
        body, name="mix_bwd", grid=(t // tm,),
        in_specs=[row(D_MODEL), row(D_MODEL), row(D_MODEL), row(GATE_WIDTH), row(512), row(512),
                  row(256), _res_spec(l1, tm), _res_spec(l2, tm), row(128), full(1, 128),
                  full(D_MODEL, D_MODEL), full(512, D_MODEL), full(512, D_MODEL), full(D_MODEL, GATE_WIDTH)],
        out_specs=[row(GATE_WIDTH), row(D_MODEL), row(D_MODEL), row(D_MODEL),
                   row(512), res(d1, 512), res(d2, 512), row(256), res(d1, 256), res(d2, 256),
                   row(512), row(128), full(1, GATE_WIDTH), full(1, 128)],
        out_shape=[sds((t, GATE_WIDTH), BF16), sds((t, D_MODEL), F32), sds((t, D_MODEL), BF16),
                   sds((t, D_MODEL), BF16), sds((t, 512), BF16), sds((d1, t // d1, 512), BF16),
                   sds((d2, t // d2, 512), BF16), sds((t, 256), F32), sds((d1, t // d1, 256), F32),
                   sds((d2, t // d2, 256), F32), sds((t, 512), BF16),
                   sds((t, 128), F32), sds((1, GATE_WIDTH), F32), sds((1, 128), F32)],
        scratch_shapes=[_lane_scratch(tm, 512), _lane_scratch(tm, 256)],
        compiler_params=_params(("arbitrary",)),
    )(dx1, ya, yb, gts, oa, ob, l0, l1, l2, lb, sink_row, w_out, w_a, w_b, w_g)


def _attn_bwd(name, pv, dov, lsev, cv, cosv, sinv, swa, tq, dep=None):
    d, ls = pv.shape[0], pv.shape[1]
    n, nsb = ls // tq, tq // BAND
    pairs = _attn_layout(swa)
    ncol = 1 if swa else 2
    ow = 128 * len(pairs)

    def body(cur_ref, tail_ref, do_ref, lse_ref, c_ref, cos_ref, sin_ref, out_ref, acc_ref, carry_ref):
        i = pl.program_id(2)
        acc_ref[...] = jnp.zeros_like(acc_ref)

        @pl.when(i < n)
        def _():
            qk_a, v_a = _head_a_masks(BAND)
            for s in range(nsb):
                mask = _band_mask(i, s)
                mask2 = jnp.concatenate([mask, mask], axis=0)
                rows = slice(s * BAND, (s + 1) * BAND)
                krows = slice(s * BAND, (s + 2) * BAND)
                for j, (qo, ko, vo) in enumerate(pairs):
                    kk = _kv_rows(cur_ref, tail_ref, s, ko)
                    vv = _kv_rows(cur_ref, tail_ref, s, vo)
                    q2 = _stack_heads(cur_ref[rows, qo:qo + 128], qk_a)
                    do2 = _stack_heads(do_ref[rows, j * 128:(j + 1) * 128], v_a)
                    col2 = lambda ref: jnp.concatenate([ref[rows, 2 * j:2 * j + 1], ref[rows, 2 * j + 1:2 * j + 2]], axis=0)
                    sc = _dot_nt(q2, kk)
                    p = jnp.exp(jnp.where(mask2, sc, -jnp.inf) - col2(lse_ref))
                    dp = _dot_nt(do2, vv)
                    ds = (p * (dp - col2(c_ref))).astype(BF16)
                    dq2 = _dot(ds, kk)
                    acc_ref[BAND + s * BAND:BAND + (s + 1) * BAND, qo:qo + 128] += jnp.where(qk_a, dq2[:BAND], dq2[BAND:])
                    acc_ref[krows, ko:ko + 128] += _dot_tn(ds, q2)
                    acc_ref[krows, vo:vo + 128] += _dot_tn(p.astype(BF16), do2)

        @pl.when(i >= 1)
        def _():
            if tq > BAND:
                fin = jnp.concatenate([carry_ref[0:tq - BAND, :], carry_ref[tq - BAND:, :] + acc_ref[0:BAND, :]], axis=0)
            else:
                fin = carry_ref[...] + acc_ref[0:BAND, :]
            out_ref[...] = _rope(fin, cos_ref[...], sin_ref[...], swa, -1).astype(BF16)

        carry_ref[...] = acc_ref[BAND:, :]

    qi = lambda i: jnp.minimum(i, n - 1)
    pi = lambda i: jnp.maximum(i - 1, 0)
    blk = lambda rows, w, row_of: pl.BlockSpec((None, rows, w), lambda r, cb, i: (r, row_of(i), cb))
    return pl.pallas_call(
        _with_dep(body, 7, dep), name=name, grid=(d, ncol, n + 1),
        in_specs=[blk(tq, PBLK, qi), blk(BAND, PBLK, lambda i: jnp.maximum(qi(i) * nsb - 1, 0)),
                  blk(tq, ow, qi), blk(tq, 128, qi), blk(tq, 128, qi),
                  pl.BlockSpec((None, tq, 128), lambda r, cb, i: (r, pi(i), 0)),
                  pl.BlockSpec((None, tq, 128), lambda r, cb, i: (r, pi(i), 0))] + _dep_spec(dep),
        out_specs=blk(tq, PBLK, pi),
        out_shape=jax.ShapeDtypeStruct((d, ls, ncol * PBLK), BF16),
        scratch_shapes=[pltpu.VMEM((tq + BAND, PBLK), F32), pltpu.VMEM((tq, PBLK), F32)],
        compiler_params=_params(("arbitrary", "arbitrary", "arbitrary")),
    )(pv, pv, dov, lsev, cv, cosv, sinv, *_dep_arg(dep))


def _dx(dp0, dp1, dp2, dpb, w_p, dh_part, dx1, x, g_mix, tm, dep=None):
    t = x.shape[0]
    gw = 2 * PBLK

    def body(dp0_ref, dp1_ref, dp2_ref, dpb_ref, w_ref, dhp_ref, dx1_ref, x_ref, g_ref, gx_ref, dg_ref,
             dpt_ref, scr_ref):
        @pl.when(pl.program_id(0) == 0)
        def _():
            dg_ref[...] = jnp.zeros_like(dg_ref)

        dpt_ref[:, 0:gw] = dp0_ref[...]
        dpt_ref[:, gw:2 * gw] = _interleave(dp1_ref, scr_ref).astype(BF16)
        dpt_ref[:, 2 * gw:3 * gw] = _interleave(dp2_ref, scr_ref).astype(BF16)
        dpt_ref[:, 3 * gw:] = dpb_ref[...]
        dh = _dot_nt(dpt_ref[...], w_ref[...]) + dhp_ref[...]
        xt = x_ref[...]
        dx, dg = _rms_bwd(dh, xt, _rms(xt), g_ref[...])
        gx_ref[...] = dx1_ref[...] + dx
        dg_ref[...] += dg

    row = lambda w: pl.BlockSpec((tm, w), lambda i: (i, 0))
    full = lambda a, b: pl.BlockSpec((a, b), lambda i: (0, 0))
    return pl.pallas_call(
        _with_dep(body, 9, dep), name="dx", grid=(t // tm,),
        in_specs=[row(gw), _res_spec(dp1, tm), _res_spec(dp2, tm), row(PBLK), full(D_MODEL, P_WIDTH),
                  row(D_MODEL), row(D_MODEL), row(D_MODEL), full(1, D_MODEL)] + _dep_spec(dep),
        out_specs=[row(D_MODEL), full(1, D_MODEL)],
        out_shape=[jax.ShapeDtypeStruct((t, D_MODEL), F32), jax.ShapeDtypeStruct((1, D_MODEL), F32)],
        scratch_shapes=[pltpu.VMEM((tm, P_WIDTH), BF16), _lane_scratch(tm, gw)],
        compiler_params=_params(("arbitrary",)),
    )(dp0, dp1, dp2, dpb, w_p, dh_part, dx1, x, g_mix, *_dep_arg(dep))


MESH = pl.DeviceIdType.MESH
HBM_SPEC = pl.BlockSpec(memory_space=pltpu.HBM)
VMEM_SPEC = pl.BlockSpec(memory_space=pltpu.VMEM)


def _all_gather(xp):
    def body(x_ref, out_ref, send_sems, recv_sems, local_sem):
        x, y, c = lax.axis_index("x"), lax.axis_index("y"), lax.axis_index("c")
        me, sibling = (x, y, c), (x, y, 1 - c)
        chips = [(1 - x, y), (x, 1 - y), (1 - x, 1 - y)]

        def rows(px, py, pc):
            return out_ref.at[4 * px + 2 * py + pc]

        def copy(k, block, to, src=None):
            return pltpu.make_async_remote_copy(
                src_ref=rows(*block) if src is None else src, dst_ref=rows(*block),
                send_sem=send_sems.at[k], recv_sem=recv_sems.at[k], device_id=to, device_id_type=MESH)

        mine = pltpu.make_async_copy(x_ref, rows(*me), local_sem)
        mine.start()
        first = [copy(0, me, sibling, src=x_ref)]
        first += [copy(1 + j, me, (*chip, c), src=x_ref) for j, chip in enumerate(chips)]
        for cp in first:
            cp.start()
        passed = [copy(4 + j, (*chip, c), sibling) for j, chip in enumerate(chips)]
        for j, chip in enumerate(chips):
            copy(1 + j, (*chip, c), me).wait_recv()
            passed[j].start()
        copy(0, sibling, me).wait_recv()
        for j, chip in enumerate(chips):
            copy(4 + j, (*chip, 1 - c), me).wait_recv()
        for cp in first + passed:
            cp.wait_send()
        mine.wait()

    return pl.pallas_call(
        body, name="all_gather",
        out_shape=jax.ShapeDtypeStruct((N_DEV,) + xp.shape, xp.dtype),
        in_specs=[HBM_SPEC], out_specs=HBM_SPEC,
        scratch_shapes=[pltpu.SemaphoreType.DMA((7,)), pltpu.SemaphoreType.DMA((7,)), pltpu.SemaphoreType.DMA],
    )(xp)


def _peers():
    x, y, c = lax.axis_index("x"), lax.axis_index("y"), lax.axis_index("c")
    out = []
    for k in range(1, N_DEV):
        px = 1 - x if k & 4 else x
        py = 1 - y if k & 2 else y
        pc = 1 - c if k & 1 else c
        out.append((k, (px, py, pc), 4 * px + 2 * py + pc))
    return out


def _my_index():
    return 4 * lax.axis_index("x") + 2 * lax.axis_index("y") + lax.axis_index("c")


SEM_SPEC = pl.BlockSpec(memory_space=pltpu.SEMAPHORE)
ANY_SPEC = pl.BlockSpec(memory_space=pl.ANY)
_SPLIT_PARAMS = pltpu.CompilerParams(has_side_effects=pltpu.SideEffectType.DATAFLOW_SIDE_EFFECTING)


def _split_copies(gather, src_refs, land_refs, send_sems, recv_sems):
    me_idx = _my_index()
    out = []
    for a, (src_ref, land_ref) in enumerate(zip(src_refs, land_refs)):
        for k, peer, peer_idx in _peers():
            if gather:
                src, dst = src_ref, land_ref.at[me_idx]
            else:
                src, dst = src_ref.at[peer_idx], land_ref.at[k - 1]
            out.append(pltpu.make_async_remote_copy(
                src_ref=src, dst_ref=dst, send_sem=send_sems.at[7 * a + k - 1], recv_sem=recv_sems.at[7 * a + k - 1],
                device_id=peer, device_id_type=MESH))
    return out


def _split_start(name, gather, srcs):
    n = len(srcs)

    def body(*refs):
        send_sems, recv_sems = refs[n], refs[n + 1]
        for cp in _split_copies(gather, refs[:n], refs[2 * n + 2:3 * n + 2], send_sems, recv_sems):
            cp.start()
        token = refs[-1]
        token[...] = jnp.zeros_like(token)

    lands = [pltpu.HBM((N_DEV,) + a.shape if gather else (N_DEV - 1,) + a.shape[1:], a.dtype) for a in srcs]
    return pl.pallas_call(
        body, name=name,
        out_shape=(pltpu.SemaphoreType.DMA((7 * n,)), pltpu.SemaphoreType.DMA((7 * n,)),
                   *[pltpu.HBM(a.shape, a.dtype) for a in srcs], *lands, jax.ShapeDtypeStruct((8, 128), F32)),
        in_specs=(HBM_SPEC,) * n, out_specs=(SEM_SPEC, SEM_SPEC) + (HBM_SPEC,) * (2 * n) + (VMEM_SPEC,),
        input_output_aliases={i: 2 + i for i in range(n)}, compiler_params=_SPLIT_PARAMS,
    )(*[pltpu.with_memory_space_constraint(a, pltpu.HBM) for a in srcs])


def _split_wait(name, gather, started, after):
    send_sems, recv_sems, bufs = started[0], started[1], started[2:-1]
    n = len(bufs) // 2

    def body(*refs):
        for cp in _split_copies(gather, refs[:n], refs[n:2 * n], refs[2 * n], refs[2 * n + 1]):
            cp.wait_send()
            cp.wait_recv()

    out = pl.pallas_call(
        body, name=name, out_shape=tuple(pltpu.HBM(a.shape, a.dtype) for a in bufs),
        in_specs=(HBM_SPEC,) * (2 * n) + (SEM_SPEC, SEM_SPEC, ANY_SPEC), out_specs=(HBM_SPEC,) * (2 * n),
        input_output_aliases={i: i for i in range(2 * n)}, compiler_params=_SPLIT_PARAMS,
    )(*bufs, send_sems, recv_sems, after)
    return out[:n], out[n:]


def _adam_update(g, w, m, v):
    nm = ADAM_B1 * m + (1.0 - ADAM_B1) * g
    nv = ADAM_B2 * v + (1.0 - ADAM_B2) * (g * g)
    m_hat = nm / (1.0 - ADAM_B1 ** ADAM_STEP)
    v_hat = nv / (1.0 - ADAM_B2 ** ADAM_STEP)
    return -ADAM_LR * (m_hat / (jnp.sqrt(v_hat) + ADAM_EPS) + ADAM_WD * w), nm, nv


def _adamw(name, me, sent, got, w, m, v, tr):
    r, c = w.shape

    def body(me_ref, own_ref, got_ref, w_ref, m_ref, v_ref, g_ref, d_ref, nm_ref, nv_ref):
        g = own_ref[...].astype(F32)
        for k in range(N_DEV - 1):
            g = g + got_ref[k].astype(F32)
        g_ref[...] = g
        d_ref[...], nm_ref[...], nv_ref[...] = _adam_update(g, w_ref[...], m_ref[...], v_ref[...])

    blk = pl.BlockSpec((tr, c), lambda i, me_ref: (i, 0))
    return pl.pallas_call(
        body, name=name,
        grid_spec=pltpu.PrefetchScalarGridSpec(
            num_scalar_prefetch=1, grid=(r // tr,),
            in_specs=[pl.BlockSpec((None, tr, c), lambda i, me_ref: (me_ref[0], i, 0)),
                      pl.BlockSpec((N_DEV - 1, tr, c), lambda i, me_ref: (0, i, 0)), blk, blk, blk],
            out_specs=[blk] * 4),
        out_shape=[jax.ShapeDtypeStruct((r, c), F32)] * 4,
        compiler_params=_params(("arbitrary",)),
    )(me, sent, got, w, m, v)


def _adamw_small(srecv, ws, ms, vs):
    nv_ = len(ws)

    def body(*refs):
        s_ref = refs[0]
        ins, outs = refs[1:1 + 3 * nv_], refs[1 + 3 * nv_:]
        g_all = s_ref[0]
        for k in range(1, N_DEV):
            g_all = g_all + s_ref[k]
        for i in range(nv_):
            n = ins[i].shape[1]
            g = g_all[i:i + 1, :n]
            d, nm, nv = _adam_update(g, ins[i][...], ins[nv_ + i][...], ins[2 * nv_ + i][...])
            outs[i][...], outs[nv_ + i][...], outs[2 * nv_ + i][...], outs[3 * nv_ + i][...] = g, d, nm, nv

    shapes = [jax.ShapeDtypeStruct(a.shape, F32) for a in ws]
    res = pl.pallas_call(body, name="adamw_small", out_shape=shapes * 4, compiler_params=_params())(
        srecv, *ws, *ms, *vs)
    return [res[k * nv_:(k + 1) * nv_] for k in range(4)]


def _cols_from_shards(a):
    return jnp.swapaxes(a, 0, 1).reshape(a.shape[1], N_DEV * a.shape[2])


def _shards_from_cols(a):
    return jnp.swapaxes(a.reshape(a.shape[0], N_DEV, a.shape[1] // N_DEV), 0, 1)


def _shards_from_rows(a):
    return a.reshape(N_DEV, a.shape[0] // N_DEV, a.shape[1])


def _pair_lanes(a):
    lead = a.shape[:-1]
    return a.reshape(lead + (2, 2, HEAD_DIM // 2)).swapaxes(-3, -2).reshape(lead + (128,))


def _split_w_in(w_in):
    rows = w_in.shape[0]
    dil = w_in[:, :3 * DIL_WIDTH].reshape(rows, 3, 3, 4, 128)
    dil = jnp.concatenate([_pair_lanes(dil[:, :2]), dil[:, 2:]], axis=1)
    dil = dil.transpose(0, 2, 3, 1, 4).reshape(rows, 3 * DIL_WIDTH)
    o = 3 * DIL_WIDTH
    qb = w_in[:, o:o + SWA_Q_WIDTH].reshape(rows, 2, 4, HEAD_DIM).transpose(0, 2, 1, 3).reshape(rows, 4, 128)
    qb = _pair_lanes(qb).reshape(rows, SWA_Q_WIDTH)
    kb = _pair_lanes(w_in[:, o + SWA_Q_WIDTH:o + SWA_Q_WIDTH + SWA_KV_WIDTH])
    vb = w_in[:, o + SWA_Q_WIDTH + SWA_KV_WIDTH:P_WIDTH]
    return jnp.concatenate([dil, qb, kb, vb], axis=1), w_in[:, P_WIDTH:]


def _merge_w_in(dw_p, dw_g):
    rows = dw_p.shape[0]
    dil = dw_p[:, :3 * DIL_WIDTH].reshape(rows, 3, 4, 3, 128).transpose(0, 3, 1, 2, 4)
    dil = jnp.concatenate([_pair_lanes(dil[:, :2]), dil[:, 2:]], axis=1).reshape(rows, 3 * DIL_WIDTH)
    o = 3 * DIL_WIDTH
    qb = _pair_lanes(dw_p[:, o:o + SWA_Q_WIDTH].reshape(rows, 4, 128))
    qb = qb.reshape(rows, 4, 2, HEAD_DIM).transpose(0, 2, 1, 3).reshape(rows, SWA_Q_WIDTH)
    kb = _pair_lanes(dw_p[:, o + SWA_Q_WIDTH:o + SWA_Q_WIDTH + SWA_KV_WIDTH])
    vb = dw_p[:, o + SWA_Q_WIDTH + SWA_KV_WIDTH:]
    return jnp.concatenate([dil, qb, kb, vb, dw_g], axis=1)


def _swa_rows(w_b):
    return w_b.reshape(2, 4, HEAD_DIM, -1).transpose(1, 0, 2, 3).reshape(SWA_Q_WIDTH, -1)


def _swa_rows_inv(dw_b):
    return dw_b.reshape(4, 2, HEAD_DIM, -1).transpose(1, 0, 2, 3).reshape(SWA_Q_WIDTH, -1)


def _rope_tables(pos):
    half = HEAD_DIM // 2
    inv = ROPE_THETA ** (-jnp.arange(half, dtype=F32) / half)
    ang = pos.astype(F32)[:, None] * inv
    c, s = jnp.cos(ang), jnp.sin(ang)
    return jnp.concatenate([c, c, c, c], axis=1), jnp.concatenate([-s, -s, s, s], axis=1)


def _local_step(x, mem, pos, target, w_in, dep, rest_weights, on_grads, g_mix, g_cross, g_mem, g_mlp, g_final, sink):
    t = x.shape[0]
    tm = min(512, t)
    tq = min(512, t // 16)
    tw = min(2048, t)
    w_p, w_g = _split_w_in(w_in)
    cos, sin = _rope_tables(pos)
    sink_row = jnp.pad(sink.reshape(2, 4).T.reshape(1, 8), ((0, 0), (0, 120)))
    tabs = [(cos[None], sin[None])]
    for _, d in DIL_GROUPS[1:]:
        c_d, s_d = _rope_tables(pos.reshape(t // d, d).T.reshape(-1))
        tabs.append((c_d.reshape(d, t // d, 128), s_d.reshape(d, t // d, 128)))
    tabs.append(tabs[0])

    h, h1, h2, p0, p1, p2, pb = _inproj(x, g_mix, w_p, [(cos, sin), tabs[1], tabs[2]], tm, dep)
    ps = [p0[None], p1, p2, pb[None]]
    outs, lses = [], []
    for gi, pv in enumerate(ps):
        o, l = _attn_fwd(f"attn_fwd{gi}", pv, gi == 3, sink_row[0, :8], tq)
        outs.append(o)
        lses.append(l)
    o0, l0, ob, lb = outs[0][0], lses[0][0], outs[3][0], lses[3][0]
    wts = rest_weights(lb)
    w_b = _swa_rows(wts["w_branch_b"])
    tf = 2048
    gts = _gates(h, w_g, wts["b_gate"].reshape(1, GATE_WIDTH), tm, 1024)
    oa, ya, yb, merged, x1, hc = _mix(o0, outs[1], outs[2], l0, lses[1], lses[2], ob, gts, x,
                                      wts["w_branch_a"], w_b, wts["w_out"], g_cross, tm)
    mn, kv = _memkv(mem, g_mem, wts["w_ckv"])
    q, o, x2, hm = _cross(hc, x1, kv, wts["w_cq"], wts["w_co"], g_mlp, tm)
    a, dx3, loss, dg_final = _mlp(hm, x2, wts["w_1"], wts["w_2"], g_final.reshape(1, D_MODEL), target, tm, tf)

    grads = {}
    dz, dx2, dg_mlp = _mlp_bwd(dx3, a, wts["w_1"], wts["w_2"], x2, g_mlp, tm, tf)
    grads["w_2"] = _shards_from_rows(_wgrad("dw_2", a, dx3, 1024, 1024, tw, square=True))
    grads["w_1"] = _wgrad("dw_1", hm, dz, 1024, 1024, tw, col_shards=True)
    dep = on_grads(GROUP_A, grads)
    dq, dx1, dkv, dg_cross = _cross_bwd(dx2, x1, q, kv, wts["w_cq"], wts["w_co"], g_cross, tm, dep)
    grads["w_co"] = _shards_from_rows(_wgrad("dw_co", o, dx2, 1024, 1024, tw))
    grads["w_cq"] = _shards_from_rows(_wgrad("dw_cq", hc, dq, 1024, 1024, tw))
    grads["w_ckv"], dg_mem = _memkv_bwd(dkv, mn, mem, wts["w_ckv"], g_mem)
    (dgt, dh_part, dya, dyb, do0, do1, do2, c0, c1, c2, dob, cb, db_gate, dsink) = _mix_bwd(
        dx1, ya, yb, gts, oa, ob, l0, lses[1], lses[2], lb, sink_row,
        wts["w_out"], wts["w_branch_a"], w_b, w_g, min(256, tm))
    grads["w_out"] = _shards_from_rows(_wgrad("dw_out", merged, dx1, 1024, 1024, tw))
    grads["w_branch_a"] = _shards_from_cols(_wgrad("dw_a", oa, dya, 512, 1024, tw))
    grads["w_branch_b"] = _shards_from_cols(_swa_rows_inv(_wgrad("dw_b", ob, dyb, 512, 1024, tw)))
    grads["b_gate"] = _shards_from_cols(db_gate.reshape(2, D_MODEL)).astype(BF16)
    dep = on_grads(GROUP_B, grads)
    dw_g = _wgrad("dw_g", h, dgt, 1024, 1024, tw)
    dps = []
    for gi, (pv, do_g, c_g) in enumerate(zip(ps, (do0[None], do1, do2, dob[None]), (c0[None], c1, c2, cb[None]))):
        dps.append(_attn_bwd(f"attn_bwd{gi}", pv, do_g, lses[gi], c_g, tabs[gi][0], tabs[gi][1], gi == 3, tq,
                             dep if gi == 0 else None))
    dw_p = jnp.concatenate(
        [_wgrad(f"dw_p{gi}", hh.reshape(t, D_MODEL), dpg.reshape(t, -1), 1024, PBLK, tw)
         for gi, (hh, dpg) in enumerate(zip((h, h1, h2, h), dps))], axis=1)
    grads["w_in"] = _shards_from_cols(_merge_w_in(dw_p, dw_g))
    dep = on_grads(GROUP_C, grads)
    grad_x, dg_mix = _dx(dps[0][0], dps[1], dps[2], dps[3][0], w_p, dh_part, dx1, x, g_mix, min(256, tm), dep)
    dsink_heads = dsink[0, :8].reshape(4, 2).T.reshape(8)
    small = {"g_mix": dg_mix[0], "g_cross": dg_cross[0], "g_mem": dg_mem[0], "g_mlp": dg_mlp[0],
             "g_final": dg_final[0], "sink": dsink_heads}
    return loss[0, 0], grad_x, small


def kernel(x, mem, positions, g_mix, w_in, b_gate, sink, w_branch_a, w_branch_b, w_out, g_cross, g_mem, w_cq, w_ckv, w_co, g_mlp, w_1, w_2, g_final, loss_target, m_g_mix, m_w_in, m_b_gate, m_sink, m_w_branch_a, m_w_branch_b, m_w_out, m_g_cross, m_g_mem, m_w_cq, m_w_ckv, m_w_co, m_g_mlp, m_w_1, m_w_2, m_g_final, v_g_mix, v_w_in, v_b_gate, v_sink, v_w_branch_a, v_w_branch_b, v_w_out, v_g_cross, v_g_mem, v_w_cq, v_w_ckv, v_w_co, v_g_mlp, v_w_1, v_w_2, v_g_final):
    local = dict(locals())
    shard = {n: local[n][0] for n in GROUP_A + GROUP_B + GROUP_C}
    me = _my_index()
    me_arr = me.reshape(1).astype(jnp.int32)
    tags = {GROUP_A: "a", GROUP_B: "b", GROUP_C: "c"}

    w_in_full = _cols_from_shards(_all_gather(shard["w_in"].astype(BF16)))
    rest = GROUP_A + GROUP_B

    def gathered(name, started, after):
        srcs, lands = _split_wait(name, True, started, after)
        return [lax.dynamic_update_slice(land, src[None], (me,) + (0,) * src.ndim) for src, land in zip(srcs, lands)]

    gather = _split_start("gather_start", True,
                          [shard[n] if n == "b_gate" else shard[n].astype(BF16) for n in rest])

    def rest_weights(after):
        full = {}
        for name, a in zip(rest, gathered("gather_wait", gather, after)):
            if name in ("w_1", "w_ckv"):
                full[name] = a
            elif name in _COL_SHARDED:
                full[name] = _cols_from_shards(a)
            else:
                full[name] = a.reshape(N_DEV * a.shape[1], a.shape[2])
        return full

    scatters = {}

    def on_grads(names, grads):
        scatters[names] = _split_start("scatter_start_" + tags[names], False, [grads[n] for n in names])
        return scatters[names][-1]

    loss, grad_x, small = _local_step(
        x[0], mem[0], positions[0], loss_target[0], w_in_full, gather[-1], rest_weights, on_grads,
        g_mix, g_cross, g_mem, g_mlp, g_final, sink[0])

    sp = jnp.stack([small[n] if n != "sink" else jnp.pad(small[n], (0, LANES - 8)) for n in SMALL]
                   + [jnp.zeros((LANES,), F32)] * 2)
    small_gather = _split_start("small_start", True, [sp])

    after, updated = small_gather[-1], {}
    for names in (GROUP_A, GROUP_B, GROUP_C):
        sent, got = _split_wait("scatter_wait_" + tags[names], False, scatters[names], after)
        for i, name in enumerate(names):
            outs = _adamw("adamw_" + name, me_arr, sent[i], got[i], shard[name],
                          local["m_" + name][0], local["v_" + name][0], ADAM_ROWS[name])
            updated[name] = [a[None] for a in outs]
            after = outs[3]

    flat = lambda prefix: [local[prefix + n].reshape(1, -1) for n in SMALL]
    outs = _adamw_small(gathered("small_wait", small_gather, after)[0], flat(""), flat("m_"), flat("v_"))
    for i, name in enumerate(SMALL):
        updated[name] = [outs[which][i].reshape(local[name].shape) for which in range(4)]

    order = ["g_mix", "w_in", "b_gate", "sink", "w_branch_a", "w_branch_b", "w_out", "g_cross", "g_mem", "w_cq",
             "w_ckv", "w_co", "g_mlp", "w_1", "w_2", "g_final"]
    res = [lax.psum(loss, ("x", "y", "c")), grad_x[None]]
    for which in range(4):
        res += [updated[n][which] for n in order]
    return tuple(res)
```

```python
import functools
import math

import jax
import jax.numpy as jnp
from jax import lax
from jax.experimental import pallas as pl
from jax.experimental.pallas import tpu as pltpu

F32 = jnp.float32
BF16 = jnp.bfloat16

D_MODEL = 1024
HEAD_DIM = 64
DIL_GROUPS = ((128, 1), (512, 4), (2048, 16))
ROPE_THETA = 10000.0
X_HEADS = 4
X_HEAD_DIM = D_MODEL // X_HEADS
D_FF = 4 * D_MODEL
EPS = 1e-6
DIL_WIDTH = 1536
SWA_Q_WIDTH = 512
SWA_KV_WIDTH = 128
P_WIDTH = 3 * DIL_WIDTH + SWA_Q_WIDTH + 2 * SWA_KV_WIDTH
GATE_WIDTH = 2 * D_MODEL
IN_WIDTH = P_WIDTH + GATE_WIDTH
BAND = 128
PBLK = 768
Q_SCALE = HEAD_DIM ** -0.5
X_SCALE = X_HEAD_DIM ** -0.5

ADAM_LR = 0.001
ADAM_B1 = 0.9
ADAM_B2 = 0.999
ADAM_EPS = 1e-08
ADAM_WD = 0.01
ADAM_STEP = 10

N_DEV = 8
LANES = 1024
VMEM_LIMIT = 52 * 1024 * 1024

NT = (((1,), (1,)), ((), ()))
TN = (((0,), (0,)), ((), ()))

GROUP_A = ("w_1", "w_2")
GROUP_B = ("w_branch_a", "w_branch_b", "w_out", "w_cq", "w_ckv", "w_co", "b_gate")
GROUP_C = ("w_in",)
_COL_SHARDED = ("w_in", "w_branch_a", "w_branch_b", "w_ckv", "w_1", "b_gate")
ADAM_ROWS = {"w_in": 256, "w_branch_a": 512, "w_branch_b": 512, "w_out": 128, "w_cq": 128, "w_ckv": 512,
             "w_co": 128, "w_1": 256, "w_2": 256, "b_gate": 2}
SMALL = ("g_mix", "g_cross", "g_mem", "g_mlp", "g_final", "sink")


def _params(sem=None):
    return pltpu.CompilerParams(dimension_semantics=sem, vmem_limit_bytes=VMEM_LIMIT)


def _dot(a, b):
    return jnp.dot(a, b, preferred_element_type=F32)


def _dot_nt(a, b):
    return lax.dot_general(a, b, NT, preferred_element_type=F32)


def _dot_tn(a, b):
    return lax.dot_general(a, b, TN, preferred_element_type=F32)


def _rms(xt):
    return lax.rsqrt(jnp.mean(xt * xt, axis=-1, keepdims=True) + EPS)


def _rms_bwd(dh, xt, r, g):
    xn = xt * r
    dxn = dh * g
    dx = r * (dxn - xn * jnp.mean(dxn * xn, axis=-1, keepdims=True))
    return dx, jnp.sum(dh * xn, axis=0, keepdims=True)


def _rope(x, c, s, swa, sign):
    kinds = "qqqqkv" if swa else "qkvqkv"
    cq, sq = c * Q_SCALE, s * (sign * Q_SCALE)
    sk = s * sign if sign != 1 else s
    out = []
    for ci, kind in enumerate(kinds):
        xc = x[:, ci * 128:(ci + 1) * 128]
        if kind == "v":
            out.append(xc)
        elif kind == "q":
            out.append(xc * cq + pltpu.roll(xc, 64, 1) * sq)
        else:
            out.append(xc * c + pltpu.roll(xc, 64, 1) * sk)
    return jnp.concatenate(out, axis=1)


def _lane_scratch(rows, w):
    return pltpu.VMEM((w // 128, rows, 128), F32)


def _deinterleave(val, scr_ref, dst_ref, dtype):
    d, n = dst_ref.shape[0], dst_ref.shape[1]
    nc = val.shape[1] // 128
    for c in range(nc):
        scr_ref[c] = val[:, c * 128:(c + 1) * 128]
    for r in range(d):
        rows = [scr_ref.at[c][pl.ds(r, n, stride=d), :] for c in range(nc)]
        dst_ref[r] = jnp.concatenate(rows, axis=1).astype(dtype)


def _res_spec(a, tm):
    d, w = a.shape[0], a.shape[2]
    return pl.BlockSpec((d, tm // d, w), lambda i: (0, i, 0))


def _interleave(src_ref, scr_ref):
    d, n = src_ref.shape[0], src_ref.shape[1]
    nc = src_ref.shape[2] // 128
    for r in range(d):
        v = src_ref[r].astype(F32)
        for c in range(nc):
            scr_ref.at[c][pl.ds(r, n, stride=d), :] = v[:, c * 128:(c + 1) * 128]
    return jnp.concatenate([scr_ref[c] for c in range(nc)], axis=1)


def _with_dep(body, n_in, dep):
    if dep is None:
        return body
    return lambda *refs: body(*refs[:n_in], *refs[n_in + 1:])


def _dep_spec(dep):
    return [] if dep is None else [pl.BlockSpec(memory_space=pl.ANY)]


def _dep_arg(dep):
    return [] if dep is None else [dep]


def _inproj(x, g, w_p, tabs, tm, dep=None):
    t = x.shape[0]
    gw = 2 * PBLK
    (cos, sin), (cos1, sin1), (cos2, sin2) = tabs[0], tabs[1], tabs[2]

    def body(x_ref, g_ref, w_ref, c_ref, s_ref, c1_ref, s1_ref, c2_ref, s2_ref,
             h_ref, h1_ref, h2_ref, p0_ref, p1_ref, p2_ref, pb_ref, hf_ref):
        xt = x_ref[...]
        hf = xt * _rms(xt) * g_ref[...]
        h_ref[...] = hf.astype(BF16)
        _deinterleave(hf, hf_ref, h1_ref, BF16)
        _deinterleave(hf, hf_ref, h2_ref, BF16)
        rows = lambda ref: ref[...].reshape(tm, ref.shape[-1])
        groups = ((h_ref, c_ref, s_ref, p0_ref), (h1_ref, c1_ref, s1_ref, p1_ref), (h2_ref, c2_ref, s2_ref, p2_ref))
        for gi, (lhs_ref, cc_ref, ss_ref, out_ref) in enumerate(groups):
            lhs, cc, ss = rows(lhs_ref), rows(cc_ref), rows(ss_ref)
            for half in range(2):
                col = gi * gw + half * PBLK
                val = _rope(_dot(lhs, w_ref[:, col:col + PBLK]), cc, ss, False, 1).astype(BF16)
                if out_ref.ndim == 3:
                    out_ref[:, :, half * PBLK:(half + 1) * PBLK] = val.reshape(out_ref.shape[:2] + (PBLK,))
                else:
                    out_ref[:, half * PBLK:(half + 1) * PBLK] = val
        pb_ref[...] = _rope(_dot(h_ref[...], w_ref[:, 3 * gw:]), c_ref[...], s_ref[...], True, 1).astype(BF16)

    d1, d2 = DIL_GROUPS[1][1], DIL_GROUPS[2][1]
    row = lambda w: pl.BlockSpec((tm, w), lambda i: (i, 0))
    res = lambda d, w: pl.BlockSpec((d, tm // d, w), lambda i: (0, i, 0))
    sds = jax.ShapeDtypeStruct
    return pl.pallas_call(
        _with_dep(body, 9, dep), name="inproj", grid=(t // tm,),
        in_specs=[row(D_MODEL), pl.BlockSpec((1, D_MODEL), lambda i: (0, 0)),
                  pl.BlockSpec((D_MODEL, P_WIDTH), lambda i: (0, 0), pipeline_mode=pl.Buffered(1)),
                  row(128), row(128), res(d1, 128), res(d1, 128), res(d2, 128), res(d2, 128)] + _dep_spec(dep),
        out_specs=[row(D_MODEL), res(d1, D_MODEL), res(d2, D_MODEL), row(gw), res(d1, gw), res(d2, gw), row(PBLK)],
        out_shape=[sds((t, D_MODEL), BF16), sds((d1, t // d1, D_MODEL), BF16), sds((d2, t // d2, D_MODEL), BF16),
                   sds((t, gw), BF16), sds((d1, t // d1, gw), BF16), sds((d2, t // d2, gw), BF16),
                   sds((t, PBLK), BF16)],
        scratch_shapes=[_lane_scratch(tm, D_MODEL)],
        compiler_params=_params(("arbitrary",)),
    )(x, g, w_p, cos, sin, cos1, sin1, cos2, sin2, *_dep_arg(dep))


def _gates(h, w_g, b, tm, tn):
    t = h.shape[0]

    def body(h_ref, w_ref, b_ref, o_ref):
        z = _dot(h_ref[...], w_ref[...]) + b_ref[...]
        o_ref[...] = jax.nn.sigmoid(z).astype(BF16)

    return pl.pallas_call(
        body, name="gates", grid=(t // tm, GATE_WIDTH // tn),
        in_specs=[pl.BlockSpec((tm, D_MODEL), lambda i, j: (i, 0)),
                  pl.BlockSpec((D_MODEL, tn), lambda i, j: (0, j)),
                  pl.BlockSpec((1, tn), lambda i, j: (0, j))],
        out_specs=pl.BlockSpec((tm, tn), lambda i, j: (i, j)),
        out_shape=jax.ShapeDtypeStruct((t, GATE_WIDTH), BF16),
        compiler_params=_params(("arbitrary", "arbitrary")),
    )(h, w_g, b)


def _band_mask(i, s):
    row = lax.broadcasted_iota(jnp.int32, (BAND, 2 * BAND), 0)
    col = lax.broadcasted_iota(jnp.int32, (BAND, 2 * BAND), 1)
    band = (col >= row) & (col <= row + BAND)
    if s == 0:
        band = band & ((col >= BAND) | (i > 0))
    return band


def _head_a_masks(rows):
    lane = lax.broadcasted_iota(jnp.int32, (rows, 128), 1)
    return (lane % HEAD_DIM) < HEAD_DIM // 2, lane < HEAD_DIM


def _stack_heads(x, head_a):
    zero = jnp.zeros_like(x)
    return jnp.concatenate([jnp.where(head_a, x, zero), jnp.where(head_a, zero, x)], axis=0)


def _kv_rows(cur_ref, tail_ref, s, off):
    if s == 0:
        return jnp.concatenate([tail_ref[:, off:off + 128], cur_ref[0:BAND, off:off + 128]], axis=0)
    return cur_ref[(s - 1) * BAND:(s + 1) * BAND, off:off + 128]


def _attn_layout(swa):
    if swa:
        return [(128 * j, 512, 640) for j in range(4)]
    return [(0, 128, 256), (384, 512, 640)]


def _attn_fwd(name, pv, swa, sinks, tq):
    d, ls = pv.shape[0], pv.shape[1]
    n, nsb = ls // tq, tq // BAND
    pairs = _attn_layout(swa)
    ncol = 1 if swa else 2
    ow = 128 * len(pairs)

    def body(cur_ref, tail_ref, *rest):
        sink_ref, o_ref, lse_ref, o32_ref = rest if swa else (None,) + rest + (None,)
        i = pl.program_id(2)
        lane = lax.broadcasted_iota(jnp.int32, (BAND, 128), 1)
        qk_a, v_a = _head_a_masks(BAND)
        first = lax.broadcasted_iota(jnp.int32, (2 * BAND, 1), 0) < BAND
        for s in range(nsb):
            mask = _band_mask(i, s)
            mask2 = jnp.concatenate([mask, mask], axis=0)
            rows = slice(s * BAND, (s + 1) * BAND)
            lse_tile = jnp.zeros((BAND, 128), F32)
            for j, (qo, ko, vo) in enumerate(pairs):
                q = cur_ref[rows, qo:qo + 128]
                kk = _kv_rows(cur_ref, tail_ref, s, ko)
                vv = _kv_rows(cur_ref, tail_ref, s, vo)
                sc = _dot_nt(_stack_heads(q, qk_a), kk)
                sc = jnp.where(mask2, sc, -jnp.inf)
                m = jnp.max(sc, axis=-1, keepdims=True)
                if swa:
                    sk = jnp.where(first, sink_ref[2 * j], sink_ref[2 * j + 1])
                    m = jnp.maximum(m, sk)
                p = jnp.exp(sc - m)
                den = jnp.sum(p, axis=-1, keepdims=True)
                if swa:
                    den = den + jnp.exp(sk - m)
                lse = m + jnp.log(den)
                lse_tile = jnp.where(lane == 2 * j, lse[:BAND], jnp.where(lane == 2 * j + 1, lse[BAND:], lse_tile))
                o2 = _dot((p * (1.0 / den)).astype(BF16), vv)
                o = jnp.where(v_a, o2[:BAND], o2[BAND:])
                o_ref[rows, j * 128:(j + 1) * 128] = o.astype(BF16)
                if swa:
                    o32_ref[rows, j * 128:(j + 1) * 128] = o
            lse_ref[rows, :] = lse_tile

    in_specs = [pl.BlockSpec((None, tq, PBLK), lambda r, cb, i: (r, i, cb)),
                pl.BlockSpec((None, BAND, PBLK), lambda r, cb, i: (r, jnp.maximum(i * nsb - 1, 0), cb))]
    args = [pv, pv]
    out_specs = [pl.BlockSpec((None, tq, ow), lambda r, cb, i: (r, i, cb)),
                 pl.BlockSpec((None, tq, 128), lambda r, cb, i: (r, i, cb))]
    out_shape = [jax.ShapeDtypeStruct((d, ls, 512), BF16), jax.ShapeDtypeStruct((d, ls, 128 * ncol), F32)]
    if swa:
        in_specs.append(pl.BlockSpec(memory_space=pltpu.SMEM))
        args.append(sinks)
        out_specs.append(out_specs[0])
        out_shape.append(jax.ShapeDtypeStruct((d, ls, 512), F32))
    return pl.pallas_call(
        body, name=name, grid=(d, ncol, n),
        in_specs=in_specs, out_specs=out_specs, out_shape=out_shape,
        compiler_params=_params(("arbitrary", "arbitrary", "arbitrary")),
    )(*args)


def _lse_lane(h):
    return (h // 4) * 128 + h % 4


def _head_scale(x, tile, lanes):
    lane = lax.broadcasted_iota(jnp.int32, (x.shape[0], 128), 1)
    lo = lane < HEAD_DIM
    out = []
    for c in range(x.shape[1] // 128):
        a0 = tile[:, lanes[2 * c]:lanes[2 * c] + 1]
        a1 = tile[:, lanes[2 * c + 1]:lanes[2 * c + 1] + 1]
        out.append(x[:, c * 128:(c + 1) * 128] * jnp.where(lo, a0, a1))
    return jnp.concatenate(out, axis=1)


def _head_sums(x, lanes, width):
    lane = lax.broadcasted_iota(jnp.int32, (x.shape[0], width), 1)
    out = jnp.zeros((x.shape[0], width), F32)
    for h in range(x.shape[1] // HEAD_DIM):
        sm = jnp.sum(x[:, h * HEAD_DIM:(h + 1) * HEAD_DIM], axis=-1, keepdims=True)
        out = jnp.where(lane == lanes[h], sm, out)
    return out


def _alphas(l0, l1, l2):
    m = jnp.maximum(jnp.maximum(l0, l1), l2)
    e0, e1, e2 = jnp.exp(l0 - m), jnp.exp(l1 - m), jnp.exp(l2 - m)
    den = e0 + e1 + e2
    return e0 / den, e1 / den, e2 / den


DIL_LANES = [_lse_lane(h) for h in range(8)]
SWA_LANES = list(range(8))


def _mix(o0, o1, o2, l0, l1, l2, ob, gts, x, w_a, w_b, w_out, g_cross, tm):
    t = x.shape[0]

    def body(o0_ref, o1_ref, o2_ref, l0_ref, l1_ref, l2_ref, ob_ref, g_ref, x_ref, wa_ref, wb_ref, wo_ref,
             gc_ref, oa_ref, ya_ref, yb_ref, mg_ref, x1_ref, hc_ref, so_ref, sl_ref):
        a0, a1, a2 = _alphas(l0_ref[...], _interleave(l1_ref, sl_ref), _interleave(l2_ref, sl_ref))
        oa = (_head_scale(o0_ref[...].astype(F32), a0, DIL_LANES)
              + _head_scale(_interleave(o1_ref, so_ref), a1, DIL_LANES)
              + _head_scale(_interleave(o2_ref, so_ref), a2, DIL_LANES))
        oab = oa.astype(BF16)
        oa_ref[...] = oab
        ya = _dot(oab, wa_ref[...])
        yb = _dot(ob_ref[...], wb_ref[...])
        ya_ref[...] = ya.astype(BF16)
        yb_ref[...] = yb.astype(BF16)
        merged = (g_ref[:, :D_MODEL].astype(F32) * ya + g_ref[:, D_MODEL:].astype(F32) * yb).astype(BF16)
        mg_ref[...] = merged
        x1 = x_ref[...] + _dot(merged, wo_ref[...])
        x1_ref[...] = x1
        hc_ref[...] = (x1 * _rms(x1) * gc_ref[...]).astype(BF16)

    row = lambda w: pl.BlockSpec((tm, w), lambda i: (i, 0))
    full = lambda a, b: pl.BlockSpec((a, b), lambda i: (0, 0))
    return pl.pallas_call(
        body, name="mix", grid=(t // tm,),
        in_specs=[row(512), _res_spec(o1, tm), _res_spec(o2, tm), row(256), _res_spec(l1, tm), _res_spec(l2, tm),
                  row(512), row(GATE_WIDTH),
                  row(D_MODEL), full(512, D_MODEL), full(512, D_MODEL), full(D_MODEL, D_MODEL), full(1, D_MODEL)],
        out_specs=[row(512), row(D_MODEL), row(D_MODEL), row(D_MODEL), row(D_MODEL), row(D_MODEL)],
        out_shape=[jax.ShapeDtypeStruct((t, 512), BF16), jax.ShapeDtypeStruct((t, D_MODEL), BF16),
                   jax.ShapeDtypeStruct((t, D_MODEL), BF16), jax.ShapeDtypeStruct((t, D_MODEL), BF16),
                   jax.ShapeDtypeStruct((t, D_MODEL), F32), jax.ShapeDtypeStruct((t, D_MODEL), BF16)],
        scratch_shapes=[_lane_scratch(tm, 512), _lane_scratch(tm, 256)],
        compiler_params=_params(("arbitrary",)),
    )(o0, o1, o2, l0, l1, l2, ob, gts, x, w_a, w_b, w_out, g_cross)


def _memkv(mem, g_mem, w_ckv):
    m = mem.shape[0]
    ws = w_ckv.shape[2]

    def body(mem_ref, g_ref, w_ref, mn_ref, kv_ref):
        xt = mem_ref[...]
        mn = (xt * _rms(xt) * g_ref[...]).astype(BF16)
        mn_ref[...] = mn
        for j in range(N_DEV):
            kv_ref[:, j * ws:(j + 1) * ws] = _dot(mn, w_ref[j]).astype(BF16)

    return pl.pallas_call(
        body, name="memkv",
        out_shape=[jax.ShapeDtypeStruct((m, D_MODEL), BF16), jax.ShapeDtypeStruct((m, 2 * D_MODEL), BF16)],
        compiler_params=_params(),
    )(mem, g_mem, w_ckv)


def _cross_probs(q, kv_ref, h):
    k = kv_ref[:, h * X_HEAD_DIM:(h + 1) * X_HEAD_DIM]
    sc = _dot_nt(q[:, h * X_HEAD_DIM:(h + 1) * X_HEAD_DIM], k)
    m = jnp.max(sc, axis=-1, keepdims=True)
    p = jnp.exp(sc - m)
    return p / jnp.sum(p, axis=-1, keepdims=True)


def _cross(hc, x1, kv, w_cq, w_co, g_mlp, tm):
    t = x1.shape[0]
    m = kv.shape[0]

    def body(hc_ref, x1_ref, kv_ref, wq_ref, wo_ref, g_ref, q_ref, o_ref, x2_ref, hm_ref):
        q = (_dot(hc_ref[...], wq_ref[...]) * X_SCALE).astype(BF16)
        q_ref[...] = q
        outs = []
        for h in range(X_HEADS):
            p = _cross_probs(q, kv_ref, h)
            v = kv_ref[:, D_MODEL + h * X_HEAD_DIM:D_MODEL + (h + 1) * X_HEAD_DIM]
            outs.append(_dot(p.astype(BF16), v))
        o = jnp.concatenate(outs, axis=1).astype(BF16)
        o_ref[...] = o
        x2 = x1_ref[...] + _dot(o, wo_ref[...])
        x2_ref[...] = x2
        hm_ref[...] = (x2 * _rms(x2) * g_ref[...]).astype(BF16)

    row = lambda w: pl.BlockSpec((tm, w), lambda i: (i, 0))
    full = lambda a, b: pl.BlockSpec((a, b), lambda i: (0, 0))
    return pl.pallas_call(
        body, name="cross", grid=(t // tm,),
        in_specs=[row(D_MODEL), row(D_MODEL), full(m, 2 * D_MODEL), full(D_MODEL, D_MODEL),
                  full(D_MODEL, D_MODEL), full(1, D_MODEL)],
        out_specs=[row(D_MODEL)] * 4,
        out_shape=[jax.ShapeDtypeStruct((t, D_MODEL), BF16), jax.ShapeDtypeStruct((t, D_MODEL), BF16),
                   jax.ShapeDtypeStruct((t, D_MODEL), F32), jax.ShapeDtypeStruct((t, D_MODEL), BF16)],
        compiler_params=_params(("arbitrary",)),
    )(hc, x1, kv, w_cq, w_co, g_mlp)


def _mlp(hm, x2, w_1, w_2, g_final, target, tm, tf):
    t = x2.shape[0]
    nf = D_FF // tf

    def body(hm_ref, x2_ref, w1_ref, w2_ref, g_ref, tg_ref, a_ref, dx3_ref, loss_ref, dg_ref, acc_ref):
        i, f = pl.program_id(0), pl.program_id(1)
        hm_t = hm_ref[...]
        a = jnp.concatenate([jnp.maximum(_dot(hm_t, w1_ref[s]), 0.0) for s in range(w1_ref.shape[0])], axis=1)
        a_ref[...] = a.astype(BF16)
        part = _dot((a * a).astype(BF16), w2_ref[...])

        @pl.when(f == 0)
        def _():
            acc_ref[...] = part

        @pl.when(f > 0)
        def _():
            acc_ref[...] += part

        @pl.when((i == 0) & (f == 0))
        def _():
            loss_ref[...] = jnp.zeros_like(loss_ref)
            dg_ref[...] = jnp.zeros_like(dg_ref)

        @pl.when(f == nf - 1)
        def _():
            x3 = x2_ref[...] + acc_ref[...]
            r = _rms(x3)
            g = g_ref[...]
            diff = x3 * r * g - tg_ref[...]
            loss_ref[...] += 0.5 * jnp.sum(jnp.mean(diff * diff, axis=-1, keepdims=True))
            dx3, dg = _rms_bwd(diff / D_MODEL, x3, r, g)
            dx3_ref[...] = dx3
            dg_ref[...] += dg

    return pl.pallas_call(
        body, name="mlp", grid=(t // tm, nf),
        in_specs=[pl.BlockSpec((tm, D_MODEL), lambda i, f: (i, 0)),
                  pl.BlockSpec((tm, D_MODEL), lambda i, f: (i, 0)),
                  pl.BlockSpec((tf // w_1.shape[2], D_MODEL, w_1.shape[2]), lambda i, f: (f, 0, 0)),
                  pl.BlockSpec((tf, D_MODEL), lambda i, f: (f, 0)),
                  pl.BlockSpec((1, D_MODEL), lambda i, f: (0, 0)),
                  pl.BlockSpec((tm, D_MODEL), lambda i, f: (i, 0))],
        out_specs=[pl.BlockSpec((tm, tf), lambda i, f: (i, f)),
                   pl.BlockSpec((tm, D_MODEL), lambda i, f: (i, 0)),
                   pl.BlockSpec((1, 128), lambda i, f: (0, 0)),
                   pl.BlockSpec((1, D_MODEL), lambda i, f: (0, 0))],
        out_shape=[jax.ShapeDtypeStruct((t, D_FF), BF16), jax.ShapeDtypeStruct((t, D_MODEL), F32),
                   jax.ShapeDtypeStruct((1, 128), F32), jax.ShapeDtypeStruct((1, D_MODEL), F32)],
        scratch_shapes=[pltpu.VMEM((tm, D_MODEL), F32)],
        compiler_params=_params(("arbitrary", "arbitrary")),
    )(hm, x2, w_1, w_2, g_final, target)


def _mlp_bwd(dx3, a, w_1, w_2, x2, g_mlp, tm, tf):
    t = x2.shape[0]
    nf = D_FF // tf

    def body(dx3_ref, a_ref, w1_ref, w2_ref, x2_ref, g_ref, dz_ref, dx2_ref, dg_ref, acc_ref):
        i, f = pl.program_id(0), pl.program_id(1)
        da2 = _dot_nt(dx3_ref[...].astype(BF16), w2_ref[...])
        dz = (2.0 * a_ref[...].astype(F32) * da2).astype(BF16)
        dz_ref[...] = dz
        sw = w1_ref.shape[2]
        part = _dot_nt(dz[:, 0:sw], w1_ref[0])
        for s in range(1, w1_ref.shape[0]):
            part = part + _dot_nt(dz[:, s * sw:(s + 1) * sw], w1_ref[s])

        @pl.when(f == 0)
        def _():
            acc_ref[...] = part

        @pl.when(f > 0)
        def _():
            acc_ref[...] += part

        @pl.when((i == 0) & (f == 0))
        def _():
            dg_ref[...] = jnp.zeros_like(dg_ref)

        @pl.when(f == nf - 1)
        def _():
            xt = x2_ref[...]
            dx, dg = _rms_bwd(acc_ref[...], xt, _rms(xt), g_ref[...])
            dx2_ref[...] = dx3_ref[...] + dx
            dg_ref[...] += dg

    return pl.pallas_call(
        body, name="mlp_bwd", grid=(t // tm, nf),
        in_specs=[pl.BlockSpec((tm, D_MODEL), lambda i, f: (i, 0)),
                  pl.BlockSpec((tm, tf), lambda i, f: (i, f)),
                  pl.BlockSpec((tf // w_1.shape[2], D_MODEL, w_1.shape[2]), lambda i, f: (f, 0, 0)),
                  pl.BlockSpec((tf, D_MODEL), lambda i, f: (f, 0)),
                  pl.BlockSpec((tm, D_MODEL), lambda i, f: (i, 0)),
                  pl.BlockSpec((1, D_MODEL), lambda i, f: (0, 0))],
        out_specs=[pl.BlockSpec((tm, tf), lambda i, f: (i, f)),
                   pl.BlockSpec((tm, D_MODEL), lambda i, f: (i, 0)),
                   pl.BlockSpec((1, D_MODEL), lambda i, f: (0, 0))],
        out_shape=[jax.ShapeDtypeStruct((t, D_FF), BF16), jax.ShapeDtypeStruct((t, D_MODEL), F32),
                   jax.ShapeDtypeStruct((1, D_MODEL), F32)],
        scratch_shapes=[pltpu.VMEM((tm, D_MODEL), F32)],
        compiler_params=_params(("arbitrary", "arbitrary")),
    )(dx3, a, w_1, w_2, x2, g_mlp)


def _wgrad(name, a, b, tka, tn, tm, square=False, col_shards=False):
    t, ka = a.shape
    n = b.shape[1]
    nk = t // tm

    def body(a_ref, b_ref, o_ref, acc_ref):
        at = a_ref[...].astype(BF16)
        if square:
            at = at * at
        part = _dot_tn(at, b_ref[...].astype(BF16))
        k = pl.program_id(2)

        @pl.when(k == 0)
        def _():
            acc_ref[...] = part

        @pl.when(k > 0)
        def _():
            acc_ref[...] += part

        @pl.when(k == nk - 1)
        def _():
            if col_shards:
                for s in range(tn // sw):
                    o_ref[s] = acc_ref[:, s * sw:(s + 1) * sw].astype(BF16)
            else:
                o_ref[...] = acc_ref[...].astype(BF16)

    if col_shards:
        sw = n // N_DEV
        out_spec = pl.BlockSpec((tn // sw, tka, sw), lambda p, q, k: (q, p, 0))
        out_shape = jax.ShapeDtypeStruct((N_DEV, ka, sw), BF16)
    else:
        out_spec = pl.BlockSpec((tka, tn), lambda p, q, k: (p, q))
        out_shape = jax.ShapeDtypeStruct((ka, n), BF16)
    return pl.pallas_call(
        body, name=name, grid=(ka // tka, n // tn, nk),
        in_specs=[pl.BlockSpec((tm, tka), lambda p, q, k: (k, p)),
                  pl.BlockSpec((tm, tn), lambda p, q, k: (k, q))],
        out_specs=out_spec, out_shape=out_shape,
        scratch_shapes=[pltpu.VMEM((tka, tn), F32)],
        compiler_params=_params(("arbitrary", "arbitrary", "arbitrary")),
    )(a, b)


def _cross_bwd(dx2, x1, q, kv, w_cq, w_co, g_cross, tm, dep=None):
    t = x1.shape[0]
    m = kv.shape[0]

    def body(dx2_ref, x1_ref, q_ref, kv_ref, wq_ref, wo_ref, g_ref, dq_ref, dx1_ref, dkv_ref, dg_ref):
        @pl.when(pl.program_id(0) == 0)
        def _():
            dkv_ref[...] = jnp.zeros_like(dkv_ref)
            dg_ref[...] = jnp.zeros_like(dg_ref)

        do = _dot_nt(dx2_ref[...].astype(BF16), wo_ref[...]).astype(BF16)
        q = q_ref[...]
        dqs = []
        for h in range(X_HEADS):
            hs = slice(h * X_HEAD_DIM, (h + 1) * X_HEAD_DIM)
            vs = slice(D_MODEL + h * X_HEAD_DIM, D_MODEL + (h + 1) * X_HEAD_DIM)
            p = _cross_probs(q, kv_ref, h)
            dp = _dot_nt(do[:, hs], kv_ref[:, vs])
            ds = (p * (dp - jnp.sum(dp * p, axis=-1, keepdims=True))).astype(BF16)
            dqs.append(_dot(ds, kv_ref[:, hs]))
            dkv_ref[:, hs] += _dot_tn(ds, q[:, hs])
            dkv_ref[:, vs] += _dot_tn(p.astype(BF16), do[:, hs])
        dq = (jnp.concatenate(dqs, axis=1) * X_SCALE).astype(BF16)
        dq_ref[...] = dq
        xt = x1_ref[...]
        dx, dg = _rms_bwd(_dot_nt(dq, wq_ref[...]), xt, _rms(xt), g_ref[...])
        dx1_ref[...] = dx2_ref[...] + dx
        dg_ref[...] += dg

    row = lambda w: pl.BlockSpec((tm, w), lambda i: (i, 0))
    full = lambda a, b: pl.BlockSpec((a, b), lambda i: (0, 0))
    return pl.pallas_call(
        _with_dep(body, 7, dep), name="cross_bwd", grid=(t // tm,),
        in_specs=[row(D_MODEL), row(D_MODEL), row(D_MODEL), full(m, 2 * D_MODEL), full(D_MODEL, D_MODEL),
                  full(D_MODEL, D_MODEL), full(1, D_MODEL)] + _dep_spec(dep),
        out_specs=[row(D_MODEL), row(D_MODEL), full(m, 2 * D_MODEL), full(1, D_MODEL)],
        out_shape=[jax.ShapeDtypeStruct((t, D_MODEL), BF16), jax.ShapeDtypeStruct((t, D_MODEL), F32),
                   jax.ShapeDtypeStruct((m, 2 * D_MODEL), F32), jax.ShapeDtypeStruct((1, D_MODEL), F32)],
        compiler_params=_params(("arbitrary",)),
    )(dx2, x1, q, kv, w_cq, w_co, g_cross, *_dep_arg(dep))


def _memkv_bwd(dkv, mn, mem, w_ckv, g_mem):
    ws = w_ckv.shape[2]

    def body(dkv_ref, mn_ref, mem_ref, w_ref, g_ref, dw_ref, dg_ref):
        mn = mn_ref[...]
        dmn = jnp.zeros(mn.shape, F32)
        for j in range(N_DEV):
            dkvb = dkv_ref[:, j * ws:(j + 1) * ws].astype(BF16)
            dw_ref[j] = _dot_tn(mn, dkvb).astype(BF16)
            dmn = dmn + _dot_nt(dkvb, w_ref[j])
        xt = mem_ref[...]
        dg_ref[...] = jnp.sum(dmn * xt * _rms(xt), axis=0, keepdims=True)

    return pl.pallas_call(
        body, name="memkv_bwd",
        out_shape=[jax.ShapeDtypeStruct(w_ckv.shape, BF16), jax.ShapeDtypeStruct((1, D_MODEL), F32)],
        compiler_params=_params(),
    )(dkv, mn, mem, w_ckv, g_mem)


def _merge_bwd(dx1, ya, yb, gts, w_out, w_g, tm):
    t = dx1.shape[0]

    def body(dx1_ref, ya_ref, yb_ref, g_ref, wo_ref, wg_ref, dg_ref, dhp_ref, dya_ref, dyb_ref, db_ref):
        @pl.when(pl.program_id(0) == 0)
        def _():
            db_ref[...] = jnp.zeros_like(db_ref)

        dm = _dot_nt(dx1_ref[...].astype(BF16), wo_ref[...])
        ga = g_ref[:, :D_MODEL].astype(F32)
        gb = g_ref[:, D_MODEL:].astype(F32)
        dya_ref[...] = (dm * ga).astype(BF16)
        dyb_ref[...] = (dm * gb).astype(BF16)
        dpa = dm * ya_ref[...].astype(F32) * ga * (1.0 - ga)
        dpb = dm * yb_ref[...].astype(F32) * gb * (1.0 - gb)
        dpre = jnp.concatenate([dpa, dpb], axis=1)
        db_ref[...] += jnp.sum(dpre, axis=0, keepdims=True)
        dpreb = dpre.astype(BF16)
        dg_ref[...] = dpreb
        dhp_ref[...] = _dot_nt(dpreb, wg_ref[...])

    row = lambda w: pl.BlockSpec((tm, w), lambda i: (i, 0))
    once = lambda a, b: pl.BlockSpec((a, b), lambda i: (0, 0), pipeline_mode=pl.Buffered(1))
    sds = jax.ShapeDtypeStruct
    return pl.pallas_call(
        body, name="merge_bwd", grid=(t // tm,),
        in_specs=[row(D_MODEL), row(D_MODEL), row(D_MODEL), row(GATE_WIDTH),
                  once(D_MODEL, D_MODEL), once(D_MODEL, GATE_WIDTH)],
        out_specs=[row(GATE_WIDTH), row(D_MODEL), row(D_MODEL), row(D_MODEL),
                   pl.BlockSpec((1, GATE_WIDTH), lambda i: (0, 0))],
        out_shape=[sds((t, GATE_WIDTH), BF16), sds((t, D_MODEL), F32), sds((t, D_MODEL), BF16),
                   sds((t, D_MODEL), BF16), sds((1, GATE_WIDTH), F32)],
        compiler_params=_params(("arbitrary",)),
    )(dx1, ya, yb, gts, w_out, w_g)


def _combine_bwd(dya, dyb, oa, ob, l0, l1, l2, lb, sink_row, w_a, w_b, tm):
    t = dya.shape[0]

    def body(dya_ref, dyb_ref, oa_ref, ob_ref, l0_ref, l1_ref, l2_ref, lb_ref, sk_ref, wa_ref, wb_ref,
             do0_ref, do1_ref, do2_ref, c0_ref, c1_ref, c2_ref, dob_ref, cb_ref, dsk_ref, so_ref, sl_ref):
        @pl.when(pl.program_id(0) == 0)
        def _():
            dsk_ref[...] = jnp.zeros_like(dsk_ref)

        doa = _dot_nt(dya_ref[...], wa_ref[...])
        dob = _dot_nt(dyb_ref[...], wb_ref[...])
        dsum = _head_sums(doa * oa_ref[...].astype(F32), DIL_LANES, 256)
        a0, a1, a2 = _alphas(l0_ref[...], _interleave(l1_ref, sl_ref), _interleave(l2_ref, sl_ref))
        c0_ref[...] = a0 * dsum
        do0_ref[...] = _head_scale(doa, a0, DIL_LANES).astype(BF16)
        for al, do_ref, c_ref in ((a1, do1_ref, c1_ref), (a2, do2_ref, c2_ref)):
            _deinterleave(al * dsum, sl_ref, c_ref, F32)
            _deinterleave(_head_scale(doa, al, DIL_LANES), so_ref, do_ref, BF16)
        dob_ref[...] = dob.astype(BF16)
        cb = _head_sums(dob * ob_ref[...], SWA_LANES, 128)
        cb_ref[...] = cb
        lane = lax.broadcasted_iota(jnp.int32, cb.shape, 1)
        psink = jnp.where(lane < 8, jnp.exp(sk_ref[...] - lb_ref[...]), 0.0)
        dsk_ref[...] += jnp.sum(-psink * cb, axis=0, keepdims=True)

    row = lambda w: pl.BlockSpec((tm, w), lambda i: (i, 0))
    full = lambda a, b: pl.BlockSpec((a, b), lambda i: (0, 0))
    sds = jax.ShapeDtypeStruct
    d1, d2 = l1.shape[0], l2.shape[0]
    res = lambda d, w: pl.BlockSpec((d, tm // d, w), lambda i: (0, i, 0))
    return pl.pallas_call(
        body, name="combine_bwd", grid=(t // tm,),
        in_specs=[row(D_MODEL), row(D_MODEL), row(512), row(512),
                  row(256), _res_spec(l1, tm), _res_spec(l2, tm), row(128), full(1, 128),
                  full(512, D_MODEL), full(512, D_MODEL)],
        out_specs=[row(512), res(d1, 512), res(d2, 512), row(256), res(d1, 256), res(d2, 256),
                   row(512), row(128), full(1, 128)],
        out_shape=[sds((t, 512), BF16), sds((d1, t // d1, 512), BF16),
                   sds((d2, t // d2, 512), BF16), sds((t, 256), F32), sds((d1, t // d1, 256), F32),
                   sds((d2, t // d2, 256), F32), sds((t, 512), BF16),
                   sds((t, 128), F32), sds((1, 128), F32)],
        scratch_shapes=[_lane_scratch(tm, 512), _lane_scratch(tm, 256)],
        compiler_params=_params(("arbitrary",)),
    )(dya, dyb, oa, ob, l0, l1, l2, lb, sink_row, w_a, w_b)


def _attn_bwd(name, pv, dov, lsev, cv, cosv, sinv, swa, tq, dep=None):
    d, ls = pv.shape[0], pv.shape[1]
    n, nsb = ls // tq, tq // BAND
    pairs = _attn_layout(swa)
    ncol = 1 if swa else 2
    ow = 128 * len(pairs)

    def body(cur_ref, tail_ref, do_ref, lse_ref, c_ref, cos_ref, sin_ref, out_ref, acc_ref, carry_ref):
        i = pl.program_id(2)
        acc_ref[...] = jnp.zeros_like(acc_ref)

        @pl.when(i < n)
        def _():
            qk_a, v_a = _head_a_masks(BAND)
            for s in range(nsb):
                mask = _band_mask(i, s)
                mask2 = jnp.concatenate([mask, mask], axis=0)
                rows = slice(s * BAND, (s + 1) * BAND)
                krows = slice(s * BAND, (s + 2) * BAND)
                for j, (qo, ko, vo) in enumerate(pairs):
                    kk = _kv_rows(cur_ref, tail_ref, s, ko)
                    vv = _kv_rows(cur_ref, tail_ref, s, vo)
                    q2 = _stack_heads(cur_ref[rows, qo:qo + 128], qk_a)
                    do2 = _stack_heads(do_ref[rows, j * 128:(j + 1) * 128], v_a)
                    col2 = lambda ref: jnp.concatenate([ref[rows, 2 * j:2 * j + 1], ref[rows, 2 * j + 1:2 * j + 2]], axis=0)
                    sc = _dot_nt(q2, kk)
                    p = jnp.exp(jnp.where(mask2, sc, -jnp.inf) - col2(lse_ref))
                    dp = _dot_nt(do2, vv)
                    ds = (p * (dp - col2(c_ref))).astype(BF16)
                    dq2 = _dot(ds, kk)
                    acc_ref[BAND + s * BAND:BAND + (s + 1) * BAND, qo:qo + 128] += jnp.where(qk_a, dq2[:BAND], dq2[BAND:])
                    acc_ref[krows, ko:ko + 128] += _dot_tn(ds, q2)
                    acc_ref[krows, vo:vo + 128] += _dot_tn(p.astype(BF16), do2)

        @pl.when(i >= 1)
        def _():
            if tq > BAND:
                fin = jnp.concatenate([carry_ref[0:tq - BAND, :], carry_ref[tq - BAND:, :] + acc_ref[0:BAND, :]], axis=0)
            else:
                fin = carry_ref[...] + acc_ref[0:BAND, :]
            out_ref[...] = _rope(fin, cos_ref[...], sin_ref[...], swa, -1).astype(BF16)

        carry_ref[...] = acc_ref[BAND:, :]

    qi = lambda i: jnp.minimum(i, n - 1)
    pi = lambda i: jnp.maximum(i - 1, 0)
    blk = lambda rows, w, row_of: pl.BlockSpec((None, rows, w), lambda r, cb, i: (r, row_of(i), cb))
    return pl.pallas_call(
        _with_dep(body, 7, dep), name=name, grid=(d, ncol, n + 1),
        in_specs=[blk(tq, PBLK, qi), blk(BAND, PBLK, lambda i: jnp.maximum(qi(i) * nsb - 1, 0)),
                  blk(tq, ow, qi), blk(tq, 128, qi), blk(tq, 128, qi),
                  pl.BlockSpec((None, tq, 128), lambda r, cb, i: (r, pi(i), 0)),
                  pl.BlockSpec((None, tq, 128), lambda r, cb, i: (r, pi(i), 0))] + _dep_spec(dep),
        out_specs=blk(tq, PBLK, pi),
        out_shape=jax.ShapeDtypeStruct((d, ls, ncol * PBLK), BF16),
        scratch_shapes=[pltpu.VMEM((tq + BAND, PBLK), F32), pltpu.VMEM((tq, PBLK), F32)],
        compiler_params=_params(("arbitrary", "arbitrary", "arbitrary")),
    )(pv, pv, dov, lsev, cv, cosv, sinv, *_dep_arg(dep))


def _dx(dp0, dp1, dp2, dpb, w_p, dh_part, dx1, x, g_mix, tm, dep=None):
    t = x.shape[0]
    gw = 2 * PBLK

    def body(dp0_ref, dp1_ref, dp2_ref, dpb_ref, w_ref, dhp_ref, dx1_ref, x_ref, g_ref, gx_ref, dg_ref,
             dpt_ref, scr_ref):
        @pl.when(pl.program_id(0) == 0)
        def _():
            dg_ref[...] = jnp.zeros_like(dg_ref)

        dpt_ref[:, 0:gw] = dp0_ref[...]
        dpt_ref[:, gw:2 * gw] = _interleave(dp1_ref, scr_ref).astype(BF16)
        dpt_ref[:, 2 * gw:3 * gw] = _interleave(dp2_ref, scr_ref).astype(BF16)
        dpt_ref[:, 3 * gw:] = dpb_ref[...]
        dh = _dot_nt(dpt_ref[...], w_ref[...]) + dhp_ref[...]
        xt = x_ref[...]
        dx, dg = _rms_bwd(dh, xt, _rms(xt), g_ref[...])
        gx_ref[...] = dx1_ref[...] + dx
        dg_ref[...] += dg

    row = lambda w: pl.BlockSpec((tm, w), lambda i: (i, 0))
    full = lambda a, b: pl.BlockSpec((a, b), lambda i: (0, 0))
    return pl.pallas_call(
        _with_dep(body, 9, dep), name="dx", grid=(t // tm,),
        in_specs=[row(gw), _res_spec(dp1, tm), _res_spec(dp2, tm), row(PBLK),
                  pl.BlockSpec((D_MODEL, P_WIDTH), lambda i: (0, 0), pipeline_mode=pl.Buffered(1)),
                  row(D_MODEL), row(D_MODEL), row(D_MODEL), full(1, D_MODEL)] + _dep_spec(dep),
        out_specs=[row(D_MODEL), full(1, D_MODEL)],
        out_shape=[jax.ShapeDtypeStruct((t, D_MODEL), F32), jax.ShapeDtypeStruct((1, D_MODEL), F32)],
        scratch_shapes=[pltpu.VMEM((tm, P_WIDTH), BF16), _lane_scratch(tm, gw)],
        compiler_params=_params(("arbitrary",)),
    )(dp0, dp1, dp2, dpb, w_p, dh_part, dx1, x, g_mix, *_dep_arg(dep))


MESH = pl.DeviceIdType.MESH
HBM_SPEC = pl.BlockSpec(memory_space=pltpu.HBM)
VMEM_SPEC = pl.BlockSpec(memory_space=pltpu.VMEM)


def _all_gather(xp):
    def body(x_ref, out_ref, send_sems, recv_sems, local_sem):
        x, y, c = lax.axis_index("x"), lax.axis_index("y"), lax.axis_index("c")
        me, sibling = (x, y, c), (x, y, 1 - c)
        chips = [(1 - x, y), (x, 1 - y), (1 - x, 1 - y)]

        def rows(px, py, pc):
            return out_ref.at[4 * px + 2 * py + pc]

        def copy(k, block, to, src=None):
            return pltpu.make_async_remote_copy(
                src_ref=rows(*block) if src is None else src, dst_ref=rows(*block),
                send_sem=send_sems.at[k], recv_sem=recv_sems.at[k], device_id=to, device_id_type=MESH)

        mine = pltpu.make_async_copy(x_ref, rows(*me), local_sem)
        mine.start()
        first = [copy(0, me, sibling, src=x_ref)]
        first += [copy(1 + j, me, (*chip, c), src=x_ref) for j, chip in enumerate(chips)]
        for cp in first:
            cp.start()
        passed = [copy(4 + j, (*chip, c), sibling) for j, chip in enumerate(chips)]
        for j, chip in enumerate(chips):
            copy(1 + j, (*chip, c), me).wait_recv()
            passed[j].start()
        copy(0, sibling, me).wait_recv()
        for j, chip in enumerate(chips):
            copy(4 + j, (*chip, 1 - c), me).wait_recv()
        for cp in first + passed:
            cp.wait_send()
        mine.wait()

    return pl.pallas_call(
        body, name="all_gather",
        out_shape=jax.ShapeDtypeStruct((N_DEV,) + xp.shape, xp.dtype),
        in_specs=[HBM_SPEC], out_specs=HBM_SPEC,
        scratch_shapes=[pltpu.SemaphoreType.DMA((7,)), pltpu.SemaphoreType.DMA((7,)), pltpu.SemaphoreType.DMA],
    )(xp)


def _peers():
    x, y, c = lax.axis_index("x"), lax.axis_index("y"), lax.axis_index("c")
    out = []
    for k in range(1, N_DEV):
        px = 1 - x if k & 4 else x
        py = 1 - y if k & 2 else y
        pc = 1 - c if k & 1 else c
        out.append((k, (px, py, pc), 4 * px + 2 * py + pc))
    return out


def _my_index():
    return 4 * lax.axis_index("x") + 2 * lax.axis_index("y") + lax.axis_index("c")


SEM_SPEC = pl.BlockSpec(memory_space=pltpu.SEMAPHORE)
ANY_SPEC = pl.BlockSpec(memory_space=pl.ANY)
_SPLIT_PARAMS = pltpu.CompilerParams(has_side_effects=pltpu.SideEffectType.DATAFLOW_SIDE_EFFECTING)


def _split_copies(gather, src_refs, land_refs, send_sems, recv_sems):
    me_idx = _my_index()
    out = []
    for a, (src_ref, land_ref) in enumerate(zip(src_refs, land_refs)):
        for k, peer, peer_idx in _peers():
            if gather:
                src, dst = src_ref, land_ref.at[me_idx]
            else:
                src, dst = src_ref.at[peer_idx], land_ref.at[k - 1]
            out.append(pltpu.make_async_remote_copy(
                src_ref=src, dst_ref=dst, send_sem=send_sems.at[7 * a + k - 1], recv_sem=recv_sems.at[7 * a + k - 1],
                device_id=peer, device_id_type=MESH))
    return out


def _split_start(name, gather, srcs):
    n = len(srcs)

    def body(*refs):
        send_sems, recv_sems = refs[n], refs[n + 1]
        for cp in _split_copies(gather, refs[:n], refs[2 * n + 2:3 * n + 2], send_sems, recv_sems):
            cp.start()
        token = refs[-1]
        token[...] = jnp.zeros_like(token)

    lands = [pltpu.HBM((N_DEV,) + a.shape if gather else (N_DEV - 1,) + a.shape[1:], a.dtype) for a in srcs]
    return pl.pallas_call(
        body, name=name,
        out_shape=(pltpu.SemaphoreType.DMA((7 * n,)), pltpu.SemaphoreType.DMA((7 * n,)),
                   *[pltpu.HBM(a.shape, a.dtype) for a in srcs], *lands, jax.ShapeDtypeStruct((8, 128), F32)),
        in_specs=(HBM_SPEC,) * n, out_specs=(SEM_SPEC, SEM_SPEC) + (HBM_SPEC,) * (2 * n) + (VMEM_SPEC,),
        input_output_aliases={i: 2 + i for i in range(n)}, compiler_params=_SPLIT_PARAMS,
    )(*[pltpu.with_memory_space_constraint(a, pltpu.HBM) for a in srcs])


def _split_wait(name, gather, started, after):
    send_sems, recv_sems, bufs = started[0], started[1], started[2:-1]
    n = len(bufs) // 2

    def body(*refs):
        for cp in _split_copies(gather, refs[:n], refs[n:2 * n], refs[2 * n], refs[2 * n + 1]):
            cp.wait_send()
            cp.wait_recv()

    out = pl.pallas_call(
        body, name=name, out_shape=tuple(pltpu.HBM(a.shape, a.dtype) for a in bufs),
        in_specs=(HBM_SPEC,) * (2 * n) + (SEM_SPEC, SEM_SPEC, ANY_SPEC), out_specs=(HBM_SPEC,) * (2 * n),
        input_output_aliases={i: i for i in range(2 * n)}, compiler_params=_SPLIT_PARAMS,
    )(*bufs, send_sems, recv_sems, after)
    return out[:n], out[n:]


def _adam_update(g, w, m, v):
    nm = ADAM_B1 * m + (1.0 - ADAM_B1) * g
    nv = ADAM_B2 * v + (1.0 - ADAM_B2) * (g * g)
    m_hat = nm / (1.0 - ADAM_B1 ** ADAM_STEP)
    v_hat = nv / (1.0 - ADAM_B2 ** ADAM_STEP)
    return -ADAM_LR * (m_hat / (jnp.sqrt(v_hat) + ADAM_EPS) + ADAM_WD * w), nm, nv


def _adamw(name, me, sent, got, w, m, v, tr):
    r, c = w.shape

    def body(me_ref, own_ref, got_ref, w_ref, m_ref, v_ref, g_ref, d_ref, nm_ref, nv_ref):
        g = own_ref[...].astype(F32)
        for k in range(N_DEV - 1):
            g = g + got_ref[k].astype(F32)
        g_ref[...] = g
        d_ref[...], nm_ref[...], nv_ref[...] = _adam_update(g, w_ref[...], m_ref[...], v_ref[...])

    blk = pl.BlockSpec((tr, c), lambda i, me_ref: (i, 0))
    return pl.pallas_call(
        body, name=name,
        grid_spec=pltpu.PrefetchScalarGridSpec(
            num_scalar_prefetch=1, grid=(r // tr,),
            in_specs=[pl.BlockSpec((None, tr, c), lambda i, me_ref: (me_ref[0], i, 0)),
                      pl.BlockSpec((N_DEV - 1, tr, c), lambda i, me_ref: (0, i, 0)), blk, blk, blk],
            out_specs=[blk] * 4),
        out_shape=[jax.ShapeDtypeStruct((r, c), F32)] * 4,
        compiler_params=_params(("arbitrary",)),
    )(me, sent, got, w, m, v)


def _adamw_small(srecv, ws, ms, vs):
    nv_ = len(ws)

    def body(*refs):
        s_ref = refs[0]
        ins, outs = refs[1:1 + 3 * nv_], refs[1 + 3 * nv_:]
        g_all = s_ref[0]
        for k in range(1, N_DEV):
            g_all = g_all + s_ref[k]
        for i in range(nv_):
            n = ins[i].shape[1]
            g = g_all[i:i + 1, :n]
            d, nm, nv = _adam_update(g, ins[i][...], ins[nv_ + i][...], ins[2 * nv_ + i][...])
            outs[i][...], outs[nv_ + i][...], outs[2 * nv_ + i][...], outs[3 * nv_ + i][...] = g, d, nm, nv

    shapes = [jax.ShapeDtypeStruct(a.shape, F32) for a in ws]
    res = pl.pallas_call(body, name="adamw_small", out_shape=shapes * 4, compiler_params=_params())(
        srecv, *ws, *ms, *vs)
    return [res[k * nv_:(k + 1) * nv_] for k in range(4)]


def _cols_from_shards(a):
    return jnp.swapaxes(a, 0, 1).reshape(a.shape[1], N_DEV * a.shape[2])


def _shards_from_cols(a):
    return jnp.swapaxes(a.reshape(a.shape[0], N_DEV, a.shape[1] // N_DEV), 0, 1)


def _shards_from_rows(a):
    return a.reshape(N_DEV, a.shape[0] // N_DEV, a.shape[1])


def _pair_lanes(a):
    lead = a.shape[:-1]
    return a.reshape(lead + (2, 2, HEAD_DIM // 2)).swapaxes(-3, -2).reshape(lead + (128,))


def _split_w_in(w_in):
    rows = w_in.shape[0]
    dil = w_in[:, :3 * DIL_WIDTH].reshape(rows, 3, 3, 4, 128)
    dil = jnp.concatenate([_pair_lanes(dil[:, :2]), dil[:, 2:]], axis=1)
    dil = dil.transpose(0, 2, 3, 1, 4).reshape(rows, 3 * DIL_WIDTH)
    o = 3 * DIL_WIDTH
    qb = w_in[:, o:o + SWA_Q_WIDTH].reshape(rows, 2, 4, HEAD_DIM).transpose(0, 2, 1, 3).reshape(rows, 4, 128)
    qb = _pair_lanes(qb).reshape(rows, SWA_Q_WIDTH)
    kb = _pair_lanes(w_in[:, o + SWA_Q_WIDTH:o + SWA_Q_WIDTH + SWA_KV_WIDTH])
    vb = w_in[:, o + SWA_Q_WIDTH + SWA_KV_WIDTH:P_WIDTH]
    return jnp.concatenate([dil, qb, kb, vb], axis=1), w_in[:, P_WIDTH:]


def _merge_w_in(dw_p, dw_g):
    rows = dw_p.shape[0]
    dil = dw_p[:, :3 * DIL_WIDTH].reshape(rows, 3, 4, 3, 128).transpose(0, 3, 1, 2, 4)
    dil = jnp.concatenate([_pair_lanes(dil[:, :2]), dil[:, 2:]], axis=1).reshape(rows, 3 * DIL_WIDTH)
    o = 3 * DIL_WIDTH
    qb = _pair_lanes(dw_p[:, o:o + SWA_Q_WIDTH].reshape(rows, 4, 128))
    qb = qb.reshape(rows, 4, 2, HEAD_DIM).transpose(0, 2, 1, 3).reshape(rows, SWA_Q_WIDTH)
    kb = _pair_lanes(dw_p[:, o + SWA_Q_WIDTH:o + SWA_Q_WIDTH + SWA_KV_WIDTH])
    vb = dw_p[:, o + SWA_Q_WIDTH + SWA_KV_WIDTH:]
    return jnp.concatenate([dil, qb, kb, vb, dw_g], axis=1)


def _swa_rows(w_b):
    return w_b.reshape(2, 4, HEAD_DIM, -1).transpose(1, 0, 2, 3).reshape(SWA_Q_WIDTH, -1)


def _swa_rows_inv(dw_b):
    return dw_b.reshape(4, 2, HEAD_DIM, -1).transpose(1, 0, 2, 3).reshape(SWA_Q_WIDTH, -1)


def _rope_tables(pos):
    half = HEAD_DIM // 2
    inv = ROPE_THETA ** (-jnp.arange(half, dtype=F32) / half)
    ang = pos.astype(F32)[:, None] * jnp.tile(inv, 4)
    sign = jnp.repeat(jnp.array([-1.0, 1.0], F32), 2 * half)
    return jnp.cos(ang), jnp.sin(ang) * sign


def _local_step(x, mem, pos, target, w_in, dep, rest_weights, on_grads, g_mix, g_cross, g_mem, g_mlp, g_final, sink):
    t = x.shape[0]
    tm = min(512, t)
    tq = min(512, t // 16)
    tw = min(2048, t)
    w_p, w_g = _split_w_in(w_in)
    cos, sin = _rope_tables(pos)
    sink_row = jnp.pad(sink.reshape(2, 4).T.reshape(1, 8), ((0, 0), (0, 120)))
    tabs = [(cos[None], sin[None])]
    for _, d in DIL_GROUPS[1:]:
        tabs.append(tuple(a.reshape(t // d, d, 128).swapaxes(0, 1) for a in (cos, sin)))
    tabs.append(tabs[0])

    h, h1, h2, p0, p1, p2, pb = _inproj(x, g_mix, w_p, [(cos, sin), tabs[1], tabs[2]], tm, dep)
    ps = [p0[None], p1, p2, pb[None]]
    outs, lses = [], []
    for gi, pv in enumerate(ps):
        res = _attn_fwd(f"attn_fwd{gi}", pv, gi == 3, sink_row[0, :8], tq)
        outs.append(res[0])
        lses.append(res[1])
    o0, l0, ob, lb, ob32 = outs[0][0], lses[0][0], outs[3][0], lses[3][0], res[2][0]
    wts = rest_weights(lb)
    w_b = _swa_rows(wts["w_branch_b"])
    tf = 2048
    gts = _gates(h, w_g, wts["b_gate"].reshape(1, GATE_WIDTH), tm, 1024)
    oa, ya, yb, merged, x1, hc = _mix(o0, outs[1], outs[2], l0, lses[1], lses[2], ob, gts, x,
                                      wts["w_branch_a"], w_b, wts["w_out"], g_cross, tm)
    mn, kv = _memkv(mem, g_mem, wts["w_ckv"])
    q, o, x2, hm = _cross(hc, x1, kv, wts["w_cq"], wts["w_co"], g_mlp, tm)
    a, dx3, loss, dg_final = _mlp(hm, x2, wts["w_1"], wts["w_2"], g_final.reshape(1, D_MODEL), target, tm, tf)

    grads = {}
    dz, dx2, dg_mlp = _mlp_bwd(dx3, a, wts["w_1"], wts["w_2"], x2, g_mlp, tm, tf)
    grads["w_2"] = _shards_from_rows(_wgrad("dw_2", a, dx3, 1024, 1024, tw, square=True))
    grads["w_1"] = _wgrad("dw_1", hm, dz, 1024, 1024, tw, col_shards=True)
    dep = on_grads(GROUP_A, grads)
    dq, dx1, dkv, dg_cross = _cross_bwd(dx2, x1, q, kv, wts["w_cq"], wts["w_co"], g_cross, tm, dep)
    grads["w_co"] = _shards_from_rows(_wgrad("dw_co", o, dx2, 1024, 1024, tw))
    grads["w_cq"] = _shards_from_rows(_wgrad("dw_cq", hc, dq, 1024, 1024, tw))
    grads["w_ckv"], dg_mem = _memkv_bwd(dkv, mn, mem, wts["w_ckv"], g_mem)
    dgt, dh_part, dya, dyb, db_gate = _merge_bwd(dx1, ya, yb, gts, wts["w_out"], w_g, tm)
    do0, do1, do2, c0, c1, c2, dob, cb, dsink = _combine_bwd(
        dya, dyb, oa, ob32, l0, lses[1], lses[2], lb, sink_row, wts["w_branch_a"], w_b, tm)
    grads["w_out"] = _shards_from_rows(_wgrad("dw_out", merged, dx1, 1024, 1024, tw))
    grads["w_branch_a"] = _shards_from_cols(_wgrad("dw_a", oa, dya, 512, 1024, tw))
    grads["w_branch_b"] = _shards_from_cols(_swa_rows_inv(_wgrad("dw_b", ob, dyb, 512, 1024, tw)))
    grads["b_gate"] = _shards_from_cols(db_gate.reshape(2, D_MODEL)).astype(BF16)
    dep = on_grads(GROUP_B, grads)
    dw_g = _wgrad("dw_g", h, dgt, 1024, 1024, tw)
    dps = []
    for gi, (pv, do_g, c_g) in enumerate(zip(ps, (do0[None], do1, do2, dob[None]), (c0[None], c1, c2, cb[None]))):
        dps.append(_attn_bwd(f"attn_bwd{gi}", pv, do_g, lses[gi], c_g, tabs[gi][0], tabs[gi][1], gi == 3, tq,
                             dep if gi == 0 else None))
    dw_p = jnp.concatenate(
        [_wgrad(f"dw_p{gi}", hh.reshape(t, D_MODEL), dpg.reshape(t, -1), 1024, PBLK, tw)
         for gi, (hh, dpg) in enumerate(zip((h, h1, h2, h), dps))], axis=1)
    grads["w_in"] = _shards_from_cols(_merge_w_in(dw_p, dw_g))
    dep = on_grads(GROUP_C, grads)
    grad_x, dg_mix = _dx(dps[0][0], dps[1], dps[2], dps[3][0], w_p, dh_part, dx1, x, g_mix, tm, dep)
    dsink_heads = dsink[0, :8].reshape(4, 2).T.reshape(8)
    small = {"g_mix": dg_mix[0], "g_cross": dg_cross[0], "g_mem": dg_mem[0], "g_mlp": dg_mlp[0],
             "g_final": dg_final[0], "sink": dsink_heads}
    return loss[0, 0], grad_x, small


def kernel(x, mem, positions, g_mix, w_in, b_gate, sink, w_branch_a, w_branch_b, w_out, g_cross, g_mem, w_cq, w_ckv, w_co, g_mlp, w_1, w_2, g_final, loss_target, m_g_mix, m_w_in, m_b_gate, m_sink, m_w_branch_a, m_w_branch_b, m_w_out, m_g_cross, m_g_mem, m_w_cq, m_w_ckv, m_w_co, m_g_mlp, m_w_1, m_w_2, m_g_final, v_g_mix, v_w_in, v_b_gate, v_sink, v_w_branch_a, v_w_branch_b, v_w_out, v_g_cross, v_g_mem, v_w_cq, v_w_ckv, v_w_co, v_g_mlp, v_w_1, v_w_2, v_g_final):
    local = dict(locals())
    shard = {n: local[n][0] for n in GROUP_A + GROUP_B + GROUP_C}
    me = _my_index()
    me_arr = me.reshape(1).astype(jnp.int32)
    tags = {GROUP_A: "a", GROUP_B: "b", GROUP_C: "c"}

    w_in_full = _cols_from_shards(_all_gather(shard["w_in"].astype(BF16)))
    rest = GROUP_A + GROUP_B

    def gathered(name, started, after):
        srcs, lands = _split_wait(name, True, started, after)
        return [lax.dynamic_update_slice(land, src[None], (me,) + (0,) * src.ndim) for src, land in zip(srcs, lands)]

    gather = _split_start("gather_start", True,
                          [shard[n] if n == "b_gate" else shard[n].astype(BF16) for n in rest])

    def rest_weights(after):
        full = {}
        for name, a in zip(rest, gathered("gather_wait", gather, after)):
            if name in ("w_1", "w_ckv"):
                full[name] = a
            elif name in _COL_SHARDED:
                full[name] = _cols_from_shards(a)
            else:
                full[name] = a.reshape(N_DEV * a.shape[1], a.shape[2])
        return full

    scatters = {}

    def on_grads(names, grads):
        scatters[names] = _split_start("scatter_start_" + tags[names], False, [grads[n] for n in names])
        return scatters[names][-1]

    loss, grad_x, small = _local_step(
        x[0], mem[0], positions[0], loss_target[0], w_in_full, gather[-1], rest_weights, on_grads,
        g_mix, g_cross, g_mem, g_mlp, g_final, sink[0])

    sp = jnp.stack([small[n] if n != "sink" else jnp.pad(small[n], (0, LANES - 8)) for n in SMALL]
                   + [jnp.zeros((LANES,), F32)] * 2)
    small_gather = _split_start("small_start", True, [sp])

    after, updated = small_gather[-1], {}
    for names in (GROUP_A, GROUP_B, GROUP_C):
        sent, got = _split_wait("scatter_wait_" + tags[names], False, scatters[names], after)
        for i, name in enumerate(names):
            outs = _adamw("adamw_" + name, me_arr, sent[i], got[i], shard[name],
                          local["m_" + name][0], local["v_" + name][0], ADAM_ROWS[name])
            updated[name] = [a[None] for a in outs]
            after = outs[3]

    flat = lambda prefix: [local[prefix + n].reshape(1, -1) for n in SMALL]
    outs = _adamw_small(gathered("small_wait", small_gather, after)[0], flat(""), flat("m_"), flat("v_"))
    for i, name in enumerate(SMALL):
        updated[name] = [outs[which][i].reshape(local[name].shape) for which in range(4)]

    order = ["g_mix", "w_in", "b_gate", "sink", "w_branch_a", "w_branch_b", "w_out", "g_cross", "g_mem", "w_cq",
             "w_ckv", "w_co", "g_mlp", "w_1", "w_2", "g_final"]
    res = [lax.psum(loss, ("x", "y", "c")), grad_x[None]]
    for which in range(4):
        res += [updated[n][which] for n in order]
    return tuple(res)
```

```python
import functools
import math

import jax
import jax.numpy as jnp
from jax import lax
from jax.experimental import pallas as pl
from jax.experimental.pallas import tpu as pltpu

F32 = jnp.float32
BF16 = jnp.bfloat16

D_MODEL = 1024
HEAD_DIM = 64
DIL_GROUPS = ((128, 1), (512, 4), (2048, 16))
ROPE_THETA = 10000.0
X_HEADS = 4
X_HEAD_DIM = D_MODEL // X_HEADS
D_FF = 4 * D_MODEL
EPS = 1e-6
DIL_WIDTH = 1536
SWA_Q_WIDTH = 512
SWA_KV_WIDTH = 128
P_WIDTH = 3 * DIL_WIDTH + SWA_Q_WIDTH + 2 * SWA_KV_WIDTH
GATE_WIDTH = 2 * D_MODEL
IN_WIDTH = P_WIDTH + GATE_WIDTH
BAND = 128
PBLK = 768
Q_SCALE = HEAD_DIM ** -0.5
X_SCALE = X_HEAD_DIM ** -0.5

ADAM_LR = 0.001
ADAM_B1 = 0.9
ADAM_B2 = 0.999
ADAM_EPS = 1e-08
ADAM_WD = 0.01
ADAM_STEP = 10

N_DEV = 8
LANES = 1024
VMEM_LIMIT = 52 * 1024 * 1024

NT = (((1,), (1,)), ((), ()))
TN = (((0,), (0,)), ((), ()))

GROUP_A = ("w_1", "w_2")
GROUP_B = ("w_branch_a", "w_branch_b", "w_out", "w_cq", "w_ckv", "w_co", "b_gate")
GROUP_C = ("w_in",)
_COL_SHARDED = ("w_in", "w_branch_a", "w_branch_b", "w_ckv", "w_1", "b_gate")
ADAM_ROWS = {"w_in": 256, "w_branch_a": 512, "w_branch_b": 512, "w_out": 128, "w_cq": 128, "w_ckv": 512,
             "w_co": 128, "w_1": 256, "w_2": 256, "b_gate": 2}
SMALL = ("g_mix", "g_cross", "g_mem", "g_mlp", "g_final", "sink")


def _params(sem=None):
    return pltpu.CompilerParams(dimension_semantics=sem, vmem_limit_bytes=VMEM_LIMIT)


def _dot(a, b):
    return jnp.dot(a, b, preferred_element_type=F32)


def _dot_nt(a, b):
    return lax.dot_general(a, b, NT, preferred_element_type=F32)


def _dot_tn(a, b):
    return lax.dot_general(a, b, TN, preferred_element_type=F32)


def _rms(xt):
    return lax.rsqrt(jnp.mean(xt * xt, axis=-1, keepdims=True) + EPS)


def _rms_bwd(dh, xt, r, g):
    xn = xt * r
    dxn = dh * g
    dx = r * (dxn - xn * jnp.mean(dxn * xn, axis=-1, keepdims=True))
    return dx, jnp.sum(dh * xn, axis=0, keepdims=True)


def _rope(x, c, s, swa, sign):
    kinds = "qqqqkv" if swa else "qkvqkv"
    cq, sq = c * Q_SCALE, s * (sign * Q_SCALE)
    sk = s * sign if sign != 1 else s
    out = []
    for ci, kind in enumerate(kinds):
        xc = x[:, ci * 128:(ci + 1) * 128]
        if kind == "v":
            out.append(xc)
        elif kind == "q":
            out.append(xc * cq + pltpu.roll(xc, 64, 1) * sq)
        else:
            out.append(xc * c + pltpu.roll(xc, 64, 1) * sk)
    return jnp.concatenate(out, axis=1)


def _lane_scratch(rows, w):
    return pltpu.VMEM((w // 128, rows, 128), F32)


def _deinterleave(val, scr_ref, dst_ref, dtype):
    d, n = dst_ref.shape[0], dst_ref.shape[1]
    nc = val.shape[1] // 128
    for c in range(nc):
        scr_ref[c] = val[:, c * 128:(c + 1) * 128]
    for r in range(d):
        rows = [scr_ref.at[c][pl.ds(r, n, stride=d), :] for c in range(nc)]
        dst_ref[r] = jnp.concatenate(rows, axis=1).astype(dtype)


def _res_spec(a, tm):
    d, w = a.shape[0], a.shape[2]
    return pl.BlockSpec((d, tm // d, w), lambda i: (0, i, 0))


def _interleave(src_ref, scr_ref):
    d, n = src_ref.shape[0], src_ref.shape[1]
    nc = src_ref.shape[2] // 128
    for r in range(d):
        v = src_ref[r].astype(F32)
        for c in range(nc):
            scr_ref.at[c][pl.ds(r, n, stride=d), :] = v[:, c * 128:(c + 1) * 128]
    return jnp.concatenate([scr_ref[c] for c in range(nc)], axis=1)


def _with_dep(body, n_in, dep):
    if dep is None:
        return body
    return lambda *refs: body(*refs[:n_in], *refs[n_in + 1:])


def _dep_spec(dep):
    return [] if dep is None else [pl.BlockSpec(memory_space=pl.ANY)]


def _dep_arg(dep):
    return [] if dep is None else [dep]


def _inproj(x, g, w_p, tabs, tm, dep=None):
    t = x.shape[0]
    gw = 2 * PBLK
    (cos, sin), (cos1, sin1), (cos2, sin2) = tabs[0], tabs[1], tabs[2]

    def body(x_ref, g_ref, w_ref, c_ref, s_ref, c1_ref, s1_ref, c2_ref, s2_ref,
             h_ref, h1_ref, h2_ref, p0_ref, p1_ref, p2_ref, pb_ref, hf_ref):
        xt = x_ref[...]
        hf = xt * _rms(xt) * g_ref[...]
        h_ref[...] = hf.astype(BF16)
        _deinterleave(hf, hf_ref, h1_ref, BF16)
        _deinterleave(hf, hf_ref, h2_ref, BF16)
        rows = lambda ref: ref[...].reshape(tm, ref.shape[-1])
        groups = ((h_ref, c_ref, s_ref, p0_ref), (h1_ref, c1_ref, s1_ref, p1_ref), (h2_ref, c2_ref, s2_ref, p2_ref))
        for gi, (lhs_ref, cc_ref, ss_ref, out_ref) in enumerate(groups):
            lhs, cc, ss = rows(lhs_ref), rows(cc_ref), rows(ss_ref)
            for half in range(2):
                col = gi * gw + half * PBLK
                val = _rope(_dot(lhs, w_ref[:, col:col + PBLK]), cc, ss, False, 1).astype(BF16)
                if out_ref.ndim == 3:
                    out_ref[:, :, half * PBLK:(half + 1) * PBLK] = val.reshape(out_ref.shape[:2] + (PBLK,))
                else:
                    out_ref[:, half * PBLK:(half + 1) * PBLK] = val
        pb_ref[...] = _rope(_dot(h_ref[...], w_ref[:, 3 * gw:]), c_ref[...], s_ref[...], True, 1).astype(BF16)

    d1, d2 = DIL_GROUPS[1][1], DIL_GROUPS[2][1]
    row = lambda w: pl.BlockSpec((tm, w), lambda i: (i, 0))
    res = lambda d, w: pl.BlockSpec((d, tm // d, w), lambda i: (0, i, 0))
    sds = jax.ShapeDtypeStruct
    return pl.pallas_call(
        _with_dep(body, 9, dep), name="inproj", grid=(t // tm,),
        in_specs=[row(D_MODEL), pl.BlockSpec((1, D_MODEL), lambda i: (0, 0)),
                  pl.BlockSpec((D_MODEL, P_WIDTH), lambda i: (0, 0), pipeline_mode=pl.Buffered(1)),
                  row(128), row(128), res(d1, 128), res(d1, 128), res(d2, 128), res(d2, 128)] + _dep_spec(dep),
        out_specs=[row(D_MODEL), res(d1, D_MODEL), res(d2, D_MODEL), row(gw), res(d1, gw), res(d2, gw), row(PBLK)],
        out_shape=[sds((t, D_MODEL), BF16), sds((d1, t // d1, D_MODEL), BF16), sds((d2, t // d2, D_MODEL), BF16),
                   sds((t, gw), BF16), sds((d1, t // d1, gw), BF16), sds((d2, t // d2, gw), BF16),
                   sds((t, PBLK), BF16)],
        scratch_shapes=[_lane_scratch(tm, D_MODEL)],
        compiler_params=_params(("arbitrary",)),
    )(x, g, w_p, cos, sin, cos1, sin1, cos2, sin2, *_dep_arg(dep))


def _gates(h, w_g, b, tm, tn):
    t = h.shape[0]

    def body(h_ref, w_ref, b_ref, o_ref):
        z = _dot(h_ref[...], w_ref[...]) + b_ref[...]
        o_ref[...] = jax.nn.sigmoid(z).astype(BF16)

    return pl.pallas_call(
        body, name="gates", grid=(t // tm, GATE_WIDTH // tn),
        in_specs=[pl.BlockSpec((tm, D_MODEL), lambda i, j: (i, 0)),
                  pl.BlockSpec((D_MODEL, tn), lambda i, j: (0, j)),
                  pl.BlockSpec((1, tn), lambda i, j: (0, j))],
        out_specs=pl.BlockSpec((tm, tn), lambda i, j: (i, j)),
        out_shape=jax.ShapeDtypeStruct((t, GATE_WIDTH), BF16),
        compiler_params=_params(("arbitrary", "arbitrary")),
    )(h, w_g, b)


def _band_mask(i, s):
    row = lax.broadcasted_iota(jnp.int32, (BAND, 2 * BAND), 0)
    col = lax.broadcasted_iota(jnp.int32, (BAND, 2 * BAND), 1)
    band = (col >= row) & (col <= row + BAND)
    if s == 0:
        band = band & ((col >= BAND) | (i > 0))
    return band


def _head_a_masks(rows):
    lane = lax.broadcasted_iota(jnp.int32, (rows, 128), 1)
    return (lane % HEAD_DIM) < HEAD_DIM // 2, lane < HEAD_DIM


def _stack_heads(x, head_a):
    zero = jnp.zeros_like(x)
    return jnp.concatenate([jnp.where(head_a, x, zero), jnp.where(head_a, zero, x)], axis=0)


def _kv_rows(cur_ref, tail_ref, s, off):
    if s == 0:
        return jnp.concatenate([tail_ref[:, off:off + 128], cur_ref[0:BAND, off:off + 128]], axis=0)
    return cur_ref[(s - 1) * BAND:(s + 1) * BAND, off:off + 128]


def _attn_layout(swa):
    if swa:
        return [(128 * j, 512, 640) for j in range(4)]
    return [(0, 128, 256), (384, 512, 640)]


def _attn_fwd(name, pv, swa, sinks, tq):
    d, ls = pv.shape[0], pv.shape[1]
    n, nsb = ls // tq, tq // BAND
    pairs = _attn_layout(swa)
    ncol = 1 if swa else 2
    ow = 128 * len(pairs)

    def body(cur_ref, tail_ref, *rest):
        sink_ref, o_ref, lse_ref, o32_ref = rest if swa else (None,) + rest + (None,)
        i = pl.program_id(2)
        lane = lax.broadcasted_iota(jnp.int32, (BAND, 128), 1)
        qk_a, v_a = _head_a_masks(BAND)
        first = lax.broadcasted_iota(jnp.int32, (2 * BAND, 1), 0) < BAND
        for s in range(nsb):
            mask = _band_mask(i, s)
            mask2 = jnp.concatenate([mask, mask], axis=0)
            rows = slice(s * BAND, (s + 1) * BAND)
            lse_tile = jnp.zeros((BAND, 128), F32)
            for j, (qo, ko, vo) in enumerate(pairs):
                q = cur_ref[rows, qo:qo + 128]
                kk = _kv_rows(cur_ref, tail_ref, s, ko)
                vv = _kv_rows(cur_ref, tail_ref, s, vo)
                sc = _dot_nt(_stack_heads(q, qk_a), kk)
                sc = jnp.where(mask2, sc, -jnp.inf)
                m = jnp.max(sc, axis=-1, keepdims=True)
                if swa:
                    sk = jnp.where(first, sink_ref[2 * j], sink_ref[2 * j + 1])
                    m = jnp.maximum(m, sk)
                p = jnp.exp(sc - m)
                den = jnp.sum(p, axis=-1, keepdims=True)
                if swa:
                    den = den + jnp.exp(sk - m)
                lse = m + jnp.log(den)
                lse_tile = jnp.where(lane == 2 * j, lse[:BAND], jnp.where(lane == 2 * j + 1, lse[BAND:], lse_tile))
                o2 = _dot((p * (1.0 / den)).astype(BF16), vv)
                o = jnp.where(v_a, o2[:BAND], o2[BAND:])
                o_ref[rows, j * 128:(j + 1) * 128] = o.astype(BF16)
                if swa:
                    o32_ref[rows, j * 128:(j + 1) * 128] = o
            lse_ref[rows, :] = lse_tile

    in_specs = [pl.BlockSpec((None, tq, PBLK), lambda r, cb, i: (r, i, cb)),
                pl.BlockSpec((None, BAND, PBLK), lambda r, cb, i: (r, jnp.maximum(i * nsb - 1, 0), cb))]
    args = [pv, pv]
    out_specs = [pl.BlockSpec((None, tq, ow), lambda r, cb, i: (r, i, cb)),
                 pl.BlockSpec((None, tq, 128), lambda r, cb, i: (r, i, cb))]
    out_shape = [jax.ShapeDtypeStruct((d, ls, 512), BF16), jax.ShapeDtypeStruct((d, ls, 128 * ncol), F32)]
    if swa:
        in_specs.append(pl.BlockSpec(memory_space=pltpu.SMEM))
        args.append(sinks)
        out_specs.append(out_specs[0])
        out_shape.append(jax.ShapeDtypeStruct((d, ls, 512), F32))
    return pl.pallas_call(
        body, name=name, grid=(d, ncol, n),
        in_specs=in_specs, out_specs=out_specs, out_shape=out_shape,
        compiler_params=_params(("arbitrary", "arbitrary", "arbitrary")),
    )(*args)


def _lse_lane(h):
    return (h // 4) * 128 + h % 4


def _head_scale(x, tile, lanes):
    lane = lax.broadcasted_iota(jnp.int32, (x.shape[0], 128), 1)
    lo = lane < HEAD_DIM
    out = []
    for c in range(x.shape[1] // 128):
        a0 = tile[:, lanes[2 * c]:lanes[2 * c] + 1]
        a1 = tile[:, lanes[2 * c + 1]:lanes[2 * c + 1] + 1]
        out.append(x[:, c * 128:(c + 1) * 128] * jnp.where(lo, a0, a1))
    return jnp.concatenate(out, axis=1)


def _head_sums(x, lanes, width):
    lane = lax.broadcasted_iota(jnp.int32, (x.shape[0], width), 1)
    out = jnp.zeros((x.shape[0], width), F32)
    for h in range(x.shape[1] // HEAD_DIM):
        sm = jnp.sum(x[:, h * HEAD_DIM:(h + 1) * HEAD_DIM], axis=-1, keepdims=True)
        out = jnp.where(lane == lanes[h], sm, out)
    return out


def _alphas(l0, l1, l2):
    m = jnp.maximum(jnp.maximum(l0, l1), l2)
    e0, e1, e2 = jnp.exp(l0 - m), jnp.exp(l1 - m), jnp.exp(l2 - m)
    den = e0 + e1 + e2
    return e0 / den, e1 / den, e2 / den


DIL_LANES = [_lse_lane(h) for h in range(8)]
SWA_LANES = list(range(8))


def _mix(o0, o1, o2, l0, l1, l2, ob, gts, x, w_a, w_b, w_out, g_cross, tm):
    t = x.shape[0]

    def body(o0_ref, o1_ref, o2_ref, l0_ref, l1_ref, l2_ref, ob_ref, g_ref, x_ref, wa_ref, wb_ref, wo_ref,
             gc_ref, oa_ref, ya_ref, yb_ref, mg_ref, x1_ref, hc_ref, so_ref, sl_ref):
        a0, a1, a2 = _alphas(l0_ref[...], _interleave(l1_ref, sl_ref), _interleave(l2_ref, sl_ref))
        oa = (_head_scale(o0_ref[...].astype(F32), a0, DIL_LANES)
              + _head_scale(_interleave(o1_ref, so_ref), a1, DIL_LANES)
              + _head_scale(_interleave(o2_ref, so_ref), a2, DIL_LANES))
        oab = oa.astype(BF16)
        oa_ref[...] = oab
        ya = _dot(oab, wa_ref[...])
        yb = _dot(ob_ref[...], wb_ref[...])
        ya_ref[...] = ya.astype(BF16)
        yb_ref[...] = yb.astype(BF16)
        merged = (g_ref[:, :D_MODEL].astype(F32) * ya + g_ref[:, D_MODEL:].astype(F32) * yb).astype(BF16)
        mg_ref[...] = merged
        x1 = x_ref[...] + _dot(merged, wo_ref[...])
        x1_ref[...] = x1
        hc_ref[...] = (x1 * _rms(x1) * gc_ref[...]).astype(BF16)

    row = lambda w: pl.BlockSpec((tm, w), lambda i: (i, 0))
    full = lambda a, b: pl.BlockSpec((a, b), lambda i: (0, 0))
    return pl.pallas_call(
        body, name="mix", grid=(t // tm,),
        in_specs=[row(512), _res_spec(o1, tm), _res_spec(o2, tm), row(256), _res_spec(l1, tm), _res_spec(l2, tm),
                  row(512), row(GATE_WIDTH),
                  row(D_MODEL), full(512, D_MODEL), full(512, D_MODEL), full(D_MODEL, D_MODEL), full(1, D_MODEL)],
        out_specs=[row(512), row(D_MODEL), row(D_MODEL), row(D_MODEL), row(D_MODEL), row(D_MODEL)],
        out_shape=[jax.ShapeDtypeStruct((t, 512), BF16), jax.ShapeDtypeStruct((t, D_MODEL), BF16),
                   jax.ShapeDtypeStruct((t, D_MODEL), BF16), jax.ShapeDtypeStruct((t, D_MODEL), BF16),
                   jax.ShapeDtypeStruct((t, D_MODEL), F32), jax.ShapeDtypeStruct((t, D_MODEL), BF16)],
        scratch_shapes=[_lane_scratch(tm, 512), _lane_scratch(tm, 256)],
        compiler_params=_params(("arbitrary",)),
    )(o0, o1, o2, l0, l1, l2, ob, gts, x, w_a, w_b, w_out, g_cross)


def _memkv(mem, g_mem, w_ckv):
    m = mem.shape[0]
    ws = w_ckv.shape[2]

    def body(mem_ref, g_ref, w_ref, mn_ref, kv_ref):
        xt = mem_ref[...]
        mn = (xt * _rms(xt) * g_ref[...]).astype(BF16)
        mn_ref[...] = mn
        for j in range(N_DEV):
            kv_ref[:, j * ws:(j + 1) * ws] = _dot(mn, w_ref[j]).astype(BF16)

    return pl.pallas_call(
        body, name="memkv",
        out_shape=[jax.ShapeDtypeStruct((m, D_MODEL), BF16), jax.ShapeDtypeStruct((m, 2 * D_MODEL), BF16)],
        compiler_params=_params(),
    )(mem, g_mem, w_ckv)


def _cross_probs(q, kv_ref, h):
    k = kv_ref[:, h * X_HEAD_DIM:(h + 1) * X_HEAD_DIM]
    sc = _dot_nt(q[:, h * X_HEAD_DIM:(h + 1) * X_HEAD_DIM], k)
    m = jnp.max(sc, axis=-1, keepdims=True)
    p = jnp.exp(sc - m)
    return p / jnp.sum(p, axis=-1, keepdims=True)


def _cross(hc, x1, kv, w_cq, w_co, g_mlp, tm):
    t = x1.shape[0]
    m = kv.shape[0]

    def body(hc_ref, x1_ref, kv_ref, wq_ref, wo_ref, g_ref, q_ref, o_ref, x2_ref, hm_ref):
        q = (_dot(hc_ref[...], wq_ref[...]) * X_SCALE).astype(BF16)
        q_ref[...] = q
        outs = []
        for h in range(X_HEADS):
            p = _cross_probs(q, kv_ref, h)
            v = kv_ref[:, D_MODEL + h * X_HEAD_DIM:D_MODEL + (h + 1) * X_HEAD_DIM]
            outs.append(_dot(p.astype(BF16), v))
        o = jnp.concatenate(outs, axis=1).astype(BF16)
        o_ref[...] = o
        x2 = x1_ref[...] + _dot(o, wo_ref[...])
        x2_ref[...] = x2
        hm_ref[...] = (x2 * _rms(x2) * g_ref[...]).astype(BF16)

    row = lambda w: pl.BlockSpec((tm, w), lambda i: (i, 0))
    full = lambda a, b: pl.BlockSpec((a, b), lambda i: (0, 0))
    return pl.pallas_call(
        body, name="cross", grid=(t // tm,),
        in_specs=[row(D_MODEL), row(D_MODEL), full(m, 2 * D_MODEL), full(D_MODEL, D_MODEL),
                  full(D_MODEL, D_MODEL), full(1, D_MODEL)],
        out_specs=[row(D_MODEL)] * 4,
        out_shape=[jax.ShapeDtypeStruct((t, D_MODEL), BF16), jax.ShapeDtypeStruct((t, D_MODEL), BF16),
                   jax.ShapeDtypeStruct((t, D_MODEL), F32), jax.ShapeDtypeStruct((t, D_MODEL), BF16)],
        compiler_params=_params(("arbitrary",)),
    )(hc, x1, kv, w_cq, w_co, g_mlp)


def _mlp(hm, x2, w_1, w_2, g_final, target, tm, tf):
    t = x2.shape[0]
    nf = D_FF // tf

    def body(hm_ref, x2_ref, w1_ref, w2_ref, g_ref, tg_ref, a_ref, dx3_ref, loss_ref, dg_ref, acc_ref):
        i, f = pl.program_id(0), pl.program_id(1)
        hm_t = hm_ref[...]
        sw = w1_ref.shape[2]
        part = None
        for s in range(w1_ref.shape[0]):
            a = jnp.maximum(_dot(hm_t, w1_ref[s]), 0.0)
            a_ref[:, s * sw:(s + 1) * sw] = a.astype(BF16)
            p_s = _dot((a * a).astype(BF16), w2_ref[s * sw:(s + 1) * sw, :])
            part = p_s if part is None else part + p_s

        @pl.when(f == 0)
        def _():
            acc_ref[...] = part

        @pl.when(f > 0)
        def _():
            acc_ref[...] += part

        @pl.when((i == 0) & (f == 0))
        def _():
            loss_ref[...] = jnp.zeros_like(loss_ref)
            dg_ref[...] = jnp.zeros_like(dg_ref)

        @pl.when(f == nf - 1)
        def _():
            x3 = x2_ref[...] + acc_ref[...]
            r = _rms(x3)
            g = g_ref[...]
            diff = x3 * r * g - tg_ref[...]
            loss_ref[...] += 0.5 * jnp.sum(jnp.mean(diff * diff, axis=-1, keepdims=True))
            dx3, dg = _rms_bwd(diff / D_MODEL, x3, r, g)
            dx3_ref[...] = dx3
            dg_ref[...] += dg

    return pl.pallas_call(
        body, name="mlp", grid=(t // tm, nf),
        in_specs=[pl.BlockSpec((tm, D_MODEL), lambda i, f: (i, 0)),
                  pl.BlockSpec((tm, D_MODEL), lambda i, f: (i, 0)),
                  pl.BlockSpec((tf // w_1.shape[2], D_MODEL, w_1.shape[2]), lambda i, f: (f, 0, 0)),
                  pl.BlockSpec((tf, D_MODEL), lambda i, f: (f, 0)),
                  pl.BlockSpec((1, D_MODEL), lambda i, f: (0, 0)),
                  pl.BlockSpec((tm, D_MODEL), lambda i, f: (i, 0))],
        out_specs=[pl.BlockSpec((tm, tf), lambda i, f: (i, f)),
                   pl.BlockSpec((tm, D_MODEL), lambda i, f: (i, 0)),
                   pl.BlockSpec((1, 128), lambda i, f: (0, 0)),
                   pl.BlockSpec((1, D_MODEL), lambda i, f: (0, 0))],
        out_shape=[jax.ShapeDtypeStruct((t, D_FF), BF16), jax.ShapeDtypeStruct((t, D_MODEL), F32),
                   jax.ShapeDtypeStruct((1, 128), F32), jax.ShapeDtypeStruct((1, D_MODEL), F32)],
        scratch_shapes=[pltpu.VMEM((tm, D_MODEL), F32)],
        compiler_params=_params(("arbitrary", "arbitrary")),
    )(hm, x2, w_1, w_2, g_final, target)


def _mlp_bwd(dx3, a, w_1, w_2, x2, g_mlp, tm, tf):
    t = x2.shape[0]
    nf = D_FF // tf

    def body(dx3_ref, a_ref, w1_ref, w2_ref, x2_ref, g_ref, dz_ref, dx2_ref, dg_ref, acc_ref):
        i, f = pl.program_id(0), pl.program_id(1)
        da2 = _dot_nt(dx3_ref[...].astype(BF16), w2_ref[...])
        dz = (2.0 * a_ref[...].astype(F32) * da2).astype(BF16)
        dz_ref[...] = dz
        sw = w1_ref.shape[2]
        part = _dot_nt(dz[:, 0:sw], w1_ref[0])
        for s in range(1, w1_ref.shape[0]):
            part = part + _dot_nt(dz[:, s * sw:(s + 1) * sw], w1_ref[s])

        @pl.when(f == 0)
        def _():
            acc_ref[...] = part

        @pl.when(f > 0)
        def _():
            acc_ref[...] += part

        @pl.when((i == 0) & (f == 0))
        def _():
            dg_ref[...] = jnp.zeros_like(dg_ref)

        @pl.when(f == nf - 1)
        def _():
            xt = x2_ref[...]
            dx, dg = _rms_bwd(acc_ref[...], xt, _rms(xt), g_ref[...])
            dx2_ref[...] = dx3_ref[...] + dx
            dg_ref[...] += dg

    return pl.pallas_call(
        body, name="mlp_bwd", grid=(t // tm, nf),
        in_specs=[pl.BlockSpec((tm, D_MODEL), lambda i, f: (i, 0)),
                  pl.BlockSpec((tm, tf), lambda i, f: (i, f)),
                  pl.BlockSpec((tf // w_1.shape[2], D_MODEL, w_1.shape[2]), lambda i, f: (f, 0, 0)),
                  pl.BlockSpec((tf, D_MODEL), lambda i, f: (f, 0)),
                  pl.BlockSpec((tm, D_MODEL), lambda i, f: (i, 0)),
                  pl.BlockSpec((1, D_MODEL), lambda i, f: (0, 0))],
        out_specs=[pl.BlockSpec((tm, tf), lambda i, f: (i, f)),
                   pl.BlockSpec((tm, D_MODEL), lambda i, f: (i, 0)),
                   pl.BlockSpec((1, D_MODEL), lambda i, f: (0, 0))],
        out_shape=[jax.ShapeDtypeStruct((t, D_FF), BF16), jax.ShapeDtypeStruct((t, D_MODEL), F32),
                   jax.ShapeDtypeStruct((1, D_MODEL), F32)],
        scratch_shapes=[pltpu.VMEM((tm, D_MODEL), F32)],
        compiler_params=_params(("arbitrary", "arbitrary")),
    )(dx3, a, w_1, w_2, x2, g_mlp)


def _wgrad(name, a, b, tka, tn, tm, square=False, col_shards=False):
    t, ka = a.shape
    n = b.shape[1]
    nk = t // tm

    def body(a_ref, b_ref, o_ref, acc_ref):
        at = a_ref[...].astype(BF16)
        if square:
            at = at * at
        part = _dot_tn(at, b_ref[...].astype(BF16))
        k = pl.program_id(2)

        @pl.when(k == 0)
        def _():
            acc_ref[...] = part

        @pl.when(k > 0)
        def _():
            acc_ref[...] += part

        @pl.when(k == nk - 1)
        def _():
            if col_shards:
                for s in range(tn // sw):
                    o_ref[s] = acc_ref[:, s * sw:(s + 1) * sw].astype(BF16)
            else:
                o_ref[...] = acc_ref[...].astype(BF16)

    if col_shards:
        sw = n // N_DEV
        out_spec = pl.BlockSpec((tn // sw, tka, sw), lambda p, q, k: (q, p, 0))
        out_shape = jax.ShapeDtypeStruct((N_DEV, ka, sw), BF16)
    else:
        out_spec = pl.BlockSpec((tka, tn), lambda p, q, k: (p, q))
        out_shape = jax.ShapeDtypeStruct((ka, n), BF16)
    return pl.pallas_call(
        body, name=name, grid=(ka // tka, n // tn, nk),
        in_specs=[pl.BlockSpec((tm, tka), lambda p, q, k: (k, p)),
                  pl.BlockSpec((tm, tn), lambda p, q, k: (k, q))],
        out_specs=out_spec, out_shape=out_shape,
        scratch_shapes=[pltpu.VMEM((tka, tn), F32)],
        compiler_params=_params(("arbitrary", "arbitrary", "arbitrary")),
    )(a, b)


def _cross_bwd(dx2, x1, q, kv, w_cq, w_co, g_cross, tm, dep=None):
    t = x1.shape[0]
    m = kv.shape[0]

    def body(dx2_ref, x1_ref, q_ref, kv_ref, wq_ref, wo_ref, g_ref, dq_ref, dx1_ref, dkv_ref, dg_ref):
        @pl.when(pl.program_id(0) == 0)
        def _():
            dkv_ref[...] = jnp.zeros_like(dkv_ref)
            dg_ref[...] = jnp.zeros_like(dg_ref)

        do = _dot_nt(dx2_ref[...].astype(BF16), wo_ref[...]).astype(BF16)
        q = q_ref[...]
        dqs = []
        for h in range(X_HEADS):
            hs = slice(h * X_HEAD_DIM, (h + 1) * X_HEAD_DIM)
            vs = slice(D_MODEL + h * X_HEAD_DIM, D_MODEL + (h + 1) * X_HEAD_DIM)
            p = _cross_probs(q, kv_ref, h)
            dp = _dot_nt(do[:, hs], kv_ref[:, vs])
            ds = (p * (dp - jnp.sum(dp * p, axis=-1, keepdims=True))).astype(BF16)
            dqs.append(_dot(ds, kv_ref[:, hs]))
            dkv_ref[:, hs] += _dot_tn(ds, q[:, hs])
            dkv_ref[:, vs] += _dot_tn(p.astype(BF16), do[:, hs])
        dq = (jnp.concatenate(dqs, axis=1) * X_SCALE).astype(BF16)
        dq_ref[...] = dq
        xt = x1_ref[...]
        dx, dg = _rms_bwd(_dot_nt(dq, wq_ref[...]), xt, _rms(xt), g_ref[...])
        dx1_ref[...] = dx2_ref[...] + dx
        dg_ref[...] += dg

    row = lambda w: pl.BlockSpec((tm, w), lambda i: (i, 0))
    full = lambda a, b: pl.BlockSpec((a, b), lambda i: (0, 0))
    return pl.pallas_call(
        _with_dep(body, 7, dep), name="cross_bwd", grid=(t // tm,),
        in_specs=[row(D_MODEL), row(D_MODEL), row(D_MODEL), full(m, 2 * D_MODEL), full(D_MODEL, D_MODEL),
                  full(D_MODEL, D_MODEL), full(1, D_MODEL)] + _dep_spec(dep),
        out_specs=[row(D_MODEL), row(D_MODEL), full(m, 2 * D_MODEL), full(1, D_MODEL)],
        out_shape=[jax.ShapeDtypeStruct((t, D_MODEL), BF16), jax.ShapeDtypeStruct((t, D_MODEL), F32),
                   jax.ShapeDtypeStruct((m, 2 * D_MODEL), F32), jax.ShapeDtypeStruct((1, D_MODEL), F32)],
        compiler_params=_params(("arbitrary",)),
    )(dx2, x1, q, kv, w_cq, w_co, g_cross, *_dep_arg(dep))


def _memkv_bwd(dkv, mn, mem, w_ckv, g_mem):
    ws = w_ckv.shape[2]

    def body(dkv_ref, mn_ref, mem_ref, w_ref, g_ref, dw_ref, dg_ref):
        mn = mn_ref[...]
        dmn = jnp.zeros(mn.shape, F32)
        for j in range(N_DEV):
            dkvb = dkv_ref[:, j * ws:(j + 1) * ws].astype(BF16)
            dw_ref[j] = _dot_tn(mn, dkvb).astype(BF16)
            dmn = dmn + _dot_nt(dkvb, w_ref[j])
        xt = mem_ref[...]
        dg_ref[...] = jnp.sum(dmn * xt * _rms(xt), axis=0, keepdims=True)

    return pl.pallas_call(
        body, name="memkv_bwd",
        out_shape=[jax.ShapeDtypeStruct(w_ckv.shape, BF16), jax.ShapeDtypeStruct((1, D_MODEL), F32)],
        compiler_params=_params(),
    )(dkv, mn, mem, w_ckv, g_mem)


def _merge_bwd(dx1, ya, yb, gts, w_out, w_g, tm):
    t = dx1.shape[0]

    def body(dx1_ref, ya_ref, yb_ref, g_ref, wo_ref, wg_ref, dg_ref, dhp_ref, dya_ref, dyb_ref, db_ref):
        @pl.when(pl.program_id(0) == 0)
        def _():
            db_ref[...] = jnp.zeros_like(db_ref)

        dm = _dot_nt(dx1_ref[...].astype(BF16), wo_ref[...])
        ga = g_ref[:, :D_MODEL].astype(F32)
        gb = g_ref[:, D_MODEL:].astype(F32)
        dya_ref[...] = (dm * ga).astype(BF16)
        dyb_ref[...] = (dm * gb).astype(BF16)
        dpa = dm * ya_ref[...].astype(F32) * ga * (1.0 - ga)
        dpb = dm * yb_ref[...].astype(F32) * gb * (1.0 - gb)
        dpre = jnp.concatenate([dpa, dpb], axis=1)
        db_ref[...] += jnp.sum(dpre, axis=0, keepdims=True)
        dpreb = dpre.astype(BF16)
        dg_ref[...] = dpreb
        dhp_ref[...] = _dot_nt(dpreb, wg_ref[...])

    row = lambda w: pl.BlockSpec((tm, w), lambda i: (i, 0))
    once = lambda a, b: pl.BlockSpec((a, b), lambda i: (0, 0), pipeline_mode=pl.Buffered(1))
    sds = jax.ShapeDtypeStruct
    return pl.pallas_call(
        body, name="merge_bwd", grid=(t // tm,),
        in_specs=[row(D_MODEL), row(D_MODEL), row(D_MODEL), row(GATE_WIDTH),
                  once(D_MODEL, D_MODEL), once(D_MODEL, GATE_WIDTH)],
        out_specs=[row(GATE_WIDTH), row(D_MODEL), row(D_MODEL), row(D_MODEL),
                   pl.BlockSpec((1, GATE_WIDTH), lambda i: (0, 0))],
        out_shape=[sds((t, GATE_WIDTH), BF16), sds((t, D_MODEL), F32), sds((t, D_MODEL), BF16),
                   sds((t, D_MODEL), BF16), sds((1, GATE_WIDTH), F32)],
        compiler_params=_params(("arbitrary",)),
    )(dx1, ya, yb, gts, w_out, w_g)


def _combine_bwd(dya, dyb, oa, ob, l0, l1, l2, lb, sink_row, w_a, w_b, tm):
    t = dya.shape[0]

    def body(dya_ref, dyb_ref, oa_ref, ob_ref, l0_ref, l1_ref, l2_ref, lb_ref, sk_ref, wa_ref, wb_ref,
             do0_ref, do1_ref, do2_ref, c0_ref, c1_ref, c2_ref, dob_ref, cb_ref, dsk_ref, so_ref, sl_ref):
        @pl.when(pl.program_id(0) == 0)
        def _():
            dsk_ref[...] = jnp.zeros_like(dsk_ref)

        doa = _dot_nt(dya_ref[...], wa_ref[...])
        dob = _dot_nt(dyb_ref[...], wb_ref[...])
        dsum = _head_sums(doa * oa_ref[...].astype(F32), DIL_LANES, 256)
        a0, a1, a2 = _alphas(l0_ref[...], _interleave(l1_ref, sl_ref), _interleave(l2_ref, sl_ref))
        c0_ref[...] = a0 * dsum
        do0_ref[...] = _head_scale(doa, a0, DIL_LANES).astype(BF16)
        for al, do_ref, c_ref in ((a1, do1_ref, c1_ref), (a2, do2_ref, c2_ref)):
            _deinterleave(al * dsum, sl_ref, c_ref, F32)
            _deinterleave(_head_scale(doa, al, DIL_LANES), so_ref, do_ref, BF16)
        dob_ref[...] = dob.astype(BF16)
        cb = _head_sums(dob * ob_ref[...], SWA_LANES, 128)
        cb_ref[...] = cb
        lane = lax.broadcasted_iota(jnp.int32, cb.shape, 1)
        psink = jnp.where(lane < 8, jnp.exp(sk_ref[...] - lb_ref[...]), 0.0)
        dsk_ref[...] += jnp.sum(-psink * cb, axis=0, keepdims=True)

    row = lambda w: pl.BlockSpec((tm, w), lambda i: (i, 0))
    full = lambda a, b: pl.BlockSpec((a, b), lambda i: (0, 0))
    sds = jax.ShapeDtypeStruct
    d1, d2 = l1.shape[0], l2.shape[0]
    res = lambda d, w: pl.BlockSpec((d, tm // d, w), lambda i: (0, i, 0))
    return pl.pallas_call(
        body, name="combine_bwd", grid=(t // tm,),
        in_specs=[row(D_MODEL), row(D_MODEL), row(512), row(512),
                  row(256), _res_spec(l1, tm), _res_spec(l2, tm), row(128), full(1, 128),
                  full(512, D_MODEL), full(512, D_MODEL)],
        out_specs=[row(512), res(d1, 512), res(d2, 512), row(256), res(d1, 256), res(d2, 256),
                   row(512), row(128), full(1, 128)],
        out_shape=[sds((t, 512), BF16), sds((d1, t // d1, 512), BF16),
                   sds((d2, t // d2, 512), BF16), sds((t, 256), F32), sds((d1, t // d1, 256), F32),
                   sds((d2, t // d2, 256), F32), sds((t, 512), BF16),
                   sds((t, 128), F32), sds((1, 128), F32)],
        scratch_shapes=[_lane_scratch(tm, 512), _lane_scratch(tm, 256)],
        compiler_params=_params(("arbitrary",)),
    )(dya, dyb, oa, ob, l0, l1, l2, lb, sink_row, w_a, w_b)


def _attn_bwd(name, pv, dov, lsev, cv, cosv, sinv, swa, tq, dep=None):
    d, ls = pv.shape[0], pv.shape[1]
    n, nsb = ls // tq, tq // BAND
    pairs = _attn_layout(swa)
    ncol = 1 if swa else 2
    ow = 128 * len(pairs)

    def body(cur_ref, tail_ref, do_ref, lse_ref, c_ref, cos_ref, sin_ref, out_ref, acc_ref, carry_ref):
        i = pl.program_id(2)
        acc_ref[...] = jnp.zeros_like(acc_ref)

        @pl.when(i < n)
        def _():
            qk_a, v_a = _head_a_masks(BAND)
            for s in range(nsb):
                mask = _band_mask(i, s)
                mask2 = jnp.concatenate([mask, mask], axis=0)
                rows = slice(s * BAND, (s + 1) * BAND)
                krows = slice(s * BAND, (s + 2) * BAND)
                for j, (qo, ko, vo) in enumerate(pairs):
                    kk = _kv_rows(cur_ref, tail_ref, s, ko)
                    vv = _kv_rows(cur_ref, tail_ref, s, vo)
                    q2 = _stack_heads(cur_ref[rows, qo:qo + 128], qk_a)
                    do2 = _stack_heads(do_ref[rows, j * 128:(j + 1) * 128], v_a)
                    col2 = lambda ref: jnp.concatenate([ref[rows, 2 * j:2 * j + 1], ref[rows, 2 * j + 1:2 * j + 2]], axis=0)
                    sc = _dot_nt(q2, kk)
                    p = jnp.exp(jnp.where(mask2, sc, -jnp.inf) - col2(lse_ref))
                    dp = _dot_nt(do2, vv)
                    ds = (p * (dp - col2(c_ref))).astype(BF16)
                    dq2 = _dot(ds, kk)
                    acc_ref[BAND + s * BAND:BAND + (s + 1) * BAND, qo:qo + 128] += jnp.where(qk_a, dq2[:BAND], dq2[BAND:])
                    acc_ref[krows, ko:ko + 128] += _dot_tn(ds, q2)
                    acc_ref[krows, vo:vo + 128] += _dot_tn(p.astype(BF16), do2)

        @pl.when(i >= 1)
        def _():
            if tq > BAND:
                fin = jnp.concatenate([carry_ref[0:tq - BAND, :], carry_ref[tq - BAND:, :] + acc_ref[0:BAND, :]], axis=0)
            else:
                fin = carry_ref[...] + acc_ref[0:BAND, :]
            out_ref[...] = _rope(fin, cos_ref[...], sin_ref[...], swa, -1).astype(BF16)

        carry_ref[...] = acc_ref[BAND:, :]

    qi = lambda i: jnp.minimum(i, n - 1)
    pi = lambda i: jnp.maximum(i - 1, 0)
    blk = lambda rows, w, row_of: pl.BlockSpec((None, rows, w), lambda r, cb, i: (r, row_of(i), cb))
    return pl.pallas_call(
        _with_dep(body, 7, dep), name=name, grid=(d, ncol, n + 1),
        in_specs=[blk(tq, PBLK, qi), blk(BAND, PBLK, lambda i: jnp.maximum(qi(i) * nsb - 1, 0)),
                  blk(tq, ow, qi), blk(tq, 128, qi), blk(tq, 128, qi),
                  pl.BlockSpec((None, tq, 128), lambda r, cb, i: (r, pi(i), 0)),
                  pl.BlockSpec((None, tq, 128), lambda r, cb, i: (r, pi(i), 0))] + _dep_spec(dep),
        out_specs=blk(tq, PBLK, pi),
        out_shape=jax.ShapeDtypeStruct((d, ls, ncol * PBLK), BF16),
        scratch_shapes=[pltpu.VMEM((tq + BAND, PBLK), F32), pltpu.VMEM((tq, PBLK), F32)],
        compiler_params=_params(("arbitrary", "arbitrary", "arbitrary")),
    )(pv, pv, dov, lsev, cv, cosv, sinv, *_dep_arg(dep))


def _dx(dp0, dp1, dp2, dpb, w_p, dh_part, dx1, x, g_mix, tm, dep=None):
    t = x.shape[0]
    gw = 2 * PBLK

    def body(dp0_ref, dp1_ref, dp2_ref, dpb_ref, w_ref, dhp_ref, dx1_ref, x_ref, g_ref, gx_ref, dg_ref,
             dpt_ref, scr_ref):
        @pl.when(pl.program_id(0) == 0)
        def _():
            dg_ref[...] = jnp.zeros_like(dg_ref)

        dpt_ref[:, 0:gw] = dp0_ref[...]
        dpt_ref[:, gw:2 * gw] = _interleave(dp1_ref, scr_ref).astype(BF16)
        dpt_ref[:, 2 * gw:3 * gw] = _interleave(dp2_ref, scr_ref).astype(BF16)
        dpt_ref[:, 3 * gw:] = dpb_ref[...]
        dh = _dot_nt(dpt_ref[...], w_ref[...]) + dhp_ref[...]
        xt = x_ref[...]
        dx, dg = _rms_bwd(dh, xt, _rms(xt), g_ref[...])
        gx_ref[...] = dx1_ref[...] + dx
        dg_ref[...] += dg

    row = lambda w: pl.BlockSpec((tm, w), lambda i: (i, 0))
    full = lambda a, b: pl.BlockSpec((a, b), lambda i: (0, 0))
    return pl.pallas_call(
        _with_dep(body, 9, dep), name="dx", grid=(t // tm,),
        in_specs=[row(gw), _res_spec(dp1, tm), _res_spec(dp2, tm), row(PBLK),
                  pl.BlockSpec((D_MODEL, P_WIDTH), lambda i: (0, 0), pipeline_mode=pl.Buffered(1)),
                  row(D_MODEL), row(D_MODEL), row(D_MODEL), full(1, D_MODEL)] + _dep_spec(dep),
        out_specs=[row(D_MODEL), full(1, D_MODEL)],
        out_shape=[jax.ShapeDtypeStruct((t, D_MODEL), F32), jax.ShapeDtypeStruct((1, D_MODEL), F32)],
        scratch_shapes=[pltpu.VMEM((tm, P_WIDTH), BF16), _lane_scratch(tm, gw)],
        compiler_params=_params(("arbitrary",)),
    )(dp0, dp1, dp2, dpb, w_p, dh_part, dx1, x, g_mix, *_dep_arg(dep))


MESH = pl.DeviceIdType.MESH
HBM_SPEC = pl.BlockSpec(memory_space=pltpu.HBM)
VMEM_SPEC = pl.BlockSpec(memory_space=pltpu.VMEM)


def _all_gather(xp):
    def body(x_ref, out_ref, send_sems, recv_sems, local_sem):
        x, y, c = lax.axis_index("x"), lax.axis_index("y"), lax.axis_index("c")
        me, sibling = (x, y, c), (x, y, 1 - c)
        chips = [(1 - x, y), (x, 1 - y), (1 - x, 1 - y)]

        def rows(px, py, pc):
            return out_ref.at[4 * px + 2 * py + pc]

        def copy(k, block, to, src=None):
            return pltpu.make_async_remote_copy(
                src_ref=rows(*block) if src is None else src, dst_ref=rows(*block),
                send_sem=send_sems.at[k], recv_sem=recv_sems.at[k], device_id=to, device_id_type=MESH)

        mine = pltpu.make_async_copy(x_ref, rows(*me), local_sem)
        mine.start()
        first = [copy(0, me, sibling, src=x_ref)]
        first += [copy(1 + j, me, (*chip, c), src=x_ref) for j, chip in enumerate(chips)]
        for cp in first:
            cp.start()
        passed = [copy(4 + j, (*chip, c), sibling) for j, chip in enumerate(chips)]
        for j, chip in enumerate(chips):
            copy(1 + j, (*chip, c), me).wait_recv()
            passed[j].start()
        copy(0, sibling, me).wait_recv()
        for j, chip in enumerate(chips):
            copy(4 + j, (*chip, 1 - c), me).wait_recv()
        for cp in first + passed:
            cp.wait_send()
        mine.wait()

    return pl.pallas_call(
        body, name="all_gather",
        out_shape=jax.ShapeDtypeStruct((N_DEV,) + xp.shape, xp.dtype),
        in_specs=[HBM_SPEC], out_specs=HBM_SPEC,
        scratch_shapes=[pltpu.SemaphoreType.DMA((7,)), pltpu.SemaphoreType.DMA((7,)), pltpu.SemaphoreType.DMA],
    )(xp)


def _peers():
    x, y, c = lax.axis_index("x"), lax.axis_index("y"), lax.axis_index("c")
    out = []
    for k in range(1, N_DEV):
        px = 1 - x if k & 4 else x
        py = 1 - y if k & 2 else y
        pc = 1 - c if k & 1 else c
        out.append((k, (px, py, pc), 4 * px + 2 * py + pc))
    return out


def _my_index():
    return 4 * lax.axis_index("x") + 2 * lax.axis_index("y") + lax.axis_index("c")


SEM_SPEC = pl.BlockSpec(memory_space=pltpu.SEMAPHORE)
ANY_SPEC = pl.BlockSpec(memory_space=pl.ANY)
_SPLIT_PARAMS = pltpu.CompilerParams(has_side_effects=pltpu.SideEffectType.DATAFLOW_SIDE_EFFECTING)


def _split_copies(gather, src_refs, land_refs, send_sems, recv_sems):
    me_idx = _my_index()
    out = []
    for a, (src_ref, land_ref) in enumerate(zip(src_refs, land_refs)):
        for k, peer, peer_idx in _peers():
            if gather:
                src, dst = src_ref, land_ref.at[me_idx]
            else:
                src, dst = src_ref.at[peer_idx], land_ref.at[k - 1]
            out.append(pltpu.make_async_remote_copy(
                src_ref=src, dst_ref=dst, send_sem=send_sems.at[7 * a + k - 1], recv_sem=recv_sems.at[7 * a + k - 1],
                device_id=peer, device_id_type=MESH))
    return out


def _split_start(name, gather, srcs):
    n = len(srcs)

    def body(*refs):
        send_sems, recv_sems = refs[n], refs[n + 1]
        for cp in _split_copies(gather, refs[:n], refs[2 * n + 2:3 * n + 2], send_sems, recv_sems):
            cp.start()
        token = refs[-1]
        token[...] = jnp.zeros_like(token)

    lands = [pltpu.HBM((N_DEV,) + a.shape if gather else (N_DEV - 1,) + a.shape[1:], a.dtype) for a in srcs]
    return pl.pallas_call(
        body, name=name,
        out_shape=(pltpu.SemaphoreType.DMA((7 * n,)), pltpu.SemaphoreType.DMA((7 * n,)),
                   *[pltpu.HBM(a.shape, a.dtype) for a in srcs], *lands, jax.ShapeDtypeStruct((8, 128), F32)),
        in_specs=(HBM_SPEC,) * n, out_specs=(SEM_SPEC, SEM_SPEC) + (HBM_SPEC,) * (2 * n) + (VMEM_SPEC,),
        input_output_aliases={i: 2 + i for i in range(n)}, compiler_params=_SPLIT_PARAMS,
    )(*[pltpu.with_memory_space_constraint(a, pltpu.HBM) for a in srcs])


def _split_wait(name, gather, started, after):
    send_sems, recv_sems, bufs = started[0], started[1], started[2:-1]
    n = len(bufs) // 2

    def body(*refs):
        for cp in _split_copies(gather, refs[:n], refs[n:2 * n], refs[2 * n], refs[2 * n + 1]):
            cp.wait_send()
            cp.wait_recv()

    out = pl.pallas_call(
        body, name=name, out_shape=tuple(pltpu.HBM(a.shape, a.dtype) for a in bufs),
        in_specs=(HBM_SPEC,) * (2 * n) + (SEM_SPEC, SEM_SPEC, ANY_SPEC), out_specs=(HBM_SPEC,) * (2 * n),
        input_output_aliases={i: i for i in range(2 * n)}, compiler_params=_SPLIT_PARAMS,
    )(*bufs, send_sems, recv_sems, after)
    return out[:n], out[n:]


def _adam_update(g, w, m, v):
    nm = ADAM_B1 * m + (1.0 - ADAM_B1) * g
    nv = ADAM_B2 * v + (1.0 - ADAM_B2) * (g * g)
    m_hat = nm / (1.0 - ADAM_B1 ** ADAM_STEP)
    v_hat = nv / (1.0 - ADAM_B2 ** ADAM_STEP)
    return -ADAM_LR * (m_hat / (jnp.sqrt(v_hat) + ADAM_EPS) + ADAM_WD * w), nm, nv


def _adamw(name, me, sent, got, w, m, v, tr):
    r, c = w.shape

    def body(me_ref, own_ref, got_ref, w_ref, m_ref, v_ref, g_ref, d_ref, nm_ref, nv_ref):
        g = own_ref[...].astype(F32)
        for k in range(N_DEV - 1):
            g = g + got_ref[k].astype(F32)
        g_ref[...] = g
        d_ref[...], nm_ref[...], nv_ref[...] = _adam_update(g, w_ref[...], m_ref[...], v_ref[...])

    blk = pl.BlockSpec((tr, c), lambda i, me_ref: (i, 0))
    return pl.pallas_call(
        body, name=name,
        grid_spec=pltpu.PrefetchScalarGridSpec(
            num_scalar_prefetch=1, grid=(r // tr,),
            in_specs=[pl.BlockSpec((None, tr, c), lambda i, me_ref: (me_ref[0], i, 0)),
                      pl.BlockSpec((N_DEV - 1, tr, c), lambda i, me_ref: (0, i, 0)), blk, blk, blk],
            out_specs=[blk] * 4),
        out_shape=[jax.ShapeDtypeStruct((r, c), F32)] * 4,
        compiler_params=_params(("arbitrary",)),
    )(me, sent, got, w, m, v)


def _adamw_small(srecv, ws, ms, vs):
    nv_ = len(ws)

    def body(*refs):
        s_ref = refs[0]
        ins, outs = refs[1:1 + 3 * nv_], refs[1 + 3 * nv_:]
        g_all = s_ref[0]
        for k in range(1, N_DEV):
            g_all = g_all + s_ref[k]
        for i in range(nv_):
            n = ins[i].shape[1]
            g = g_all[i:i + 1, :n]
            d, nm, nv = _adam_update(g, ins[i][...], ins[nv_ + i][...], ins[2 * nv_ + i][...])
            outs[i][...], outs[nv_ + i][...], outs[2 * nv_ + i][...], outs[3 * nv_ + i][...] = g, d, nm, nv
        outs[-1][...] = g_all[nv_:nv_ + 1, :128]

    shapes = [jax.ShapeDtypeStruct(a.shape, F32) for a in ws]
    res = pl.pallas_call(body, name="adamw_small", out_shape=shapes * 4 + [jax.ShapeDtypeStruct((1, 128), F32)],
                         compiler_params=_params())(srecv, *ws, *ms, *vs)
    return [res[k * nv_:(k + 1) * nv_] for k in range(4)], res[-1]


def _cols_from_shards(a):
    return jnp.swapaxes(a, 0, 1).reshape(a.shape[1], N_DEV * a.shape[2])


def _shards_from_cols(a):
    return jnp.swapaxes(a.reshape(a.shape[0], N_DEV, a.shape[1] // N_DEV), 0, 1)


def _shards_from_rows(a):
    return a.reshape(N_DEV, a.shape[0] // N_DEV, a.shape[1])


def _pair_lanes(a):
    lead = a.shape[:-1]
    return a.reshape(lead + (2, 2, HEAD_DIM // 2)).swapaxes(-3, -2).reshape(lead + (128,))


def _split_w_in(w_in):
    rows = w_in.shape[0]
    dil = w_in[:, :3 * DIL_WIDTH].reshape(rows, 3, 3, 4, 128)
    dil = jnp.concatenate([_pair_lanes(dil[:, :2]), dil[:, 2:]], axis=1)
    dil = dil.transpose(0, 2, 3, 1, 4).reshape(rows, 3 * DIL_WIDTH)
    o = 3 * DIL_WIDTH
    qb = w_in[:, o:o + SWA_Q_WIDTH].reshape(rows, 2, 4, HEAD_DIM).transpose(0, 2, 1, 3).reshape(rows, 4, 128)
    qb = _pair_lanes(qb).reshape(rows, SWA_Q_WIDTH)
    kb = _pair_lanes(w_in[:, o + SWA_Q_WIDTH:o + SWA_Q_WIDTH + SWA_KV_WIDTH])
    vb = w_in[:, o + SWA_Q_WIDTH + SWA_KV_WIDTH:P_WIDTH]
    return jnp.concatenate([dil, qb, kb, vb], axis=1), w_in[:, P_WIDTH:]


def _merge_w_in(dw_p, dw_g):
    rows = dw_p.shape[0]
    dil = dw_p[:, :3 * DIL_WIDTH].reshape(rows, 3, 4, 3, 128).transpose(0, 3, 1, 2, 4)
    dil = jnp.concatenate([_pair_lanes(dil[:, :2]), dil[:, 2:]], axis=1).reshape(rows, 3 * DIL_WIDTH)
    o = 3 * DIL_WIDTH
    qb = _pair_lanes(dw_p[:, o:o + SWA_Q_WIDTH].reshape(rows, 4, 128))
    qb = qb.reshape(rows, 4, 2, HEAD_DIM).transpose(0, 2, 1, 3).reshape(rows, SWA_Q_WIDTH)
    kb = _pair_lanes(dw_p[:, o + SWA_Q_WIDTH:o + SWA_Q_WIDTH + SWA_KV_WIDTH])
    vb = dw_p[:, o + SWA_Q_WIDTH + SWA_KV_WIDTH:]
    return jnp.concatenate([dil, qb, kb, vb, dw_g], axis=1)


def _swa_rows(w_b):
    return w_b.reshape(2, 4, HEAD_DIM, -1).transpose(1, 0, 2, 3).reshape(SWA_Q_WIDTH, -1)


def _swa_rows_inv(dw_b):
    return dw_b.reshape(4, 2, HEAD_DIM, -1).transpose(1, 0, 2, 3).reshape(SWA_Q_WIDTH, -1)


def _rope_tables(pos):
    half = HEAD_DIM // 2
    inv = ROPE_THETA ** (-jnp.arange(half, dtype=F32) / half)
    ang = pos.astype(F32)[:, None] * jnp.tile(inv, 4)
    sign = jnp.repeat(jnp.array([-1.0, 1.0], F32), 2 * half)
    return jnp.cos(ang), jnp.sin(ang) * sign


def _local_step(x, mem, pos, target, w_in, dep, rest_weights, on_grads, g_mix, g_cross, g_mem, g_mlp, g_final, sink):
    t = x.shape[0]
    tm = min(512, t)
    tq = 1024
    tw = min(2048, t)
    w_p, w_g = _split_w_in(w_in)
    cos, sin = lax.optimization_barrier(_rope_tables(pos))
    sink_row = jnp.pad(sink.reshape(2, 4).T.reshape(1, 8), ((0, 0), (0, 120)))
    tabs = [(cos[None], sin[None])]
    for _, d in DIL_GROUPS[1:]:
        tabs.append(tuple(a.reshape(t // d, d, 128).swapaxes(0, 1) for a in (cos, sin)))
    tabs.append(tabs[0])

    h, h1, h2, p0, p1, p2, pb = _inproj(x, g_mix, w_p, [(cos, sin), tabs[1], tabs[2]], tm, dep)
    ps = [p0[None], p1, p2, pb[None]]
    outs, lses = [], []
    for gi, pv in enumerate(ps):
        res = _attn_fwd(f"attn_fwd{gi}", pv, gi == 3, sink_row[0, :8], min(tq, pv.shape[1]))
        outs.append(res[0])
        lses.append(res[1])
    o0, l0, ob, lb, ob32 = outs[0][0], lses[0][0], outs[3][0], lses[3][0], res[2][0]
    wts = rest_weights(lb)
    w_b = _swa_rows(wts["w_branch_b"])
    tf = 2048
    gts = _gates(h, w_g, wts["b_gate"].reshape(1, GATE_WIDTH), tm, 1024)
    oa, ya, yb, merged, x1, hc = _mix(o0, outs[1], outs[2], l0, lses[1], lses[2], ob, gts, x,
                                      wts["w_branch_a"], w_b, wts["w_out"], g_cross, tm)
    mn, kv = _memkv(mem, g_mem, wts["w_ckv"])
    q, o, x2, hm = _cross(hc, x1, kv, wts["w_cq"], wts["w_co"], g_mlp, tm)
    a, dx3, loss, dg_final = _mlp(hm, x2, wts["w_1"], wts["w_2"], g_final.reshape(1, D_MODEL), target, tm, tf)

    grads = {}
    dz, dx2, dg_mlp = _mlp_bwd(dx3, a, wts["w_1"], wts["w_2"], x2, g_mlp, tm, tf)
    grads["w_2"] = _shards_from_rows(_wgrad("dw_2", a, dx3, 1024, 1024, tw, square=True))
    grads["w_1"] = _wgrad("dw_1", hm, dz, 1024, 1024, tw, col_shards=True)
    dep = on_grads(GROUP_A, grads)
    dq, dx1, dkv, dg_cross = _cross_bwd(dx2, x1, q, kv, wts["w_cq"], wts["w_co"], g_cross, tm, dep)
    grads["w_co"] = _shards_from_rows(_wgrad("dw_co", o, dx2, 1024, 1024, tw))
    grads["w_cq"] = _shards_from_rows(_wgrad("dw_cq", hc, dq, 1024, 1024, tw))
    grads["w_ckv"], dg_mem = _memkv_bwd(dkv, mn, mem, wts["w_ckv"], g_mem)
    dgt, dh_part, dya, dyb, db_gate = _merge_bwd(dx1, ya, yb, gts, wts["w_out"], w_g, tm)
    do0, do1, do2, c0, c1, c2, dob, cb, dsink = _combine_bwd(
        dya, dyb, oa, ob32, l0, lses[1], lses[2], lb, sink_row, wts["w_branch_a"], w_b, tm)
    grads["w_out"] = _shards_from_rows(_wgrad("dw_out", merged, dx1, 1024, 1024, tw))
    grads["w_branch_a"] = _shards_from_cols(_wgrad("dw_a", oa, dya, 512, 1024, tw))
    grads["w_branch_b"] = _shards_from_cols(_swa_rows_inv(_wgrad("dw_b", ob, dyb, 512, 1024, tw)))
    grads["b_gate"] = _shards_from_cols(db_gate.reshape(2, D_MODEL)).astype(BF16)
    dep = on_grads(GROUP_B, grads)
    dw_g = _wgrad("dw_g", h, dgt, 1024, 1024, tw)
    dps = []
    for gi, (pv, do_g, c_g) in enumerate(zip(ps, (do0[None], do1, do2, dob[None]), (c0[None], c1, c2, cb[None]))):
        dps.append(_attn_bwd(f"attn_bwd{gi}", pv, do_g, lses[gi], c_g, tabs[gi][0], tabs[gi][1], gi == 3,
                             min(tq, pv.shape[1]),
                             dep if gi == 0 else None))
    dw_p = jnp.concatenate(
        [_wgrad(f"dw_p{gi}", hh.reshape(t, D_MODEL), dpg.reshape(t, -1), 1024, PBLK, tw)
         for gi, (hh, dpg) in enumerate(zip((h, h1, h2, h), dps))], axis=1)
    grads["w_in"] = _shards_from_cols(_merge_w_in(dw_p, dw_g))
    dep = on_grads(GROUP_C, grads)
    grad_x, dg_mix = _dx(dps[0][0], dps[1], dps[2], dps[3][0], w_p, dh_part, dx1, x, g_mix, tm, dep)
    dsink_heads = dsink[0, :8].reshape(4, 2).T.reshape(8)
    small = {"g_mix": dg_mix[0], "g_cross": dg_cross[0], "g_mem": dg_mem[0], "g_mlp": dg_mlp[0],
             "g_final": dg_final[0], "sink": dsink_heads}
    return loss[0, 0], grad_x, small


def kernel(x, mem, positions, g_mix, w_in, b_gate, sink, w_branch_a, w_branch_b, w_out, g_cross, g_mem, w_cq, w_ckv, w_co, g_mlp, w_1, w_2, g_final, loss_target, m_g_mix, m_w_in, m_b_gate, m_sink, m_w_branch_a, m_w_branch_b, m_w_out, m_g_cross, m_g_mem, m_w_cq, m_w_ckv, m_w_co, m_g_mlp, m_w_1, m_w_2, m_g_final, v_g_mix, v_w_in, v_b_gate, v_sink, v_w_branch_a, v_w_branch_b, v_w_out, v_g_cross, v_g_mem, v_w_cq, v_w_ckv, v_w_co, v_g_mlp, v_w_1, v_w_2, v_g_final):
    local = dict(locals())
    shard = {n: local[n][0] for n in GROUP_A + GROUP_B + GROUP_C}
    me = _my_index()
    me_arr = me.reshape(1).astype(jnp.int32)
    tags = {GROUP_A: "a", GROUP_B: "b", GROUP_C: "c"}

    w_in_full = _cols_from_shards(_all_gather(shard["w_in"].astype(BF16)))
    rest = GROUP_A + GROUP_B

    def gathered(name, started, after):
        srcs, lands = _split_wait(name, True, started, after)
        return [lax.dynamic_update_slice(land, src[None], (me,) + (0,) * src.ndim) for src, land in zip(srcs, lands)]

    gather = _split_start("gather_start", True,
                          [shard[n] if n == "b_gate" else shard[n].astype(BF16) for n in rest])

    def rest_weights(after):
        full = {}
        for name, a in zip(rest, gathered("gather_wait", gather, after)):
            if name in ("w_1", "w_ckv"):
                full[name] = a
            elif name in _COL_SHARDED:
                full[name] = _cols_from_shards(a)
            else:
                full[name] = a.reshape(N_DEV * a.shape[1], a.shape[2])
        return full

    scatters = {}

    def on_grads(names, grads):
        scatters[names] = _split_start("scatter_start_" + tags[names], False, [grads[n] for n in names])
        return scatters[names][-1]

    loss, grad_x, small = _local_step(
        x[0], mem[0], positions[0], loss_target[0], w_in_full, gather[-1], rest_weights, on_grads,
        g_mix, g_cross, g_mem, g_mlp, g_final, sink[0])

    sp = jnp.stack([small[n] if n != "sink" else jnp.pad(small[n], (0, LANES - 8)) for n in SMALL]
                   + [jnp.pad(loss.reshape(1), (0, LANES - 1)), jnp.zeros((LANES,), F32)])
    small_gather = _split_start("small_start", True, [sp])

    after, updated = small_gather[-1], {}
    for names in (GROUP_A, GROUP_B, GROUP_C):
        sent, got = _split_wait("scatter_wait_" + tags[names], False, scatters[names], after)
        for i, name in enumerate(names):
            outs = _adamw("adamw_" + name, me_arr, sent[i], got[i], shard[name],
                          local["m_" + name][0], local["v_" + name][0], ADAM_ROWS[name])
            updated[name] = [a[None] for a in outs]
            after = outs[3]

    flat = lambda prefix: [local[prefix + n].reshape(1, -1) for n in SMALL]
    outs, loss_row = _adamw_small(gathered("small_wait", small_gather, after)[0], flat(""), flat("m_"), flat("v_"))
    for i, name in enumerate(SMALL):
        updated[name] = [outs[which][i].reshape(local[name].shape) for which in range(4)]

    order = ["g_mix", "w_in", "b_gate", "sink", "w_branch_a", "w_branch_b", "w_out", "g_cross", "g_mem", "w_cq",
             "w_ckv", "w_co", "g_mlp", "w_1", "w_2", "g_final"]
    res = [loss_row[0, 0], grad_x[None]]
    for which in range(4):
        res += [updated[n][which] for n in order]
    return tuple(res)
```

```python
import functools
import math

import jax
import jax.numpy as jnp
from jax import lax
from jax.experimental import pallas as pl
from jax.experimental.pallas import tpu as pltpu

F32 = jnp.float32
BF16 = jnp.bfloat16

D_MODEL = 1024
HEAD_DIM = 64
DIL_GROUPS = ((128, 1), (512, 4), (2048, 16))
ROPE_THETA = 10000.0
X_HEADS = 4
X_HEAD_DIM = D_MODEL // X_HEADS
D_FF = 4 * D_MODEL
EPS = 1e-6
DIL_WIDTH = 1536
SWA_Q_WIDTH = 512
SWA_KV_WIDTH = 128
P_WIDTH = 3 * DIL_WIDTH + SWA_Q_WIDTH + 2 * SWA_KV_WIDTH
GATE_WIDTH = 2 * D_MODEL
IN_WIDTH = P_WIDTH + GATE_WIDTH
BAND = 128
PBLK = 768
Q_SCALE = HEAD_DIM ** -0.5
X_SCALE = X_HEAD_DIM ** -0.5

ADAM_LR = 0.001
ADAM_B1 = 0.9
ADAM_B2 = 0.999
ADAM_EPS = 1e-08
ADAM_WD = 0.01
ADAM_STEP = 10

N_DEV = 8
LANES = 1024
VMEM_LIMIT = 52 * 1024 * 1024

NT = (((1,), (1,)), ((), ()))
TN = (((0,), (0,)), ((), ()))

GROUP_A = ("w_1", "w_2")
GROUP_B = ("w_branch_a", "w_branch_b", "w_out", "w_cq", "w_ckv", "w_co", "b_gate")
GROUP_C = ("w_in",)
_COL_SHARDED = ("w_in", "w_branch_a", "w_branch_b", "w_ckv", "w_1", "b_gate")
ADAM_ROWS = {"w_in": 256, "w_branch_a": 512, "w_branch_b": 512, "w_out": 128, "w_cq": 128, "w_ckv": 512,
             "w_co": 128, "w_1": 256, "w_2": 256, "b_gate": 2}
SMALL = ("g_mix", "g_cross", "g_mem", "g_mlp", "g_final", "sink")


def _params(sem=None):
    return pltpu.CompilerParams(dimension_semantics=sem, vmem_limit_bytes=VMEM_LIMIT)


def _dot(a, b):
    return jnp.dot(a, b, preferred_element_type=F32)


def _dot_nt(a, b):
    return lax.dot_general(a, b, NT, preferred_element_type=F32)


def _dot_tn(a, b):
    return lax.dot_general(a, b, TN, preferred_element_type=F32)


def _rms(xt):
    return lax.rsqrt(jnp.mean(xt * xt, axis=-1, keepdims=True) + EPS)


def _rms_bwd(dh, xt, r, g):
    xn = xt * r
    dxn = dh * g
    dx = r * (dxn - xn * jnp.mean(dxn * xn, axis=-1, keepdims=True))
    return dx, jnp.sum(dh * xn, axis=0, keepdims=True)


def _rope(x, c, s, swa, sign):
    kinds = "qqqqkv" if swa else "qkvqkv"
    cq, sq = c * Q_SCALE, s * (sign * Q_SCALE)
    sk = s * sign if sign != 1 else s
    out = []
    for ci, kind in enumerate(kinds):
        xc = x[:, ci * 128:(ci + 1) * 128]
        if kind == "v":
            out.append(xc)
        elif kind == "q":
            out.append(xc * cq + pltpu.roll(xc, 64, 1) * sq)
        else:
            out.append(xc * c + pltpu.roll(xc, 64, 1) * sk)
    return jnp.concatenate(out, axis=1)


def _lane_scratch(rows, w):
    return pltpu.VMEM((w // 128, rows, 128), F32)


def _deinterleave(val, scr_ref, dst_ref, dtype):
    d, n = dst_ref.shape[0], dst_ref.shape[1]
    nc = val.shape[1] // 128
    for c in range(nc):
        scr_ref[c] = val[:, c * 128:(c + 1) * 128]
    for r in range(d):
        rows = [scr_ref.at[c][pl.ds(r, n, stride=d), :] for c in range(nc)]
        dst_ref[r] = jnp.concatenate(rows, axis=1).astype(dtype)


def _res_spec(a, tm):
    d, w = a.shape[0], a.shape[2]
    return pl.BlockSpec((d, tm // d, w), lambda i: (0, i, 0))


def _interleave(src_ref, scr_ref):
    d, n = src_ref.shape[0], src_ref.shape[1]
    nc = src_ref.shape[2] // 128
    for r in range(d):
        v = src_ref[r].astype(F32)
        for c in range(nc):
            scr_ref.at[c][pl.ds(r, n, stride=d), :] = v[:, c * 128:(c + 1) * 128]
    return jnp.concatenate([scr_ref[c] for c in range(nc)], axis=1)


def _with_dep(body, n_in, dep):
    if dep is None:
        return body
    return lambda *refs: body(*refs[:n_in], *refs[n_in + 1:])


def _dep_spec(dep):
    return [] if dep is None else [pl.BlockSpec(memory_space=pl.ANY)]


def _dep_arg(dep):
    return [] if dep is None else [dep]


def _inproj(x, g, w_p, tabs, tm, dep=None):
    t = x.shape[0]
    gw = 2 * PBLK
    (cos, sin), (cos1, sin1), (cos2, sin2) = tabs[0], tabs[1], tabs[2]

    def body(x_ref, g_ref, w_ref, c_ref, s_ref, c1_ref, s1_ref, c2_ref, s2_ref,
             h_ref, h1_ref, h2_ref, p0_ref, p1_ref, p2_ref, pb_ref, hf_ref):
        xt = x_ref[...]
        hf = xt * _rms(xt) * g_ref[...]
        h_ref[...] = hf.astype(BF16)
        _deinterleave(hf, hf_ref, h1_ref, BF16)
        _deinterleave(hf, hf_ref, h2_ref, BF16)
        rows = lambda ref: ref[...].reshape(tm, ref.shape[-1])
        groups = ((h_ref, c_ref, s_ref, p0_ref), (h1_ref, c1_ref, s1_ref, p1_ref), (h2_ref, c2_ref, s2_ref, p2_ref))
        for gi, (lhs_ref, cc_ref, ss_ref, out_ref) in enumerate(groups):
            lhs, cc, ss = rows(lhs_ref), rows(cc_ref), rows(ss_ref)
            for half in range(2):
                col = gi * gw + half * PBLK
                val = _rope(_dot(lhs, w_ref[:, col:col + PBLK]), cc, ss, False, 1).astype(BF16)
                if out_ref.ndim == 3:
                    out_ref[:, :, half * PBLK:(half + 1) * PBLK] = val.reshape(out_ref.shape[:2] + (PBLK,))
                else:
                    out_ref[:, half * PBLK:(half + 1) * PBLK] = val
        pb_ref[...] = _rope(_dot(h_ref[...], w_ref[:, 3 * gw:]), c_ref[...], s_ref[...], True, 1).astype(BF16)

    d1, d2 = DIL_GROUPS[1][1], DIL_GROUPS[2][1]
    row = lambda w: pl.BlockSpec((tm, w), lambda i: (i, 0))
    res = lambda d, w: pl.BlockSpec((d, tm // d, w), lambda i: (0, i, 0))
    sds = jax.ShapeDtypeStruct
    return pl.pallas_call(
        _with_dep(body, 9, dep), name="inproj", grid=(t // tm,),
        in_specs=[row(D_MODEL), pl.BlockSpec((1, D_MODEL), lambda i: (0, 0)),
                  pl.BlockSpec((D_MODEL, P_WIDTH), lambda i: (0, 0), pipeline_mode=pl.Buffered(1)),
                  row(128), row(128), res(d1, 128), res(d1, 128), res(d2, 128), res(d2, 128)] + _dep_spec(dep),
        out_specs=[row(D_MODEL), res(d1, D_MODEL), res(d2, D_MODEL), row(gw), res(d1, gw), res(d2, gw), row(PBLK)],
        out_shape=[sds((t, D_MODEL), BF16), sds((d1, t // d1, D_MODEL), BF16), sds((d2, t // d2, D_MODEL), BF16),
                   sds((t, gw), BF16), sds((d1, t // d1, gw), BF16), sds((d2, t // d2, gw), BF16),
                   sds((t, PBLK), BF16)],
        scratch_shapes=[_lane_scratch(tm, D_MODEL)],
        compiler_params=_params(("arbitrary",)),
    )(x, g, w_p, cos, sin, cos1, sin1, cos2, sin2, *_dep_arg(dep))


def _gates(h, w_g, b, tm, tn):
    t = h.shape[0]

    def body(h_ref, w_ref, b_ref, o_ref):
        z = _dot(h_ref[...], w_ref[...]) + b_ref[...]
        o_ref[...] = jax.nn.sigmoid(z).astype(BF16)

    return pl.pallas_call(
        body, name="gates", grid=(t // tm, GATE_WIDTH // tn),
        in_specs=[pl.BlockSpec((tm, D_MODEL), lambda i, j: (i, 0)),
                  pl.BlockSpec((D_MODEL, tn), lambda i, j: (0, j)),
                  pl.BlockSpec((1, tn), lambda i, j: (0, j))],
        out_specs=pl.BlockSpec((tm, tn), lambda i, j: (i, j)),
        out_shape=jax.ShapeDtypeStruct((t, GATE_WIDTH), BF16),
        compiler_params=_params(("arbitrary", "arbitrary")),
    )(h, w_g, b)


def _band_mask(i, s):
    row = lax.broadcasted_iota(jnp.int32, (BAND, 2 * BAND), 0)
    col = lax.broadcasted_iota(jnp.int32, (BAND, 2 * BAND), 1)
    band = (col >= row) & (col <= row + BAND)
    if s == 0:
        band = band & ((col >= BAND) | (i > 0))
    return band


def _head_a_masks(rows):
    lane = lax.broadcasted_iota(jnp.int32, (rows, 128), 1)
    return (lane % HEAD_DIM) < HEAD_DIM // 2, lane < HEAD_DIM


def _stack_heads(x, head_a):
    zero = jnp.zeros_like(x)
    return jnp.concatenate([jnp.where(head_a, x, zero), jnp.where(head_a, zero, x)], axis=0)


def _kv_rows(cur_ref, tail_ref, s, off):
    if s == 0:
        return jnp.concatenate([tail_ref[:, off:off + 128], cur_ref[0:BAND, off:off + 128]], axis=0)
    return cur_ref[(s - 1) * BAND:(s + 1) * BAND, off:off + 128]


def _attn_layout(swa):
    if swa:
        return [(128 * j, 512, 640) for j in range(4)]
    return [(0, 128, 256), (384, 512, 640)]


def _attn_fwd(name, pv, swa, sinks, tq):
    d, ls = pv.shape[0], pv.shape[1]
    n, nsb = ls // tq, tq // BAND
    pairs = _attn_layout(swa)
    ncol = 1 if swa else 2
    ow = 128 * len(pairs)

    def body(cur_ref, tail_ref, *rest):
        sink_ref, o_ref, lse_ref, o32_ref = rest if swa else (None,) + rest + (None,)
        i = pl.program_id(2)
        lane = lax.broadcasted_iota(jnp.int32, (BAND, 128), 1)
        qk_a, v_a = _head_a_masks(BAND)
        first = lax.broadcasted_iota(jnp.int32, (2 * BAND, 1), 0) < BAND
        for s in range(nsb):
            mask = _band_mask(i, s)
            mask2 = jnp.concatenate([mask, mask], axis=0)
            rows = slice(s * BAND, (s + 1) * BAND)
            lse_tile = jnp.zeros((BAND, 128), F32)
            for j, (qo, ko, vo) in enumerate(pairs):
                q = cur_ref[rows, qo:qo + 128]
                kk = _kv_rows(cur_ref, tail_ref, s, ko)
                vv = _kv_rows(cur_ref, tail_ref, s, vo)
                sc = _dot_nt(_stack_heads(q, qk_a), kk)
                sc = jnp.where(mask2, sc, -jnp.inf)
                m = jnp.max(sc, axis=-1, keepdims=True)
                if swa:
                    sk = jnp.where(first, sink_ref[2 * j], sink_ref[2 * j + 1])
                    m = jnp.maximum(m, sk)
                p = jnp.exp(sc - m)
                den = jnp.sum(p, axis=-1, keepdims=True)
                if swa:
                    den = den + jnp.exp(sk - m)
                lse = m + jnp.log(den)
                lse_tile = jnp.where(lane == 2 * j, lse[:BAND], jnp.where(lane == 2 * j + 1, lse[BAND:], lse_tile))
                o2 = _dot((p * (1.0 / den)).astype(BF16), vv)
                o = jnp.where(v_a, o2[:BAND], o2[BAND:])
                o_ref[rows, j * 128:(j + 1) * 128] = o.astype(BF16)
                if swa:
                    o32_ref[rows, j * 128:(j + 1) * 128] = o
            lse_ref[rows, :] = lse_tile

    in_specs = [pl.BlockSpec((None, tq, PBLK), lambda r, cb, i: (r, i, cb)),
                pl.BlockSpec((None, BAND, PBLK), lambda r, cb, i: (r, jnp.maximum(i * nsb - 1, 0), cb))]
    args = [pv, pv]
    out_specs = [pl.BlockSpec((None, tq, ow), lambda r, cb, i: (r, i, cb)),
                 pl.BlockSpec((None, tq, 128), lambda r, cb, i: (r, i, cb))]
    out_shape = [jax.ShapeDtypeStruct((d, ls, 512), BF16), jax.ShapeDtypeStruct((d, ls, 128 * ncol), F32)]
    if swa:
        in_specs.append(pl.BlockSpec(memory_space=pltpu.SMEM))
        args.append(sinks)
        out_specs.append(out_specs[0])
        out_shape.append(jax.ShapeDtypeStruct((d, ls, 512), F32))
    return pl.pallas_call(
        body, name=name, grid=(d, ncol, n),
        in_specs=in_specs, out_specs=out_specs, out_shape=out_shape,
        compiler_params=_params(("arbitrary", "arbitrary", "arbitrary")),
    )(*args)


def _lse_lane(h):
    return (h // 4) * 128 + h % 4


def _head_scale(x, tile, lanes):
    lane = lax.broadcasted_iota(jnp.int32, (x.shape[0], 128), 1)
    lo = lane < HEAD_DIM
    out = []
    for c in range(x.shape[1] // 128):
        a0 = tile[:, lanes[2 * c]:lanes[2 * c] + 1]
        a1 = tile[:, lanes[2 * c + 1]:lanes[2 * c + 1] + 1]
        out.append(x[:, c * 128:(c + 1) * 128] * jnp.where(lo, a0, a1))
    return jnp.concatenate(out, axis=1)


def _head_sums(x, lanes, width):
    lane = lax.broadcasted_iota(jnp.int32, (x.shape[0], width), 1)
    out = jnp.zeros((x.shape[0], width), F32)
    for h in range(x.shape[1] // HEAD_DIM):
        sm = jnp.sum(x[:, h * HEAD_DIM:(h + 1) * HEAD_DIM], axis=-1, keepdims=True)
        out = jnp.where(lane == lanes[h], sm, out)
    return out


def _alphas(l0, l1, l2):
    m = jnp.maximum(jnp.maximum(l0, l1), l2)
    e0, e1, e2 = jnp.exp(l0 - m), jnp.exp(l1 - m), jnp.exp(l2 - m)
    den = e0 + e1 + e2
    return e0 / den, e1 / den, e2 / den


DIL_LANES = [_lse_lane(h) for h in range(8)]
SWA_LANES = list(range(8))


def _mix(o0, o1, o2, l0, l1, l2, ob, gts, x, w_a, w_b, w_out, g_cross, tm):
    t = x.shape[0]

    def body(o0_ref, o1_ref, o2_ref, l0_ref, l1_ref, l2_ref, ob_ref, g_ref, x_ref, wa_ref, wb_ref, wo_ref,
             gc_ref, oa_ref, ya_ref, yb_ref, mg_ref, x1_ref, hc_ref, so_ref, sl_ref):
        a0, a1, a2 = _alphas(l0_ref[...], _interleave(l1_ref, sl_ref), _interleave(l2_ref, sl_ref))
        oa = (_head_scale(o0_ref[...].astype(F32), a0, DIL_LANES)
              + _head_scale(_interleave(o1_ref, so_ref), a1, DIL_LANES)
              + _head_scale(_interleave(o2_ref, so_ref), a2, DIL_LANES))
        oab = oa.astype(BF16)
        oa_ref[...] = oab
        ya = _dot(oab, wa_ref[...])
        yb = _dot(ob_ref[...], wb_ref[...])
        ya_ref[...] = ya.astype(BF16)
        yb_ref[...] = yb.astype(BF16)
        merged = (g_ref[:, :D_MODEL].astype(F32) * ya + g_ref[:, D_MODEL:].astype(F32) * yb).astype(BF16)
        mg_ref[...] = merged
        x1 = x_ref[...] + _dot(merged, wo_ref[...])
        x1_ref[...] = x1
        hc_ref[...] = (x1 * _rms(x1) * gc_ref[...]).astype(BF16)

    row = lambda w: pl.BlockSpec((tm, w), lambda i: (i, 0))
    full = lambda a, b: pl.BlockSpec((a, b), lambda i: (0, 0))
    return pl.pallas_call(
        body, name="mix", grid=(t // tm,),
        in_specs=[row(512), _res_spec(o1, tm), _res_spec(o2, tm), row(256), _res_spec(l1, tm), _res_spec(l2, tm),
                  row(512), row(GATE_WIDTH),
                  row(D_MODEL), full(512, D_MODEL), full(512, D_MODEL), full(D_MODEL, D_MODEL), full(1, D_MODEL)],
        out_specs=[row(512), row(D_MODEL), row(D_MODEL), row(D_MODEL), row(D_MODEL), row(D_MODEL)],
        out_shape=[jax.ShapeDtypeStruct((t, 512), BF16), jax.ShapeDtypeStruct((t, D_MODEL), BF16),
                   jax.ShapeDtypeStruct((t, D_MODEL), BF16), jax.ShapeDtypeStruct((t, D_MODEL), BF16),
                   jax.ShapeDtypeStruct((t, D_MODEL), F32), jax.ShapeDtypeStruct((t, D_MODEL), BF16)],
        scratch_shapes=[_lane_scratch(tm, 512), _lane_scratch(tm, 256)],
        compiler_params=_params(("arbitrary",)),
    )(o0, o1, o2, l0, l1, l2, ob, gts, x, w_a, w_b, w_out, g_cross)


def _memkv(mem, g_mem, w_ckv):
    m = mem.shape[0]
    ws = w_ckv.shape[2]

    def body(mem_ref, g_ref, w_ref, mn_ref, kv_ref):
        xt = mem_ref[...]
        mn = (xt * _rms(xt) * g_ref[...]).astype(BF16)
        mn_ref[...] = mn
        for j in range(N_DEV):
            kv_ref[:, j * ws:(j + 1) * ws] = _dot(mn, w_ref[j]).astype(BF16)

    return pl.pallas_call(
        body, name="memkv",
        out_shape=[jax.ShapeDtypeStruct((m, D_MODEL), BF16), jax.ShapeDtypeStruct((m, 2 * D_MODEL), BF16)],
        compiler_params=_params(),
    )(mem, g_mem, w_ckv)


def _cross_probs(q, kv_ref, h):
    k = kv_ref[:, h * X_HEAD_DIM:(h + 1) * X_HEAD_DIM]
    sc = _dot_nt(q[:, h * X_HEAD_DIM:(h + 1) * X_HEAD_DIM], k)
    m = jnp.max(sc, axis=-1, keepdims=True)
    p = jnp.exp(sc - m)
    return p / jnp.sum(p, axis=-1, keepdims=True)


def _cross(hc, x1, kv, w_cq, w_co, g_mlp, tm):
    t = x1.shape[0]
    m = kv.shape[0]

    def body(hc_ref, x1_ref, kv_ref, wq_ref, wo_ref, g_ref, q_ref, o_ref, x2_ref, hm_ref):
        q = (_dot(hc_ref[...], wq_ref[...]) * X_SCALE).astype(BF16)
        q_ref[...] = q
        outs = []
        for h in range(X_HEADS):
            p = _cross_probs(q, kv_ref, h)
            v = kv_ref[:, D_MODEL + h * X_HEAD_DIM:D_MODEL + (h + 1) * X_HEAD_DIM]
            outs.append(_dot(p.astype(BF16), v))
        o = jnp.concatenate(outs, axis=1).astype(BF16)
        o_ref[...] = o
        x2 = x1_ref[...] + _dot(o, wo_ref[...])
        x2_ref[...] = x2
        hm_ref[...] = (x2 * _rms(x2) * g_ref[...]).astype(BF16)

    row = lambda w: pl.BlockSpec((tm, w), lambda i: (i, 0))
    full = lambda a, b: pl.BlockSpec((a, b), lambda i: (0, 0))
    return pl.pallas_call(
        body, name="cross", grid=(t // tm,),
        in_specs=[row(D_MODEL), row(D_MODEL), full(m, 2 * D_MODEL), full(D_MODEL, D_MODEL),
                  full(D_MODEL, D_MODEL), full(1, D_MODEL)],
        out_specs=[row(D_MODEL)] * 4,
        out_shape=[jax.ShapeDtypeStruct((t, D_MODEL), BF16), jax.ShapeDtypeStruct((t, D_MODEL), BF16),
                   jax.ShapeDtypeStruct((t, D_MODEL), F32), jax.ShapeDtypeStruct((t, D_MODEL), BF16)],
        compiler_params=_params(("arbitrary",)),
    )(hc, x1, kv, w_cq, w_co, g_mlp)


def _mlp(hm, x2, w_1, w_2, g_final, target, tm, tf):
    t = x2.shape[0]
    nf = D_FF // tf

    def body(hm_ref, x2_ref, w1_ref, w2_ref, g_ref, tg_ref, a_ref, dx3_ref, loss_ref, dg_ref, acc_ref):
        i, f = pl.program_id(0), pl.program_id(1)
        hm_t = hm_ref[...]
        sw = w1_ref.shape[2]
        part = None
        for s in range(w1_ref.shape[0]):
            a = jnp.maximum(_dot(hm_t, w1_ref[s]), 0.0)
            a_ref[:, s * sw:(s + 1) * sw] = a.astype(BF16)
            p_s = _dot((a * a).astype(BF16), w2_ref[s * sw:(s + 1) * sw, :])
            part = p_s if part is None else part + p_s

        @pl.when(f == 0)
        def _():
            acc_ref[...] = part

        @pl.when(f > 0)
        def _():
            acc_ref[...] += part

        @pl.when((i == 0) & (f == 0))
        def _():
            loss_ref[...] = jnp.zeros_like(loss_ref)
            dg_ref[...] = jnp.zeros_like(dg_ref)

        @pl.when(f == nf - 1)
        def _():
            x3 = x2_ref[...] + acc_ref[...]
            r = _rms(x3)
            g = g_ref[...]
            diff = x3 * r * g - tg_ref[...]
            loss_ref[...] += 0.5 * jnp.sum(jnp.mean(diff * diff, axis=-1, keepdims=True))
            dx3, dg = _rms_bwd(diff / D_MODEL, x3, r, g)
            dx3_ref[...] = dx3
            dg_ref[...] += dg

    return pl.pallas_call(
        body, name="mlp", grid=(t // tm, nf),
        in_specs=[pl.BlockSpec((tm, D_MODEL), lambda i, f: (i, 0)),
                  pl.BlockSpec((tm, D_MODEL), lambda i, f: (i, 0)),
                  pl.BlockSpec((tf // w_1.shape[2], D_MODEL, w_1.shape[2]), lambda i, f: (f, 0, 0)),
                  pl.BlockSpec((tf, D_MODEL), lambda i, f: (f, 0)),
                  pl.BlockSpec((1, D_MODEL), lambda i, f: (0, 0)),
                  pl.BlockSpec((tm, D_MODEL), lambda i, f: (i, 0))],
        out_specs=[pl.BlockSpec((tm, tf), lambda i, f: (i, f)),
                   pl.BlockSpec((tm, D_MODEL), lambda i, f: (i, 0)),
                   pl.BlockSpec((1, 128), lambda i, f: (0, 0)),
                   pl.BlockSpec((1, D_MODEL), lambda i, f: (0, 0))],
        out_shape=[jax.ShapeDtypeStruct((t, D_FF), BF16), jax.ShapeDtypeStruct((t, D_MODEL), F32),
                   jax.ShapeDtypeStruct((1, 128), F32), jax.ShapeDtypeStruct((1, D_MODEL), F32)],
        scratch_shapes=[pltpu.VMEM((tm, D_MODEL), F32)],
        compiler_params=_params(("arbitrary", "arbitrary")),
    )(hm, x2, w_1, w_2, g_final, target)


def _mlp_bwd(dx3, a, w_1, w_2, x2, g_mlp, tm, tf):
    t = x2.shape[0]
    nf = D_FF // tf

    def body(dx3_ref, a_ref, w1_ref, w2_ref, x2_ref, g_ref, dz_ref, dx2_ref, dg_ref, acc_ref):
        i, f = pl.program_id(0), pl.program_id(1)
        da2 = _dot_nt(dx3_ref[...].astype(BF16), w2_ref[...])
        dz = (2.0 * a_ref[...].astype(F32) * da2).astype(BF16)
        dz_ref[...] = dz
        sw = w1_ref.shape[2]
        part = _dot_nt(dz[:, 0:sw], w1_ref[0])
        for s in range(1, w1_ref.shape[0]):
            part = part + _dot_nt(dz[:, s * sw:(s + 1) * sw], w1_ref[s])

        @pl.when(f == 0)
        def _():
            acc_ref[...] = part

        @pl.when(f > 0)
        def _():
            acc_ref[...] += part

        @pl.when((i == 0) & (f == 0))
        def _():
            dg_ref[...] = jnp.zeros_like(dg_ref)

        @pl.when(f == nf - 1)
        def _():
            xt = x2_ref[...]
            dx, dg = _rms_bwd(acc_ref[...], xt, _rms(xt), g_ref[...])
            dx2_ref[...] = dx3_ref[...] + dx
            dg_ref[...] += dg

    return pl.pallas_call(
        body, name="mlp_bwd", grid=(t // tm, nf),
        in_specs=[pl.BlockSpec((tm, D_MODEL), lambda i, f: (i, 0)),
                  pl.BlockSpec((tm, tf), lambda i, f: (i, f)),
                  pl.BlockSpec((tf // w_1.shape[2], D_MODEL, w_1.shape[2]), lambda i, f: (f, 0, 0)),
                  pl.BlockSpec((tf, D_MODEL), lambda i, f: (f, 0)),
                  pl.BlockSpec((tm, D_MODEL), lambda i, f: (i, 0)),
                  pl.BlockSpec((1, D_MODEL), lambda i, f: (0, 0))],
        out_specs=[pl.BlockSpec((tm, tf), lambda i, f: (i, f)),
                   pl.BlockSpec((tm, D_MODEL), lambda i, f: (i, 0)),
                   pl.BlockSpec((1, D_MODEL), lambda i, f: (0, 0))],
        out_shape=[jax.ShapeDtypeStruct((t, D_FF), BF16), jax.ShapeDtypeStruct((t, D_MODEL), F32),
                   jax.ShapeDtypeStruct((1, D_MODEL), F32)],
        scratch_shapes=[pltpu.VMEM((tm, D_MODEL), F32)],
        compiler_params=_params(("arbitrary", "arbitrary")),
    )(dx3, a, w_1, w_2, x2, g_mlp)


def _wgrad(name, a, b, tka, tn, tm, square=False, col_shards=False):
    t, ka = a.shape
    n = b.shape[1]
    nk = t // tm

    def body(a_ref, b_ref, o_ref, acc_ref):
        at = a_ref[...].astype(BF16)
        if square:
            at = at * at
        part = _dot_tn(at, b_ref[...].astype(BF16))
        k = pl.program_id(2)

        @pl.when(k == 0)
        def _():
            acc_ref[...] = part

        @pl.when(k > 0)
        def _():
            acc_ref[...] += part

        @pl.when(k == nk - 1)
        def _():
            if col_shards:
                for s in range(tn // sw):
                    o_ref[s] = acc_ref[:, s * sw:(s + 1) * sw].astype(BF16)
            else:
                o_ref[...] = acc_ref[...].astype(BF16)

    if col_shards:
        sw = n // N_DEV
        out_spec = pl.BlockSpec((tn // sw, tka, sw), lambda p, q, k: (q, p, 0))
        out_shape = jax.ShapeDtypeStruct((N_DEV, ka, sw), BF16)
    else:
        out_spec = pl.BlockSpec((tka, tn), lambda p, q, k: (p, q))
        out_shape = jax.ShapeDtypeStruct((ka, n), BF16)
    return pl.pallas_call(
        body, name=name, grid=(ka // tka, n // tn, nk),
        in_specs=[pl.BlockSpec((tm, tka), lambda p, q, k: (k, p)),
                  pl.BlockSpec((tm, tn), lambda p, q, k: (k, q))],
        out_specs=out_spec, out_shape=out_shape,
        scratch_shapes=[pltpu.VMEM((tka, tn), F32)],
        compiler_params=_params(("arbitrary", "arbitrary", "arbitrary")),
    )(a, b)


def _cross_bwd(dx2, x1, q, kv, w_cq, w_co, g_cross, tm, dep=None):
    t = x1.shape[0]
    m = kv.shape[0]

    def body(dx2_ref, x1_ref, q_ref, kv_ref, wq_ref, wo_ref, g_ref, dq_ref, dx1_ref, dkv_ref, dg_ref):
        @pl.when(pl.program_id(0) == 0)
        def _():
            dkv_ref[...] = jnp.zeros_like(dkv_ref)
            dg_ref[...] = jnp.zeros_like(dg_ref)

        do = _dot_nt(dx2_ref[...].astype(BF16), wo_ref[...]).astype(BF16)
        q = q_ref[...]
        dqs = []
        for h in range(X_HEADS):
            hs = slice(h * X_HEAD_DIM, (h + 1) * X_HEAD_DIM)
            vs = slice(D_MODEL + h * X_HEAD_DIM, D_MODEL + (h + 1) * X_HEAD_DIM)
            p = _cross_probs(q, kv_ref, h)
            dp = _dot_nt(do[:, hs], kv_ref[:, vs])
            ds = (p * (dp - jnp.sum(dp * p, axis=-1, keepdims=True))).astype(BF16)
            dqs.append(_dot(ds, kv_ref[:, hs]))
            dkv_ref[:, hs] += _dot_tn(ds, q[:, hs])
            dkv_ref[:, vs] += _dot_tn(p.astype(BF16), do[:, hs])
        dq = (jnp.concatenate(dqs, axis=1) * X_SCALE).astype(BF16)
        dq_ref[...] = dq
        xt = x1_ref[...]
        dx, dg = _rms_bwd(_dot_nt(dq, wq_ref[...]), xt, _rms(xt), g_ref[...])
        dx1_ref[...] = dx2_ref[...] + dx
        dg_ref[...] += dg

    row = lambda w: pl.BlockSpec((tm, w), lambda i: (i, 0))
    full = lambda a, b: pl.BlockSpec((a, b), lambda i: (0, 0))
    return pl.pallas_call(
        _with_dep(body, 7, dep), name="cross_bwd", grid=(t // tm,),
        in_specs=[row(D_MODEL), row(D_MODEL), row(D_MODEL), full(m, 2 * D_MODEL), full(D_MODEL, D_MODEL),
                  full(D_MODEL, D_MODEL), full(1, D_MODEL)] + _dep_spec(dep),
        out_specs=[row(D_MODEL), row(D_MODEL), full(m, 2 * D_MODEL), full(1, D_MODEL)],
        out_shape=[jax.ShapeDtypeStruct((t, D_MODEL), BF16), jax.ShapeDtypeStruct((t, D_MODEL), F32),
                   jax.ShapeDtypeStruct((m, 2 * D_MODEL), F32), jax.ShapeDtypeStruct((1, D_MODEL), F32)],
        compiler_params=_params(("arbitrary",)),
    )(dx2, x1, q, kv, w_cq, w_co, g_cross, *_dep_arg(dep))


def _memkv_bwd(dkv, mn, mem, w_ckv, g_mem):
    ws = w_ckv.shape[2]

    def body(dkv_ref, mn_ref, mem_ref, w_ref, g_ref, dw_ref, dg_ref):
        mn = mn_ref[...]
        dmn = jnp.zeros(mn.shape, F32)
        for j in range(N_DEV):
            dkvb = dkv_ref[:, j * ws:(j + 1) * ws].astype(BF16)
            dw_ref[j] = _dot_tn(mn, dkvb).astype(BF16)
            dmn = dmn + _dot_nt(dkvb, w_ref[j])
        xt = mem_ref[...]
        dg_ref[...] = jnp.sum(dmn * xt * _rms(xt), axis=0, keepdims=True)

    return pl.pallas_call(
        body, name="memkv_bwd",
        out_shape=[jax.ShapeDtypeStruct(w_ckv.shape, BF16), jax.ShapeDtypeStruct((1, D_MODEL), F32)],
        compiler_params=_params(),
    )(dkv, mn, mem, w_ckv, g_mem)


def _merge_bwd(dx1, ya, yb, gts, w_out, w_g, tm):
    t = dx1.shape[0]

    def body(dx1_ref, ya_ref, yb_ref, g_ref, wo_ref, wg_ref, dg_ref, dhp_ref, dya_ref, dyb_ref, db_ref):
        @pl.when(pl.program_id(0) == 0)
        def _():
            db_ref[...] = jnp.zeros_like(db_ref)

        dm = _dot_nt(dx1_ref[...].astype(BF16), wo_ref[...])
        ga = g_ref[:, :D_MODEL].astype(F32)
        gb = g_ref[:, D_MODEL:].astype(F32)
        dya_ref[...] = (dm * ga).astype(BF16)
        dyb_ref[...] = (dm * gb).astype(BF16)
        dpa = dm * ya_ref[...].astype(F32) * ga * (1.0 - ga)
        dpb = dm * yb_ref[...].astype(F32) * gb * (1.0 - gb)
        dpre = jnp.concatenate([dpa, dpb], axis=1)
        db_ref[...] += jnp.sum(dpre, axis=0, keepdims=True)
        dpreb = dpre.astype(BF16)
        dg_ref[...] = dpreb
        dhp_ref[...] = _dot_nt(dpreb, wg_ref[...])

    row = lambda w: pl.BlockSpec((tm, w), lambda i: (i, 0))
    once = lambda a, b: pl.BlockSpec((a, b), lambda i: (0, 0), pipeline_mode=pl.Buffered(1))
    sds = jax.ShapeDtypeStruct
    return pl.pallas_call(
        body, name="merge_bwd", grid=(t // tm,),
        in_specs=[row(D_MODEL), row(D_MODEL), row(D_MODEL), row(GATE_WIDTH),
                  once(D_MODEL, D_MODEL), once(D_MODEL, GATE_WIDTH)],
        out_specs=[row(GATE_WIDTH), row(D_MODEL), row(D_MODEL), row(D_MODEL),
                   pl.BlockSpec((1, GATE_WIDTH), lambda i: (0, 0))],
        out_shape=[sds((t, GATE_WIDTH), BF16), sds((t, D_MODEL), F32), sds((t, D_MODEL), BF16),
                   sds((t, D_MODEL), BF16), sds((1, GATE_WIDTH), F32)],
        compiler_params=_params(("arbitrary",)),
    )(dx1, ya, yb, gts, w_out, w_g)


def _combine_bwd(dya, dyb, oa, ob, l0, l1, l2, lb, sink_row, w_a, w_b, tm):
    t = dya.shape[0]

    def body(dya_ref, dyb_ref, oa_ref, ob_ref, l0_ref, l1_ref, l2_ref, lb_ref, sk_ref, wa_ref, wb_ref,
             do0_ref, do1_ref, do2_ref, c0_ref, c1_ref, c2_ref, dob_ref, cb_ref, dsk_ref, so_ref, sl_ref):
        @pl.when(pl.program_id(0) == 0)
        def _():
            dsk_ref[...] = jnp.zeros_like(dsk_ref)

        doa = _dot_nt(dya_ref[...], wa_ref[...])
        dob = _dot_nt(dyb_ref[...], wb_ref[...])
        dsum = _head_sums(doa * oa_ref[...].astype(F32), DIL_LANES, 256)
        a0, a1, a2 = _alphas(l0_ref[...], _interleave(l1_ref, sl_ref), _interleave(l2_ref, sl_ref))
        c0_ref[...] = a0 * dsum
        do0_ref[...] = _head_scale(doa, a0, DIL_LANES).astype(BF16)
        for al, do_ref, c_ref in ((a1, do1_ref, c1_ref), (a2, do2_ref, c2_ref)):
            _deinterleave(al * dsum, sl_ref, c_ref, F32)
            _deinterleave(_head_scale(doa, al, DIL_LANES), so_ref, do_ref, BF16)
        dob_ref[...] = dob.astype(BF16)
        cb = _head_sums(dob * ob_ref[...], SWA_LANES, 128)
        cb_ref[...] = cb
        lane = lax.broadcasted_iota(jnp.int32, cb.shape, 1)
        psink = jnp.where(lane < 8, jnp.exp(sk_ref[...] - lb_ref[...]), 0.0)
        dsk_ref[...] += jnp.sum(-psink * cb, axis=0, keepdims=True)

    row = lambda w: pl.BlockSpec((tm, w), lambda i: (i, 0))
    full = lambda a, b: pl.BlockSpec((a, b), lambda i: (0, 0))
    sds = jax.ShapeDtypeStruct
    d1, d2 = l1.shape[0], l2.shape[0]
    res = lambda d, w: pl.BlockSpec((d, tm // d, w), lambda i: (0, i, 0))
    return pl.pallas_call(
        body, name="combine_bwd", grid=(t // tm,),
        in_specs=[row(D_MODEL), row(D_MODEL), row(512), row(512),
                  row(256), _res_spec(l1, tm), _res_spec(l2, tm), row(128), full(1, 128),
                  full(512, D_MODEL), full(512, D_MODEL)],
        out_specs=[row(512), res(d1, 512), res(d2, 512), row(256), res(d1, 256), res(d2, 256),
                   row(512), row(128), full(1, 128)],
        out_shape=[sds((t, 512), BF16), sds((d1, t // d1, 512), BF16),
                   sds((d2, t // d2, 512), BF16), sds((t, 256), F32), sds((d1, t // d1, 256), F32),
                   sds((d2, t // d2, 256), F32), sds((t, 512), BF16),
                   sds((t, 128), F32), sds((1, 128), F32)],
        scratch_shapes=[_lane_scratch(tm, 512), _lane_scratch(tm, 256)],
        compiler_params=_params(("arbitrary",)),
    )(dya, dyb, oa, ob, l0, l1, l2, lb, sink_row, w_a, w_b)


def _attn_bwd(name, pv, dov, lsev, cv, cosv, sinv, swa, tq, dep=None):
    d, ls = pv.shape[0], pv.shape[1]
    n, nsb = ls // tq, tq // BAND
    pairs = _attn_layout(swa)
    ncol = 1 if swa else 2
    ow = 128 * len(pairs)

    def body(cur_ref, tail_ref, do_ref, lse_ref, c_ref, cos_ref, sin_ref, out_ref, acc_ref, carry_ref):
        i = pl.program_id(2)
        blk_i = n - 1 - i
        acc_ref[...] = jnp.zeros_like(acc_ref)

        @pl.when(i == 0)
        def _():
            carry_ref[...] = jnp.zeros_like(carry_ref)

        qk_a, v_a = _head_a_masks(BAND)
        for s in range(nsb):
            mask = _band_mask(blk_i, s)
            mask2 = jnp.concatenate([mask, mask], axis=0)
            rows = slice(s * BAND, (s + 1) * BAND)
            krows = slice(s * BAND, (s + 2) * BAND)
            for j, (qo, ko, vo) in enumerate(pairs):
                kk = _kv_rows(cur_ref, tail_ref, s, ko)
                vv = _kv_rows(cur_ref, tail_ref, s, vo)
                q2 = _stack_heads(cur_ref[rows, qo:qo + 128], qk_a)
                do2 = _stack_heads(do_ref[rows, j * 128:(j + 1) * 128], v_a)
                col2 = lambda ref: jnp.concatenate([ref[rows, 2 * j:2 * j + 1], ref[rows, 2 * j + 1:2 * j + 2]], axis=0)
                sc = _dot_nt(q2, kk)
                p = jnp.exp(jnp.where(mask2, sc, -jnp.inf) - col2(lse_ref))
                dp = _dot_nt(do2, vv)
                ds = (p * (dp - col2(c_ref))).astype(BF16)
                dq2 = _dot(ds, kk)
                acc_ref[BAND + s * BAND:BAND + (s + 1) * BAND, qo:qo + 128] += jnp.where(qk_a, dq2[:BAND], dq2[BAND:])
                acc_ref[krows, ko:ko + 128] += _dot_tn(ds, q2)
                acc_ref[krows, vo:vo + 128] += _dot_tn(p.astype(BF16), do2)

        last = acc_ref[tq:, :] + carry_ref[...]
        fin = last if tq == BAND else jnp.concatenate([acc_ref[BAND:tq, :], last], axis=0)
        out_ref[...] = _rope(fin, cos_ref[...], sin_ref[...], swa, -1).astype(BF16)
        carry_ref[...] = acc_ref[0:BAND, :]

    rev = lambda i: n - 1 - i
    blk = lambda rows, w, row_of: pl.BlockSpec((None, rows, w), lambda r, cb, i: (r, row_of(i), cb))
    tab = pl.BlockSpec((None, tq, 128), lambda r, cb, i: (r, rev(i), 0))
    return pl.pallas_call(
        _with_dep(body, 7, dep), name=name, grid=(d, ncol, n),
        in_specs=[blk(tq, PBLK, rev), blk(BAND, PBLK, lambda i: jnp.maximum(rev(i) * nsb - 1, 0)),
                  blk(tq, ow, rev), blk(tq, 128, rev), blk(tq, 128, rev), tab, tab] + _dep_spec(dep),
        out_specs=blk(tq, PBLK, rev),
        out_shape=jax.ShapeDtypeStruct((d, ls, ncol * PBLK), BF16),
        scratch_shapes=[pltpu.VMEM((tq + BAND, PBLK), F32), pltpu.VMEM((BAND, PBLK), F32)],
        compiler_params=_params(("arbitrary", "arbitrary", "arbitrary")),
    )(pv, pv, dov, lsev, cv, cosv, sinv, *_dep_arg(dep))


def _dx(dp0, dp1, dp2, dpb, w_p, dh_part, dx1, x, g_mix, tm, dep=None):
    t = x.shape[0]
    gw = 2 * PBLK

    def body(dp0_ref, dp1_ref, dp2_ref, dpb_ref, w_ref, dhp_ref, dx1_ref, x_ref, g_ref, gx_ref, dg_ref,
             dpt_ref, scr_ref):
        @pl.when(pl.program_id(0) == 0)
        def _():
            dg_ref[...] = jnp.zeros_like(dg_ref)

        dpt_ref[:, 0:gw] = dp0_ref[...]
        dpt_ref[:, gw:2 * gw] = _interleave(dp1_ref, scr_ref).astype(BF16)
        dpt_ref[:, 2 * gw:3 * gw] = _interleave(dp2_ref, scr_ref).astype(BF16)
        dpt_ref[:, 3 * gw:] = dpb_ref[...]
        dh = _dot_nt(dpt_ref[...], w_ref[...]) + dhp_ref[...]
        xt = x_ref[...]
        dx, dg = _rms_bwd(dh, xt, _rms(xt), g_ref[...])
        gx_ref[...] = dx1_ref[...] + dx
        dg_ref[...] += dg

    row = lambda w: pl.BlockSpec((tm, w), lambda i: (i, 0))
    full = lambda a, b: pl.BlockSpec((a, b), lambda i: (0, 0))
    return pl.pallas_call(
        _with_dep(body, 9, dep), name="dx", grid=(t // tm,),
        in_specs=[row(gw), _res_spec(dp1, tm), _res_spec(dp2, tm), row(PBLK),
                  pl.BlockSpec((D_MODEL, P_WIDTH), lambda i: (0, 0), pipeline_mode=pl.Buffered(1)),
                  row(D_MODEL), row(D_MODEL), row(D_MODEL), full(1, D_MODEL)] + _dep_spec(dep),
        out_specs=[row(D_MODEL), full(1, D_MODEL)],
        out_shape=[jax.ShapeDtypeStruct((t, D_MODEL), F32), jax.ShapeDtypeStruct((1, D_MODEL), F32)],
        scratch_shapes=[pltpu.VMEM((tm, P_WIDTH), BF16), _lane_scratch(tm, gw)],
        compiler_params=_params(("arbitrary",)),
    )(dp0, dp1, dp2, dpb, w_p, dh_part, dx1, x, g_mix, *_dep_arg(dep))


MESH = pl.DeviceIdType.MESH
HBM_SPEC = pl.BlockSpec(memory_space=pltpu.HBM)
VMEM_SPEC = pl.BlockSpec(memory_space=pltpu.VMEM)


def _all_gather(xp):
    def body(x_ref, out_ref, send_sems, recv_sems, local_sem):
        x, y, c = lax.axis_index("x"), lax.axis_index("y"), lax.axis_index("c")
        me, sibling = (x, y, c), (x, y, 1 - c)
        chips = [(1 - x, y), (x, 1 - y), (1 - x, 1 - y)]

        def rows(px, py, pc):
            return out_ref.at[4 * px + 2 * py + pc]

        def copy(k, block, to, src=None):
            return pltpu.make_async_remote_copy(
                src_ref=rows(*block) if src is None else src, dst_ref=rows(*block),
                send_sem=send_sems.at[k], recv_sem=recv_sems.at[k], device_id=to, device_id_type=MESH)

        mine = pltpu.make_async_copy(x_ref, rows(*me), local_sem)
        mine.start()
        first = [copy(0, me, sibling, src=x_ref)]
        first += [copy(1 + j, me, (*chip, c), src=x_ref) for j, chip in enumerate(chips)]
        for cp in first:
            cp.start()
        passed = [copy(4 + j, (*chip, c), sibling) for j, chip in enumerate(chips)]
        for j, chip in enumerate(chips):
            copy(1 + j, (*chip, c), me).wait_recv()
            passed[j].start()
        copy(0, sibling, me).wait_recv()
        for j, chip in enumerate(chips):
            copy(4 + j, (*chip, 1 - c), me).wait_recv()
        for cp in first + passed:
            cp.wait_send()
        mine.wait()

    return pl.pallas_call(
        body, name="all_gather",
        out_shape=jax.ShapeDtypeStruct((N_DEV,) + xp.shape, xp.dtype),
        in_specs=[HBM_SPEC], out_specs=HBM_SPEC,
        scratch_shapes=[pltpu.SemaphoreType.DMA((7,)), pltpu.SemaphoreType.DMA((7,)), pltpu.SemaphoreType.DMA],
    )(xp)


def _peers():
    x, y, c = lax.axis_index("x"), lax.axis_index("y"), lax.axis_index("c")
    out = []
    for k in range(1, N_DEV):
        px = 1 - x if k & 4 else x
        py = 1 - y if k & 2 else y
        pc = 1 - c if k & 1 else c
        out.append((k, (px, py, pc), 4 * px + 2 * py + pc))
    return out


def _my_index():
    return 4 * lax.axis_index("x") + 2 * lax.axis_index("y") + lax.axis_index("c")


SEM_SPEC = pl.BlockSpec(memory_space=pltpu.SEMAPHORE)
ANY_SPEC = pl.BlockSpec(memory_space=pl.ANY)
_SPLIT_PARAMS = pltpu.CompilerParams(has_side_effects=pltpu.SideEffectType.DATAFLOW_SIDE_EFFECTING)


def _split_copies(gather, src_refs, land_refs, send_sems, recv_sems):
    me_idx = _my_index()
    out = []
    for a, (src_ref, land_ref) in enumerate(zip(src_refs, land_refs)):
        for k, peer, peer_idx in _peers():
            if gather:
                src, dst = src_ref, land_ref.at[me_idx]
            else:
                src, dst = src_ref.at[peer_idx], land_ref.at[k - 1]
            out.append(pltpu.make_async_remote_copy(
                src_ref=src, dst_ref=dst, send_sem=send_sems.at[7 * a + k - 1], recv_sem=recv_sems.at[7 * a + k - 1],
                device_id=peer, device_id_type=MESH))
    return out


def _split_start(name, gather, srcs):
    n = len(srcs)

    def body(*refs):
        send_sems, recv_sems = refs[n], refs[n + 1]
        for cp in _split_copies(gather, refs[:n], refs[2 * n + 2:3 * n + 2], send_sems, recv_sems):
            cp.start()
        token = refs[-1]
        token[...] = jnp.zeros_like(token)

    lands = [pltpu.HBM((N_DEV,) + a.shape if gather else (N_DEV - 1,) + a.shape[1:], a.dtype) for a in srcs]
    return pl.pallas_call(
        body, name=name,
        out_shape=(pltpu.SemaphoreType.DMA((7 * n,)), pltpu.SemaphoreType.DMA((7 * n,)),
                   *[pltpu.HBM(a.shape, a.dtype) for a in srcs], *lands, jax.ShapeDtypeStruct((8, 128), F32)),
        in_specs=(HBM_SPEC,) * n, out_specs=(SEM_SPEC, SEM_SPEC) + (HBM_SPEC,) * (2 * n) + (VMEM_SPEC,),
        input_output_aliases={i: 2 + i for i in range(n)}, compiler_params=_SPLIT_PARAMS,
    )(*[pltpu.with_memory_space_constraint(a, pltpu.HBM) for a in srcs])


def _split_wait(name, gather, started, after):
    send_sems, recv_sems, bufs = started[0], started[1], started[2:-1]
    n = len(bufs) // 2

    def body(*refs):
        for cp in _split_copies(gather, refs[:n], refs[n:2 * n], refs[2 * n], refs[2 * n + 1]):
            cp.wait_send()
            cp.wait_recv()

    out = pl.pallas_call(
        body, name=name, out_shape=tuple(pltpu.HBM(a.shape, a.dtype) for a in bufs),
        in_specs=(HBM_SPEC,) * (2 * n) + (SEM_SPEC, SEM_SPEC, ANY_SPEC), out_specs=(HBM_SPEC,) * (2 * n),
        input_output_aliases={i: i for i in range(2 * n)}, compiler_params=_SPLIT_PARAMS,
    )(*bufs, send_sems, recv_sems, after)
    return out[:n], out[n:]


def _adam_update(g, w, m, v):
    nm = ADAM_B1 * m + (1.0 - ADAM_B1) * g
    nv = ADAM_B2 * v + (1.0 - ADAM_B2) * (g * g)
    m_hat = nm / (1.0 - ADAM_B1 ** ADAM_STEP)
    v_hat = nv / (1.0 - ADAM_B2 ** ADAM_STEP)
    return -ADAM_LR * (m_hat / (jnp.sqrt(v_hat) + ADAM_EPS) + ADAM_WD * w), nm, nv


def _adamw(name, me, sent, got, w, m, v, tr):
    r, c = w.shape

    def body(me_ref, own_ref, got_ref, w_ref, m_ref, v_ref, g_ref, d_ref, nm_ref, nv_ref):
        g = own_ref[...].astype(F32)
        for k in range(N_DEV - 1):
            g = g + got_ref[k].astype(F32)
        g_ref[...] = g
        d_ref[...], nm_ref[...], nv_ref[...] = _adam_update(g, w_ref[...], m_ref[...], v_ref[...])

    blk = pl.BlockSpec((tr, c), lambda i, me_ref: (i, 0))
    return pl.pallas_call(
        body, name=name,
        grid_spec=pltpu.PrefetchScalarGridSpec(
            num_scalar_prefetch=1, grid=(r // tr,),
            in_specs=[pl.BlockSpec((None, tr, c), lambda i, me_ref: (me_ref[0], i, 0)),
                      pl.BlockSpec((N_DEV - 1, tr, c), lambda i, me_ref: (0, i, 0)), blk, blk, blk],
            out_specs=[blk] * 4),
        out_shape=[jax.ShapeDtypeStruct((r, c), F32)] * 4,
        compiler_params=_params(("arbitrary",)),
    )(me, sent, got, w, m, v)


def _adamw_small(srecv, ws, ms, vs):
    nv_ = len(ws)

    def body(*refs):
        s_ref = refs[0]
        ins, outs = refs[1:1 + 3 * nv_], refs[1 + 3 * nv_:]
        g_all = s_ref[0]
        for k in range(1, N_DEV):
            g_all = g_all + s_ref[k]
        for i in range(nv_):
            n = ins[i].shape[1]
            g = g_all[i:i + 1, :n]
            d, nm, nv = _adam_update(g, ins[i][...], ins[nv_ + i][...], ins[2 * nv_ + i][...])
            outs[i][...], outs[nv_ + i][...], outs[2 * nv_ + i][...], outs[3 * nv_ + i][...] = g, d, nm, nv
        outs[-1][...] = g_all[nv_:nv_ + 1, :128]

    shapes = [jax.ShapeDtypeStruct(a.shape, F32) for a in ws]
    res = pl.pallas_call(body, name="adamw_small", out_shape=shapes * 4 + [jax.ShapeDtypeStruct((1, 128), F32)],
                         compiler_params=_params())(srecv, *ws, *ms, *vs)
    return [res[k * nv_:(k + 1) * nv_] for k in range(4)], res[-1]


def _cols_from_shards(a):
    return jnp.swapaxes(a, 0, 1).reshape(a.shape[1], N_DEV * a.shape[2])


def _shards_from_cols(a):
    return jnp.swapaxes(a.reshape(a.shape[0], N_DEV, a.shape[1] // N_DEV), 0, 1)


def _shards_from_rows(a):
    return a.reshape(N_DEV, a.shape[0] // N_DEV, a.shape[1])


def _pair_lanes(a):
    lead = a.shape[:-1]
    return a.reshape(lead + (2, 2, HEAD_DIM // 2)).swapaxes(-3, -2).reshape(lead + (128,))


def _split_w_in(w_in):
    rows = w_in.shape[0]
    dil = w_in[:, :3 * DIL_WIDTH].reshape(rows, 3, 3, 4, 128)
    dil = jnp.concatenate([_pair_lanes(dil[:, :2]), dil[:, 2:]], axis=1)
    dil = dil.transpose(0, 2, 3, 1, 4).reshape(rows, 3 * DIL_WIDTH)
    o = 3 * DIL_WIDTH
    qb = w_in[:, o:o + SWA_Q_WIDTH].reshape(rows, 2, 4, HEAD_DIM).transpose(0, 2, 1, 3).reshape(rows, 4, 128)
    qb = _pair_lanes(qb).reshape(rows, SWA_Q_WIDTH)
    kb = _pair_lanes(w_in[:, o + SWA_Q_WIDTH:o + SWA_Q_WIDTH + SWA_KV_WIDTH])
    vb = w_in[:, o + SWA_Q_WIDTH + SWA_KV_WIDTH:P_WIDTH]
    return jnp.concatenate([dil, qb, kb, vb], axis=1), w_in[:, P_WIDTH:]


def _merge_w_in(dw_p, dw_g):
    rows = dw_p.shape[0]
    dil = dw_p[:, :3 * DIL_WIDTH].reshape(rows, 3, 4, 3, 128).transpose(0, 3, 1, 2, 4)
    dil = jnp.concatenate([_pair_lanes(dil[:, :2]), dil[:, 2:]], axis=1).reshape(rows, 3 * DIL_WIDTH)
    o = 3 * DIL_WIDTH
    qb = _pair_lanes(dw_p[:, o:o + SWA_Q_WIDTH].reshape(rows, 4, 128))
    qb = qb.reshape(rows, 4, 2, HEAD_DIM).transpose(0, 2, 1, 3).reshape(rows, SWA_Q_WIDTH)
    kb = _pair_lanes(dw_p[:, o + SWA_Q_WIDTH:o + SWA_Q_WIDTH + SWA_KV_WIDTH])
    vb = dw_p[:, o + SWA_Q_WIDTH + SWA_KV_WIDTH:]
    return jnp.concatenate([dil, qb, kb, vb, dw_g], axis=1)


def _swa_rows(w_b):
    return w_b.reshape(2, 4, HEAD_DIM, -1).transpose(1, 0, 2, 3).reshape(SWA_Q_WIDTH, -1)


def _swa_rows_inv(dw_b):
    return dw_b.reshape(4, 2, HEAD_DIM, -1).transpose(1, 0, 2, 3).reshape(SWA_Q_WIDTH, -1)


def _rope_tables(pos):
    half = HEAD_DIM // 2
    inv = ROPE_THETA ** (-jnp.arange(half, dtype=F32) / half)
    ang = pos.astype(F32)[:, None] * jnp.tile(inv, 4)
    sign = jnp.repeat(jnp.array([-1.0, 1.0], F32), 2 * half)
    return jnp.cos(ang), jnp.sin(ang) * sign


def _local_step(x, mem, pos, target, w_in, dep, rest_weights, on_grads, g_mix, g_cross, g_mem, g_mlp, g_final, sink):
    t = x.shape[0]
    tm = min(512, t)
    tq = 1024
    tw = min(2048, t)
    w_p, w_g = _split_w_in(w_in)
    cos, sin = lax.optimization_barrier(_rope_tables(pos))
    sink_row = jnp.pad(sink.reshape(2, 4).T.reshape(1, 8), ((0, 0), (0, 120)))
    tabs = [(cos[None], sin[None])]
    for _, d in DIL_GROUPS[1:]:
        tabs.append(tuple(a.reshape(t // d, d, 128).swapaxes(0, 1) for a in (cos, sin)))
    tabs.append(tabs[0])

    h, h1, h2, p0, p1, p2, pb = _inproj(x, g_mix, w_p, [(cos, sin), tabs[1], tabs[2]], tm, dep)
    ps = [p0[None], p1, p2, pb[None]]
    outs, lses = [], []
    for gi, pv in enumerate(ps):
        res = _attn_fwd(f"attn_fwd{gi}", pv, gi == 3, sink_row[0, :8], min(tq, pv.shape[1]))
        outs.append(res[0])
        lses.append(res[1])
    o0, l0, ob, lb, ob32 = outs[0][0], lses[0][0], outs[3][0], lses[3][0], res[2][0]
    wts = rest_weights(lb)
    w_b = _swa_rows(wts["w_branch_b"])
    tf = 2048
    gts = _gates(h, w_g, wts["b_gate"].reshape(1, GATE_WIDTH), tm, 1024)
    oa, ya, yb, merged, x1, hc = _mix(o0, outs[1], outs[2], l0, lses[1], lses[2], ob, gts, x,
                                      wts["w_branch_a"], w_b, wts["w_out"], g_cross, tm)
    mn, kv = _memkv(mem, g_mem, wts["w_ckv"])
    q, o, x2, hm = _cross(hc, x1, kv, wts["w_cq"], wts["w_co"], g_mlp, tm)
    a, dx3, loss, dg_final = _mlp(hm, x2, wts["w_1"], wts["w_2"], g_final.reshape(1, D_MODEL), target, tm, tf)

    grads = {}
    dz, dx2, dg_mlp = _mlp_bwd(dx3, a, wts["w_1"], wts["w_2"], x2, g_mlp, tm, tf)
    grads["w_2"] = _shards_from_rows(_wgrad("dw_2", a, dx3, 1024, 1024, tw, square=True))
    grads["w_1"] = _wgrad("dw_1", hm, dz, 1024, 1024, tw, col_shards=True)
    dep = on_grads(GROUP_A, grads)
    dq, dx1, dkv, dg_cross = _cross_bwd(dx2, x1, q, kv, wts["w_cq"], wts["w_co"], g_cross, tm, dep)
    grads["w_co"] = _shards_from_rows(_wgrad("dw_co", o, dx2, 1024, 1024, tw))
    grads["w_cq"] = _shards_from_rows(_wgrad("dw_cq", hc, dq, 1024, 1024, tw))
    grads["w_ckv"], dg_mem = _memkv_bwd(dkv, mn, mem, wts["w_ckv"], g_mem)
    dgt, dh_part, dya, dyb, db_gate = _merge_bwd(dx1, ya, yb, gts, wts["w_out"], w_g, tm)
    do0, do1, do2, c0, c1, c2, dob, cb, dsink = _combine_bwd(
        dya, dyb, oa, ob32, l0, lses[1], lses[2], lb, sink_row, wts["w_branch_a"], w_b, tm)
    grads["w_out"] = _shards_from_rows(_wgrad("dw_out", merged, dx1, 1024, 1024, tw))
    grads["w_branch_a"] = _shards_from_cols(_wgrad("dw_a", oa, dya, 512, 1024, tw))
    grads["w_branch_b"] = _shards_from_cols(_swa_rows_inv(_wgrad("dw_b", ob, dyb, 512, 1024, tw)))
    grads["b_gate"] = _shards_from_cols(db_gate.reshape(2, D_MODEL)).astype(BF16)
    dep = on_grads(GROUP_B, grads)
    dw_g = _wgrad("dw_g", h, dgt, 1024, 1024, tw)
    dps = []
    for gi, (pv, do_g, c_g) in enumerate(zip(ps, (do0[None], do1, do2, dob[None]), (c0[None], c1, c2, cb[None]))):
        dps.append(_attn_bwd(f"attn_bwd{gi}", pv, do_g, lses[gi], c_g, tabs[gi][0], tabs[gi][1], gi == 3,
                             min(tq, pv.shape[1]),
                             dep if gi == 0 else None))
    dw_p = jnp.concatenate(
        [_wgrad(f"dw_p{gi}", hh.reshape(t, D_MODEL), dpg.reshape(t, -1), 1024, PBLK, tw)
         for gi, (hh, dpg) in enumerate(zip((h, h1, h2, h), dps))], axis=1)
    grads["w_in"] = _shards_from_cols(_merge_w_in(dw_p, dw_g))
    dep = on_grads(GROUP_C, grads)
    grad_x, dg_mix = _dx(dps[0][0], dps[1], dps[2], dps[3][0], w_p, dh_part, dx1, x, g_mix, tm, dep)
    dsink_heads = dsink[0, :8].reshape(4, 2).T.reshape(8)
    small = {"g_mix": dg_mix[0], "g_cross": dg_cross[0], "g_mem": dg_mem[0], "g_mlp": dg_mlp[0],
             "g_final": dg_final[0], "sink": dsink_heads}
    return loss[0, 0], grad_x, small


def kernel(x, mem, positions, g_mix, w_in, b_gate, sink, w_branch_a, w_branch_b, w_out, g_cross, g_mem, w_cq, w_ckv, w_co, g_mlp, w_1, w_2, g_final, loss_target, m_g_mix, m_w_in, m_b_gate, m_sink, m_w_branch_a, m_w_branch_b, m_w_out, m_g_cross, m_g_mem, m_w_cq, m_w_ckv, m_w_co, m_g_mlp, m_w_1, m_w_2, m_g_final, v_g_mix, v_w_in, v_b_gate, v_sink, v_w_branch_a, v_w_branch_b, v_w_out, v_g_cross, v_g_mem, v_w_cq, v_w_ckv, v_w_co, v_g_mlp, v_w_1, v_w_2, v_g_final):
    local = dict(locals())
    shard = {n: local[n][0] for n in GROUP_A + GROUP_B + GROUP_C}
    me = _my_index()
    me_arr = me.reshape(1).astype(jnp.int32)
    tags = {GROUP_A: "a", GROUP_B: "b", GROUP_C: "c"}

    w_in_full = _cols_from_shards(_all_gather(shard["w_in"].astype(BF16)))
    rest = GROUP_A + GROUP_B

    def gathered(name, started, after):
        srcs, lands = _split_wait(name, True, started, after)
        return [lax.dynamic_update_slice(land, src[None], (me,) + (0,) * src.ndim) for src, land in zip(srcs, lands)]

    gather = _split_start("gather_start", True,
                          [shard[n] if n == "b_gate" else shard[n].astype(BF16) for n in rest])

    def rest_weights(after):
        full = {}
        for name, a in zip(rest, gathered("gather_wait", gather, after)):
            if name in ("w_1", "w_ckv"):
                full[name] = a
            elif name in _COL_SHARDED:
                full[name] = _cols_from_shards(a)
            else:
                full[name] = a.reshape(N_DEV * a.shape[1], a.shape[2])
        return full

    scatters = {}

    def on_grads(names, grads):
        scatters[names] = _split_start("scatter_start_" + tags[names], False, [grads[n] for n in names])
        return scatters[names][-1]

    loss, grad_x, small = _local_step(
        x[0], mem[0], positions[0], loss_target[0], w_in_full, gather[-1], rest_weights, on_grads,
        g_mix, g_cross, g_mem, g_mlp, g_final, sink[0])

    sp = jnp.stack([small[n] if n != "sink" else jnp.pad(small[n], (0, LANES - 8)) for n in SMALL]
                   + [jnp.pad(loss.reshape(1), (0, LANES - 1)), jnp.zeros((LANES,), F32)])
    small_gather = _split_start("small_start", True, [sp])

    after, updated = small_gather[-1], {}
    for names in (GROUP_A, GROUP_B, GROUP_C):
        sent, got = _split_wait("scatter_wait_" + tags[names], False, scatters[names], after)
        for i, name in enumerate(names):
            outs = _adamw("adamw_" + name, me_arr, sent[i], got[i], shard[name],
                          local["m_" + name][0], local["v_" + name][0], ADAM_ROWS[name])
            updated[name] = [a[None] for a in outs]
            after = outs[3]

    flat = lambda prefix: [local[prefix + n].reshape(1, -1) for n in SMALL]
    outs, loss_row = _adamw_small(gathered("small_wait", small_gather, after)[0], flat(""), flat("m_"), flat("v_"))
    for i, name in enumerate(SMALL):
        updated[name] = [outs[which][i].reshape(local[name].shape) for which in range(4)]

    order = ["g_mix", "w_in", "b_gate", "sink", "w_branch_a", "w_branch_b", "w_out", "g_cross", "g_mem", "w_cq",
             "w_ckv", "w_co", "g_mlp", "w_1", "w_2", "g_final"]
    res = [loss_row[0, 0], grad_x[None]]
    for which in range(4):
        res += [updated[n][which] for n in order]
    return tuple(res)
```

```python
import functools
import math

import jax
import jax.numpy as jnp
from jax import lax
from jax.experimental import pallas as pl
from jax.experimental.pallas import tpu as pltpu

F32 = jnp.float32
BF16 = jnp.bfloat16

D_MODEL = 1024
HEAD_DIM = 64
DIL_GROUPS = ((128, 1), (512, 4), (2048, 16))
ROPE_THETA = 10000.0
X_HEADS = 4
X_HEAD_DIM = D_MODEL // X_HEADS
D_FF = 4 * D_MODEL
EPS = 1e-6
DIL_WIDTH = 1536
SWA_Q_WIDTH = 512
SWA_KV_WIDTH = 128
P_WIDTH = 3 * DIL_WIDTH + SWA_Q_WIDTH + 2 * SWA_KV_WIDTH
GATE_WIDTH = 2 * D_MODEL
IN_WIDTH = P_WIDTH + GATE_WIDTH
BAND = 128
PBLK = 768
Q_SCALE = HEAD_DIM ** -0.5
X_SCALE = X_HEAD_DIM ** -0.5

ADAM_LR = 0.001
ADAM_B1 = 0.9
ADAM_B2 = 0.999
ADAM_EPS = 1e-08
ADAM_WD = 0.01
ADAM_STEP = 10

N_DEV = 8
LANES = 1024
VMEM_LIMIT = 52 * 1024 * 1024

NT = (((1,), (1,)), ((), ()))
TN = (((0,), (0,)), ((), ()))

GROUP_A = ("w_1", "w_2")
GROUP_B = ("w_branch_a", "w_branch_b", "w_out", "w_cq", "w_ckv", "w_co", "b_gate")
GROUP_C = ("w_in",)
_COL_SHARDED = ("w_in", "w_branch_a", "w_branch_b", "w_ckv", "w_1", "b_gate")
ADAM_ROWS = {"w_in": 256, "w_branch_a": 512, "w_branch_b": 512, "w_out": 128, "w_cq": 128, "w_ckv": 512,
             "w_co": 128, "w_1": 256, "w_2": 256, "b_gate": 2}
SMALL = ("g_mix", "g_cross", "g_mem", "g_mlp", "g_final", "sink")


def _params(sem=None):
    return pltpu.CompilerParams(dimension_semantics=sem, vmem_limit_bytes=VMEM_LIMIT)


def _dot(a, b):
    return jnp.dot(a, b, preferred_element_type=F32)


def _dot_nt(a, b):
    return lax.dot_general(a, b, NT, preferred_element_type=F32)


def _dot_tn(a, b):
    return lax.dot_general(a, b, TN, preferred_element_type=F32)


def _rms(xt):
    return lax.rsqrt(jnp.mean(xt * xt, axis=-1, keepdims=True) + EPS)


def _rms_bwd(dh, xt, r, g):
    xn = xt * r
    dxn = dh * g
    dx = r * (dxn - xn * jnp.mean(dxn * xn, axis=-1, keepdims=True))
    return dx, jnp.sum(dh * xn, axis=0, keepdims=True)


def _rope(x, c, s, swa, sign):
    kinds = "qqqqkv" if swa else "qkvqkv"
    cq, sq = c * Q_SCALE, s * (sign * Q_SCALE)
    sk = s * sign if sign != 1 else s
    out = []
    for ci, kind in enumerate(kinds):
        xc = x[:, ci * 128:(ci + 1) * 128]
        if kind == "v":
            out.append(xc)
        elif kind == "q":
            out.append(xc * cq + pltpu.roll(xc, 64, 1) * sq)
        else:
            out.append(xc * c + pltpu.roll(xc, 64, 1) * sk)
    return jnp.concatenate(out, axis=1)


def _lane_scratch(rows, w):
    return pltpu.VMEM((w // 128, rows, 128), F32)


def _deinterleave(val, scr_ref, dst_ref, dtype):
    d, n = dst_ref.shape[0], dst_ref.shape[1]
    nc = val.shape[1] // 128
    for c in range(nc):
        scr_ref[c] = val[:, c * 128:(c + 1) * 128]
    for r in range(d):
        rows = [scr_ref.at[c][pl.ds(r, n, stride=d), :] for c in range(nc)]
        dst_ref[r] = jnp.concatenate(rows, axis=1).astype(dtype)


def _res_spec(a, tm):
    d, w = a.shape[0], a.shape[2]
    return pl.BlockSpec((d, tm // d, w), lambda i: (0, i, 0))


def _interleave(src_ref, scr_ref):
    d, n = src_ref.shape[0], src_ref.shape[1]
    nc = src_ref.shape[2] // 128
    for r in range(d):
        v = src_ref[r].astype(F32)
        for c in range(nc):
            scr_ref.at[c][pl.ds(r, n, stride=d), :] = v[:, c * 128:(c + 1) * 128]
    return jnp.concatenate([scr_ref[c] for c in range(nc)], axis=1)


def _with_dep(body, n_in, dep):
    if dep is None:
        return body
    return lambda *refs: body(*refs[:n_in], *refs[n_in + 1:])


def _dep_spec(dep):
    return [] if dep is None else [pl.BlockSpec(memory_space=pl.ANY)]


def _dep_arg(dep):
    return [] if dep is None else [dep]


def _inproj(x, g, w_p, tabs, tm, dep=None):
    t = x.shape[0]
    gw = 2 * PBLK
    (cos, sin), (cos1, sin1), (cos2, sin2) = tabs[0], tabs[1], tabs[2]

    def body(x_ref, g_ref, w_ref, c_ref, s_ref, c1_ref, s1_ref, c2_ref, s2_ref,
             h_ref, h1_ref, h2_ref, p0_ref, p1_ref, p2_ref, pb_ref, hf_ref):
        xt = x_ref[...]
        hf = xt * _rms(xt) * g_ref[...]
        h_ref[...] = hf.astype(BF16)
        _deinterleave(hf, hf_ref, h1_ref, BF16)
        _deinterleave(hf, hf_ref, h2_ref, BF16)
        rows = lambda ref: ref[...].reshape(tm, ref.shape[-1])
        groups = ((h_ref, c_ref, s_ref, p0_ref), (h1_ref, c1_ref, s1_ref, p1_ref), (h2_ref, c2_ref, s2_ref, p2_ref))
        for gi, (lhs_ref, cc_ref, ss_ref, out_ref) in enumerate(groups):
            lhs, cc, ss = rows(lhs_ref), rows(cc_ref), rows(ss_ref)
            for half in range(2):
                col = gi * gw + half * PBLK
                val = _rope(_dot(lhs, w_ref[:, col:col + PBLK]), cc, ss, False, 1).astype(BF16)
                if out_ref.ndim == 3:
                    out_ref[:, :, half * PBLK:(half + 1) * PBLK] = val.reshape(out_ref.shape[:2] + (PBLK,))
                else:
                    out_ref[:, half * PBLK:(half + 1) * PBLK] = val
        pb_ref[...] = _rope(_dot(h_ref[...], w_ref[:, 3 * gw:]), c_ref[...], s_ref[...], True, 1).astype(BF16)

    d1, d2 = DIL_GROUPS[1][1], DIL_GROUPS[2][1]
    row = lambda w: pl.BlockSpec((tm, w), lambda i: (i, 0))
    res = lambda d, w: pl.BlockSpec((d, tm // d, w), lambda i: (0, i, 0))
    sds = jax.ShapeDtypeStruct
    return pl.pallas_call(
        _with_dep(body, 9, dep), name="inproj", grid=(t // tm,),
        in_specs=[row(D_MODEL), pl.BlockSpec((1, D_MODEL), lambda i: (0, 0)),
                  pl.BlockSpec((D_MODEL, P_WIDTH), lambda i: (0, 0), pipeline_mode=pl.Buffered(1)),
                  row(128), row(128), res(d1, 128), res(d1, 128), res(d2, 128), res(d2, 128)] + _dep_spec(dep),
        out_specs=[row(D_MODEL), res(d1, D_MODEL), res(d2, D_MODEL), row(gw), res(d1, gw), res(d2, gw), row(PBLK)],
        out_shape=[sds((t, D_MODEL), BF16), sds((d1, t // d1, D_MODEL), BF16), sds((d2, t // d2, D_MODEL), BF16),
                   sds((t, gw), BF16), sds((d1, t // d1, gw), BF16), sds((d2, t // d2, gw), BF16),
                   sds((t, PBLK), BF16)],
        scratch_shapes=[_lane_scratch(tm, D_MODEL)],
        compiler_params=_params(("arbitrary",)),
    )(x, g, w_p, cos, sin, cos1, sin1, cos2, sin2, *_dep_arg(dep))


def _gates(h, w_g, b, tm, tn):
    t = h.shape[0]

    def body(h_ref, w_ref, b_ref, o_ref):
        z = _dot(h_ref[...], w_ref[...]) + b_ref[...]
        o_ref[...] = jax.nn.sigmoid(z).astype(BF16)

    return pl.pallas_call(
        body, name="gates", grid=(t // tm, GATE_WIDTH // tn),
        in_specs=[pl.BlockSpec((tm, D_MODEL), lambda i, j: (i, 0)),
                  pl.BlockSpec((D_MODEL, tn), lambda i, j: (0, j)),
                  pl.BlockSpec((1, tn), lambda i, j: (0, j))],
        out_specs=pl.BlockSpec((tm, tn), lambda i, j: (i, j)),
        out_shape=jax.ShapeDtypeStruct((t, GATE_WIDTH), BF16),
        compiler_params=_params(("arbitrary", "arbitrary")),
    )(h, w_g, b)


def _band_mask(i, s):
    row = lax.broadcasted_iota(jnp.int32, (BAND, 2 * BAND), 0)
    col = lax.broadcasted_iota(jnp.int32, (BAND, 2 * BAND), 1)
    band = (col >= row) & (col <= row + BAND)
    if s == 0:
        band = band & ((col >= BAND) | (i > 0))
    return band


def _head_a_masks(rows):
    lane = lax.broadcasted_iota(jnp.int32, (rows, 128), 1)
    return (lane % HEAD_DIM) < HEAD_DIM // 2, lane < HEAD_DIM


def _stack_heads(x, head_a):
    zero = jnp.zeros_like(x)
    return jnp.concatenate([jnp.where(head_a, x, zero), jnp.where(head_a, zero, x)], axis=0)


def _kv_rows(cur_ref, tail_ref, s, off):
    if s == 0:
        return jnp.concatenate([tail_ref[:, off:off + 128], cur_ref[0:BAND, off:off + 128]], axis=0)
    return cur_ref[(s - 1) * BAND:(s + 1) * BAND, off:off + 128]


def _attn_layout(swa):
    if swa:
        return [(128 * j, 512, 640) for j in range(4)]
    return [(0, 128, 256), (384, 512, 640)]


def _attn_fwd(name, pv, swa, sinks, tq):
    d, ls = pv.shape[0], pv.shape[1]
    n, nsb = ls // tq, tq // BAND
    pairs = _attn_layout(swa)
    ncol = 1 if swa else 2
    ow = 128 * len(pairs)

    def body(cur_ref, tail_ref, *rest):
        sink_ref, o_ref, lse_ref, o32_ref = rest if swa else (None,) + rest + (None,)
        i = pl.program_id(2)
        lane = lax.broadcasted_iota(jnp.int32, (BAND, 128), 1)
        qk_a, v_a = _head_a_masks(BAND)
        first = lax.broadcasted_iota(jnp.int32, (2 * BAND, 1), 0) < BAND
        for s in range(nsb):
            mask = _band_mask(i, s)
            mask2 = jnp.concatenate([mask, mask], axis=0)
            rows = slice(s * BAND, (s + 1) * BAND)
            lse_tile = jnp.zeros((BAND, 128), F32)
            for j, (qo, ko, vo) in enumerate(pairs):
                q = cur_ref[rows, qo:qo + 128]
                kk = _kv_rows(cur_ref, tail_ref, s, ko)
                vv = _kv_rows(cur_ref, tail_ref, s, vo)
                sc = _dot_nt(_stack_heads(q, qk_a), kk)
                sc = jnp.where(mask2, sc, -jnp.inf)
                m = jnp.max(sc, axis=-1, keepdims=True)
                if swa:
                    sk = jnp.where(first, sink_ref[2 * j], sink_ref[2 * j + 1])
                    m = jnp.maximum(m, sk)
                p = jnp.exp(sc - m)
                den = jnp.sum(p, axis=-1, keepdims=True)
                if swa:
                    den = den + jnp.exp(sk - m)
                lse = m + jnp.log(den)
                lse_tile = jnp.where(lane == 2 * j, lse[:BAND], jnp.where(lane == 2 * j + 1, lse[BAND:], lse_tile))
                o2 = _dot((p * (1.0 / den)).astype(BF16), vv)
                o = jnp.where(v_a, o2[:BAND], o2[BAND:])
                o_ref[rows, j * 128:(j + 1) * 128] = o.astype(BF16)
                if swa:
                    o32_ref[rows, j * 128:(j + 1) * 128] = o
            lse_ref[rows, :] = lse_tile

    in_specs = [pl.BlockSpec((None, tq, PBLK), lambda r, cb, i: (r, i, cb)),
                pl.BlockSpec((None, BAND, PBLK), lambda r, cb, i: (r, jnp.maximum(i * nsb - 1, 0), cb))]
    args = [pv, pv]
    out_specs = [pl.BlockSpec((None, tq, ow), lambda r, cb, i: (r, i, cb)),
                 pl.BlockSpec((None, tq, 128), lambda r, cb, i: (r, i, cb))]
    out_shape = [jax.ShapeDtypeStruct((d, ls, 512), BF16), jax.ShapeDtypeStruct((d, ls, 128 * ncol), F32)]
    if swa:
        in_specs.append(pl.BlockSpec(memory_space=pltpu.SMEM))
        args.append(sinks)
        out_specs.append(out_specs[0])
        out_shape.append(jax.ShapeDtypeStruct((d, ls, 512), F32))
    return pl.pallas_call(
        body, name=name, grid=(d, ncol, n),
        in_specs=in_specs, out_specs=out_specs, out_shape=out_shape,
        compiler_params=_params(("arbitrary", "arbitrary", "arbitrary")),
    )(*args)


def _lse_lane(head):
    return (head // 4) * 128 + head % 4


def _dil_head_spread():
    lane = lax.broadcasted_iota(jnp.int32, (256, 512), 0)
    head = lax.broadcasted_iota(jnp.int32, (256, 512), 1) // HEAD_DIM
    return (lane == _lse_lane(head)).astype(BF16)


def _head_scale(x, tile, spread):
    return x * _dot(tile.astype(BF16), spread)


def _head_gather(width, dil):
    head = lax.broadcasted_iota(jnp.int32, (8 * HEAD_DIM, width), 0) // HEAD_DIM
    lane = lax.broadcasted_iota(jnp.int32, (8 * HEAD_DIM, width), 1)
    return (lane == (_lse_lane(head) if dil else head)).astype(BF16)


def _head_sums(x, gather):
    hi = x.astype(BF16)
    lo = (x - hi.astype(F32)).astype(BF16)
    return _dot(hi, gather) + _dot(lo, gather)


def _alphas(l0, l1, l2):
    m = jnp.maximum(jnp.maximum(l0, l1), l2)
    e0, e1, e2 = jnp.exp(l0 - m), jnp.exp(l1 - m), jnp.exp(l2 - m)
    den = e0 + e1 + e2
    return e0 / den, e1 / den, e2 / den


def _mix(o0, o1, o2, l0, l1, l2, ob, gts, x, w_a, w_b, w_out, g_cross, tm):
    t = x.shape[0]

    def body(o0_ref, o1_ref, o2_ref, l0_ref, l1_ref, l2_ref, ob_ref, g_ref, x_ref, wa_ref, wb_ref, wo_ref,
             gc_ref, oa_ref, ya_ref, yb_ref, mg_ref, x1_ref, hc_ref, so_ref, sl_ref):
        a0, a1, a2 = _alphas(l0_ref[...], _interleave(l1_ref, sl_ref), _interleave(l2_ref, sl_ref))
        spread = _dil_head_spread()
        oa = (_head_scale(o0_ref[...].astype(F32), a0, spread)
              + _head_scale(_interleave(o1_ref, so_ref), a1, spread)
              + _head_scale(_interleave(o2_ref, so_ref), a2, spread))
        oab = oa.astype(BF16)
        oa_ref[...] = oab
        ya = _dot(oab, wa_ref[...])
        yb = _dot(ob_ref[...], wb_ref[...])
        ya_ref[...] = ya.astype(BF16)
        yb_ref[...] = yb.astype(BF16)
        merged = (g_ref[:, :D_MODEL].astype(F32) * ya + g_ref[:, D_MODEL:].astype(F32) * yb).astype(BF16)
        mg_ref[...] = merged
        x1 = x_ref[...] + _dot(merged, wo_ref[...])
        x1_ref[...] = x1
        hc_ref[...] = (x1 * _rms(x1) * gc_ref[...]).astype(BF16)

    row = lambda w: pl.BlockSpec((tm, w), lambda i: (i, 0))
    full = lambda a, b: pl.BlockSpec((a, b), lambda i: (0, 0))
    return pl.pallas_call(
        body, name="mix", grid=(t // tm,),
        in_specs=[row(512), _res_spec(o1, tm), _res_spec(o2, tm), row(256), _res_spec(l1, tm), _res_spec(l2, tm),
                  row(512), row(GATE_WIDTH),
                  row(D_MODEL), full(512, D_MODEL), full(512, D_MODEL), full(D_MODEL, D_MODEL), full(1, D_MODEL)],
        out_specs=[row(512), row(D_MODEL), row(D_MODEL), row(D_MODEL), row(D_MODEL), row(D_MODEL)],
        out_shape=[jax.ShapeDtypeStruct((t, 512), BF16), jax.ShapeDtypeStruct((t, D_MODEL), BF16),
                   jax.ShapeDtypeStruct((t, D_MODEL), BF16), jax.ShapeDtypeStruct((t, D_MODEL), BF16),
                   jax.ShapeDtypeStruct((t, D_MODEL), F32), jax.ShapeDtypeStruct((t, D_MODEL), BF16)],
        scratch_shapes=[_lane_scratch(tm, 512), _lane_scratch(tm, 256)],
        compiler_params=_params(("arbitrary",)),
    )(o0, o1, o2, l0, l1, l2, ob, gts, x, w_a, w_b, w_out, g_cross)


def _memkv(mem, g_mem, w_ckv):
    m = mem.shape[0]
    ws = w_ckv.shape[2]

    def body(mem_ref, g_ref, w_ref, mn_ref, kv_ref):
        xt = mem_ref[...]
        mn = (xt * _rms(xt) * g_ref[...]).astype(BF16)
        mn_ref[...] = mn
        for j in range(N_DEV):
            kv_ref[:, j * ws:(j + 1) * ws] = _dot(mn, w_ref[j]).astype(BF16)

    return pl.pallas_call(
        body, name="memkv",
        out_shape=[jax.ShapeDtypeStruct((m, D_MODEL), BF16), jax.ShapeDtypeStruct((m, 2 * D_MODEL), BF16)],
        compiler_params=_params(),
    )(mem, g_mem, w_ckv)


def _cross_probs(q, kv_ref, h):
    k = kv_ref[:, h * X_HEAD_DIM:(h + 1) * X_HEAD_DIM]
    sc = _dot_nt(q[:, h * X_HEAD_DIM:(h + 1) * X_HEAD_DIM], k)
    m = jnp.max(sc, axis=-1, keepdims=True)
    p = jnp.exp(sc - m)
    return p / jnp.sum(p, axis=-1, keepdims=True)


def _cross(hc, x1, kv, w_cq, w_co, g_mlp, tm):
    t = x1.shape[0]
    m = kv.shape[0]

    def body(hc_ref, x1_ref, kv_ref, wq_ref, wo_ref, g_ref, q_ref, o_ref, x2_ref, hm_ref):
        q = (_dot(hc_ref[...], wq_ref[...]) * X_SCALE).astype(BF16)
        q_ref[...] = q
        outs = []
        for h in range(X_HEADS):
            p = _cross_probs(q, kv_ref, h)
            v = kv_ref[:, D_MODEL + h * X_HEAD_DIM:D_MODEL + (h + 1) * X_HEAD_DIM]
            outs.append(_dot(p.astype(BF16), v))
        o = jnp.concatenate(outs, axis=1).astype(BF16)
        o_ref[...] = o
        x2 = x1_ref[...] + _dot(o, wo_ref[...])
        x2_ref[...] = x2
        hm_ref[...] = (x2 * _rms(x2) * g_ref[...]).astype(BF16)

    row = lambda w: pl.BlockSpec((tm, w), lambda i: (i, 0))
    full = lambda a, b: pl.BlockSpec((a, b), lambda i: (0, 0))
    return pl.pallas_call(
        body, name="cross", grid=(t // tm,),
        in_specs=[row(D_MODEL), row(D_MODEL), full(m, 2 * D_MODEL), full(D_MODEL, D_MODEL),
                  full(D_MODEL, D_MODEL), full(1, D_MODEL)],
        out_specs=[row(D_MODEL)] * 4,
        out_shape=[jax.ShapeDtypeStruct((t, D_MODEL), BF16), jax.ShapeDtypeStruct((t, D_MODEL), BF16),
                   jax.ShapeDtypeStruct((t, D_MODEL), F32), jax.ShapeDtypeStruct((t, D_MODEL), BF16)],
        compiler_params=_params(("arbitrary",)),
    )(hc, x1, kv, w_cq, w_co, g_mlp)


def _mlp(hm, x2, w_1, w_2, g_final, target, tm, tf):
    t = x2.shape[0]
    nf = D_FF // tf

    def body(hm_ref, x2_ref, w1_ref, w2_ref, g_ref, tg_ref, a_ref, dx3_ref, loss_ref, dg_ref, acc_ref):
        i, f = pl.program_id(0), pl.program_id(1)
        hm_t = hm_ref[...]
        sw = w1_ref.shape[2]
        part = None
        for s in range(w1_ref.shape[0]):
            a = jnp.maximum(_dot(hm_t, w1_ref[s]), 0.0).astype(BF16)
            a_ref[:, s * sw:(s + 1) * sw] = a
            p_s = _dot(a * a, w2_ref[s * sw:(s + 1) * sw, :])
            part = p_s if part is None else part + p_s

        @pl.when(f == 0)
        def _():
            acc_ref[...] = part

        @pl.when(f > 0)
        def _():
            acc_ref[...] += part

        @pl.when((i == 0) & (f == 0))
        def _():
            loss_ref[...] = jnp.zeros_like(loss_ref)
            dg_ref[...] = jnp.zeros_like(dg_ref)

        @pl.when(f == nf - 1)
        def _():
            x3 = x2_ref[...] + acc_ref[...]
            r = _rms(x3)
            g = g_ref[...]
            diff = x3 * r * g - tg_ref[...]
            loss_ref[...] += 0.5 * jnp.sum(jnp.mean(diff * diff, axis=-1, keepdims=True))
            dx3, dg = _rms_bwd(diff / D_MODEL, x3, r, g)
            dx3_ref[...] = dx3
            dg_ref[...] += dg

    return pl.pallas_call(
        body, name="mlp", grid=(t // tm, nf),
        in_specs=[pl.BlockSpec((tm, D_MODEL), lambda i, f: (i, 0)),
                  pl.BlockSpec((tm, D_MODEL), lambda i, f: (i, 0)),
                  pl.BlockSpec((tf // w_1.shape[2], D_MODEL, w_1.shape[2]), lambda i, f: (f, 0, 0)),
                  pl.BlockSpec((tf, D_MODEL), lambda i, f: (f, 0)),
                  pl.BlockSpec((1, D_MODEL), lambda i, f: (0, 0)),
                  pl.BlockSpec((tm, D_MODEL), lambda i, f: (i, 0))],
        out_specs=[pl.BlockSpec((tm, tf), lambda i, f: (i, f)),
                   pl.BlockSpec((tm, D_MODEL), lambda i, f: (i, 0)),
                   pl.BlockSpec((1, 128), lambda i, f: (0, 0)),
                   pl.BlockSpec((1, D_MODEL), lambda i, f: (0, 0))],
        out_shape=[jax.ShapeDtypeStruct((t, D_FF), BF16), jax.ShapeDtypeStruct((t, D_MODEL), F32),
                   jax.ShapeDtypeStruct((1, 128), F32), jax.ShapeDtypeStruct((1, D_MODEL), F32)],
        scratch_shapes=[pltpu.VMEM((tm, D_MODEL), F32)],
        compiler_params=_params(("arbitrary", "arbitrary")),
    )(hm, x2, w_1, w_2, g_final, target)


def _mlp_bwd(dx3, a, w_1, w_2, x2, g_mlp, tm, tf):
    t = x2.shape[0]
    nf = D_FF // tf

    def body(dx3_ref, a_ref, w1_ref, w2_ref, x2_ref, g_ref, dz_ref, dx2_ref, dg_ref, acc_ref):
        i, f = pl.program_id(0), pl.program_id(1)
        da2 = _dot_nt(dx3_ref[...].astype(BF16), w2_ref[...])
        dz = (2.0 * a_ref[...].astype(F32) * da2).astype(BF16)
        dz_ref[...] = dz
        sw = w1_ref.shape[2]
        part = _dot_nt(dz[:, 0:sw], w1_ref[0])
        for s in range(1, w1_ref.shape[0]):
            part = part + _dot_nt(dz[:, s * sw:(s + 1) * sw], w1_ref[s])

        @pl.when(f == 0)
        def _():
            acc_ref[...] = part

        @pl.when(f > 0)
        def _():
            acc_ref[...] += part

        @pl.when((i == 0) & (f == 0))
        def _():
            dg_ref[...] = jnp.zeros_like(dg_ref)

        @pl.when(f == nf - 1)
        def _():
            xt = x2_ref[...]
            dx, dg = _rms_bwd(acc_ref[...], xt, _rms(xt), g_ref[...])
            dx2_ref[...] = dx3_ref[...] + dx
            dg_ref[...] += dg

    return pl.pallas_call(
        body, name="mlp_bwd", grid=(t // tm, nf),
        in_specs=[pl.BlockSpec((tm, D_MODEL), lambda i, f: (i, 0)),
                  pl.BlockSpec((tm, tf), lambda i, f: (i, f)),
                  pl.BlockSpec((tf // w_1.shape[2], D_MODEL, w_1.shape[2]), lambda i, f: (f, 0, 0)),
                  pl.BlockSpec((tf, D_MODEL), lambda i, f: (f, 0)),
                  pl.BlockSpec((tm, D_MODEL), lambda i, f: (i, 0)),
                  pl.BlockSpec((1, D_MODEL), lambda i, f: (0, 0))],
        out_specs=[pl.BlockSpec((tm, tf), lambda i, f: (i, f)),
                   pl.BlockSpec((tm, D_MODEL), lambda i, f: (i, 0)),
                   pl.BlockSpec((1, D_MODEL), lambda i, f: (0, 0))],
        out_shape=[jax.ShapeDtypeStruct((t, D_FF), BF16), jax.ShapeDtypeStruct((t, D_MODEL), F32),
                   jax.ShapeDtypeStruct((1, D_MODEL), F32)],
        scratch_shapes=[pltpu.VMEM((tm, D_MODEL), F32)],
        compiler_params=_params(("arbitrary", "arbitrary")),
    )(dx3, a, w_1, w_2, x2, g_mlp)


def _wgrad(name, a, b, tka, tn, tm, square=False, col_shards=False):
    t, ka = a.shape
    n = b.shape[1]
    nk = t // tm

    def body(a_ref, b_ref, o_ref, acc_ref):
        at = a_ref[...].astype(BF16)
        if square:
            at = at * at
        part = _dot_tn(at, b_ref[...].astype(BF16))
        k = pl.program_id(2)

        @pl.when(k == 0)
        def _():
            acc_ref[...] = part

        @pl.when(k > 0)
        def _():
            acc_ref[...] += part

        @pl.when(k == nk - 1)
        def _():
            if col_shards:
                for s in range(tn // sw):
                    o_ref[s] = acc_ref[:, s * sw:(s + 1) * sw].astype(BF16)
            else:
                o_ref[...] = acc_ref[...].astype(BF16)

    if col_shards:
        sw = n // N_DEV
        out_spec = pl.BlockSpec((tn // sw, tka, sw), lambda p, q, k: (q, p, 0))
        out_shape = jax.ShapeDtypeStruct((N_DEV, ka, sw), BF16)
    else:
        out_spec = pl.BlockSpec((tka, tn), lambda p, q, k: (p, q))
        out_shape = jax.ShapeDtypeStruct((ka, n), BF16)
    return pl.pallas_call(
        body, name=name, grid=(ka // tka, n // tn, nk),
        in_specs=[pl.BlockSpec((tm, tka), lambda p, q, k: (k, p)),
                  pl.BlockSpec((tm, tn), lambda p, q, k: (k, q))],
        out_specs=out_spec, out_shape=out_shape,
        scratch_shapes=[pltpu.VMEM((tka, tn), F32)],
        compiler_params=_params(("arbitrary", "arbitrary", "arbitrary")),
    )(a, b)


def _cross_bwd(dx2, x1, q, kv, w_cq, w_co, g_cross, tm, dep=None):
    t = x1.shape[0]
    m = kv.shape[0]

    def body(dx2_ref, x1_ref, q_ref, kv_ref, wq_ref, wo_ref, g_ref, dq_ref, dx1_ref, dkv_ref, dg_ref):
        @pl.when(pl.program_id(0) == 0)
        def _():
            dkv_ref[...] = jnp.zeros_like(dkv_ref)
            dg_ref[...] = jnp.zeros_like(dg_ref)

        do = _dot_nt(dx2_ref[...].astype(BF16), wo_ref[...]).astype(BF16)
        q = q_ref[...]
        dqs = []
        for h in range(X_HEADS):
            hs = slice(h * X_HEAD_DIM, (h + 1) * X_HEAD_DIM)
            vs = slice(D_MODEL + h * X_HEAD_DIM, D_MODEL + (h + 1) * X_HEAD_DIM)
            p = _cross_probs(q, kv_ref, h)
            dp = _dot_nt(do[:, hs], kv_ref[:, vs])
            ds = (p * (dp - jnp.sum(dp * p, axis=-1, keepdims=True))).astype(BF16)
            dqs.append(_dot(ds, kv_ref[:, hs]))
            dkv_ref[:, hs] += _dot_tn(ds, q[:, hs])
            dkv_ref[:, vs] += _dot_tn(p.astype(BF16), do[:, hs])
        dq = (jnp.concatenate(dqs, axis=1) * X_SCALE).astype(BF16)
        dq_ref[...] = dq
        xt = x1_ref[...]
        dx, dg = _rms_bwd(_dot_nt(dq, wq_ref[...]), xt, _rms(xt), g_ref[...])
        dx1_ref[...] = dx2_ref[...] + dx
        dg_ref[...] += dg

    row = lambda w: pl.BlockSpec((tm, w), lambda i: (i, 0))
    full = lambda a, b: pl.BlockSpec((a, b), lambda i: (0, 0))
    return pl.pallas_call(
        _with_dep(body, 7, dep), name="cross_bwd", grid=(t // tm,),
        in_specs=[row(D_MODEL), row(D_MODEL), row(D_MODEL), full(m, 2 * D_MODEL), full(D_MODEL, D_MODEL),
                  full(D_MODEL, D_MODEL), full(1, D_MODEL)] + _dep_spec(dep),
        out_specs=[row(D_MODEL), row(D_MODEL), full(m, 2 * D_MODEL), full(1, D_MODEL)],
        out_shape=[jax.ShapeDtypeStruct((t, D_MODEL), BF16), jax.ShapeDtypeStruct((t, D_MODEL), F32),
                   jax.ShapeDtypeStruct((m, 2 * D_MODEL), F32), jax.ShapeDtypeStruct((1, D_MODEL), F32)],
        compiler_params=_params(("arbitrary",)),
    )(dx2, x1, q, kv, w_cq, w_co, g_cross, *_dep_arg(dep))


def _memkv_bwd(dkv, mn, mem, w_ckv, g_mem):
    ws = w_ckv.shape[2]

    def body(dkv_ref, mn_ref, mem_ref, w_ref, g_ref, dw_ref, dg_ref):
        mn = mn_ref[...]
        dmn = jnp.zeros(mn.shape, F32)
        for j in range(N_DEV):
            dkvb = dkv_ref[:, j * ws:(j + 1) * ws].astype(BF16)
            dw_ref[j] = _dot_tn(mn, dkvb).astype(BF16)
            dmn = dmn + _dot_nt(dkvb, w_ref[j])
        xt = mem_ref[...]
        dg_ref[...] = jnp.sum(dmn * xt * _rms(xt), axis=0, keepdims=True)

    return pl.pallas_call(
        body, name="memkv_bwd",
        out_shape=[jax.ShapeDtypeStruct(w_ckv.shape, BF16), jax.ShapeDtypeStruct((1, D_MODEL), F32)],
        compiler_params=_params(),
    )(dkv, mn, mem, w_ckv, g_mem)


def _merge_bwd(dx1, ya, yb, gts, w_out, w_g, tm):
    t = dx1.shape[0]

    def body(dx1_ref, ya_ref, yb_ref, g_ref, wo_ref, wg_ref, dg_ref, dhp_ref, dya_ref, dyb_ref, db_ref):
        @pl.when(pl.program_id(0) == 0)
        def _():
            db_ref[...] = jnp.zeros_like(db_ref)

        dm = _dot_nt(dx1_ref[...].astype(BF16), wo_ref[...])
        ga = g_ref[:, :D_MODEL].astype(F32)
        gb = g_ref[:, D_MODEL:].astype(F32)
        dya_ref[...] = (dm * ga).astype(BF16)
        dyb_ref[...] = (dm * gb).astype(BF16)
        dpa = dm * ya_ref[...].astype(F32) * ga * (1.0 - ga)
        dpb = dm * yb_ref[...].astype(F32) * gb * (1.0 - gb)
        dpre = jnp.concatenate([dpa, dpb], axis=1)
        db_ref[...] += jnp.sum(dpre, axis=0, keepdims=True)
        dpreb = dpre.astype(BF16)
        dg_ref[...] = dpreb
        dhp_ref[...] = _dot_nt(dpreb, wg_ref[...])

    row = lambda w: pl.BlockSpec((tm, w), lambda i: (i, 0))
    once = lambda a, b: pl.BlockSpec((a, b), lambda i: (0, 0), pipeline_mode=pl.Buffered(1))
    sds = jax.ShapeDtypeStruct
    return pl.pallas_call(
        body, name="merge_bwd", grid=(t // tm,),
        in_specs=[row(D_MODEL), row(D_MODEL), row(D_MODEL), row(GATE_WIDTH),
                  once(D_MODEL, D_MODEL), once(D_MODEL, GATE_WIDTH)],
        out_specs=[row(GATE_WIDTH), row(D_MODEL), row(D_MODEL), row(D_MODEL),
                   pl.BlockSpec((1, GATE_WIDTH), lambda i: (0, 0))],
        out_shape=[sds((t, GATE_WIDTH), BF16), sds((t, D_MODEL), F32), sds((t, D_MODEL), BF16),
                   sds((t, D_MODEL), BF16), sds((1, GATE_WIDTH), F32)],
        compiler_params=_params(("arbitrary",)),
    )(dx1, ya, yb, gts, w_out, w_g)


def _combine_bwd(dya, dyb, oa, ob, l0, l1, l2, lb, sink_row, w_a, w_b, tm):
    t = dya.shape[0]

    def body(dya_ref, dyb_ref, oa_ref, ob_ref, l0_ref, l1_ref, l2_ref, lb_ref, sk_ref, wa_ref, wb_ref,
             do0_ref, do1_ref, do2_ref, c0_ref, c1_ref, c2_ref, dob_ref, cb_ref, dsk_ref, so_ref, sl_ref):
        @pl.when(pl.program_id(0) == 0)
        def _():
            dsk_ref[...] = jnp.zeros_like(dsk_ref)

        doa = _dot_nt(dya_ref[...], wa_ref[...])
        dob = _dot_nt(dyb_ref[...], wb_ref[...])
        dsum = _head_sums(doa * oa_ref[...].astype(F32), _head_gather(256, True))
        a0, a1, a2 = _alphas(l0_ref[...], _interleave(l1_ref, sl_ref), _interleave(l2_ref, sl_ref))
        c0_ref[...] = a0 * dsum
        spread = _dil_head_spread()
        do0_ref[...] = _head_scale(doa, a0, spread).astype(BF16)
        for al, do_ref, c_ref in ((a1, do1_ref, c1_ref), (a2, do2_ref, c2_ref)):
            _deinterleave(al * dsum, sl_ref, c_ref, F32)
            _deinterleave(_head_scale(doa, al, spread), so_ref, do_ref, BF16)
        dob_ref[...] = dob.astype(BF16)
        cb = _head_sums(dob * ob_ref[...], _head_gather(128, False))
        cb_ref[...] = cb
        lane = lax.broadcasted_iota(jnp.int32, cb.shape, 1)
        psink = jnp.where(lane < 8, jnp.exp(sk_ref[...] - lb_ref[...]), 0.0)
        dsk_ref[...] += jnp.sum(-psink * cb, axis=0, keepdims=True)

    row = lambda w: pl.BlockSpec((tm, w), lambda i: (i, 0))
    full = lambda a, b: pl.BlockSpec((a, b), lambda i: (0, 0))
    sds = jax.ShapeDtypeStruct
    d1, d2 = l1.shape[0], l2.shape[0]
    res = lambda d, w: pl.BlockSpec((d, tm // d, w), lambda i: (0, i, 0))
    return pl.pallas_call(
        body, name="combine_bwd", grid=(t // tm,),
        in_specs=[row(D_MODEL), row(D_MODEL), row(512), row(512),
                  row(256), _res_spec(l1, tm), _res_spec(l2, tm), row(128), full(1, 128),
                  full(512, D_MODEL), full(512, D_MODEL)],
        out_specs=[row(512), res(d1, 512), res(d2, 512), row(256), res(d1, 256), res(d2, 256),
                   row(512), row(128), full(1, 128)],
        out_shape=[sds((t, 512), BF16), sds((d1, t // d1, 512), BF16),
                   sds((d2, t // d2, 512), BF16), sds((t, 256), F32), sds((d1, t // d1, 256), F32),
                   sds((d2, t // d2, 256), F32), sds((t, 512), BF16),
                   sds((t, 128), F32), sds((1, 128), F32)],
        scratch_shapes=[_lane_scratch(tm, 512), _lane_scratch(tm, 256)],
        compiler_params=_params(("arbitrary",)),
    )(dya, dyb, oa, ob, l0, l1, l2, lb, sink_row, w_a, w_b)


def _attn_bwd(name, pv, dov, lsev, cv, cosv, sinv, swa, tq, dep=None):
    d, ls = pv.shape[0], pv.shape[1]
    n, nsb = ls // tq, tq // BAND
    pairs = _attn_layout(swa)
    ncol = 1 if swa else 2
    ow = 128 * len(pairs)

    def body(cur_ref, tail_ref, do_ref, lse_ref, c_ref, cos_ref, sin_ref, out_ref, acc_ref, carry_ref):
        i = pl.program_id(2)
        blk_i = n - 1 - i
        acc_ref[...] = jnp.zeros_like(acc_ref)

        @pl.when(i == 0)
        def _():
            carry_ref[...] = jnp.zeros_like(carry_ref)

        qk_a, v_a = _head_a_masks(BAND)
        for s in range(nsb):
            mask = _band_mask(blk_i, s)
            mask2 = jnp.concatenate([mask, mask], axis=0)
            rows = slice(s * BAND, (s + 1) * BAND)
            krows = slice(s * BAND, (s + 2) * BAND)
            for j, (qo, ko, vo) in enumerate(pairs):
                kk = _kv_rows(cur_ref, tail_ref, s, ko)
                vv = _kv_rows(cur_ref, tail_ref, s, vo)
                q2 = _stack_heads(cur_ref[rows, qo:qo + 128], qk_a)
                do2 = _stack_heads(do_ref[rows, j * 128:(j + 1) * 128], v_a)
                col2 = lambda ref: jnp.concatenate([ref[rows, 2 * j:2 * j + 1], ref[rows, 2 * j + 1:2 * j + 2]], axis=0)
                sc = _dot_nt(q2, kk)
                p = jnp.exp(jnp.where(mask2, sc, -jnp.inf) - col2(lse_ref))
                dp = _dot_nt(do2, vv)
                ds = (p * (dp - col2(c_ref))).astype(BF16)
                dq2 = _dot(ds, kk)
                acc_ref[BAND + s * BAND:BAND + (s + 1) * BAND, qo:qo + 128] += jnp.where(qk_a, dq2[:BAND], dq2[BAND:])
                acc_ref[krows, ko:ko + 128] += _dot_tn(ds, q2)
                acc_ref[krows, vo:vo + 128] += _dot_tn(p.astype(BF16), do2)

        last = acc_ref[tq:, :] + carry_ref[...]
        fin = last if tq == BAND else jnp.concatenate([acc_ref[BAND:tq, :], last], axis=0)
        out_ref[...] = _rope(fin, cos_ref[...], sin_ref[...], swa, -1).astype(BF16)
        carry_ref[...] = acc_ref[0:BAND, :]

    rev = lambda i: n - 1 - i
    blk = lambda rows, w, row_of: pl.BlockSpec((None, rows, w), lambda r, cb, i: (r, row_of(i), cb))
    tab = pl.BlockSpec((None, tq, 128), lambda r, cb, i: (r, rev(i), 0))
    return pl.pallas_call(
        _with_dep(body, 7, dep), name=name, grid=(d, ncol, n),
        in_specs=[blk(tq, PBLK, rev), blk(BAND, PBLK, lambda i: jnp.maximum(rev(i) * nsb - 1, 0)),
                  blk(tq, ow, rev), blk(tq, 128, rev), blk(tq, 128, rev), tab, tab] + _dep_spec(dep),
        out_specs=blk(tq, PBLK, rev),
        out_shape=jax.ShapeDtypeStruct((d, ls, ncol * PBLK), BF16),
        scratch_shapes=[pltpu.VMEM((tq + BAND, PBLK), F32), pltpu.VMEM((BAND, PBLK), F32)],
        compiler_params=_params(("arbitrary", "arbitrary", "arbitrary")),
    )(pv, pv, dov, lsev, cv, cosv, sinv, *_dep_arg(dep))


def _dx(dp0, dp1, dp2, dpb, w_p, dh_part, dx1, x, g_mix, tm, dep=None):
    t = x.shape[0]
    gw = 2 * PBLK

    def body(dp0_ref, dp1_ref, dp2_ref, dpb_ref, w_ref, dhp_ref, dx1_ref, x_ref, g_ref, gx_ref, dg_ref,
             dpt_ref, scr_ref):
        @pl.when(pl.program_id(0) == 0)
        def _():
            dg_ref[...] = jnp.zeros_like(dg_ref)

        dpt_ref[:, 0:gw] = dp0_ref[...]
        dpt_ref[:, gw:2 * gw] = _interleave(dp1_ref, scr_ref).astype(BF16)
        dpt_ref[:, 2 * gw:3 * gw] = _interleave(dp2_ref, scr_ref).astype(BF16)
        dpt_ref[:, 3 * gw:] = dpb_ref[...]
        dh = _dot_nt(dpt_ref[...], w_ref[...]) + dhp_ref[...]
        xt = x_ref[...]
        dx, dg = _rms_bwd(dh, xt, _rms(xt), g_ref[...])
        gx_ref[...] = dx1_ref[...] + dx
        dg_ref[...] += dg

    row = lambda w: pl.BlockSpec((tm, w), lambda i: (i, 0))
    full = lambda a, b: pl.BlockSpec((a, b), lambda i: (0, 0))
    return pl.pallas_call(
        _with_dep(body, 9, dep), name="dx", grid=(t // tm,),
        in_specs=[row(gw), _res_spec(dp1, tm), _res_spec(dp2, tm), row(PBLK),
                  pl.BlockSpec((D_MODEL, P_WIDTH), lambda i: (0, 0), pipeline_mode=pl.Buffered(1)),
                  row(D_MODEL), row(D_MODEL), row(D_MODEL), full(1, D_MODEL)] + _dep_spec(dep),
        out_specs=[row(D_MODEL), full(1, D_MODEL)],
        out_shape=[jax.ShapeDtypeStruct((t, D_MODEL), F32), jax.ShapeDtypeStruct((1, D_MODEL), F32)],
        scratch_shapes=[pltpu.VMEM((tm, P_WIDTH), BF16), _lane_scratch(tm, gw)],
        compiler_params=_params(("arbitrary",)),
    )(dp0, dp1, dp2, dpb, w_p, dh_part, dx1, x, g_mix, *_dep_arg(dep))


MESH = pl.DeviceIdType.MESH
HBM_SPEC = pl.BlockSpec(memory_space=pltpu.HBM)
VMEM_SPEC = pl.BlockSpec(memory_space=pltpu.VMEM)


def _all_gather(xp):
    def body(x_ref, out_ref, send_sems, recv_sems, local_sem):
        x, y, c = lax.axis_index("x"), lax.axis_index("y"), lax.axis_index("c")
        me, sibling = (x, y, c), (x, y, 1 - c)
        chips = [(1 - x, y), (x, 1 - y), (1 - x, 1 - y)]

        def rows(px, py, pc):
            return out_ref.at[4 * px + 2 * py + pc]

        def copy(k, block, to, src=None):
            return pltpu.make_async_remote_copy(
                src_ref=rows(*block) if src is None else src, dst_ref=rows(*block),
                send_sem=send_sems.at[k], recv_sem=recv_sems.at[k], device_id=to, device_id_type=MESH)

        mine = pltpu.make_async_copy(x_ref, rows(*me), local_sem)
        mine.start()
        first = [copy(0, me, sibling, src=x_ref)]
        first += [copy(1 + j, me, (*chip, c), src=x_ref) for j, chip in enumerate(chips)]
        for cp in first:
            cp.start()
        passed = [copy(4 + j, (*chip, c), sibling) for j, chip in enumerate(chips)]
        for j, chip in enumerate(chips):
            copy(1 + j, (*chip, c), me).wait_recv()
            passed[j].start()
        copy(0, sibling, me).wait_recv()
        for j, chip in enumerate(chips):
            copy(4 + j, (*chip, 1 - c), me).wait_recv()
        for cp in first + passed:
            cp.wait_send()
        mine.wait()

    return pl.pallas_call(
        body, name="all_gather",
        out_shape=jax.ShapeDtypeStruct((N_DEV,) + xp.shape, xp.dtype),
        in_specs=[HBM_SPEC], out_specs=HBM_SPEC,
        scratch_shapes=[pltpu.SemaphoreType.DMA((7,)), pltpu.SemaphoreType.DMA((7,)), pltpu.SemaphoreType.DMA],
    )(xp)


def _peers():
    x, y, c = lax.axis_index("x"), lax.axis_index("y"), lax.axis_index("c")
    out = []
    for k in range(1, N_DEV):
        px = 1 - x if k & 4 else x
        py = 1 - y if k & 2 else y
        pc = 1 - c if k & 1 else c
        out.append((k, (px, py, pc), 4 * px + 2 * py + pc))
    return out


def _my_index():
    return 4 * lax.axis_index("x") + 2 * lax.axis_index("y") + lax.axis_index("c")


SEM_SPEC = pl.BlockSpec(memory_space=pltpu.SEMAPHORE)
ANY_SPEC = pl.BlockSpec(memory_space=pl.ANY)
_SPLIT_PARAMS = pltpu.CompilerParams(has_side_effects=pltpu.SideEffectType.DATAFLOW_SIDE_EFFECTING)


def _split_copies(gather, src_refs, land_refs, send_sems, recv_sems):
    me_idx = _my_index()
    out = []
    for a, (src_ref, land_ref) in enumerate(zip(src_refs, land_refs)):
        for k, peer, peer_idx in _peers():
            if gather:
                src, dst = src_ref, land_ref.at[me_idx]
            else:
                src, dst = src_ref.at[peer_idx], land_ref.at[k - 1]
            out.append(pltpu.make_async_remote_copy(
                src_ref=src, dst_ref=dst, send_sem=send_sems.at[7 * a + k - 1], recv_sem=recv_sems.at[7 * a + k - 1],
                device_id=peer, device_id_type=MESH))
    return out


def _split_start(name, gather, srcs):
    n = len(srcs)

    def body(*refs):
        send_sems, recv_sems = refs[n], refs[n + 1]
        for cp in _split_copies(gather, refs[:n], refs[2 * n + 2:3 * n + 2], send_sems, recv_sems):
            cp.start()
        token = refs[-1]
        token[...] = jnp.zeros_like(token)

    lands = [pltpu.HBM((N_DEV,) + a.shape if gather else (N_DEV - 1,) + a.shape[1:], a.dtype) for a in srcs]
    return pl.pallas_call(
        body, name=name,
        out_shape=(pltpu.SemaphoreType.DMA((7 * n,)), pltpu.SemaphoreType.DMA((7 * n,)),
                   *[pltpu.HBM(a.shape, a.dtype) for a in srcs], *lands, jax.ShapeDtypeStruct((8, 128), F32)),
        in_specs=(HBM_SPEC,) * n, out_specs=(SEM_SPEC, SEM_SPEC) + (HBM_SPEC,) * (2 * n) + (VMEM_SPEC,),
        input_output_aliases={i: 2 + i for i in range(n)}, compiler_params=_SPLIT_PARAMS,
    )(*[pltpu.with_memory_space_constraint(a, pltpu.HBM) for a in srcs])


def _split_wait(name, gather, started, after):
    send_sems, recv_sems, bufs = started[0], started[1], started[2:-1]
    n = len(bufs) // 2

    def body(*refs):
        for cp in _split_copies(gather, refs[:n], refs[n:2 * n], refs[2 * n], refs[2 * n + 1]):
            cp.wait_send()
            cp.wait_recv()

    out = pl.pallas_call(
        body, name=name, out_shape=tuple(pltpu.HBM(a.shape, a.dtype) for a in bufs),
        in_specs=(HBM_SPEC,) * (2 * n) + (SEM_SPEC, SEM_SPEC, ANY_SPEC), out_specs=(HBM_SPEC,) * (2 * n),
        input_output_aliases={i: i for i in range(2 * n)}, compiler_params=_SPLIT_PARAMS,
    )(*bufs, send_sems, recv_sems, after)
    return out[:n], out[n:]


def _adam_update(g, w, m, v):
    nm = ADAM_B1 * m + (1.0 - ADAM_B1) * g
    nv = ADAM_B2 * v + (1.0 - ADAM_B2) * (g * g)
    m_hat = nm / (1.0 - ADAM_B1 ** ADAM_STEP)
    v_hat = nv / (1.0 - ADAM_B2 ** ADAM_STEP)
    return -ADAM_LR * (m_hat / (jnp.sqrt(v_hat) + ADAM_EPS) + ADAM_WD * w), nm, nv


def _adamw(name, me, sent, got, w, m, v, tr):
    r, c = w.shape

    def body(me_ref, own_ref, got_ref, w_ref, m_ref, v_ref, g_ref, d_ref, nm_ref, nv_ref):
        g = own_ref[...].astype(F32)
        for k in range(N_DEV - 1):
            g = g + got_ref[k].astype(F32)
        g_ref[...] = g
        d_ref[...], nm_ref[...], nv_ref[...] = _adam_update(g, w_ref[...], m_ref[...], v_ref[...])

    blk = pl.BlockSpec((tr, c), lambda i, me_ref: (i, 0))
    return pl.pallas_call(
        body, name=name,
        grid_spec=pltpu.PrefetchScalarGridSpec(
            num_scalar_prefetch=1, grid=(r // tr,),
            in_specs=[pl.BlockSpec((None, tr, c), lambda i, me_ref: (me_ref[0], i, 0)),
                      pl.BlockSpec((N_DEV - 1, tr, c), lambda i, me_ref: (0, i, 0)), blk, blk, blk],
            out_specs=[blk] * 4),
        out_shape=[jax.ShapeDtypeStruct((r, c), F32)] * 4,
        compiler_params=_params(("arbitrary",)),
    )(me, sent, got, w, m, v)


def _adamw_small(srecv, ws, ms, vs):
    nv_ = len(ws)

    def body(*refs):
        s_ref = refs[0]
        ins, outs = refs[1:1 + 3 * nv_], refs[1 + 3 * nv_:]
        g_all = s_ref[0]
        for k in range(1, N_DEV):
            g_all = g_all + s_ref[k]
        for i in range(nv_):
            n = ins[i].shape[1]
            g = g_all[i:i + 1, :n]
            d, nm, nv = _adam_update(g, ins[i][...], ins[nv_ + i][...], ins[2 * nv_ + i][...])
            outs[i][...], outs[nv_ + i][...], outs[2 * nv_ + i][...], outs[3 * nv_ + i][...] = g, d, nm, nv
        outs[-1][...] = g_all[nv_:nv_ + 1, :128]

    shapes = [jax.ShapeDtypeStruct(a.shape, F32) for a in ws]
    res = pl.pallas_call(body, name="adamw_small", out_shape=shapes * 4 + [jax.ShapeDtypeStruct((1, 128), F32)],
                         compiler_params=_params())(srecv, *ws, *ms, *vs)
    return [res[k * nv_:(k + 1) * nv_] for k in range(4)], res[-1]


def _cols_from_shards(a):
    return jnp.swapaxes(a, 0, 1).reshape(a.shape[1], N_DEV * a.shape[2])


def _shards_from_cols(a):
    return jnp.swapaxes(a.reshape(a.shape[0], N_DEV, a.shape[1] // N_DEV), 0, 1)


def _shards_from_rows(a):
    return a.reshape(N_DEV, a.shape[0] // N_DEV, a.shape[1])


def _pair_lanes(a):
    lead = a.shape[:-1]
    return a.reshape(lead + (2, 2, HEAD_DIM // 2)).swapaxes(-3, -2).reshape(lead + (128,))


def _split_w_in(w_in):
    rows = w_in.shape[0]
    dil = w_in[:, :3 * DIL_WIDTH].reshape(rows, 3, 3, 4, 128)
    dil = jnp.concatenate([_pair_lanes(dil[:, :2]), dil[:, 2:]], axis=1)
    dil = dil.transpose(0, 2, 3, 1, 4).reshape(rows, 3 * DIL_WIDTH)
    o = 3 * DIL_WIDTH
    qb = w_in[:, o:o + SWA_Q_WIDTH].reshape(rows, 2, 4, HEAD_DIM).transpose(0, 2, 1, 3).reshape(rows, 4, 128)
    qb = _pair_lanes(qb).reshape(rows, SWA_Q_WIDTH)
    kb = _pair_lanes(w_in[:, o + SWA_Q_WIDTH:o + SWA_Q_WIDTH + SWA_KV_WIDTH])
    vb = w_in[:, o + SWA_Q_WIDTH + SWA_KV_WIDTH:P_WIDTH]
    return jnp.concatenate([dil, qb, kb, vb], axis=1), w_in[:, P_WIDTH:]


def _merge_w_in(dw_p, dw_g):
    rows = dw_p.shape[0]
    dil = dw_p[:, :3 * DIL_WIDTH].reshape(rows, 3, 4, 3, 128).transpose(0, 3, 1, 2, 4)
    dil = jnp.concatenate([_pair_lanes(dil[:, :2]), dil[:, 2:]], axis=1).reshape(rows, 3 * DIL_WIDTH)
    o = 3 * DIL_WIDTH
    qb = _pair_lanes(dw_p[:, o:o + SWA_Q_WIDTH].reshape(rows, 4, 128))
    qb = qb.reshape(rows, 4, 2, HEAD_DIM).transpose(0, 2, 1, 3).reshape(rows, SWA_Q_WIDTH)
    kb = _pair_lanes(dw_p[:, o + SWA_Q_WIDTH:o + SWA_Q_WIDTH + SWA_KV_WIDTH])
    vb = dw_p[:, o + SWA_Q_WIDTH + SWA_KV_WIDTH:]
    return jnp.concatenate([dil, qb, kb, vb, dw_g], axis=1)


def _swa_rows(w_b):
    return w_b.reshape(2, 4, HEAD_DIM, -1).transpose(1, 0, 2, 3).reshape(SWA_Q_WIDTH, -1)


def _swa_rows_inv(dw_b):
    return dw_b.reshape(4, 2, HEAD_DIM, -1).transpose(1, 0, 2, 3).reshape(SWA_Q_WIDTH, -1)


def _rope_tables(pos):
    half = HEAD_DIM // 2
    inv = ROPE_THETA ** (-jnp.arange(half, dtype=F32) / half)
    ang = pos.astype(F32)[:, None] * jnp.tile(inv, 4)
    sign = jnp.repeat(jnp.array([-1.0, 1.0], F32), 2 * half)
    return jnp.cos(ang), jnp.sin(ang) * sign


def _local_step(x, mem, pos, target, w_in, dep, rest_weights, on_grads, g_mix, g_cross, g_mem, g_mlp, g_final, sink):
    t = x.shape[0]
    tm = min(512, t)
    tq = 1024
    tw = min(2048, t)
    w_p, w_g = _split_w_in(w_in)
    cos, sin = lax.optimization_barrier(_rope_tables(pos))
    sink_row = jnp.pad(sink.reshape(2, 4).T.reshape(1, 8), ((0, 0), (0, 120)))
    tabs = [(cos[None], sin[None])]
    for _, d in DIL_GROUPS[1:]:
        tabs.append(tuple(a.reshape(t // d, d, 128).swapaxes(0, 1) for a in (cos, sin)))
    tabs.append(tabs[0])

    h, h1, h2, p0, p1, p2, pb = _inproj(x, g_mix, w_p, [(cos, sin), tabs[1], tabs[2]], tm, dep)
    ps = [p0[None], p1, p2, pb[None]]
    outs, lses = [], []
    for gi, pv in enumerate(ps):
        res = _attn_fwd(f"attn_fwd{gi}", pv, gi == 3, sink_row[0, :8], min(tq, pv.shape[1]))
        outs.append(res[0])
        lses.append(res[1])
    o0, l0, ob, lb, ob32 = outs[0][0], lses[0][0], outs[3][0], lses[3][0], res[2][0]
    wts = rest_weights(lb)
    w_b = _swa_rows(wts["w_branch_b"])
    tf = 2048
    gts = _gates(h, w_g, wts["b_gate"].reshape(1, GATE_WIDTH), tm, 1024)
    oa, ya, yb, merged, x1, hc = _mix(o0, outs[1], outs[2], l0, lses[1], lses[2], ob, gts, x,
                                      wts["w_branch_a"], w_b, wts["w_out"], g_cross, tm)
    mn, kv = _memkv(mem, g_mem, wts["w_ckv"])
    q, o, x2, hm = _cross(hc, x1, kv, wts["w_cq"], wts["w_co"], g_mlp, tm)
    a, dx3, loss, dg_final = _mlp(hm, x2, wts["w_1"], wts["w_2"], g_final.reshape(1, D_MODEL), target, tm, tf)

    grads = {}
    dz, dx2, dg_mlp = _mlp_bwd(dx3, a, wts["w_1"], wts["w_2"], x2, g_mlp, tm, tf)
    grads["w_2"] = _shards_from_rows(_wgrad("dw_2", a, dx3, 1024, 1024, tw, square=True))
    grads["w_1"] = _wgrad("dw_1", hm, dz, 1024, 1024, tw, col_shards=True)
    dep = on_grads(GROUP_A, grads)
    dq, dx1, dkv, dg_cross = _cross_bwd(dx2, x1, q, kv, wts["w_cq"], wts["w_co"], g_cross, tm, dep)
    grads["w_co"] = _shards_from_rows(_wgrad("dw_co", o, dx2, 1024, 1024, tw))
    grads["w_cq"] = _shards_from_rows(_wgrad("dw_cq", hc, dq, 1024, 1024, tw))
    grads["w_ckv"], dg_mem = _memkv_bwd(dkv, mn, mem, wts["w_ckv"], g_mem)
    dgt, dh_part, dya, dyb, db_gate = _merge_bwd(dx1, ya, yb, gts, wts["w_out"], w_g, tm)
    do0, do1, do2, c0, c1, c2, dob, cb, dsink = _combine_bwd(
        dya, dyb, oa, ob32, l0, lses[1], lses[2], lb, sink_row, wts["w_branch_a"], w_b, tm)
    grads["w_out"] = _shards_from_rows(_wgrad("dw_out", merged, dx1, 1024, 1024, tw))
    grads["w_branch_a"] = _shards_from_cols(_wgrad("dw_a", oa, dya, 512, 1024, tw))
    grads["w_branch_b"] = _shards_from_cols(_swa_rows_inv(_wgrad("dw_b", ob, dyb, 512, 1024, tw)))
    grads["b_gate"] = _shards_from_cols(db_gate.reshape(2, D_MODEL)).astype(BF16)
    dep = on_grads(GROUP_B, grads)
    dw_g = _wgrad("dw_g", h, dgt, 1024, 1024, tw)
    dps = []
    for gi, (pv, do_g, c_g) in enumerate(zip(ps, (do0[None], do1, do2, dob[None]), (c0[None], c1, c2, cb[None]))):
        dps.append(_attn_bwd(f"attn_bwd{gi}", pv, do_g, lses[gi], c_g, tabs[gi][0], tabs[gi][1], gi == 3,
                             min(tq, pv.shape[1]),
                             dep if gi == 0 else None))
    dw_p = jnp.concatenate(
        [_wgrad(f"dw_p{gi}", hh.reshape(t, D_MODEL), dpg.reshape(t, -1), 1024, PBLK, tw)
         for gi, (hh, dpg) in enumerate(zip((h, h1, h2, h), dps))], axis=1)
    grads["w_in"] = _shards_from_cols(_merge_w_in(dw_p, dw_g))
    dep = on_grads(GROUP_C, grads)
    grad_x, dg_mix = _dx(dps[0][0], dps[1], dps[2], dps[3][0], w_p, dh_part, dx1, x, g_mix, tm, dep)
    dsink_heads = dsink[0, :8].reshape(4, 2).T.reshape(8)
    small = {"g_mix": dg_mix[0], "g_cross": dg_cross[0], "g_mem": dg_mem[0], "g_mlp": dg_mlp[0],
             "g_final": dg_final[0], "sink": dsink_heads}
    return loss[0, 0], grad_x, small


def kernel(x, mem, positions, g_mix, w_in, b_gate, sink, w_branch_a, w_branch_b, w_out, g_cross, g_mem, w_cq, w_ckv, w_co, g_mlp, w_1, w_2, g_final, loss_target, m_g_mix, m_w_in, m_b_gate, m_sink, m_w_branch_a, m_w_branch_b, m_w_out, m_g_cross, m_g_mem, m_w_cq, m_w_ckv, m_w_co, m_g_mlp, m_w_1, m_w_2, m_g_final, v_g_mix, v_w_in, v_b_gate, v_sink, v_w_branch_a, v_w_branch_b, v_w_out, v_g_cross, v_g_mem, v_w_cq, v_w_ckv, v_w_co, v_g_mlp, v_w_1, v_w_2, v_g_final):
    local = dict(locals())
    shard = {n: local[n][0] for n in GROUP_A + GROUP_B + GROUP_C}
    me = _my_index()
    me_arr = me.reshape(1).astype(jnp.int32)
    tags = {GROUP_A: "a", GROUP_B: "b", GROUP_C: "c"}

    w_in_full = _cols_from_shards(_all_gather(shard["w_in"].astype(BF16)))
    rest = GROUP_A + GROUP_B

    def gathered(name, started, after):
        srcs, lands = _split_wait(name, True, started, after)
        return [lax.dynamic_update_slice(land, src[None], (me,) + (0,) * src.ndim) for src, land in zip(srcs, lands)]

    gather = _split_start("gather_start", True,
                          [shard[n] if n == "b_gate" else shard[n].astype(BF16) for n in rest])

    def rest_weights(after):
        full = {}
        for name, a in zip(rest, gathered("gather_wait", gather, after)):
            if name in ("w_1", "w_ckv"):
                full[name] = a
            elif name in _COL_SHARDED:
                full[name] = _cols_from_shards(a)
            else:
                full[name] = a.reshape(N_DEV * a.shape[1], a.shape[2])
        return full

    scatters = {}

    def on_grads(names, grads):
        scatters[names] = _split_start("scatter_start_" + tags[names], False, [grads[n] for n in names])
        return scatters[names][-1]

    loss, grad_x, small = _local_step(
        x[0], mem[0], positions[0], loss_target[0], w_in_full, gather[-1], rest_weights, on_grads,
        g_mix, g_cross, g_mem, g_mlp, g_final, sink[0])

    sp = jnp.stack([small[n] if n != "sink" else jnp.pad(small[n], (0, LANES - 8)) for n in SMALL]
                   + [jnp.pad(loss.reshape(1), (0, LANES - 1)), jnp.zeros((LANES,), F32)])
    small_gather = _split_start("small_start", True, [sp])

    after, updated = small_gather[-1], {}
    for names in (GROUP_A, GROUP_B, GROUP_C):
        sent, got = _split_wait("scatter_wait_" + tags[names], False, scatters[names], after)
        for i, name in enumerate(names):
            outs = _adamw("adamw_" + name, me_arr, sent[i], got[i], shard[name],
                          local["m_" + name][0], local["v_" + name][0], ADAM_ROWS[name])
            updated[name] = [a[None] for a in outs]
            after = outs[3]

    flat = lambda prefix: [local[prefix + n].reshape(1, -1) for n in SMALL]
    outs, loss_row = _adamw_small(gathered("small_wait", small_gather, after)[0], flat(""), flat("m_"), flat("v_"))
    for i, name in enumerate(SMALL):
        updated[name] = [outs[which][i].reshape(local[name].shape) for which in range(4)]

    order = ["g_mix", "w_in", "b_gate", "sink", "w_branch_a", "w_branch_b", "w_out", "g_cross", "g_mem", "w_cq",
             "w_ckv", "w_co", "g_mlp", "w_1", "w_2", "g_final"]
    res = [loss_row[0, 0], grad_x[None]]
    for which in range(4):
        res += [updated[n][which] for n in order]
    return tuple(res)
```

```python
import functools
import math

import jax
import jax.numpy as jnp
from jax import lax
from jax.experimental import pallas as pl
from jax.experimental.pallas import tpu as pltpu

F32 = jnp.float32
BF16 = jnp.bfloat16

D_MODEL = 1024
HEAD_DIM = 64
DIL_GROUPS = ((128, 1), (512, 4), (2048, 16))
ROPE_THETA = 10000.0
X_HEADS = 4
X_HEAD_DIM = D_MODEL // X_HEADS
D_FF = 4 * D_MODEL
EPS = 1e-6
DIL_WIDTH = 1536
SWA_Q_WIDTH = 512
SWA_KV_WIDTH = 128
P_WIDTH = 3 * DIL_WIDTH + SWA_Q_WIDTH + 2 * SWA_KV_WIDTH
GATE_WIDTH = 2 * D_MODEL
IN_WIDTH = P_WIDTH + GATE_WIDTH
BAND = 128
PBLK = 768
Q_SCALE = HEAD_DIM ** -0.5
X_SCALE = X_HEAD_DIM ** -0.5

ADAM_LR = 0.001
ADAM_B1 = 0.9
ADAM_B2 = 0.999
ADAM_EPS = 1e-08
ADAM_WD = 0.01
ADAM_STEP = 10

N_DEV = 8
LANES = 1024
VMEM_LIMIT = 52 * 1024 * 1024

NT = (((1,), (1,)), ((), ()))
TN = (((0,), (0,)), ((), ()))

GROUP_A = ("w_1", "w_2")
GROUP_B = ("w_branch_a", "w_branch_b", "w_out", "w_cq", "w_ckv", "w_co", "b_gate")
GROUP_C = ("w_in",)
_COL_SHARDED = ("w_in", "w_branch_a", "w_branch_b", "w_ckv", "w_1", "b_gate")
ADAM_ROWS = {"w_in": 256, "w_branch_a": 512, "w_branch_b": 512, "w_out": 128, "w_cq": 128, "w_ckv": 512,
             "w_co": 128, "w_1": 256, "w_2": 256, "b_gate": 2}
SMALL = ("g_mix", "g_cross", "g_mem", "g_mlp", "g_final", "sink")


def _params(sem=None):
    return pltpu.CompilerParams(dimension_semantics=sem, vmem_limit_bytes=VMEM_LIMIT)


def _dot(a, b):
    return jnp.dot(a, b, preferred_element_type=F32)


def _dot_nt(a, b):
    return lax.dot_general(a, b, NT, preferred_element_type=F32)


def _dot_tn(a, b):
    return lax.dot_general(a, b, TN, preferred_element_type=F32)


def _rms(xt):
    return lax.rsqrt(jnp.mean(xt * xt, axis=-1, keepdims=True) + EPS)


def _rms_bwd(dh, xt, r, g):
    xn = xt * r
    dxn = dh * g
    dx = r * (dxn - xn * jnp.mean(dxn * xn, axis=-1, keepdims=True))
    return dx, jnp.sum(dh * xn, axis=0, keepdims=True)


def _rope(x, c, s, swa, sign):
    kinds = "qqqqkv" if swa else "qkvqkv"
    cq, sq = c * Q_SCALE, s * (sign * Q_SCALE)
    sk = s * sign if sign != 1 else s
    out = []
    for ci, kind in enumerate(kinds):
        xc = x[:, ci * 128:(ci + 1) * 128]
        if kind == "v":
            out.append(xc)
        elif kind == "q":
            out.append(xc * cq + pltpu.roll(xc, 64, 1) * sq)
        else:
            out.append(xc * c + pltpu.roll(xc, 64, 1) * sk)
    return jnp.concatenate(out, axis=1)


def _lane_scratch(rows, w):
    return pltpu.VMEM((w // 128, rows, 128), F32)


def _deinterleave(val, scr_ref, dst_ref, dtype):
    d, n = dst_ref.shape[0], dst_ref.shape[1]
    nc = val.shape[1] // 128
    for c in range(nc):
        scr_ref[c] = val[:, c * 128:(c + 1) * 128]
    for r in range(d):
        rows = [scr_ref.at[c][pl.ds(r, n, stride=d), :] for c in range(nc)]
        dst_ref[r] = jnp.concatenate(rows, axis=1).astype(dtype)


def _res_spec(a, tm):
    d, w = a.shape[0], a.shape[2]
    return pl.BlockSpec((d, tm // d, w), lambda i: (0, i, 0))


def _interleave(src_ref, scr_ref):
    d, n = src_ref.shape[0], src_ref.shape[1]
    nc = src_ref.shape[2] // 128
    for r in range(d):
        v = src_ref[r].astype(F32)
        for c in range(nc):
            scr_ref.at[c][pl.ds(r, n, stride=d), :] = v[:, c * 128:(c + 1) * 128]
    return jnp.concatenate([scr_ref[c] for c in range(nc)], axis=1)


def _with_dep(body, n_in, dep):
    if dep is None:
        return body
    return lambda *refs: body(*refs[:n_in], *refs[n_in + 1:])


def _dep_spec(dep):
    return [] if dep is None else [pl.BlockSpec(memory_space=pl.ANY)]


def _dep_arg(dep):
    return [] if dep is None else [dep]


def _inproj(x, g, w_p, tabs, tm, dep=None):
    t = x.shape[0]
    gw = 2 * PBLK
    (cos, sin), (cos1, sin1), (cos2, sin2) = tabs[0], tabs[1], tabs[2]

    def body(x_ref, g_ref, w_ref, c_ref, s_ref, c1_ref, s1_ref, c2_ref, s2_ref,
             h_ref, h1_ref, h2_ref, p0_ref, p1_ref, p2_ref, pb_ref, hf_ref):
        xt = x_ref[...]
        hf = xt * _rms(xt) * g_ref[...]
        h_ref[...] = hf.astype(BF16)
        _deinterleave(hf, hf_ref, h1_ref, BF16)
        _deinterleave(hf, hf_ref, h2_ref, BF16)
        rows = lambda ref: ref[...].reshape(tm, ref.shape[-1])
        groups = ((h_ref, c_ref, s_ref, p0_ref), (h1_ref, c1_ref, s1_ref, p1_ref), (h2_ref, c2_ref, s2_ref, p2_ref))
        for gi, (lhs_ref, cc_ref, ss_ref, out_ref) in enumerate(groups):
            lhs, cc, ss = rows(lhs_ref), rows(cc_ref), rows(ss_ref)
            for half in range(2):
                col = gi * gw + half * PBLK
                val = _rope(_dot(lhs, w_ref[:, col:col + PBLK]), cc, ss, False, 1).astype(BF16)
                if out_ref.ndim == 3:
                    out_ref[:, :, half * PBLK:(half + 1) * PBLK] = val.reshape(out_ref.shape[:2] + (PBLK,))
                else:
                    out_ref[:, half * PBLK:(half + 1) * PBLK] = val
        pb_ref[...] = _rope(_dot(h_ref[...], w_ref[:, 3 * gw:]), c_ref[...], s_ref[...], True, 1).astype(BF16)

    d1, d2 = DIL_GROUPS[1][1], DIL_GROUPS[2][1]
    row = lambda w: pl.BlockSpec((tm, w), lambda i: (i, 0))
    res = lambda d, w: pl.BlockSpec((d, tm // d, w), lambda i: (0, i, 0))
    sds = jax.ShapeDtypeStruct
    return pl.pallas_call(
        _with_dep(body, 9, dep), name="inproj", grid=(t // tm,),
        in_specs=[row(D_MODEL), pl.BlockSpec((1, D_MODEL), lambda i: (0, 0)),
                  pl.BlockSpec((D_MODEL, P_WIDTH), lambda i: (0, 0), pipeline_mode=pl.Buffered(1)),
                  row(128), row(128), res(d1, 128), res(d1, 128), res(d2, 128), res(d2, 128)] + _dep_spec(dep),
        out_specs=[row(D_MODEL), res(d1, D_MODEL), res(d2, D_MODEL), row(gw), res(d1, gw), res(d2, gw), row(PBLK)],
        out_shape=[sds((t, D_MODEL), BF16), sds((d1, t // d1, D_MODEL), BF16), sds((d2, t // d2, D_MODEL), BF16),
                   sds((t, gw), BF16), sds((d1, t // d1, gw), BF16), sds((d2, t // d2, gw), BF16),
                   sds((t, PBLK), BF16)],
        scratch_shapes=[_lane_scratch(tm, D_MODEL)],
        compiler_params=_params(("arbitrary",)),
    )(x, g, w_p, cos, sin, cos1, sin1, cos2, sin2, *_dep_arg(dep))


def _gates(h, w_g, b, tm, tn):
    t = h.shape[0]

    def body(h_ref, w_ref, b_ref, o_ref):
        z = _dot(h_ref[...], w_ref[...]) + b_ref[...]
        o_ref[...] = jax.nn.sigmoid(z).astype(BF16)

    return pl.pallas_call(
        body, name="gates", grid=(t // tm, GATE_WIDTH // tn),
        in_specs=[pl.BlockSpec((tm, D_MODEL), lambda i, j: (i, 0)),
                  pl.BlockSpec((D_MODEL, tn), lambda i, j: (0, j)),
                  pl.BlockSpec((1, tn), lambda i, j: (0, j))],
        out_specs=pl.BlockSpec((tm, tn), lambda i, j: (i, j)),
        out_shape=jax.ShapeDtypeStruct((t, GATE_WIDTH), BF16),
        compiler_params=_params(("arbitrary", "arbitrary")),
    )(h, w_g, b)


def _band_mask(i, s):
    row = lax.broadcasted_iota(jnp.int32, (BAND, 2 * BAND), 0)
    col = lax.broadcasted_iota(jnp.int32, (BAND, 2 * BAND), 1)
    band = (col >= row) & (col <= row + BAND)
    if s == 0:
        band = band & ((col >= BAND) | (i > 0))
    return band


def _head_a_masks(rows):
    lane = lax.broadcasted_iota(jnp.int32, (rows, 128), 1)
    return (lane % HEAD_DIM) < HEAD_DIM // 2, lane < HEAD_DIM


def _stack_heads(x, head_a):
    zero = jnp.zeros_like(x)
    return jnp.concatenate([jnp.where(head_a, x, zero), jnp.where(head_a, zero, x)], axis=0)


def _stack_heads_t(xt, head_a_t):
    zero = jnp.zeros_like(xt)
    return jnp.concatenate([jnp.where(head_a_t, xt, zero), jnp.where(head_a_t, zero, xt)], axis=1)


def _kv_rows(cur_ref, tail_ref, s, off):
    if s == 0:
        return jnp.concatenate([tail_ref[:, off:off + 128], cur_ref[0:BAND, off:off + 128]], axis=0)
    return cur_ref[(s - 1) * BAND:(s + 1) * BAND, off:off + 128]


def _attn_layout(swa):
    if swa:
        return [(128 * j, 512, 640) for j in range(4)]
    return [(0, 128, 256), (384, 512, 640)]


def _attn_fwd(name, pv, swa, sinks, tq):
    d, ls = pv.shape[0], pv.shape[1]
    n, nsb = ls // tq, tq // BAND
    pairs = _attn_layout(swa)
    ncol = 1 if swa else 2
    ow = 128 * len(pairs)

    def body(cur_ref, tail_ref, *rest):
        sink_ref, o_ref, lse_ref, o32_ref = rest if swa else (None,) + rest + (None,)
        i = pl.program_id(2)
        lane = lax.broadcasted_iota(jnp.int32, (BAND, 128), 1)
        qk_a, v_a = _head_a_masks(BAND)
        first = lax.broadcasted_iota(jnp.int32, (2 * BAND, 1), 0) < BAND
        for s in range(nsb):
            mask = _band_mask(i, s)
            mask2 = jnp.concatenate([mask, mask], axis=0)
            rows = slice(s * BAND, (s + 1) * BAND)
            lse_tile = jnp.zeros((BAND, 128), F32)
            for j, (qo, ko, vo) in enumerate(pairs):
                q = cur_ref[rows, qo:qo + 128]
                kk = _kv_rows(cur_ref, tail_ref, s, ko)
                vv = _kv_rows(cur_ref, tail_ref, s, vo)
                sc = _dot_nt(_stack_heads(q, qk_a), kk)
                sc = jnp.where(mask2, sc, -jnp.inf)
                m = jnp.max(sc, axis=-1, keepdims=True)
                if swa:
                    sk = jnp.where(first, sink_ref[2 * j], sink_ref[2 * j + 1])
                    m = jnp.maximum(m, sk)
                p = jnp.exp(sc - m)
                den = jnp.sum(p, axis=-1, keepdims=True)
                if swa:
                    den = den + jnp.exp(sk - m)
                lse = m + jnp.log(den)
                lse_tile = jnp.where(lane == 2 * j, lse[:BAND], jnp.where(lane == 2 * j + 1, lse[BAND:], lse_tile))
                o2 = _dot((p * (1.0 / den)).astype(BF16), vv)
                o = jnp.where(v_a, o2[:BAND], o2[BAND:])
                o_ref[rows, j * 128:(j + 1) * 128] = o.astype(BF16)
                if swa:
                    o32_ref[rows, j * 128:(j + 1) * 128] = o
            lse_ref[rows, :] = lse_tile

    in_specs = [pl.BlockSpec((None, tq, PBLK), lambda r, cb, i: (r, i, cb)),
                pl.BlockSpec((None, BAND, PBLK), lambda r, cb, i: (r, jnp.maximum(i * nsb - 1, 0), cb))]
    args = [pv, pv]
    out_specs = [pl.BlockSpec((None, tq, ow), lambda r, cb, i: (r, i, cb)),
                 pl.BlockSpec((None, tq, 128), lambda r, cb, i: (r, i, cb))]
    out_shape = [jax.ShapeDtypeStruct((d, ls, 512), BF16), jax.ShapeDtypeStruct((d, ls, 128 * ncol), F32)]
    if swa:
        in_specs.append(pl.BlockSpec(memory_space=pltpu.SMEM))
        args.append(sinks)
        out_specs.append(out_specs[0])
        out_shape.append(jax.ShapeDtypeStruct((d, ls, 512), F32))
    return pl.pallas_call(
        body, name=name, grid=(d, ncol, n),
        in_specs=in_specs, out_specs=out_specs, out_shape=out_shape,
        compiler_params=_params(("arbitrary", "arbitrary", "arbitrary")),
    )(*args)


def _lse_lane(head):
    return (head // 4) * 128 + head % 4


def _dil_head_spread():
    lane = lax.broadcasted_iota(jnp.int32, (256, 512), 0)
    head = lax.broadcasted_iota(jnp.int32, (256, 512), 1) // HEAD_DIM
    return (lane == _lse_lane(head)).astype(BF16)


def _head_scale(x, tile, spread):
    return x * _dot(tile.astype(BF16), spread)


def _head_gather(width, dil):
    head = lax.broadcasted_iota(jnp.int32, (8 * HEAD_DIM, width), 0) // HEAD_DIM
    lane = lax.broadcasted_iota(jnp.int32, (8 * HEAD_DIM, width), 1)
    return (lane == (_lse_lane(head) if dil else head)).astype(BF16)


def _head_sums(x, gather):
    hi = x.astype(BF16)
    lo = (x - hi.astype(F32)).astype(BF16)
    return _dot(hi, gather) + _dot(lo, gather)


def _alphas(l0, l1, l2):
    m = jnp.maximum(jnp.maximum(l0, l1), l2)
    e0, e1, e2 = jnp.exp(l0 - m), jnp.exp(l1 - m), jnp.exp(l2 - m)
    den = e0 + e1 + e2
    return e0 / den, e1 / den, e2 / den


def _mix(o0, o1, o2, l0, l1, l2, ob, gts, x, w_a, w_b, w_out, g_cross, tm):
    t = x.shape[0]

    def body(o0_ref, o1_ref, o2_ref, l0_ref, l1_ref, l2_ref, ob_ref, g_ref, x_ref, wa_ref, wb_ref, wo_ref,
             gc_ref, oa_ref, ya_ref, yb_ref, mg_ref, x1_ref, hc_ref, so_ref, sl_ref):
        a0, a1, a2 = _alphas(l0_ref[...], _interleave(l1_ref, sl_ref), _interleave(l2_ref, sl_ref))
        spread = _dil_head_spread()
        oa = (_head_scale(o0_ref[...].astype(F32), a0, spread)
              + _head_scale(_interleave(o1_ref, so_ref), a1, spread)
              + _head_scale(_interleave(o2_ref, so_ref), a2, spread))
        oab = oa.astype(BF16)
        oa_ref[...] = oab
        ya = _dot(oab, wa_ref[...])
        yb = _dot(ob_ref[...], wb_ref[...])
        ya_ref[...] = ya.astype(BF16)
        yb_ref[...] = yb.astype(BF16)
        merged = (g_ref[:, :D_MODEL].astype(F32) * ya + g_ref[:, D_MODEL:].astype(F32) * yb).astype(BF16)
        mg_ref[...] = merged
        x1 = x_ref[...] + _dot(merged, wo_ref[...])
        x1_ref[...] = x1
        hc_ref[...] = (x1 * _rms(x1) * gc_ref[...]).astype(BF16)

    row = lambda w: pl.BlockSpec((tm, w), lambda i: (i, 0))
    full = lambda a, b: pl.BlockSpec((a, b), lambda i: (0, 0))
    return pl.pallas_call(
        body, name="mix", grid=(t // tm,),
        in_specs=[row(512), _res_spec(o1, tm), _res_spec(o2, tm), row(256), _res_spec(l1, tm), _res_spec(l2, tm),
                  row(512), row(GATE_WIDTH),
                  row(D_MODEL), full(512, D_MODEL), full(512, D_MODEL), full(D_MODEL, D_MODEL), full(1, D_MODEL)],
        out_specs=[row(512), row(D_MODEL), row(D_MODEL), row(D_MODEL), row(D_MODEL), row(D_MODEL)],
        out_shape=[jax.ShapeDtypeStruct((t, 512), BF16), jax.ShapeDtypeStruct((t, D_MODEL), BF16),
                   jax.ShapeDtypeStruct((t, D_MODEL), BF16), jax.ShapeDtypeStruct((t, D_MODEL), BF16),
                   jax.ShapeDtypeStruct((t, D_MODEL), F32), jax.ShapeDtypeStruct((t, D_MODEL), BF16)],
        scratch_shapes=[_lane_scratch(tm, 512), _lane_scratch(tm, 256)],
        compiler_params=_params(("arbitrary",)),
    )(o0, o1, o2, l0, l1, l2, ob, gts, x, w_a, w_b, w_out, g_cross)


def _memkv(mem, g_mem, w_ckv):
    m = mem.shape[0]
    ws = w_ckv.shape[2]

    def body(mem_ref, g_ref, w_ref, mn_ref, kv_ref):
        xt = mem_ref[...]
        mn = (xt * _rms(xt) * g_ref[...]).astype(BF16)
        mn_ref[...] = mn
        for j in range(N_DEV):
            kv_ref[:, j * ws:(j + 1) * ws] = _dot(mn, w_ref[j]).astype(BF16)

    return pl.pallas_call(
        body, name="memkv",
        out_shape=[jax.ShapeDtypeStruct((m, D_MODEL), BF16), jax.ShapeDtypeStruct((m, 2 * D_MODEL), BF16)],
        compiler_params=_params(),
    )(mem, g_mem, w_ckv)


def _cross_probs(q, kv_ref, h):
    k = kv_ref[:, h * X_HEAD_DIM:(h + 1) * X_HEAD_DIM]
    sc = _dot_nt(q[:, h * X_HEAD_DIM:(h + 1) * X_HEAD_DIM], k)
    m = jnp.max(sc, axis=-1, keepdims=True)
    p = jnp.exp(sc - m)
    return p / jnp.sum(p, axis=-1, keepdims=True)


def _cross(hc, x1, kv, w_cq, w_co, g_mlp, tm):
    t = x1.shape[0]
    m = kv.shape[0]

    def body(hc_ref, x1_ref, kv_ref, wq_ref, wo_ref, g_ref, q_ref, o_ref, x2_ref, hm_ref):
        q = (_dot(hc_ref[...], wq_ref[...]) * X_SCALE).astype(BF16)
        q_ref[...] = q
        outs = []
        for h in range(X_HEADS):
            p = _cross_probs(q, kv_ref, h)
            v = kv_ref[:, D_MODEL + h * X_HEAD_DIM:D_MODEL + (h + 1) * X_HEAD_DIM]
            outs.append(_dot(p.astype(BF16), v))
        o = jnp.concatenate(outs, axis=1).astype(BF16)
        o_ref[...] = o
        x2 = x1_ref[...] + _dot(o, wo_ref[...])
        x2_ref[...] = x2
        hm_ref[...] = (x2 * _rms(x2) * g_ref[...]).astype(BF16)

    row = lambda w: pl.BlockSpec((tm, w), lambda i: (i, 0))
    full = lambda a, b: pl.BlockSpec((a, b), lambda i: (0, 0))
    return pl.pallas_call(
        body, name="cross", grid=(t // tm,),
        in_specs=[row(D_MODEL), row(D_MODEL), full(m, 2 * D_MODEL), full(D_MODEL, D_MODEL),
                  full(D_MODEL, D_MODEL), full(1, D_MODEL)],
        out_specs=[row(D_MODEL)] * 4,
        out_shape=[jax.ShapeDtypeStruct((t, D_MODEL), BF16), jax.ShapeDtypeStruct((t, D_MODEL), BF16),
                   jax.ShapeDtypeStruct((t, D_MODEL), F32), jax.ShapeDtypeStruct((t, D_MODEL), BF16)],
        compiler_params=_params(("arbitrary",)),
    )(hc, x1, kv, w_cq, w_co, g_mlp)


def _mlp(hm, x2, w_1, w_2, g_final, target, tm, tf):
    t = x2.shape[0]
    nf = D_FF // tf

    def body(hm_ref, x2_ref, w1_ref, w2_ref, g_ref, tg_ref, a_ref, dx3_ref, loss_ref, dg_ref, acc_ref):
        i, f = pl.program_id(0), pl.program_id(1)
        hm_t = hm_ref[...]
        sw = w1_ref.shape[2]
        part = None
        for s in range(w1_ref.shape[0]):
            a = jnp.maximum(_dot(hm_t, w1_ref[s]), 0.0).astype(BF16)
            a_ref[:, s * sw:(s + 1) * sw] = a
            p_s = _dot(a * a, w2_ref[s * sw:(s + 1) * sw, :])
            part = p_s if part is None else part + p_s

        @pl.when(f == 0)
        def _():
            acc_ref[...] = part

        @pl.when(f > 0)
        def _():
            acc_ref[...] += part

        @pl.when((i == 0) & (f == 0))
        def _():
            loss_ref[...] = jnp.zeros_like(loss_ref)
            dg_ref[...] = jnp.zeros_like(dg_ref)

        @pl.when(f == nf - 1)
        def _():
            x3 = x2_ref[...] + acc_ref[...]
            r = _rms(x3)
            g = g_ref[...]
            diff = x3 * r * g - tg_ref[...]
            loss_ref[...] += 0.5 * jnp.sum(jnp.mean(diff * diff, axis=-1, keepdims=True))
            dx3, dg = _rms_bwd(diff / D_MODEL, x3, r, g)
            dx3_ref[...] = dx3
            dg_ref[...] += dg

    return pl.pallas_call(
        body, name="mlp", grid=(t // tm, nf),
        in_specs=[pl.BlockSpec((tm, D_MODEL), lambda i, f: (i, 0)),
                  pl.BlockSpec((tm, D_MODEL), lambda i, f: (i, 0)),
                  pl.BlockSpec((tf // w_1.shape[2], D_MODEL, w_1.shape[2]), lambda i, f: (f, 0, 0)),
                  pl.BlockSpec((tf, D_MODEL), lambda i, f: (f, 0)),
                  pl.BlockSpec((1, D_MODEL), lambda i, f: (0, 0)),
                  pl.BlockSpec((tm, D_MODEL), lambda i, f: (i, 0))],
        out_specs=[pl.BlockSpec((tm, tf), lambda i, f: (i, f)),
                   pl.BlockSpec((tm, D_MODEL), lambda i, f: (i, 0)),
                   pl.BlockSpec((1, 128), lambda i, f: (0, 0)),
                   pl.BlockSpec((1, D_MODEL), lambda i, f: (0, 0))],
        out_shape=[jax.ShapeDtypeStruct((t, D_FF), BF16), jax.ShapeDtypeStruct((t, D_MODEL), F32),
                   jax.ShapeDtypeStruct((1, 128), F32), jax.ShapeDtypeStruct((1, D_MODEL), F32)],
        scratch_shapes=[pltpu.VMEM((tm, D_MODEL), F32)],
        compiler_params=_params(("arbitrary", "arbitrary")),
    )(hm, x2, w_1, w_2, g_final, target)


def _mlp_bwd(dx3, a, w_1, w_2, x2, g_mlp, tm, tf):
    t = x2.shape[0]
    nf = D_FF // tf

    def body(dx3_ref, a_ref, w1_ref, w2_ref, x2_ref, g_ref, dz_ref, dx2_ref, dg_ref, acc_ref):
        i, f = pl.program_id(0), pl.program_id(1)
        da2 = _dot_nt(dx3_ref[...].astype(BF16), w2_ref[...])
        dz = (2.0 * a_ref[...].astype(F32) * da2).astype(BF16)
        dz_ref[...] = dz
        sw = w1_ref.shape[2]
        part = _dot_nt(dz[:, 0:sw], w1_ref[0])
        for s in range(1, w1_ref.shape[0]):
            part = part + _dot_nt(dz[:, s * sw:(s + 1) * sw], w1_ref[s])

        @pl.when(f == 0)
        def _():
            acc_ref[...] = part

        @pl.when(f > 0)
        def _():
            acc_ref[...] += part

        @pl.when((i == 0) & (f == 0))
        def _():
            dg_ref[...] = jnp.zeros_like(dg_ref)

        @pl.when(f == nf - 1)
        def _():
            xt = x2_ref[...]
            dx, dg = _rms_bwd(acc_ref[...], xt, _rms(xt), g_ref[...])
            dx2_ref[...] = dx3_ref[...] + dx
            dg_ref[...] += dg

    return pl.pallas_call(
        body, name="mlp_bwd", grid=(t // tm, nf),
        in_specs=[pl.BlockSpec((tm, D_MODEL), lambda i, f: (i, 0)),
                  pl.BlockSpec((tm, tf), lambda i, f: (i, f)),
                  pl.BlockSpec((tf // w_1.shape[2], D_MODEL, w_1.shape[2]), lambda i, f: (f, 0, 0)),
                  pl.BlockSpec((tf, D_MODEL), lambda i, f: (f, 0)),
                  pl.BlockSpec((tm, D_MODEL), lambda i, f: (i, 0)),
                  pl.BlockSpec((1, D_MODEL), lambda i, f: (0, 0))],
        out_specs=[pl.BlockSpec((tm, tf), lambda i, f: (i, f)),
                   pl.BlockSpec((tm, D_MODEL), lambda i, f: (i, 0)),
                   pl.BlockSpec((1, D_MODEL), lambda i, f: (0, 0))],
        out_shape=[jax.ShapeDtypeStruct((t, D_FF), BF16), jax.ShapeDtypeStruct((t, D_MODEL), F32),
                   jax.ShapeDtypeStruct((1, D_MODEL), F32)],
        scratch_shapes=[pltpu.VMEM((tm, D_MODEL), F32)],
        compiler_params=_params(("arbitrary", "arbitrary")),
    )(dx3, a, w_1, w_2, x2, g_mlp)


def _wgrad(name, a, b, tka, tn, tm, square=False, col_shards=False):
    t, ka = a.shape
    n = b.shape[1]
    nk = t // tm

    def body(a_ref, b_ref, o_ref, acc_ref):
        at = a_ref[...].astype(BF16)
        if square:
            at = at * at
        part = _dot_tn(at, b_ref[...].astype(BF16))
        k = pl.program_id(2)

        @pl.when(k == 0)
        def _():
            acc_ref[...] = part

        @pl.when(k > 0)
        def _():
            acc_ref[...] += part

        @pl.when(k == nk - 1)
        def _():
            if col_shards:
                for s in range(tn // sw):
                    o_ref[s] = acc_ref[:, s * sw:(s + 1) * sw].astype(BF16)
            else:
                o_ref[...] = acc_ref[...].astype(BF16)

    if col_shards:
        sw = n // N_DEV
        out_spec = pl.BlockSpec((tn // sw, tka, sw), lambda p, q, k: (q, p, 0))
        out_shape = jax.ShapeDtypeStruct((N_DEV, ka, sw), BF16)
    else:
        out_spec = pl.BlockSpec((tka, tn), lambda p, q, k: (p, q))
        out_shape = jax.ShapeDtypeStruct((ka, n), BF16)
    return pl.pallas_call(
        body, name=name, grid=(ka // tka, n // tn, nk),
        in_specs=[pl.BlockSpec((tm, tka), lambda p, q, k: (k, p)),
                  pl.BlockSpec((tm, tn), lambda p, q, k: (k, q))],
        out_specs=out_spec, out_shape=out_shape,
        scratch_shapes=[pltpu.VMEM((tka, tn), F32)],
        compiler_params=_params(("arbitrary", "arbitrary", "arbitrary")),
    )(a, b)


def _cross_bwd(dx2, x1, q, kv, w_cq, w_co, g_cross, tm, dep=None):
    t = x1.shape[0]
    m = kv.shape[0]

    def body(dx2_ref, x1_ref, q_ref, kv_ref, wq_ref, wo_ref, g_ref, dq_ref, dx1_ref, dkv_ref, dg_ref):
        @pl.when(pl.program_id(0) == 0)
        def _():
            dkv_ref[...] = jnp.zeros_like(dkv_ref)
            dg_ref[...] = jnp.zeros_like(dg_ref)

        do = _dot_nt(dx2_ref[...].astype(BF16), wo_ref[...]).astype(BF16)
        q = q_ref[...]
        dqs = []
        for h in range(X_HEADS):
            hs = slice(h * X_HEAD_DIM, (h + 1) * X_HEAD_DIM)
            vs = slice(D_MODEL + h * X_HEAD_DIM, D_MODEL + (h + 1) * X_HEAD_DIM)
            p = _cross_probs(q, kv_ref, h)
            dp = _dot_nt(do[:, hs], kv_ref[:, vs])
            ds = (p * (dp - jnp.sum(dp * p, axis=-1, keepdims=True))).astype(BF16)
            dqs.append(_dot(ds, kv_ref[:, hs]))
            dkv_ref[:, hs] += _dot_tn(ds, q[:, hs])
            dkv_ref[:, vs] += _dot_tn(p.astype(BF16), do[:, hs])
        dq = (jnp.concatenate(dqs, axis=1) * X_SCALE).astype(BF16)
        dq_ref[...] = dq
        xt = x1_ref[...]
        dx, dg = _rms_bwd(_dot_nt(dq, wq_ref[...]), xt, _rms(xt), g_ref[...])
        dx1_ref[...] = dx2_ref[...] + dx
        dg_ref[...] += dg

    row = lambda w: pl.BlockSpec((tm, w), lambda i: (i, 0))
    full = lambda a, b: pl.BlockSpec((a, b), lambda i: (0, 0))
    return pl.pallas_call(
        _with_dep(body, 7, dep), name="cross_bwd", grid=(t // tm,),
        in_specs=[row(D_MODEL), row(D_MODEL), row(D_MODEL), full(m, 2 * D_MODEL), full(D_MODEL, D_MODEL),
                  full(D_MODEL, D_MODEL), full(1, D_MODEL)] + _dep_spec(dep),
        out_specs=[row(D_MODEL), row(D_MODEL), full(m, 2 * D_MODEL), full(1, D_MODEL)],
        out_shape=[jax.ShapeDtypeStruct((t, D_MODEL), BF16), jax.ShapeDtypeStruct((t, D_MODEL), F32),
                   jax.ShapeDtypeStruct((m, 2 * D_MODEL), F32), jax.ShapeDtypeStruct((1, D_MODEL), F32)],
        compiler_params=_params(("arbitrary",)),
    )(dx2, x1, q, kv, w_cq, w_co, g_cross, *_dep_arg(dep))


def _memkv_bwd(dkv, mn, mem, w_ckv, g_mem):
    ws = w_ckv.shape[2]

    def body(dkv_ref, mn_ref, mem_ref, w_ref, g_ref, dw_ref, dg_ref):
        mn = mn_ref[...]
        dmn = jnp.zeros(mn.shape, F32)
        for j in range(N_DEV):
            dkvb = dkv_ref[:, j * ws:(j + 1) * ws].astype(BF16)
            dw_ref[j] = _dot_tn(mn, dkvb).astype(BF16)
            dmn = dmn + _dot_nt(dkvb, w_ref[j])
        xt = mem_ref[...]
        dg_ref[...] = jnp.sum(dmn * xt * _rms(xt), axis=0, keepdims=True)

    return pl.pallas_call(
        body, name="memkv_bwd",
        out_shape=[jax.ShapeDtypeStruct(w_ckv.shape, BF16), jax.ShapeDtypeStruct((1, D_MODEL), F32)],
        compiler_params=_params(),
    )(dkv, mn, mem, w_ckv, g_mem)


def _merge_bwd(dx1, ya, yb, gts, w_out, w_g, tm):
    t = dx1.shape[0]

    def body(dx1_ref, ya_ref, yb_ref, g_ref, wo_ref, wg_ref, dg_ref, dhp_ref, dya_ref, dyb_ref, db_ref):
        @pl.when(pl.program_id(0) == 0)
        def _():
            db_ref[...] = jnp.zeros_like(db_ref)

        dm = _dot_nt(dx1_ref[...].astype(BF16), wo_ref[...])
        ga = g_ref[:, :D_MODEL].astype(F32)
        gb = g_ref[:, D_MODEL:].astype(F32)
        dya_ref[...] = (dm * ga).astype(BF16)
        dyb_ref[...] = (dm * gb).astype(BF16)
        dpa = dm * ya_ref[...].astype(F32) * ga * (1.0 - ga)
        dpb = dm * yb_ref[...].astype(F32) * gb * (1.0 - gb)
        dpre = jnp.concatenate([dpa, dpb], axis=1)
        db_ref[...] += jnp.sum(dpre, axis=0, keepdims=True)
        dpreb = dpre.astype(BF16)
        dg_ref[...] = dpreb
        dhp_ref[...] = _dot_nt(dpreb, wg_ref[...])

    row = lambda w: pl.BlockSpec((tm, w), lambda i: (i, 0))
    once = lambda a, b: pl.BlockSpec((a, b), lambda i: (0, 0), pipeline_mode=pl.Buffered(1))
    sds = jax.ShapeDtypeStruct
    return pl.pallas_call(
        body, name="merge_bwd", grid=(t // tm,),
        in_specs=[row(D_MODEL), row(D_MODEL), row(D_MODEL), row(GATE_WIDTH),
                  once(D_MODEL, D_MODEL), once(D_MODEL, GATE_WIDTH)],
        out_specs=[row(GATE_WIDTH), row(D_MODEL), row(D_MODEL), row(D_MODEL),
                   pl.BlockSpec((1, GATE_WIDTH), lambda i: (0, 0))],
        out_shape=[sds((t, GATE_WIDTH), BF16), sds((t, D_MODEL), F32), sds((t, D_MODEL), BF16),
                   sds((t, D_MODEL), BF16), sds((1, GATE_WIDTH), F32)],
        compiler_params=_params(("arbitrary",)),
    )(dx1, ya, yb, gts, w_out, w_g)


def _combine_bwd(dya, dyb, oa, ob, l0, l1, l2, lb, sink_row, w_a, w_b, tm):
    t = dya.shape[0]

    def body(dya_ref, dyb_ref, oa_ref, ob_ref, l0_ref, l1_ref, l2_ref, lb_ref, sk_ref, wa_ref, wb_ref,
             do0_ref, do1_ref, do2_ref, c0_ref, c1_ref, c2_ref, dob_ref, cb_ref, dsk_ref, so_ref, sl_ref):
        @pl.when(pl.program_id(0) == 0)
        def _():
            dsk_ref[...] = jnp.zeros_like(dsk_ref)

        doa = _dot_nt(dya_ref[...], wa_ref[...])
        dob = _dot_nt(dyb_ref[...], wb_ref[...])
        dsum = _head_sums(doa * oa_ref[...].astype(F32), _head_gather(256, True))
        a0, a1, a2 = _alphas(l0_ref[...], _interleave(l1_ref, sl_ref), _interleave(l2_ref, sl_ref))
        c0_ref[...] = a0 * dsum
        spread = _dil_head_spread()
        do0_ref[...] = _head_scale(doa, a0, spread).astype(BF16)
        for al, do_ref, c_ref in ((a1, do1_ref, c1_ref), (a2, do2_ref, c2_ref)):
            _deinterleave(al * dsum, sl_ref, c_ref, F32)
            _deinterleave(_head_scale(doa, al, spread), so_ref, do_ref, BF16)
        dob_ref[...] = dob.astype(BF16)
        cb = _head_sums(dob * ob_ref[...], _head_gather(128, False))
        cb_ref[...] = cb
        lane = lax.broadcasted_iota(jnp.int32, cb.shape, 1)
        psink = jnp.where(lane < 8, jnp.exp(sk_ref[...] - lb_ref[...]), 0.0)
        dsk_ref[...] += jnp.sum(-psink * cb, axis=0, keepdims=True)

    row = lambda w: pl.BlockSpec((tm, w), lambda i: (i, 0))
    full = lambda a, b: pl.BlockSpec((a, b), lambda i: (0, 0))
    sds = jax.ShapeDtypeStruct
    d1, d2 = l1.shape[0], l2.shape[0]
    res = lambda d, w: pl.BlockSpec((d, tm // d, w), lambda i: (0, i, 0))
    return pl.pallas_call(
        body, name="combine_bwd", grid=(t // tm,),
        in_specs=[row(D_MODEL), row(D_MODEL), row(512), row(512),
                  row(256), _res_spec(l1, tm), _res_spec(l2, tm), row(128), full(1, 128),
                  full(512, D_MODEL), full(512, D_MODEL)],
        out_specs=[row(512), res(d1, 512), res(d2, 512), row(256), res(d1, 256), res(d2, 256),
                   row(512), row(128), full(1, 128)],
        out_shape=[sds((t, 512), BF16), sds((d1, t // d1, 512), BF16),
                   sds((d2, t // d2, 512), BF16), sds((t, 256), F32), sds((d1, t // d1, 256), F32),
                   sds((d2, t // d2, 256), F32), sds((t, 512), BF16),
                   sds((t, 128), F32), sds((1, 128), F32)],
        scratch_shapes=[_lane_scratch(tm, 512), _lane_scratch(tm, 256)],
        compiler_params=_params(("arbitrary",)),
    )(dya, dyb, oa, ob, l0, l1, l2, lb, sink_row, w_a, w_b)


def _attn_bwd(name, pv, dov, lsev, cv, cosv, sinv, swa, tq, dep=None):
    d, ls = pv.shape[0], pv.shape[1]
    n, nsb = ls // tq, tq // BAND
    pairs = _attn_layout(swa)
    ncol = 1 if swa else 2
    ow = 128 * len(pairs)

    kv_slots = sorted({(ko, vo) for _, ko, vo in pairs})

    def body(cur_ref, tail_ref, do_ref, lse_ref, c_ref, cos_ref, sin_ref, out_ref, acc_ref, carry_ref, acct_ref):
        i = pl.program_id(2)
        blk_i = n - 1 - i
        acc_ref[...] = jnp.zeros_like(acc_ref)
        acct_ref[...] = jnp.zeros_like(acct_ref)

        @pl.when(i == 0)
        def _():
            carry_ref[...] = jnp.zeros_like(carry_ref)

        qk_a, v_a = _head_a_masks(BAND)
        dim = lax.broadcasted_iota(jnp.int32, (128, BAND), 0)
        qk_at, v_at = (dim % HEAD_DIM) < HEAD_DIM // 2, dim < HEAD_DIM
        for s in range(nsb):
            mask = _band_mask(blk_i, s)
            mask2 = jnp.concatenate([mask, mask], axis=0)
            rows = slice(s * BAND, (s + 1) * BAND)
            kcols = slice(s * BAND, (s + 2) * BAND)
            for j, (qo, ko, vo) in enumerate(pairs):
                slot = kv_slots.index((ko, vo))
                kk = _kv_rows(cur_ref, tail_ref, s, ko)
                vv = _kv_rows(cur_ref, tail_ref, s, vo)
                q, do = cur_ref[rows, qo:qo + 128], do_ref[rows, j * 128:(j + 1) * 128]
                q2, do2 = _stack_heads(q, qk_a), _stack_heads(do, v_a)
                col2 = lambda ref: jnp.concatenate([ref[rows, 2 * j:2 * j + 1], ref[rows, 2 * j + 1:2 * j + 2]], axis=0)
                sc = _dot_nt(q2, kk)
                p = jnp.exp(jnp.where(mask2, sc, -jnp.inf) - col2(lse_ref))
                dp = _dot_nt(do2, vv)
                ds = (p * (dp - col2(c_ref))).astype(BF16)
                dq2 = _dot(ds, kk)
                acc_ref[BAND + s * BAND:BAND + (s + 1) * BAND, qo:qo + 128] += jnp.where(qk_a, dq2[:BAND], dq2[BAND:])
                acct_ref[2 * slot, :, kcols] += _dot(_stack_heads_t(q.T, qk_at), ds)
                acct_ref[2 * slot + 1, :, kcols] += _dot(_stack_heads_t(do.T, v_at), p.astype(BF16))
        for slot, (ko, vo) in enumerate(kv_slots):
            acc_ref[:, ko:ko + 128] += acct_ref[2 * slot].T
            acc_ref[:, vo:vo + 128] += acct_ref[2 * slot + 1].T

        last = acc_ref[tq:, :] + carry_ref[...]
        fin = last if tq == BAND else jnp.concatenate([acc_ref[BAND:tq, :], last], axis=0)
        out_ref[...] = _rope(fin, cos_ref[...], sin_ref[...], swa, -1).astype(BF16)
        carry_ref[...] = acc_ref[0:BAND, :]

    rev = lambda i: n - 1 - i
    blk = lambda rows, w, row_of: pl.BlockSpec((None, rows, w), lambda r, cb, i: (r, row_of(i), cb))
    tab = pl.BlockSpec((None, tq, 128), lambda r, cb, i: (r, rev(i), 0))
    return pl.pallas_call(
        _with_dep(body, 7, dep), name=name, grid=(d, ncol, n),
        in_specs=[blk(tq, PBLK, rev), blk(BAND, PBLK, lambda i: jnp.maximum(rev(i) * nsb - 1, 0)),
                  blk(tq, ow, rev), blk(tq, 128, rev), blk(tq, 128, rev), tab, tab] + _dep_spec(dep),
        out_specs=blk(tq, PBLK, rev),
        out_shape=jax.ShapeDtypeStruct((d, ls, ncol * PBLK), BF16),
        scratch_shapes=[pltpu.VMEM((tq + BAND, PBLK), F32), pltpu.VMEM((BAND, PBLK), F32),
                        pltpu.VMEM((2 * len(kv_slots), 128, tq + BAND), F32)],
        compiler_params=_params(("arbitrary", "arbitrary", "arbitrary")),
    )(pv, pv, dov, lsev, cv, cosv, sinv, *_dep_arg(dep))


def _dx(dp0, dp1, dp2, dpb, w_p, dh_part, dx1, x, g_mix, tm, dep=None):
    t = x.shape[0]
    gw = 2 * PBLK

    def body(dp0_ref, dp1_ref, dp2_ref, dpb_ref, w_ref, dhp_ref, dx1_ref, x_ref, g_ref, gx_ref, dg_ref,
             dpt_ref, scr_ref):
        @pl.when(pl.program_id(0) == 0)
        def _():
            dg_ref[...] = jnp.zeros_like(dg_ref)

        dpt_ref[:, 0:gw] = dp0_ref[...]
        dpt_ref[:, gw:2 * gw] = _interleave(dp1_ref, scr_ref).astype(BF16)
        dpt_ref[:, 2 * gw:3 * gw] = _interleave(dp2_ref, scr_ref).astype(BF16)
        dpt_ref[:, 3 * gw:] = dpb_ref[...]
        dh = _dot_nt(dpt_ref[...], w_ref[...]) + dhp_ref[...]
        xt = x_ref[...]
        dx, dg = _rms_bwd(dh, xt, _rms(xt), g_ref[...])
        gx_ref[...] = dx1_ref[...] + dx
        dg_ref[...] += dg

    row = lambda w: pl.BlockSpec((tm, w), lambda i: (i, 0))
    full = lambda a, b: pl.BlockSpec((a, b), lambda i: (0, 0))
    return pl.pallas_call(
        _with_dep(body, 9, dep), name="dx", grid=(t // tm,),
        in_specs=[row(gw), _res_spec(dp1, tm), _res_spec(dp2, tm), row(PBLK),
                  pl.BlockSpec((D_MODEL, P_WIDTH), lambda i: (0, 0), pipeline_mode=pl.Buffered(1)),
                  row(D_MODEL), row(D_MODEL), row(D_MODEL), full(1, D_MODEL)] + _dep_spec(dep),
        out_specs=[row(D_MODEL), full(1, D_MODEL)],
        out_shape=[jax.ShapeDtypeStruct((t, D_MODEL), F32), jax.ShapeDtypeStruct((1, D_MODEL), F32)],
        scratch_shapes=[pltpu.VMEM((tm, P_WIDTH), BF16), _lane_scratch(tm, gw)],
        compiler_params=_params(("arbitrary",)),
    )(dp0, dp1, dp2, dpb, w_p, dh_part, dx1, x, g_mix, *_dep_arg(dep))


MESH = pl.DeviceIdType.MESH
HBM_SPEC = pl.BlockSpec(memory_space=pltpu.HBM)
VMEM_SPEC = pl.BlockSpec(memory_space=pltpu.VMEM)


def _all_gather(xp):
    def body(x_ref, out_ref, send_sems, recv_sems, local_sem):
        x, y, c = lax.axis_index("x"), lax.axis_index("y"), lax.axis_index("c")
        me, sibling = (x, y, c), (x, y, 1 - c)
        chips = [(1 - x, y), (x, 1 - y), (1 - x, 1 - y)]

        def rows(px, py, pc):
            return out_ref.at[4 * px + 2 * py + pc]

        def copy(k, block, to, src=None):
            return pltpu.make_async_remote_copy(
                src_ref=rows(*block) if src is None else src, dst_ref=rows(*block),
                send_sem=send_sems.at[k], recv_sem=recv_sems.at[k], device_id=to, device_id_type=MESH)

        mine = pltpu.make_async_copy(x_ref, rows(*me), local_sem)
        mine.start()
        first = [copy(0, me, sibling, src=x_ref)]
        first += [copy(1 + j, me, (*chip, c), src=x_ref) for j, chip in enumerate(chips)]
        for cp in first:
            cp.start()
        passed = [copy(4 + j, (*chip, c), sibling) for j, chip in enumerate(chips)]
        for j, chip in enumerate(chips):
            copy(1 + j, (*chip, c), me).wait_recv()
            passed[j].start()
        copy(0, sibling, me).wait_recv()
        for j, chip in enumerate(chips):
            copy(4 + j, (*chip, 1 - c), me).wait_recv()
        for cp in first + passed:
            cp.wait_send()
        mine.wait()

    return pl.pallas_call(
        body, name="all_gather",
        out_shape=jax.ShapeDtypeStruct((N_DEV,) + xp.shape, xp.dtype),
        in_specs=[HBM_SPEC], out_specs=HBM_SPEC,
        scratch_shapes=[pltpu.SemaphoreType.DMA((7,)), pltpu.SemaphoreType.DMA((7,)), pltpu.SemaphoreType.DMA],
    )(xp)


def _peers():
    x, y, c = lax.axis_index("x"), lax.axis_index("y"), lax.axis_index("c")
    out = []
    for k in range(1, N_DEV):
        px = 1 - x if k & 4 else x
        py = 1 - y if k & 2 else y
        pc = 1 - c if k & 1 else c
        out.append((k, (px, py, pc), 4 * px + 2 * py + pc))
    return out


def _my_index():
    return 4 * lax.axis_index("x") + 2 * lax.axis_index("y") + lax.axis_index("c")


SEM_SPEC = pl.BlockSpec(memory_space=pltpu.SEMAPHORE)
ANY_SPEC = pl.BlockSpec(memory_space=pl.ANY)
_SPLIT_PARAMS = pltpu.CompilerParams(has_side_effects=pltpu.SideEffectType.DATAFLOW_SIDE_EFFECTING)


def _split_copies(gather, src_refs, land_refs, send_sems, recv_sems):
    me_idx = _my_index()
    out = []
    for a, (src_ref, land_ref) in enumerate(zip(src_refs, land_refs)):
        for k, peer, peer_idx in _peers():
            if gather:
                src, dst = src_ref, land_ref.at[me_idx]
            else:
                src, dst = src_ref.at[peer_idx], land_ref.at[k - 1]
            out.append(pltpu.make_async_remote_copy(
                src_ref=src, dst_ref=dst, send_sem=send_sems.at[7 * a + k - 1], recv_sem=recv_sems.at[7 * a + k - 1],
                device_id=peer, device_id_type=MESH))
    return out


def _split_start(name, gather, srcs):
    n = len(srcs)

    def body(*refs):
        send_sems, recv_sems = refs[n], refs[n + 1]
        for cp in _split_copies(gather, refs[:n], refs[2 * n + 2:3 * n + 2], send_sems, recv_sems):
            cp.start()
        token = refs[-1]
        token[...] = jnp.zeros_like(token)

    lands = [pltpu.HBM((N_DEV,) + a.shape if gather else (N_DEV - 1,) + a.shape[1:], a.dtype) for a in srcs]
    return pl.pallas_call(
        body, name=name,
        out_shape=(pltpu.SemaphoreType.DMA((7 * n,)), pltpu.SemaphoreType.DMA((7 * n,)),
                   *[pltpu.HBM(a.shape, a.dtype) for a in srcs], *lands, jax.ShapeDtypeStruct((8, 128), F32)),
        in_specs=(HBM_SPEC,) * n, out_specs=(SEM_SPEC, SEM_SPEC) + (HBM_SPEC,) * (2 * n) + (VMEM_SPEC,),
        input_output_aliases={i: 2 + i for i in range(n)}, compiler_params=_SPLIT_PARAMS,
    )(*[pltpu.with_memory_space_constraint(a, pltpu.HBM) for a in srcs])


def _split_wait(name, gather, started, after):
    send_sems, recv_sems, bufs = started[0], started[1], started[2:-1]
    n = len(bufs) // 2

    def body(*refs):
        for cp in _split_copies(gather, refs[:n], refs[n:2 * n], refs[2 * n], refs[2 * n + 1]):
            cp.wait_send()
            cp.wait_recv()

    out = pl.pallas_call(
        body, name=name, out_shape=tuple(pltpu.HBM(a.shape, a.dtype) for a in bufs),
        in_specs=(HBM_SPEC,) * (2 * n) + (SEM_SPEC, SEM_SPEC, ANY_SPEC), out_specs=(HBM_SPEC,) * (2 * n),
        input_output_aliases={i: i for i in range(2 * n)}, compiler_params=_SPLIT_PARAMS,
    )(*bufs, send_sems, recv_sems, after)
    return out[:n], out[n:]


def _adam_update(g, w, m, v):
    nm = ADAM_B1 * m + (1.0 - ADAM_B1) * g
    nv = ADAM_B2 * v + (1.0 - ADAM_B2) * (g * g)
    m_hat = nm / (1.0 - ADAM_B1 ** ADAM_STEP)
    v_hat = nv / (1.0 - ADAM_B2 ** ADAM_STEP)
    return -ADAM_LR * (m_hat / (jnp.sqrt(v_hat) + ADAM_EPS) + ADAM_WD * w), nm, nv


def _adamw(name, me, sent, got, w, m, v, tr):
    r, c = w.shape

    def body(me_ref, own_ref, got_ref, w_ref, m_ref, v_ref, g_ref, d_ref, nm_ref, nv_ref):
        g = own_ref[...].astype(F32)
        for k in range(N_DEV - 1):
            g = g + got_ref[k].astype(F32)
        g_ref[...] = g
        d_ref[...], nm_ref[...], nv_ref[...] = _adam_update(g, w_ref[...], m_ref[...], v_ref[...])

    blk = pl.BlockSpec((tr, c), lambda i, me_ref: (i, 0))
    return pl.pallas_call(
        body, name=name,
        grid_spec=pltpu.PrefetchScalarGridSpec(
            num_scalar_prefetch=1, grid=(r // tr,),
            in_specs=[pl.BlockSpec((None, tr, c), lambda i, me_ref: (me_ref[0], i, 0)),
                      pl.BlockSpec((N_DEV - 1, tr, c), lambda i, me_ref: (0, i, 0)), blk, blk, blk],
            out_specs=[blk] * 4),
        out_shape=[jax.ShapeDtypeStruct((r, c), F32)] * 4,
        compiler_params=_params(("arbitrary",)),
    )(me, sent, got, w, m, v)


def _adamw_small(srecv, ws, ms, vs):
    nv_ = len(ws)

    def body(*refs):
        s_ref = refs[0]
        ins, outs = refs[1:1 + 3 * nv_], refs[1 + 3 * nv_:]
        g_all = s_ref[0]
        for k in range(1, N_DEV):
            g_all = g_all + s_ref[k]
        for i in range(nv_):
            n = ins[i].shape[1]
            g = g_all[i:i + 1, :n]
            d, nm, nv = _adam_update(g, ins[i][...], ins[nv_ + i][...], ins[2 * nv_ + i][...])
            outs[i][...], outs[nv_ + i][...], outs[2 * nv_ + i][...], outs[3 * nv_ + i][...] = g, d, nm, nv
        outs[-1][...] = g_all[nv_:nv_ + 1, :128]

    shapes = [jax.ShapeDtypeStruct(a.shape, F32) for a in ws]
    res = pl.pallas_call(body, name="adamw_small", out_shape=shapes * 4 + [jax.ShapeDtypeStruct((1, 128), F32)],
                         compiler_params=_params())(srecv, *ws, *ms, *vs)
    return [res[k * nv_:(k + 1) * nv_] for k in range(4)], res[-1]


def _cols_from_shards(a):
    return jnp.swapaxes(a, 0, 1).reshape(a.shape[1], N_DEV * a.shape[2])


def _shards_from_cols(a):
    return jnp.swapaxes(a.reshape(a.shape[0], N_DEV, a.shape[1] // N_DEV), 0, 1)


def _shards_from_rows(a):
    return a.reshape(N_DEV, a.shape[0] // N_DEV, a.shape[1])


def _pair_lanes(a):
    lead = a.shape[:-1]
    return a.reshape(lead + (2, 2, HEAD_DIM // 2)).swapaxes(-3, -2).reshape(lead + (128,))


def _split_w_in(w_in):
    rows = w_in.shape[0]
    dil = w_in[:, :3 * DIL_WIDTH].reshape(rows, 3, 3, 4, 128)
    dil = jnp.concatenate([_pair_lanes(dil[:, :2]), dil[:, 2:]], axis=1)
    dil = dil.transpose(0, 2, 3, 1, 4).reshape(rows, 3 * DIL_WIDTH)
    o = 3 * DIL_WIDTH
    qb = w_in[:, o:o + SWA_Q_WIDTH].reshape(rows, 2, 4, HEAD_DIM).transpose(0, 2, 1, 3).reshape(rows, 4, 128)
    qb = _pair_lanes(qb).reshape(rows, SWA_Q_WIDTH)
    kb = _pair_lanes(w_in[:, o + SWA_Q_WIDTH:o + SWA_Q_WIDTH + SWA_KV_WIDTH])
    vb = w_in[:, o + SWA_Q_WIDTH + SWA_KV_WIDTH:P_WIDTH]
    return jnp.concatenate([dil, qb, kb, vb], axis=1), w_in[:, P_WIDTH:]


def _merge_w_in(dw_p, dw_g):
    rows = dw_p.shape[0]
    dil = dw_p[:, :3 * DIL_WIDTH].reshape(rows, 3, 4, 3, 128).transpose(0, 3, 1, 2, 4)
    dil = jnp.concatenate([_pair_lanes(dil[:, :2]), dil[:, 2:]], axis=1).reshape(rows, 3 * DIL_WIDTH)
    o = 3 * DIL_WIDTH
    qb = _pair_lanes(dw_p[:, o:o + SWA_Q_WIDTH].reshape(rows, 4, 128))
    qb = qb.reshape(rows, 4, 2, HEAD_DIM).transpose(0, 2, 1, 3).reshape(rows, SWA_Q_WIDTH)
    kb = _pair_lanes(dw_p[:, o + SWA_Q_WIDTH:o + SWA_Q_WIDTH + SWA_KV_WIDTH])
    vb = dw_p[:, o + SWA_Q_WIDTH + SWA_KV_WIDTH:]
    return jnp.concatenate([dil, qb, kb, vb, dw_g], axis=1)


def _swa_rows(w_b):
    return w_b.reshape(2, 4, HEAD_DIM, -1).transpose(1, 0, 2, 3).reshape(SWA_Q_WIDTH, -1)


def _swa_rows_inv(dw_b):
    return dw_b.reshape(4, 2, HEAD_DIM, -1).transpose(1, 0, 2, 3).reshape(SWA_Q_WIDTH, -1)


def _rope_tables(pos):
    half = HEAD_DIM // 2
    inv = ROPE_THETA ** (-jnp.arange(half, dtype=F32) / half)
    ang = pos.astype(F32)[:, None] * jnp.tile(inv, 4)
    sign = jnp.repeat(jnp.array([-1.0, 1.0], F32), 2 * half)
    return jnp.cos(ang), jnp.sin(ang) * sign


def _local_step(x, mem, pos, target, w_in, dep, rest_weights, on_grads, g_mix, g_cross, g_mem, g_mlp, g_final, sink):
    t = x.shape[0]
    tm = min(512, t)
    tq = 1024
    tw = min(2048, t)
    w_p, w_g = _split_w_in(w_in)
    cos, sin = lax.optimization_barrier(_rope_tables(pos))
    sink_row = jnp.pad(sink.reshape(2, 4).T.reshape(1, 8), ((0, 0), (0, 120)))
    tabs = [(cos[None], sin[None])]
    for _, d in DIL_GROUPS[1:]:
        tabs.append(tuple(a.reshape(t // d, d, 128).swapaxes(0, 1) for a in (cos, sin)))
    tabs.append(tabs[0])

    h, h1, h2, p0, p1, p2, pb = _inproj(x, g_mix, w_p, [(cos, sin), tabs[1], tabs[2]], tm, dep)
    ps = [p0[None], p1, p2, pb[None]]
    outs, lses = [], []
    for gi, pv in enumerate(ps):
        res = _attn_fwd(f"attn_fwd{gi}", pv, gi == 3, sink_row[0, :8], min(tq, pv.shape[1]))
        outs.append(res[0])
        lses.append(res[1])
    o0, l0, ob, lb, ob32 = outs[0][0], lses[0][0], outs[3][0], lses[3][0], res[2][0]
    wts = rest_weights(lb)
    w_b = _swa_rows(wts["w_branch_b"])
    tf = 2048
    gts = _gates(h, w_g, wts["b_gate"].reshape(1, GATE_WIDTH), tm, 1024)
    oa, ya, yb, merged, x1, hc = _mix(o0, outs[1], outs[2], l0, lses[1], lses[2], ob, gts, x,
                                      wts["w_branch_a"], w_b, wts["w_out"], g_cross, tm)
    mn, kv = _memkv(mem, g_mem, wts["w_ckv"])
    q, o, x2, hm = _cross(hc, x1, kv, wts["w_cq"], wts["w_co"], g_mlp, tm)
    a, dx3, loss, dg_final = _mlp(hm, x2, wts["w_1"], wts["w_2"], g_final.reshape(1, D_MODEL), target, tm, tf)

    grads = {}
    dz, dx2, dg_mlp = _mlp_bwd(dx3, a, wts["w_1"], wts["w_2"], x2, g_mlp, tm, tf)
    grads["w_2"] = _shards_from_rows(_wgrad("dw_2", a, dx3, 1024, 1024, tw, square=True))
    grads["w_1"] = _wgrad("dw_1", hm, dz, 1024, 1024, tw, col_shards=True)
    dep = on_grads(GROUP_A, grads)
    dq, dx1, dkv, dg_cross = _cross_bwd(dx2, x1, q, kv, wts["w_cq"], wts["w_co"], g_cross, tm, dep)
    grads["w_co"] = _shards_from_rows(_wgrad("dw_co", o, dx2, 1024, 1024, tw))
    grads["w_cq"] = _shards_from_rows(_wgrad("dw_cq", hc, dq, 1024, 1024, tw))
    grads["w_ckv"], dg_mem = _memkv_bwd(dkv, mn, mem, wts["w_ckv"], g_mem)
    dgt, dh_part, dya, dyb, db_gate = _merge_bwd(dx1, ya, yb, gts, wts["w_out"], w_g, tm)
    do0, do1, do2, c0, c1, c2, dob, cb, dsink = _combine_bwd(
        dya, dyb, oa, ob32, l0, lses[1], lses[2], lb, sink_row, wts["w_branch_a"], w_b, tm)
    grads["w_out"] = _shards_from_rows(_wgrad("dw_out", merged, dx1, 1024, 1024, tw))
    grads["w_branch_a"] = _shards_from_cols(_wgrad("dw_a", oa, dya, 512, 1024, tw))
    grads["w_branch_b"] = _shards_from_cols(_swa_rows_inv(_wgrad("dw_b", ob, dyb, 512, 1024, tw)))
    grads["b_gate"] = _shards_from_cols(db_gate.reshape(2, D_MODEL)).astype(BF16)
    dep = on_grads(GROUP_B, grads)
    dw_g = _wgrad("dw_g", h, dgt, 1024, 1024, tw)
    dps = []
    for gi, (pv, do_g, c_g) in enumerate(zip(ps, (do0[None], do1, do2, dob[None]), (c0[None], c1, c2, cb[None]))):
        dps.append(_attn_bwd(f"attn_bwd{gi}", pv, do_g, lses[gi], c_g, tabs[gi][0], tabs[gi][1], gi == 3,
                             min(tq, pv.shape[1]),
                             dep if gi == 0 else None))
    dw_p = jnp.concatenate(
        [_wgrad(f"dw_p{gi}", hh.reshape(t, D_MODEL), dpg.reshape(t, -1), 1024, PBLK, tw)
         for gi, (hh, dpg) in enumerate(zip((h, h1, h2, h), dps))], axis=1)
    grads["w_in"] = _shards_from_cols(_merge_w_in(dw_p, dw_g))
    dep = on_grads(GROUP_C, grads)
    grad_x, dg_mix = _dx(dps[0][0], dps[1], dps[2], dps[3][0], w_p, dh_part, dx1, x, g_mix, tm, dep)
    dsink_heads = dsink[0, :8].reshape(4, 2).T.reshape(8)
    small = {"g_mix": dg_mix[0], "g_cross": dg_cross[0], "g_mem": dg_mem[0], "g_mlp": dg_mlp[0],
             "g_final": dg_final[0], "sink": dsink_heads}
    return loss[0, 0], grad_x, small


def kernel(x, mem, positions, g_mix, w_in, b_gate, sink, w_branch_a, w_branch_b, w_out, g_cross, g_mem, w_cq, w_ckv, w_co, g_mlp, w_1, w_2, g_final, loss_target, m_g_mix, m_w_in, m_b_gate, m_sink, m_w_branch_a, m_w_branch_b, m_w_out, m_g_cross, m_g_mem, m_w_cq, m_w_ckv, m_w_co, m_g_mlp, m_w_1, m_w_2, m_g_final, v_g_mix, v_w_in, v_b_gate, v_sink, v_w_branch_a, v_w_branch_b, v_w_out, v_g_cross, v_g_mem, v_w_cq, v_w_ckv, v_w_co, v_g_mlp, v_w_1, v_w_2, v_g_final):
    local = dict(locals())
    shard = {n: local[n][0] for n in GROUP_A + GROUP_B + GROUP_C}
    me = _my_index()
    me_arr = me.reshape(1).astype(jnp.int32)
    tags = {GROUP_A: "a", GROUP_B: "b", GROUP_C: "c"}

    w_in_full = _cols_from_shards(_all_gather(shard["w_in"].astype(BF16)))
    rest = GROUP_A + GROUP_B

    def gathered(name, started, after):
        srcs, lands = _split_wait(name, True, started, after)
        return [lax.dynamic_update_slice(land, src[None], (me,) + (0,) * src.ndim) for src, land in zip(srcs, lands)]

    gather = _split_start("gather_start", True,
                          [shard[n] if n == "b_gate" else shard[n].astype(BF16) for n in rest])

    def rest_weights(after):
        full = {}
        for name, a in zip(rest, gathered("gather_wait", gather, after)):
            if name in ("w_1", "w_ckv"):
                full[name] = a
            elif name in _COL_SHARDED:
                full[name] = _cols_from_shards(a)
            else:
                full[name] = a.reshape(N_DEV * a.shape[1], a.shape[2])
        return full

    scatters = {}

    def on_grads(names, grads):
        scatters[names] = _split_start("scatter_start_" + tags[names], False, [grads[n] for n in names])
        return scatters[names][-1]

    loss, grad_x, small = _local_step(
        x[0], mem[0], positions[0], loss_target[0], w_in_full, gather[-1], rest_weights, on_grads,
        g_mix, g_cross, g_mem, g_mlp, g_final, sink[0])

    sp = jnp.stack([small[n] if n != "sink" else jnp.pad(small[n], (0, LANES - 8)) for n in SMALL]
                   + [jnp.pad(loss.reshape(1), (0, LANES - 1)), jnp.zeros((LANES,), F32)])
    small_gather = _split_start("small_start", True, [sp])

    after, updated = small_gather[-1], {}
    for names in (GROUP_A, GROUP_B, GROUP_C):
        sent, got = _split_wait("scatter_wait_" + tags[names], False, scatters[names], after)
        for i, name in enumerate(names):
            outs = _adamw("adamw_" + name, me_arr, sent[i], got[i], shard[name],
                          local["m_" + name][0], local["v_" + name][0], ADAM_ROWS[name])
            updated[name] = [a[None] for a in outs]
            after = outs[3]

    flat = lambda prefix: [local[prefix + n].reshape(1, -1) for n in SMALL]
    outs, loss_row = _adamw_small(gathered("small_wait", small_gather, after)[0], flat(""), flat("m_"), flat("v_"))
    for i, name in enumerate(SMALL):
        updated[name] = [outs[which][i].reshape(local[name].shape) for which in range(4)]

    order = ["g_mix", "w_in", "b_gate", "sink", "w_branch_a", "w_branch_b", "w_out", "g_cross", "g_mem", "w_cq",
             "w_ckv", "w_co", "g_mlp", "w_1", "w_2", "g_final"]
    res = [loss_row[0, 0], grad_x[None]]
    for which in range(4):
        res += [updated[n][which] for n in order]
    return tuple(res)
```

```python
import functools
import math

import jax
import jax.numpy as jnp
from jax import lax
from jax.experimental import pallas as pl
from jax.experimental.pallas import tpu as pltpu

F32 = jnp.float32
BF16 = jnp.bfloat16

D_MODEL = 1024
HEAD_DIM = 64
DIL_GROUPS = ((128, 1), (512, 4), (2048, 16))
ROPE_THETA = 10000.0
X_HEADS = 4
X_HEAD_DIM = D_MODEL // X_HEADS
D_FF = 4 * D_MODEL
EPS = 1e-6
DIL_WIDTH = 1536
SWA_Q_WIDTH = 512
SWA_KV_WIDTH = 128
P_WIDTH = 3 * DIL_WIDTH + SWA_Q_WIDTH + 2 * SWA_KV_WIDTH
GATE_WIDTH = 2 * D_MODEL
IN_WIDTH = P_WIDTH + GATE_WIDTH
BAND = 128
PBLK = 768
Q_SCALE = HEAD_DIM ** -0.5
X_SCALE = X_HEAD_DIM ** -0.5

ADAM_LR = 0.001
ADAM_B1 = 0.9
ADAM_B2 = 0.999
ADAM_EPS = 1e-08
ADAM_WD = 0.01
ADAM_STEP = 10

N_DEV = 8
LANES = 1024
VMEM_LIMIT = 52 * 1024 * 1024

NT = (((1,), (1,)), ((), ()))
TN = (((0,), (0,)), ((), ()))

GROUP_A = ("w_1", "w_2")
GROUP_B = ("w_branch_a", "w_branch_b", "w_out", "w_cq", "w_ckv", "w_co", "b_gate")
GROUP_C = ("w_in",)
_COL_SHARDED = ("w_in", "w_branch_a", "w_branch_b", "w_ckv", "w_1", "b_gate")
QKV_SRC = 2 * -(-P_WIDTH // (2 * IN_WIDTH // N_DEV))
ADAM_ROWS = {"w_in": 256, "w_branch_a": 512, "w_branch_b": 512, "w_out": 128, "w_cq": 128, "w_ckv": 512,
             "w_co": 128, "w_1": 256, "w_2": 256, "b_gate": 2}
SMALL = ("g_mix", "g_cross", "g_mem", "g_mlp", "g_final", "sink")


def _params(sem=None):
    return pltpu.CompilerParams(dimension_semantics=sem, vmem_limit_bytes=VMEM_LIMIT)


def _dot(a, b):
    return jnp.dot(a, b, preferred_element_type=F32)


def _dot_nt(a, b):
    return lax.dot_general(a, b, NT, preferred_element_type=F32)


def _dot_tn(a, b):
    return lax.dot_general(a, b, TN, preferred_element_type=F32)


def _rms(xt):
    return lax.rsqrt(jnp.mean(xt * xt, axis=-1, keepdims=True) + EPS)


def _rms_bwd(dh, xt, r, g):
    xn = xt * r
    dxn = dh * g
    dx = r * (dxn - xn * jnp.mean(dxn * xn, axis=-1, keepdims=True))
    return dx, jnp.sum(dh * xn, axis=0, keepdims=True)


def _rope(x, c, s, swa, sign):
    kinds = "qqqqkv" if swa else "qkvqkv"
    cq, sq = c * Q_SCALE, s * (sign * Q_SCALE)
    sk = s * sign if sign != 1 else s
    out = []
    for ci, kind in enumerate(kinds):
        xc = x[:, ci * 128:(ci + 1) * 128]
        if kind == "v":
            out.append(xc)
        elif kind == "q":
            out.append(xc * cq + pltpu.roll(xc, 64, 1) * sq)
        else:
            out.append(xc * c + pltpu.roll(xc, 64, 1) * sk)
    return jnp.concatenate(out, axis=1)


def _lane_scratch(rows, w):
    return pltpu.VMEM((w // 128, rows, 128), F32)


def _deinterleave(val, scr_ref, dst_ref, dtype):
    d, n = dst_ref.shape[0], dst_ref.shape[1]
    nc = val.shape[1] // 128
    for c in range(nc):
        scr_ref[c] = val[:, c * 128:(c + 1) * 128]
    for r in range(d):
        rows = [scr_ref.at[c][pl.ds(r, n, stride=d), :] for c in range(nc)]
        dst_ref[r] = jnp.concatenate(rows, axis=1).astype(dtype)


def _res_spec(a, tm):
    d, w = a.shape[0], a.shape[2]
    return pl.BlockSpec((d, tm // d, w), lambda i: (0, i, 0))


def _interleave(src_ref, scr_ref):
    d, n = src_ref.shape[0], src_ref.shape[1]
    nc = src_ref.shape[2] // 128
    for r in range(d):
        v = src_ref[r].astype(F32)
        for c in range(nc):
            scr_ref.at[c][pl.ds(r, n, stride=d), :] = v[:, c * 128:(c + 1) * 128]
    return jnp.concatenate([scr_ref[c] for c in range(nc)], axis=1)


def _with_dep(body, n_in, dep):
    if dep is None:
        return body
    return lambda *refs: body(*refs[:n_in], *refs[n_in + 1:])


def _dep_spec(dep):
    return [] if dep is None else [pl.BlockSpec(memory_space=pl.ANY)]


def _dep_arg(dep):
    return [] if dep is None else [dep]


def _inproj(x, g, w_p, tabs, tm, dep=None):
    t = x.shape[0]
    gw = 2 * PBLK
    (cos, sin), (cos1, sin1), (cos2, sin2) = tabs[0], tabs[1], tabs[2]

    def body(x_ref, g_ref, w_ref, c_ref, s_ref, c1_ref, s1_ref, c2_ref, s2_ref,
             h_ref, h1_ref, h2_ref, p0_ref, p1_ref, p2_ref, pb_ref, hf_ref):
        xt = x_ref[...]
        hf = xt * _rms(xt) * g_ref[...]
        h_ref[...] = hf.astype(BF16)
        _deinterleave(hf, hf_ref, h1_ref, BF16)
        _deinterleave(hf, hf_ref, h2_ref, BF16)
        rows = lambda ref: ref[...].reshape(tm, ref.shape[-1])
        groups = ((h_ref, c_ref, s_ref, p0_ref), (h1_ref, c1_ref, s1_ref, p1_ref), (h2_ref, c2_ref, s2_ref, p2_ref))
        for gi, (lhs_ref, cc_ref, ss_ref, out_ref) in enumerate(groups):
            lhs, cc, ss = rows(lhs_ref), rows(cc_ref), rows(ss_ref)
            for half in range(2):
                col = gi * gw + half * PBLK
                val = _rope(_dot(lhs, w_ref[:, col:col + PBLK]), cc, ss, False, 1).astype(BF16)
                if out_ref.ndim == 3:
                    out_ref[:, :, half * PBLK:(half + 1) * PBLK] = val.reshape(out_ref.shape[:2] + (PBLK,))
                else:
                    out_ref[:, half * PBLK:(half + 1) * PBLK] = val
        pb_ref[...] = _rope(_dot(h_ref[...], w_ref[:, 3 * gw:]), c_ref[...], s_ref[...], True, 1).astype(BF16)

    d1, d2 = DIL_GROUPS[1][1], DIL_GROUPS[2][1]
    row = lambda w: pl.BlockSpec((tm, w), lambda i: (i, 0))
    res = lambda d, w: pl.BlockSpec((d, tm // d, w), lambda i: (0, i, 0))
    sds = jax.ShapeDtypeStruct
    return pl.pallas_call(
        _with_dep(body, 9, dep), name="inproj", grid=(t // tm,),
        in_specs=[row(D_MODEL), pl.BlockSpec((1, D_MODEL), lambda i: (0, 0)),
                  pl.BlockSpec((D_MODEL, P_WIDTH), lambda i: (0, 0), pipeline_mode=pl.Buffered(1)),
                  row(128), row(128), res(d1, 128), res(d1, 128), res(d2, 128), res(d2, 128)] + _dep_spec(dep),
        out_specs=[row(D_MODEL), res(d1, D_MODEL), res(d2, D_MODEL), row(gw), res(d1, gw), res(d2, gw), row(PBLK)],
        out_shape=[sds((t, D_MODEL), BF16), sds((d1, t // d1, D_MODEL), BF16), sds((d2, t // d2, D_MODEL), BF16),
                   sds((t, gw), BF16), sds((d1, t // d1, gw), BF16), sds((d2, t // d2, gw), BF16),
                   sds((t, PBLK), BF16)],
        scratch_shapes=[_lane_scratch(tm, D_MODEL)],
        compiler_params=_params(("arbitrary",)),
    )(x, g, w_p, cos, sin, cos1, sin1, cos2, sin2, *_dep_arg(dep))


def _gates(h, w_g, b, tm, tn):
    t = h.shape[0]

    def body(h_ref, w_ref, b_ref, o_ref):
        z = _dot(h_ref[...], w_ref[...]) + b_ref[...]
        o_ref[...] = jax.nn.sigmoid(z).astype(BF16)

    return pl.pallas_call(
        body, name="gates", grid=(t // tm, GATE_WIDTH // tn),
        in_specs=[pl.BlockSpec((tm, D_MODEL), lambda i, j: (i, 0)),
                  pl.BlockSpec((D_MODEL, tn), lambda i, j: (0, j)),
                  pl.BlockSpec((1, tn), lambda i, j: (0, j))],
        out_specs=pl.BlockSpec((tm, tn), lambda i, j: (i, j)),
        out_shape=jax.ShapeDtypeStruct((t, GATE_WIDTH), BF16),
        compiler_params=_params(("arbitrary", "arbitrary")),
    )(h, w_g, b)


def _band_mask(i, s):
    row = lax.broadcasted_iota(jnp.int32, (BAND, 2 * BAND), 0)
    col = lax.broadcasted_iota(jnp.int32, (BAND, 2 * BAND), 1)
    band = (col >= row) & (col <= row + BAND)
    if s == 0:
        band = band & ((col >= BAND) | (i > 0))
    return band


def _head_a_masks(rows):
    lane = lax.broadcasted_iota(jnp.int32, (rows, 128), 1)
    return (lane % HEAD_DIM) < HEAD_DIM // 2, lane < HEAD_DIM


def _stack_heads(x, head_a):
    zero = jnp.zeros_like(x)
    return jnp.concatenate([jnp.where(head_a, x, zero), jnp.where(head_a, zero, x)], axis=0)


def _stack_heads_t(xt, head_a_t):
    zero = jnp.zeros_like(xt)
    return jnp.concatenate([jnp.where(head_a_t, xt, zero), jnp.where(head_a_t, zero, xt)], axis=1)


def _kv_rows(cur_ref, tail_ref, s, off):
    if s == 0:
        return jnp.concatenate([tail_ref[:, off:off + 128], cur_ref[0:BAND, off:off + 128]], axis=0)
    return cur_ref[(s - 1) * BAND:(s + 1) * BAND, off:off + 128]


def _attn_layout(swa):
    if swa:
        return [(128 * j, 512, 640) for j in range(4)]
    return [(0, 128, 256), (384, 512, 640)]


def _attn_fwd(name, pv, swa, sinks, tq):
    d, ls = pv.shape[0], pv.shape[1]
    n, nsb = ls // tq, tq // BAND
    pairs = _attn_layout(swa)
    ncol = 1 if swa else 2
    ow = 128 * len(pairs)

    def body(cur_ref, tail_ref, *rest):
        sink_ref, o_ref, lse_ref, o32_ref = rest if swa else (None,) + rest + (None,)
        i = pl.program_id(2)
        lane = lax.broadcasted_iota(jnp.int32, (BAND, 128), 1)
        qk_a, v_a = _head_a_masks(BAND)
        first = lax.broadcasted_iota(jnp.int32, (2 * BAND, 1), 0) < BAND
        for s in range(nsb):
            mask = _band_mask(i, s)
            mask2 = jnp.concatenate([mask, mask], axis=0)
            rows = slice(s * BAND, (s + 1) * BAND)
            lse_tile = jnp.zeros((BAND, 128), F32)
            for j, (qo, ko, vo) in enumerate(pairs):
                q = cur_ref[rows, qo:qo + 128]
                kk = _kv_rows(cur_ref, tail_ref, s, ko)
                vv = _kv_rows(cur_ref, tail_ref, s, vo)
                sc = _dot_nt(_stack_heads(q, qk_a), kk)
                sc = jnp.where(mask2, sc, -jnp.inf)
                m = jnp.max(sc, axis=-1, keepdims=True)
                if swa:
                    sk = jnp.where(first, sink_ref[2 * j], sink_ref[2 * j + 1])
                    m = jnp.maximum(m, sk)
                p = jnp.exp(sc - m)
                den = jnp.sum(p, axis=-1, keepdims=True)
                if swa:
                    den = den + jnp.exp(sk - m)
                lse = m + jnp.log(den)
                lse_tile = jnp.where(lane == 2 * j, lse[:BAND], jnp.where(lane == 2 * j + 1, lse[BAND:], lse_tile))
                o2 = _dot((p * (1.0 / den)).astype(BF16), vv)
                o = jnp.where(v_a, o2[:BAND], o2[BAND:])
                o_ref[rows, j * 128:(j + 1) * 128] = o.astype(BF16)
                if swa:
                    o32_ref[rows, j * 128:(j + 1) * 128] = o
            lse_ref[rows, :] = lse_tile

    in_specs = [pl.BlockSpec((None, tq, PBLK), lambda r, cb, i: (r, i, cb)),
                pl.BlockSpec((None, BAND, PBLK), lambda r, cb, i: (r, jnp.maximum(i * nsb - 1, 0), cb))]
    args = [pv, pv]
    out_specs = [pl.BlockSpec((None, tq, ow), lambda r, cb, i: (r, i, cb)),
                 pl.BlockSpec((None, tq, 128), lambda r, cb, i: (r, i, cb))]
    out_shape = [jax.ShapeDtypeStruct((d, ls, 512), BF16), jax.ShapeDtypeStruct((d, ls, 128 * ncol), F32)]
    if swa:
        in_specs.append(pl.BlockSpec(memory_space=pltpu.SMEM))
        args.append(sinks)
        out_specs.append(out_specs[0])
        out_shape.append(jax.ShapeDtypeStruct((d, ls, 512), F32))
    return pl.pallas_call(
        body, name=name, grid=(d, ncol, n),
        in_specs=in_specs, out_specs=out_specs, out_shape=out_shape,
        compiler_params=_params(("arbitrary", "arbitrary", "arbitrary")),
    )(*args)


def _lse_lane(head):
    return (head // 4) * 128 + head % 4


def _dil_head_spread():
    lane = lax.broadcasted_iota(jnp.int32, (256, 512), 0)
    head = lax.broadcasted_iota(jnp.int32, (256, 512), 1) // HEAD_DIM
    return (lane == _lse_lane(head)).astype(BF16)


def _head_scale(x, tile, spread):
    return x * _dot(tile.astype(BF16), spread)


def _head_gather(width, dil):
    head = lax.broadcasted_iota(jnp.int32, (8 * HEAD_DIM, width), 0) // HEAD_DIM
    lane = lax.broadcasted_iota(jnp.int32, (8 * HEAD_DIM, width), 1)
    return (lane == (_lse_lane(head) if dil else head)).astype(BF16)


def _head_sums(x, gather):
    hi = x.astype(BF16)
    lo = (x - hi.astype(F32)).astype(BF16)
    return _dot(hi, gather) + _dot(lo, gather)


def _alphas(l0, l1, l2):
    m = jnp.maximum(jnp.maximum(l0, l1), l2)
    e0, e1, e2 = jnp.exp(l0 - m), jnp.exp(l1 - m), jnp.exp(l2 - m)
    den = e0 + e1 + e2
    return e0 / den, e1 / den, e2 / den


def _mix(o0, o1, o2, l0, l1, l2, ob, gts, x, w_a, w_b, w_out, g_cross, tm):
    t = x.shape[0]

    def body(o0_ref, o1_ref, o2_ref, l0_ref, l1_ref, l2_ref, ob_ref, g_ref, x_ref, wa_ref, wb_ref, wo_ref,
             gc_ref, oa_ref, ya_ref, yb_ref, mg_ref, x1_ref, hc_ref, so_ref, sl_ref):
        a0, a1, a2 = _alphas(l0_ref[...], _interleave(l1_ref, sl_ref), _interleave(l2_ref, sl_ref))
        spread = _dil_head_spread()
        oa = (_head_scale(o0_ref[...].astype(F32), a0, spread)
              + _head_scale(_interleave(o1_ref, so_ref), a1, spread)
              + _head_scale(_interleave(o2_ref, so_ref), a2, spread))
        oab = oa.astype(BF16)
        oa_ref[...] = oab
        ya = _dot(oab, wa_ref[...])
        yb = _dot(ob_ref[...], wb_ref[...])
        ya_ref[...] = ya.astype(BF16)
        yb_ref[...] = yb.astype(BF16)
        merged = (g_ref[:, :D_MODEL].astype(F32) * ya + g_ref[:, D_MODEL:].astype(F32) * yb).astype(BF16)
        mg_ref[...] = merged
        x1 = x_ref[...] + _dot(merged, wo_ref[...])
        x1_ref[...] = x1
        hc_ref[...] = (x1 * _rms(x1) * gc_ref[...]).astype(BF16)

    row = lambda w: pl.BlockSpec((tm, w), lambda i: (i, 0))
    full = lambda a, b: pl.BlockSpec((a, b), lambda i: (0, 0))
    return pl.pallas_call(
        body, name="mix", grid=(t // tm,),
        in_specs=[row(512), _res_spec(o1, tm), _res_spec(o2, tm), row(256), _res_spec(l1, tm), _res_spec(l2, tm),
                  row(512), row(GATE_WIDTH),
                  row(D_MODEL), full(512, D_MODEL), full(512, D_MODEL), full(D_MODEL, D_MODEL), full(1, D_MODEL)],
        out_specs=[row(512), row(D_MODEL), row(D_MODEL), row(D_MODEL), row(D_MODEL), row(D_MODEL)],
        out_shape=[jax.ShapeDtypeStruct((t, 512), BF16), jax.ShapeDtypeStruct((t, D_MODEL), BF16),
                   jax.ShapeDtypeStruct((t, D_MODEL), BF16), jax.ShapeDtypeStruct((t, D_MODEL), BF16),
                   jax.ShapeDtypeStruct((t, D_MODEL), F32), jax.ShapeDtypeStruct((t, D_MODEL), BF16)],
        scratch_shapes=[_lane_scratch(tm, 512), _lane_scratch(tm, 256)],
        compiler_params=_params(("arbitrary",)),
    )(o0, o1, o2, l0, l1, l2, ob, gts, x, w_a, w_b, w_out, g_cross)


def _memkv(mem, g_mem, w_ckv):
    m = mem.shape[0]
    ws = w_ckv.shape[2]

    def body(mem_ref, g_ref, w_ref, mn_ref, kv_ref):
        xt = mem_ref[...]
        mn = (xt * _rms(xt) * g_ref[...]).astype(BF16)
        mn_ref[...] = mn
        for j in range(N_DEV):
            kv_ref[:, j * ws:(j + 1) * ws] = _dot(mn, w_ref[j]).astype(BF16)

    return pl.pallas_call(
        body, name="memkv",
        out_shape=[jax.ShapeDtypeStruct((m, D_MODEL), BF16), jax.ShapeDtypeStruct((m, 2 * D_MODEL), BF16)],
        compiler_params=_params(),
    )(mem, g_mem, w_ckv)


def _cross_probs(q, kv_ref, h):
    k = kv_ref[:, h * X_HEAD_DIM:(h + 1) * X_HEAD_DIM]
    sc = _dot_nt(q[:, h * X_HEAD_DIM:(h + 1) * X_HEAD_DIM], k)
    m = jnp.max(sc, axis=-1, keepdims=True)
    p = jnp.exp(sc - m)
    return p / jnp.sum(p, axis=-1, keepdims=True)


def _cross(hc, x1, kv, w_cq, w_co, g_mlp, tm):
    t = x1.shape[0]
    m = kv.shape[0]

    def body(hc_ref, x1_ref, kv_ref, wq_ref, wo_ref, g_ref, q_ref, o_ref, x2_ref, hm_ref):
        q = (_dot(hc_ref[...], wq_ref[...]) * X_SCALE).astype(BF16)
        q_ref[...] = q
        outs = []
        for h in range(X_HEADS):
            p = _cross_probs(q, kv_ref, h)
            v = kv_ref[:, D_MODEL + h * X_HEAD_DIM:D_MODEL + (h + 1) * X_HEAD_DIM]
            outs.append(_dot(p.astype(BF16), v))
        o = jnp.concatenate(outs, axis=1).astype(BF16)
        o_ref[...] = o
        x2 = x1_ref[...] + _dot(o, wo_ref[...])
        x2_ref[...] = x2
        hm_ref[...] = (x2 * _rms(x2) * g_ref[...]).astype(BF16)

    row = lambda w: pl.BlockSpec((tm, w), lambda i: (i, 0))
    full = lambda a, b: pl.BlockSpec((a, b), lambda i: (0, 0))
    return pl.pallas_call(
        body, name="cross", grid=(t // tm,),
        in_specs=[row(D_MODEL), row(D_MODEL), full(m, 2 * D_MODEL), full(D_MODEL, D_MODEL),
                  full(D_MODEL, D_MODEL), full(1, D_MODEL)],
        out_specs=[row(D_MODEL)] * 4,
        out_shape=[jax.ShapeDtypeStruct((t, D_MODEL), BF16), jax.ShapeDtypeStruct((t, D_MODEL), BF16),
                   jax.ShapeDtypeStruct((t, D_MODEL), F32), jax.ShapeDtypeStruct((t, D_MODEL), BF16)],
        compiler_params=_params(("arbitrary",)),
    )(hc, x1, kv, w_cq, w_co, g_mlp)


def _mlp(hm, x2, w_1, w_2, g_final, target, tm, tf):
    t = x2.shape[0]
    nf = D_FF // tf

    def body(hm_ref, x2_ref, w1_ref, w2_ref, g_ref, tg_ref, a_ref, dx3_ref, loss_ref, dg_ref, acc_ref):
        i, f = pl.program_id(0), pl.program_id(1)
        hm_t = hm_ref[...]
        sw = w1_ref.shape[2]
        part = None
        for s in range(w1_ref.shape[0]):
            a = jnp.maximum(_dot(hm_t, w1_ref[s]), 0.0).astype(BF16)
            a_ref[:, s * sw:(s + 1) * sw] = a
            p_s = _dot(a * a, w2_ref[s * sw:(s + 1) * sw, :])
            part = p_s if part is None else part + p_s

        @pl.when(f == 0)
        def _():
            acc_ref[...] = part

        @pl.when(f > 0)
        def _():
            acc_ref[...] += part

        @pl.when((i == 0) & (f == 0))
        def _():
            loss_ref[...] = jnp.zeros_like(loss_ref)
            dg_ref[...] = jnp.zeros_like(dg_ref)

        @pl.when(f == nf - 1)
        def _():
            x3 = x2_ref[...] + acc_ref[...]
            r = _rms(x3)
            g = g_ref[...]
            diff = x3 * r * g - tg_ref[...]
            loss_ref[...] += 0.5 * jnp.sum(jnp.mean(diff * diff, axis=-1, keepdims=True))
            dx3, dg = _rms_bwd(diff / D_MODEL, x3, r, g)
            dx3_ref[...] = dx3
            dg_ref[...] += dg

    return pl.pallas_call(
        body, name="mlp", grid=(t // tm, nf),
        in_specs=[pl.BlockSpec((tm, D_MODEL), lambda i, f: (i, 0)),
                  pl.BlockSpec((tm, D_MODEL), lambda i, f: (i, 0)),
                  pl.BlockSpec((tf // w_1.shape[2], D_MODEL, w_1.shape[2]), lambda i, f: (f, 0, 0)),
                  pl.BlockSpec((tf, D_MODEL), lambda i, f: (f, 0)),
                  pl.BlockSpec((1, D_MODEL), lambda i, f: (0, 0)),
                  pl.BlockSpec((tm, D_MODEL), lambda i, f: (i, 0))],
        out_specs=[pl.BlockSpec((tm, tf), lambda i, f: (i, f)),
                   pl.BlockSpec((tm, D_MODEL), lambda i, f: (i, 0)),
                   pl.BlockSpec((1, 128), lambda i, f: (0, 0)),
                   pl.BlockSpec((1, D_MODEL), lambda i, f: (0, 0))],
        out_shape=[jax.ShapeDtypeStruct((t, D_FF), BF16), jax.ShapeDtypeStruct((t, D_MODEL), F32),
                   jax.ShapeDtypeStruct((1, 128), F32), jax.ShapeDtypeStruct((1, D_MODEL), F32)],
        scratch_shapes=[pltpu.VMEM((tm, D_MODEL), F32)],
        compiler_params=_params(("arbitrary", "arbitrary")),
    )(hm, x2, w_1, w_2, g_final, target)


def _mlp_bwd(dx3, a, w_1, w_2, x2, g_mlp, tm, tf):
    t = x2.shape[0]
    nf = D_FF // tf

    def body(dx3_ref, a_ref, w1_ref, w2_ref, x2_ref, g_ref, dz_ref, dx2_ref, dg_ref, acc_ref):
        i, f = pl.program_id(0), pl.program_id(1)
        da2 = _dot_nt(dx3_ref[...].astype(BF16), w2_ref[...])
        dz = (2.0 * a_ref[...].astype(F32) * da2).astype(BF16)
        dz_ref[...] = dz
        sw = w1_ref.shape[2]
        part = _dot_nt(dz[:, 0:sw], w1_ref[0])
        for s in range(1, w1_ref.shape[0]):
            part = part + _dot_nt(dz[:, s * sw:(s + 1) * sw], w1_ref[s])

        @pl.when(f == 0)
        def _():
            acc_ref[...] = part

        @pl.when(f > 0)
        def _():
            acc_ref[...] += part

        @pl.when((i == 0) & (f == 0))
        def _():
            dg_ref[...] = jnp.zeros_like(dg_ref)

        @pl.when(f == nf - 1)
        def _():
            xt = x2_ref[...]
            dx, dg = _rms_bwd(acc_ref[...], xt, _rms(xt), g_ref[...])
            dx2_ref[...] = dx3_ref[...] + dx
            dg_ref[...] += dg

    return pl.pallas_call(
        body, name="mlp_bwd", grid=(t // tm, nf),
        in_specs=[pl.BlockSpec((tm, D_MODEL), lambda i, f: (i, 0)),
                  pl.BlockSpec((tm, tf), lambda i, f: (i, f)),
                  pl.BlockSpec((tf // w_1.shape[2], D_MODEL, w_1.shape[2]), lambda i, f: (f, 0, 0)),
                  pl.BlockSpec((tf, D_MODEL), lambda i, f: (f, 0)),
                  pl.BlockSpec((tm, D_MODEL), lambda i, f: (i, 0)),
                  pl.BlockSpec((1, D_MODEL), lambda i, f: (0, 0))],
        out_specs=[pl.BlockSpec((tm, tf), lambda i, f: (i, f)),
                   pl.BlockSpec((tm, D_MODEL), lambda i, f: (i, 0)),
                   pl.BlockSpec((1, D_MODEL), lambda i, f: (0, 0))],
        out_shape=[jax.ShapeDtypeStruct((t, D_FF), BF16), jax.ShapeDtypeStruct((t, D_MODEL), F32),
                   jax.ShapeDtypeStruct((1, D_MODEL), F32)],
        scratch_shapes=[pltpu.VMEM((tm, D_MODEL), F32)],
        compiler_params=_params(("arbitrary", "arbitrary")),
    )(dx3, a, w_1, w_2, x2, g_mlp)


def _wgrad(name, a, b, tka, tn, tm, square=False, col_shards=False):
    t, ka = a.shape
    n = b.shape[1]
    nk = t // tm

    def body(a_ref, b_ref, o_ref, acc_ref):
        at = a_ref[...].astype(BF16)
        if square:
            at = at * at
        part = _dot_tn(at, b_ref[...].astype(BF16))
        k = pl.program_id(2)

        @pl.when(k == 0)
        def _():
            acc_ref[...] = part

        @pl.when(k > 0)
        def _():
            acc_ref[...] += part

        @pl.when(k == nk - 1)
        def _():
            if col_shards:
                for s in range(tn // sw):
                    o_ref[s] = acc_ref[:, s * sw:(s + 1) * sw].astype(BF16)
            else:
                o_ref[...] = acc_ref[...].astype(BF16)

    if col_shards:
        sw = n // N_DEV
        out_spec = pl.BlockSpec((tn // sw, tka, sw), lambda p, q, k: (q, p, 0))
        out_shape = jax.ShapeDtypeStruct((N_DEV, ka, sw), BF16)
    else:
        out_spec = pl.BlockSpec((tka, tn), lambda p, q, k: (p, q))
        out_shape = jax.ShapeDtypeStruct((ka, n), BF16)
    return pl.pallas_call(
        body, name=name, grid=(ka // tka, n // tn, nk),
        in_specs=[pl.BlockSpec((tm, tka), lambda p, q, k: (k, p)),
                  pl.BlockSpec((tm, tn), lambda p, q, k: (k, q))],
        out_specs=out_spec, out_shape=out_shape,
        scratch_shapes=[pltpu.VMEM((tka, tn), F32)],
        compiler_params=_params(("arbitrary", "arbitrary", "arbitrary")),
    )(a, b)


def _cross_bwd(dx2, x1, q, kv, w_cq, w_co, g_cross, tm, dep=None):
    t = x1.shape[0]
    m = kv.shape[0]

    def body(dx2_ref, x1_ref, q_ref, kv_ref, wq_ref, wo_ref, g_ref, dq_ref, dx1_ref, dkv_ref, dg_ref):
        @pl.when(pl.program_id(0) == 0)
        def _():
            dkv_ref[...] = jnp.zeros_like(dkv_ref)
            dg_ref[...] = jnp.zeros_like(dg_ref)

        do = _dot_nt(dx2_ref[...].astype(BF16), wo_ref[...]).astype(BF16)
        q = q_ref[...]
        dqs = []
        for h in range(X_HEADS):
            hs = slice(h * X_HEAD_DIM, (h + 1) * X_HEAD_DIM)
            vs = slice(D_MODEL + h * X_HEAD_DIM, D_MODEL + (h + 1) * X_HEAD_DIM)
            p = _cross_probs(q, kv_ref, h)
            dp = _dot_nt(do[:, hs], kv_ref[:, vs])
            ds = (p * (dp - jnp.sum(dp * p, axis=-1, keepdims=True))).astype(BF16)
            dqs.append(_dot(ds, kv_ref[:, hs]))
            dkv_ref[:, hs] += _dot_tn(ds, q[:, hs])
            dkv_ref[:, vs] += _dot_tn(p.astype(BF16), do[:, hs])
        dq = (jnp.concatenate(dqs, axis=1) * X_SCALE).astype(BF16)
        dq_ref[...] = dq
        xt = x1_ref[...]
        dx, dg = _rms_bwd(_dot_nt(dq, wq_ref[...]), xt, _rms(xt), g_ref[...])
        dx1_ref[...] = dx2_ref[...] + dx
        dg_ref[...] += dg

    row = lambda w: pl.BlockSpec((tm, w), lambda i: (i, 0))
    full = lambda a, b: pl.BlockSpec((a, b), lambda i: (0, 0))
    return pl.pallas_call(
        _with_dep(body, 7, dep), name="cross_bwd", grid=(t // tm,),
        in_specs=[row(D_MODEL), row(D_MODEL), row(D_MODEL), full(m, 2 * D_MODEL), full(D_MODEL, D_MODEL),
                  full(D_MODEL, D_MODEL), full(1, D_MODEL)] + _dep_spec(dep),
        out_specs=[row(D_MODEL), row(D_MODEL), full(m, 2 * D_MODEL), full(1, D_MODEL)],
        out_shape=[jax.ShapeDtypeStruct((t, D_MODEL), BF16), jax.ShapeDtypeStruct((t, D_MODEL), F32),
                   jax.ShapeDtypeStruct((m, 2 * D_MODEL), F32), jax.ShapeDtypeStruct((1, D_MODEL), F32)],
        compiler_params=_params(("arbitrary",)),
    )(dx2, x1, q, kv, w_cq, w_co, g_cross, *_dep_arg(dep))


def _memkv_bwd(dkv, mn, mem, w_ckv, g_mem):
    ws = w_ckv.shape[2]

    def body(dkv_ref, mn_ref, mem_ref, w_ref, g_ref, dw_ref, dg_ref):
        mn = mn_ref[...]
        dmn = jnp.zeros(mn.shape, F32)
        for j in range(N_DEV):
            dkvb = dkv_ref[:, j * ws:(j + 1) * ws].astype(BF16)
            dw_ref[j] = _dot_tn(mn, dkvb).astype(BF16)
            dmn = dmn + _dot_nt(dkvb, w_ref[j])
        xt = mem_ref[...]
        dg_ref[...] = jnp.sum(dmn * xt * _rms(xt), axis=0, keepdims=True)

    return pl.pallas_call(
        body, name="memkv_bwd",
        out_shape=[jax.ShapeDtypeStruct(w_ckv.shape, BF16), jax.ShapeDtypeStruct((1, D_MODEL), F32)],
        compiler_params=_params(),
    )(dkv, mn, mem, w_ckv, g_mem)


def _merge_bwd(dx1, ya, yb, gts, w_out, w_g, tm):
    t = dx1.shape[0]

    def body(dx1_ref, ya_ref, yb_ref, g_ref, wo_ref, wg_ref, dg_ref, dhp_ref, dya_ref, dyb_ref, db_ref):
        @pl.when(pl.program_id(0) == 0)
        def _():
            db_ref[...] = jnp.zeros_like(db_ref)

        dm = _dot_nt(dx1_ref[...].astype(BF16), wo_ref[...])
        ga = g_ref[:, :D_MODEL].astype(F32)
        gb = g_ref[:, D_MODEL:].astype(F32)
        dya_ref[...] = (dm * ga).astype(BF16)
        dyb_ref[...] = (dm * gb).astype(BF16)
        dpa = dm * ya_ref[...].astype(F32) * ga * (1.0 - ga)
        dpb = dm * yb_ref[...].astype(F32) * gb * (1.0 - gb)
        dpre = jnp.concatenate([dpa, dpb], axis=1)
        db_ref[...] += jnp.sum(dpre, axis=0, keepdims=True)
        dpreb = dpre.astype(BF16)
        dg_ref[...] = dpreb
        dhp_ref[...] = _dot_nt(dpreb, wg_ref[...])

    row = lambda w: pl.BlockSpec((tm, w), lambda i: (i, 0))
    once = lambda a, b: pl.BlockSpec((a, b), lambda i: (0, 0), pipeline_mode=pl.Buffered(1))
    sds = jax.ShapeDtypeStruct
    return pl.pallas_call(
        body, name="merge_bwd", grid=(t // tm,),
        in_specs=[row(D_MODEL), row(D_MODEL), row(D_MODEL), row(GATE_WIDTH),
                  once(D_MODEL, D_MODEL), once(D_MODEL, GATE_WIDTH)],
        out_specs=[row(GATE_WIDTH), row(D_MODEL), row(D_MODEL), row(D_MODEL),
                   pl.BlockSpec((1, GATE_WIDTH), lambda i: (0, 0))],
        out_shape=[sds((t, GATE_WIDTH), BF16), sds((t, D_MODEL), F32), sds((t, D_MODEL), BF16),
                   sds((t, D_MODEL), BF16), sds((1, GATE_WIDTH), F32)],
        compiler_params=_params(("arbitrary",)),
    )(dx1, ya, yb, gts, w_out, w_g)


def _combine_bwd(dya, dyb, oa, ob, l0, l1, l2, lb, sink_row, w_a, w_b, tm):
    t = dya.shape[0]

    def body(dya_ref, dyb_ref, oa_ref, ob_ref, l0_ref, l1_ref, l2_ref, lb_ref, sk_ref, wa_ref, wb_ref,
             do0_ref, do1_ref, do2_ref, c0_ref, c1_ref, c2_ref, dob_ref, cb_ref, dsk_ref, so_ref, sl_ref):
        @pl.when(pl.program_id(0) == 0)
        def _():
            dsk_ref[...] = jnp.zeros_like(dsk_ref)

        doa = _dot_nt(dya_ref[...], wa_ref[...])
        dob = _dot_nt(dyb_ref[...], wb_ref[...])
        dsum = _head_sums(doa * oa_ref[...].astype(F32), _head_gather(256, True))
        a0, a1, a2 = _alphas(l0_ref[...], _interleave(l1_ref, sl_ref), _interleave(l2_ref, sl_ref))
        c0_ref[...] = a0 * dsum
        spread = _dil_head_spread()
        do0_ref[...] = _head_scale(doa, a0, spread).astype(BF16)
        for al, do_ref, c_ref in ((a1, do1_ref, c1_ref), (a2, do2_ref, c2_ref)):
            _deinterleave(al * dsum, sl_ref, c_ref, F32)
            _deinterleave(_head_scale(doa, al, spread), so_ref, do_ref, BF16)
        dob_ref[...] = dob.astype(BF16)
        cb = _head_sums(dob * ob_ref[...], _head_gather(128, False))
        cb_ref[...] = cb
        lane = lax.broadcasted_iota(jnp.int32, cb.shape, 1)
        psink = jnp.where(lane < 8, jnp.exp(sk_ref[...] - lb_ref[...]), 0.0)
        dsk_ref[...] += jnp.sum(-psink * cb, axis=0, keepdims=True)

    row = lambda w: pl.BlockSpec((tm, w), lambda i: (i, 0))
    full = lambda a, b: pl.BlockSpec((a, b), lambda i: (0, 0))
    sds = jax.ShapeDtypeStruct
    d1, d2 = l1.shape[0], l2.shape[0]
    res = lambda d, w: pl.BlockSpec((d, tm // d, w), lambda i: (0, i, 0))
    return pl.pallas_call(
        body, name="combine_bwd", grid=(t // tm,),
        in_specs=[row(D_MODEL), row(D_MODEL), row(512), row(512),
                  row(256), _res_spec(l1, tm), _res_spec(l2, tm), row(128), full(1, 128),
                  full(512, D_MODEL), full(512, D_MODEL)],
        out_specs=[row(512), res(d1, 512), res(d2, 512), row(256), res(d1, 256), res(d2, 256),
                   row(512), row(128), full(1, 128)],
        out_shape=[sds((t, 512), BF16), sds((d1, t // d1, 512), BF16),
                   sds((d2, t // d2, 512), BF16), sds((t, 256), F32), sds((d1, t // d1, 256), F32),
                   sds((d2, t // d2, 256), F32), sds((t, 512), BF16),
                   sds((t, 128), F32), sds((1, 128), F32)],
        scratch_shapes=[_lane_scratch(tm, 512), _lane_scratch(tm, 256)],
        compiler_params=_params(("arbitrary",)),
    )(dya, dyb, oa, ob, l0, l1, l2, lb, sink_row, w_a, w_b)


def _attn_bwd(name, pv, dov, lsev, cv, cosv, sinv, swa, tq, dep=None):
    d, ls = pv.shape[0], pv.shape[1]
    n, nsb = ls // tq, tq // BAND
    pairs = _attn_layout(swa)
    ncol = 1 if swa else 2
    ow = 128 * len(pairs)

    kv_slots = sorted({(ko, vo) for _, ko, vo in pairs})

    def body(cur_ref, tail_ref, do_ref, lse_ref, c_ref, cos_ref, sin_ref, out_ref, acc_ref, carry_ref, acct_ref):
        i = pl.program_id(2)
        blk_i = n - 1 - i
        acc_ref[...] = jnp.zeros_like(acc_ref)
        acct_ref[...] = jnp.zeros_like(acct_ref)

        @pl.when(i == 0)
        def _():
            carry_ref[...] = jnp.zeros_like(carry_ref)

        qk_a, v_a = _head_a_masks(BAND)
        dim = lax.broadcasted_iota(jnp.int32, (128, BAND), 0)
        qk_at, v_at = (dim % HEAD_DIM) < HEAD_DIM // 2, dim < HEAD_DIM
        for s in range(nsb):
            mask = _band_mask(blk_i, s)
            mask2 = jnp.concatenate([mask, mask], axis=0)
            rows = slice(s * BAND, (s + 1) * BAND)
            kcols = slice(s * BAND, (s + 2) * BAND)
            for j, (qo, ko, vo) in enumerate(pairs):
                slot = kv_slots.index((ko, vo))
                kk = _kv_rows(cur_ref, tail_ref, s, ko)
                vv = _kv_rows(cur_ref, tail_ref, s, vo)
                q, do = cur_ref[rows, qo:qo + 128], do_ref[rows, j * 128:(j + 1) * 128]
                q2, do2 = _stack_heads(q, qk_a), _stack_heads(do, v_a)
                col2 = lambda ref: jnp.concatenate([ref[rows, 2 * j:2 * j + 1], ref[rows, 2 * j + 1:2 * j + 2]], axis=0)
                sc = _dot_nt(q2, kk)
                p = jnp.exp(jnp.where(mask2, sc, -jnp.inf) - col2(lse_ref))
                dp = _dot_nt(do2, vv)
                ds = (p * (dp - col2(c_ref))).astype(BF16)
                dq2 = _dot(ds, kk)
                acc_ref[BAND + s * BAND:BAND + (s + 1) * BAND, qo:qo + 128] += jnp.where(qk_a, dq2[:BAND], dq2[BAND:])
                acct_ref[2 * slot, :, kcols] += _dot(_stack_heads_t(q.T, qk_at), ds)
                acct_ref[2 * slot + 1, :, kcols] += _dot(_stack_heads_t(do.T, v_at), p.astype(BF16))
        for slot, (ko, vo) in enumerate(kv_slots):
            acc_ref[:, ko:ko + 128] += acct_ref[2 * slot].T
            acc_ref[:, vo:vo + 128] += acct_ref[2 * slot + 1].T

        last = acc_ref[tq:, :] + carry_ref[...]
        fin = last if tq == BAND else jnp.concatenate([acc_ref[BAND:tq, :], last], axis=0)
        out_ref[...] = _rope(fin, cos_ref[...], sin_ref[...], swa, -1).astype(BF16)
        carry_ref[...] = acc_ref[0:BAND, :]

    rev = lambda i: n - 1 - i
    blk = lambda rows, w, row_of: pl.BlockSpec((None, rows, w), lambda r, cb, i: (r, row_of(i), cb))
    tab = pl.BlockSpec((None, tq, 128), lambda r, cb, i: (r, rev(i), 0))
    return pl.pallas_call(
        _with_dep(body, 7, dep), name=name, grid=(d, ncol, n),
        in_specs=[blk(tq, PBLK, rev), blk(BAND, PBLK, lambda i: jnp.maximum(rev(i) * nsb - 1, 0)),
                  blk(tq, ow, rev), blk(tq, 128, rev), blk(tq, 128, rev), tab, tab] + _dep_spec(dep),
        out_specs=blk(tq, PBLK, rev),
        out_shape=jax.ShapeDtypeStruct((d, ls, ncol * PBLK), BF16),
        scratch_shapes=[pltpu.VMEM((tq + BAND, PBLK), F32), pltpu.VMEM((BAND, PBLK), F32),
                        pltpu.VMEM((2 * len(kv_slots), 128, tq + BAND), F32)],
        compiler_params=_params(("arbitrary", "arbitrary", "arbitrary")),
    )(pv, pv, dov, lsev, cv, cosv, sinv, *_dep_arg(dep))


def _dx(dp0, dp1, dp2, dpb, w_p, dh_part, dx1, x, g_mix, tm, dep=None):
    t = x.shape[0]
    gw = 2 * PBLK

    def body(dp0_ref, dp1_ref, dp2_ref, dpb_ref, w_ref, dhp_ref, dx1_ref, x_ref, g_ref, gx_ref, dg_ref,
             dpt_ref, scr_ref):
        @pl.when(pl.program_id(0) == 0)
        def _():
            dg_ref[...] = jnp.zeros_like(dg_ref)

        dpt_ref[:, 0:gw] = dp0_ref[...]
        dpt_ref[:, gw:2 * gw] = _interleave(dp1_ref, scr_ref).astype(BF16)
        dpt_ref[:, 2 * gw:3 * gw] = _interleave(dp2_ref, scr_ref).astype(BF16)
        dpt_ref[:, 3 * gw:] = dpb_ref[...]
        dh = _dot_nt(dpt_ref[...], w_ref[...]) + dhp_ref[...]
        xt = x_ref[...]
        dx, dg = _rms_bwd(dh, xt, _rms(xt), g_ref[...])
        gx_ref[...] = dx1_ref[...] + dx
        dg_ref[...] += dg

    row = lambda w: pl.BlockSpec((tm, w), lambda i: (i, 0))
    full = lambda a, b: pl.BlockSpec((a, b), lambda i: (0, 0))
    return pl.pallas_call(
        _with_dep(body, 9, dep), name="dx", grid=(t // tm,),
        in_specs=[row(gw), _res_spec(dp1, tm), _res_spec(dp2, tm), row(PBLK),
                  pl.BlockSpec((D_MODEL, P_WIDTH), lambda i: (0, 0), pipeline_mode=pl.Buffered(1)),
                  row(D_MODEL), row(D_MODEL), row(D_MODEL), full(1, D_MODEL)] + _dep_spec(dep),
        out_specs=[row(D_MODEL), full(1, D_MODEL)],
        out_shape=[jax.ShapeDtypeStruct((t, D_MODEL), F32), jax.ShapeDtypeStruct((1, D_MODEL), F32)],
        scratch_shapes=[pltpu.VMEM((tm, P_WIDTH), BF16), _lane_scratch(tm, gw)],
        compiler_params=_params(("arbitrary",)),
    )(dp0, dp1, dp2, dpb, w_p, dh_part, dx1, x, g_mix, *_dep_arg(dep))


MESH = pl.DeviceIdType.MESH
HBM_SPEC = pl.BlockSpec(memory_space=pltpu.HBM)
VMEM_SPEC = pl.BlockSpec(memory_space=pltpu.VMEM)


def _all_gather(xp, n_src):
    def body(x_ref, out_ref, send_sems, recv_sems, local_sem):
        x, y, c = lax.axis_index("x"), lax.axis_index("y"), lax.axis_index("c")
        me, sibling = (x, y, c), (x, y, 1 - c)
        chips = [(1 - x, y), (x, 1 - y), (1 - x, 1 - y)]
        is_src = lambda px, py: 4 * px + 2 * py < n_src
        i_send = is_src(x, y)

        def rows(px, py, pc):
            return out_ref.at[4 * px + 2 * py + pc]

        def copy(k, block, to, src=None):
            return pltpu.make_async_remote_copy(
                src_ref=rows(*block) if src is None else src, dst_ref=rows(*block),
                send_sem=send_sems.at[k], recv_sem=recv_sems.at[k], device_id=to, device_id_type=MESH)

        mine = pltpu.make_async_copy(x_ref, rows(*me), local_sem)
        first = [copy(0, me, sibling, src=x_ref)]
        first += [copy(1 + j, me, (*chip, c), src=x_ref) for j, chip in enumerate(chips)]
        passed = [copy(4 + j, (*chip, c), sibling) for j, chip in enumerate(chips)]

        @pl.when(i_send)
        def _():
            mine.start()
            for cp in first:
                cp.start()

        for j, chip in enumerate(chips):
            @pl.when(is_src(*chip))
            def _():
                copy(1 + j, (*chip, c), me).wait_recv()
                passed[j].start()

        @pl.when(i_send)
        def _():
            copy(0, sibling, me).wait_recv()

        for j, chip in enumerate(chips):
            @pl.when(is_src(*chip))
            def _():
                copy(4 + j, (*chip, 1 - c), me).wait_recv()
                passed[j].wait_send()

        @pl.when(i_send)
        def _():
            for cp in first:
                cp.wait_send()
            mine.wait()

    return pl.pallas_call(
        body, name="all_gather",
        out_shape=jax.ShapeDtypeStruct((N_DEV,) + xp.shape, xp.dtype),
        in_specs=[HBM_SPEC], out_specs=HBM_SPEC,
        scratch_shapes=[pltpu.SemaphoreType.DMA((7,)), pltpu.SemaphoreType.DMA((7,)), pltpu.SemaphoreType.DMA],
    )(xp)


def _peers():
    x, y, c = lax.axis_index("x"), lax.axis_index("y"), lax.axis_index("c")
    out = []
    for k in range(1, N_DEV):
        px = 1 - x if k & 4 else x
        py = 1 - y if k & 2 else y
        pc = 1 - c if k & 1 else c
        out.append((k, (px, py, pc), 4 * px + 2 * py + pc))
    return out


def _my_index():
    return 4 * lax.axis_index("x") + 2 * lax.axis_index("y") + lax.axis_index("c")


SEM_SPEC = pl.BlockSpec(memory_space=pltpu.SEMAPHORE)
ANY_SPEC = pl.BlockSpec(memory_space=pl.ANY)
_SPLIT_PARAMS = pltpu.CompilerParams(has_side_effects=pltpu.SideEffectType.DATAFLOW_SIDE_EFFECTING)


def _split_copies(gather, src_refs, land_refs, send_sems, recv_sems):
    me_idx = _my_index()
    out = []
    for a, (src_ref, land_ref) in enumerate(zip(src_refs, land_refs)):
        for k, peer, peer_idx in _peers():
            if gather:
                src, dst = src_ref, land_ref.at[me_idx]
            else:
                src, dst = src_ref.at[peer_idx], land_ref.at[k - 1]
            out.append(pltpu.make_async_remote_copy(
                src_ref=src, dst_ref=dst, send_sem=send_sems.at[7 * a + k - 1], recv_sem=recv_sems.at[7 * a + k - 1],
                device_id=peer, device_id_type=MESH))
    return out


def _split_start(name, gather, srcs):
    n = len(srcs)

    def body(*refs):
        send_sems, recv_sems = refs[n], refs[n + 1]
        for cp in _split_copies(gather, refs[:n], refs[2 * n + 2:3 * n + 2], send_sems, recv_sems):
            cp.start()
        token = refs[-1]
        token[...] = jnp.zeros_like(token)

    lands = [pltpu.HBM((N_DEV,) + a.shape if gather else (N_DEV - 1,) + a.shape[1:], a.dtype) for a in srcs]
    return pl.pallas_call(
        body, name=name,
        out_shape=(pltpu.SemaphoreType.DMA((7 * n,)), pltpu.SemaphoreType.DMA((7 * n,)),
                   *[pltpu.HBM(a.shape, a.dtype) for a in srcs], *lands, jax.ShapeDtypeStruct((8, 128), F32)),
        in_specs=(HBM_SPEC,) * n, out_specs=(SEM_SPEC, SEM_SPEC) + (HBM_SPEC,) * (2 * n) + (VMEM_SPEC,),
        input_output_aliases={i: 2 + i for i in range(n)}, compiler_params=_SPLIT_PARAMS,
    )(*[pltpu.with_memory_space_constraint(a, pltpu.HBM) for a in srcs])


def _split_wait(name, gather, started, after):
    send_sems, recv_sems, bufs = started[0], started[1], started[2:-1]
    n = len(bufs) // 2

    def body(*refs):
        for cp in _split_copies(gather, refs[:n], refs[n:2 * n], refs[2 * n], refs[2 * n + 1]):
            cp.wait_send()
            cp.wait_recv()

    out = pl.pallas_call(
        body, name=name, out_shape=tuple(pltpu.HBM(a.shape, a.dtype) for a in bufs),
        in_specs=(HBM_SPEC,) * (2 * n) + (SEM_SPEC, SEM_SPEC, ANY_SPEC), out_specs=(HBM_SPEC,) * (2 * n),
        input_output_aliases={i: i for i in range(2 * n)}, compiler_params=_SPLIT_PARAMS,
    )(*bufs, send_sems, recv_sems, after)
    return out[:n], out[n:]


def _adam_update(g, w, m, v):
    nm = ADAM_B1 * m + (1.0 - ADAM_B1) * g
    nv = ADAM_B2 * v + (1.0 - ADAM_B2) * (g * g)
    m_hat = nm / (1.0 - ADAM_B1 ** ADAM_STEP)
    v_hat = nv / (1.0 - ADAM_B2 ** ADAM_STEP)
    return -ADAM_LR * (m_hat / (jnp.sqrt(v_hat) + ADAM_EPS) + ADAM_WD * w), nm, nv


def _adamw(name, me, sent, got, w, m, v, tr):
    r, c = w.shape

    def body(me_ref, own_ref, got_ref, w_ref, m_ref, v_ref, g_ref, d_ref, nm_ref, nv_ref):
        g = own_ref[...].astype(F32)
        for k in range(N_DEV - 1):
            g = g + got_ref[k].astype(F32)
        g_ref[...] = g
        d_ref[...], nm_ref[...], nv_ref[...] = _adam_update(g, w_ref[...], m_ref[...], v_ref[...])

    blk = pl.BlockSpec((tr, c), lambda i, me_ref: (i, 0))
    return pl.pallas_call(
        body, name=name,
        grid_spec=pltpu.PrefetchScalarGridSpec(
            num_scalar_prefetch=1, grid=(r // tr,),
            in_specs=[pl.BlockSpec((None, tr, c), lambda i, me_ref: (me_ref[0], i, 0)),
                      pl.BlockSpec((N_DEV - 1, tr, c), lambda i, me_ref: (0, i, 0)), blk, blk, blk],
            out_specs=[blk] * 4),
        out_shape=[jax.ShapeDtypeStruct((r, c), F32)] * 4,
        compiler_params=_params(("arbitrary",)),
    )(me, sent, got, w, m, v)


def _adamw_small(srecv, ws, ms, vs):
    nv_ = len(ws)

    def body(*refs):
        s_ref = refs[0]
        ins, outs = refs[1:1 + 3 * nv_], refs[1 + 3 * nv_:]
        g_all = s_ref[0]
        for k in range(1, N_DEV):
            g_all = g_all + s_ref[k]
        for i in range(nv_):
            n = ins[i].shape[1]
            g = g_all[i:i + 1, :n]
            d, nm, nv = _adam_update(g, ins[i][...], ins[nv_ + i][...], ins[2 * nv_ + i][...])
            outs[i][...], outs[nv_ + i][...], outs[2 * nv_ + i][...], outs[3 * nv_ + i][...] = g, d, nm, nv
        outs[-1][...] = g_all[nv_:nv_ + 1, :128]

    shapes = [jax.ShapeDtypeStruct(a.shape, F32) for a in ws]
    res = pl.pallas_call(body, name="adamw_small", out_shape=shapes * 4 + [jax.ShapeDtypeStruct((1, 128), F32)],
                         compiler_params=_params())(srecv, *ws, *ms, *vs)
    return [res[k * nv_:(k + 1) * nv_] for k in range(4)], res[-1]


def _cols_from_shards(a):
    return jnp.swapaxes(a, 0, 1).reshape(a.shape[1], a.shape[0] * a.shape[2])


def _shards_from_cols(a):
    return jnp.swapaxes(a.reshape(a.shape[0], N_DEV, a.shape[1] // N_DEV), 0, 1)


def _shards_from_rows(a):
    return a.reshape(N_DEV, a.shape[0] // N_DEV, a.shape[1])


def _pair_lanes(a):
    lead = a.shape[:-1]
    return a.reshape(lead + (2, 2, HEAD_DIM // 2)).swapaxes(-3, -2).reshape(lead + (128,))


def _split_w_in(w_in):
    rows = w_in.shape[0]
    dil = w_in[:, :3 * DIL_WIDTH].reshape(rows, 3, 3, 4, 128)
    dil = jnp.concatenate([_pair_lanes(dil[:, :2]), dil[:, 2:]], axis=1)
    dil = dil.transpose(0, 2, 3, 1, 4).reshape(rows, 3 * DIL_WIDTH)
    o = 3 * DIL_WIDTH
    qb = w_in[:, o:o + SWA_Q_WIDTH].reshape(rows, 2, 4, HEAD_DIM).transpose(0, 2, 1, 3).reshape(rows, 4, 128)
    qb = _pair_lanes(qb).reshape(rows, SWA_Q_WIDTH)
    kb = _pair_lanes(w_in[:, o + SWA_Q_WIDTH:o + SWA_Q_WIDTH + SWA_KV_WIDTH])
    vb = w_in[:, o + SWA_Q_WIDTH + SWA_KV_WIDTH:P_WIDTH]
    return jnp.concatenate([dil, qb, kb, vb], axis=1)


def _merge_w_in(dw_p, dw_g):
    rows = dw_p.shape[0]
    dil = dw_p[:, :3 * DIL_WIDTH].reshape(rows, 3, 4, 3, 128).transpose(0, 3, 1, 2, 4)
    dil = jnp.concatenate([_pair_lanes(dil[:, :2]), dil[:, 2:]], axis=1).reshape(rows, 3 * DIL_WIDTH)
    o = 3 * DIL_WIDTH
    qb = _pair_lanes(dw_p[:, o:o + SWA_Q_WIDTH].reshape(rows, 4, 128))
    qb = qb.reshape(rows, 4, 2, HEAD_DIM).transpose(0, 2, 1, 3).reshape(rows, SWA_Q_WIDTH)
    kb = _pair_lanes(dw_p[:, o + SWA_Q_WIDTH:o + SWA_Q_WIDTH + SWA_KV_WIDTH])
    vb = dw_p[:, o + SWA_Q_WIDTH + SWA_KV_WIDTH:]
    return jnp.concatenate([dil, qb, kb, vb, dw_g], axis=1)


def _swa_rows(w_b):
    return w_b.reshape(2, 4, HEAD_DIM, -1).transpose(1, 0, 2, 3).reshape(SWA_Q_WIDTH, -1)


def _swa_rows_inv(dw_b):
    return dw_b.reshape(4, 2, HEAD_DIM, -1).transpose(1, 0, 2, 3).reshape(SWA_Q_WIDTH, -1)


def _rope_tables(pos):
    half = HEAD_DIM // 2
    inv = ROPE_THETA ** (-jnp.arange(half, dtype=F32) / half)
    ang = pos.astype(F32)[:, None] * jnp.tile(inv, 4)
    sign = jnp.repeat(jnp.array([-1.0, 1.0], F32), 2 * half)
    return jnp.cos(ang), jnp.sin(ang) * sign


def _local_step(x, mem, pos, target, w_in, dep, rest_weights, on_grads, g_mix, g_cross, g_mem, g_mlp, g_final, sink):
    t = x.shape[0]
    tm = min(512, t)
    tq = 1024
    tw = min(2048, t)
    w_p = _split_w_in(w_in)
    cos, sin = lax.optimization_barrier(_rope_tables(pos))
    sink_row = jnp.pad(sink.reshape(2, 4).T.reshape(1, 8), ((0, 0), (0, 120)))
    tabs = [(cos[None], sin[None])]
    for _, d in DIL_GROUPS[1:]:
        tabs.append(tuple(a.reshape(t // d, d, 128).swapaxes(0, 1) for a in (cos, sin)))
    tabs.append(tabs[0])

    h, h1, h2, p0, p1, p2, pb = _inproj(x, g_mix, w_p, [(cos, sin), tabs[1], tabs[2]], tm, dep)
    ps = [p0[None], p1, p2, pb[None]]
    outs, lses = [], []
    for gi, pv in enumerate(ps):
        res = _attn_fwd(f"attn_fwd{gi}", pv, gi == 3, sink_row[0, :8], min(tq, pv.shape[1]))
        outs.append(res[0])
        lses.append(res[1])
    o0, l0, ob, lb, ob32 = outs[0][0], lses[0][0], outs[3][0], lses[3][0], res[2][0]
    wts = rest_weights(lb)
    w_b = _swa_rows(wts["w_branch_b"])
    tf = 2048
    w_g = wts["w_g"]
    gts = _gates(h, w_g, wts["b_gate"].reshape(1, GATE_WIDTH), tm, 1024)
    oa, ya, yb, merged, x1, hc = _mix(o0, outs[1], outs[2], l0, lses[1], lses[2], ob, gts, x,
                                      wts["w_branch_a"], w_b, wts["w_out"], g_cross, tm)
    mn, kv = _memkv(mem, g_mem, wts["w_ckv"])
    q, o, x2, hm = _cross(hc, x1, kv, wts["w_cq"], wts["w_co"], g_mlp, tm)
    a, dx3, loss, dg_final = _mlp(hm, x2, wts["w_1"], wts["w_2"], g_final.reshape(1, D_MODEL), target, tm, tf)

    grads = {}
    dz, dx2, dg_mlp = _mlp_bwd(dx3, a, wts["w_1"], wts["w_2"], x2, g_mlp, tm, tf)
    grads["w_2"] = _shards_from_rows(_wgrad("dw_2", a, dx3, 1024, 1024, tw, square=True))
    grads["w_1"] = _wgrad("dw_1", hm, dz, 1024, 1024, tw, col_shards=True)
    dep = on_grads(GROUP_A, grads)
    dq, dx1, dkv, dg_cross = _cross_bwd(dx2, x1, q, kv, wts["w_cq"], wts["w_co"], g_cross, tm, dep)
    grads["w_co"] = _shards_from_rows(_wgrad("dw_co", o, dx2, 1024, 1024, tw))
    grads["w_cq"] = _shards_from_rows(_wgrad("dw_cq", hc, dq, 1024, 1024, tw))
    grads["w_ckv"], dg_mem = _memkv_bwd(dkv, mn, mem, wts["w_ckv"], g_mem)
    dgt, dh_part, dya, dyb, db_gate = _merge_bwd(dx1, ya, yb, gts, wts["w_out"], w_g, tm)
    do0, do1, do2, c0, c1, c2, dob, cb, dsink = _combine_bwd(
        dya, dyb, oa, ob32, l0, lses[1], lses[2], lb, sink_row, wts["w_branch_a"], w_b, tm)
    grads["w_out"] = _shards_from_rows(_wgrad("dw_out", merged, dx1, 1024, 1024, tw))
    grads["w_branch_a"] = _shards_from_cols(_wgrad("dw_a", oa, dya, 512, 1024, tw))
    grads["w_branch_b"] = _shards_from_cols(_swa_rows_inv(_wgrad("dw_b", ob, dyb, 512, 1024, tw)))
    grads["b_gate"] = _shards_from_cols(db_gate.reshape(2, D_MODEL)).astype(BF16)
    dep = on_grads(GROUP_B, grads)
    dw_g = _wgrad("dw_g", h, dgt, 1024, 1024, tw)
    dps = []
    for gi, (pv, do_g, c_g) in enumerate(zip(ps, (do0[None], do1, do2, dob[None]), (c0[None], c1, c2, cb[None]))):
        dps.append(_attn_bwd(f"attn_bwd{gi}", pv, do_g, lses[gi], c_g, tabs[gi][0], tabs[gi][1], gi == 3,
                             min(tq, pv.shape[1]),
                             dep if gi == 0 else None))
    dw_p = jnp.concatenate(
        [_wgrad(f"dw_p{gi}", hh.reshape(t, D_MODEL), dpg.reshape(t, -1), 1024, PBLK, tw)
         for gi, (hh, dpg) in enumerate(zip((h, h1, h2, h), dps))], axis=1)
    grads["w_in"] = _shards_from_cols(_merge_w_in(dw_p, dw_g))
    dep = on_grads(GROUP_C, grads)
    grad_x, dg_mix = _dx(dps[0][0], dps[1], dps[2], dps[3][0], w_p, dh_part, dx1, x, g_mix, tm, dep)
    dsink_heads = dsink[0, :8].reshape(4, 2).T.reshape(8)
    small = {"g_mix": dg_mix[0], "g_cross": dg_cross[0], "g_mem": dg_mem[0], "g_mlp": dg_mlp[0],
             "g_final": dg_final[0], "sink": dsink_heads}
    return loss[0, 0], grad_x, small


def kernel(x, mem, positions, g_mix, w_in, b_gate, sink, w_branch_a, w_branch_b, w_out, g_cross, g_mem, w_cq, w_ckv, w_co, g_mlp, w_1, w_2, g_final, loss_target, m_g_mix, m_w_in, m_b_gate, m_sink, m_w_branch_a, m_w_branch_b, m_w_out, m_g_cross, m_g_mem, m_w_cq, m_w_ckv, m_w_co, m_g_mlp, m_w_1, m_w_2, m_g_final, v_g_mix, v_w_in, v_b_gate, v_sink, v_w_branch_a, v_w_branch_b, v_w_out, v_g_cross, v_g_mem, v_w_cq, v_w_ckv, v_w_co, v_g_mlp, v_w_1, v_w_2, v_g_final):
    local = dict(locals())
    shard = {n: local[n][0] for n in GROUP_A + GROUP_B + GROUP_C}
    me = _my_index()
    me_arr = me.reshape(1).astype(jnp.int32)
    tags = {GROUP_A: "a", GROUP_B: "b", GROUP_C: "c"}

    w_in16 = shard["w_in"].astype(BF16)
    w_qkv = _cols_from_shards(_all_gather(w_in16, QKV_SRC)[:QKV_SRC])
    rest = GROUP_A + GROUP_B + GROUP_C

    def gathered(name, started, after):
        srcs, lands = _split_wait(name, True, started, after)
        return [lax.dynamic_update_slice(land, src[None], (me,) + (0,) * src.ndim) for src, land in zip(srcs, lands)]

    gather = _split_start("gather_start", True,
                          [shard[n] if n == "b_gate" else w_in16 if n == "w_in" else shard[n].astype(BF16)
                           for n in rest])

    def rest_weights(after):
        full = {}
        for name, a in zip(rest, gathered("gather_wait", gather, after)):
            if name == "w_in":
                tail = _cols_from_shards(a[QKV_SRC - 1:])
                full["w_g"] = tail[:, tail.shape[1] - GATE_WIDTH:]
            elif name in ("w_1", "w_ckv"):
                full[name] = a
            elif name in _COL_SHARDED:
                full[name] = _cols_from_shards(a)
            else:
                full[name] = a.reshape(N_DEV * a.shape[1], a.shape[2])
        return full

    scatters = {}

    def on_grads(names, grads):
        scatters[names] = _split_start("scatter_start_" + tags[names], False, [grads[n] for n in names])
        return scatters[names][-1]

    loss, grad_x, small = _local_step(
        x[0], mem[0], positions[0], loss_target[0], w_qkv, gather[-1], rest_weights, on_grads,
        g_mix, g_cross, g_mem, g_mlp, g_final, sink[0])

    sp = jnp.stack([small[n] if n != "sink" else jnp.pad(small[n], (0, LANES - 8)) for n in SMALL]
                   + [jnp.pad(loss.reshape(1), (0, LANES - 1)), jnp.zeros((LANES,), F32)])
    small_gather = _split_start("small_start", True, [sp])

    after, updated = small_gather[-1], {}
    for names in (GROUP_A, GROUP_B, GROUP_C):
        sent, got = _split_wait("scatter_wait_" + tags[names], False, scatters[names], after)
        for i, name in enumerate(names):
            outs = _adamw("adamw_" + name, me_arr, sent[i], got[i], shard[name],
                          local["m_" + name][0], local["v_" + name][0], ADAM_ROWS[name])
            updated[name] = [a[None] for a in outs]
            after = outs[3]

    flat = lambda prefix: [local[prefix + n].reshape(1, -1) for n in SMALL]
    outs, loss_row = _adamw_small(gathered("small_wait", small_gather, after)[0], flat(""), flat("m_"), flat("v_"))
    for i, name in enumerate(SMALL):
        updated[name] = [outs[which][i].reshape(local[name].shape) for which in range(4)]

    order = ["g_mix", "w_in", "b_gate", "sink", "w_branch_a", "w_branch_b", "w_out", "g_cross", "g_mem", "w_cq",
             "w_ckv", "w_co", "g_mlp", "w_1", "w_2", "g_final"]
    res = [loss_row[0, 0], grad_x[None]]
    for which in range(4):
        res += [updated[n][which] for n in order]
    return tuple(res)
```

```python
import functools
import math

import jax
import jax.numpy as jnp
from jax import lax
from jax.experimental import pallas as pl
from jax.experimental.pallas import tpu as pltpu

F32 = jnp.float32
BF16 = jnp.bfloat16

D_MODEL = 1024
HEAD_DIM = 64
DIL_GROUPS = ((128, 1), (512, 4), (2048, 16))
ROPE_THETA = 10000.0
X_HEADS = 4
X_HEAD_DIM = D_MODEL // X_HEADS
D_FF = 4 * D_MODEL
EPS = 1e-6
DIL_WIDTH = 1536
SWA_Q_WIDTH = 512
SWA_KV_WIDTH = 128
P_WIDTH = 3 * DIL_WIDTH + SWA_Q_WIDTH + 2 * SWA_KV_WIDTH
GATE_WIDTH = 2 * D_MODEL
IN_WIDTH = P_WIDTH + GATE_WIDTH
BAND = 128
PBLK = 768
Q_SCALE = HEAD_DIM ** -0.5
X_SCALE = X_HEAD_DIM ** -0.5

ADAM_LR = 0.001
ADAM_B1 = 0.9
ADAM_B2 = 0.999
ADAM_EPS = 1e-08
ADAM_WD = 0.01
ADAM_STEP = 10

N_DEV = 8
LANES = 1024
VMEM_LIMIT = 52 * 1024 * 1024

NT = (((1,), (1,)), ((), ()))
TN = (((0,), (0,)), ((), ()))

GROUP_A = ("w_1", "w_2")
GROUP_B = ("w_branch_a", "w_branch_b", "w_out", "w_cq", "w_ckv", "w_co", "b_gate")
GROUP_C = ("w_in",)
_COL_SHARDED = ("w_in", "w_branch_a", "w_branch_b", "w_ckv", "w_1", "b_gate")
ADAM_ROWS = {"w_in": 256, "w_branch_a": 512, "w_branch_b": 512, "w_out": 128, "w_cq": 128, "w_ckv": 512,
             "w_co": 128, "w_1": 256, "w_2": 256, "b_gate": 2}
SMALL = ("g_mix", "g_cross", "g_mem", "g_mlp", "g_final", "sink")


def _params(sem=None):
    return pltpu.CompilerParams(dimension_semantics=sem, vmem_limit_bytes=VMEM_LIMIT)


def _dot(a, b):
    return jnp.dot(a, b, preferred_element_type=F32)


def _dot_nt(a, b):
    return lax.dot_general(a, b, NT, preferred_element_type=F32)


def _dot_tn(a, b):
    return lax.dot_general(a, b, TN, preferred_element_type=F32)


def _rms(xt):
    return lax.rsqrt(jnp.mean(xt * xt, axis=-1, keepdims=True) + EPS)


def _rms_bwd(dh, xt, r, g):
    xn = xt * r
    dxn = dh * g
    dx = r * (dxn - xn * jnp.mean(dxn * xn, axis=-1, keepdims=True))
    return dx, jnp.sum(dh * xn, axis=0, keepdims=True)


def _rope(x, c, s, swa, sign):
    kinds = "qqqqkv" if swa else "qkvqkv"
    cq, sq = c * Q_SCALE, s * (sign * Q_SCALE)
    sk = s * sign if sign != 1 else s
    out = []
    for ci, kind in enumerate(kinds):
        xc = x[:, ci * 128:(ci + 1) * 128]
        if kind == "v":
            out.append(xc)
        elif kind == "q":
            out.append(xc * cq + pltpu.roll(xc, 64, 1) * sq)
        else:
            out.append(xc * c + pltpu.roll(xc, 64, 1) * sk)
    return jnp.concatenate(out, axis=1)


def _lane_scratch(rows, w):
    return pltpu.VMEM((w // 128, rows, 128), F32)


def _deinterleave(val, scr_ref, dst_ref, dtype):
    d, n = dst_ref.shape[0], dst_ref.shape[1]
    nc = val.shape[1] // 128
    for c in range(nc):
        scr_ref[c] = val[:, c * 128:(c + 1) * 128]
    for r in range(d):
        rows = [scr_ref.at[c][pl.ds(r, n, stride=d), :] for c in range(nc)]
        dst_ref[r] = jnp.concatenate(rows, axis=1).astype(dtype)


def _res_spec(a, tm):
    d, w = a.shape[0], a.shape[2]
    return pl.BlockSpec((d, tm // d, w), lambda i: (0, i, 0))


def _interleave(src_ref, scr_ref):
    d, n = src_ref.shape[0], src_ref.shape[1]
    nc = src_ref.shape[2] // 128
    for r in range(d):
        v = src_ref[r].astype(F32)
        for c in range(nc):
            scr_ref.at[c][pl.ds(r, n, stride=d), :] = v[:, c * 128:(c + 1) * 128]
    return jnp.concatenate([scr_ref[c] for c in range(nc)], axis=1)


def _with_dep(body, n_in, dep):
    if dep is None:
        return body
    return lambda *refs: body(*refs[:n_in], *refs[n_in + 1:])


def _dep_spec(dep):
    return [] if dep is None else [pl.BlockSpec(memory_space=pl.ANY)]


def _dep_arg(dep):
    return [] if dep is None else [dep]


def _inproj(x, g, w_p, tabs, tm, dep=None):
    t = x.shape[0]
    gw = 2 * PBLK
    (cos, sin), (cos1, sin1), (cos2, sin2) = tabs[0], tabs[1], tabs[2]

    def body(x_ref, g_ref, w_ref, c_ref, s_ref, c1_ref, s1_ref, c2_ref, s2_ref,
             h_ref, h1_ref, h2_ref, p0_ref, p1_ref, p2_ref, pb_ref, hf_ref):
        xt = x_ref[...]
        hf = xt * _rms(xt) * g_ref[...]
        h_ref[...] = hf.astype(BF16)
        _deinterleave(hf, hf_ref, h1_ref, BF16)
        _deinterleave(hf, hf_ref, h2_ref, BF16)
        rows = lambda ref: ref[...].reshape(tm, ref.shape[-1])
        groups = ((h_ref, c_ref, s_ref, p0_ref), (h1_ref, c1_ref, s1_ref, p1_ref), (h2_ref, c2_ref, s2_ref, p2_ref))
        for gi, (lhs_ref, cc_ref, ss_ref, out_ref) in enumerate(groups):
            lhs, cc, ss = rows(lhs_ref), rows(cc_ref), rows(ss_ref)
            for half in range(2):
                col = gi * gw + half * PBLK
                val = _rope(_dot(lhs, w_ref[:, col:col + PBLK]), cc, ss, False, 1).astype(BF16)
                if out_ref.ndim == 3:
                    out_ref[:, :, half * PBLK:(half + 1) * PBLK] = val.reshape(out_ref.shape[:2] + (PBLK,))
                else:
                    out_ref[:, half * PBLK:(half + 1) * PBLK] = val
        pb_ref[...] = _rope(_dot(h_ref[...], w_ref[:, 3 * gw:]), c_ref[...], s_ref[...], True, 1).astype(BF16)

    d1, d2 = DIL_GROUPS[1][1], DIL_GROUPS[2][1]
    row = lambda w: pl.BlockSpec((tm, w), lambda i: (i, 0))
    res = lambda d, w: pl.BlockSpec((d, tm // d, w), lambda i: (0, i, 0))
    sds = jax.ShapeDtypeStruct
    return pl.pallas_call(
        _with_dep(body, 9, dep), name="inproj", grid=(t // tm,),
        in_specs=[row(D_MODEL), pl.BlockSpec((1, D_MODEL), lambda i: (0, 0)),
                  pl.BlockSpec((D_MODEL, P_WIDTH), lambda i: (0, 0), pipeline_mode=pl.Buffered(1)),
                  row(128), row(128), res(d1, 128), res(d1, 128), res(d2, 128), res(d2, 128)] + _dep_spec(dep),
        out_specs=[row(D_MODEL), res(d1, D_MODEL), res(d2, D_MODEL), row(gw), res(d1, gw), res(d2, gw), row(PBLK)],
        out_shape=[sds((t, D_MODEL), BF16), sds((d1, t // d1, D_MODEL), BF16), sds((d2, t // d2, D_MODEL), BF16),
                   sds((t, gw), BF16), sds((d1, t // d1, gw), BF16), sds((d2, t // d2, gw), BF16),
                   sds((t, PBLK), BF16)],
        scratch_shapes=[_lane_scratch(tm, D_MODEL)],
        compiler_params=_params(("arbitrary",)),
    )(x, g, w_p, cos, sin, cos1, sin1, cos2, sin2, *_dep_arg(dep))


def _gates(h, w_g, b, tm, tn):
    t = h.shape[0]

    def body(h_ref, w_ref, b_ref, o_ref):
        z = _dot(h_ref[...], w_ref[...]) + b_ref[...]
        o_ref[...] = (0.5 * jnp.tanh(0.5 * z) + 0.5).astype(BF16)

    return pl.pallas_call(
        body, name="gates", grid=(t // tm, GATE_WIDTH // tn),
        in_specs=[pl.BlockSpec((tm, D_MODEL), lambda i, j: (i, 0)),
                  pl.BlockSpec((D_MODEL, tn), lambda i, j: (0, j)),
                  pl.BlockSpec((1, tn), lambda i, j: (0, j))],
        out_specs=pl.BlockSpec((tm, tn), lambda i, j: (i, j)),
        out_shape=jax.ShapeDtypeStruct((t, GATE_WIDTH), BF16),
        compiler_params=_params(("arbitrary", "arbitrary")),
    )(h, w_g, b)


def _band_mask(i, s):
    row = lax.broadcasted_iota(jnp.int32, (BAND, 2 * BAND), 0)
    col = lax.broadcasted_iota(jnp.int32, (BAND, 2 * BAND), 1)
    band = (col >= row) & (col <= row + BAND)
    if s == 0:
        band = band & ((col >= BAND) | (i > 0))
    return band


def _head_a_masks(rows):
    lane = lax.broadcasted_iota(jnp.int32, (rows, 128), 1)
    return (lane % HEAD_DIM) < HEAD_DIM // 2, lane < HEAD_DIM


def _stack_heads(x, head_a):
    zero = jnp.zeros_like(x)
    return jnp.concatenate([jnp.where(head_a, x, zero), jnp.where(head_a, zero, x)], axis=0)


def _stack_heads_t(xt, head_a_t):
    zero = jnp.zeros_like(xt)
    return jnp.concatenate([jnp.where(head_a_t, xt, zero), jnp.where(head_a_t, zero, xt)], axis=1)


def _kv_rows(cur_ref, tail_ref, s, off):
    if s == 0:
        return jnp.concatenate([tail_ref[:, off:off + 128], cur_ref[0:BAND, off:off + 128]], axis=0)
    return cur_ref[(s - 1) * BAND:(s + 1) * BAND, off:off + 128]


def _attn_layout(swa):
    if swa:
        return [(128 * j, 512, 640) for j in range(4)]
    return [(0, 128, 256), (384, 512, 640)]


def _attn_fwd(name, pv, swa, sinks, tq):
    d, ls = pv.shape[0], pv.shape[1]
    n, nsb = ls // tq, tq // BAND
    pairs = _attn_layout(swa)
    ncol = 1 if swa else 2
    ow = 128 * len(pairs)

    def body(cur_ref, tail_ref, *rest):
        sink_ref, o_ref, lse_ref, o32_ref = rest if swa else (None,) + rest + (None,)
        i = pl.program_id(2)
        lane = lax.broadcasted_iota(jnp.int32, (BAND, 128), 1)
        qk_a, v_a = _head_a_masks(BAND)
        first = lax.broadcasted_iota(jnp.int32, (2 * BAND, 1), 0) < BAND
        for s in range(nsb):
            mask = _band_mask(i, s)
            mask2 = jnp.concatenate([mask, mask], axis=0)
            rows = slice(s * BAND, (s + 1) * BAND)
            lse_tile = jnp.zeros((BAND, 128), F32)
            for j, (qo, ko, vo) in enumerate(pairs):
                q = cur_ref[rows, qo:qo + 128]
                kk = _kv_rows(cur_ref, tail_ref, s, ko)
                vv = _kv_rows(cur_ref, tail_ref, s, vo)
                sc = _dot_nt(_stack_heads(q, qk_a), kk)
                sc = jnp.where(mask2, sc, -jnp.inf)
                m = jnp.max(sc, axis=-1, keepdims=True)
                if swa:
                    sk = jnp.where(first, sink_ref[2 * j], sink_ref[2 * j + 1])
                    m = jnp.maximum(m, sk)
                p = jnp.exp(sc - m)
                den = jnp.sum(p, axis=-1, keepdims=True)
                if swa:
                    den = den + jnp.exp(sk - m)
                lse = m + jnp.log(den)
                lse_tile = jnp.where(lane == 2 * j, lse[:BAND], jnp.where(lane == 2 * j + 1, lse[BAND:], lse_tile))
                o2 = _dot(p.astype(BF16), vv) * (1.0 / den)
                o = jnp.where(v_a, o2[:BAND], o2[BAND:])
                o_ref[rows, j * 128:(j + 1) * 128] = o.astype(BF16)
                if swa:
                    o32_ref[rows, j * 128:(j + 1) * 128] = o
            lse_ref[rows, :] = lse_tile

    in_specs = [pl.BlockSpec((None, tq, PBLK), lambda r, cb, i: (r, i, cb)),
                pl.BlockSpec((None, BAND, PBLK), lambda r, cb, i: (r, jnp.maximum(i * nsb - 1, 0), cb))]
    args = [pv, pv]
    out_specs = [pl.BlockSpec((None, tq, ow), lambda r, cb, i: (r, i, cb)),
                 pl.BlockSpec((None, tq, 128), lambda r, cb, i: (r, i, cb))]
    out_shape = [jax.ShapeDtypeStruct((d, ls, 512), BF16), jax.ShapeDtypeStruct((d, ls, 128 * ncol), F32)]
    if swa:
        in_specs.append(pl.BlockSpec(memory_space=pltpu.SMEM))
        args.append(sinks)
        out_specs.append(out_specs[0])
        out_shape.append(jax.ShapeDtypeStruct((d, ls, 512), F32))
    return pl.pallas_call(
        body, name=name, grid=(d, ncol, n),
        in_specs=in_specs, out_specs=out_specs, out_shape=out_shape,
        compiler_params=_params(("arbitrary", "arbitrary", "arbitrary")),
    )(*args)


def _lse_lane(head):
    return (head // 4) * 128 + head % 4


def _dil_head_spread():
    lane = lax.broadcasted_iota(jnp.int32, (256, 512), 0)
    head = lax.broadcasted_iota(jnp.int32, (256, 512), 1) // HEAD_DIM
    return (lane == _lse_lane(head)).astype(BF16)


def _head_scale(x, tile, spread):
    return x * _dot(tile.astype(BF16), spread)


def _head_gather(width, dil):
    head = lax.broadcasted_iota(jnp.int32, (8 * HEAD_DIM, width), 0) // HEAD_DIM
    lane = lax.broadcasted_iota(jnp.int32, (8 * HEAD_DIM, width), 1)
    return (lane == (_lse_lane(head) if dil else head)).astype(BF16)


def _head_sums(x, gather):
    hi = x.astype(BF16)
    lo = (x - hi.astype(F32)).astype(BF16)
    return _dot(hi, gather) + _dot(lo, gather)


def _alphas(l0, l1, l2):
    m = jnp.maximum(jnp.maximum(l0, l1), l2)
    e0, e1, e2 = jnp.exp(l0 - m), jnp.exp(l1 - m), jnp.exp(l2 - m)
    den = e0 + e1 + e2
    return e0 / den, e1 / den, e2 / den


def _mix(o0, o1, o2, l0, l1, l2, ob, gts, x, w_a, w_b, w_out, g_cross, tm):
    t = x.shape[0]

    def body(o0_ref, o1_ref, o2_ref, l0_ref, l1_ref, l2_ref, ob_ref, g_ref, x_ref, wa_ref, wb_ref, wo_ref,
             gc_ref, oa_ref, ya_ref, yb_ref, mg_ref, x1_ref, hc_ref, so_ref, sl_ref):
        a0, a1, a2 = _alphas(l0_ref[...], _interleave(l1_ref, sl_ref), _interleave(l2_ref, sl_ref))
        spread = _dil_head_spread()
        oa = (_head_scale(o0_ref[...].astype(F32), a0, spread)
              + _head_scale(_interleave(o1_ref, so_ref), a1, spread)
              + _head_scale(_interleave(o2_ref, so_ref), a2, spread))
        oab = oa.astype(BF16)
        oa_ref[...] = oab
        ya = _dot(oab, wa_ref[...])
        yb = _dot(ob_ref[...], wb_ref[...])
        ya_ref[...] = ya.astype(BF16)
        yb_ref[...] = yb.astype(BF16)
        merged = (g_ref[:, :D_MODEL].astype(F32) * ya + g_ref[:, D_MODEL:].astype(F32) * yb).astype(BF16)
        mg_ref[...] = merged
        x1 = x_ref[...] + _dot(merged, wo_ref[...])
        x1_ref[...] = x1
        hc_ref[...] = (x1 * _rms(x1) * gc_ref[...]).astype(BF16)

    row = lambda w: pl.BlockSpec((tm, w), lambda i: (i, 0))
    full = lambda a, b: pl.BlockSpec((a, b), lambda i: (0, 0))
    return pl.pallas_call(
        body, name="mix", grid=(t // tm,),
        in_specs=[row(512), _res_spec(o1, tm), _res_spec(o2, tm), row(256), _res_spec(l1, tm), _res_spec(l2, tm),
                  row(512), row(GATE_WIDTH),
                  row(D_MODEL), full(512, D_MODEL), full(512, D_MODEL), full(D_MODEL, D_MODEL), full(1, D_MODEL)],
        out_specs=[row(512), row(D_MODEL), row(D_MODEL), row(D_MODEL), row(D_MODEL), row(D_MODEL)],
        out_shape=[jax.ShapeDtypeStruct((t, 512), BF16), jax.ShapeDtypeStruct((t, D_MODEL), BF16),
                   jax.ShapeDtypeStruct((t, D_MODEL), BF16), jax.ShapeDtypeStruct((t, D_MODEL), BF16),
                   jax.ShapeDtypeStruct((t, D_MODEL), F32), jax.ShapeDtypeStruct((t, D_MODEL), BF16)],
        scratch_shapes=[_lane_scratch(tm, 512), _lane_scratch(tm, 256)],
        compiler_params=_params(("arbitrary",)),
    )(o0, o1, o2, l0, l1, l2, ob, gts, x, w_a, w_b, w_out, g_cross)


def _memkv(mem, g_mem, w_ckv):
    m = mem.shape[0]
    ws = w_ckv.shape[2]

    def body(mem_ref, g_ref, w_ref, mn_ref, kv_ref):
        xt = mem_ref[...]
        mn = (xt * _rms(xt) * g_ref[...]).astype(BF16)
        mn_ref[...] = mn
        for j in range(N_DEV):
            kv_ref[:, j * ws:(j + 1) * ws] = _dot(mn, w_ref[j]).astype(BF16)

    return pl.pallas_call(
        body, name="memkv",
        out_shape=[jax.ShapeDtypeStruct((m, D_MODEL), BF16), jax.ShapeDtypeStruct((m, 2 * D_MODEL), BF16)],
        compiler_params=_params(),
    )(mem, g_mem, w_ckv)


def _cross_probs(q, kv_ref, h):
    k = kv_ref[:, h * X_HEAD_DIM:(h + 1) * X_HEAD_DIM]
    sc = _dot_nt(q[:, h * X_HEAD_DIM:(h + 1) * X_HEAD_DIM], k)
    m = jnp.max(sc, axis=-1, keepdims=True)
    p = jnp.exp(sc - m)
    return p / jnp.sum(p, axis=-1, keepdims=True)


def _cross(hc, x1, kv, w_cq, w_co, g_mlp, tm):
    t = x1.shape[0]
    m = kv.shape[0]

    def body(hc_ref, x1_ref, kv_ref, wq_ref, wo_ref, g_ref, q_ref, o_ref, x2_ref, hm_ref):
        q = (_dot(hc_ref[...], wq_ref[...]) * X_SCALE).astype(BF16)
        q_ref[...] = q
        outs = []
        for h in range(X_HEADS):
            p = _cross_probs(q, kv_ref, h)
            v = kv_ref[:, D_MODEL + h * X_HEAD_DIM:D_MODEL + (h + 1) * X_HEAD_DIM]
            outs.append(_dot(p.astype(BF16), v))
        o = jnp.concatenate(outs, axis=1).astype(BF16)
        o_ref[...] = o
        x2 = x1_ref[...] + _dot(o, wo_ref[...])
        x2_ref[...] = x2
        hm_ref[...] = (x2 * _rms(x2) * g_ref[...]).astype(BF16)

    row = lambda w: pl.BlockSpec((tm, w), lambda i: (i, 0))
    full = lambda a, b: pl.BlockSpec((a, b), lambda i: (0, 0))
    return pl.pallas_call(
        body, name="cross", grid=(t // tm,),
        in_specs=[row(D_MODEL), row(D_MODEL), full(m, 2 * D_MODEL), full(D_MODEL, D_MODEL),
                  full(D_MODEL, D_MODEL), full(1, D_MODEL)],
        out_specs=[row(D_MODEL)] * 4,
        out_shape=[jax.ShapeDtypeStruct((t, D_MODEL), BF16), jax.ShapeDtypeStruct((t, D_MODEL), BF16),
                   jax.ShapeDtypeStruct((t, D_MODEL), F32), jax.ShapeDtypeStruct((t, D_MODEL), BF16)],
        compiler_params=_params(("arbitrary",)),
    )(hc, x1, kv, w_cq, w_co, g_mlp)


def _mlp(hm, x2, w_1, w_2, g_final, target, tm, tf):
    t = x2.shape[0]
    nf = D_FF // tf

    def body(hm_ref, x2_ref, w1_ref, w2_ref, g_ref, tg_ref, a_ref, dx3_ref, loss_ref, dg_ref, acc_ref):
        i, f = pl.program_id(0), pl.program_id(1)
        hm_t = hm_ref[...]
        sw = w1_ref.shape[2]
        part = None
        for s in range(w1_ref.shape[0]):
            a = jnp.maximum(_dot(hm_t, w1_ref[s]), 0.0).astype(BF16)
            a_ref[:, s * sw:(s + 1) * sw] = a
            p_s = _dot(a * a, w2_ref[s * sw:(s + 1) * sw, :])
            part = p_s if part is None else part + p_s

        @pl.when(f == 0)
        def _():
            acc_ref[...] = part

        @pl.when(f > 0)
        def _():
            acc_ref[...] += part

        @pl.when((i == 0) & (f == 0))
        def _():
            loss_ref[...] = jnp.zeros_like(loss_ref)
            dg_ref[...] = jnp.zeros_like(dg_ref)

        @pl.when(f == nf - 1)
        def _():
            x3 = x2_ref[...] + acc_ref[...]
            r = _rms(x3)
            g = g_ref[...]
            diff = x3 * r * g - tg_ref[...]
            loss_ref[...] += 0.5 * jnp.sum(jnp.mean(diff * diff, axis=-1, keepdims=True))
            dx3, dg = _rms_bwd(diff / D_MODEL, x3, r, g)
            dx3_ref[...] = dx3
            dg_ref[...] += dg

    return pl.pallas_call(
        body, name="mlp", grid=(t // tm, nf),
        in_specs=[pl.BlockSpec((tm, D_MODEL), lambda i, f: (i, 0)),
                  pl.BlockSpec((tm, D_MODEL), lambda i, f: (i, 0)),
                  pl.BlockSpec((tf // w_1.shape[2], D_MODEL, w_1.shape[2]), lambda i, f: (f, 0, 0)),
                  pl.BlockSpec((tf, D_MODEL), lambda i, f: (f, 0)),
                  pl.BlockSpec((1, D_MODEL), lambda i, f: (0, 0)),
                  pl.BlockSpec((tm, D_MODEL), lambda i, f: (i, 0))],
        out_specs=[pl.BlockSpec((tm, tf), lambda i, f: (i, f)),
                   pl.BlockSpec((tm, D_MODEL), lambda i, f: (i, 0)),
                   pl.BlockSpec((1, 128), lambda i, f: (0, 0)),
                   pl.BlockSpec((1, D_MODEL), lambda i, f: (0, 0))],
        out_shape=[jax.ShapeDtypeStruct((t, D_FF), BF16), jax.ShapeDtypeStruct((t, D_MODEL), F32),
                   jax.ShapeDtypeStruct((1, 128), F32), jax.ShapeDtypeStruct((1, D_MODEL), F32)],
        scratch_shapes=[pltpu.VMEM((tm, D_MODEL), F32)],
        compiler_params=_params(("arbitrary", "arbitrary")),
    )(hm, x2, w_1, w_2, g_final, target)


def _mlp_bwd(dx3, a, w_1, w_2, x2, g_mlp, tm, tf):
    t = x2.shape[0]
    nf = D_FF // tf

    def body(dx3_ref, a_ref, w1_ref, w2_ref, x2_ref, g_ref, dz_ref, dx2_ref, dg_ref, acc_ref):
        i, f = pl.program_id(0), pl.program_id(1)
        da2 = _dot_nt(dx3_ref[...].astype(BF16), w2_ref[...])
        dz = (2.0 * a_ref[...].astype(F32) * da2).astype(BF16)
        dz_ref[...] = dz
        sw = w1_ref.shape[2]
        part = _dot_nt(dz[:, 0:sw], w1_ref[0])
        for s in range(1, w1_ref.shape[0]):
            part = part + _dot_nt(dz[:, s * sw:(s + 1) * sw], w1_ref[s])

        @pl.when(f == 0)
        def _():
            acc_ref[...] = part

        @pl.when(f > 0)
        def _():
            acc_ref[...] += part

        @pl.when((i == 0) & (f == 0))
        def _():
            dg_ref[...] = jnp.zeros_like(dg_ref)

        @pl.when(f == nf - 1)
        def _():
            xt = x2_ref[...]
            dx, dg = _rms_bwd(acc_ref[...], xt, _rms(xt), g_ref[...])
            dx2_ref[...] = dx3_ref[...] + dx
            dg_ref[...] += dg

    return pl.pallas_call(
        body, name="mlp_bwd", grid=(t // tm, nf),
        in_specs=[pl.BlockSpec((tm, D_MODEL), lambda i, f: (i, 0)),
                  pl.BlockSpec((tm, tf), lambda i, f: (i, f)),
                  pl.BlockSpec((tf // w_1.shape[2], D_MODEL, w_1.shape[2]), lambda i, f: (f, 0, 0)),
                  pl.BlockSpec((tf, D_MODEL), lambda i, f: (f, 0)),
                  pl.BlockSpec((tm, D_MODEL), lambda i, f: (i, 0)),
                  pl.BlockSpec((1, D_MODEL), lambda i, f: (0, 0))],
        out_specs=[pl.BlockSpec((tm, tf), lambda i, f: (i, f)),
                   pl.BlockSpec((tm, D_MODEL), lambda i, f: (i, 0)),
                   pl.BlockSpec((1, D_MODEL), lambda i, f: (0, 0))],
        out_shape=[jax.ShapeDtypeStruct((t, D_FF), BF16), jax.ShapeDtypeStruct((t, D_MODEL), F32),
                   jax.ShapeDtypeStruct((1, D_MODEL), F32)],
        scratch_shapes=[pltpu.VMEM((tm, D_MODEL), F32)],
        compiler_params=_params(("arbitrary", "arbitrary")),
    )(dx3, a, w_1, w_2, x2, g_mlp)


def _wgrad(name, a, b, tka, tn, tm, square=False, col_shards=False):
    t, ka = a.shape
    n = b.shape[1]
    nk = t // tm

    def body(a_ref, b_ref, o_ref, acc_ref):
        at = a_ref[...].astype(BF16)
        if square:
            at = at * at
        part = _dot_tn(at, b_ref[...].astype(BF16))
        k = pl.program_id(2)

        @pl.when(k == 0)
        def _():
            acc_ref[...] = part

        @pl.when(k > 0)
        def _():
            acc_ref[...] += part

        @pl.when(k == nk - 1)
        def _():
            if col_shards:
                for s in range(tn // sw):
                    o_ref[s] = acc_ref[:, s * sw:(s + 1) * sw].astype(BF16)
            else:
                o_ref[...] = acc_ref[...].astype(BF16)

    if col_shards:
        sw = n // N_DEV
        out_spec = pl.BlockSpec((tn // sw, tka, sw), lambda p, q, k: (q, p, 0))
        out_shape = jax.ShapeDtypeStruct((N_DEV, ka, sw), BF16)
    else:
        out_spec = pl.BlockSpec((tka, tn), lambda p, q, k: (p, q))
        out_shape = jax.ShapeDtypeStruct((ka, n), BF16)
    return pl.pallas_call(
        body, name=name, grid=(ka // tka, n // tn, nk),
        in_specs=[pl.BlockSpec((tm, tka), lambda p, q, k: (k, p)),
                  pl.BlockSpec((tm, tn), lambda p, q, k: (k, q))],
        out_specs=out_spec, out_shape=out_shape,
        scratch_shapes=[pltpu.VMEM((tka, tn), F32)],
        compiler_params=_params(("arbitrary", "arbitrary", "arbitrary")),
    )(a, b)


def _cross_bwd(dx2, x1, q, kv, w_cq, w_co, g_cross, tm, dep=None):
    t = x1.shape[0]
    m = kv.shape[0]

    def body(dx2_ref, x1_ref, q_ref, kv_ref, wq_ref, wo_ref, g_ref, dq_ref, dx1_ref, dkv_ref, dg_ref):
        @pl.when(pl.program_id(0) == 0)
        def _():
            dkv_ref[...] = jnp.zeros_like(dkv_ref)
            dg_ref[...] = jnp.zeros_like(dg_ref)

        do = _dot_nt(dx2_ref[...].astype(BF16), wo_ref[...]).astype(BF16)
        q = q_ref[...]
        dqs = []
        for h in range(X_HEADS):
            hs = slice(h * X_HEAD_DIM, (h + 1) * X_HEAD_DIM)
            vs = slice(D_MODEL + h * X_HEAD_DIM, D_MODEL + (h + 1) * X_HEAD_DIM)
            p = _cross_probs(q, kv_ref, h)
            dp = _dot_nt(do[:, hs], kv_ref[:, vs])
            ds = (p * (dp - jnp.sum(dp * p, axis=-1, keepdims=True))).astype(BF16)
            dqs.append(_dot(ds, kv_ref[:, hs]))
            dkv_ref[:, hs] += _dot_tn(ds, q[:, hs])
            dkv_ref[:, vs] += _dot_tn(p.astype(BF16), do[:, hs])
        dq = (jnp.concatenate(dqs, axis=1) * X_SCALE).astype(BF16)
        dq_ref[...] = dq
        xt = x1_ref[...]
        dx, dg = _rms_bwd(_dot_nt(dq, wq_ref[...]), xt, _rms(xt), g_ref[...])
        dx1_ref[...] = dx2_ref[...] + dx
        dg_ref[...] += dg

    row = lambda w: pl.BlockSpec((tm, w), lambda i: (i, 0))
    full = lambda a, b: pl.BlockSpec((a, b), lambda i: (0, 0))
    return pl.pallas_call(
        _with_dep(body, 7, dep), name="cross_bwd", grid=(t // tm,),
        in_specs=[row(D_MODEL), row(D_MODEL), row(D_MODEL), full(m, 2 * D_MODEL), full(D_MODEL, D_MODEL),
                  full(D_MODEL, D_MODEL), full(1, D_MODEL)] + _dep_spec(dep),
        out_specs=[row(D_MODEL), row(D_MODEL), full(m, 2 * D_MODEL), full(1, D_MODEL)],
        out_shape=[jax.ShapeDtypeStruct((t, D_MODEL), BF16), jax.ShapeDtypeStruct((t, D_MODEL), F32),
                   jax.ShapeDtypeStruct((m, 2 * D_MODEL), F32), jax.ShapeDtypeStruct((1, D_MODEL), F32)],
        compiler_params=_params(("arbitrary",)),
    )(dx2, x1, q, kv, w_cq, w_co, g_cross, *_dep_arg(dep))


def _memkv_bwd(dkv, mn, mem, w_ckv, g_mem):
    ws = w_ckv.shape[2]

    def body(dkv_ref, mn_ref, mem_ref, w_ref, g_ref, dw_ref, dg_ref):
        mn = mn_ref[...]
        dmn = jnp.zeros(mn.shape, F32)
        for j in range(N_DEV):
            dkvb = dkv_ref[:, j * ws:(j + 1) * ws].astype(BF16)
            dw_ref[j] = _dot_tn(mn, dkvb).astype(BF16)
            dmn = dmn + _dot_nt(dkvb, w_ref[j])
        xt = mem_ref[...]
        dg_ref[...] = jnp.sum(dmn * xt * _rms(xt), axis=0, keepdims=True)

    return pl.pallas_call(
        body, name="memkv_bwd",
        out_shape=[jax.ShapeDtypeStruct(w_ckv.shape, BF16), jax.ShapeDtypeStruct((1, D_MODEL), F32)],
        compiler_params=_params(),
    )(dkv, mn, mem, w_ckv, g_mem)


def _merge_bwd(dx1, ya, yb, gts, w_out, w_g, tm):
    t = dx1.shape[0]

    def body(dx1_ref, ya_ref, yb_ref, g_ref, wo_ref, wg_ref, dg_ref, dhp_ref, dya_ref, dyb_ref, db_ref):
        @pl.when(pl.program_id(0) == 0)
        def _():
            db_ref[...] = jnp.zeros_like(db_ref)

        dm = _dot_nt(dx1_ref[...].astype(BF16), wo_ref[...])
        ga = g_ref[:, :D_MODEL].astype(F32)
        gb = g_ref[:, D_MODEL:].astype(F32)
        dya_ref[...] = (dm * ga).astype(BF16)
        dyb_ref[...] = (dm * gb).astype(BF16)
        dpa = dm * ya_ref[...].astype(F32) * ga * (1.0 - ga)
        dpb = dm * yb_ref[...].astype(F32) * gb * (1.0 - gb)
        dpre = jnp.concatenate([dpa, dpb], axis=1)
        db_ref[...] += jnp.sum(dpre, axis=0, keepdims=True)
        dpreb = dpre.astype(BF16)
        dg_ref[...] = dpreb
        dhp_ref[...] = _dot_nt(dpreb, wg_ref[...])

    row = lambda w: pl.BlockSpec((tm, w), lambda i: (i, 0))
    once = lambda a, b: pl.BlockSpec((a, b), lambda i: (0, 0), pipeline_mode=pl.Buffered(1))
    sds = jax.ShapeDtypeStruct
    return pl.pallas_call(
        body, name="merge_bwd", grid=(t // tm,),
        in_specs=[row(D_MODEL), row(D_MODEL), row(D_MODEL), row(GATE_WIDTH),
                  once(D_MODEL, D_MODEL), once(D_MODEL, GATE_WIDTH)],
        out_specs=[row(GATE_WIDTH), row(D_MODEL), row(D_MODEL), row(D_MODEL),
                   pl.BlockSpec((1, GATE_WIDTH), lambda i: (0, 0))],
        out_shape=[sds((t, GATE_WIDTH), BF16), sds((t, D_MODEL), F32), sds((t, D_MODEL), BF16),
                   sds((t, D_MODEL), BF16), sds((1, GATE_WIDTH), F32)],
        compiler_params=_params(("arbitrary",)),
    )(dx1, ya, yb, gts, w_out, w_g)


def _combine_bwd(dya, dyb, oa, ob, l0, l1, l2, lb, sink_row, w_a, w_b, tm):
    t = dya.shape[0]

    def body(dya_ref, dyb_ref, oa_ref, ob_ref, l0_ref, l1_ref, l2_ref, lb_ref, sk_ref, wa_ref, wb_ref,
             do0_ref, do1_ref, do2_ref, c0_ref, c1_ref, c2_ref, dob_ref, cb_ref, dsk_ref, so_ref, sl_ref):
        @pl.when(pl.program_id(0) == 0)
        def _():
            dsk_ref[...] = jnp.zeros_like(dsk_ref)

        doa = _dot_nt(dya_ref[...], wa_ref[...])
        dob = _dot_nt(dyb_ref[...], wb_ref[...])
        dsum = _head_sums(doa * oa_ref[...].astype(F32), _head_gather(256, True))
        a0, a1, a2 = _alphas(l0_ref[...], _interleave(l1_ref, sl_ref), _interleave(l2_ref, sl_ref))
        c0_ref[...] = a0 * dsum
        spread = _dil_head_spread()
        do0_ref[...] = _head_scale(doa, a0, spread).astype(BF16)
        for al, do_ref, c_ref in ((a1, do1_ref, c1_ref), (a2, do2_ref, c2_ref)):
            _deinterleave(al * dsum, sl_ref, c_ref, F32)
            _deinterleave(_head_scale(doa, al, spread), so_ref, do_ref, BF16)
        dob_ref[...] = dob.astype(BF16)
        cb = _head_sums(dob * ob_ref[...], _head_gather(128, False))
        cb_ref[...] = cb
        lane = lax.broadcasted_iota(jnp.int32, cb.shape, 1)
        psink = jnp.where(lane < 8, jnp.exp(sk_ref[...] - lb_ref[...]), 0.0)
        dsk_ref[...] += jnp.sum(-psink * cb, axis=0, keepdims=True)

    row = lambda w: pl.BlockSpec((tm, w), lambda i: (i, 0))
    full = lambda a, b: pl.BlockSpec((a, b), lambda i: (0, 0))
    sds = jax.ShapeDtypeStruct
    d1, d2 = l1.shape[0], l2.shape[0]
    res = lambda d, w: pl.BlockSpec((d, tm // d, w), lambda i: (0, i, 0))
    return pl.pallas_call(
        body, name="combine_bwd", grid=(t // tm,),
        in_specs=[row(D_MODEL), row(D_MODEL), row(512), row(512),
                  row(256), _res_spec(l1, tm), _res_spec(l2, tm), row(128), full(1, 128),
                  full(512, D_MODEL), full(512, D_MODEL)],
        out_specs=[row(512), res(d1, 512), res(d2, 512), row(256), res(d1, 256), res(d2, 256),
                   row(512), row(128), full(1, 128)],
        out_shape=[sds((t, 512), BF16), sds((d1, t // d1, 512), BF16),
                   sds((d2, t // d2, 512), BF16), sds((t, 256), F32), sds((d1, t // d1, 256), F32),
                   sds((d2, t // d2, 256), F32), sds((t, 512), BF16),
                   sds((t, 128), F32), sds((1, 128), F32)],
        scratch_shapes=[_lane_scratch(tm, 512), _lane_scratch(tm, 256)],
        compiler_params=_params(("arbitrary",)),
    )(dya, dyb, oa, ob, l0, l1, l2, lb, sink_row, w_a, w_b)


def _attn_bwd(name, pv, dov, lsev, cv, cosv, sinv, swa, tq, dep=None):
    d, ls = pv.shape[0], pv.shape[1]
    n, nsb = ls // tq, tq // BAND
    pairs = _attn_layout(swa)
    ncol = 1 if swa else 2
    ow = 128 * len(pairs)

    kv_slots = sorted({(ko, vo) for _, ko, vo in pairs})

    def body(cur_ref, tail_ref, do_ref, lse_ref, c_ref, cos_ref, sin_ref, out_ref, acc_ref, carry_ref, acct_ref):
        i = pl.program_id(2)
        blk_i = n - 1 - i
        acc_ref[...] = jnp.zeros_like(acc_ref)
        acct_ref[...] = jnp.zeros_like(acct_ref)

        @pl.when(i == 0)
        def _():
            carry_ref[...] = jnp.zeros_like(carry_ref)

        qk_a, v_a = _head_a_masks(BAND)
        dim = lax.broadcasted_iota(jnp.int32, (128, BAND), 0)
        qk_at, v_at = (dim % HEAD_DIM) < HEAD_DIM // 2, dim < HEAD_DIM
        for s in range(nsb):
            mask = _band_mask(blk_i, s)
            mask2 = jnp.concatenate([mask, mask], axis=0)
            rows = slice(s * BAND, (s + 1) * BAND)
            kcols = slice(s * BAND, (s + 2) * BAND)
            for j, (qo, ko, vo) in enumerate(pairs):
                slot = kv_slots.index((ko, vo))
                kk = _kv_rows(cur_ref, tail_ref, s, ko)
                vv = _kv_rows(cur_ref, tail_ref, s, vo)
                q, do = cur_ref[rows, qo:qo + 128], do_ref[rows, j * 128:(j + 1) * 128]
                q2, do2 = _stack_heads(q, qk_a), _stack_heads(do, v_a)
                col2 = lambda ref: jnp.concatenate([ref[rows, 2 * j:2 * j + 1], ref[rows, 2 * j + 1:2 * j + 2]], axis=0)
                sc = _dot_nt(q2, kk)
                p = jnp.exp(jnp.where(mask2, sc, -jnp.inf) - col2(lse_ref))
                dp = _dot_nt(do2, vv)
                ds = (p * (dp - col2(c_ref))).astype(BF16)
                dq2 = _dot(ds, kk)
                acc_ref[BAND + s * BAND:BAND + (s + 1) * BAND, qo:qo + 128] += jnp.where(qk_a, dq2[:BAND], dq2[BAND:])
                acct_ref[2 * slot, :, kcols] += _dot(_stack_heads_t(q.T, qk_at), ds)
                acct_ref[2 * slot + 1, :, kcols] += _dot(_stack_heads_t(do.T, v_at), p.astype(BF16))
        for slot, (ko, vo) in enumerate(kv_slots):
            acc_ref[:, ko:ko + 128] += acct_ref[2 * slot].T
            acc_ref[:, vo:vo + 128] += acct_ref[2 * slot + 1].T

        last = acc_ref[tq:, :] + carry_ref[...]
        fin = last if tq == BAND else jnp.concatenate([acc_ref[BAND:tq, :], last], axis=0)
        out_ref[...] = _rope(fin, cos_ref[...], sin_ref[...], swa, -1).astype(BF16)
        carry_ref[...] = acc_ref[0:BAND, :]

    rev = lambda i: n - 1 - i
    blk = lambda rows, w, row_of: pl.BlockSpec((None, rows, w), lambda r, cb, i: (r, row_of(i), cb))
    tab = pl.BlockSpec((None, tq, 128), lambda r, cb, i: (r, rev(i), 0))
    return pl.pallas_call(
        _with_dep(body, 7, dep), name=name, grid=(d, ncol, n),
        in_specs=[blk(tq, PBLK, rev), blk(BAND, PBLK, lambda i: jnp.maximum(rev(i) * nsb - 1, 0)),
                  blk(tq, ow, rev), blk(tq, 128, rev), blk(tq, 128, rev), tab, tab] + _dep_spec(dep),
        out_specs=blk(tq, PBLK, rev),
        out_shape=jax.ShapeDtypeStruct((d, ls, ncol * PBLK), BF16),
        scratch_shapes=[pltpu.VMEM((tq + BAND, PBLK), F32), pltpu.VMEM((BAND, PBLK), F32),
                        pltpu.VMEM((2 * len(kv_slots), 128, tq + BAND), F32)],
        compiler_params=_params(("arbitrary", "arbitrary", "arbitrary")),
    )(pv, pv, dov, lsev, cv, cosv, sinv, *_dep_arg(dep))


def _dx(dp0, dp1, dp2, dpb, w_p, dh_part, dx1, x, g_mix, tm, dep=None):
    t = x.shape[0]
    gw = 2 * PBLK

    def body(dp0_ref, dp1_ref, dp2_ref, dpb_ref, w_ref, dhp_ref, dx1_ref, x_ref, g_ref, gx_ref, dg_ref,
             dpt_ref, scr_ref):
        @pl.when(pl.program_id(0) == 0)
        def _():
            dg_ref[...] = jnp.zeros_like(dg_ref)

        dpt_ref[:, 0:gw] = dp0_ref[...]
        dpt_ref[:, gw:2 * gw] = _interleave(dp1_ref, scr_ref).astype(BF16)
        dpt_ref[:, 2 * gw:3 * gw] = _interleave(dp2_ref, scr_ref).astype(BF16)
        dpt_ref[:, 3 * gw:] = dpb_ref[...]
        dh = _dot_nt(dpt_ref[...], w_ref[...]) + dhp_ref[...]
        xt = x_ref[...]
        dx, dg = _rms_bwd(dh, xt, _rms(xt), g_ref[...])
        gx_ref[...] = dx1_ref[...] + dx
        dg_ref[...] += dg

    row = lambda w: pl.BlockSpec((tm, w), lambda i: (i, 0))
    full = lambda a, b: pl.BlockSpec((a, b), lambda i: (0, 0))
    return pl.pallas_call(
        _with_dep(body, 9, dep), name="dx", grid=(t // tm,),
        in_specs=[row(gw), _res_spec(dp1, tm), _res_spec(dp2, tm), row(PBLK),
                  pl.BlockSpec((D_MODEL, P_WIDTH), lambda i: (0, 0), pipeline_mode=pl.Buffered(1)),
                  row(D_MODEL), row(D_MODEL), row(D_MODEL), full(1, D_MODEL)] + _dep_spec(dep),
        out_specs=[row(D_MODEL), full(1, D_MODEL)],
        out_shape=[jax.ShapeDtypeStruct((t, D_MODEL), F32), jax.ShapeDtypeStruct((1, D_MODEL), F32)],
        scratch_shapes=[pltpu.VMEM((tm, P_WIDTH), BF16), _lane_scratch(tm, gw)],
        compiler_params=_params(("arbitrary",)),
    )(dp0, dp1, dp2, dpb, w_p, dh_part, dx1, x, g_mix, *_dep_arg(dep))


MESH = pl.DeviceIdType.MESH
HBM_SPEC = pl.BlockSpec(memory_space=pltpu.HBM)
VMEM_SPEC = pl.BlockSpec(memory_space=pltpu.VMEM)


def _all_gather(xp):
    def body(x_ref, out_ref, send_sems, recv_sems, local_sem):
        x, y, c = lax.axis_index("x"), lax.axis_index("y"), lax.axis_index("c")
        me, sibling = (x, y, c), (x, y, 1 - c)
        chips = [(1 - x, y), (x, 1 - y), (1 - x, 1 - y)]

        def rows(px, py, pc):
            return out_ref.at[4 * px + 2 * py + pc]

        def copy(k, block, to, src=None):
            return pltpu.make_async_remote_copy(
                src_ref=rows(*block) if src is None else src, dst_ref=rows(*block),
                send_sem=send_sems.at[k], recv_sem=recv_sems.at[k], device_id=to, device_id_type=MESH)

        mine = pltpu.make_async_copy(x_ref, rows(*me), local_sem)
        mine.start()
        first = [copy(0, me, sibling, src=x_ref)]
        first += [copy(1 + j, me, (*chip, c), src=x_ref) for j, chip in enumerate(chips)]
        for cp in first:
            cp.start()
        passed = [copy(4 + j, (*chip, c), sibling) for j, chip in enumerate(chips)]
        for j, chip in enumerate(chips):
            copy(1 + j, (*chip, c), me).wait_recv()
            passed[j].start()
        copy(0, sibling, me).wait_recv()
        for j, chip in enumerate(chips):
            copy(4 + j, (*chip, 1 - c), me).wait_recv()
        for cp in first + passed:
            cp.wait_send()
        mine.wait()

    return pl.pallas_call(
        body, name="all_gather",
        out_shape=jax.ShapeDtypeStruct((N_DEV,) + xp.shape, xp.dtype),
        in_specs=[HBM_SPEC], out_specs=HBM_SPEC,
        scratch_shapes=[pltpu.SemaphoreType.DMA((7,)), pltpu.SemaphoreType.DMA((7,)), pltpu.SemaphoreType.DMA],
    )(xp)


def _peers():
    x, y, c = lax.axis_index("x"), lax.axis_index("y"), lax.axis_index("c")
    out = []
    for k in range(1, N_DEV):
        px = 1 - x if k & 4 else x
        py = 1 - y if k & 2 else y
        pc = 1 - c if k & 1 else c
        out.append((k, (px, py, pc), 4 * px + 2 * py + pc))
    return out


def _my_index():
    return 4 * lax.axis_index("x") + 2 * lax.axis_index("y") + lax.axis_index("c")


SEM_SPEC = pl.BlockSpec(memory_space=pltpu.SEMAPHORE)
ANY_SPEC = pl.BlockSpec(memory_space=pl.ANY)
_SPLIT_PARAMS = pltpu.CompilerParams(has_side_effects=pltpu.SideEffectType.DATAFLOW_SIDE_EFFECTING)


def _split_copies(gather, src_refs, land_refs, send_sems, recv_sems):
    me_idx = _my_index()
    out = []
    for a, (src_ref, land_ref) in enumerate(zip(src_refs, land_refs)):
        for k, peer, peer_idx in _peers():
            if gather:
                src, dst = src_ref, land_ref.at[me_idx]
            else:
                src, dst = src_ref.at[peer_idx], land_ref.at[k - 1]
            out.append(pltpu.make_async_remote_copy(
                src_ref=src, dst_ref=dst, send_sem=send_sems.at[7 * a + k - 1], recv_sem=recv_sems.at[7 * a + k - 1],
                device_id=peer, device_id_type=MESH))
    return out


def _split_start(name, gather, srcs):
    n = len(srcs)

    def body(*refs):
        send_sems, recv_sems = refs[n], refs[n + 1]
        for cp in _split_copies(gather, refs[:n], refs[2 * n + 2:3 * n + 2], send_sems, recv_sems):
            cp.start()
        token = refs[-1]
        token[...] = jnp.zeros_like(token)

    lands = [pltpu.HBM((N_DEV,) + a.shape if gather else (N_DEV - 1,) + a.shape[1:], a.dtype) for a in srcs]
    return pl.pallas_call(
        body, name=name,
        out_shape=(pltpu.SemaphoreType.DMA((7 * n,)), pltpu.SemaphoreType.DMA((7 * n,)),
                   *[pltpu.HBM(a.shape, a.dtype) for a in srcs], *lands, jax.ShapeDtypeStruct((8, 128), F32)),
        in_specs=(HBM_SPEC,) * n, out_specs=(SEM_SPEC, SEM_SPEC) + (HBM_SPEC,) * (2 * n) + (VMEM_SPEC,),
        input_output_aliases={i: 2 + i for i in range(n)}, compiler_params=_SPLIT_PARAMS,
    )(*[pltpu.with_memory_space_constraint(a, pltpu.HBM) for a in srcs])


def _split_wait(name, gather, started, after):
    send_sems, recv_sems, bufs = started[0], started[1], started[2:-1]
    n = len(bufs) // 2

    def body(*refs):
        for cp in _split_copies(gather, refs[:n], refs[n:2 * n], refs[2 * n], refs[2 * n + 1]):
            cp.wait_send()
            cp.wait_recv()

    out = pl.pallas_call(
        body, name=name, out_shape=tuple(pltpu.HBM(a.shape, a.dtype) for a in bufs),
        in_specs=(HBM_SPEC,) * (2 * n) + (SEM_SPEC, SEM_SPEC, ANY_SPEC), out_specs=(HBM_SPEC,) * (2 * n),
        input_output_aliases={i: i for i in range(2 * n)}, compiler_params=_SPLIT_PARAMS,
    )(*bufs, send_sems, recv_sems, after)
    return out[:n], out[n:]


def _adam_update(g, w, m, v):
    nm = ADAM_B1 * m + (1.0 - ADAM_B1) * g
    nv = ADAM_B2 * v + (1.0 - ADAM_B2) * (g * g)
    m_hat = nm / (1.0 - ADAM_B1 ** ADAM_STEP)
    v_hat = nv / (1.0 - ADAM_B2 ** ADAM_STEP)
    return -ADAM_LR * (m_hat / (jnp.sqrt(v_hat) + ADAM_EPS) + ADAM_WD * w), nm, nv


def _adamw(name, me, sent, got, w, m, v, tr):
    r, c = w.shape

    def body(me_ref, own_ref, got_ref, w_ref, m_ref, v_ref, g_ref, d_ref, nm_ref, nv_ref):
        g = own_ref[...].astype(F32)
        for k in range(N_DEV - 1):
            g = g + got_ref[k].astype(F32)
        g_ref[...] = g
        d_ref[...], nm_ref[...], nv_ref[...] = _adam_update(g, w_ref[...], m_ref[...], v_ref[...])

    blk = pl.BlockSpec((tr, c), lambda i, me_ref: (i, 0))
    return pl.pallas_call(
        body, name=name,
        grid_spec=pltpu.PrefetchScalarGridSpec(
            num_scalar_prefetch=1, grid=(r // tr,),
            in_specs=[pl.BlockSpec((None, tr, c), lambda i, me_ref: (me_ref[0], i, 0)),
                      pl.BlockSpec((N_DEV - 1, tr, c), lambda i, me_ref: (0, i, 0)), blk, blk, blk],
            out_specs=[blk] * 4),
        out_shape=[jax.ShapeDtypeStruct((r, c), F32)] * 4,
        compiler_params=_params(("arbitrary",)),
    )(me, sent, got, w, m, v)


def _adamw_small(srecv, ws, ms, vs):
    nv_ = len(ws)

    def body(*refs):
        s_ref = refs[0]
        ins, outs = refs[1:1 + 3 * nv_], refs[1 + 3 * nv_:]
        g_all = s_ref[0]
        for k in range(1, N_DEV):
            g_all = g_all + s_ref[k]
        for i in range(nv_):
            n = ins[i].shape[1]
            g = g_all[i:i + 1, :n]
            d, nm, nv = _adam_update(g, ins[i][...], ins[nv_ + i][...], ins[2 * nv_ + i][...])
            outs[i][...], outs[nv_ + i][...], outs[2 * nv_ + i][...], outs[3 * nv_ + i][...] = g, d, nm, nv
        outs[-1][...] = g_all[nv_:nv_ + 1, :128]

    shapes = [jax.ShapeDtypeStruct(a.shape, F32) for a in ws]
    res = pl.pallas_call(body, name="adamw_small", out_shape=shapes * 4 + [jax.ShapeDtypeStruct((1, 128), F32)],
                         compiler_params=_params())(srecv, *ws, *ms, *vs)
    return [res[k * nv_:(k + 1) * nv_] for k in range(4)], res[-1]


def _cols_from_shards(a):
    return jnp.swapaxes(a, 0, 1).reshape(a.shape[1], a.shape[0] * a.shape[2])


def _shards_from_cols(a):
    return jnp.swapaxes(a.reshape(a.shape[0], N_DEV, a.shape[1] // N_DEV), 0, 1)


def _shards_from_rows(a):
    return a.reshape(N_DEV, a.shape[0] // N_DEV, a.shape[1])


def _pair_lanes(a):
    lead = a.shape[:-1]
    return a.reshape(lead + (2, 2, HEAD_DIM // 2)).swapaxes(-3, -2).reshape(lead + (128,))


def _split_w_in(w_in):
    rows = w_in.shape[0]
    dil = w_in[:, :3 * DIL_WIDTH].reshape(rows, 3, 3, 4, 128)
    dil = jnp.concatenate([_pair_lanes(dil[:, :2]), dil[:, 2:]], axis=1)
    dil = dil.transpose(0, 2, 3, 1, 4).reshape(rows, 3 * DIL_WIDTH)
    o = 3 * DIL_WIDTH
    qb = w_in[:, o:o + SWA_Q_WIDTH].reshape(rows, 2, 4, HEAD_DIM).transpose(0, 2, 1, 3).reshape(rows, 4, 128)
    qb = _pair_lanes(qb).reshape(rows, SWA_Q_WIDTH)
    kb = _pair_lanes(w_in[:, o + SWA_Q_WIDTH:o + SWA_Q_WIDTH + SWA_KV_WIDTH])
    vb = w_in[:, o + SWA_Q_WIDTH + SWA_KV_WIDTH:P_WIDTH]
    return jnp.concatenate([dil, qb, kb, vb], axis=1)


def _merge_w_in(dw_p, dw_g):
    rows = dw_p.shape[0]
    dil = dw_p[:, :3 * DIL_WIDTH].reshape(rows, 3, 4, 3, 128).transpose(0, 3, 1, 2, 4)
    dil = jnp.concatenate([_pair_lanes(dil[:, :2]), dil[:, 2:]], axis=1).reshape(rows, 3 * DIL_WIDTH)
    o = 3 * DIL_WIDTH
    qb = _pair_lanes(dw_p[:, o:o + SWA_Q_WIDTH].reshape(rows, 4, 128))
    qb = qb.reshape(rows, 4, 2, HEAD_DIM).transpose(0, 2, 1, 3).reshape(rows, SWA_Q_WIDTH)
    kb = _pair_lanes(dw_p[:, o + SWA_Q_WIDTH:o + SWA_Q_WIDTH + SWA_KV_WIDTH])
    vb = dw_p[:, o + SWA_Q_WIDTH + SWA_KV_WIDTH:]
    return jnp.concatenate([dil, qb, kb, vb, dw_g], axis=1)


def _swa_rows(w_b):
    return w_b.reshape(2, 4, HEAD_DIM, -1).transpose(1, 0, 2, 3).reshape(SWA_Q_WIDTH, -1)


def _swa_rows_inv(dw_b):
    return dw_b.reshape(4, 2, HEAD_DIM, -1).transpose(1, 0, 2, 3).reshape(SWA_Q_WIDTH, -1)


def _rope_tables(pos):
    half = HEAD_DIM // 2
    inv = ROPE_THETA ** (-jnp.arange(half, dtype=F32) / half)
    ang = pos.astype(F32)[:, None] * jnp.tile(inv, 4)
    sign = jnp.repeat(jnp.array([-1.0, 1.0], F32), 2 * half)
    return jnp.cos(ang), jnp.sin(ang) * sign


def _local_step(x, mem, pos, target, w_in, dep, rest_weights, on_grads, g_mix, g_cross, g_mem, g_mlp, g_final, sink):
    t = x.shape[0]
    tm = min(512, t)
    tq = 1024
    tw = min(2048, t)
    w_p = _split_w_in(w_in)
    cos, sin = lax.optimization_barrier(_rope_tables(pos))
    sink_row = jnp.pad(sink.reshape(2, 4).T.reshape(1, 8), ((0, 0), (0, 120)))
    tabs = [(cos[None], sin[None])]
    for _, d in DIL_GROUPS[1:]:
        tabs.append(tuple(a.reshape(t // d, d, 128).swapaxes(0, 1) for a in (cos, sin)))
    tabs.append(tabs[0])

    h, h1, h2, p0, p1, p2, pb = _inproj(x, g_mix, w_p, [(cos, sin), tabs[1], tabs[2]], tm, dep)
    ps = [p0[None], p1, p2, pb[None]]
    outs, lses = [], []
    for gi, pv in enumerate(ps):
        res = _attn_fwd(f"attn_fwd{gi}", pv, gi == 3, sink_row[0, :8], min(tq, pv.shape[1]))
        outs.append(res[0])
        lses.append(res[1])
    o0, l0, ob, lb, ob32 = outs[0][0], lses[0][0], outs[3][0], lses[3][0], res[2][0]
    wts = rest_weights(lb)
    w_b = _swa_rows(wts["w_branch_b"])
    tf = 2048
    w_g = wts["w_g"]
    gts = _gates(h, w_g, wts["b_gate"].reshape(1, GATE_WIDTH), tm, 1024)
    oa, ya, yb, merged, x1, hc = _mix(o0, outs[1], outs[2], l0, lses[1], lses[2], ob, gts, x,
                                      wts["w_branch_a"], w_b, wts["w_out"], g_cross, tm)
    mn, kv = _memkv(mem, g_mem, wts["w_ckv"])
    q, o, x2, hm = _cross(hc, x1, kv, wts["w_cq"], wts["w_co"], g_mlp, tm)
    a, dx3, loss, dg_final = _mlp(hm, x2, wts["w_1"], wts["w_2"], g_final.reshape(1, D_MODEL), target, tm, tf)

    grads = {}
    dz, dx2, dg_mlp = _mlp_bwd(dx3, a, wts["w_1"], wts["w_2"], x2, g_mlp, tm, tf)
    grads["w_2"] = _shards_from_rows(_wgrad("dw_2", a, dx3, 1024, 1024, tw, square=True))
    grads["w_1"] = _wgrad("dw_1", hm, dz, 1024, 1024, tw, col_shards=True)
    dep = on_grads(GROUP_A, grads)
    dq, dx1, dkv, dg_cross = _cross_bwd(dx2, x1, q, kv, wts["w_cq"], wts["w_co"], g_cross, tm, dep)
    grads["w_co"] = _shards_from_rows(_wgrad("dw_co", o, dx2, 1024, 1024, tw))
    grads["w_cq"] = _shards_from_rows(_wgrad("dw_cq", hc, dq, 1024, 1024, tw))
    grads["w_ckv"], dg_mem = _memkv_bwd(dkv, mn, mem, wts["w_ckv"], g_mem)
    dgt, dh_part, dya, dyb, db_gate = _merge_bwd(dx1, ya, yb, gts, wts["w_out"], w_g, tm)
    do0, do1, do2, c0, c1, c2, dob, cb, dsink = _combine_bwd(
        dya, dyb, oa, ob32, l0, lses[1], lses[2], lb, sink_row, wts["w_branch_a"], w_b, tm)
    grads["w_out"] = _shards_from_rows(_wgrad("dw_out", merged, dx1, 1024, 1024, tw))
    grads["w_branch_a"] = _shards_from_cols(_wgrad("dw_a", oa, dya, 512, 1024, tw))
    grads["w_branch_b"] = _shards_from_cols(_swa_rows_inv(_wgrad("dw_b", ob, dyb, 512, 1024, tw)))
    grads["b_gate"] = _shards_from_cols(db_gate.reshape(2, D_MODEL)).astype(BF16)
    dep = on_grads(GROUP_B, grads)
    dw_g = _wgrad("dw_g", h, dgt, 1024, 1024, tw)
    dps = []
    for gi, (pv, do_g, c_g) in enumerate(zip(ps, (do0[None], do1, do2, dob[None]), (c0[None], c1, c2, cb[None]))):
        dps.append(_attn_bwd(f"attn_bwd{gi}", pv, do_g, lses[gi], c_g, tabs[gi][0], tabs[gi][1], gi == 3,
                             min(tq, pv.shape[1]),
                             dep if gi == 0 else None))
    dw_p = jnp.concatenate(
        [_wgrad(f"dw_p{gi}", hh.reshape(t, D_MODEL), dpg.reshape(t, -1), 1024, PBLK, tw)
         for gi, (hh, dpg) in enumerate(zip((h, h1, h2, h), dps))], axis=1)
    grads["w_in"] = _shards_from_cols(_merge_w_in(dw_p, dw_g))
    dep = on_grads(GROUP_C, grads)
    grad_x, dg_mix = _dx(dps[0][0], dps[1], dps[2], dps[3][0], w_p, dh_part, dx1, x, g_mix, tm, dep)
    dsink_heads = dsink[0, :8].reshape(4, 2).T.reshape(8)
    small = {"g_mix": dg_mix[0], "g_cross": dg_cross[0], "g_mem": dg_mem[0], "g_mlp": dg_mlp[0],
             "g_final": dg_final[0], "sink": dsink_heads}
    return loss[0, 0], grad_x, small


def kernel(x, mem, positions, g_mix, w_in, b_gate, sink, w_branch_a, w_branch_b, w_out, g_cross, g_mem, w_cq, w_ckv, w_co, g_mlp, w_1, w_2, g_final, loss_target, m_g_mix, m_w_in, m_b_gate, m_sink, m_w_branch_a, m_w_branch_b, m_w_out, m_g_cross, m_g_mem, m_w_cq, m_w_ckv, m_w_co, m_g_mlp, m_w_1, m_w_2, m_g_final, v_g_mix, v_w_in, v_b_gate, v_sink, v_w_branch_a, v_w_branch_b, v_w_out, v_g_cross, v_g_mem, v_w_cq, v_w_ckv, v_w_co, v_g_mlp, v_w_1, v_w_2, v_g_final):
    local = dict(locals())
    shard = {n: local[n][0] for n in GROUP_A + GROUP_B + GROUP_C}
    me = _my_index()
    me_arr = me.reshape(1).astype(jnp.int32)
    tags = {GROUP_A: "a", GROUP_B: "b", GROUP_C: "c"}

    w_in_full = _cols_from_shards(_all_gather(shard["w_in"].astype(BF16)))
    rest = GROUP_A + GROUP_B

    def gathered(name, started, after):
        srcs, lands = _split_wait(name, True, started, after)
        return [lax.dynamic_update_slice(land, src[None], (me,) + (0,) * src.ndim) for src, land in zip(srcs, lands)]

    gather = _split_start("gather_start", True,
                          [shard[n] if n == "b_gate" else shard[n].astype(BF16) for n in rest])

    def rest_weights(after):
        full = {"w_g": w_in_full[:, P_WIDTH:]}
        for name, a in zip(rest, gathered("gather_wait", gather, after)):
            if name in ("w_1", "w_ckv"):
                full[name] = a
            elif name in _COL_SHARDED:
                full[name] = _cols_from_shards(a)
            else:
                full[name] = a.reshape(N_DEV * a.shape[1], a.shape[2])
        return full

    scatters = {}

    def on_grads(names, grads):
        scatters[names] = _split_start("scatter_start_" + tags[names], False, [grads[n] for n in names])
        return scatters[names][-1]

    loss, grad_x, small = _local_step(
        x[0], mem[0], positions[0], loss_target[0], w_in_full, gather[-1], rest_weights, on_grads,
        g_mix, g_cross, g_mem, g_mlp, g_final, sink[0])

    sp = jnp.stack([small[n] if n != "sink" else jnp.pad(small[n], (0, LANES - 8)) for n in SMALL]
                   + [jnp.pad(loss.reshape(1), (0, LANES - 1)), jnp.zeros((LANES,), F32)])
    small_gather = _split_start("small_start", True, [sp])

    after, updated = small_gather[-1], {}
    for names in (GROUP_A, GROUP_B, GROUP_C):
        sent, got = _split_wait("scatter_wait_" + tags[names], False, scatters[names], after)
        for i, name in enumerate(names):
            outs = _adamw("adamw_" + name, me_arr, sent[i], got[i], shard[name],
                          local["m_" + name][0], local["v_" + name][0], ADAM_ROWS[name])
            updated[name] = [a[None] for a in outs]
            after = outs[3]

    flat = lambda prefix: [local[prefix + n].reshape(1, -1) for n in SMALL]
    outs, loss_row = _adamw_small(gathered("small_wait", small_gather, after)[0], flat(""), flat("m_"), flat("v_"))
    for i, name in enumerate(SMALL):
        updated[name] = [outs[which][i].reshape(local[name].shape) for which in range(4)]

    order = ["g_mix", "w_in", "b_gate", "sink", "w_branch_a", "w_branch_b", "w_out", "g_cross", "g_mem", "w_cq",
             "w_ckv", "w_co", "g_mlp", "w_1", "w_2", "g_final"]
    res = [loss_row[0, 0], grad_x[None]]
    for which in range(4):
        res += [updated[n][which] for n in order]
    return tuple(res)
```

```python
import functools
import math

import jax
import jax.numpy as jnp
from jax import lax
from jax.experimental import pallas as pl
from jax.experimental.pallas import tpu as pltpu

F32 = jnp.float32
BF16 = jnp.bfloat16

D_MODEL = 1024
HEAD_DIM = 64
DIL_GROUPS = ((128, 1), (512, 4), (2048, 16))
ROPE_THETA = 10000.0
X_HEADS = 4
X_HEAD_DIM = D_MODEL // X_HEADS
D_FF = 4 * D_MODEL
EPS = 1e-6
DIL_WIDTH = 1536
SWA_Q_WIDTH = 512
SWA_KV_WIDTH = 128
P_WIDTH = 3 * DIL_WIDTH + SWA_Q_WIDTH + 2 * SWA_KV_WIDTH
GATE_WIDTH = 2 * D_MODEL
IN_WIDTH = P_WIDTH + GATE_WIDTH
BAND = 128
PBLK = 768
Q_SCALE = HEAD_DIM ** -0.5
X_SCALE = X_HEAD_DIM ** -0.5

ADAM_LR = 0.001
ADAM_B1 = 0.9
ADAM_B2 = 0.999
ADAM_EPS = 1e-08
ADAM_WD = 0.01
ADAM_STEP = 10

N_DEV = 8
LANES = 1024
VMEM_LIMIT = 52 * 1024 * 1024

NT = (((1,), (1,)), ((), ()))
TN = (((0,), (0,)), ((), ()))

GROUP_A = ("w_1", "w_2")
GROUP_B = ("w_branch_a", "w_branch_b", "w_out", "w_cq", "w_ckv", "w_co", "b_gate")
GROUP_C = ("w_in",)
_COL_SHARDED = ("w_in", "w_branch_a", "w_branch_b", "w_ckv", "w_1", "b_gate")
ADAM_ROWS = {"w_in": 256, "w_branch_a": 512, "w_branch_b": 512, "w_out": 128, "w_cq": 128, "w_ckv": 512,
             "w_co": 128, "w_1": 256, "w_2": 256, "b_gate": 2}
SMALL = ("g_mix", "g_cross", "g_mem", "g_mlp", "g_final", "sink")


def _params(sem=None):
    return pltpu.CompilerParams(dimension_semantics=sem, vmem_limit_bytes=VMEM_LIMIT)


def _dot(a, b):
    return jnp.dot(a, b, preferred_element_type=F32)


def _dot_nt(a, b):
    return lax.dot_general(a, b, NT, preferred_element_type=F32)


def _dot_tn(a, b):
    return lax.dot_general(a, b, TN, preferred_element_type=F32)


def _rms(xt):
    return lax.rsqrt(jnp.mean(xt * xt, axis=-1, keepdims=True) + EPS)


def _rms_bwd(dh, xt, r, g):
    xn = xt * r
    dxn = dh * g
    dx = r * (dxn - xn * jnp.mean(dxn * xn, axis=-1, keepdims=True))
    return dx, jnp.sum(dh * xn, axis=0, keepdims=True)


def _rope(x, c, s, swa, sign):
    kinds = "qqqqkv" if swa else "qkvqkv"
    cq, sq = c * Q_SCALE, s * (sign * Q_SCALE)
    sk = s * sign if sign != 1 else s
    out = []
    for ci, kind in enumerate(kinds):
        xc = x[:, ci * 128:(ci + 1) * 128]
        if kind == "v":
            out.append(xc)
        elif kind == "q":
            out.append(xc * cq + pltpu.roll(xc, 64, 1) * sq)
        else:
            out.append(xc * c + pltpu.roll(xc, 64, 1) * sk)
    return jnp.concatenate(out, axis=1)


def _lane_scratch(rows, w):
    return pltpu.VMEM((w // 128, rows, 128), F32)


def _deinterleave(val, scr_ref, dst_ref, dtype):
    d, n = dst_ref.shape[0], dst_ref.shape[1]
    nc = val.shape[1] // 128
    for c in range(nc):
        scr_ref[c] = val[:, c * 128:(c + 1) * 128]
    for r in range(d):
        rows = [scr_ref.at[c][pl.ds(r, n, stride=d), :] for c in range(nc)]
        dst_ref[r] = jnp.concatenate(rows, axis=1).astype(dtype)


def _res_spec(a, tm):
    d, w = a.shape[0], a.shape[2]
    return pl.BlockSpec((d, tm // d, w), lambda i: (0, i, 0))


def _interleave(src_ref, scr_ref):
    d, n = src_ref.shape[0], src_ref.shape[1]
    nc = src_ref.shape[2] // 128
    for r in range(d):
        v = src_ref[r].astype(F32)
        for c in range(nc):
            scr_ref.at[c][pl.ds(r, n, stride=d), :] = v[:, c * 128:(c + 1) * 128]
    return jnp.concatenate([scr_ref[c] for c in range(nc)], axis=1)


def _with_dep(body, n_in, dep):
    if dep is None:
        return body
    return lambda *refs: body(*refs[:n_in], *refs[n_in + 1:])


def _dep_spec(dep):
    return [] if dep is None else [pl.BlockSpec(memory_space=pl.ANY)]


def _dep_arg(dep):
    return [] if dep is None else [dep]


def _inproj(h, h1, h2, w_p, tabs, tm, dep=None):
    t = h.shape[0]
    gw = 2 * PBLK
    (cos, sin), (cos1, sin1), (cos2, sin2) = tabs[0], tabs[1], tabs[2]

    def body(h_ref, h1_ref, h2_ref, w_ref, c_ref, s_ref, c1_ref, s1_ref, c2_ref, s2_ref,
             p0_ref, p1_ref, p2_ref, pb_ref):
        rows = lambda ref: ref[...].reshape(tm, ref.shape[-1])
        groups = ((h_ref, c_ref, s_ref, p0_ref), (h1_ref, c1_ref, s1_ref, p1_ref), (h2_ref, c2_ref, s2_ref, p2_ref))
        for gi, (lhs_ref, cc_ref, ss_ref, out_ref) in enumerate(groups):
            lhs, cc, ss = rows(lhs_ref), rows(cc_ref), rows(ss_ref)
            for half in range(2):
                col = gi * gw + half * PBLK
                val = _rope(_dot(lhs, w_ref[:, col:col + PBLK]), cc, ss, False, 1).astype(BF16)
                if out_ref.ndim == 3:
                    out_ref[:, :, half * PBLK:(half + 1) * PBLK] = val.reshape(out_ref.shape[:2] + (PBLK,))
                else:
                    out_ref[:, half * PBLK:(half + 1) * PBLK] = val
        pb_ref[...] = _rope(_dot(h_ref[...], w_ref[:, 3 * gw:]), c_ref[...], s_ref[...], True, 1).astype(BF16)

    d1, d2 = DIL_GROUPS[1][1], DIL_GROUPS[2][1]
    row = lambda w: pl.BlockSpec((tm, w), lambda i: (i, 0))
    res = lambda d, w: pl.BlockSpec((d, tm // d, w), lambda i: (0, i, 0))
    sds = jax.ShapeDtypeStruct
    return pl.pallas_call(
        _with_dep(body, 10, dep), name="inproj", grid=(t // tm,),
        in_specs=[row(D_MODEL), res(d1, D_MODEL), res(d2, D_MODEL),
                  pl.BlockSpec((D_MODEL, P_WIDTH), lambda i: (0, 0), pipeline_mode=pl.Buffered(1)),
                  row(128), row(128), res(d1, 128), res(d1, 128), res(d2, 128), res(d2, 128)] + _dep_spec(dep),
        out_specs=[row(gw), res(d1, gw), res(d2, gw), row(PBLK)],
        out_shape=[sds((t, gw), BF16), sds((d1, t // d1, gw), BF16), sds((d2, t // d2, gw), BF16),
                   sds((t, PBLK), BF16)],
        compiler_params=_params(("arbitrary",)),
    )(h, h1, h2, w_p, cos, sin, cos1, sin1, cos2, sin2, *_dep_arg(dep))


def _gates(h, w_g, b, tm, tn):
    t = h.shape[0]

    def body(h_ref, w_ref, b_ref, o_ref):
        z = _dot(h_ref[...], w_ref[...]) + b_ref[...]
        o_ref[...] = (0.5 * jnp.tanh(0.5 * z) + 0.5).astype(BF16)

    return pl.pallas_call(
        body, name="gates", grid=(t // tm, GATE_WIDTH // tn),
        in_specs=[pl.BlockSpec((tm, D_MODEL), lambda i, j: (i, 0)),
                  pl.BlockSpec((D_MODEL, tn), lambda i, j: (0, j)),
                  pl.BlockSpec((1, tn), lambda i, j: (0, j))],
        out_specs=pl.BlockSpec((tm, tn), lambda i, j: (i, j)),
        out_shape=jax.ShapeDtypeStruct((t, GATE_WIDTH), BF16),
        compiler_params=_params(("arbitrary", "arbitrary")),
    )(h, w_g, b)


def _band_mask(i, s):
    row = lax.broadcasted_iota(jnp.int32, (BAND, 2 * BAND), 0)
    col = lax.broadcasted_iota(jnp.int32, (BAND, 2 * BAND), 1)
    band = (col >= row) & (col <= row + BAND)
    if s == 0:
        band = band & ((col >= BAND) | (i > 0))
    return band


def _head_a_masks(rows):
    lane = lax.broadcasted_iota(jnp.int32, (rows, 128), 1)
    return (lane % HEAD_DIM) < HEAD_DIM // 2, lane < HEAD_DIM


def _stack_heads(x, head_a):
    zero = jnp.zeros_like(x)
    return jnp.concatenate([jnp.where(head_a, x, zero), jnp.where(head_a, zero, x)], axis=0)


def _stack_heads_t(xt, head_a_t):
    zero = jnp.zeros_like(xt)
    return jnp.concatenate([jnp.where(head_a_t, xt, zero), jnp.where(head_a_t, zero, xt)], axis=1)


def _kv_rows(cur_ref, tail_ref, s, off):
    if s == 0:
        return jnp.concatenate([tail_ref[:, off:off + 128], cur_ref[0:BAND, off:off + 128]], axis=0)
    return cur_ref[(s - 1) * BAND:(s + 1) * BAND, off:off + 128]


def _attn_layout(swa):
    if swa:
        return [(128 * j, 512, 640) for j in range(4)]
    return [(0, 128, 256), (384, 512, 640)]


def _attn_fwd(name, pv, swa, sinks, tq):
    d, ls = pv.shape[0], pv.shape[1]
    n, nsb = ls // tq, tq // BAND
    pairs = _attn_layout(swa)
    ncol = 1 if swa else 2
    ow = 128 * len(pairs)

    def body(cur_ref, tail_ref, *rest):
        sink_ref, o_ref, lse_ref, o32_ref = rest if swa else (None,) + rest + (None,)
        i = pl.program_id(2)
        lane = lax.broadcasted_iota(jnp.int32, (BAND, 128), 1)
        qk_a, v_a = _head_a_masks(BAND)
        first = lax.broadcasted_iota(jnp.int32, (2 * BAND, 1), 0) < BAND
        for s in range(nsb):
            mask = _band_mask(i, s)
            mask2 = jnp.concatenate([mask, mask], axis=0)
            rows = slice(s * BAND, (s + 1) * BAND)
            lse_tile = jnp.zeros((BAND, 128), F32)
            for j, (qo, ko, vo) in enumerate(pairs):
                q = cur_ref[rows, qo:qo + 128]
                kk = _kv_rows(cur_ref, tail_ref, s, ko)
                vv = _kv_rows(cur_ref, tail_ref, s, vo)
                sc = _dot_nt(_stack_heads(q, qk_a), kk)
                sc = jnp.where(mask2, sc, -jnp.inf)
                m = jnp.max(sc, axis=-1, keepdims=True)
                if swa:
                    sk = jnp.where(first, sink_ref[2 * j], sink_ref[2 * j + 1])
                    m = jnp.maximum(m, sk)
                p = jnp.exp(sc - m)
                den = jnp.sum(p, axis=-1, keepdims=True)
                if swa:
                    den = den + jnp.exp(sk - m)
                lse = m + jnp.log(den)
                lse_tile = jnp.where(lane == 2 * j, lse[:BAND], jnp.where(lane == 2 * j + 1, lse[BAND:], lse_tile))
                o2 = _dot(p.astype(BF16), vv) * (1.0 / den)
                o = jnp.where(v_a, o2[:BAND], o2[BAND:])
                o_ref[rows, j * 128:(j + 1) * 128] = o.astype(BF16)
                if swa:
                    o32_ref[rows, j * 128:(j + 1) * 128] = o
            lse_ref[rows, :] = lse_tile

    in_specs = [pl.BlockSpec((None, tq, PBLK), lambda r, cb, i: (r, i, cb)),
                pl.BlockSpec((None, BAND, PBLK), lambda r, cb, i: (r, jnp.maximum(i * nsb - 1, 0), cb))]
    args = [pv, pv]
    out_specs = [pl.BlockSpec((None, tq, ow), lambda r, cb, i: (r, i, cb)),
                 pl.BlockSpec((None, tq, 128), lambda r, cb, i: (r, i, cb))]
    out_shape = [jax.ShapeDtypeStruct((d, ls, 512), BF16), jax.ShapeDtypeStruct((d, ls, 128 * ncol), F32)]
    if swa:
        in_specs.append(pl.BlockSpec(memory_space=pltpu.SMEM))
        args.append(sinks)
        out_specs.append(out_specs[0])
        out_shape.append(jax.ShapeDtypeStruct((d, ls, 512), F32))
    return pl.pallas_call(
        body, name=name, grid=(d, ncol, n),
        in_specs=in_specs, out_specs=out_specs, out_shape=out_shape,
        compiler_params=_params(("arbitrary", "arbitrary", "arbitrary")),
    )(*args)


def _lse_lane(head):
    return (head // 4) * 128 + head % 4


def _dil_head_spread():
    lane = lax.broadcasted_iota(jnp.int32, (256, 512), 0)
    head = lax.broadcasted_iota(jnp.int32, (256, 512), 1) // HEAD_DIM
    return (lane == _lse_lane(head)).astype(BF16)


def _head_scale(x, tile, spread):
    return x * _dot(tile.astype(BF16), spread)


def _head_gather(width, dil):
    head = lax.broadcasted_iota(jnp.int32, (8 * HEAD_DIM, width), 0) // HEAD_DIM
    lane = lax.broadcasted_iota(jnp.int32, (8 * HEAD_DIM, width), 1)
    return (lane == (_lse_lane(head) if dil else head)).astype(BF16)


def _head_sums(x, gather):
    hi = x.astype(BF16)
    lo = (x - hi.astype(F32)).astype(BF16)
    return _dot(hi, gather) + _dot(lo, gather)


def _alphas(l0, l1, l2):
    m = jnp.maximum(jnp.maximum(l0, l1), l2)
    e0, e1, e2 = jnp.exp(l0 - m), jnp.exp(l1 - m), jnp.exp(l2 - m)
    den = e0 + e1 + e2
    return e0 / den, e1 / den, e2 / den


def _mix(o0, o1, o2, l0, l1, l2, ob, gts, x, w_a, w_b, w_out, g_cross, tm):
    t = x.shape[0]

    def body(o0_ref, o1_ref, o2_ref, l0_ref, l1_ref, l2_ref, ob_ref, g_ref, x_ref, wa_ref, wb_ref, wo_ref,
             gc_ref, oa_ref, ya_ref, yb_ref, mg_ref, x1_ref, hc_ref, so_ref, sl_ref):
        a0, a1, a2 = _alphas(l0_ref[...], _interleave(l1_ref, sl_ref), _interleave(l2_ref, sl_ref))
        spread = _dil_head_spread()
        oa = (_head_scale(o0_ref[...].astype(F32), a0, spread)
              + _head_scale(_interleave(o1_ref, so_ref), a1, spread)
              + _head_scale(_interleave(o2_ref, so_ref), a2, spread))
        oab = oa.astype(BF16)
        oa_ref[...] = oab
        ya = _dot(oab, wa_ref[...])
        yb = _dot(ob_ref[...], wb_ref[...])
        ya_ref[...] = ya.astype(BF16)
        yb_ref[...] = yb.astype(BF16)
        merged = (g_ref[:, :D_MODEL].astype(F32) * ya + g_ref[:, D_MODEL:].astype(F32) * yb).astype(BF16)
        mg_ref[...] = merged
        x1 = x_ref[...] + _dot(merged, wo_ref[...])
        x1_ref[...] = x1
        hc_ref[...] = (x1 * _rms(x1) * gc_ref[...]).astype(BF16)

    row = lambda w: pl.BlockSpec((tm, w), lambda i: (i, 0))
    full = lambda a, b: pl.BlockSpec((a, b), lambda i: (0, 0))
    return pl.pallas_call(
        body, name="mix", grid=(t // tm,),
        in_specs=[row(512), _res_spec(o1, tm), _res_spec(o2, tm), row(256), _res_spec(l1, tm), _res_spec(l2, tm),
                  row(512), row(GATE_WIDTH),
                  row(D_MODEL), full(512, D_MODEL), full(512, D_MODEL), full(D_MODEL, D_MODEL), full(1, D_MODEL)],
        out_specs=[row(512), row(D_MODEL), row(D_MODEL), row(D_MODEL), row(D_MODEL), row(D_MODEL)],
        out_shape=[jax.ShapeDtypeStruct((t, 512), BF16), jax.ShapeDtypeStruct((t, D_MODEL), BF16),
                   jax.ShapeDtypeStruct((t, D_MODEL), BF16), jax.ShapeDtypeStruct((t, D_MODEL), BF16),
                   jax.ShapeDtypeStruct((t, D_MODEL), F32), jax.ShapeDtypeStruct((t, D_MODEL), BF16)],
        scratch_shapes=[_lane_scratch(tm, 512), _lane_scratch(tm, 256)],
        compiler_params=_params(("arbitrary",)),
    )(o0, o1, o2, l0, l1, l2, ob, gts, x, w_a, w_b, w_out, g_cross)


def _memkv(mem, g_mem, w_ckv):
    m = mem.shape[0]
    ws = w_ckv.shape[2]

    def body(mem_ref, g_ref, w_ref, mn_ref, kv_ref):
        xt = mem_ref[...]
        mn = (xt * _rms(xt) * g_ref[...]).astype(BF16)
        mn_ref[...] = mn
        for j in range(N_DEV):
            kv_ref[:, j * ws:(j + 1) * ws] = _dot(mn, w_ref[j]).astype(BF16)

    return pl.pallas_call(
        body, name="memkv",
        out_shape=[jax.ShapeDtypeStruct((m, D_MODEL), BF16), jax.ShapeDtypeStruct((m, 2 * D_MODEL), BF16)],
        compiler_params=_params(),
    )(mem, g_mem, w_ckv)


def _cross_probs(q, kv_ref, h):
    k = kv_ref[:, h * X_HEAD_DIM:(h + 1) * X_HEAD_DIM]
    sc = _dot_nt(q[:, h * X_HEAD_DIM:(h + 1) * X_HEAD_DIM], k)
    m = jnp.max(sc, axis=-1, keepdims=True)
    p = jnp.exp(sc - m)
    return p / jnp.sum(p, axis=-1, keepdims=True)


def _cross(hc, x1, kv, w_cq, w_co, g_mlp, tm):
    t = x1.shape[0]
    m = kv.shape[0]

    def body(hc_ref, x1_ref, kv_ref, wq_ref, wo_ref, g_ref, q_ref, o_ref, x2_ref, hm_ref):
        q = (_dot(hc_ref[...], wq_ref[...]) * X_SCALE).astype(BF16)
        q_ref[...] = q
        outs = []
        for h in range(X_HEADS):
            p = _cross_probs(q, kv_ref, h)
            v = kv_ref[:, D_MODEL + h * X_HEAD_DIM:D_MODEL + (h + 1) * X_HEAD_DIM]
            outs.append(_dot(p.astype(BF16), v))
        o = jnp.concatenate(outs, axis=1).astype(BF16)
        o_ref[...] = o
        x2 = x1_ref[...] + _dot(o, wo_ref[...])
        x2_ref[...] = x2
        hm_ref[...] = (x2 * _rms(x2) * g_ref[...]).astype(BF16)

    row = lambda w: pl.BlockSpec((tm, w), lambda i: (i, 0))
    full = lambda a, b: pl.BlockSpec((a, b), lambda i: (0, 0))
    return pl.pallas_call(
        body, name="cross", grid=(t // tm,),
        in_specs=[row(D_MODEL), row(D_MODEL), full(m, 2 * D_MODEL), full(D_MODEL, D_MODEL),
                  full(D_MODEL, D_MODEL), full(1, D_MODEL)],
        out_specs=[row(D_MODEL)] * 4,
        out_shape=[jax.ShapeDtypeStruct((t, D_MODEL), BF16), jax.ShapeDtypeStruct((t, D_MODEL), BF16),
                   jax.ShapeDtypeStruct((t, D_MODEL), F32), jax.ShapeDtypeStruct((t, D_MODEL), BF16)],
        compiler_params=_params(("arbitrary",)),
    )(hc, x1, kv, w_cq, w_co, g_mlp)


def _mlp(hm, x2, w_1, w_2, g_final, target, tm, tf):
    t = x2.shape[0]
    nf = D_FF // tf

    def body(hm_ref, x2_ref, w1_ref, w2_ref, g_ref, tg_ref, a_ref, dx3_ref, loss_ref, dg_ref, acc_ref):
        i, f = pl.program_id(0), pl.program_id(1)
        hm_t = hm_ref[...]
        sw = w1_ref.shape[2]
        part = None
        for s in range(w1_ref.shape[0]):
            a = jnp.maximum(_dot(hm_t, w1_ref[s]), 0.0).astype(BF16)
            a_ref[:, s * sw:(s + 1) * sw] = a
            p_s = _dot(a * a, w2_ref[s * sw:(s + 1) * sw, :])
            part = p_s if part is None else part + p_s

        @pl.when(f == 0)
        def _():
            acc_ref[...] = part

        @pl.when(f > 0)
        def _():
            acc_ref[...] += part

        @pl.when((i == 0) & (f == 0))
        def _():
            loss_ref[...] = jnp.zeros_like(loss_ref)
            dg_ref[...] = jnp.zeros_like(dg_ref)

        @pl.when(f == nf - 1)
        def _():
            x3 = x2_ref[...] + acc_ref[...]
            r = _rms(x3)
            g = g_ref[...]
            diff = x3 * r * g - tg_ref[...]
            loss_ref[...] += 0.5 * jnp.sum(jnp.mean(diff * diff, axis=-1, keepdims=True))
            dx3, dg = _rms_bwd(diff / D_MODEL, x3, r, g)
            dx3_ref[...] = dx3
            dg_ref[...] += dg

    return pl.pallas_call(
        body, name="mlp", grid=(t // tm, nf),
        in_specs=[pl.BlockSpec((tm, D_MODEL), lambda i, f: (i, 0)),
                  pl.BlockSpec((tm, D_MODEL), lambda i, f: (i, 0)),
                  pl.BlockSpec((tf // w_1.shape[2], D_MODEL, w_1.shape[2]), lambda i, f: (f, 0, 0)),
                  pl.BlockSpec((tf, D_MODEL), lambda i, f: (f, 0)),
                  pl.BlockSpec((1, D_MODEL), lambda i, f: (0, 0)),
                  pl.BlockSpec((tm, D_MODEL), lambda i, f: (i, 0))],
        out_specs=[pl.BlockSpec((tm, tf), lambda i, f: (i, f)),
                   pl.BlockSpec((tm, D_MODEL), lambda i, f: (i, 0)),
                   pl.BlockSpec((1, 128), lambda i, f: (0, 0)),
                   pl.BlockSpec((1, D_MODEL), lambda i, f: (0, 0))],
        out_shape=[jax.ShapeDtypeStruct((t, D_FF), BF16), jax.ShapeDtypeStruct((t, D_MODEL), F32),
                   jax.ShapeDtypeStruct((1, 128), F32), jax.ShapeDtypeStruct((1, D_MODEL), F32)],
        scratch_shapes=[pltpu.VMEM((tm, D_MODEL), F32)],
        compiler_params=_params(("arbitrary", "arbitrary")),
    )(hm, x2, w_1, w_2, g_final, target)


def _mlp_bwd(dx3, a, w_1, w_2, x2, g_mlp, tm, tf):
    t = x2.shape[0]
    nf = D_FF // tf

    def body(dx3_ref, a_ref, w1_ref, w2_ref, x2_ref, g_ref, dz_ref, dx2_ref, dg_ref, acc_ref):
        i, f = pl.program_id(0), pl.program_id(1)
        dx3_b = dx3_ref[...].astype(BF16)
        sw = w1_ref.shape[2]
        part = None
        for s in range(w1_ref.shape[0]):
            cols = slice(s * sw, (s + 1) * sw)
            da2 = _dot_nt(dx3_b, w2_ref[cols, :])
            dz = (2.0 * a_ref[:, cols].astype(F32) * da2).astype(BF16)
            dz_ref[:, cols] = dz
            p_s = _dot_nt(dz, w1_ref[s])
            part = p_s if part is None else part + p_s

        @pl.when(f == 0)
        def _():
            acc_ref[...] = part

        @pl.when(f > 0)
        def _():
            acc_ref[...] += part

        @pl.when((i == 0) & (f == 0))
        def _():
            dg_ref[...] = jnp.zeros_like(dg_ref)

        @pl.when(f == nf - 1)
        def _():
            xt = x2_ref[...]
            dx, dg = _rms_bwd(acc_ref[...], xt, _rms(xt), g_ref[...])
            dx2_ref[...] = dx3_ref[...] + dx
            dg_ref[...] += dg

    return pl.pallas_call(
        body, name="mlp_bwd", grid=(t // tm, nf),
        in_specs=[pl.BlockSpec((tm, D_MODEL), lambda i, f: (i, 0)),
                  pl.BlockSpec((tm, tf), lambda i, f: (i, f)),
                  pl.BlockSpec((tf // w_1.shape[2], D_MODEL, w_1.shape[2]), lambda i, f: (f, 0, 0)),
                  pl.BlockSpec((tf, D_MODEL), lambda i, f: (f, 0)),
                  pl.BlockSpec((tm, D_MODEL), lambda i, f: (i, 0)),
                  pl.BlockSpec((1, D_MODEL), lambda i, f: (0, 0))],
        out_specs=[pl.BlockSpec((tm, tf), lambda i, f: (i, f)),
                   pl.BlockSpec((tm, D_MODEL), lambda i, f: (i, 0)),
                   pl.BlockSpec((1, D_MODEL), lambda i, f: (0, 0))],
        out_shape=[jax.ShapeDtypeStruct((t, D_FF), BF16), jax.ShapeDtypeStruct((t, D_MODEL), F32),
                   jax.ShapeDtypeStruct((1, D_MODEL), F32)],
        scratch_shapes=[pltpu.VMEM((tm, D_MODEL), F32)],
        compiler_params=_params(("arbitrary", "arbitrary")),
    )(dx3, a, w_1, w_2, x2, g_mlp)


def _wgrad(name, a, b, tka, tn, tm, square=False, col_shards=False):
    t, ka = a.shape
    n = b.shape[1]
    nk = t // tm

    def body(a_ref, b_ref, o_ref, acc_ref):
        at = a_ref[...].astype(BF16)
        if square:
            at = at * at
        part = _dot_tn(at, b_ref[...].astype(BF16))
        k = pl.program_id(2)

        @pl.when(k == 0)
        def _():
            acc_ref[...] = part

        @pl.when(k > 0)
        def _():
            acc_ref[...] += part

        @pl.when(k == nk - 1)
        def _():
            if col_shards:
                for s in range(tn // sw):
                    o_ref[s] = acc_ref[:, s * sw:(s + 1) * sw].astype(BF16)
            else:
                o_ref[...] = acc_ref[...].astype(BF16)

    if col_shards:
        sw = n // N_DEV
        out_spec = pl.BlockSpec((tn // sw, tka, sw), lambda p, q, k: (q, p, 0))
        out_shape = jax.ShapeDtypeStruct((N_DEV, ka, sw), BF16)
    else:
        out_spec = pl.BlockSpec((tka, tn), lambda p, q, k: (p, q))
        out_shape = jax.ShapeDtypeStruct((ka, n), BF16)
    return pl.pallas_call(
        body, name=name, grid=(ka // tka, n // tn, nk),
        in_specs=[pl.BlockSpec((tm, tka), lambda p, q, k: (k, p)),
                  pl.BlockSpec((tm, tn), lambda p, q, k: (k, q))],
        out_specs=out_spec, out_shape=out_shape,
        scratch_shapes=[pltpu.VMEM((tka, tn), F32)],
        compiler_params=_params(("arbitrary", "arbitrary", "arbitrary")),
    )(a, b)


def _cross_bwd(dx2, x1, q, kv, w_cq, w_co, g_cross, tm, dep=None):
    t = x1.shape[0]
    m = kv.shape[0]

    def body(dx2_ref, x1_ref, q_ref, kv_ref, wq_ref, wo_ref, g_ref, dq_ref, dx1_ref, dkv_ref, dg_ref):
        @pl.when(pl.program_id(0) == 0)
        def _():
            dkv_ref[...] = jnp.zeros_like(dkv_ref)
            dg_ref[...] = jnp.zeros_like(dg_ref)

        do = _dot_nt(dx2_ref[...].astype(BF16), wo_ref[...]).astype(BF16)
        q = q_ref[...]
        dqs = []
        for h in range(X_HEADS):
            hs = slice(h * X_HEAD_DIM, (h + 1) * X_HEAD_DIM)
            vs = slice(D_MODEL + h * X_HEAD_DIM, D_MODEL + (h + 1) * X_HEAD_DIM)
            p = _cross_probs(q, kv_ref, h)
            dp = _dot_nt(do[:, hs], kv_ref[:, vs])
            ds = (p * (dp - jnp.sum(dp * p, axis=-1, keepdims=True))).astype(BF16)
            dqs.append(_dot(ds, kv_ref[:, hs]))
            dkv_ref[:, hs] += _dot_tn(ds, q[:, hs])
            dkv_ref[:, vs] += _dot_tn(p.astype(BF16), do[:, hs])
        dq = (jnp.concatenate(dqs, axis=1) * X_SCALE).astype(BF16)
        dq_ref[...] = dq
        xt = x1_ref[...]
        dx, dg = _rms_bwd(_dot_nt(dq, wq_ref[...]), xt, _rms(xt), g_ref[...])
        dx1_ref[...] = dx2_ref[...] + dx
        dg_ref[...] += dg

    row = lambda w: pl.BlockSpec((tm, w), lambda i: (i, 0))
    full = lambda a, b: pl.BlockSpec((a, b), lambda i: (0, 0))
    return pl.pallas_call(
        _with_dep(body, 7, dep), name="cross_bwd", grid=(t // tm,),
        in_specs=[row(D_MODEL), row(D_MODEL), row(D_MODEL), full(m, 2 * D_MODEL), full(D_MODEL, D_MODEL),
                  full(D_MODEL, D_MODEL), full(1, D_MODEL)] + _dep_spec(dep),
        out_specs=[row(D_MODEL), row(D_MODEL), full(m, 2 * D_MODEL), full(1, D_MODEL)],
        out_shape=[jax.ShapeDtypeStruct((t, D_MODEL), BF16), jax.ShapeDtypeStruct((t, D_MODEL), F32),
                   jax.ShapeDtypeStruct((m, 2 * D_MODEL), F32), jax.ShapeDtypeStruct((1, D_MODEL), F32)],
        compiler_params=_params(("arbitrary",)),
    )(dx2, x1, q, kv, w_cq, w_co, g_cross, *_dep_arg(dep))


def _memkv_bwd(dkv, mn, mem, w_ckv, g_mem):
    ws = w_ckv.shape[2]

    def body(dkv_ref, mn_ref, mem_ref, w_ref, g_ref, dw_ref, dg_ref):
        mn = mn_ref[...]
        dmn = jnp.zeros(mn.shape, F32)
        for j in range(N_DEV):
            dkvb = dkv_ref[:, j * ws:(j + 1) * ws].astype(BF16)
            dw_ref[j] = _dot_tn(mn, dkvb).astype(BF16)
            dmn = dmn + _dot_nt(dkvb, w_ref[j])
        xt = mem_ref[...]
        dg_ref[...] = jnp.sum(dmn * xt * _rms(xt), axis=0, keepdims=True)

    return pl.pallas_call(
        body, name="memkv_bwd",
        out_shape=[jax.ShapeDtypeStruct(w_ckv.shape, BF16), jax.ShapeDtypeStruct((1, D_MODEL), F32)],
        compiler_params=_params(),
    )(dkv, mn, mem, w_ckv, g_mem)


def _merge_bwd(dx1, ya, yb, gts, w_out, w_g, tm):
    t = dx1.shape[0]

    def body(dx1_ref, ya_ref, yb_ref, g_ref, wo_ref, wg_ref, dg_ref, dhp_ref, dya_ref, dyb_ref, db_ref):
        @pl.when(pl.program_id(0) == 0)
        def _():
            db_ref[...] = jnp.zeros_like(db_ref)

        dm = _dot_nt(dx1_ref[...].astype(BF16), wo_ref[...])
        ga = g_ref[:, :D_MODEL].astype(F32)
        gb = g_ref[:, D_MODEL:].astype(F32)
        dya_ref[...] = (dm * ga).astype(BF16)
        dyb_ref[...] = (dm * gb).astype(BF16)
        dpa = dm * ya_ref[...].astype(F32) * ga * (1.0 - ga)
        dpb = dm * yb_ref[...].astype(F32) * gb * (1.0 - gb)
        dpre = jnp.concatenate([dpa, dpb], axis=1)
        db_ref[...] += jnp.sum(dpre, axis=0, keepdims=True)
        dpreb = dpre.astype(BF16)
        dg_ref[...] = dpreb
        dhp_ref[...] = _dot_nt(dpreb, wg_ref[...])

    row = lambda w: pl.BlockSpec((tm, w), lambda i: (i, 0))
    once = lambda a, b: pl.BlockSpec((a, b), lambda i: (0, 0), pipeline_mode=pl.Buffered(1))
    sds = jax.ShapeDtypeStruct
    return pl.pallas_call(
        body, name="merge_bwd", grid=(t // tm,),
        in_specs=[row(D_MODEL), row(D_MODEL), row(D_MODEL), row(GATE_WIDTH),
                  once(D_MODEL, D_MODEL), once(D_MODEL, GATE_WIDTH)],
        out_specs=[row(GATE_WIDTH), row(D_MODEL), row(D_MODEL), row(D_MODEL),
                   pl.BlockSpec((1, GATE_WIDTH), lambda i: (0, 0))],
        out_shape=[sds((t, GATE_WIDTH), BF16), sds((t, D_MODEL), F32), sds((t, D_MODEL), BF16),
                   sds((t, D_MODEL), BF16), sds((1, GATE_WIDTH), F32)],
        compiler_params=_params(("arbitrary",)),
    )(dx1, ya, yb, gts, w_out, w_g)


def _combine_bwd(dya, dyb, oa, ob, l0, l1, l2, lb, sink_row, w_a, w_b, tm):
    t = dya.shape[0]

    def body(dya_ref, dyb_ref, oa_ref, ob_ref, l0_ref, l1_ref, l2_ref, lb_ref, sk_ref, wa_ref, wb_ref,
             do0_ref, do1_ref, do2_ref, c0_ref, c1_ref, c2_ref, dob_ref, cb_ref, dsk_ref, so_ref, sl_ref):
        @pl.when(pl.program_id(0) == 0)
        def _():
            dsk_ref[...] = jnp.zeros_like(dsk_ref)

        doa = _dot_nt(dya_ref[...], wa_ref[...])
        dob = _dot_nt(dyb_ref[...], wb_ref[...])
        dsum = _head_sums(doa * oa_ref[...].astype(F32), _head_gather(256, True))
        a0, a1, a2 = _alphas(l0_ref[...], _interleave(l1_ref, sl_ref), _interleave(l2_ref, sl_ref))
        c0_ref[...] = a0 * dsum
        spread = _dil_head_spread()
        do0_ref[...] = _head_scale(doa, a0, spread).astype(BF16)
        for al, do_ref, c_ref in ((a1, do1_ref, c1_ref), (a2, do2_ref, c2_ref)):
            _deinterleave(al * dsum, sl_ref, c_ref, F32)
            _deinterleave(_head_scale(doa, al, spread), so_ref, do_ref, BF16)
        dob_ref[...] = dob.astype(BF16)
        cb = _head_sums(dob * ob_ref[...], _head_gather(128, False))
        cb_ref[...] = cb
        lane = lax.broadcasted_iota(jnp.int32, cb.shape, 1)
        psink = jnp.where(lane < 8, jnp.exp(sk_ref[...] - lb_ref[...]), 0.0)
        dsk_ref[...] += jnp.sum(-psink * cb, axis=0, keepdims=True)

    row = lambda w: pl.BlockSpec((tm, w), lambda i: (i, 0))
    full = lambda a, b: pl.BlockSpec((a, b), lambda i: (0, 0))
    sds = jax.ShapeDtypeStruct
    d1, d2 = l1.shape[0], l2.shape[0]
    res = lambda d, w: pl.BlockSpec((d, tm // d, w), lambda i: (0, i, 0))
    return pl.pallas_call(
        body, name="combine_bwd", grid=(t // tm,),
        in_specs=[row(D_MODEL), row(D_MODEL), row(512), row(512),
                  row(256), _res_spec(l1, tm), _res_spec(l2, tm), row(128), full(1, 128),
                  full(512, D_MODEL), full(512, D_MODEL)],
        out_specs=[row(512), res(d1, 512), res(d2, 512), row(256), res(d1, 256), res(d2, 256),
                   row(512), row(128), full(1, 128)],
        out_shape=[sds((t, 512), BF16), sds((d1, t // d1, 512), BF16),
                   sds((d2, t // d2, 512), BF16), sds((t, 256), F32), sds((d1, t // d1, 256), F32),
                   sds((d2, t // d2, 256), F32), sds((t, 512), BF16),
                   sds((t, 128), F32), sds((1, 128), F32)],
        scratch_shapes=[_lane_scratch(tm, 512), _lane_scratch(tm, 256)],
        compiler_params=_params(("arbitrary",)),
    )(dya, dyb, oa, ob, l0, l1, l2, lb, sink_row, w_a, w_b)


def _attn_bwd(name, pv, dov, lsev, cv, cosv, sinv, swa, tq, dep=None):
    d, ls = pv.shape[0], pv.shape[1]
    n, nsb = ls // tq, tq // BAND
    pairs = _attn_layout(swa)
    ncol = 1 if swa else 2
    ow = 128 * len(pairs)

    kv_slots = sorted({(ko, vo) for _, ko, vo in pairs})

    def body(cur_ref, tail_ref, do_ref, lse_ref, c_ref, cos_ref, sin_ref, out_ref, acc_ref, carry_ref, acct_ref):
        i = pl.program_id(2)
        blk_i = n - 1 - i
        acc_ref[...] = jnp.zeros_like(acc_ref)
        acct_ref[...] = jnp.zeros_like(acct_ref)

        @pl.when(i == 0)
        def _():
            carry_ref[...] = jnp.zeros_like(carry_ref)

        qk_a, v_a = _head_a_masks(BAND)
        dim = lax.broadcasted_iota(jnp.int32, (128, BAND), 0)
        qk_at, v_at = (dim % HEAD_DIM) < HEAD_DIM // 2, dim < HEAD_DIM
        for s in range(nsb):
            mask = _band_mask(blk_i, s)
            mask2 = jnp.concatenate([mask, mask], axis=0)
            rows = slice(s * BAND, (s + 1) * BAND)
            kcols = slice(s * BAND, (s + 2) * BAND)
            for j, (qo, ko, vo) in enumerate(pairs):
                slot = kv_slots.index((ko, vo))
                kk = _kv_rows(cur_ref, tail_ref, s, ko)
                vv = _kv_rows(cur_ref, tail_ref, s, vo)
                q, do = cur_ref[rows, qo:qo + 128], do_ref[rows, j * 128:(j + 1) * 128]
                q2, do2 = _stack_heads(q, qk_a), _stack_heads(do, v_a)
                col2 = lambda ref: jnp.concatenate([ref[rows, 2 * j:2 * j + 1], ref[rows, 2 * j + 1:2 * j + 2]], axis=0)
                sc = _dot_nt(q2, kk)
                p = jnp.exp(jnp.where(mask2, sc, -jnp.inf) - col2(lse_ref))
                dp = _dot_nt(do2, vv)
                ds = (p * (dp - col2(c_ref))).astype(BF16)
                dq2 = _dot(ds, kk)
                acc_ref[BAND + s * BAND:BAND + (s + 1) * BAND, qo:qo + 128] += jnp.where(qk_a, dq2[:BAND], dq2[BAND:])
                acct_ref[2 * slot, :, kcols] += _dot(_stack_heads_t(q.T, qk_at), ds)
                acct_ref[2 * slot + 1, :, kcols] += _dot(_stack_heads_t(do.T, v_at), p.astype(BF16))
        for slot, (ko, vo) in enumerate(kv_slots):
            acc_ref[:, ko:ko + 128] += acct_ref[2 * slot].T
            acc_ref[:, vo:vo + 128] += acct_ref[2 * slot + 1].T

        last = acc_ref[tq:, :] + carry_ref[...]
        fin = last if tq == BAND else jnp.concatenate([acc_ref[BAND:tq, :], last], axis=0)
        out_ref[...] = _rope(fin, cos_ref[...], sin_ref[...], swa, -1).astype(BF16)
        carry_ref[...] = acc_ref[0:BAND, :]

    rev = lambda i: n - 1 - i
    blk = lambda rows, w, row_of: pl.BlockSpec((None, rows, w), lambda r, cb, i: (r, row_of(i), cb))
    tab = pl.BlockSpec((None, tq, 128), lambda r, cb, i: (r, rev(i), 0))
    return pl.pallas_call(
        _with_dep(body, 7, dep), name=name, grid=(d, ncol, n),
        in_specs=[blk(tq, PBLK, rev), blk(BAND, PBLK, lambda i: jnp.maximum(rev(i) * nsb - 1, 0)),
                  blk(tq, ow, rev), blk(tq, 128, rev), blk(tq, 128, rev), tab, tab] + _dep_spec(dep),
        out_specs=blk(tq, PBLK, rev),
        out_shape=jax.ShapeDtypeStruct((d, ls, ncol * PBLK), BF16),
        scratch_shapes=[pltpu.VMEM((tq + BAND, PBLK), F32), pltpu.VMEM((BAND, PBLK), F32),
                        pltpu.VMEM((2 * len(kv_slots), 128, tq + BAND), F32)],
        compiler_params=_params(("arbitrary", "arbitrary", "arbitrary")),
    )(pv, pv, dov, lsev, cv, cosv, sinv, *_dep_arg(dep))


def _dx(dp0, dp1, dp2, dpb, w_p, dh_part, dx1, x, g_mix, tm, dep=None):
    t = x.shape[0]
    gw = 2 * PBLK

    def body(dp0_ref, dp1_ref, dp2_ref, dpb_ref, w_ref, dhp_ref, dx1_ref, x_ref, g_ref, gx_ref, dg_ref,
             dpt_ref, scr_ref):
        @pl.when(pl.program_id(0) == 0)
        def _():
            dg_ref[...] = jnp.zeros_like(dg_ref)

        dpt_ref[:, 0:gw] = dp0_ref[...]
        dpt_ref[:, gw:2 * gw] = _interleave(dp1_ref, scr_ref).astype(BF16)
        dpt_ref[:, 2 * gw:3 * gw] = _interleave(dp2_ref, scr_ref).astype(BF16)
        dpt_ref[:, 3 * gw:] = dpb_ref[...]
        dh = _dot_nt(dpt_ref[...], w_ref[...]) + dhp_ref[...]
        xt = x_ref[...]
        dx, dg = _rms_bwd(dh, xt, _rms(xt), g_ref[...])
        gx_ref[...] = dx1_ref[...] + dx
        dg_ref[...] += dg

    row = lambda w: pl.BlockSpec((tm, w), lambda i: (i, 0))
    full = lambda a, b: pl.BlockSpec((a, b), lambda i: (0, 0))
    return pl.pallas_call(
        _with_dep(body, 9, dep), name="dx", grid=(t // tm,),
        in_specs=[row(gw), _res_spec(dp1, tm), _res_spec(dp2, tm), row(PBLK),
                  pl.BlockSpec((D_MODEL, P_WIDTH), lambda i: (0, 0), pipeline_mode=pl.Buffered(1)),
                  row(D_MODEL), row(D_MODEL), row(D_MODEL), full(1, D_MODEL)] + _dep_spec(dep),
        out_specs=[row(D_MODEL), full(1, D_MODEL)],
        out_shape=[jax.ShapeDtypeStruct((t, D_MODEL), F32), jax.ShapeDtypeStruct((1, D_MODEL), F32)],
        scratch_shapes=[pltpu.VMEM((tm, P_WIDTH), BF16), _lane_scratch(tm, gw)],
        compiler_params=_params(("arbitrary",)),
    )(dp0, dp1, dp2, dpb, w_p, dh_part, dx1, x, g_mix, *_dep_arg(dep))


MESH = pl.DeviceIdType.MESH
HBM_SPEC = pl.BlockSpec(memory_space=pltpu.HBM)
VMEM_SPEC = pl.BlockSpec(memory_space=pltpu.VMEM)


def _all_gather(xp, act, g, tm):
    t = act.shape[0]
    d1, d2 = DIL_GROUPS[1][1], DIL_GROUPS[2][1]

    def body(x_ref, act_ref, g_ref, out_ref, h_ref, h1_ref, h2_ref, send_sems, recv_sems, local_sem, hf_ref):
        x, y, c = lax.axis_index("x"), lax.axis_index("y"), lax.axis_index("c")
        me, sibling = (x, y, c), (x, y, 1 - c)
        chips = [(1 - x, y), (x, 1 - y), (1 - x, 1 - y)]

        def rows(px, py, pc):
            return out_ref.at[4 * px + 2 * py + pc]

        def copy(k, block, to, src=None):
            return pltpu.make_async_remote_copy(
                src_ref=rows(*block) if src is None else src, dst_ref=rows(*block),
                send_sem=send_sems.at[k], recv_sem=recv_sems.at[k], device_id=to, device_id_type=MESH)

        mine = pltpu.make_async_copy(x_ref, rows(*me), local_sem)
        mine.start()
        first = [copy(0, me, sibling, src=x_ref)]
        first += [copy(1 + j, me, (*chip, c), src=x_ref) for j, chip in enumerate(chips)]
        for cp in first:
            cp.start()

        def norm(a_blk, h_blk, h1_blk, h2_blk):
            xt = a_blk[...]
            hf = xt * _rms(xt) * g_ref[...]
            h_blk[...] = hf.astype(BF16)
            _deinterleave(hf, hf_ref, h1_blk, BF16)
            _deinterleave(hf, hf_ref, h2_blk, BF16)

        res = lambda d: pl.BlockSpec((d, tm // d, D_MODEL), lambda i: (0, i, 0))
        row = pl.BlockSpec((tm, D_MODEL), lambda i: (i, 0))
        pltpu.emit_pipeline(norm, grid=(t // tm,), in_specs=[row], out_specs=[row, res(d1), res(d2)])(
            act_ref, h_ref, h1_ref, h2_ref)

        passed = [copy(4 + j, (*chip, c), sibling) for j, chip in enumerate(chips)]
        for j, chip in enumerate(chips):
            copy(1 + j, (*chip, c), me).wait_recv()
            passed[j].start()
        copy(0, sibling, me).wait_recv()
        for j, chip in enumerate(chips):
            copy(4 + j, (*chip, 1 - c), me).wait_recv()
        for cp in first + passed:
            cp.wait_send()
        mine.wait()

    sds = jax.ShapeDtypeStruct
    return pl.pallas_call(
        body, name="all_gather",
        out_shape=[sds((N_DEV,) + xp.shape, xp.dtype), sds((t, D_MODEL), BF16),
                   sds((d1, t // d1, D_MODEL), BF16), sds((d2, t // d2, D_MODEL), BF16)],
        in_specs=[HBM_SPEC, HBM_SPEC, VMEM_SPEC], out_specs=[HBM_SPEC] * 4,
        scratch_shapes=[pltpu.SemaphoreType.DMA((7,)), pltpu.SemaphoreType.DMA((7,)), pltpu.SemaphoreType.DMA,
                        _lane_scratch(tm, D_MODEL)],
        compiler_params=pltpu.CompilerParams(vmem_limit_bytes=VMEM_LIMIT),
    )(xp, act, g)


def _peers():
    x, y, c = lax.axis_index("x"), lax.axis_index("y"), lax.axis_index("c")
    out = []
    for k in range(1, N_DEV):
        px = 1 - x if k & 4 else x
        py = 1 - y if k & 2 else y
        pc = 1 - c if k & 1 else c
        out.append((k, (px, py, pc), 4 * px + 2 * py + pc))
    return out


def _my_index():
    return 4 * lax.axis_index("x") + 2 * lax.axis_index("y") + lax.axis_index("c")


SEM_SPEC = pl.BlockSpec(memory_space=pltpu.SEMAPHORE)
ANY_SPEC = pl.BlockSpec(memory_space=pl.ANY)
_SPLIT_PARAMS = pltpu.CompilerParams(has_side_effects=pltpu.SideEffectType.DATAFLOW_SIDE_EFFECTING)


def _split_copies(gather, src_refs, land_refs, send_sems, recv_sems):
    me_idx = _my_index()
    out = []
    for a, (src_ref, land_ref) in enumerate(zip(src_refs, land_refs)):
        for k, peer, peer_idx in _peers():
            if gather:
                src, dst = src_ref, land_ref.at[me_idx]
            else:
                src, dst = src_ref.at[peer_idx], land_ref.at[k - 1]
            out.append(pltpu.make_async_remote_copy(
                src_ref=src, dst_ref=dst, send_sem=send_sems.at[7 * a + k - 1], recv_sem=recv_sems.at[7 * a + k - 1],
                device_id=peer, device_id_type=MESH))
    return out


def _split_start(name, gather, srcs):
    n = len(srcs)

    def body(*refs):
        send_sems, recv_sems = refs[n], refs[n + 1]
        for cp in _split_copies(gather, refs[:n], refs[2 * n + 2:3 * n + 2], send_sems, recv_sems):
            cp.start()
        token = refs[-1]
        token[...] = jnp.zeros_like(token)

    lands = [pltpu.HBM((N_DEV,) + a.shape if gather else (N_DEV - 1,) + a.shape[1:], a.dtype) for a in srcs]
    return pl.pallas_call(
        body, name=name,
        out_shape=(pltpu.SemaphoreType.DMA((7 * n,)), pltpu.SemaphoreType.DMA((7 * n,)),
                   *[pltpu.HBM(a.shape, a.dtype) for a in srcs], *lands, jax.ShapeDtypeStruct((8, 128), F32)),
        in_specs=(HBM_SPEC,) * n, out_specs=(SEM_SPEC, SEM_SPEC) + (HBM_SPEC,) * (2 * n) + (VMEM_SPEC,),
        input_output_aliases={i: 2 + i for i in range(n)}, compiler_params=_SPLIT_PARAMS,
    )(*[pltpu.with_memory_space_constraint(a, pltpu.HBM) for a in srcs])


def _split_wait(name, gather, started, after):
    send_sems, recv_sems, bufs = started[0], started[1], started[2:-1]
    n = len(bufs) // 2

    def body(*refs):
        for cp in _split_copies(gather, refs[:n], refs[n:2 * n], refs[2 * n], refs[2 * n + 1]):
            cp.wait_send()
            cp.wait_recv()

    out = pl.pallas_call(
        body, name=name, out_shape=tuple(pltpu.HBM(a.shape, a.dtype) for a in bufs),
        in_specs=(HBM_SPEC,) * (2 * n) + (SEM_SPEC, SEM_SPEC, ANY_SPEC), out_specs=(HBM_SPEC,) * (2 * n),
        input_output_aliases={i: i for i in range(2 * n)}, compiler_params=_SPLIT_PARAMS,
    )(*bufs, send_sems, recv_sems, after)
    return out[:n], out[n:]


def _adam_update(g, w, m, v):
    nm = ADAM_B1 * m + (1.0 - ADAM_B1) * g
    nv = ADAM_B2 * v + (1.0 - ADAM_B2) * (g * g)
    m_hat = nm / (1.0 - ADAM_B1 ** ADAM_STEP)
    v_hat = nv / (1.0 - ADAM_B2 ** ADAM_STEP)
    return -ADAM_LR * (m_hat / (jnp.sqrt(v_hat) + ADAM_EPS) + ADAM_WD * w), nm, nv


def _adamw(name, me, sent, got, w, m, v, tr):
    r, c = w.shape

    def body(me_ref, own_ref, got_ref, w_ref, m_ref, v_ref, g_ref, d_ref, nm_ref, nv_ref):
        g = own_ref[...].astype(F32)
        for k in range(N_DEV - 1):
            g = g + got_ref[k].astype(F32)
        g_ref[...] = g
        d_ref[...], nm_ref[...], nv_ref[...] = _adam_update(g, w_ref[...], m_ref[...], v_ref[...])

    blk = pl.BlockSpec((tr, c), lambda i, me_ref: (i, 0))
    return pl.pallas_call(
        body, name=name,
        grid_spec=pltpu.PrefetchScalarGridSpec(
            num_scalar_prefetch=1, grid=(r // tr,),
            in_specs=[pl.BlockSpec((None, tr, c), lambda i, me_ref: (me_ref[0], i, 0)),
                      pl.BlockSpec((N_DEV - 1, tr, c), lambda i, me_ref: (0, i, 0)), blk, blk, blk],
            out_specs=[blk] * 4),
        out_shape=[jax.ShapeDtypeStruct((r, c), F32)] * 4,
        compiler_params=_params(("arbitrary",)),
    )(me, sent, got, w, m, v)


def _adamw_small(srecv, ws, ms, vs):
    nv_ = len(ws)

    def body(*refs):
        s_ref = refs[0]
        ins, outs = refs[1:1 + 3 * nv_], refs[1 + 3 * nv_:]
        g_all = s_ref[0]
        for k in range(1, N_DEV):
            g_all = g_all + s_ref[k]
        for i in range(nv_):
            n = ins[i].shape[1]
            g = g_all[i:i + 1, :n]
            d, nm, nv = _adam_update(g, ins[i][...], ins[nv_ + i][...], ins[2 * nv_ + i][...])
            outs[i][...], outs[nv_ + i][...], outs[2 * nv_ + i][...], outs[3 * nv_ + i][...] = g, d, nm, nv
        outs[-1][...] = g_all[nv_:nv_ + 1, :128]

    shapes = [jax.ShapeDtypeStruct(a.shape, F32) for a in ws]
    res = pl.pallas_call(body, name="adamw_small", out_shape=shapes * 4 + [jax.ShapeDtypeStruct((1, 128), F32)],
                         compiler_params=_params())(srecv, *ws, *ms, *vs)
    return [res[k * nv_:(k + 1) * nv_] for k in range(4)], res[-1]


def _cols_from_shards(a):
    return jnp.swapaxes(a, 0, 1).reshape(a.shape[1], a.shape[0] * a.shape[2])


def _shards_from_cols(a):
    return jnp.swapaxes(a.reshape(a.shape[0], N_DEV, a.shape[1] // N_DEV), 0, 1)


def _shards_from_rows(a):
    return a.reshape(N_DEV, a.shape[0] // N_DEV, a.shape[1])


def _pair_lanes(a):
    lead = a.shape[:-1]
    return a.reshape(lead + (2, 2, HEAD_DIM // 2)).swapaxes(-3, -2).reshape(lead + (128,))


def _split_w_in(w_in):
    rows = w_in.shape[0]
    dil = w_in[:, :3 * DIL_WIDTH].reshape(rows, 3, 3, 4, 128)
    dil = jnp.concatenate([_pair_lanes(dil[:, :2]), dil[:, 2:]], axis=1)
    dil = dil.transpose(0, 2, 3, 1, 4).reshape(rows, 3 * DIL_WIDTH)
    o = 3 * DIL_WIDTH
    qb = w_in[:, o:o + SWA_Q_WIDTH].reshape(rows, 2, 4, HEAD_DIM).transpose(0, 2, 1, 3).reshape(rows, 4, 128)
    qb = _pair_lanes(qb).reshape(rows, SWA_Q_WIDTH)
    kb = _pair_lanes(w_in[:, o + SWA_Q_WIDTH:o + SWA_Q_WIDTH + SWA_KV_WIDTH])
    vb = w_in[:, o + SWA_Q_WIDTH + SWA_KV_WIDTH:P_WIDTH]
    return jnp.concatenate([dil, qb, kb, vb], axis=1)


def _merge_w_in(dw_p, dw_g):
    rows = dw_p.shape[0]
    dil = dw_p[:, :3 * DIL_WIDTH].reshape(rows, 3, 4, 3, 128).transpose(0, 3, 1, 2, 4)
    dil = jnp.concatenate([_pair_lanes(dil[:, :2]), dil[:, 2:]], axis=1).reshape(rows, 3 * DIL_WIDTH)
    o = 3 * DIL_WIDTH
    qb = _pair_lanes(dw_p[:, o:o + SWA_Q_WIDTH].reshape(rows, 4, 128))
    qb = qb.reshape(rows, 4, 2, HEAD_DIM).transpose(0, 2, 1, 3).reshape(rows, SWA_Q_WIDTH)
    kb = _pair_lanes(dw_p[:, o + SWA_Q_WIDTH:o + SWA_Q_WIDTH + SWA_KV_WIDTH])
    vb = dw_p[:, o + SWA_Q_WIDTH + SWA_KV_WIDTH:]
    return jnp.concatenate([dil, qb, kb, vb, dw_g], axis=1)


def _swa_rows(w_b):
    return w_b.reshape(2, 4, HEAD_DIM, -1).transpose(1, 0, 2, 3).reshape(SWA_Q_WIDTH, -1)


def _swa_rows_inv(dw_b):
    return dw_b.reshape(4, 2, HEAD_DIM, -1).transpose(1, 0, 2, 3).reshape(SWA_Q_WIDTH, -1)


def _rope_tables(pos):
    half = HEAD_DIM // 2
    inv = ROPE_THETA ** (-jnp.arange(half, dtype=F32) / half)
    ang = pos.astype(F32)[:, None] * jnp.tile(inv, 4)
    sign = jnp.repeat(jnp.array([-1.0, 1.0], F32), 2 * half)
    return jnp.cos(ang), jnp.sin(ang) * sign


def _local_step(x, hs, mem, pos, target, w_in, dep, rest_weights, on_grads, g_mix, g_cross, g_mem, g_mlp, g_final, sink):
    t = x.shape[0]
    tm = min(512, t)
    tq = 1024
    tw = min(2048, t)
    w_p = _split_w_in(w_in)
    cos, sin = lax.optimization_barrier(_rope_tables(pos))
    sink_row = jnp.pad(sink.reshape(2, 4).T.reshape(1, 8), ((0, 0), (0, 120)))
    tabs = [(cos[None], sin[None])]
    for _, d in DIL_GROUPS[1:]:
        tabs.append(tuple(a.reshape(t // d, d, 128).swapaxes(0, 1) for a in (cos, sin)))
    tabs.append(tabs[0])

    h, h1, h2 = hs
    p0, p1, p2, pb = _inproj(h, h1, h2, w_p, [(cos, sin), tabs[1], tabs[2]], tm, dep)
    ps = [p0[None], p1, p2, pb[None]]
    outs, lses = [], []
    for gi, pv in enumerate(ps):
        res = _attn_fwd(f"attn_fwd{gi}", pv, gi == 3, sink_row[0, :8], min(tq, pv.shape[1]))
        outs.append(res[0])
        lses.append(res[1])
    o0, l0, ob, lb, ob32 = outs[0][0], lses[0][0], outs[3][0], lses[3][0], res[2][0]
    wts = rest_weights(lb)
    w_b = _swa_rows(wts["w_branch_b"])
    tf = 2048
    w_g = wts["w_g"]
    gts = _gates(h, w_g, wts["b_gate"].reshape(1, GATE_WIDTH), tm, 1024)
    oa, ya, yb, merged, x1, hc = _mix(o0, outs[1], outs[2], l0, lses[1], lses[2], ob, gts, x,
                                      wts["w_branch_a"], w_b, wts["w_out"], g_cross, tm)
    mn, kv = _memkv(mem, g_mem, wts["w_ckv"])
    q, o, x2, hm = _cross(hc, x1, kv, wts["w_cq"], wts["w_co"], g_mlp, tm)
    a, dx3, loss, dg_final = _mlp(hm, x2, wts["w_1"], wts["w_2"], g_final.reshape(1, D_MODEL), target, tm, tf)

    grads = {}
    dz, dx2, dg_mlp = _mlp_bwd(dx3, a, wts["w_1"], wts["w_2"], x2, g_mlp, tm, tf)
    grads["w_2"] = _shards_from_rows(_wgrad("dw_2", a, dx3, 1024, 1024, tw, square=True))
    grads["w_1"] = _wgrad("dw_1", hm, dz, 1024, 1024, tw, col_shards=True)
    dep = on_grads(GROUP_A, grads)
    dq, dx1, dkv, dg_cross = _cross_bwd(dx2, x1, q, kv, wts["w_cq"], wts["w_co"], g_cross, tm, dep)
    grads["w_co"] = _shards_from_rows(_wgrad("dw_co", o, dx2, 1024, 1024, tw))
    grads["w_cq"] = _shards_from_rows(_wgrad("dw_cq", hc, dq, 1024, 1024, tw))
    grads["w_ckv"], dg_mem = _memkv_bwd(dkv, mn, mem, wts["w_ckv"], g_mem)
    dgt, dh_part, dya, dyb, db_gate = _merge_bwd(dx1, ya, yb, gts, wts["w_out"], w_g, tm)
    do0, do1, do2, c0, c1, c2, dob, cb, dsink = _combine_bwd(
        dya, dyb, oa, ob32, l0, lses[1], lses[2], lb, sink_row, wts["w_branch_a"], w_b, tm)
    grads["w_out"] = _shards_from_rows(_wgrad("dw_out", merged, dx1, 1024, 1024, tw))
    grads["w_branch_a"] = _shards_from_cols(_wgrad("dw_a", oa, dya, 512, 1024, tw))
    grads["w_branch_b"] = _shards_from_cols(_swa_rows_inv(_wgrad("dw_b", ob, dyb, 512, 1024, tw)))
    grads["b_gate"] = _shards_from_cols(db_gate.reshape(2, D_MODEL)).astype(BF16)
    dep = on_grads(GROUP_B, grads)
    dw_g = _wgrad("dw_g", h, dgt, 1024, 1024, tw)
    dps = []
    for gi, (pv, do_g, c_g) in enumerate(zip(ps, (do0[None], do1, do2, dob[None]), (c0[None], c1, c2, cb[None]))):
        dps.append(_attn_bwd(f"attn_bwd{gi}", pv, do_g, lses[gi], c_g, tabs[gi][0], tabs[gi][1], gi == 3,
                             min(tq, pv.shape[1]),
                             dep if gi == 0 else None))
    dw_p = jnp.concatenate(
        [_wgrad(f"dw_p{gi}", hh.reshape(t, D_MODEL), dpg.reshape(t, -1), 1024, PBLK, tw)
         for gi, (hh, dpg) in enumerate(zip((h, h1, h2, h), dps))], axis=1)
    grads["w_in"] = _shards_from_cols(_merge_w_in(dw_p, dw_g))
    dep = on_grads(GROUP_C, grads)
    grad_x, dg_mix = _dx(dps[0][0], dps[1], dps[2], dps[3][0], w_p, dh_part, dx1, x, g_mix, tm, dep)
    dsink_heads = dsink[0, :8].reshape(4, 2).T.reshape(8)
    small = {"g_mix": dg_mix[0], "g_cross": dg_cross[0], "g_mem": dg_mem[0], "g_mlp": dg_mlp[0],
             "g_final": dg_final[0], "sink": dsink_heads}
    return loss[0, 0], grad_x, small


def kernel(x, mem, positions, g_mix, w_in, b_gate, sink, w_branch_a, w_branch_b, w_out, g_cross, g_mem, w_cq, w_ckv, w_co, g_mlp, w_1, w_2, g_final, loss_target, m_g_mix, m_w_in, m_b_gate, m_sink, m_w_branch_a, m_w_branch_b, m_w_out, m_g_cross, m_g_mem, m_w_cq, m_w_ckv, m_w_co, m_g_mlp, m_w_1, m_w_2, m_g_final, v_g_mix, v_w_in, v_b_gate, v_sink, v_w_branch_a, v_w_branch_b, v_w_out, v_g_cross, v_g_mem, v_w_cq, v_w_ckv, v_w_co, v_g_mlp, v_w_1, v_w_2, v_g_final):
    local = dict(locals())
    shard = {n: local[n][0] for n in GROUP_A + GROUP_B + GROUP_C}
    me = _my_index()
    me_arr = me.reshape(1).astype(jnp.int32)
    tags = {GROUP_A: "a", GROUP_B: "b", GROUP_C: "c"}

    gathered_w_in, *hs = _all_gather(shard["w_in"].astype(BF16), x[0], g_mix, min(512, x.shape[1]))
    w_in_full = _cols_from_shards(gathered_w_in)
    rest = GROUP_A + GROUP_B

    def gathered(name, started, after):
        srcs, lands = _split_wait(name, True, started, after)
        return [lax.dynamic_update_slice(land, src[None], (me,) + (0,) * src.ndim) for src, land in zip(srcs, lands)]

    gather = _split_start("gather_start", True,
                          [shard[n] if n == "b_gate" else shard[n].astype(BF16) for n in rest])

    def rest_weights(after):
        full = {"w_g": w_in_full[:, P_WIDTH:]}
        for name, a in zip(rest, gathered("gather_wait", gather, after)):
            if name in ("w_1", "w_ckv"):
                full[name] = a
            elif name in _COL_SHARDED:
                full[name] = _cols_from_shards(a)
            else:
                full[name] = a.reshape(N_DEV * a.shape[1], a.shape[2])
        return full

    scatters = {}

    def on_grads(names, grads):
        scatters[names] = _split_start("scatter_start_" + tags[names], False, [grads[n] for n in names])
        return scatters[names][-1]

    loss, grad_x, small = _local_step(
        x[0], hs, mem[0], positions[0], loss_target[0], w_in_full, gather[-1], rest_weights, on_grads,
        g_mix, g_cross, g_mem, g_mlp, g_final, sink[0])

    sp = jnp.stack([small[n] if n != "sink" else jnp.pad(small[n], (0, LANES - 8)) for n in SMALL]
                   + [jnp.pad(loss.reshape(1), (0, LANES - 1)), jnp.zeros((LANES,), F32)])
    small_gather = _split_start("small_start", True, [sp])

    after, updated = small_gather[-1], {}
    for names in (GROUP_A, GROUP_B, GROUP_C):
        sent, got = _split_wait("scatter_wait_" + tags[names], False, scatters[names], after)
        for i, name in enumerate(names):
            outs = _adamw("adamw_" + name, me_arr, sent[i], got[i], shard[name],
                          local["m_" + name][0], local["v_" + name][0], ADAM_ROWS[name])
            updated[name] = [a[None] for a in outs]
            after = outs[3]

    flat = lambda prefix: [local[prefix + n].reshape(1, -1) for n in SMALL]
    outs, loss_row = _adamw_small(gathered("small_wait", small_gather, after)[0], flat(""), flat("m_"), flat("v_"))
    for i, name in enumerate(SMALL):
        updated[name] = [outs[which][i].reshape(local[name].shape) for which in range(4)]

    order = ["g_mix", "w_in", "b_gate", "sink", "w_branch_a", "w_branch_b", "w_out", "g_cross", "g_mem", "w_cq",
             "w_ckv", "w_co", "g_mlp", "w_1", "w_2", "g_final"]
    res = [loss_row[0, 0], grad_x[None]]
    for which in range(4):
        res += [updated[n][which] for n in order]
    return tuple(res)
```

```python
import functools
import math

import jax
import jax.numpy as jnp
from jax import lax
from jax.experimental import pallas as pl
from jax.experimental.pallas import tpu as pltpu

F32 = jnp.float32
BF16 = jnp.bfloat16

D_MODEL = 1024
HEAD_DIM = 64
DIL_GROUPS = ((128, 1), (512, 4), (2048, 16))
ROPE_THETA = 10000.0
X_HEADS = 4
X_HEAD_DIM = D_MODEL // X_HEADS
D_FF = 4 * D_MODEL
EPS = 1e-6
DIL_WIDTH = 1536
SWA_Q_WIDTH = 512
SWA_KV_WIDTH = 128
P_WIDTH = 3 * DIL_WIDTH + SWA_Q_WIDTH + 2 * SWA_KV_WIDTH
GATE_WIDTH = 2 * D_MODEL
IN_WIDTH = P_WIDTH + GATE_WIDTH
BAND = 128
PBLK = 768
Q_SCALE = HEAD_DIM ** -0.5
X_SCALE = X_HEAD_DIM ** -0.5

ADAM_LR = 0.001
ADAM_B1 = 0.9
ADAM_B2 = 0.999
ADAM_EPS = 1e-08
ADAM_WD = 0.01
ADAM_STEP = 10

N_DEV = 8
LANES = 1024
VMEM_LIMIT = 52 * 1024 * 1024

NT = (((1,), (1,)), ((), ()))
TN = (((0,), (0,)), ((), ()))

GROUP_A = ("w_1", "w_2")
GROUP_B = ("w_branch_a", "w_branch_b", "w_out", "w_cq", "w_ckv", "w_co", "b_gate")
GROUP_C = ("w_in",)
_COL_SHARDED = ("w_in", "w_branch_a", "w_branch_b", "w_ckv", "w_1", "b_gate")
ADAM_ROWS = {"w_in": 256, "w_branch_a": 512, "w_branch_b": 512, "w_out": 128, "w_cq": 128, "w_ckv": 512,
             "w_co": 128, "w_1": 256, "w_2": 256, "b_gate": 2}
SMALL = ("g_mix", "g_cross", "g_mem", "g_mlp", "g_final", "sink")


def _params(sem=None):
    return pltpu.CompilerParams(dimension_semantics=sem, vmem_limit_bytes=VMEM_LIMIT)


def _dot(a, b):
    return jnp.dot(a, b, preferred_element_type=F32)


def _dot_nt(a, b):
    return lax.dot_general(a, b, NT, preferred_element_type=F32)


def _dot_tn(a, b):
    return lax.dot_general(a, b, TN, preferred_element_type=F32)


def _rms(xt):
    return lax.rsqrt(jnp.mean(xt * xt, axis=-1, keepdims=True) + EPS)


def _rms_bwd(dh, xt, r, g):
    xn = xt * r
    dxn = dh * g
    dx = r * (dxn - xn * jnp.mean(dxn * xn, axis=-1, keepdims=True))
    return dx, jnp.sum(dh * xn, axis=0, keepdims=True)


def _rope(x, c, s, swa, sign):
    kinds = "qqqqkv" if swa else "qkvqkv"
    cq, sq = c * Q_SCALE, s * (sign * Q_SCALE)
    sk = s * sign if sign != 1 else s
    out = []
    for ci, kind in enumerate(kinds):
        xc = x[:, ci * 128:(ci + 1) * 128]
        if kind == "v":
            out.append(xc)
        elif kind == "q":
            out.append(xc * cq + pltpu.roll(xc, 64, 1) * sq)
        else:
            out.append(xc * c + pltpu.roll(xc, 64, 1) * sk)
    return jnp.concatenate(out, axis=1)


def _lane_scratch(rows, w):
    return pltpu.VMEM((w // 128, rows, 128), F32)


def _deinterleave(val, scr_ref, dst_ref, dtype):
    d, n = dst_ref.shape[0], dst_ref.shape[1]
    nc = val.shape[1] // 128
    for c in range(nc):
        scr_ref[c] = val[:, c * 128:(c + 1) * 128]
    for r in range(d):
        rows = [scr_ref.at[c][pl.ds(r, n, stride=d), :] for c in range(nc)]
        dst_ref[r] = jnp.concatenate(rows, axis=1).astype(dtype)


def _res_spec(a, tm):
    d, w = a.shape[0], a.shape[2]
    return pl.BlockSpec((d, tm // d, w), lambda i: (0, i, 0))


def _interleave(src_ref, scr_ref):
    d, n = src_ref.shape[0], src_ref.shape[1]
    nc = src_ref.shape[2] // 128
    for r in range(d):
        v = src_ref[r].astype(F32)
        for c in range(nc):
            scr_ref.at[c][pl.ds(r, n, stride=d), :] = v[:, c * 128:(c + 1) * 128]
    return jnp.concatenate([scr_ref[c] for c in range(nc)], axis=1)


def _with_dep(body, n_in, dep):
    if dep is None:
        return body
    return lambda *refs: body(*refs[:n_in], *refs[n_in + 1:])


def _dep_spec(dep):
    return [] if dep is None else [pl.BlockSpec(memory_space=pl.ANY)]


def _dep_arg(dep):
    return [] if dep is None else [dep]


def _inproj(h, h1, h2, w_p, tabs, tm, dep=None):
    t = h.shape[0]
    gw = 2 * PBLK
    (cos, sin), (cos1, sin1), (cos2, sin2) = tabs[0], tabs[1], tabs[2]

    def body(h_ref, h1_ref, h2_ref, w_ref, c_ref, s_ref, c1_ref, s1_ref, c2_ref, s2_ref,
             p0_ref, p1_ref, p2_ref, pb_ref):
        rows = lambda ref: ref[...].reshape(tm, ref.shape[-1])
        groups = ((h_ref, c_ref, s_ref, p0_ref), (h1_ref, c1_ref, s1_ref, p1_ref), (h2_ref, c2_ref, s2_ref, p2_ref))
        for gi, (lhs_ref, cc_ref, ss_ref, out_ref) in enumerate(groups):
            lhs, cc, ss = rows(lhs_ref), rows(cc_ref), rows(ss_ref)
            for half in range(2):
                col = gi * gw + half * PBLK
                val = _rope(_dot(lhs, w_ref[:, col:col + PBLK]), cc, ss, False, 1).astype(BF16)
                if out_ref.ndim == 3:
                    out_ref[:, :, half * PBLK:(half + 1) * PBLK] = val.reshape(out_ref.shape[:2] + (PBLK,))
                else:
                    out_ref[:, half * PBLK:(half + 1) * PBLK] = val
        pb_ref[...] = _rope(_dot(h_ref[...], w_ref[:, 3 * gw:]), c_ref[...], s_ref[...], True, 1).astype(BF16)

    d1, d2 = DIL_GROUPS[1][1], DIL_GROUPS[2][1]
    row = lambda w: pl.BlockSpec((tm, w), lambda i: (i, 0))
    res = lambda d, w: pl.BlockSpec((d, tm // d, w), lambda i: (0, i, 0))
    sds = jax.ShapeDtypeStruct
    return pl.pallas_call(
        _with_dep(body, 10, dep), name="inproj", grid=(t // tm,),
        in_specs=[row(D_MODEL), res(d1, D_MODEL), res(d2, D_MODEL),
                  pl.BlockSpec((D_MODEL, P_WIDTH), lambda i: (0, 0), pipeline_mode=pl.Buffered(1)),
                  row(128), row(128), res(d1, 128), res(d1, 128), res(d2, 128), res(d2, 128)] + _dep_spec(dep),
        out_specs=[row(gw), res(d1, gw), res(d2, gw), row(PBLK)],
        out_shape=[sds((t, gw), BF16), sds((d1, t // d1, gw), BF16), sds((d2, t // d2, gw), BF16),
                   sds((t, PBLK), BF16)],
        compiler_params=_params(("arbitrary",)),
    )(h, h1, h2, w_p, cos, sin, cos1, sin1, cos2, sin2, *_dep_arg(dep))


def _gates(h, w_g, b, tm, tn):
    t = h.shape[0]

    def body(h_ref, w_ref, b_ref, o_ref):
        z = _dot(h_ref[...], w_ref[...]) + b_ref[...]
        o_ref[...] = (0.5 * jnp.tanh(0.5 * z) + 0.5).astype(BF16)

    return pl.pallas_call(
        body, name="gates", grid=(t // tm, GATE_WIDTH // tn),
        in_specs=[pl.BlockSpec((tm, D_MODEL), lambda i, j: (i, 0)),
                  pl.BlockSpec((D_MODEL, tn), lambda i, j: (0, j)),
                  pl.BlockSpec((1, tn), lambda i, j: (0, j))],
        out_specs=pl.BlockSpec((tm, tn), lambda i, j: (i, j)),
        out_shape=jax.ShapeDtypeStruct((t, GATE_WIDTH), BF16),
        compiler_params=_params(("arbitrary", "arbitrary")),
    )(h, w_g, b)


def _band_mask(i, s):
    row = lax.broadcasted_iota(jnp.int32, (BAND, 2 * BAND), 0)
    col = lax.broadcasted_iota(jnp.int32, (BAND, 2 * BAND), 1)
    band = (col >= row) & (col <= row + BAND)
    if s == 0:
        band = band & ((col >= BAND) | (i > 0))
    return band


def _head_a_masks(rows):
    lane = lax.broadcasted_iota(jnp.int32, (rows, 128), 1)
    return (lane % HEAD_DIM) < HEAD_DIM // 2, lane < HEAD_DIM


def _stack_heads(x, head_a):
    zero = jnp.zeros_like(x)
    return jnp.concatenate([jnp.where(head_a, x, zero), jnp.where(head_a, zero, x)], axis=0)


def _stack_heads_t(xt, head_a_t):
    zero = jnp.zeros_like(xt)
    return jnp.concatenate([jnp.where(head_a_t, xt, zero), jnp.where(head_a_t, zero, xt)], axis=1)


def _kv_rows(cur_ref, tail_ref, s, off):
    if s == 0:
        return jnp.concatenate([tail_ref[:, off:off + 128], cur_ref[0:BAND, off:off + 128]], axis=0)
    return cur_ref[(s - 1) * BAND:(s + 1) * BAND, off:off + 128]


def _attn_layout(swa):
    if swa:
        return [(128 * j, 512, 640) for j in range(4)]
    return [(0, 128, 256), (384, 512, 640)]


def _attn_fwd(name, pv, swa, sinks, tq):
    d, ls = pv.shape[0], pv.shape[1]
    n, nsb = ls // tq, tq // BAND
    pairs = _attn_layout(swa)
    ncol = 1 if swa else 2
    ow = 128 * len(pairs)

    def body(cur_ref, tail_ref, *rest):
        sink_ref, o_ref, lse_ref, o32_ref = rest if swa else (None,) + rest + (None,)
        i = pl.program_id(2)
        lane = lax.broadcasted_iota(jnp.int32, (BAND, 128), 1)
        qk_a, v_a = _head_a_masks(BAND)
        first = lax.broadcasted_iota(jnp.int32, (2 * BAND, 1), 0) < BAND
        for s in range(nsb):
            mask = _band_mask(i, s)
            mask2 = jnp.concatenate([mask, mask], axis=0)
            rows = slice(s * BAND, (s + 1) * BAND)
            lse_tile = jnp.zeros((BAND, 128), F32)
            for j, (qo, ko, vo) in enumerate(pairs):
                q = cur_ref[rows, qo:qo + 128]
                kk = _kv_rows(cur_ref, tail_ref, s, ko)
                vv = _kv_rows(cur_ref, tail_ref, s, vo)
                sc = _dot_nt(_stack_heads(q, qk_a), kk)
                sc = jnp.where(mask2, sc, -jnp.inf)
                m = jnp.max(sc, axis=-1, keepdims=True)
                if swa:
                    sk = jnp.where(first, sink_ref[2 * j], sink_ref[2 * j + 1])
                    m = jnp.maximum(m, sk)
                p = jnp.exp(sc - m)
                den = jnp.sum(p, axis=-1, keepdims=True)
                if swa:
                    den = den + jnp.exp(sk - m)
                lse = m + jnp.log(den)
                lse_tile = jnp.where(lane == 2 * j, lse[:BAND], jnp.where(lane == 2 * j + 1, lse[BAND:], lse_tile))
                o2 = _dot(p.astype(BF16), vv) * (1.0 / den)
                o = jnp.where(v_a, o2[:BAND], o2[BAND:])
                o_ref[rows, j * 128:(j + 1) * 128] = o.astype(BF16)
                if swa:
                    o32_ref[rows, j * 128:(j + 1) * 128] = o
            lse_ref[rows, :] = lse_tile

    in_specs = [pl.BlockSpec((None, tq, PBLK), lambda r, cb, i: (r, i, cb)),
                pl.BlockSpec((None, BAND, PBLK), lambda r, cb, i: (r, jnp.maximum(i * nsb - 1, 0), cb))]
    args = [pv, pv]
    out_specs = [pl.BlockSpec((None, tq, ow), lambda r, cb, i: (r, i, cb)),
                 pl.BlockSpec((None, tq, 128), lambda r, cb, i: (r, i, cb))]
    out_shape = [jax.ShapeDtypeStruct((d, ls, 512), BF16), jax.ShapeDtypeStruct((d, ls, 128 * ncol), F32)]
    if swa:
        in_specs.append(pl.BlockSpec(memory_space=pltpu.SMEM))
        args.append(sinks)
        out_specs.append(out_specs[0])
        out_shape.append(jax.ShapeDtypeStruct((d, ls, 512), F32))
    return pl.pallas_call(
        body, name=name, grid=(d, ncol, n),
        in_specs=in_specs, out_specs=out_specs, out_shape=out_shape,
        compiler_params=_params(("arbitrary", "arbitrary", "arbitrary")),
    )(*args)


def _lse_lane(head):
    return (head // 4) * 128 + head % 4


def _dil_head_spread():
    lane = lax.broadcasted_iota(jnp.int32, (256, 512), 0)
    head = lax.broadcasted_iota(jnp.int32, (256, 512), 1) // HEAD_DIM
    return (lane == _lse_lane(head)).astype(BF16)


def _head_scale(x, tile, spread):
    return x * _dot(tile.astype(BF16), spread)


def _head_gather(width, dil):
    head = lax.broadcasted_iota(jnp.int32, (8 * HEAD_DIM, width), 0) // HEAD_DIM
    lane = lax.broadcasted_iota(jnp.int32, (8 * HEAD_DIM, width), 1)
    return (lane == (_lse_lane(head) if dil else head)).astype(BF16)


def _head_sums(x, gather):
    hi = x.astype(BF16)
    lo = (x - hi.astype(F32)).astype(BF16)
    return _dot(hi, gather) + _dot(lo, gather)


def _alphas(l0, l1, l2):
    m = jnp.maximum(jnp.maximum(l0, l1), l2)
    e0, e1, e2 = jnp.exp(l0 - m), jnp.exp(l1 - m), jnp.exp(l2 - m)
    den = e0 + e1 + e2
    return e0 / den, e1 / den, e2 / den


def _mix(o0, o1, o2, l0, l1, l2, ob, gts, x, w_a, w_b, w_out, g_cross, tm):
    t = x.shape[0]

    def body(o0_ref, o1_ref, o2_ref, l0_ref, l1_ref, l2_ref, ob_ref, g_ref, x_ref, wa_ref, wb_ref, wo_ref,
             gc_ref, oa_ref, ya_ref, yb_ref, mg_ref, x1_ref, hc_ref, so_ref, sl_ref):
        a0, a1, a2 = _alphas(l0_ref[...], _interleave(l1_ref, sl_ref), _interleave(l2_ref, sl_ref))
        spread = _dil_head_spread()
        oa = (_head_scale(o0_ref[...].astype(F32), a0, spread)
              + _head_scale(_interleave(o1_ref, so_ref), a1, spread)
              + _head_scale(_interleave(o2_ref, so_ref), a2, spread))
        oab = oa.astype(BF16)
        oa_ref[...] = oab
        ya = _dot(oab, wa_ref[...])
        yb = _dot(ob_ref[...], wb_ref[...])
        ya_ref[...] = ya.astype(BF16)
        yb_ref[...] = yb.astype(BF16)
        merged = (g_ref[:, :D_MODEL].astype(F32) * ya + g_ref[:, D_MODEL:].astype(F32) * yb).astype(BF16)
        mg_ref[...] = merged
        x1 = x_ref[...] + _dot(merged, wo_ref[...])
        x1_ref[...] = x1
        hc_ref[...] = (x1 * _rms(x1) * gc_ref[...]).astype(BF16)

    row = lambda w: pl.BlockSpec((tm, w), lambda i: (i, 0))
    full = lambda a, b: pl.BlockSpec((a, b), lambda i: (0, 0))
    return pl.pallas_call(
        body, name="mix", grid=(t // tm,),
        in_specs=[row(512), _res_spec(o1, tm), _res_spec(o2, tm), row(256), _res_spec(l1, tm), _res_spec(l2, tm),
                  row(512), row(GATE_WIDTH),
                  row(D_MODEL), full(512, D_MODEL), full(512, D_MODEL), full(D_MODEL, D_MODEL), full(1, D_MODEL)],
        out_specs=[row(512), row(D_MODEL), row(D_MODEL), row(D_MODEL), row(D_MODEL), row(D_MODEL)],
        out_shape=[jax.ShapeDtypeStruct((t, 512), BF16), jax.ShapeDtypeStruct((t, D_MODEL), BF16),
                   jax.ShapeDtypeStruct((t, D_MODEL), BF16), jax.ShapeDtypeStruct((t, D_MODEL), BF16),
                   jax.ShapeDtypeStruct((t, D_MODEL), F32), jax.ShapeDtypeStruct((t, D_MODEL), BF16)],
        scratch_shapes=[_lane_scratch(tm, 512), _lane_scratch(tm, 256)],
        compiler_params=_params(("arbitrary",)),
    )(o0, o1, o2, l0, l1, l2, ob, gts, x, w_a, w_b, w_out, g_cross)


def _memkv(mem, g_mem, w_ckv):
    m = mem.shape[0]
    ws = w_ckv.shape[2]

    def body(mem_ref, g_ref, w_ref, mn_ref, kv_ref):
        xt = mem_ref[...]
        mn = (xt * _rms(xt) * g_ref[...]).astype(BF16)
        mn_ref[...] = mn
        for j in range(N_DEV):
            kv_ref[:, j * ws:(j + 1) * ws] = _dot(mn, w_ref[j]).astype(BF16)

    return pl.pallas_call(
        body, name="memkv",
        out_shape=[jax.ShapeDtypeStruct((m, D_MODEL), BF16), jax.ShapeDtypeStruct((m, 2 * D_MODEL), BF16)],
        compiler_params=_params(),
    )(mem, g_mem, w_ckv)


def _cross_probs(q, kv_ref, h):
    k = kv_ref[:, h * X_HEAD_DIM:(h + 1) * X_HEAD_DIM]
    sc = _dot_nt(q[:, h * X_HEAD_DIM:(h + 1) * X_HEAD_DIM], k)
    m = jnp.max(sc, axis=-1, keepdims=True)
    p = jnp.exp(sc - m)
    return p / jnp.sum(p, axis=-1, keepdims=True)


def _cross(hc, x1, kv, w_cq, w_co, g_mlp, tm):
    t = x1.shape[0]
    m = kv.shape[0]

    def body(hc_ref, x1_ref, kv_ref, wq_ref, wo_ref, g_ref, q_ref, o_ref, x2_ref, hm_ref):
        q = (_dot(hc_ref[...], wq_ref[...]) * X_SCALE).astype(BF16)
        q_ref[...] = q
        outs = []
        for h in range(X_HEADS):
            p = _cross_probs(q, kv_ref, h)
            v = kv_ref[:, D_MODEL + h * X_HEAD_DIM:D_MODEL + (h + 1) * X_HEAD_DIM]
            outs.append(_dot(p.astype(BF16), v))
        o = jnp.concatenate(outs, axis=1).astype(BF16)
        o_ref[...] = o
        x2 = x1_ref[...] + _dot(o, wo_ref[...])
        x2_ref[...] = x2
        hm_ref[...] = (x2 * _rms(x2) * g_ref[...]).astype(BF16)

    row = lambda w: pl.BlockSpec((tm, w), lambda i: (i, 0))
    full = lambda a, b: pl.BlockSpec((a, b), lambda i: (0, 0))
    return pl.pallas_call(
        body, name="cross", grid=(t // tm,),
        in_specs=[row(D_MODEL), row(D_MODEL), full(m, 2 * D_MODEL), full(D_MODEL, D_MODEL),
                  full(D_MODEL, D_MODEL), full(1, D_MODEL)],
        out_specs=[row(D_MODEL)] * 4,
        out_shape=[jax.ShapeDtypeStruct((t, D_MODEL), BF16), jax.ShapeDtypeStruct((t, D_MODEL), BF16),
                   jax.ShapeDtypeStruct((t, D_MODEL), F32), jax.ShapeDtypeStruct((t, D_MODEL), BF16)],
        compiler_params=_params(("arbitrary",)),
    )(hc, x1, kv, w_cq, w_co, g_mlp)


def _mlp(hm, x2, w_1, w_2, g_final, target, tm, tf):
    t = x2.shape[0]
    nf = D_FF // tf

    def body(hm_ref, x2_ref, w1_ref, w2_ref, g_ref, tg_ref, a_ref, dx3_ref, loss_ref, dg_ref, acc_ref):
        i, f = pl.program_id(0), pl.program_id(1)
        hm_t = hm_ref[...]
        sw = w1_ref.shape[2]
        part = None
        for s in range(w1_ref.shape[0]):
            a = jnp.maximum(_dot(hm_t, w1_ref[s]), 0.0).astype(BF16)
            a_ref[:, s * sw:(s + 1) * sw] = a
            p_s = _dot(a * a, w2_ref[s * sw:(s + 1) * sw, :])
            part = p_s if part is None else part + p_s

        @pl.when(f == 0)
        def _():
            acc_ref[...] = part

        @pl.when(f > 0)
        def _():
            acc_ref[...] += part

        @pl.when((i == 0) & (f == 0))
        def _():
            loss_ref[...] = jnp.zeros_like(loss_ref)
            dg_ref[...] = jnp.zeros_like(dg_ref)

        @pl.when(f == nf - 1)
        def _():
            x3 = x2_ref[...] + acc_ref[...]
            r = _rms(x3)
            g = g_ref[...]
            diff = x3 * r * g - tg_ref[...]
            loss_ref[...] += 0.5 * jnp.sum(jnp.mean(diff * diff, axis=-1, keepdims=True))
            dx3, dg = _rms_bwd(diff / D_MODEL, x3, r, g)
            dx3_ref[...] = dx3
            dg_ref[...] += dg

    return pl.pallas_call(
        body, name="mlp", grid=(t // tm, nf),
        in_specs=[pl.BlockSpec((tm, D_MODEL), lambda i, f: (i, 0)),
                  pl.BlockSpec((tm, D_MODEL), lambda i, f: (i, 0)),
                  pl.BlockSpec((tf // w_1.shape[2], D_MODEL, w_1.shape[2]), lambda i, f: (f, 0, 0)),
                  pl.BlockSpec((tf, D_MODEL), lambda i, f: (f, 0)),
                  pl.BlockSpec((1, D_MODEL), lambda i, f: (0, 0)),
                  pl.BlockSpec((tm, D_MODEL), lambda i, f: (i, 0))],
        out_specs=[pl.BlockSpec((tm, tf), lambda i, f: (i, f)),
                   pl.BlockSpec((tm, D_MODEL), lambda i, f: (i, 0)),
                   pl.BlockSpec((1, 128), lambda i, f: (0, 0)),
                   pl.BlockSpec((1, D_MODEL), lambda i, f: (0, 0))],
        out_shape=[jax.ShapeDtypeStruct((t, D_FF), BF16), jax.ShapeDtypeStruct((t, D_MODEL), F32),
                   jax.ShapeDtypeStruct((1, 128), F32), jax.ShapeDtypeStruct((1, D_MODEL), F32)],
        scratch_shapes=[pltpu.VMEM((tm, D_MODEL), F32)],
        compiler_params=_params(("arbitrary", "arbitrary")),
    )(hm, x2, w_1, w_2, g_final, target)


def _mlp_bwd(dx3, a, w_1, w_2, x2, g_mlp, tm, tf):
    t = x2.shape[0]
    nf = D_FF // tf

    def body(dx3_ref, a_ref, w1_ref, w2_ref, x2_ref, g_ref, dz_ref, dx2_ref, dg_ref, acc_ref):
        i, f = pl.program_id(0), pl.program_id(1)
        dx3_b = dx3_ref[...].astype(BF16)
        sw = w1_ref.shape[2]
        part = None
        for s in range(w1_ref.shape[0]):
            cols = slice(s * sw, (s + 1) * sw)
            da2 = _dot_nt(dx3_b, w2_ref[cols, :])
            dz = (2.0 * a_ref[:, cols].astype(F32) * da2).astype(BF16)
            dz_ref[:, cols] = dz
            p_s = _dot_nt(dz, w1_ref[s])
            part = p_s if part is None else part + p_s

        @pl.when(f == 0)
        def _():
            acc_ref[...] = part

        @pl.when(f > 0)
        def _():
            acc_ref[...] += part

        @pl.when((i == 0) & (f == 0))
        def _():
            dg_ref[...] = jnp.zeros_like(dg_ref)

        @pl.when(f == nf - 1)
        def _():
            xt = x2_ref[...]
            dx, dg = _rms_bwd(acc_ref[...], xt, _rms(xt), g_ref[...])
            dx2_ref[...] = dx3_ref[...] + dx
            dg_ref[...] += dg

    return pl.pallas_call(
        body, name="mlp_bwd", grid=(t // tm, nf),
        in_specs=[pl.BlockSpec((tm, D_MODEL), lambda i, f: (i, 0)),
                  pl.BlockSpec((tm, tf), lambda i, f: (i, f)),
                  pl.BlockSpec((tf // w_1.shape[2], D_MODEL, w_1.shape[2]), lambda i, f: (f, 0, 0)),
                  pl.BlockSpec((tf, D_MODEL), lambda i, f: (f, 0)),
                  pl.BlockSpec((tm, D_MODEL), lambda i, f: (i, 0)),
                  pl.BlockSpec((1, D_MODEL), lambda i, f: (0, 0))],
        out_specs=[pl.BlockSpec((tm, tf), lambda i, f: (i, f)),
                   pl.BlockSpec((tm, D_MODEL), lambda i, f: (i, 0)),
                   pl.BlockSpec((1, D_MODEL), lambda i, f: (0, 0))],
        out_shape=[jax.ShapeDtypeStruct((t, D_FF), BF16), jax.ShapeDtypeStruct((t, D_MODEL), F32),
                   jax.ShapeDtypeStruct((1, D_MODEL), F32)],
        scratch_shapes=[pltpu.VMEM((tm, D_MODEL), F32)],
        compiler_params=_params(("arbitrary", "arbitrary")),
    )(dx3, a, w_1, w_2, x2, g_mlp)


def _wgrad(name, a, b, tka, tn, tm, square=False, col_shards=False):
    t, ka = a.shape
    n = b.shape[1]
    nk = t // tm

    def body(a_ref, b_ref, o_ref, acc_ref):
        at = a_ref[...].astype(BF16)
        if square:
            at = at * at
        part = _dot_tn(at, b_ref[...].astype(BF16))
        k = pl.program_id(2)

        @pl.when(k == 0)
        def _():
            acc_ref[...] = part

        @pl.when(k > 0)
        def _():
            acc_ref[...] += part

        @pl.when(k == nk - 1)
        def _():
            if col_shards:
                for s in range(tn // sw):
                    o_ref[s] = acc_ref[:, s * sw:(s + 1) * sw].astype(BF16)
            else:
                o_ref[...] = acc_ref[...].astype(BF16)

    if col_shards:
        sw = n // N_DEV
        out_spec = pl.BlockSpec((tn // sw, tka, sw), lambda p, q, k: (q, p, 0))
        out_shape = jax.ShapeDtypeStruct((N_DEV, ka, sw), BF16)
    else:
        out_spec = pl.BlockSpec((tka, tn), lambda p, q, k: (p, q))
        out_shape = jax.ShapeDtypeStruct((ka, n), BF16)
    return pl.pallas_call(
        body, name=name, grid=(ka // tka, n // tn, nk),
        in_specs=[pl.BlockSpec((tm, tka), lambda p, q, k: (k, p)),
                  pl.BlockSpec((tm, tn), lambda p, q, k: (k, q))],
        out_specs=out_spec, out_shape=out_shape,
        scratch_shapes=[pltpu.VMEM((tka, tn), F32)],
        compiler_params=_params(("arbitrary", "arbitrary", "arbitrary")),
    )(a, b)


def _cross_bwd(dx2, x1, q, kv, w_cq, w_co, g_cross, tm, dep=None):
    t = x1.shape[0]
    m = kv.shape[0]

    def body(dx2_ref, x1_ref, q_ref, kv_ref, wq_ref, wo_ref, g_ref, dq_ref, dx1_ref, dkv_ref, dg_ref):
        @pl.when(pl.program_id(0) == 0)
        def _():
            dkv_ref[...] = jnp.zeros_like(dkv_ref)
            dg_ref[...] = jnp.zeros_like(dg_ref)

        do = _dot_nt(dx2_ref[...].astype(BF16), wo_ref[...]).astype(BF16)
        q = q_ref[...]
        dqs = []
        for h in range(X_HEADS):
            hs = slice(h * X_HEAD_DIM, (h + 1) * X_HEAD_DIM)
            vs = slice(D_MODEL + h * X_HEAD_DIM, D_MODEL + (h + 1) * X_HEAD_DIM)
            p = _cross_probs(q, kv_ref, h)
            dp = _dot_nt(do[:, hs], kv_ref[:, vs])
            ds = (p * (dp - jnp.sum(dp * p, axis=-1, keepdims=True))).astype(BF16)
            dqs.append(_dot(ds, kv_ref[:, hs]))
            dkv_ref[:, hs] += _dot_tn(ds, q[:, hs])
            dkv_ref[:, vs] += _dot_tn(p.astype(BF16), do[:, hs])
        dq = (jnp.concatenate(dqs, axis=1) * X_SCALE).astype(BF16)
        dq_ref[...] = dq
        xt = x1_ref[...]
        dx, dg = _rms_bwd(_dot_nt(dq, wq_ref[...]), xt, _rms(xt), g_ref[...])
        dx1_ref[...] = dx2_ref[...] + dx
        dg_ref[...] += dg

    row = lambda w: pl.BlockSpec((tm, w), lambda i: (i, 0))
    full = lambda a, b: pl.BlockSpec((a, b), lambda i: (0, 0))
    return pl.pallas_call(
        _with_dep(body, 7, dep), name="cross_bwd", grid=(t // tm,),
        in_specs=[row(D_MODEL), row(D_MODEL), row(D_MODEL), full(m, 2 * D_MODEL), full(D_MODEL, D_MODEL),
                  full(D_MODEL, D_MODEL), full(1, D_MODEL)] + _dep_spec(dep),
        out_specs=[row(D_MODEL), row(D_MODEL), full(m, 2 * D_MODEL), full(1, D_MODEL)],
        out_shape=[jax.ShapeDtypeStruct((t, D_MODEL), BF16), jax.ShapeDtypeStruct((t, D_MODEL), F32),
                   jax.ShapeDtypeStruct((m, 2 * D_MODEL), F32), jax.ShapeDtypeStruct((1, D_MODEL), F32)],
        compiler_params=_params(("arbitrary",)),
    )(dx2, x1, q, kv, w_cq, w_co, g_cross, *_dep_arg(dep))


def _memkv_bwd(dkv, mn, mem, w_ckv, g_mem):
    ws = w_ckv.shape[2]

    def body(dkv_ref, mn_ref, mem_ref, w_ref, g_ref, dw_ref, dg_ref):
        mn = mn_ref[...]
        dmn = jnp.zeros(mn.shape, F32)
        for j in range(N_DEV):
            dkvb = dkv_ref[:, j * ws:(j + 1) * ws].astype(BF16)
            dw_ref[j] = _dot_tn(mn, dkvb).astype(BF16)
            dmn = dmn + _dot_nt(dkvb, w_ref[j])
        xt = mem_ref[...]
        dg_ref[...] = jnp.sum(dmn * xt * _rms(xt), axis=0, keepdims=True)

    return pl.pallas_call(
        body, name="memkv_bwd",
        out_shape=[jax.ShapeDtypeStruct(w_ckv.shape, BF16), jax.ShapeDtypeStruct((1, D_MODEL), F32)],
        compiler_params=_params(),
    )(dkv, mn, mem, w_ckv, g_mem)


def _merge_bwd(dx1, ya, yb, gts, w_out, w_g, tm):
    t = dx1.shape[0]

    def body(dx1_ref, ya_ref, yb_ref, g_ref, wo_ref, wg_ref, dg_ref, dhp_ref, dya_ref, dyb_ref, db_ref):
        @pl.when(pl.program_id(0) == 0)
        def _():
            db_ref[...] = jnp.zeros_like(db_ref)

        dm = _dot_nt(dx1_ref[...].astype(BF16), wo_ref[...])
        ga = g_ref[:, :D_MODEL].astype(F32)
        gb = g_ref[:, D_MODEL:].astype(F32)
        dya_ref[...] = (dm * ga).astype(BF16)
        dyb_ref[...] = (dm * gb).astype(BF16)
        dpa = dm * ya_ref[...].astype(F32) * ga * (1.0 - ga)
        dpb = dm * yb_ref[...].astype(F32) * gb * (1.0 - gb)
        dpre = jnp.concatenate([dpa, dpb], axis=1)
        db_ref[...] += jnp.sum(dpre, axis=0, keepdims=True)
        dpreb = dpre.astype(BF16)
        dg_ref[...] = dpreb
        dhp_ref[...] = _dot_nt(dpreb, wg_ref[...])

    row = lambda w: pl.BlockSpec((tm, w), lambda i: (i, 0))
    once = lambda a, b: pl.BlockSpec((a, b), lambda i: (0, 0), pipeline_mode=pl.Buffered(1))
    sds = jax.ShapeDtypeStruct
    return pl.pallas_call(
        body, name="merge_bwd", grid=(t // tm,),
        in_specs=[row(D_MODEL), row(D_MODEL), row(D_MODEL), row(GATE_WIDTH),
                  once(D_MODEL, D_MODEL), once(D_MODEL, GATE_WIDTH)],
        out_specs=[row(GATE_WIDTH), row(D_MODEL), row(D_MODEL), row(D_MODEL),
                   pl.BlockSpec((1, GATE_WIDTH), lambda i: (0, 0))],
        out_shape=[sds((t, GATE_WIDTH), BF16), sds((t, D_MODEL), F32), sds((t, D_MODEL), BF16),
                   sds((t, D_MODEL), BF16), sds((1, GATE_WIDTH), F32)],
        compiler_params=_params(("arbitrary",)),
    )(dx1, ya, yb, gts, w_out, w_g)


def _combine_bwd(dya, dyb, oa, ob, l0, l1, l2, lb, sink_row, w_a, w_b, tm):
    t = dya.shape[0]

    def body(dya_ref, dyb_ref, oa_ref, ob_ref, l0_ref, l1_ref, l2_ref, lb_ref, sk_ref, wa_ref, wb_ref,
             do0_ref, do1_ref, do2_ref, c0_ref, c1_ref, c2_ref, dob_ref, cb_ref, dsk_ref, so_ref, sl_ref):
        @pl.when(pl.program_id(0) == 0)
        def _():
            dsk_ref[...] = jnp.zeros_like(dsk_ref)

        doa = _dot_nt(dya_ref[...], wa_ref[...])
        dob = _dot_nt(dyb_ref[...], wb_ref[...])
        dsum = _head_sums(doa * oa_ref[...].astype(F32), _head_gather(256, True))
        a0, a1, a2 = _alphas(l0_ref[...], _interleave(l1_ref, sl_ref), _interleave(l2_ref, sl_ref))
        c0_ref[...] = a0 * dsum
        spread = _dil_head_spread()
        do0_ref[...] = _head_scale(doa, a0, spread).astype(BF16)
        for al, do_ref, c_ref in ((a1, do1_ref, c1_ref), (a2, do2_ref, c2_ref)):
            _deinterleave(al * dsum, sl_ref, c_ref, F32)
            _deinterleave(_head_scale(doa, al, spread), so_ref, do_ref, BF16)
        dob_ref[...] = dob.astype(BF16)
        cb = _head_sums(dob * ob_ref[...], _head_gather(128, False))
        cb_ref[...] = cb
        lane = lax.broadcasted_iota(jnp.int32, cb.shape, 1)
        psink = jnp.where(lane < 8, jnp.exp(sk_ref[...] - lb_ref[...]), 0.0)
        dsk_ref[...] += jnp.sum(-psink * cb, axis=0, keepdims=True)

    row = lambda w: pl.BlockSpec((tm, w), lambda i: (i, 0))
    full = lambda a, b: pl.BlockSpec((a, b), lambda i: (0, 0))
    sds = jax.ShapeDtypeStruct
    d1, d2 = l1.shape[0], l2.shape[0]
    res = lambda d, w: pl.BlockSpec((d, tm // d, w), lambda i: (0, i, 0))
    return pl.pallas_call(
        body, name="combine_bwd", grid=(t // tm,),
        in_specs=[row(D_MODEL), row(D_MODEL), row(512), row(512),
                  row(256), _res_spec(l1, tm), _res_spec(l2, tm), row(128), full(1, 128),
                  full(512, D_MODEL), full(512, D_MODEL)],
        out_specs=[row(512), res(d1, 512), res(d2, 512), row(256), res(d1, 256), res(d2, 256),
                   row(512), row(128), full(1, 128)],
        out_shape=[sds((t, 512), BF16), sds((d1, t // d1, 512), BF16),
                   sds((d2, t // d2, 512), BF16), sds((t, 256), F32), sds((d1, t // d1, 256), F32),
                   sds((d2, t // d2, 256), F32), sds((t, 512), BF16),
                   sds((t, 128), F32), sds((1, 128), F32)],
        scratch_shapes=[_lane_scratch(tm, 512), _lane_scratch(tm, 256)],
        compiler_params=_params(("arbitrary",)),
    )(dya, dyb, oa, ob, l0, l1, l2, lb, sink_row, w_a, w_b)


def _attn_bwd(name, pv, dov, lsev, cv, cosv, sinv, swa, tq, dep=None):
    d, ls = pv.shape[0], pv.shape[1]
    n, nsb = ls // tq, tq // BAND
    pairs = _attn_layout(swa)
    ncol = 1 if swa else 2
    ow = 128 * len(pairs)

    kv_slots = sorted({(ko, vo) for _, ko, vo in pairs})

    def body(cur_ref, tail_ref, do_ref, lse_ref, c_ref, cos_ref, sin_ref, out_ref, acc_ref, carry_ref, acct_ref):
        i = pl.program_id(2)
        blk_i = n - 1 - i
        acc_ref[...] = jnp.zeros_like(acc_ref)
        acct_ref[...] = jnp.zeros_like(acct_ref)

        @pl.when(i == 0)
        def _():
            carry_ref[...] = jnp.zeros_like(carry_ref)

        qk_a, v_a = _head_a_masks(BAND)
        dim = lax.broadcasted_iota(jnp.int32, (128, BAND), 0)
        qk_at, v_at = (dim % HEAD_DIM) < HEAD_DIM // 2, dim < HEAD_DIM
        for s in range(nsb):
            mask = _band_mask(blk_i, s)
            mask2 = jnp.concatenate([mask, mask], axis=0)
            rows = slice(s * BAND, (s + 1) * BAND)
            kcols = slice(s * BAND, (s + 2) * BAND)
            for j, (qo, ko, vo) in enumerate(pairs):
                slot = kv_slots.index((ko, vo))
                kk = _kv_rows(cur_ref, tail_ref, s, ko)
                vv = _kv_rows(cur_ref, tail_ref, s, vo)
                q, do = cur_ref[rows, qo:qo + 128], do_ref[rows, j * 128:(j + 1) * 128]
                q2, do2 = _stack_heads(q, qk_a), _stack_heads(do, v_a)
                col2 = lambda ref: jnp.concatenate([ref[rows, 2 * j:2 * j + 1], ref[rows, 2 * j + 1:2 * j + 2]], axis=0)
                sc = _dot_nt(q2, kk)
                p = jnp.exp(jnp.where(mask2, sc, -jnp.inf) - col2(lse_ref))
                dp = _dot_nt(do2, vv)
                ds = (p * (dp - col2(c_ref))).astype(BF16)
                dq2 = _dot(ds, kk)
                acc_ref[BAND + s * BAND:BAND + (s + 1) * BAND, qo:qo + 128] += jnp.where(qk_a, dq2[:BAND], dq2[BAND:])
                acct_ref[2 * slot, :, kcols] += _dot(_stack_heads_t(q.T, qk_at), ds)
                acct_ref[2 * slot + 1, :, kcols] += _dot(_stack_heads_t(do.T, v_at), p.astype(BF16))
        for slot, (ko, vo) in enumerate(kv_slots):
            acc_ref[:, ko:ko + 128] += acct_ref[2 * slot].T
            acc_ref[:, vo:vo + 128] += acct_ref[2 * slot + 1].T

        last = acc_ref[tq:, :] + carry_ref[...]
        fin = last if tq == BAND else jnp.concatenate([acc_ref[BAND:tq, :], last], axis=0)
        out_ref[...] = _rope(fin, cos_ref[...], sin_ref[...], swa, -1).astype(BF16)
        carry_ref[...] = acc_ref[0:BAND, :]

    rev = lambda i: n - 1 - i
    blk = lambda rows, w, row_of: pl.BlockSpec((None, rows, w), lambda r, cb, i: (r, row_of(i), cb))
    tab = pl.BlockSpec((None, tq, 128), lambda r, cb, i: (r, rev(i), 0))
    return pl.pallas_call(
        _with_dep(body, 7, dep), name=name, grid=(d, ncol, n),
        in_specs=[blk(tq, PBLK, rev), blk(BAND, PBLK, lambda i: jnp.maximum(rev(i) * nsb - 1, 0)),
                  blk(tq, ow, rev), blk(tq, 128, rev), blk(tq, 128, rev), tab, tab] + _dep_spec(dep),
        out_specs=blk(tq, PBLK, rev),
        out_shape=jax.ShapeDtypeStruct((d, ls, ncol * PBLK), BF16),
        scratch_shapes=[pltpu.VMEM((tq + BAND, PBLK), F32), pltpu.VMEM((BAND, PBLK), F32),
                        pltpu.VMEM((2 * len(kv_slots), 128, tq + BAND), F32)],
        compiler_params=_params(("arbitrary", "arbitrary", "arbitrary")),
    )(pv, pv, dov, lsev, cv, cosv, sinv, *_dep_arg(dep))


def _dx(dp0, dp1, dp2, dpb, w_p, dh_part, dx1, x, g_mix, tm, dep=None):
    t = x.shape[0]
    gw = 2 * PBLK

    def body(dp0_ref, dp1_ref, dp2_ref, dpb_ref, w_ref, dhp_ref, dx1_ref, x_ref, g_ref, gx_ref, dg_ref,
             dpt_ref, scr_ref):
        @pl.when(pl.program_id(0) == 0)
        def _():
            dg_ref[...] = jnp.zeros_like(dg_ref)

        dpt_ref[:, 0:gw] = dp0_ref[...]
        dpt_ref[:, gw:2 * gw] = _interleave(dp1_ref, scr_ref).astype(BF16)
        dpt_ref[:, 2 * gw:3 * gw] = _interleave(dp2_ref, scr_ref).astype(BF16)
        dpt_ref[:, 3 * gw:] = dpb_ref[...]
        dh = _dot_nt(dpt_ref[...], w_ref[...]) + dhp_ref[...]
        xt = x_ref[...]
        dx, dg = _rms_bwd(dh, xt, _rms(xt), g_ref[...])
        gx_ref[...] = dx1_ref[...] + dx
        dg_ref[...] += dg

    row = lambda w: pl.BlockSpec((tm, w), lambda i: (i, 0))
    full = lambda a, b: pl.BlockSpec((a, b), lambda i: (0, 0))
    return pl.pallas_call(
        _with_dep(body, 9, dep), name="dx", grid=(t // tm,),
        in_specs=[row(gw), _res_spec(dp1, tm), _res_spec(dp2, tm), row(PBLK),
                  pl.BlockSpec((D_MODEL, P_WIDTH), lambda i: (0, 0), pipeline_mode=pl.Buffered(1)),
                  row(D_MODEL), row(D_MODEL), row(D_MODEL), full(1, D_MODEL)] + _dep_spec(dep),
        out_specs=[row(D_MODEL), full(1, D_MODEL)],
        out_shape=[jax.ShapeDtypeStruct((t, D_MODEL), F32), jax.ShapeDtypeStruct((1, D_MODEL), F32)],
        scratch_shapes=[pltpu.VMEM((tm, P_WIDTH), BF16), _lane_scratch(tm, gw)],
        compiler_params=_params(("arbitrary",)),
    )(dp0, dp1, dp2, dpb, w_p, dh_part, dx1, x, g_mix, *_dep_arg(dep))


MESH = pl.DeviceIdType.MESH
HBM_SPEC = pl.BlockSpec(memory_space=pltpu.HBM)
VMEM_SPEC = pl.BlockSpec(memory_space=pltpu.VMEM)


def _all_gather(xp, act, g, tm):
    t = act.shape[0]
    d1, d2 = DIL_GROUPS[1][1], DIL_GROUPS[2][1]

    def body(x_ref, act_ref, g_ref, out_ref, h_ref, h1_ref, h2_ref, send_sems, recv_sems, local_sem, hf_ref):
        x, y, c = lax.axis_index("x"), lax.axis_index("y"), lax.axis_index("c")
        me, sibling = (x, y, c), (x, y, 1 - c)
        chips = [(1 - x, y), (x, 1 - y), (1 - x, 1 - y)]

        def rows(px, py, pc):
            return out_ref.at[4 * px + 2 * py + pc]

        def copy(k, block, to, src=None):
            return pltpu.make_async_remote_copy(
                src_ref=rows(*block) if src is None else src, dst_ref=rows(*block),
                send_sem=send_sems.at[k], recv_sem=recv_sems.at[k], device_id=to, device_id_type=MESH)

        mine = pltpu.make_async_copy(x_ref, rows(*me), local_sem)
        mine.start()
        first = [copy(0, me, sibling, src=x_ref)]
        first += [copy(1 + j, me, (*chip, c), src=x_ref) for j, chip in enumerate(chips)]
        for cp in first:
            cp.start()

        def norm(a_blk, h_blk, h1_blk, h2_blk):
            xt = a_blk[...]
            hf = xt * _rms(xt) * g_ref[...]
            h_blk[...] = hf.astype(BF16)
            _deinterleave(hf, hf_ref, h1_blk, BF16)
            _deinterleave(hf, hf_ref, h2_blk, BF16)

        res = lambda d: pl.BlockSpec((d, tm // d, D_MODEL), lambda i: (0, i, 0))
        row = pl.BlockSpec((tm, D_MODEL), lambda i: (i, 0))
        pltpu.emit_pipeline(norm, grid=(t // tm,), in_specs=[row], out_specs=[row, res(d1), res(d2)])(
            act_ref, h_ref, h1_ref, h2_ref)

        passed = [copy(4 + j, (*chip, c), sibling) for j, chip in enumerate(chips)]
        for j, chip in enumerate(chips):
            copy(1 + j, (*chip, c), me).wait_recv()
            passed[j].start()
        copy(0, sibling, me).wait_recv()
        for j, chip in enumerate(chips):
            copy(4 + j, (*chip, 1 - c), me).wait_recv()
        for cp in first + passed:
            cp.wait_send()
        mine.wait()

    sds = jax.ShapeDtypeStruct
    return pl.pallas_call(
        body, name="all_gather",
        out_shape=[sds((N_DEV,) + xp.shape, xp.dtype), sds((t, D_MODEL), BF16),
                   sds((d1, t // d1, D_MODEL), BF16), sds((d2, t // d2, D_MODEL), BF16)],
        in_specs=[HBM_SPEC, HBM_SPEC, VMEM_SPEC], out_specs=[HBM_SPEC] * 4,
        scratch_shapes=[pltpu.SemaphoreType.DMA((7,)), pltpu.SemaphoreType.DMA((7,)), pltpu.SemaphoreType.DMA,
                        _lane_scratch(tm, D_MODEL)],
        compiler_params=pltpu.CompilerParams(vmem_limit_bytes=VMEM_LIMIT),
    )(xp, act, g)


def _peers():
    x, y, c = lax.axis_index("x"), lax.axis_index("y"), lax.axis_index("c")
    out = []
    for k in range(1, N_DEV):
        px = 1 - x if k & 4 else x
        py = 1 - y if k & 2 else y
        pc = 1 - c if k & 1 else c
        out.append((k, (px, py, pc), 4 * px + 2 * py + pc))
    return out


def _my_index():
    return 4 * lax.axis_index("x") + 2 * lax.axis_index("y") + lax.axis_index("c")


SEM_SPEC = pl.BlockSpec(memory_space=pltpu.SEMAPHORE)
ANY_SPEC = pl.BlockSpec(memory_space=pl.ANY)
_SPLIT_PARAMS = pltpu.CompilerParams(has_side_effects=pltpu.SideEffectType.DATAFLOW_SIDE_EFFECTING)


def _split_copies(gather, src_refs, land_refs, send_sems, recv_sems):
    me_idx = _my_index()
    out = []
    for a, (src_ref, land_ref) in enumerate(zip(src_refs, land_refs)):
        for k, peer, peer_idx in _peers():
            if gather:
                src, dst = src_ref, land_ref.at[me_idx]
            else:
                src, dst = src_ref.at[peer_idx], land_ref.at[k - 1]
            out.append(pltpu.make_async_remote_copy(
                src_ref=src, dst_ref=dst, send_sem=send_sems.at[7 * a + k - 1], recv_sem=recv_sems.at[7 * a + k - 1],
                device_id=peer, device_id_type=MESH))
    return out


def _split_start(name, gather, srcs):
    n = len(srcs)

    def body(*refs):
        send_sems, recv_sems = refs[n], refs[n + 1]
        for cp in _split_copies(gather, refs[:n], refs[2 * n + 2:3 * n + 2], send_sems, recv_sems):
            cp.start()
        token = refs[-1]
        token[...] = jnp.zeros_like(token)

    lands = [pltpu.HBM((N_DEV,) + a.shape if gather else (N_DEV - 1,) + a.shape[1:], a.dtype) for a in srcs]
    return pl.pallas_call(
        body, name=name,
        out_shape=(pltpu.SemaphoreType.DMA((7 * n,)), pltpu.SemaphoreType.DMA((7 * n,)),
                   *[pltpu.HBM(a.shape, a.dtype) for a in srcs], *lands, jax.ShapeDtypeStruct((8, 128), F32)),
        in_specs=(HBM_SPEC,) * n, out_specs=(SEM_SPEC, SEM_SPEC) + (HBM_SPEC,) * (2 * n) + (VMEM_SPEC,),
        input_output_aliases={i: 2 + i for i in range(n)}, compiler_params=_SPLIT_PARAMS,
    )(*[pltpu.with_memory_space_constraint(a, pltpu.HBM) for a in srcs])


def _split_wait(name, gather, started, after):
    send_sems, recv_sems, bufs = started[0], started[1], started[2:-1]
    n = len(bufs) // 2

    def body(*refs):
        for cp in _split_copies(gather, refs[:n], refs[n:2 * n], refs[2 * n], refs[2 * n + 1]):
            cp.wait_send()
            cp.wait_recv()

    out = pl.pallas_call(
        body, name=name, out_shape=tuple(pltpu.HBM(a.shape, a.dtype) for a in bufs),
        in_specs=(HBM_SPEC,) * (2 * n) + (SEM_SPEC, SEM_SPEC, ANY_SPEC), out_specs=(HBM_SPEC,) * (2 * n),
        input_output_aliases={i: i for i in range(2 * n)}, compiler_params=_SPLIT_PARAMS,
    )(*bufs, send_sems, recv_sems, after)
    return out[:n], out[n:]


def _adam_update(g, w, m, v):
    nm = ADAM_B1 * m + (1.0 - ADAM_B1) * g
    nv = ADAM_B2 * v + (1.0 - ADAM_B2) * (g * g)
    m_hat = nm / (1.0 - ADAM_B1 ** ADAM_STEP)
    v_hat = nv / (1.0 - ADAM_B2 ** ADAM_STEP)
    return -ADAM_LR * (m_hat / (jnp.sqrt(v_hat) + ADAM_EPS) + ADAM_WD * w), nm, nv


def _adamw(name, me, sent, got, w, m, v, tr):
    r, c = w.shape

    def body(me_ref, own_ref, got_ref, w_ref, m_ref, v_ref, g_ref, d_ref, nm_ref, nv_ref):
        g = own_ref[...].astype(F32)
        for k in range(N_DEV - 1):
            g = g + got_ref[k].astype(F32)
        g_ref[...] = g
        d_ref[...], nm_ref[...], nv_ref[...] = _adam_update(g, w_ref[...], m_ref[...], v_ref[...])

    blk = pl.BlockSpec((tr, c), lambda i, me_ref: (i, 0))
    return pl.pallas_call(
        body, name=name,
        grid_spec=pltpu.PrefetchScalarGridSpec(
            num_scalar_prefetch=1, grid=(r // tr,),
            in_specs=[pl.BlockSpec((None, tr, c), lambda i, me_ref: (me_ref[0], i, 0)),
                      pl.BlockSpec((N_DEV - 1, tr, c), lambda i, me_ref: (0, i, 0)), blk, blk, blk],
            out_specs=[blk] * 4),
        out_shape=[jax.ShapeDtypeStruct((r, c), F32)] * 4,
        compiler_params=_params(("arbitrary",)),
    )(me, sent, got, w, m, v)


def _adamw_small(srecv, ws, ms, vs):
    nv_ = len(ws)

    def body(*refs):
        s_ref = refs[0]
        ins, outs = refs[1:1 + 3 * nv_], refs[1 + 3 * nv_:]
        g_all = s_ref[0]
        for k in range(1, N_DEV):
            g_all = g_all + s_ref[k]
        for i in range(nv_):
            n = ins[i].shape[1]
            g = g_all[i:i + 1, :n]
            d, nm, nv = _adam_update(g, ins[i][...], ins[nv_ + i][...], ins[2 * nv_ + i][...])
            outs[i][...], outs[nv_ + i][...], outs[2 * nv_ + i][...], outs[3 * nv_ + i][...] = g, d, nm, nv
        outs[-1][...] = g_all[nv_:nv_ + 1, :128]

    shapes = [jax.ShapeDtypeStruct(a.shape, F32) for a in ws]
    res = pl.pallas_call(body, name="adamw_small", out_shape=shapes * 4 + [jax.ShapeDtypeStruct((1, 128), F32)],
                         compiler_params=_params())(srecv, *ws, *ms, *vs)
    return [res[k * nv_:(k + 1) * nv_] for k in range(4)], res[-1]


def _cols_from_shards(a):
    return jnp.swapaxes(a, 0, 1).reshape(a.shape[1], a.shape[0] * a.shape[2])


def _shards_from_cols(a):
    return jnp.swapaxes(a.reshape(a.shape[0], N_DEV, a.shape[1] // N_DEV), 0, 1)


def _shards_from_rows(a):
    return a.reshape(N_DEV, a.shape[0] // N_DEV, a.shape[1])


def _pair_lanes(a):
    lead = a.shape[:-1]
    return a.reshape(lead + (2, 2, HEAD_DIM // 2)).swapaxes(-3, -2).reshape(lead + (128,))


def _split_w_in(w_in):
    rows = w_in.shape[0]
    dil = w_in[:, :3 * DIL_WIDTH].reshape(rows, 3, 3, 4, 128)
    dil = jnp.concatenate([_pair_lanes(dil[:, :2]), dil[:, 2:]], axis=1)
    dil = dil.transpose(0, 2, 3, 1, 4).reshape(rows, 3 * DIL_WIDTH)
    o = 3 * DIL_WIDTH
    qb = w_in[:, o:o + SWA_Q_WIDTH].reshape(rows, 2, 4, HEAD_DIM).transpose(0, 2, 1, 3).reshape(rows, 4, 128)
    qb = _pair_lanes(qb).reshape(rows, SWA_Q_WIDTH)
    kb = _pair_lanes(w_in[:, o + SWA_Q_WIDTH:o + SWA_Q_WIDTH + SWA_KV_WIDTH])
    vb = w_in[:, o + SWA_Q_WIDTH + SWA_KV_WIDTH:P_WIDTH]
    return jnp.concatenate([dil, qb, kb, vb], axis=1)


def _merge_w_in(dw_p, dw_g):
    rows = dw_p.shape[0]
    dil = dw_p[:, :3 * DIL_WIDTH].reshape(rows, 3, 4, 3, 128).transpose(0, 3, 1, 2, 4)
    dil = jnp.concatenate([_pair_lanes(dil[:, :2]), dil[:, 2:]], axis=1).reshape(rows, 3 * DIL_WIDTH)
    o = 3 * DIL_WIDTH
    qb = _pair_lanes(dw_p[:, o:o + SWA_Q_WIDTH].reshape(rows, 4, 128))
    qb = qb.reshape(rows, 4, 2, HEAD_DIM).transpose(0, 2, 1, 3).reshape(rows, SWA_Q_WIDTH)
    kb = _pair_lanes(dw_p[:, o + SWA_Q_WIDTH:o + SWA_Q_WIDTH + SWA_KV_WIDTH])
    vb = dw_p[:, o + SWA_Q_WIDTH + SWA_KV_WIDTH:]
    return jnp.concatenate([dil, qb, kb, vb, dw_g], axis=1)


def _swa_rows(w_b):
    return w_b.reshape(2, 4, HEAD_DIM, -1).transpose(1, 0, 2, 3).reshape(SWA_Q_WIDTH, -1)


def _swa_rows_inv(dw_b):
    return dw_b.reshape(4, 2, HEAD_DIM, -1).transpose(1, 0, 2, 3).reshape(SWA_Q_WIDTH, -1)


def _rope_tables(pos):
    half = HEAD_DIM // 2
    inv = ROPE_THETA ** (-jnp.arange(half, dtype=F32) / half)
    ang = pos.astype(F32)[:, None] * jnp.tile(inv, 4)
    sign = jnp.repeat(jnp.array([-1.0, 1.0], F32), 2 * half)
    return jnp.cos(ang), jnp.sin(ang) * sign


def _local_step(x, hs, mem, pos, target, w_in, dep, rest_weights, on_grads, g_mix, g_cross, g_mem, g_mlp, g_final, sink):
    t = x.shape[0]
    tm = min(512, t)
    tq = 1024
    tw = min(2048, t)
    w_p = _split_w_in(w_in)
    cos, sin = lax.optimization_barrier(_rope_tables(pos))
    sink_row = jnp.pad(sink.reshape(2, 4).T.reshape(1, 8), ((0, 0), (0, 120)))
    tabs = [(cos[None], sin[None])]
    for _, d in DIL_GROUPS[1:]:
        tabs.append(tuple(a.reshape(t // d, d, 128).swapaxes(0, 1) for a in (cos, sin)))
    tabs.append(tabs[0])

    h, h1, h2 = hs
    p0, p1, p2, pb = _inproj(h, h1, h2, w_p, [(cos, sin), tabs[1], tabs[2]], tm, dep)
    ps = [p0[None], p1, p2, pb[None]]
    outs, lses = [], []
    for gi, pv in enumerate(ps):
        res = _attn_fwd(f"attn_fwd{gi}", pv, gi == 3, sink_row[0, :8], min(tq, pv.shape[1]))
        outs.append(res[0])
        lses.append(res[1])
    o0, l0, ob, lb, ob32 = outs[0][0], lses[0][0], outs[3][0], lses[3][0], res[2][0]
    wts = rest_weights(lb)
    w_b = _swa_rows(wts["w_branch_b"])
    tf = 2048
    w_g = wts["w_g"]
    gts = _gates(h, w_g, wts["b_gate"].reshape(1, GATE_WIDTH), min(1024, t), 1024)
    oa, ya, yb, merged, x1, hc = _mix(o0, outs[1], outs[2], l0, lses[1], lses[2], ob, gts, x,
                                      wts["w_branch_a"], w_b, wts["w_out"], g_cross, tm)
    mn, kv = _memkv(mem, g_mem, wts["w_ckv"])
    q, o, x2, hm = _cross(hc, x1, kv, wts["w_cq"], wts["w_co"], g_mlp, tm)
    a, dx3, loss, dg_final = _mlp(hm, x2, wts["w_1"], wts["w_2"], g_final.reshape(1, D_MODEL), target, tm, tf)

    grads = {}
    dz, dx2, dg_mlp = _mlp_bwd(dx3, a, wts["w_1"], wts["w_2"], x2, g_mlp, tm, tf)
    grads["w_2"] = _shards_from_rows(_wgrad("dw_2", a, dx3, 1024, 1024, tw, square=True))
    grads["w_1"] = _wgrad("dw_1", hm, dz, 1024, 1024, tw, col_shards=True)
    dep = on_grads(GROUP_A, grads)
    dq, dx1, dkv, dg_cross = _cross_bwd(dx2, x1, q, kv, wts["w_cq"], wts["w_co"], g_cross, tm, dep)
    grads["w_co"] = _shards_from_rows(_wgrad("dw_co", o, dx2, 1024, 1024, tw))
    grads["w_cq"] = _shards_from_rows(_wgrad("dw_cq", hc, dq, 1024, 1024, tw))
    grads["w_ckv"], dg_mem = _memkv_bwd(dkv, mn, mem, wts["w_ckv"], g_mem)
    dgt, dh_part, dya, dyb, db_gate = _merge_bwd(dx1, ya, yb, gts, wts["w_out"], w_g, tm)
    do0, do1, do2, c0, c1, c2, dob, cb, dsink = _combine_bwd(
        dya, dyb, oa, ob32, l0, lses[1], lses[2], lb, sink_row, wts["w_branch_a"], w_b, tm)
    grads["w_out"] = _shards_from_rows(_wgrad("dw_out", merged, dx1, 1024, 1024, tw))
    grads["w_branch_a"] = _shards_from_cols(_wgrad("dw_a", oa, dya, 512, 1024, tw))
    grads["w_branch_b"] = _shards_from_cols(_swa_rows_inv(_wgrad("dw_b", ob, dyb, 512, 1024, tw)))
    grads["b_gate"] = _shards_from_cols(db_gate.reshape(2, D_MODEL)).astype(BF16)
    dep = on_grads(GROUP_B, grads)
    dw_g = _wgrad("dw_g", h, dgt, 1024, 1024, tw)
    dps = []
    for gi, (pv, do_g, c_g) in enumerate(zip(ps, (do0[None], do1, do2, dob[None]), (c0[None], c1, c2, cb[None]))):
        dps.append(_attn_bwd(f"attn_bwd{gi}", pv, do_g, lses[gi], c_g, tabs[gi][0], tabs[gi][1], gi == 3,
                             min(tq, pv.shape[1]),
                             dep if gi == 0 else None))
    dw_p = jnp.concatenate(
        [_wgrad(f"dw_p{gi}", hh.reshape(t, D_MODEL), dpg.reshape(t, -1), 1024, PBLK, tw)
         for gi, (hh, dpg) in enumerate(zip((h, h1, h2, h), dps))], axis=1)
    grads["w_in"] = _shards_from_cols(_merge_w_in(dw_p, dw_g))
    dep = on_grads(GROUP_C, grads)
    grad_x, dg_mix = _dx(dps[0][0], dps[1], dps[2], dps[3][0], w_p, dh_part, dx1, x, g_mix, tm, dep)
    dsink_heads = dsink[0, :8].reshape(4, 2).T.reshape(8)
    small = {"g_mix": dg_mix[0], "g_cross": dg_cross[0], "g_mem": dg_mem[0], "g_mlp": dg_mlp[0],
             "g_final": dg_final[0], "sink": dsink_heads}
    return loss[0, 0], grad_x, small


def kernel(x, mem, positions, g_mix, w_in, b_gate, sink, w_branch_a, w_branch_b, w_out, g_cross, g_mem, w_cq, w_ckv, w_co, g_mlp, w_1, w_2, g_final, loss_target, m_g_mix, m_w_in, m_b_gate, m_sink, m_w_branch_a, m_w_branch_b, m_w_out, m_g_cross, m_g_mem, m_w_cq, m_w_ckv, m_w_co, m_g_mlp, m_w_1, m_w_2, m_g_final, v_g_mix, v_w_in, v_b_gate, v_sink, v_w_branch_a, v_w_branch_b, v_w_out, v_g_cross, v_g_mem, v_w_cq, v_w_ckv, v_w_co, v_g_mlp, v_w_1, v_w_2, v_g_final):
    local = dict(locals())
    shard = {n: local[n][0] for n in GROUP_A + GROUP_B + GROUP_C}
    me = _my_index()
    me_arr = me.reshape(1).astype(jnp.int32)
    tags = {GROUP_A: "a", GROUP_B: "b", GROUP_C: "c"}

    gathered_w_in, *hs = _all_gather(shard["w_in"].astype(BF16), x[0], g_mix, min(512, x.shape[1]))
    w_in_full = _cols_from_shards(gathered_w_in)
    rest = GROUP_A + GROUP_B

    def gathered(name, started, after):
        srcs, lands = _split_wait(name, True, started, after)
        return [lax.dynamic_update_slice(land, src[None], (me,) + (0,) * src.ndim) for src, land in zip(srcs, lands)]

    gather = _split_start("gather_start", True,
                          [shard[n] if n == "b_gate" else shard[n].astype(BF16) for n in rest])

    def rest_weights(after):
        full = {"w_g": w_in_full[:, P_WIDTH:]}
        for name, a in zip(rest, gathered("gather_wait", gather, after)):
            if name in ("w_1", "w_ckv"):
                full[name] = a
            elif name in _COL_SHARDED:
                full[name] = _cols_from_shards(a)
            else:
                full[name] = a.reshape(N_DEV * a.shape[1], a.shape[2])
        return full

    scatters = {}

    def on_grads(names, grads):
        scatters[names] = _split_start("scatter_start_" + tags[names], False, [grads[n] for n in names])
        return scatters[names][-1]

    loss, grad_x, small = _local_step(
        x[0], hs, mem[0], positions[0], loss_target[0], w_in_full, gather[-1], rest_weights, on_grads,
        g_mix, g_cross, g_mem, g_mlp, g_final, sink[0])

    sp = jnp.stack([small[n] if n != "sink" else jnp.pad(small[n], (0, LANES - 8)) for n in SMALL]
                   + [jnp.pad(loss.reshape(1), (0, LANES - 1)), jnp.zeros((LANES,), F32)])
    small_gather = _split_start("small_start", True, [sp])

    after, updated = small_gather[-1], {}
    for names in (GROUP_A, GROUP_B, GROUP_C):
        sent, got = _split_wait("scatter_wait_" + tags[names], False, scatters[names], after)
        for i, name in enumerate(names):
            outs = _adamw("adamw_" + name, me_arr, sent[i], got[i], shard[name],
                          local["m_" + name][0], local["v_" + name][0], ADAM_ROWS[name])
            updated[name] = [a[None] for a in outs]
            after = outs[3]

    flat = lambda prefix: [local[prefix + n].reshape(1, -1) for n in SMALL]
    outs, loss_row = _adamw_small(gathered("small_wait", small_gather, after)[0], flat(""), flat("m_"), flat("v_"))
    for i, name in enumerate(SMALL):
        updated[name] = [outs[which][i].reshape(local[name].shape) for which in range(4)]

    order = ["g_mix", "w_in", "b_gate", "sink", "w_branch_a", "w_branch_b", "w_out", "g_cross", "g_mem", "w_cq",
             "w_ckv", "w_co", "g_mlp", "w_1", "w_2", "g_final"]
    res = [loss_row[0, 0], grad_x[None]]
    for which in range(4):
        res += [updated[n][which] for n in order]
    return tuple(res)
```

```python
import functools
import math

import jax
import jax.numpy as jnp
from jax import lax
from jax.experimental import pallas as pl
from jax.experimental.pallas import tpu as pltpu

F32 = jnp.float32
BF16 = jnp.bfloat16

D_MODEL = 1024
HEAD_DIM = 64
DIL_GROUPS = ((128, 1), (512, 4), (2048, 16))
ROPE_THETA = 10000.0
X_HEADS = 4
X_HEAD_DIM = D_MODEL // X_HEADS
D_FF = 4 * D_MODEL
EPS = 1e-6
DIL_WIDTH = 1536
SWA_Q_WIDTH = 512
SWA_KV_WIDTH = 128
P_WIDTH = 3 * DIL_WIDTH + SWA_Q_WIDTH + 2 * SWA_KV_WIDTH
GATE_WIDTH = 2 * D_MODEL
IN_WIDTH = P_WIDTH + GATE_WIDTH
BAND = 128
PBLK = 768
Q_SCALE = HEAD_DIM ** -0.5
X_SCALE = X_HEAD_DIM ** -0.5

ADAM_LR = 0.001
ADAM_B1 = 0.9
ADAM_B2 = 0.999
ADAM_EPS = 1e-08
ADAM_WD = 0.01
ADAM_STEP = 10

N_DEV = 8
LANES = 1024
VMEM_LIMIT = 52 * 1024 * 1024

NT = (((1,), (1,)), ((), ()))
TN = (((0,), (0,)), ((), ()))

GROUP_A = ("w_1", "w_2")
GROUP_B = ("w_branch_a", "w_branch_b", "w_out", "w_cq", "w_ckv", "w_co", "b_gate")
GROUP_C = ("w_in",)
_COL_SHARDED = ("w_in", "w_branch_a", "w_branch_b", "w_ckv", "w_1", "b_gate")
ADAM_ROWS = {"w_in": 256, "w_branch_a": 512, "w_branch_b": 512, "w_out": 128, "w_cq": 128, "w_ckv": 512,
             "w_co": 128, "w_1": 256, "w_2": 256, "b_gate": 2}
SMALL = ("g_mix", "g_cross", "g_mem", "g_mlp", "g_final", "sink")


def _params(sem=None):
    return pltpu.CompilerParams(dimension_semantics=sem, vmem_limit_bytes=VMEM_LIMIT)


def _dot(a, b):
    return jnp.dot(a, b, preferred_element_type=F32)


def _dot_nt(a, b):
    return lax.dot_general(a, b, NT, preferred_element_type=F32)


def _dot_tn(a, b):
    return lax.dot_general(a, b, TN, preferred_element_type=F32)


def _rms(xt):
    return lax.rsqrt(jnp.mean(xt * xt, axis=-1, keepdims=True) + EPS)


def _rms_bwd(dh, xt, r, g):
    xn = xt * r
    dxn = dh * g
    dx = r * (dxn - xn * jnp.mean(dxn * xn, axis=-1, keepdims=True))
    return dx, jnp.sum(dh * xn, axis=0, keepdims=True)


def _rope(x, c, s, swa, sign):
    kinds = "qqqqkv" if swa else "qkvqkv"
    cq, sq = c * Q_SCALE, s * (sign * Q_SCALE)
    sk = s * sign if sign != 1 else s
    out = []
    for ci, kind in enumerate(kinds):
        xc = x[:, ci * 128:(ci + 1) * 128]
        if kind == "v":
            out.append(xc)
        elif kind == "q":
            out.append(xc * cq + pltpu.roll(xc, 64, 1) * sq)
        else:
            out.append(xc * c + pltpu.roll(xc, 64, 1) * sk)
    return jnp.concatenate(out, axis=1)


def _lane_scratch(rows, w):
    return pltpu.VMEM((w // 128, rows, 128), F32)


def _deinterleave(val, scr_ref, dst_ref, dtype):
    d, n = dst_ref.shape[0], dst_ref.shape[1]
    nc = val.shape[1] // 128
    for c in range(nc):
        scr_ref[c] = val[:, c * 128:(c + 1) * 128]
    for r in range(d):
        rows = [scr_ref.at[c][pl.ds(r, n, stride=d), :] for c in range(nc)]
        dst_ref[r] = jnp.concatenate(rows, axis=1).astype(dtype)


def _res_spec(a, tm):
    d, w = a.shape[0], a.shape[2]
    return pl.BlockSpec((d, tm // d, w), lambda i: (0, i, 0))


def _interleave(src_ref, scr_ref):
    d, n = src_ref.shape[0], src_ref.shape[1]
    nc = src_ref.shape[2] // 128
    for r in range(d):
        v = src_ref[r].astype(F32)
        for c in range(nc):
            scr_ref.at[c][pl.ds(r, n, stride=d), :] = v[:, c * 128:(c + 1) * 128]
    return jnp.concatenate([scr_ref[c] for c in range(nc)], axis=1)


def _with_dep(body, n_in, dep):
    if dep is None:
        return body
    return lambda *refs: body(*refs[:n_in], *refs[n_in + 1:])


def _dep_spec(dep):
    return [] if dep is None else [pl.BlockSpec(memory_space=pl.ANY)]


def _dep_arg(dep):
    return [] if dep is None else [dep]


def _inproj(h, h1, h2, w_p, tabs, tm, dep=None):
    t = h.shape[0]
    gw = 2 * PBLK
    (cos, sin), (cos1, sin1), (cos2, sin2) = tabs[0], tabs[1], tabs[2]

    def body(h_ref, h1_ref, h2_ref, w_ref, c_ref, s_ref, c1_ref, s1_ref, c2_ref, s2_ref,
             p0_ref, p1_ref, p2_ref, pb_ref):
        rows = lambda ref: ref[...].reshape(tm, ref.shape[-1])
        groups = ((h_ref, c_ref, s_ref, p0_ref), (h1_ref, c1_ref, s1_ref, p1_ref), (h2_ref, c2_ref, s2_ref, p2_ref))
        for gi, (lhs_ref, cc_ref, ss_ref, out_ref) in enumerate(groups):
            lhs, cc, ss = rows(lhs_ref), rows(cc_ref), rows(ss_ref)
            for half in range(2):
                col = gi * gw + half * PBLK
                val = _rope(_dot(lhs, w_ref[:, col:col + PBLK]), cc, ss, False, 1).astype(BF16)
                if out_ref.ndim == 3:
                    out_ref[:, :, half * PBLK:(half + 1) * PBLK] = val.reshape(out_ref.shape[:2] + (PBLK,))
                else:
                    out_ref[:, half * PBLK:(half + 1) * PBLK] = val
        pb_ref[...] = _rope(_dot(h_ref[...], w_ref[:, 3 * gw:]), c_ref[...], s_ref[...], True, 1).astype(BF16)

    d1, d2 = DIL_GROUPS[1][1], DIL_GROUPS[2][1]
    row = lambda w: pl.BlockSpec((tm, w), lambda i: (i, 0))
    res = lambda d, w: pl.BlockSpec((d, tm // d, w), lambda i: (0, i, 0))
    sds = jax.ShapeDtypeStruct
    return pl.pallas_call(
        _with_dep(body, 10, dep), name="inproj", grid=(t // tm,),
        in_specs=[row(D_MODEL), res(d1, D_MODEL), res(d2, D_MODEL),
                  pl.BlockSpec((D_MODEL, P_WIDTH), lambda i: (0, 0), pipeline_mode=pl.Buffered(1)),
                  row(128), row(128), res(d1, 128), res(d1, 128), res(d2, 128), res(d2, 128)] + _dep_spec(dep),
        out_specs=[row(gw), res(d1, gw), res(d2, gw), row(PBLK)],
        out_shape=[sds((t, gw), BF16), sds((d1, t // d1, gw), BF16), sds((d2, t // d2, gw), BF16),
                   sds((t, PBLK), BF16)],
        compiler_params=_params(("arbitrary",)),
    )(h, h1, h2, w_p, cos, sin, cos1, sin1, cos2, sin2, *_dep_arg(dep))


def _gates(h, w_g, b, tm, tn):
    t = h.shape[0]

    def body(h_ref, w_ref, b_ref, o_ref):
        z = _dot(h_ref[...], w_ref[...]) + b_ref[...]
        o_ref[...] = (0.5 * jnp.tanh(0.5 * z) + 0.5).astype(BF16)

    return pl.pallas_call(
        body, name="gates", grid=(t // tm, GATE_WIDTH // tn),
        in_specs=[pl.BlockSpec((tm, D_MODEL), lambda i, j: (i, 0)),
                  pl.BlockSpec((D_MODEL, tn), lambda i, j: (0, j)),
                  pl.BlockSpec((1, tn), lambda i, j: (0, j))],
        out_specs=pl.BlockSpec((tm, tn), lambda i, j: (i, j)),
        out_shape=jax.ShapeDtypeStruct((t, GATE_WIDTH), BF16),
        compiler_params=_params(("arbitrary", "arbitrary")),
    )(h, w_g, b)


def _band_mask(i, s):
    row = lax.broadcasted_iota(jnp.int32, (BAND, 2 * BAND), 0)
    col = lax.broadcasted_iota(jnp.int32, (BAND, 2 * BAND), 1)
    band = (col >= row) & (col <= row + BAND)
    if s == 0:
        band = band & ((col >= BAND) | (i > 0))
    return band


def _head_a_masks(rows):
    lane = lax.broadcasted_iota(jnp.int32, (rows, 128), 1)
    return (lane % HEAD_DIM) < HEAD_DIM // 2, lane < HEAD_DIM


def _stack_heads(x, head_a):
    zero = jnp.zeros_like(x)
    return jnp.concatenate([jnp.where(head_a, x, zero), jnp.where(head_a, zero, x)], axis=0)


def _stack_heads_t(xt, head_a_t):
    zero = jnp.zeros_like(xt)
    return jnp.concatenate([jnp.where(head_a_t, xt, zero), jnp.where(head_a_t, zero, xt)], axis=1)


def _kv_rows(cur_ref, tail_ref, s, off):
    if s == 0:
        return jnp.concatenate([tail_ref[:, off:off + 128], cur_ref[0:BAND, off:off + 128]], axis=0)
    return cur_ref[(s - 1) * BAND:(s + 1) * BAND, off:off + 128]


def _attn_layout(swa):
    if swa:
        return [(128 * j, 512, 640) for j in range(4)]
    return [(0, 128, 256), (384, 512, 640)]


def _attn_fwd(name, pv, swa, sinks, tq):
    d, ls = pv.shape[0], pv.shape[1]
    n, nsb = ls // tq, tq // BAND
    pairs = _attn_layout(swa)
    ncol = 1 if swa else 2
    ow = 128 * len(pairs)

    def body(cur_ref, tail_ref, *rest):
        sink_ref, o_ref, lse_ref, o32_ref = rest if swa else (None,) + rest + (None,)
        i = pl.program_id(2)
        lane = lax.broadcasted_iota(jnp.int32, (BAND, 128), 1)
        qk_a, v_a = _head_a_masks(BAND)
        first = lax.broadcasted_iota(jnp.int32, (2 * BAND, 1), 0) < BAND
        for s in range(nsb):
            mask = _band_mask(i, s)
            mask2 = jnp.concatenate([mask, mask], axis=0)
            rows = slice(s * BAND, (s + 1) * BAND)
            lse_tile = jnp.zeros((BAND, 128), F32)
            for j, (qo, ko, vo) in enumerate(pairs):
                q = cur_ref[rows, qo:qo + 128]
                kk = _kv_rows(cur_ref, tail_ref, s, ko)
                vv = _kv_rows(cur_ref, tail_ref, s, vo)
                sc = _dot_nt(_stack_heads(q, qk_a), kk)
                sc = jnp.where(mask2, sc, -jnp.inf)
                m = jnp.max(sc, axis=-1, keepdims=True)
                if swa:
                    sk = jnp.where(first, sink_ref[2 * j], sink_ref[2 * j + 1])
                    m = jnp.maximum(m, sk)
                p = jnp.exp(sc - m)
                den = jnp.sum(p, axis=-1, keepdims=True)
                if swa:
                    den = den + jnp.exp(sk - m)
                lse = m + jnp.log(den)
                lse_tile = jnp.where(lane == 2 * j, lse[:BAND], jnp.where(lane == 2 * j + 1, lse[BAND:], lse_tile))
                o2 = _dot(p.astype(BF16), vv) * (1.0 / den)
                o = jnp.where(v_a, o2[:BAND], o2[BAND:])
                o_ref[rows, j * 128:(j + 1) * 128] = o.astype(BF16)
                if swa:
                    o32_ref[rows, j * 128:(j + 1) * 128] = o
            lse_ref[rows, :] = lse_tile

    in_specs = [pl.BlockSpec((None, tq, PBLK), lambda r, cb, i: (r, i, cb)),
                pl.BlockSpec((None, BAND, PBLK), lambda r, cb, i: (r, jnp.maximum(i * nsb - 1, 0), cb))]
    args = [pv, pv]
    out_specs = [pl.BlockSpec((None, tq, ow), lambda r, cb, i: (r, i, cb)),
                 pl.BlockSpec((None, tq, 128), lambda r, cb, i: (r, i, cb))]
    out_shape = [jax.ShapeDtypeStruct((d, ls, 512), BF16), jax.ShapeDtypeStruct((d, ls, 128 * ncol), F32)]
    if swa:
        in_specs.append(pl.BlockSpec(memory_space=pltpu.SMEM))
        args.append(sinks)
        out_specs.append(out_specs[0])
        out_shape.append(jax.ShapeDtypeStruct((d, ls, 512), F32))
    return pl.pallas_call(
        body, name=name, grid=(d, ncol, n),
        in_specs=in_specs, out_specs=out_specs, out_shape=out_shape,
        compiler_params=_params(("arbitrary", "arbitrary", "arbitrary")),
    )(*args)


def _lse_lane(head):
    return (head // 4) * 128 + head % 4


def _dil_head_spread():
    lane = lax.broadcasted_iota(jnp.int32, (256, 512), 0)
    head = lax.broadcasted_iota(jnp.int32, (256, 512), 1) // HEAD_DIM
    return (lane == _lse_lane(head)).astype(BF16)


def _head_scale(x, tile, spread):
    return x * _dot(tile.astype(BF16), spread)


def _head_gather(width, dil):
    head = lax.broadcasted_iota(jnp.int32, (8 * HEAD_DIM, width), 0) // HEAD_DIM
    lane = lax.broadcasted_iota(jnp.int32, (8 * HEAD_DIM, width), 1)
    return (lane == (_lse_lane(head) if dil else head)).astype(BF16)


def _head_sums(x, gather):
    hi = x.astype(BF16)
    lo = (x - hi.astype(F32)).astype(BF16)
    return _dot(hi, gather) + _dot(lo, gather)


def _alphas(l0, l1, l2):
    m = jnp.maximum(jnp.maximum(l0, l1), l2)
    e0, e1, e2 = jnp.exp(l0 - m), jnp.exp(l1 - m), jnp.exp(l2 - m)
    den = e0 + e1 + e2
    return e0 / den, e1 / den, e2 / den


def _mix(o0, o1, o2, l0, l1, l2, ob, gts, x, w_a, w_b, w_out, g_cross, tm):
    t = x.shape[0]

    def body(o0_ref, o1_ref, o2_ref, l0_ref, l1_ref, l2_ref, ob_ref, g_ref, x_ref, wa_ref, wb_ref, wo_ref,
             gc_ref, oa_ref, ya_ref, yb_ref, mg_ref, x1_ref, hc_ref, so_ref, sl_ref):
        a0, a1, a2 = _alphas(l0_ref[...], _interleave(l1_ref, sl_ref), _interleave(l2_ref, sl_ref))
        spread = _dil_head_spread()
        oa = (_head_scale(o0_ref[...].astype(F32), a0, spread)
              + _head_scale(_interleave(o1_ref, so_ref), a1, spread)
              + _head_scale(_interleave(o2_ref, so_ref), a2, spread))
        oab = oa.astype(BF16)
        oa_ref[...] = oab
        ya = _dot(oab, wa_ref[...])
        yb = _dot(ob_ref[...], wb_ref[...])
        ya_ref[...] = ya.astype(BF16)
        yb_ref[...] = yb.astype(BF16)
        merged = (g_ref[:, :D_MODEL].astype(F32) * ya + g_ref[:, D_MODEL:].astype(F32) * yb).astype(BF16)
        mg_ref[...] = merged
        x1 = x_ref[...] + _dot(merged, wo_ref[...])
        x1_ref[...] = x1
        hc_ref[...] = (x1 * _rms(x1) * gc_ref[...]).astype(BF16)

    row = lambda w: pl.BlockSpec((tm, w), lambda i: (i, 0))
    full = lambda a, b: pl.BlockSpec((a, b), lambda i: (0, 0))
    return pl.pallas_call(
        body, name="mix", grid=(t // tm,),
        in_specs=[row(512), _res_spec(o1, tm), _res_spec(o2, tm), row(256), _res_spec(l1, tm), _res_spec(l2, tm),
                  row(512), row(GATE_WIDTH),
                  row(D_MODEL), full(512, D_MODEL), full(512, D_MODEL), full(D_MODEL, D_MODEL), full(1, D_MODEL)],
        out_specs=[row(512), row(D_MODEL), row(D_MODEL), row(D_MODEL), row(D_MODEL), row(D_MODEL)],
        out_shape=[jax.ShapeDtypeStruct((t, 512), BF16), jax.ShapeDtypeStruct((t, D_MODEL), BF16),
                   jax.ShapeDtypeStruct((t, D_MODEL), BF16), jax.ShapeDtypeStruct((t, D_MODEL), BF16),
                   jax.ShapeDtypeStruct((t, D_MODEL), F32), jax.ShapeDtypeStruct((t, D_MODEL), BF16)],
        scratch_shapes=[_lane_scratch(tm, 512), _lane_scratch(tm, 256)],
        compiler_params=_params(("arbitrary",)),
    )(o0, o1, o2, l0, l1, l2, ob, gts, x, w_a, w_b, w_out, g_cross)


def _memkv(mem, g_mem, w_ckv):
    m = mem.shape[0]
    ws = w_ckv.shape[2]

    def body(mem_ref, g_ref, w_ref, mn_ref, kv_ref):
        xt = mem_ref[...]
        mn = (xt * _rms(xt) * g_ref[...]).astype(BF16)
        mn_ref[...] = mn
        for j in range(N_DEV):
            kv_ref[:, j * ws:(j + 1) * ws] = _dot(mn, w_ref[j]).astype(BF16)

    return pl.pallas_call(
        body, name="memkv",
        out_shape=[jax.ShapeDtypeStruct((m, D_MODEL), BF16), jax.ShapeDtypeStruct((m, 2 * D_MODEL), BF16)],
        compiler_params=_params(),
    )(mem, g_mem, w_ckv)


def _cross_probs(q, kv_ref, h):
    k = kv_ref[:, h * X_HEAD_DIM:(h + 1) * X_HEAD_DIM]
    sc = _dot_nt(q[:, h * X_HEAD_DIM:(h + 1) * X_HEAD_DIM], k)
    m = jnp.max(sc, axis=-1, keepdims=True)
    p = jnp.exp(sc - m)
    return p / jnp.sum(p, axis=-1, keepdims=True)


def _cross(hc, x1, kv, w_cq, w_co, g_mlp, tm):
    t = x1.shape[0]
    m = kv.shape[0]

    def body(hc_ref, x1_ref, kv_ref, wq_ref, wo_ref, g_ref, q_ref, o_ref, x2_ref, hm_ref):
        q = (_dot(hc_ref[...], wq_ref[...]) * X_SCALE).astype(BF16)
        q_ref[...] = q
        outs = []
        for h in range(X_HEADS):
            p = _cross_probs(q, kv_ref, h)
            v = kv_ref[:, D_MODEL + h * X_HEAD_DIM:D_MODEL + (h + 1) * X_HEAD_DIM]
            outs.append(_dot(p.astype(BF16), v))
        o = jnp.concatenate(outs, axis=1).astype(BF16)
        o_ref[...] = o
        x2 = x1_ref[...] + _dot(o, wo_ref[...])
        x2_ref[...] = x2
        hm_ref[...] = (x2 * _rms(x2) * g_ref[...]).astype(BF16)

    row = lambda w: pl.BlockSpec((tm, w), lambda i: (i, 0))
    full = lambda a, b: pl.BlockSpec((a, b), lambda i: (0, 0))
    return pl.pallas_call(
        body, name="cross", grid=(t // tm,),
        in_specs=[row(D_MODEL), row(D_MODEL), full(m, 2 * D_MODEL), full(D_MODEL, D_MODEL),
                  full(D_MODEL, D_MODEL), full(1, D_MODEL)],
        out_specs=[row(D_MODEL)] * 4,
        out_shape=[jax.ShapeDtypeStruct((t, D_MODEL), BF16), jax.ShapeDtypeStruct((t, D_MODEL), BF16),
                   jax.ShapeDtypeStruct((t, D_MODEL), F32), jax.ShapeDtypeStruct((t, D_MODEL), BF16)],
        compiler_params=_params(("arbitrary",)),
    )(hc, x1, kv, w_cq, w_co, g_mlp)


def _mlp(hm, x2, w_1, w_2, g_final, target, tm, tf):
    t = x2.shape[0]
    nf = D_FF // tf

    def body(hm_ref, x2_ref, w1_ref, w2_ref, g_ref, tg_ref, a_ref, dx3_ref, loss_ref, dg_ref, acc_ref, asq_ref):
        i, f = pl.program_id(0), pl.program_id(1)
        sw = w1_ref.shape[2]

        def up(slot):
            hm_t = hm_ref[...]
            for s in range(w1_ref.shape[0]):
                a = jnp.maximum(_dot(hm_t, w1_ref[s]), 0.0).astype(BF16)
                a_ref[:, s * sw:(s + 1) * sw] = a
                asq_ref[slot, :, s * sw:(s + 1) * sw] = a * a

        def down(slot, first):
            part = _dot(asq_ref[slot], w2_ref[...])
            if first:
                acc_ref[...] = part
            else:
                acc_ref[...] += part

        for ff in range(nf + 1):
            @pl.when(f == ff)
            def _():
                if ff < nf:
                    up(ff % 2)
                if ff >= 1:
                    down((ff - 1) % 2, ff == 1)

        @pl.when((i == 0) & (f == 0))
        def _():
            loss_ref[...] = jnp.zeros_like(loss_ref)
            dg_ref[...] = jnp.zeros_like(dg_ref)

        @pl.when(f == nf)
        def _():
            x3 = x2_ref[...] + acc_ref[...]
            r = _rms(x3)
            g = g_ref[...]
            diff = x3 * r * g - tg_ref[...]
            loss_ref[...] += 0.5 * jnp.sum(jnp.mean(diff * diff, axis=-1, keepdims=True))
            dx3, dg = _rms_bwd(diff / D_MODEL, x3, r, g)
            dx3_ref[...] = dx3
            dg_ref[...] += dg

    up_f = lambda f: jnp.minimum(f, nf - 1)
    down_f = lambda f: jnp.maximum(f - 1, 0)
    return pl.pallas_call(
        body, name="mlp", grid=(t // tm, nf + 1),
        in_specs=[pl.BlockSpec((tm, D_MODEL), lambda i, f: (i, 0)),
                  pl.BlockSpec((tm, D_MODEL), lambda i, f: (i, 0)),
                  pl.BlockSpec((tf // w_1.shape[2], D_MODEL, w_1.shape[2]), lambda i, f: (up_f(f), 0, 0)),
                  pl.BlockSpec((tf, D_MODEL), lambda i, f: (down_f(f), 0)),
                  pl.BlockSpec((1, D_MODEL), lambda i, f: (0, 0)),
                  pl.BlockSpec((tm, D_MODEL), lambda i, f: (i, 0))],
        out_specs=[pl.BlockSpec((tm, tf), lambda i, f: (i, up_f(f))),
                   pl.BlockSpec((tm, D_MODEL), lambda i, f: (i, 0)),
                   pl.BlockSpec((1, 128), lambda i, f: (0, 0)),
                   pl.BlockSpec((1, D_MODEL), lambda i, f: (0, 0))],
        out_shape=[jax.ShapeDtypeStruct((t, D_FF), BF16), jax.ShapeDtypeStruct((t, D_MODEL), F32),
                   jax.ShapeDtypeStruct((1, 128), F32), jax.ShapeDtypeStruct((1, D_MODEL), F32)],
        scratch_shapes=[pltpu.VMEM((tm, D_MODEL), F32), pltpu.VMEM((2, tm, tf), BF16)],
        compiler_params=_params(("arbitrary", "arbitrary")),
    )(hm, x2, w_1, w_2, g_final, target)


def _mlp_bwd(dx3, a, w_1, w_2, x2, g_mlp, tm, tf):
    t = x2.shape[0]
    nf = D_FF // tf

    def body(dx3_ref, a_ref, w1_ref, w2_ref, x2_ref, g_ref, dz_ref, dx2_ref, dg_ref, acc_ref, dzs_ref):
        i, f = pl.program_id(0), pl.program_id(1)
        sw = w1_ref.shape[2]

        def first(slot):
            dx3_b = dx3_ref[...].astype(BF16)
            for s in range(w1_ref.shape[0]):
                cols = slice(s * sw, (s + 1) * sw)
                dz = (2.0 * a_ref[:, cols].astype(F32) * _dot_nt(dx3_b, w2_ref[cols, :])).astype(BF16)
                dz_ref[:, cols] = dz
                dzs_ref[slot, :, cols] = dz

        def second(slot, init):
            part = _dot_nt(dzs_ref[slot, :, 0:sw], w1_ref[0])
            for s in range(1, w1_ref.shape[0]):
                part = part + _dot_nt(dzs_ref[slot, :, s * sw:(s + 1) * sw], w1_ref[s])
            if init:
                acc_ref[...] = part
            else:
                acc_ref[...] += part

        for ff in range(nf + 1):
            @pl.when(f == ff)
            def _():
                if ff < nf:
                    first(ff % 2)
                if ff >= 1:
                    second((ff - 1) % 2, ff == 1)

        @pl.when((i == 0) & (f == 0))
        def _():
            dg_ref[...] = jnp.zeros_like(dg_ref)

        @pl.when(f == nf)
        def _():
            xt = x2_ref[...]
            dx, dg = _rms_bwd(acc_ref[...], xt, _rms(xt), g_ref[...])
            dx2_ref[...] = dx3_ref[...] + dx
            dg_ref[...] += dg

    first_f = lambda f: jnp.minimum(f, nf - 1)
    second_f = lambda f: jnp.maximum(f - 1, 0)
    return pl.pallas_call(
        body, name="mlp_bwd", grid=(t // tm, nf + 1),
        in_specs=[pl.BlockSpec((tm, D_MODEL), lambda i, f: (i, 0)),
                  pl.BlockSpec((tm, tf), lambda i, f: (i, first_f(f))),
                  pl.BlockSpec((tf // w_1.shape[2], D_MODEL, w_1.shape[2]), lambda i, f: (second_f(f), 0, 0)),
                  pl.BlockSpec((tf, D_MODEL), lambda i, f: (first_f(f), 0)),
                  pl.BlockSpec((tm, D_MODEL), lambda i, f: (i, 0)),
                  pl.BlockSpec((1, D_MODEL), lambda i, f: (0, 0))],
        out_specs=[pl.BlockSpec((tm, tf), lambda i, f: (i, first_f(f))),
                   pl.BlockSpec((tm, D_MODEL), lambda i, f: (i, 0)),
                   pl.BlockSpec((1, D_MODEL), lambda i, f: (0, 0))],
        out_shape=[jax.ShapeDtypeStruct((t, D_FF), BF16), jax.ShapeDtypeStruct((t, D_MODEL), F32),
                   jax.ShapeDtypeStruct((1, D_MODEL), F32)],
        scratch_shapes=[pltpu.VMEM((tm, D_MODEL), F32), pltpu.VMEM((2, tm, tf), BF16)],
        compiler_params=_params(("arbitrary", "arbitrary")),
    )(dx3, a, w_1, w_2, x2, g_mlp)


def _wgrad(name, a, b, tka, tn, tm, square=False, col_shards=False):
    t, ka = a.shape
    n = b.shape[1]
    nk = t // tm

    def body(a_ref, b_ref, o_ref, acc_ref):
        at = a_ref[...].astype(BF16)
        if square:
            at = at * at
        part = _dot_tn(at, b_ref[...].astype(BF16))
        k = pl.program_id(2)

        @pl.when(k == 0)
        def _():
            acc_ref[...] = part

        @pl.when(k > 0)
        def _():
            acc_ref[...] += part

        @pl.when(k == nk - 1)
        def _():
            if col_shards:
                for s in range(tn // sw):
                    o_ref[s] = acc_ref[:, s * sw:(s + 1) * sw].astype(BF16)
            else:
                o_ref[...] = acc_ref[...].astype(BF16)

    if col_shards:
        sw = n // N_DEV
        out_spec = pl.BlockSpec((tn // sw, tka, sw), lambda p, q, k: (q, p, 0))
        out_shape = jax.ShapeDtypeStruct((N_DEV, ka, sw), BF16)
    else:
        out_spec = pl.BlockSpec((tka, tn), lambda p, q, k: (p, q))
        out_shape = jax.ShapeDtypeStruct((ka, n), BF16)
    return pl.pallas_call(
        body, name=name, grid=(ka // tka, n // tn, nk),
        in_specs=[pl.BlockSpec((tm, tka), lambda p, q, k: (k, p)),
                  pl.BlockSpec((tm, tn), lambda p, q, k: (k, q))],
        out_specs=out_spec, out_shape=out_shape,
        scratch_shapes=[pltpu.VMEM((tka, tn), F32)],
        compiler_params=_params(("arbitrary", "arbitrary", "arbitrary")),
    )(a, b)


def _cross_bwd(dx2, x1, q, kv, w_cq, w_co, g_cross, tm, dep=None):
    t = x1.shape[0]
    m = kv.shape[0]

    def body(dx2_ref, x1_ref, q_ref, kv_ref, wq_ref, wo_ref, g_ref, dq_ref, dx1_ref, dkv_ref, dg_ref):
        @pl.when(pl.program_id(0) == 0)
        def _():
            dkv_ref[...] = jnp.zeros_like(dkv_ref)
            dg_ref[...] = jnp.zeros_like(dg_ref)

        do = _dot_nt(dx2_ref[...].astype(BF16), wo_ref[...]).astype(BF16)
        q = q_ref[...]
        dqs = []
        for h in range(X_HEADS):
            hs = slice(h * X_HEAD_DIM, (h + 1) * X_HEAD_DIM)
            vs = slice(D_MODEL + h * X_HEAD_DIM, D_MODEL + (h + 1) * X_HEAD_DIM)
            p = _cross_probs(q, kv_ref, h)
            dp = _dot_nt(do[:, hs], kv_ref[:, vs])
            ds = (p * (dp - jnp.sum(dp * p, axis=-1, keepdims=True))).astype(BF16)
            dqs.append(_dot(ds, kv_ref[:, hs]))
            dkv_ref[:, hs] += _dot_tn(ds, q[:, hs])
            dkv_ref[:, vs] += _dot_tn(p.astype(BF16), do[:, hs])
        dq = (jnp.concatenate(dqs, axis=1) * X_SCALE).astype(BF16)
        dq_ref[...] = dq
        xt = x1_ref[...]
        dx, dg = _rms_bwd(_dot_nt(dq, wq_ref[...]), xt, _rms(xt), g_ref[...])
        dx1_ref[...] = dx2_ref[...] + dx
        dg_ref[...] += dg

    row = lambda w: pl.BlockSpec((tm, w), lambda i: (i, 0))
    full = lambda a, b: pl.BlockSpec((a, b), lambda i: (0, 0))
    return pl.pallas_call(
        _with_dep(body, 7, dep), name="cross_bwd", grid=(t // tm,),
        in_specs=[row(D_MODEL), row(D_MODEL), row(D_MODEL), full(m, 2 * D_MODEL), full(D_MODEL, D_MODEL),
                  full(D_MODEL, D_MODEL), full(1, D_MODEL)] + _dep_spec(dep),
        out_specs=[row(D_MODEL), row(D_MODEL), full(m, 2 * D_MODEL), full(1, D_MODEL)],
        out_shape=[jax.ShapeDtypeStruct((t, D_MODEL), BF16), jax.ShapeDtypeStruct((t, D_MODEL), F32),
                   jax.ShapeDtypeStruct((m, 2 * D_MODEL), F32), jax.ShapeDtypeStruct((1, D_MODEL), F32)],
        compiler_params=_params(("arbitrary",)),
    )(dx2, x1, q, kv, w_cq, w_co, g_cross, *_dep_arg(dep))


def _memkv_bwd(dkv, mn, mem, w_ckv, g_mem):
    ws = w_ckv.shape[2]

    def body(dkv_ref, mn_ref, mem_ref, w_ref, g_ref, dw_ref, dg_ref):
        mn = mn_ref[...]
        dmn = jnp.zeros(mn.shape, F32)
        for j in range(N_DEV):
            dkvb = dkv_ref[:, j * ws:(j + 1) * ws].astype(BF16)
            dw_ref[j] = _dot_tn(mn, dkvb).astype(BF16)
            dmn = dmn + _dot_nt(dkvb, w_ref[j])
        xt = mem_ref[...]
        dg_ref[...] = jnp.sum(dmn * xt * _rms(xt), axis=0, keepdims=True)

    return pl.pallas_call(
        body, name="memkv_bwd",
        out_shape=[jax.ShapeDtypeStruct(w_ckv.shape, BF16), jax.ShapeDtypeStruct((1, D_MODEL), F32)],
        compiler_params=_params(),
    )(dkv, mn, mem, w_ckv, g_mem)


def _merge_bwd(dx1, ya, yb, gts, w_out, w_g, tm):
    t = dx1.shape[0]

    def body(dx1_ref, ya_ref, yb_ref, g_ref, wo_ref, wg_ref, dg_ref, dhp_ref, dya_ref, dyb_ref, db_ref):
        @pl.when(pl.program_id(0) == 0)
        def _():
            db_ref[...] = jnp.zeros_like(db_ref)

        dm = _dot_nt(dx1_ref[...].astype(BF16), wo_ref[...])
        ga = g_ref[:, :D_MODEL].astype(F32)
        gb = g_ref[:, D_MODEL:].astype(F32)
        dya_ref[...] = (dm * ga).astype(BF16)
        dyb_ref[...] = (dm * gb).astype(BF16)
        dpa = dm * ya_ref[...].astype(F32) * ga * (1.0 - ga)
        dpb = dm * yb_ref[...].astype(F32) * gb * (1.0 - gb)
        dpre = jnp.concatenate([dpa, dpb], axis=1)
        db_ref[...] += jnp.sum(dpre, axis=0, keepdims=True)
        dpreb = dpre.astype(BF16)
        dg_ref[...] = dpreb
        dhp_ref[...] = _dot_nt(dpreb, wg_ref[...])

    row = lambda w: pl.BlockSpec((tm, w), lambda i: (i, 0))
    once = lambda a, b: pl.BlockSpec((a, b), lambda i: (0, 0), pipeline_mode=pl.Buffered(1))
    sds = jax.ShapeDtypeStruct
    return pl.pallas_call(
        body, name="merge_bwd", grid=(t // tm,),
        in_specs=[row(D_MODEL), row(D_MODEL), row(D_MODEL), row(GATE_WIDTH),
                  once(D_MODEL, D_MODEL), once(D_MODEL, GATE_WIDTH)],
        out_specs=[row(GATE_WIDTH), row(D_MODEL), row(D_MODEL), row(D_MODEL),
                   pl.BlockSpec((1, GATE_WIDTH), lambda i: (0, 0))],
        out_shape=[sds((t, GATE_WIDTH), BF16), sds((t, D_MODEL), F32), sds((t, D_MODEL), BF16),
                   sds((t, D_MODEL), BF16), sds((1, GATE_WIDTH), F32)],
        compiler_params=_params(("arbitrary",)),
    )(dx1, ya, yb, gts, w_out, w_g)


def _combine_bwd(dya, dyb, oa, ob, l0, l1, l2, lb, sink_row, w_a, w_b, tm):
    t = dya.shape[0]

    def body(dya_ref, dyb_ref, oa_ref, ob_ref, l0_ref, l1_ref, l2_ref, lb_ref, sk_ref, wa_ref, wb_ref,
             do0_ref, do1_ref, do2_ref, c0_ref, c1_ref, c2_ref, dob_ref, cb_ref, dsk_ref, so_ref, sl_ref):
        @pl.when(pl.program_id(0) == 0)
        def _():
            dsk_ref[...] = jnp.zeros_like(dsk_ref)

        doa = _dot_nt(dya_ref[...], wa_ref[...])
        dob = _dot_nt(dyb_ref[...], wb_ref[...])
        dsum = _head_sums(doa * oa_ref[...].astype(F32), _head_gather(256, True))
        a0, a1, a2 = _alphas(l0_ref[...], _interleave(l1_ref, sl_ref), _interleave(l2_ref, sl_ref))
        c0_ref[...] = a0 * dsum
        spread = _dil_head_spread()
        do0_ref[...] = _head_scale(doa, a0, spread).astype(BF16)
        for al, do_ref, c_ref in ((a1, do1_ref, c1_ref), (a2, do2_ref, c2_ref)):
            _deinterleave(al * dsum, sl_ref, c_ref, F32)
            _deinterleave(_head_scale(doa, al, spread), so_ref, do_ref, BF16)
        dob_ref[...] = dob.astype(BF16)
        cb = _head_sums(dob * ob_ref[...], _head_gather(128, False))
        cb_ref[...] = cb
        lane = lax.broadcasted_iota(jnp.int32, cb.shape, 1)
        psink = jnp.where(lane < 8, jnp.exp(sk_ref[...] - lb_ref[...]), 0.0)
        dsk_ref[...] += jnp.sum(-psink * cb, axis=0, keepdims=True)

    row = lambda w: pl.BlockSpec((tm, w), lambda i: (i, 0))
    full = lambda a, b: pl.BlockSpec((a, b), lambda i: (0, 0))
    sds = jax.ShapeDtypeStruct
    d1, d2 = l1.shape[0], l2.shape[0]
    res = lambda d, w: pl.BlockSpec((d, tm // d, w), lambda i: (0, i, 0))
    return pl.pallas_call(
        body, name="combine_bwd", grid=(t // tm,),
        in_specs=[row(D_MODEL), row(D_MODEL), row(512), row(512),
                  row(256), _res_spec(l1, tm), _res_spec(l2, tm), row(128), full(1, 128),
                  full(512, D_MODEL), full(512, D_MODEL)],
        out_specs=[row(512), res(d1, 512), res(d2, 512), row(256), res(d1, 256), res(d2, 256),
                   row(512), row(128), full(1, 128)],
        out_shape=[sds((t, 512), BF16), sds((d1, t // d1, 512), BF16),
                   sds((d2, t // d2, 512), BF16), sds((t, 256), F32), sds((d1, t // d1, 256), F32),
                   sds((d2, t // d2, 256), F32), sds((t, 512), BF16),
                   sds((t, 128), F32), sds((1, 128), F32)],
        scratch_shapes=[_lane_scratch(tm, 512), _lane_scratch(tm, 256)],
        compiler_params=_params(("arbitrary",)),
    )(dya, dyb, oa, ob, l0, l1, l2, lb, sink_row, w_a, w_b)


def _attn_bwd(name, pv, dov, lsev, cv, cosv, sinv, swa, tq, dep=None):
    d, ls = pv.shape[0], pv.shape[1]
    n, nsb = ls // tq, tq // BAND
    pairs = _attn_layout(swa)
    ncol = 1 if swa else 2
    ow = 128 * len(pairs)

    kv_slots = sorted({(ko, vo) for _, ko, vo in pairs})

    def body(cur_ref, tail_ref, do_ref, lse_ref, c_ref, cos_ref, sin_ref, out_ref, acc_ref, carry_ref, acct_ref):
        i = pl.program_id(2)
        blk_i = n - 1 - i
        acc_ref[...] = jnp.zeros_like(acc_ref)
        acct_ref[...] = jnp.zeros_like(acct_ref)

        @pl.when(i == 0)
        def _():
            carry_ref[...] = jnp.zeros_like(carry_ref)

        qk_a, v_a = _head_a_masks(BAND)
        dim = lax.broadcasted_iota(jnp.int32, (128, BAND), 0)
        qk_at, v_at = (dim % HEAD_DIM) < HEAD_DIM // 2, dim < HEAD_DIM
        for s in range(nsb):
            mask = _band_mask(blk_i, s)
            mask2 = jnp.concatenate([mask, mask], axis=0)
            rows = slice(s * BAND, (s + 1) * BAND)
            kcols = slice(s * BAND, (s + 2) * BAND)
            for j, (qo, ko, vo) in enumerate(pairs):
                slot = kv_slots.index((ko, vo))
                kk = _kv_rows(cur_ref, tail_ref, s, ko)
                vv = _kv_rows(cur_ref, tail_ref, s, vo)
                q, do = cur_ref[rows, qo:qo + 128], do_ref[rows, j * 128:(j + 1) * 128]
                q2, do2 = _stack_heads(q, qk_a), _stack_heads(do, v_a)
                col2 = lambda ref: jnp.concatenate([ref[rows, 2 * j:2 * j + 1], ref[rows, 2 * j + 1:2 * j + 2]], axis=0)
                sc = _dot_nt(q2, kk)
                p = jnp.exp(jnp.where(mask2, sc, -jnp.inf) - col2(lse_ref))
                dp = _dot_nt(do2, vv)
                ds = (p * (dp - col2(c_ref))).astype(BF16)
                dq2 = _dot(ds, kk)
                acc_ref[BAND + s * BAND:BAND + (s + 1) * BAND, qo:qo + 128] += jnp.where(qk_a, dq2[:BAND], dq2[BAND:])
                acct_ref[2 * slot, :, kcols] += _dot(_stack_heads_t(q.T, qk_at), ds)
                acct_ref[2 * slot + 1, :, kcols] += _dot(_stack_heads_t(do.T, v_at), p.astype(BF16))
        for slot, (ko, vo) in enumerate(kv_slots):
            acc_ref[:, ko:ko + 128] += acct_ref[2 * slot].T
            acc_ref[:, vo:vo + 128] += acct_ref[2 * slot + 1].T

        last = acc_ref[tq:, :] + carry_ref[...]
        fin = last if tq == BAND else jnp.concatenate([acc_ref[BAND:tq, :], last], axis=0)
        out_ref[...] = _rope(fin, cos_ref[...], sin_ref[...], swa, -1).astype(BF16)
        carry_ref[...] = acc_ref[0:BAND, :]

    rev = lambda i: n - 1 - i
    blk = lambda rows, w, row_of: pl.BlockSpec((None, rows, w), lambda r, cb, i: (r, row_of(i), cb))
    tab = pl.BlockSpec((None, tq, 128), lambda r, cb, i: (r, rev(i), 0))
    return pl.pallas_call(
        _with_dep(body, 7, dep), name=name, grid=(d, ncol, n),
        in_specs=[blk(tq, PBLK, rev), blk(BAND, PBLK, lambda i: jnp.maximum(rev(i) * nsb - 1, 0)),
                  blk(tq, ow, rev), blk(tq, 128, rev), blk(tq, 128, rev), tab, tab] + _dep_spec(dep),
        out_specs=blk(tq, PBLK, rev),
        out_shape=jax.ShapeDtypeStruct((d, ls, ncol * PBLK), BF16),
        scratch_shapes=[pltpu.VMEM((tq + BAND, PBLK), F32), pltpu.VMEM((BAND, PBLK), F32),
                        pltpu.VMEM((2 * len(kv_slots), 128, tq + BAND), F32)],
        compiler_params=_params(("arbitrary", "arbitrary", "arbitrary")),
    )(pv, pv, dov, lsev, cv, cosv, sinv, *_dep_arg(dep))


def _dx(dp0, dp1, dp2, dpb, w_p, dh_part, dx1, x, g_mix, tm, dep=None):
    t = x.shape[0]
    gw = 2 * PBLK

    def body(dp0_ref, dp1_ref, dp2_ref, dpb_ref, w_ref, dhp_ref, dx1_ref, x_ref, g_ref, gx_ref, dg_ref,
             dpt_ref, scr_ref):
        @pl.when(pl.program_id(0) == 0)
        def _():
            dg_ref[...] = jnp.zeros_like(dg_ref)

        dpt_ref[:, 0:gw] = dp0_ref[...]
        dpt_ref[:, gw:2 * gw] = _interleave(dp1_ref, scr_ref).astype(BF16)
        dpt_ref[:, 2 * gw:3 * gw] = _interleave(dp2_ref, scr_ref).astype(BF16)
        dpt_ref[:, 3 * gw:] = dpb_ref[...]
        dh = _dot_nt(dpt_ref[...], w_ref[...]) + dhp_ref[...]
        xt = x_ref[...]
        dx, dg = _rms_bwd(dh, xt, _rms(xt), g_ref[...])
        gx_ref[...] = dx1_ref[...] + dx
        dg_ref[...] += dg

    row = lambda w: pl.BlockSpec((tm, w), lambda i: (i, 0))
    full = lambda a, b: pl.BlockSpec((a, b), lambda i: (0, 0))
    return pl.pallas_call(
        _with_dep(body, 9, dep), name="dx", grid=(t // tm,),
        in_specs=[row(gw), _res_spec(dp1, tm), _res_spec(dp2, tm), row(PBLK),
                  pl.BlockSpec((D_MODEL, P_WIDTH), lambda i: (0, 0), pipeline_mode=pl.Buffered(1)),
                  row(D_MODEL), row(D_MODEL), row(D_MODEL), full(1, D_MODEL)] + _dep_spec(dep),
        out_specs=[row(D_MODEL), full(1, D_MODEL)],
        out_shape=[jax.ShapeDtypeStruct((t, D_MODEL), F32), jax.ShapeDtypeStruct((1, D_MODEL), F32)],
        scratch_shapes=[pltpu.VMEM((tm, P_WIDTH), BF16), _lane_scratch(tm, gw)],
        compiler_params=_params(("arbitrary",)),
    )(dp0, dp1, dp2, dpb, w_p, dh_part, dx1, x, g_mix, *_dep_arg(dep))


MESH = pl.DeviceIdType.MESH
HBM_SPEC = pl.BlockSpec(memory_space=pltpu.HBM)
VMEM_SPEC = pl.BlockSpec(memory_space=pltpu.VMEM)


def _all_gather(xp, act, g, tm):
    t = act.shape[0]
    d1, d2 = DIL_GROUPS[1][1], DIL_GROUPS[2][1]

    def body(x_ref, act_ref, g_ref, out_ref, h_ref, h1_ref, h2_ref, send_sems, recv_sems, local_sem, hf_ref):
        x, y, c = lax.axis_index("x"), lax.axis_index("y"), lax.axis_index("c")
        me, sibling = (x, y, c), (x, y, 1 - c)
        chips = [(1 - x, y), (x, 1 - y), (1 - x, 1 - y)]

        def rows(px, py, pc):
            return out_ref.at[4 * px + 2 * py + pc]

        def copy(k, block, to, src=None):
            return pltpu.make_async_remote_copy(
                src_ref=rows(*block) if src is None else src, dst_ref=rows(*block),
                send_sem=send_sems.at[k], recv_sem=recv_sems.at[k], device_id=to, device_id_type=MESH)

        mine = pltpu.make_async_copy(x_ref, rows(*me), local_sem)
        mine.start()
        first = [copy(0, me, sibling, src=x_ref)]
        first += [copy(1 + j, me, (*chip, c), src=x_ref) for j, chip in enumerate(chips)]
        for cp in first:
            cp.start()

        def norm(a_blk, h_blk, h1_blk, h2_blk):
            xt = a_blk[...]
            hf = xt * _rms(xt) * g_ref[...]
            h_blk[...] = hf.astype(BF16)
            _deinterleave(hf, hf_ref, h1_blk, BF16)
            _deinterleave(hf, hf_ref, h2_blk, BF16)

        res = lambda d: pl.BlockSpec((d, tm // d, D_MODEL), lambda i: (0, i, 0))
        row = pl.BlockSpec((tm, D_MODEL), lambda i: (i, 0))
        pltpu.emit_pipeline(norm, grid=(t // tm,), in_specs=[row], out_specs=[row, res(d1), res(d2)])(
            act_ref, h_ref, h1_ref, h2_ref)

        passed = [copy(4 + j, (*chip, c), sibling) for j, chip in enumerate(chips)]
        for j, chip in enumerate(chips):
            copy(1 + j, (*chip, c), me).wait_recv()
            passed[j].start()
        copy(0, sibling, me).wait_recv()
        for j, chip in enumerate(chips):
            copy(4 + j, (*chip, 1 - c), me).wait_recv()
        for cp in first + passed:
            cp.wait_send()
        mine.wait()

    sds = jax.ShapeDtypeStruct
    return pl.pallas_call(
        body, name="all_gather",
        out_shape=[sds((N_DEV,) + xp.shape, xp.dtype), sds((t, D_MODEL), BF16),
                   sds((d1, t // d1, D_MODEL), BF16), sds((d2, t // d2, D_MODEL), BF16)],
        in_specs=[HBM_SPEC, HBM_SPEC, VMEM_SPEC], out_specs=[HBM_SPEC] * 4,
        scratch_shapes=[pltpu.SemaphoreType.DMA((7,)), pltpu.SemaphoreType.DMA((7,)), pltpu.SemaphoreType.DMA,
                        _lane_scratch(tm, D_MODEL)],
        compiler_params=pltpu.CompilerParams(vmem_limit_bytes=VMEM_LIMIT),
    )(xp, act, g)


def _peers():
    x, y, c = lax.axis_index("x"), lax.axis_index("y"), lax.axis_index("c")
    out = []
    for k in range(1, N_DEV):
        px = 1 - x if k & 4 else x
        py = 1 - y if k & 2 else y
        pc = 1 - c if k & 1 else c
        out.append((k, (px, py, pc), 4 * px + 2 * py + pc))
    return out


def _my_index():
    return 4 * lax.axis_index("x") + 2 * lax.axis_index("y") + lax.axis_index("c")


SEM_SPEC = pl.BlockSpec(memory_space=pltpu.SEMAPHORE)
ANY_SPEC = pl.BlockSpec(memory_space=pl.ANY)
_SPLIT_PARAMS = pltpu.CompilerParams(has_side_effects=pltpu.SideEffectType.DATAFLOW_SIDE_EFFECTING)


def _split_copies(gather, src_refs, land_refs, send_sems, recv_sems):
    me_idx = _my_index()
    out = []
    for a, (src_ref, land_ref) in enumerate(zip(src_refs, land_refs)):
        for k, peer, peer_idx in _peers():
            if gather:
                src, dst = src_ref, land_ref.at[me_idx]
            else:
                src, dst = src_ref.at[peer_idx], land_ref.at[k - 1]
            out.append(pltpu.make_async_remote_copy(
                src_ref=src, dst_ref=dst, send_sem=send_sems.at[7 * a + k - 1], recv_sem=recv_sems.at[7 * a + k - 1],
                device_id=peer, device_id_type=MESH))
    return out


def _split_start(name, gather, srcs):
    n = len(srcs)

    def body(*refs):
        send_sems, recv_sems = refs[n], refs[n + 1]
        for cp in _split_copies(gather, refs[:n], refs[2 * n + 2:3 * n + 2], send_sems, recv_sems):
            cp.start()
        token = refs[-1]
        token[...] = jnp.zeros_like(token)

    lands = [pltpu.HBM((N_DEV,) + a.shape if gather else (N_DEV - 1,) + a.shape[1:], a.dtype) for a in srcs]
    return pl.pallas_call(
        body, name=name,
        out_shape=(pltpu.SemaphoreType.DMA((7 * n,)), pltpu.SemaphoreType.DMA((7 * n,)),
                   *[pltpu.HBM(a.shape, a.dtype) for a in srcs], *lands, jax.ShapeDtypeStruct((8, 128), F32)),
        in_specs=(HBM_SPEC,) * n, out_specs=(SEM_SPEC, SEM_SPEC) + (HBM_SPEC,) * (2 * n) + (VMEM_SPEC,),
        input_output_aliases={i: 2 + i for i in range(n)}, compiler_params=_SPLIT_PARAMS,
    )(*[pltpu.with_memory_space_constraint(a, pltpu.HBM) for a in srcs])


def _split_wait(name, gather, started, after):
    send_sems, recv_sems, bufs = started[0], started[1], started[2:-1]
    n = len(bufs) // 2

    def body(*refs):
        for cp in _split_copies(gather, refs[:n], refs[n:2 * n], refs[2 * n], refs[2 * n + 1]):
            cp.wait_send()
            cp.wait_recv()

    out = pl.pallas_call(
        body, name=name, out_shape=tuple(pltpu.HBM(a.shape, a.dtype) for a in bufs),
        in_specs=(HBM_SPEC,) * (2 * n) + (SEM_SPEC, SEM_SPEC, ANY_SPEC), out_specs=(HBM_SPEC,) * (2 * n),
        input_output_aliases={i: i for i in range(2 * n)}, compiler_params=_SPLIT_PARAMS,
    )(*bufs, send_sems, recv_sems, after)
    return out[:n], out[n:]


def _adam_update(g, w, m, v):
    nm = ADAM_B1 * m + (1.0 - ADAM_B1) * g
    nv = ADAM_B2 * v + (1.0 - ADAM_B2) * (g * g)
    m_hat = nm / (1.0 - ADAM_B1 ** ADAM_STEP)
    v_hat = nv / (1.0 - ADAM_B2 ** ADAM_STEP)
    return -ADAM_LR * (m_hat / (jnp.sqrt(v_hat) + ADAM_EPS) + ADAM_WD * w), nm, nv


def _adamw(name, me, sent, got, w, m, v, tr):
    r, c = w.shape

    def body(me_ref, own_ref, got_ref, w_ref, m_ref, v_ref, g_ref, d_ref, nm_ref, nv_ref):
        g = own_ref[...].astype(F32)
        for k in range(N_DEV - 1):
            g = g + got_ref[k].astype(F32)
        g_ref[...] = g
        d_ref[...], nm_ref[...], nv_ref[...] = _adam_update(g, w_ref[...], m_ref[...], v_ref[...])

    blk = pl.BlockSpec((tr, c), lambda i, me_ref: (i, 0))
    return pl.pallas_call(
        body, name=name,
        grid_spec=pltpu.PrefetchScalarGridSpec(
            num_scalar_prefetch=1, grid=(r // tr,),
            in_specs=[pl.BlockSpec((None, tr, c), lambda i, me_ref: (me_ref[0], i, 0)),
                      pl.BlockSpec((N_DEV - 1, tr, c), lambda i, me_ref: (0, i, 0)), blk, blk, blk],
            out_specs=[blk] * 4),
        out_shape=[jax.ShapeDtypeStruct((r, c), F32)] * 4,
        compiler_params=_params(("arbitrary",)),
    )(me, sent, got, w, m, v)


def _adamw_small(srecv, ws, ms, vs):
    nv_ = len(ws)

    def body(*refs):
        s_ref = refs[0]
        ins, outs = refs[1:1 + 3 * nv_], refs[1 + 3 * nv_:]
        g_all = s_ref[0]
        for k in range(1, N_DEV):
            g_all = g_all + s_ref[k]
        for i in range(nv_):
            n = ins[i].shape[1]
            g = g_all[i:i + 1, :n]
            d, nm, nv = _adam_update(g, ins[i][...], ins[nv_ + i][...], ins[2 * nv_ + i][...])
            outs[i][...], outs[nv_ + i][...], outs[2 * nv_ + i][...], outs[3 * nv_ + i][...] = g, d, nm, nv
        outs[-1][...] = g_all[nv_:nv_ + 1, :128]

    shapes = [jax.ShapeDtypeStruct(a.shape, F32) for a in ws]
    res = pl.pallas_call(body, name="adamw_small", out_shape=shapes * 4 + [jax.ShapeDtypeStruct((1, 128), F32)],
                         compiler_params=_params())(srecv, *ws, *ms, *vs)
    return [res[k * nv_:(k + 1) * nv_] for k in range(4)], res[-1]


def _cols_from_shards(a):
    return jnp.swapaxes(a, 0, 1).reshape(a.shape[1], a.shape[0] * a.shape[2])


def _shards_from_cols(a):
    return jnp.swapaxes(a.reshape(a.shape[0], N_DEV, a.shape[1] // N_DEV), 0, 1)


def _shards_from_rows(a):
    return a.reshape(N_DEV, a.shape[0] // N_DEV, a.shape[1])


def _pair_lanes(a):
    lead = a.shape[:-1]
    return a.reshape(lead + (2, 2, HEAD_DIM // 2)).swapaxes(-3, -2).reshape(lead + (128,))


def _split_w_in(w_in):
    rows = w_in.shape[0]
    dil = w_in[:, :3 * DIL_WIDTH].reshape(rows, 3, 3, 4, 128)
    dil = jnp.concatenate([_pair_lanes(dil[:, :2]), dil[:, 2:]], axis=1)
    dil = dil.transpose(0, 2, 3, 1, 4).reshape(rows, 3 * DIL_WIDTH)
    o = 3 * DIL_WIDTH
    qb = w_in[:, o:o + SWA_Q_WIDTH].reshape(rows, 2, 4, HEAD_DIM).transpose(0, 2, 1, 3).reshape(rows, 4, 128)
    qb = _pair_lanes(qb).reshape(rows, SWA_Q_WIDTH)
    kb = _pair_lanes(w_in[:, o + SWA_Q_WIDTH:o + SWA_Q_WIDTH + SWA_KV_WIDTH])
    vb = w_in[:, o + SWA_Q_WIDTH + SWA_KV_WIDTH:P_WIDTH]
    return jnp.concatenate([dil, qb, kb, vb], axis=1)


def _merge_w_in(dw_p, dw_g):
    rows = dw_p.shape[0]
    dil = dw_p[:, :3 * DIL_WIDTH].reshape(rows, 3, 4, 3, 128).transpose(0, 3, 1, 2, 4)
    dil = jnp.concatenate([_pair_lanes(dil[:, :2]), dil[:, 2:]], axis=1).reshape(rows, 3 * DIL_WIDTH)
    o = 3 * DIL_WIDTH
    qb = _pair_lanes(dw_p[:, o:o + SWA_Q_WIDTH].reshape(rows, 4, 128))
    qb = qb.reshape(rows, 4, 2, HEAD_DIM).transpose(0, 2, 1, 3).reshape(rows, SWA_Q_WIDTH)
    kb = _pair_lanes(dw_p[:, o + SWA_Q_WIDTH:o + SWA_Q_WIDTH + SWA_KV_WIDTH])
    vb = dw_p[:, o + SWA_Q_WIDTH + SWA_KV_WIDTH:]
    return jnp.concatenate([dil, qb, kb, vb, dw_g], axis=1)


def _swa_rows(w_b):
    return w_b.reshape(2, 4, HEAD_DIM, -1).transpose(1, 0, 2, 3).reshape(SWA_Q_WIDTH, -1)


def _swa_rows_inv(dw_b):
    return dw_b.reshape(4, 2, HEAD_DIM, -1).transpose(1, 0, 2, 3).reshape(SWA_Q_WIDTH, -1)


def _rope_tables(pos):
    half = HEAD_DIM // 2
    inv = ROPE_THETA ** (-jnp.arange(half, dtype=F32) / half)
    ang = pos.astype(F32)[:, None] * jnp.tile(inv, 4)
    sign = jnp.repeat(jnp.array([-1.0, 1.0], F32), 2 * half)
    return jnp.cos(ang), jnp.sin(ang) * sign


def _local_step(x, hs, mem, pos, target, w_in, dep, rest_weights, on_grads, g_mix, g_cross, g_mem, g_mlp, g_final, sink):
    t = x.shape[0]
    tm = min(512, t)
    tq = 1024
    tw = min(2048, t)
    w_p = _split_w_in(w_in)
    cos, sin = lax.optimization_barrier(_rope_tables(pos))
    sink_row = jnp.pad(sink.reshape(2, 4).T.reshape(1, 8), ((0, 0), (0, 120)))
    tabs = [(cos[None], sin[None])]
    for _, d in DIL_GROUPS[1:]:
        tabs.append(tuple(a.reshape(t // d, d, 128).swapaxes(0, 1) for a in (cos, sin)))
    tabs.append(tabs[0])

    h, h1, h2 = hs
    p0, p1, p2, pb = _inproj(h, h1, h2, w_p, [(cos, sin), tabs[1], tabs[2]], tm, dep)
    ps = [p0[None], p1, p2, pb[None]]
    outs, lses = [], []
    for gi, pv in enumerate(ps):
        res = _attn_fwd(f"attn_fwd{gi}", pv, gi == 3, sink_row[0, :8], min(tq, pv.shape[1]))
        outs.append(res[0])
        lses.append(res[1])
    o0, l0, ob, lb, ob32 = outs[0][0], lses[0][0], outs[3][0], lses[3][0], res[2][0]
    wts = rest_weights(lb)
    w_b = _swa_rows(wts["w_branch_b"])
    tf = 2048
    w_g = wts["w_g"]
    gts = _gates(h, w_g, wts["b_gate"].reshape(1, GATE_WIDTH), min(1024, t), 1024)
    oa, ya, yb, merged, x1, hc = _mix(o0, outs[1], outs[2], l0, lses[1], lses[2], ob, gts, x,
                                      wts["w_branch_a"], w_b, wts["w_out"], g_cross, tm)
    mn, kv = _memkv(mem, g_mem, wts["w_ckv"])
    q, o, x2, hm = _cross(hc, x1, kv, wts["w_cq"], wts["w_co"], g_mlp, tm)
    a, dx3, loss, dg_final = _mlp(hm, x2, wts["w_1"], wts["w_2"], g_final.reshape(1, D_MODEL), target, tm, tf)

    grads = {}
    dz, dx2, dg_mlp = _mlp_bwd(dx3, a, wts["w_1"], wts["w_2"], x2, g_mlp, tm, tf)
    grads["w_2"] = _shards_from_rows(_wgrad("dw_2", a, dx3, 1024, 1024, tw, square=True))
    grads["w_1"] = _wgrad("dw_1", hm, dz, 1024, 1024, tw, col_shards=True)
    dep = on_grads(GROUP_A, grads)
    dq, dx1, dkv, dg_cross = _cross_bwd(dx2, x1, q, kv, wts["w_cq"], wts["w_co"], g_cross, tm, dep)
    grads["w_co"] = _shards_from_rows(_wgrad("dw_co", o, dx2, 1024, 1024, tw))
    grads["w_cq"] = _shards_from_rows(_wgrad("dw_cq", hc, dq, 1024, 1024, tw))
    grads["w_ckv"], dg_mem = _memkv_bwd(dkv, mn, mem, wts["w_ckv"], g_mem)
    dgt, dh_part, dya, dyb, db_gate = _merge_bwd(dx1, ya, yb, gts, wts["w_out"], w_g, tm)
    do0, do1, do2, c0, c1, c2, dob, cb, dsink = _combine_bwd(
        dya, dyb, oa, ob32, l0, lses[1], lses[2], lb, sink_row, wts["w_branch_a"], w_b, tm)
    grads["w_out"] = _shards_from_rows(_wgrad("dw_out", merged, dx1, 1024, 1024, tw))
    grads["w_branch_a"] = _shards_from_cols(_wgrad("dw_a", oa, dya, 512, 1024, tw))
    grads["w_branch_b"] = _shards_from_cols(_swa_rows_inv(_wgrad("dw_b", ob, dyb, 512, 1024, tw)))
    grads["b_gate"] = _shards_from_cols(db_gate.reshape(2, D_MODEL)).astype(BF16)
    dep = on_grads(GROUP_B, grads)
    dw_g = _wgrad("dw_g", h, dgt, 1024, 1024, tw)
    dps = []
    for gi, (pv, do_g, c_g) in enumerate(zip(ps, (do0[None], do1, do2, dob[None]), (c0[None], c1, c2, cb[None]))):
        dps.append(_attn_bwd(f"attn_bwd{gi}", pv, do_g, lses[gi], c_g, tabs[gi][0], tabs[gi][1], gi == 3,
                             min(tq, pv.shape[1]),
                             dep if gi == 0 else None))
    dw_p = jnp.concatenate(
        [_wgrad(f"dw_p{gi}", hh.reshape(t, D_MODEL), dpg.reshape(t, -1), 1024, PBLK, tw)
         for gi, (hh, dpg) in enumerate(zip((h, h1, h2, h), dps))], axis=1)
    grads["w_in"] = _shards_from_cols(_merge_w_in(dw_p, dw_g))
    dep = on_grads(GROUP_C, grads)
    grad_x, dg_mix = _dx(dps[0][0], dps[1], dps[2], dps[3][0], w_p, dh_part, dx1, x, g_mix, tm, dep)
    dsink_heads = dsink[0, :8].reshape(4, 2).T.reshape(8)
    small = {"g_mix": dg_mix[0], "g_cross": dg_cross[0], "g_mem": dg_mem[0], "g_mlp": dg_mlp[0],
             "g_final": dg_final[0], "sink": dsink_heads}
    return loss[0, 0], grad_x, small


def kernel(x, mem, positions, g_mix, w_in, b_gate, sink, w_branch_a, w_branch_b, w_out, g_cross, g_mem, w_cq, w_ckv, w_co, g_mlp, w_1, w_2, g_final, loss_target, m_g_mix, m_w_in, m_b_gate, m_sink, m_w_branch_a, m_w_branch_b, m_w_out, m_g_cross, m_g_mem, m_w_cq, m_w_ckv, m_w_co, m_g_mlp, m_w_1, m_w_2, m_g_final, v_g_mix, v_w_in, v_b_gate, v_sink, v_w_branch_a, v_w_branch_b, v_w_out, v_g_cross, v_g_mem, v_w_cq, v_w_ckv, v_w_co, v_g_mlp, v_w_1, v_w_2, v_g_final):
    local = dict(locals())
    shard = {n: local[n][0] for n in GROUP_A + GROUP_B + GROUP_C}
    me = _my_index()
    me_arr = me.reshape(1).astype(jnp.int32)
    tags = {GROUP_A: "a", GROUP_B: "b", GROUP_C: "c"}

    gathered_w_in, *hs = _all_gather(shard["w_in"].astype(BF16), x[0], g_mix, min(512, x.shape[1]))
    w_in_full = _cols_from_shards(gathered_w_in)
    rest = GROUP_A + GROUP_B

    def gathered(name, started, after):
        srcs, lands = _split_wait(name, True, started, after)
        return [lax.dynamic_update_slice(land, src[None], (me,) + (0,) * src.ndim) for src, land in zip(srcs, lands)]

    gather = _split_start("gather_start", True,
                          [shard[n] if n == "b_gate" else shard[n].astype(BF16) for n in rest])

    def rest_weights(after):
        full = {"w_g": w_in_full[:, P_WIDTH:]}
        for name, a in zip(rest, gathered("gather_wait", gather, after)):
            if name in ("w_1", "w_ckv"):
                full[name] = a
            elif name in _COL_SHARDED:
                full[name] = _cols_from_shards(a)
            else:
                full[name] = a.reshape(N_DEV * a.shape[1], a.shape[2])
        return full

    scatters = {}

    def on_grads(names, grads):
        scatters[names] = _split_start("scatter_start_" + tags[names], False, [grads[n] for n in names])
        return scatters[names][-1]

    loss, grad_x, small = _local_step(
        x[0], hs, mem[0], positions[0], loss_target[0], w_in_full, gather[-1], rest_weights, on_grads,
        g_mix, g_cross, g_mem, g_mlp, g_final, sink[0])

    sp = jnp.stack([small[n] if n != "sink" else jnp.pad(small[n], (0, LANES - 8)) for n in SMALL]
                   + [jnp.pad(loss.reshape(1), (0, LANES - 1)), jnp.zeros((LANES,), F32)])
    small_gather = _split_start("small_start", True, [sp])

    after, updated = small_gather[-1], {}
    for names in (GROUP_A, GROUP_B, GROUP_C):
        sent, got = _split_wait("scatter_wait_" + tags[names], False, scatters[names], after)
        for i, name in enumerate(names):
            outs = _adamw("adamw_" + name, me_arr, sent[i], got[i], shard[name],
                          local["m_" + name][0], local["v_" + name][0], ADAM_ROWS[name])
            updated[name] = [a[None] for a in outs]
            after = outs[3]

    flat = lambda prefix: [local[prefix + n].reshape(1, -1) for n in SMALL]
    outs, loss_row = _adamw_small(gathered("small_wait", small_gather, after)[0], flat(""), flat("m_"), flat("v_"))
    for i, name in enumerate(SMALL):
        updated[name] = [outs[which][i].reshape(local[name].shape) for which in range(4)]

    order = ["g_mix", "w_in", "b_gate", "sink", "w_branch_a", "w_branch_b", "w_out", "g_cross", "g_mem", "w_cq",
             "w_ckv", "w_co", "g_mlp", "w_1", "w_2", "g_final"]
    res = [loss_row[0, 0], grad_x[None]]
    for which in range(4):
        res += [updated[n][which] for n in order]
    return tuple(res)
```

```python
import functools
import math

import jax
import jax.numpy as jnp
from jax import lax
from jax.experimental import pallas as pl
from jax.experimental.pallas import tpu as pltpu

F32 = jnp.float32
BF16 = jnp.bfloat16

D_MODEL = 1024
HEAD_DIM = 64
DIL_GROUPS = ((128, 1), (512, 4), (2048, 16))
ROPE_THETA = 10000.0
X_HEADS = 4
X_HEAD_DIM = D_MODEL // X_HEADS
D_FF = 4 * D_MODEL
EPS = 1e-6
DIL_WIDTH = 1536
SWA_Q_WIDTH = 512
SWA_KV_WIDTH = 128
P_WIDTH = 3 * DIL_WIDTH + SWA_Q_WIDTH + 2 * SWA_KV_WIDTH
GATE_WIDTH = 2 * D_MODEL
IN_WIDTH = P_WIDTH + GATE_WIDTH
BAND = 128
PBLK = 768
Q_SCALE = HEAD_DIM ** -0.5
X_SCALE = X_HEAD_DIM ** -0.5

ADAM_LR = 0.001
ADAM_B1 = 0.9
ADAM_B2 = 0.999
ADAM_EPS = 1e-08
ADAM_WD = 0.01
ADAM_STEP = 10

N_DEV = 8
LANES = 1024
VMEM_LIMIT = 52 * 1024 * 1024

NT = (((1,), (1,)), ((), ()))
TN = (((0,), (0,)), ((), ()))

GROUP_A = ("w_1", "w_2")
GROUP_B = ("w_branch_a", "w_branch_b", "w_out", "w_cq", "w_ckv", "w_co", "b_gate")
GROUP_C = ("w_in",)
_COL_SHARDED = ("w_in", "w_branch_a", "w_branch_b", "w_ckv", "w_1", "b_gate")
ADAM_ROWS = {"w_in": 464, "w_branch_a": 512, "w_branch_b": 512, "w_out": 128, "w_cq": 128, "w_ckv": 512,
             "w_co": 128, "w_1": 256, "w_2": 256, "b_gate": 2}
ADAM_COLS = {"w_in": 256}
SMALL = ("g_mix", "g_cross", "g_mem", "g_mlp", "g_final", "sink")


def _params(sem=None):
    return pltpu.CompilerParams(dimension_semantics=sem, vmem_limit_bytes=VMEM_LIMIT)


def _dot(a, b):
    return jnp.dot(a, b, preferred_element_type=F32)


def _dot_nt(a, b):
    return lax.dot_general(a, b, NT, preferred_element_type=F32)


def _dot_tn(a, b):
    return lax.dot_general(a, b, TN, preferred_element_type=F32)


def _rms(xt):
    return lax.rsqrt(jnp.mean(xt * xt, axis=-1, keepdims=True) + EPS)


def _rms_bwd(dh, xt, r, g):
    xn = xt * r
    dxn = dh * g
    dx = r * (dxn - xn * jnp.mean(dxn * xn, axis=-1, keepdims=True))
    return dx, jnp.sum(dh * xn, axis=0, keepdims=True)


def _rope(x, c, s, swa, sign):
    kinds = "qqqqkv" if swa else "qkvqkv"
    cq, sq = c * Q_SCALE, s * (sign * Q_SCALE)
    sk = s * sign if sign != 1 else s
    out = []
    for ci, kind in enumerate(kinds):
        xc = x[:, ci * 128:(ci + 1) * 128]
        if kind == "v":
            out.append(xc)
        elif kind == "q":
            out.append(xc * cq + pltpu.roll(xc, 64, 1) * sq)
        else:
            out.append(xc * c + pltpu.roll(xc, 64, 1) * sk)
    return jnp.concatenate(out, axis=1)


def _lane_scratch(rows, w):
    return pltpu.VMEM((w // 128, rows, 128), F32)


def _deinterleave(val, scr_ref, dst_ref, dtype):
    d, n = dst_ref.shape[0], dst_ref.shape[1]
    nc = val.shape[1] // 128
    for c in range(nc):
        scr_ref[c] = val[:, c * 128:(c + 1) * 128]
    for r in range(d):
        rows = [scr_ref.at[c][pl.ds(r, n, stride=d), :] for c in range(nc)]
        dst_ref[r] = jnp.concatenate(rows, axis=1).astype(dtype)


def _res_spec(a, tm):
    d, w = a.shape[0], a.shape[2]
    return pl.BlockSpec((d, tm // d, w), lambda i: (0, i, 0))


def _interleave(src_ref, scr_ref):
    d, n = src_ref.shape[0], src_ref.shape[1]
    nc = src_ref.shape[2] // 128
    for r in range(d):
        v = src_ref[r].astype(F32)
        for c in range(nc):
            scr_ref.at[c][pl.ds(r, n, stride=d), :] = v[:, c * 128:(c + 1) * 128]
    return jnp.concatenate([scr_ref[c] for c in range(nc)], axis=1)


def _with_dep(body, n_in, dep):
    if dep is None:
        return body
    return lambda *refs: body(*refs[:n_in], *refs[n_in + 1:])


def _dep_spec(dep):
    return [] if dep is None else [pl.BlockSpec(memory_space=pl.ANY)]


def _dep_arg(dep):
    return [] if dep is None else [dep]


def _inproj(h, h1, h2, w_p, tabs, tm, dep=None):
    t = h.shape[0]
    gw = 2 * PBLK
    (cos, sin), (cos1, sin1), (cos2, sin2) = tabs[0], tabs[1], tabs[2]

    def body(h_ref, h1_ref, h2_ref, w_ref, c_ref, s_ref, c1_ref, s1_ref, c2_ref, s2_ref,
             p0_ref, p1_ref, p2_ref, pb_ref):
        rows = lambda ref: ref[...].reshape(tm, ref.shape[-1])
        groups = ((h_ref, c_ref, s_ref, p0_ref), (h1_ref, c1_ref, s1_ref, p1_ref), (h2_ref, c2_ref, s2_ref, p2_ref))
        for gi, (lhs_ref, cc_ref, ss_ref, out_ref) in enumerate(groups):
            lhs, cc, ss = rows(lhs_ref), rows(cc_ref), rows(ss_ref)
            for half in range(2):
                col = gi * gw + half * PBLK
                val = _rope(_dot(lhs, w_ref[:, col:col + PBLK]), cc, ss, False, 1).astype(BF16)
                if out_ref.ndim == 3:
                    out_ref[:, :, half * PBLK:(half + 1) * PBLK] = val.reshape(out_ref.shape[:2] + (PBLK,))
                else:
                    out_ref[:, half * PBLK:(half + 1) * PBLK] = val
        pb_ref[...] = _rope(_dot(h_ref[...], w_ref[:, 3 * gw:]), c_ref[...], s_ref[...], True, 1).astype(BF16)

    d1, d2 = DIL_GROUPS[1][1], DIL_GROUPS[2][1]
    row = lambda w: pl.BlockSpec((tm, w), lambda i: (i, 0))
    res = lambda d, w: pl.BlockSpec((d, tm // d, w), lambda i: (0, i, 0))
    sds = jax.ShapeDtypeStruct
    return pl.pallas_call(
        _with_dep(body, 10, dep), name="inproj", grid=(t // tm,),
        in_specs=[row(D_MODEL), res(d1, D_MODEL), res(d2, D_MODEL),
                  pl.BlockSpec((D_MODEL, P_WIDTH), lambda i: (0, 0), pipeline_mode=pl.Buffered(1)),
                  row(128), row(128), res(d1, 128), res(d1, 128), res(d2, 128), res(d2, 128)] + _dep_spec(dep),
        out_specs=[row(gw), res(d1, gw), res(d2, gw), row(PBLK)],
        out_shape=[sds((t, gw), BF16), sds((d1, t // d1, gw), BF16), sds((d2, t // d2, gw), BF16),
                   sds((t, PBLK), BF16)],
        compiler_params=_params(("arbitrary",)),
    )(h, h1, h2, w_p, cos, sin, cos1, sin1, cos2, sin2, *_dep_arg(dep))


def _gates(h, w_g, b, tm, tn):
    t = h.shape[0]

    def body(h_ref, w_ref, b_ref, o_ref):
        z = _dot(h_ref[...], w_ref[...]) + b_ref[...]
        o_ref[...] = (0.5 * jnp.tanh(0.5 * z) + 0.5).astype(BF16)

    return pl.pallas_call(
        body, name="gates", grid=(t // tm, GATE_WIDTH // tn),
        in_specs=[pl.BlockSpec((tm, D_MODEL), lambda i, j: (i, 0)),
                  pl.BlockSpec((D_MODEL, tn), lambda i, j: (0, j)),
                  pl.BlockSpec((1, tn), lambda i, j: (0, j))],
        out_specs=pl.BlockSpec((tm, tn), lambda i, j: (i, j)),
        out_shape=jax.ShapeDtypeStruct((t, GATE_WIDTH), BF16),
        compiler_params=_params(("arbitrary", "arbitrary")),
    )(h, w_g, b)


def _band_mask(i, s):
    row = lax.broadcasted_iota(jnp.int32, (BAND, 2 * BAND), 0)
    col = lax.broadcasted_iota(jnp.int32, (BAND, 2 * BAND), 1)
    band = (col >= row) & (col <= row + BAND)
    if s == 0:
        band = band & ((col >= BAND) | (i > 0))
    return band


def _head_a_masks(rows):
    lane = lax.broadcasted_iota(jnp.int32, (rows, 128), 1)
    return (lane % HEAD_DIM) < HEAD_DIM // 2, lane < HEAD_DIM


def _stack_heads(x, head_a):
    zero = jnp.zeros_like(x)
    return jnp.concatenate([jnp.where(head_a, x, zero), jnp.where(head_a, zero, x)], axis=0)


def _stack_heads_t(xt, head_a_t):
    zero = jnp.zeros_like(xt)
    return jnp.concatenate([jnp.where(head_a_t, xt, zero), jnp.where(head_a_t, zero, xt)], axis=1)


def _kv_rows(cur_ref, tail_ref, s, off):
    if s == 0:
        return jnp.concatenate([tail_ref[:, off:off + 128], cur_ref[0:BAND, off:off + 128]], axis=0)
    return cur_ref[(s - 1) * BAND:(s + 1) * BAND, off:off + 128]


def _attn_layout(swa):
    if swa:
        return [(128 * j, 512, 640) for j in range(4)]
    return [(0, 128, 256), (384, 512, 640)]


def _attn_fwd(name, pv, swa, sinks, tq):
    d, ls = pv.shape[0], pv.shape[1]
    n, nsb = ls // tq, tq // BAND
    pairs = _attn_layout(swa)
    ncol = 1 if swa else 2
    ow = 128 * len(pairs)

    def body(cur_ref, tail_ref, *rest):
        sink_ref, o_ref, lse_ref, o32_ref = rest if swa else (None,) + rest + (None,)
        i = pl.program_id(2)
        lane = lax.broadcasted_iota(jnp.int32, (BAND, 128), 1)
        qk_a, v_a = _head_a_masks(BAND)
        first = lax.broadcasted_iota(jnp.int32, (2 * BAND, 1), 0) < BAND
        for s in range(nsb):
            mask = _band_mask(i, s)
            mask2 = jnp.concatenate([mask, mask], axis=0)
            rows = slice(s * BAND, (s + 1) * BAND)
            lse_tile = jnp.zeros((BAND, 128), F32)
            for j, (qo, ko, vo) in enumerate(pairs):
                q = cur_ref[rows, qo:qo + 128]
                kk = _kv_rows(cur_ref, tail_ref, s, ko)
                vv = _kv_rows(cur_ref, tail_ref, s, vo)
                sc = _dot_nt(_stack_heads(q, qk_a), kk)
                sc = jnp.where(mask2, sc, -jnp.inf)
                m = jnp.max(sc, axis=-1, keepdims=True)
                if swa:
                    sk = jnp.where(first, sink_ref[2 * j], sink_ref[2 * j + 1])
                    m = jnp.maximum(m, sk)
                p = jnp.exp(sc - m)
                den = jnp.sum(p, axis=-1, keepdims=True)
                if swa:
                    den = den + jnp.exp(sk - m)
                lse = m + jnp.log(den)
                lse_tile = jnp.where(lane == 2 * j, lse[:BAND], jnp.where(lane == 2 * j + 1, lse[BAND:], lse_tile))
                o2 = _dot(p.astype(BF16), vv) * (1.0 / den)
                o = jnp.where(v_a, o2[:BAND], o2[BAND:])
                o_ref[rows, j * 128:(j + 1) * 128] = o.astype(BF16)
                if swa:
                    o32_ref[rows, j * 128:(j + 1) * 128] = o
            lse_ref[rows, :] = lse_tile

    in_specs = [pl.BlockSpec((None, tq, PBLK), lambda r, cb, i: (r, i, cb)),
                pl.BlockSpec((None, BAND, PBLK), lambda r, cb, i: (r, jnp.maximum(i * nsb - 1, 0), cb))]
    args = [pv, pv]
    out_specs = [pl.BlockSpec((None, tq, ow), lambda r, cb, i: (r, i, cb)),
                 pl.BlockSpec((None, tq, 128), lambda r, cb, i: (r, i, cb))]
    out_shape = [jax.ShapeDtypeStruct((d, ls, 512), BF16), jax.ShapeDtypeStruct((d, ls, 128 * ncol), F32)]
    if swa:
        in_specs.append(pl.BlockSpec(memory_space=pltpu.SMEM))
        args.append(sinks)
        out_specs.append(out_specs[0])
        out_shape.append(jax.ShapeDtypeStruct((d, ls, 512), F32))
    return pl.pallas_call(
        body, name=name, grid=(d, ncol, n),
        in_specs=in_specs, out_specs=out_specs, out_shape=out_shape,
        compiler_params=_params(("arbitrary", "arbitrary", "arbitrary")),
    )(*args)


def _lse_lane(head):
    return (head // 4) * 128 + head % 4


def _dil_head_spread():
    lane = lax.broadcasted_iota(jnp.int32, (256, 512), 0)
    head = lax.broadcasted_iota(jnp.int32, (256, 512), 1) // HEAD_DIM
    return (lane == _lse_lane(head)).astype(BF16)


def _head_scale(x, tile, spread):
    return x * _dot(tile.astype(BF16), spread)


def _head_gather(width, dil):
    head = lax.broadcasted_iota(jnp.int32, (8 * HEAD_DIM, width), 0) // HEAD_DIM
    lane = lax.broadcasted_iota(jnp.int32, (8 * HEAD_DIM, width), 1)
    return (lane == (_lse_lane(head) if dil else head)).astype(BF16)


def _head_sums(x, gather):
    hi = x.astype(BF16)
    lo = (x - hi.astype(F32)).astype(BF16)
    return _dot(hi, gather) + _dot(lo, gather)


def _alphas(l0, l1, l2):
    m = jnp.maximum(jnp.maximum(l0, l1), l2)
    e0, e1, e2 = jnp.exp(l0 - m), jnp.exp(l1 - m), jnp.exp(l2 - m)
    den = e0 + e1 + e2
    return e0 / den, e1 / den, e2 / den


def _mix(o0, o1, o2, l0, l1, l2, ob, gts, x, w_a, w_b, w_out, g_cross, tm):
    t = x.shape[0]

    def body(o0_ref, o1_ref, o2_ref, l0_ref, l1_ref, l2_ref, ob_ref, g_ref, x_ref, wa_ref, wb_ref, wo_ref,
             gc_ref, oa_ref, ya_ref, yb_ref, mg_ref, x1_ref, hc_ref, so_ref, sl_ref):
        a0, a1, a2 = _alphas(l0_ref[...], _interleave(l1_ref, sl_ref), _interleave(l2_ref, sl_ref))
        spread = _dil_head_spread()
        oa = (_head_scale(o0_ref[...].astype(F32), a0, spread)
              + _head_scale(_interleave(o1_ref, so_ref), a1, spread)
              + _head_scale(_interleave(o2_ref, so_ref), a2, spread))
        oab = oa.astype(BF16)
        oa_ref[...] = oab
        ya = _dot(oab, wa_ref[...])
        yb = _dot(ob_ref[...], wb_ref[...])
        ya_ref[...] = ya.astype(BF16)
        yb_ref[...] = yb.astype(BF16)
        merged = (g_ref[:, :D_MODEL].astype(F32) * ya + g_ref[:, D_MODEL:].astype(F32) * yb).astype(BF16)
        mg_ref[...] = merged
        x1 = x_ref[...] + _dot(merged, wo_ref[...])
        x1_ref[...] = x1
        hc_ref[...] = (x1 * _rms(x1) * gc_ref[...]).astype(BF16)

    row = lambda w: pl.BlockSpec((tm, w), lambda i: (i, 0))
    full = lambda a, b: pl.BlockSpec((a, b), lambda i: (0, 0))
    return pl.pallas_call(
        body, name="mix", grid=(t // tm,),
        in_specs=[row(512), _res_spec(o1, tm), _res_spec(o2, tm), row(256), _res_spec(l1, tm), _res_spec(l2, tm),
                  row(512), row(GATE_WIDTH),
                  row(D_MODEL), full(512, D_MODEL), full(512, D_MODEL), full(D_MODEL, D_MODEL), full(1, D_MODEL)],
        out_specs=[row(512), row(D_MODEL), row(D_MODEL), row(D_MODEL), row(D_MODEL), row(D_MODEL)],
        out_shape=[jax.ShapeDtypeStruct((t, 512), BF16), jax.ShapeDtypeStruct((t, D_MODEL), BF16),
                   jax.ShapeDtypeStruct((t, D_MODEL), BF16), jax.ShapeDtypeStruct((t, D_MODEL), BF16),
                   jax.ShapeDtypeStruct((t, D_MODEL), F32), jax.ShapeDtypeStruct((t, D_MODEL), BF16)],
        scratch_shapes=[_lane_scratch(tm, 512), _lane_scratch(tm, 256)],
        compiler_params=_params(("arbitrary",)),
    )(o0, o1, o2, l0, l1, l2, ob, gts, x, w_a, w_b, w_out, g_cross)


def _memkv(mem, g_mem, w_ckv):
    m = mem.shape[0]
    ws = w_ckv.shape[2]

    def body(mem_ref, g_ref, w_ref, mn_ref, kv_ref):
        xt = mem_ref[...]
        mn = (xt * _rms(xt) * g_ref[...]).astype(BF16)
        mn_ref[...] = mn
        for j in range(N_DEV):
            kv_ref[:, j * ws:(j + 1) * ws] = _dot(mn, w_ref[j]).astype(BF16)

    return pl.pallas_call(
        body, name="memkv",
        out_shape=[jax.ShapeDtypeStruct((m, D_MODEL), BF16), jax.ShapeDtypeStruct((m, 2 * D_MODEL), BF16)],
        compiler_params=_params(),
    )(mem, g_mem, w_ckv)


def _cross_probs(q, kv_ref, h):
    k = kv_ref[:, h * X_HEAD_DIM:(h + 1) * X_HEAD_DIM]
    sc = _dot_nt(q[:, h * X_HEAD_DIM:(h + 1) * X_HEAD_DIM], k)
    m = jnp.max(sc, axis=-1, keepdims=True)
    p = jnp.exp(sc - m)
    return p / jnp.sum(p, axis=-1, keepdims=True)


def _cross(hc, x1, kv, w_cq, w_co, g_mlp, tm):
    t = x1.shape[0]
    m = kv.shape[0]

    def body(hc_ref, x1_ref, kv_ref, wq_ref, wo_ref, g_ref, q_ref, o_ref, x2_ref, hm_ref):
        q = (_dot(hc_ref[...], wq_ref[...]) * X_SCALE).astype(BF16)
        q_ref[...] = q
        outs = []
        for h in range(X_HEADS):
            p = _cross_probs(q, kv_ref, h)
            v = kv_ref[:, D_MODEL + h * X_HEAD_DIM:D_MODEL + (h + 1) * X_HEAD_DIM]
            outs.append(_dot(p.astype(BF16), v))
        o = jnp.concatenate(outs, axis=1).astype(BF16)
        o_ref[...] = o
        x2 = x1_ref[...] + _dot(o, wo_ref[...])
        x2_ref[...] = x2
        hm_ref[...] = (x2 * _rms(x2) * g_ref[...]).astype(BF16)

    row = lambda w: pl.BlockSpec((tm, w), lambda i: (i, 0))
    full = lambda a, b: pl.BlockSpec((a, b), lambda i: (0, 0))
    return pl.pallas_call(
        body, name="cross", grid=(t // tm,),
        in_specs=[row(D_MODEL), row(D_MODEL), full(m, 2 * D_MODEL), full(D_MODEL, D_MODEL),
                  full(D_MODEL, D_MODEL), full(1, D_MODEL)],
        out_specs=[row(D_MODEL)] * 4,
        out_shape=[jax.ShapeDtypeStruct((t, D_MODEL), BF16), jax.ShapeDtypeStruct((t, D_MODEL), BF16),
                   jax.ShapeDtypeStruct((t, D_MODEL), F32), jax.ShapeDtypeStruct((t, D_MODEL), BF16)],
        compiler_params=_params(("arbitrary",)),
    )(hc, x1, kv, w_cq, w_co, g_mlp)


def _mlp(hm, x2, w_1, w_2, g_final, target, tm, tf):
    t = x2.shape[0]
    nf = D_FF // tf

    def body(hm_ref, x2_ref, w1_ref, w2_ref, g_ref, tg_ref, a_ref, dx3_ref, loss_ref, dg_ref, acc_ref):
        i, f = pl.program_id(0), pl.program_id(1)
        hm_t = hm_ref[...]
        sw = w1_ref.shape[2]
        part = None
        for s in range(w1_ref.shape[0]):
            a = jnp.maximum(_dot(hm_t, w1_ref[s]), 0.0).astype(BF16)
            a_ref[:, s * sw:(s + 1) * sw] = a
            p_s = _dot(a * a, w2_ref[s * sw:(s + 1) * sw, :])
            part = p_s if part is None else part + p_s

        @pl.when(f == 0)
        def _():
            acc_ref[...] = part

        @pl.when(f > 0)
        def _():
            acc_ref[...] += part

        @pl.when((i == 0) & (f == 0))
        def _():
            loss_ref[...] = jnp.zeros_like(loss_ref)
            dg_ref[...] = jnp.zeros_like(dg_ref)

        @pl.when(f == nf - 1)
        def _():
            x3 = x2_ref[...] + acc_ref[...]
            r = _rms(x3)
            g = g_ref[...]
            diff = x3 * r * g - tg_ref[...]
            loss_ref[...] += 0.5 * jnp.sum(jnp.mean(diff * diff, axis=-1, keepdims=True))
            dx3, dg = _rms_bwd(diff / D_MODEL, x3, r, g)
            dx3_ref[...] = dx3
            dg_ref[...] += dg

    return pl.pallas_call(
        body, name="mlp", grid=(t // tm, nf),
        in_specs=[pl.BlockSpec((tm, D_MODEL), lambda i, f: (i, 0)),
                  pl.BlockSpec((tm, D_MODEL), lambda i, f: (i, 0)),
                  pl.BlockSpec((tf // w_1.shape[2], D_MODEL, w_1.shape[2]), lambda i, f: (f, 0, 0)),
                  pl.BlockSpec((tf, D_MODEL), lambda i, f: (f, 0)),
                  pl.BlockSpec((1, D_MODEL), lambda i, f: (0, 0)),
                  pl.BlockSpec((tm, D_MODEL), lambda i, f: (i, 0))],
        out_specs=[pl.BlockSpec((tm, tf), lambda i, f: (i, f)),
                   pl.BlockSpec((tm, D_MODEL), lambda i, f: (i, 0)),
                   pl.BlockSpec((1, 128), lambda i, f: (0, 0)),
                   pl.BlockSpec((1, D_MODEL), lambda i, f: (0, 0))],
        out_shape=[jax.ShapeDtypeStruct((t, D_FF), BF16), jax.ShapeDtypeStruct((t, D_MODEL), F32),
                   jax.ShapeDtypeStruct((1, 128), F32), jax.ShapeDtypeStruct((1, D_MODEL), F32)],
        scratch_shapes=[pltpu.VMEM((tm, D_MODEL), F32)],
        compiler_params=_params(("arbitrary", "arbitrary")),
    )(hm, x2, w_1, w_2, g_final, target)


def _mlp_bwd(dx3, a, w_1, w_2, x2, g_mlp, tm, tf):
    t = x2.shape[0]
    nf = D_FF // tf

    def body(dx3_ref, a_ref, w1_ref, w2_ref, x2_ref, g_ref, dz_ref, dx2_ref, dg_ref, acc_ref):
        i, f = pl.program_id(0), pl.program_id(1)
        dx3_b = dx3_ref[...].astype(BF16)
        sw = w1_ref.shape[2]
        part = None
        for s in range(w1_ref.shape[0]):
            cols = slice(s * sw, (s + 1) * sw)
            da2 = _dot_nt(dx3_b, w2_ref[cols, :])
            dz = (2.0 * a_ref[:, cols].astype(F32) * da2).astype(BF16)
            dz_ref[:, cols] = dz
            p_s = _dot_nt(dz, w1_ref[s])
            part = p_s if part is None else part + p_s

        @pl.when(f == 0)
        def _():
            acc_ref[...] = part

        @pl.when(f > 0)
        def _():
            acc_ref[...] += part

        @pl.when((i == 0) & (f == 0))
        def _():
            dg_ref[...] = jnp.zeros_like(dg_ref)

        @pl.when(f == nf - 1)
        def _():
            xt = x2_ref[...]
            dx, dg = _rms_bwd(acc_ref[...], xt, _rms(xt), g_ref[...])
            dx2_ref[...] = dx3_ref[...] + dx
            dg_ref[...] += dg

    return pl.pallas_call(
        body, name="mlp_bwd", grid=(t // tm, nf),
        in_specs=[pl.BlockSpec((tm, D_MODEL), lambda i, f: (i, 0)),
                  pl.BlockSpec((tm, tf), lambda i, f: (i, f)),
                  pl.BlockSpec((tf // w_1.shape[2], D_MODEL, w_1.shape[2]), lambda i, f: (f, 0, 0)),
                  pl.BlockSpec((tf, D_MODEL), lambda i, f: (f, 0)),
                  pl.BlockSpec((tm, D_MODEL), lambda i, f: (i, 0)),
                  pl.BlockSpec((1, D_MODEL), lambda i, f: (0, 0))],
        out_specs=[pl.BlockSpec((tm, tf), lambda i, f: (i, f)),
                   pl.BlockSpec((tm, D_MODEL), lambda i, f: (i, 0)),
                   pl.BlockSpec((1, D_MODEL), lambda i, f: (0, 0))],
        out_shape=[jax.ShapeDtypeStruct((t, D_FF), BF16), jax.ShapeDtypeStruct((t, D_MODEL), F32),
                   jax.ShapeDtypeStruct((1, D_MODEL), F32)],
        scratch_shapes=[pltpu.VMEM((tm, D_MODEL), F32)],
        compiler_params=_params(("arbitrary", "arbitrary")),
    )(dx3, a, w_1, w_2, x2, g_mlp)


def _wgrad(name, a, b, tka, tn, tm, square=False, col_shards=False):
    t, ka = a.shape
    n = b.shape[1]
    nk = t // tm

    def body(a_ref, b_ref, o_ref, acc_ref):
        at = a_ref[...].astype(BF16)
        if square:
            at = at * at
        part = _dot_tn(at, b_ref[...].astype(BF16))
        k = pl.program_id(2)

        @pl.when(k == 0)
        def _():
            acc_ref[...] = part

        @pl.when(k > 0)
        def _():
            acc_ref[...] += part

        @pl.when(k == nk - 1)
        def _():
            if col_shards:
                for s in range(tn // sw):
                    o_ref[s] = acc_ref[:, s * sw:(s + 1) * sw].astype(BF16)
            else:
                o_ref[...] = acc_ref[...].astype(BF16)

    if col_shards:
        sw = n // N_DEV
        out_spec = pl.BlockSpec((tn // sw, tka, sw), lambda p, q, k: (q, p, 0))
        out_shape = jax.ShapeDtypeStruct((N_DEV, ka, sw), BF16)
    else:
        out_spec = pl.BlockSpec((tka, tn), lambda p, q, k: (p, q))
        out_shape = jax.ShapeDtypeStruct((ka, n), BF16)
    return pl.pallas_call(
        body, name=name, grid=(ka // tka, n // tn, nk),
        in_specs=[pl.BlockSpec((tm, tka), lambda p, q, k: (k, p)),
                  pl.BlockSpec((tm, tn), lambda p, q, k: (k, q))],
        out_specs=out_spec, out_shape=out_shape,
        scratch_shapes=[pltpu.VMEM((tka, tn), F32)],
        compiler_params=_params(("arbitrary", "arbitrary", "arbitrary")),
    )(a, b)


def _cross_bwd(dx2, x1, q, kv, w_cq, w_co, g_cross, tm, dep=None):
    t = x1.shape[0]
    m = kv.shape[0]

    def body(dx2_ref, x1_ref, q_ref, kv_ref, wq_ref, wo_ref, g_ref, dq_ref, dx1_ref, dkv_ref, dg_ref):
        @pl.when(pl.program_id(0) == 0)
        def _():
            dkv_ref[...] = jnp.zeros_like(dkv_ref)
            dg_ref[...] = jnp.zeros_like(dg_ref)

        do = _dot_nt(dx2_ref[...].astype(BF16), wo_ref[...]).astype(BF16)
        q = q_ref[...]
        dqs = []
        for h in range(X_HEADS):
            hs = slice(h * X_HEAD_DIM, (h + 1) * X_HEAD_DIM)
            vs = slice(D_MODEL + h * X_HEAD_DIM, D_MODEL + (h + 1) * X_HEAD_DIM)
            p = _cross_probs(q, kv_ref, h)
            dp = _dot_nt(do[:, hs], kv_ref[:, vs])
            ds = (p * (dp - jnp.sum(dp * p, axis=-1, keepdims=True))).astype(BF16)
            dqs.append(_dot(ds, kv_ref[:, hs]))
            dkv_ref[:, hs] += _dot_tn(ds, q[:, hs])
            dkv_ref[:, vs] += _dot_tn(p.astype(BF16), do[:, hs])
        dq = (jnp.concatenate(dqs, axis=1) * X_SCALE).astype(BF16)
        dq_ref[...] = dq
        xt = x1_ref[...]
        dx, dg = _rms_bwd(_dot_nt(dq, wq_ref[...]), xt, _rms(xt), g_ref[...])
        dx1_ref[...] = dx2_ref[...] + dx
        dg_ref[...] += dg

    row = lambda w: pl.BlockSpec((tm, w), lambda i: (i, 0))
    full = lambda a, b: pl.BlockSpec((a, b), lambda i: (0, 0))
    return pl.pallas_call(
        _with_dep(body, 7, dep), name="cross_bwd", grid=(t // tm,),
        in_specs=[row(D_MODEL), row(D_MODEL), row(D_MODEL), full(m, 2 * D_MODEL), full(D_MODEL, D_MODEL),
                  full(D_MODEL, D_MODEL), full(1, D_MODEL)] + _dep_spec(dep),
        out_specs=[row(D_MODEL), row(D_MODEL), full(m, 2 * D_MODEL), full(1, D_MODEL)],
        out_shape=[jax.ShapeDtypeStruct((t, D_MODEL), BF16), jax.ShapeDtypeStruct((t, D_MODEL), F32),
                   jax.ShapeDtypeStruct((m, 2 * D_MODEL), F32), jax.ShapeDtypeStruct((1, D_MODEL), F32)],
        compiler_params=_params(("arbitrary",)),
    )(dx2, x1, q, kv, w_cq, w_co, g_cross, *_dep_arg(dep))


def _memkv_bwd(dkv, mn, mem, w_ckv, g_mem):
    ws = w_ckv.shape[2]

    def body(dkv_ref, mn_ref, mem_ref, w_ref, g_ref, dw_ref, dg_ref):
        mn = mn_ref[...]
        dmn = jnp.zeros(mn.shape, F32)
        for j in range(N_DEV):
            dkvb = dkv_ref[:, j * ws:(j + 1) * ws].astype(BF16)
            dw_ref[j] = _dot_tn(mn, dkvb).astype(BF16)
            dmn = dmn + _dot_nt(dkvb, w_ref[j])
        xt = mem_ref[...]
        dg_ref[...] = jnp.sum(dmn * xt * _rms(xt), axis=0, keepdims=True)

    return pl.pallas_call(
        body, name="memkv_bwd",
        out_shape=[jax.ShapeDtypeStruct(w_ckv.shape, BF16), jax.ShapeDtypeStruct((1, D_MODEL), F32)],
        compiler_params=_params(),
    )(dkv, mn, mem, w_ckv, g_mem)


def _merge_bwd(dx1, ya, yb, gts, w_out, w_g, tm):
    t = dx1.shape[0]

    def body(dx1_ref, ya_ref, yb_ref, g_ref, wo_ref, wg_ref, dg_ref, dhp_ref, dya_ref, dyb_ref, db_ref):
        @pl.when(pl.program_id(0) == 0)
        def _():
            db_ref[...] = jnp.zeros_like(db_ref)

        dm = _dot_nt(dx1_ref[...].astype(BF16), wo_ref[...])
        ga = g_ref[:, :D_MODEL].astype(F32)
        gb = g_ref[:, D_MODEL:].astype(F32)
        dya_ref[...] = (dm * ga).astype(BF16)
        dyb_ref[...] = (dm * gb).astype(BF16)
        dpa = dm * ya_ref[...].astype(F32) * ga * (1.0 - ga)
        dpb = dm * yb_ref[...].astype(F32) * gb * (1.0 - gb)
        dpre = jnp.concatenate([dpa, dpb], axis=1)
        db_ref[...] += jnp.sum(dpre, axis=0, keepdims=True)
        dpreb = dpre.astype(BF16)
        dg_ref[...] = dpreb
        dhp_ref[...] = _dot_nt(dpreb, wg_ref[...])

    row = lambda w: pl.BlockSpec((tm, w), lambda i: (i, 0))
    once = lambda a, b: pl.BlockSpec((a, b), lambda i: (0, 0), pipeline_mode=pl.Buffered(1))
    sds = jax.ShapeDtypeStruct
    return pl.pallas_call(
        body, name="merge_bwd", grid=(t // tm,),
        in_specs=[row(D_MODEL), row(D_MODEL), row(D_MODEL), row(GATE_WIDTH),
                  once(D_MODEL, D_MODEL), once(D_MODEL, GATE_WIDTH)],
        out_specs=[row(GATE_WIDTH), row(D_MODEL), row(D_MODEL), row(D_MODEL),
                   pl.BlockSpec((1, GATE_WIDTH), lambda i: (0, 0))],
        out_shape=[sds((t, GATE_WIDTH), BF16), sds((t, D_MODEL), F32), sds((t, D_MODEL), BF16),
                   sds((t, D_MODEL), BF16), sds((1, GATE_WIDTH), F32)],
        compiler_params=_params(("arbitrary",)),
    )(dx1, ya, yb, gts, w_out, w_g)


def _combine_bwd(dya, dyb, oa, ob, l0, l1, l2, lb, sink_row, w_a, w_b, tm):
    t = dya.shape[0]

    def body(dya_ref, dyb_ref, oa_ref, ob_ref, l0_ref, l1_ref, l2_ref, lb_ref, sk_ref, wa_ref, wb_ref,
             do0_ref, do1_ref, do2_ref, c0_ref, c1_ref, c2_ref, dob_ref, cb_ref, dsk_ref, so_ref, sl_ref):
        @pl.when(pl.program_id(0) == 0)
        def _():
            dsk_ref[...] = jnp.zeros_like(dsk_ref)

        doa = _dot_nt(dya_ref[...], wa_ref[...])
        dob = _dot_nt(dyb_ref[...], wb_ref[...])
        dsum = _head_sums(doa * oa_ref[...].astype(F32), _head_gather(256, True))
        a0, a1, a2 = _alphas(l0_ref[...], _interleave(l1_ref, sl_ref), _interleave(l2_ref, sl_ref))
        c0_ref[...] = a0 * dsum
        spread = _dil_head_spread()
        do0_ref[...] = _head_scale(doa, a0, spread).astype(BF16)
        for al, do_ref, c_ref in ((a1, do1_ref, c1_ref), (a2, do2_ref, c2_ref)):
            _deinterleave(al * dsum, sl_ref, c_ref, F32)
            _deinterleave(_head_scale(doa, al, spread), so_ref, do_ref, BF16)
        dob_ref[...] = dob.astype(BF16)
        cb = _head_sums(dob * ob_ref[...], _head_gather(128, False))
        cb_ref[...] = cb
        lane = lax.broadcasted_iota(jnp.int32, cb.shape, 1)
        psink = jnp.where(lane < 8, jnp.exp(sk_ref[...] - lb_ref[...]), 0.0)
        dsk_ref[...] += jnp.sum(-psink * cb, axis=0, keepdims=True)

    row = lambda w: pl.BlockSpec((tm, w), lambda i: (i, 0))
    full = lambda a, b: pl.BlockSpec((a, b), lambda i: (0, 0))
    sds = jax.ShapeDtypeStruct
    d1, d2 = l1.shape[0], l2.shape[0]
    res = lambda d, w: pl.BlockSpec((d, tm // d, w), lambda i: (0, i, 0))
    return pl.pallas_call(
        body, name="combine_bwd", grid=(t // tm,),
        in_specs=[row(D_MODEL), row(D_MODEL), row(512), row(512),
                  row(256), _res_spec(l1, tm), _res_spec(l2, tm), row(128), full(1, 128),
                  full(512, D_MODEL), full(512, D_MODEL)],
        out_specs=[row(512), res(d1, 512), res(d2, 512), row(256), res(d1, 256), res(d2, 256),
                   row(512), row(128), full(1, 128)],
        out_shape=[sds((t, 512), BF16), sds((d1, t // d1, 512), BF16),
                   sds((d2, t // d2, 512), BF16), sds((t, 256), F32), sds((d1, t // d1, 256), F32),
                   sds((d2, t // d2, 256), F32), sds((t, 512), BF16),
                   sds((t, 128), F32), sds((1, 128), F32)],
        scratch_shapes=[_lane_scratch(tm, 512), _lane_scratch(tm, 256)],
        compiler_params=_params(("arbitrary",)),
    )(dya, dyb, oa, ob, l0, l1, l2, lb, sink_row, w_a, w_b)


def _attn_bwd(name, pv, dov, lsev, cv, cosv, sinv, swa, tq, dep=None):
    d, ls = pv.shape[0], pv.shape[1]
    n, nsb = ls // tq, tq // BAND
    pairs = _attn_layout(swa)
    ncol = 1 if swa else 2
    ow = 128 * len(pairs)

    kv_slots = sorted({(ko, vo) for _, ko, vo in pairs})

    def body(cur_ref, tail_ref, do_ref, lse_ref, c_ref, cos_ref, sin_ref, out_ref, acc_ref, carry_ref, acct_ref):
        i = pl.program_id(2)
        blk_i = n - 1 - i
        acc_ref[...] = jnp.zeros_like(acc_ref)
        acct_ref[...] = jnp.zeros_like(acct_ref)

        @pl.when(i == 0)
        def _():
            carry_ref[...] = jnp.zeros_like(carry_ref)

        qk_a, v_a = _head_a_masks(BAND)
        dim = lax.broadcasted_iota(jnp.int32, (128, BAND), 0)
        qk_at, v_at = (dim % HEAD_DIM) < HEAD_DIM // 2, dim < HEAD_DIM
        for s in range(nsb):
            mask = _band_mask(blk_i, s)
            mask2 = jnp.concatenate([mask, mask], axis=0)
            rows = slice(s * BAND, (s + 1) * BAND)
            kcols = slice(s * BAND, (s + 2) * BAND)
            for j, (qo, ko, vo) in enumerate(pairs):
                slot = kv_slots.index((ko, vo))
                kk = _kv_rows(cur_ref, tail_ref, s, ko)
                vv = _kv_rows(cur_ref, tail_ref, s, vo)
                q, do = cur_ref[rows, qo:qo + 128], do_ref[rows, j * 128:(j + 1) * 128]
                q2, do2 = _stack_heads(q, qk_a), _stack_heads(do, v_a)
                col2 = lambda ref: jnp.concatenate([ref[rows, 2 * j:2 * j + 1], ref[rows, 2 * j + 1:2 * j + 2]], axis=0)
                sc = _dot_nt(q2, kk)
                p = jnp.exp(jnp.where(mask2, sc, -jnp.inf) - col2(lse_ref))
                dp = _dot_nt(do2, vv)
                ds = (p * (dp - col2(c_ref))).astype(BF16)
                dq2 = _dot(ds, kk)
                acc_ref[BAND + s * BAND:BAND + (s + 1) * BAND, qo:qo + 128] += jnp.where(qk_a, dq2[:BAND], dq2[BAND:])
                acct_ref[2 * slot, :, kcols] += _dot(_stack_heads_t(q.T, qk_at), ds)
                acct_ref[2 * slot + 1, :, kcols] += _dot(_stack_heads_t(do.T, v_at), p.astype(BF16))
        for slot, (ko, vo) in enumerate(kv_slots):
            acc_ref[:, ko:ko + 128] += acct_ref[2 * slot].T
            acc_ref[:, vo:vo + 128] += acct_ref[2 * slot + 1].T

        last = acc_ref[tq:, :] + carry_ref[...]
        fin = last if tq == BAND else jnp.concatenate([acc_ref[BAND:tq, :], last], axis=0)
        out_ref[...] = _rope(fin, cos_ref[...], sin_ref[...], swa, -1).astype(BF16)
        carry_ref[...] = acc_ref[0:BAND, :]

    rev = lambda i: n - 1 - i
    blk = lambda rows, w, row_of: pl.BlockSpec((None, rows, w), lambda r, cb, i: (r, row_of(i), cb))
    tab = pl.BlockSpec((None, tq, 128), lambda r, cb, i: (r, rev(i), 0))
    return pl.pallas_call(
        _with_dep(body, 7, dep), name=name, grid=(d, ncol, n),
        in_specs=[blk(tq, PBLK, rev), blk(BAND, PBLK, lambda i: jnp.maximum(rev(i) * nsb - 1, 0)),
                  blk(tq, ow, rev), blk(tq, 128, rev), blk(tq, 128, rev), tab, tab] + _dep_spec(dep),
        out_specs=blk(tq, PBLK, rev),
        out_shape=jax.ShapeDtypeStruct((d, ls, ncol * PBLK), BF16),
        scratch_shapes=[pltpu.VMEM((tq + BAND, PBLK), F32), pltpu.VMEM((BAND, PBLK), F32),
                        pltpu.VMEM((2 * len(kv_slots), 128, tq + BAND), F32)],
        compiler_params=_params(("arbitrary", "arbitrary", "arbitrary")),
    )(pv, pv, dov, lsev, cv, cosv, sinv, *_dep_arg(dep))


def _dx(dp0, dp1, dp2, dpb, w_p, dh_part, dx1, x, g_mix, tm, dep=None):
    t = x.shape[0]
    gw = 2 * PBLK

    def body(dp0_ref, dp1_ref, dp2_ref, dpb_ref, w_ref, dhp_ref, dx1_ref, x_ref, g_ref, gx_ref, dg_ref,
             dpt_ref, scr_ref):
        @pl.when(pl.program_id(0) == 0)
        def _():
            dg_ref[...] = jnp.zeros_like(dg_ref)

        dpt_ref[:, 0:gw] = dp0_ref[...]
        dpt_ref[:, gw:2 * gw] = _interleave(dp1_ref, scr_ref).astype(BF16)
        dpt_ref[:, 2 * gw:3 * gw] = _interleave(dp2_ref, scr_ref).astype(BF16)
        dpt_ref[:, 3 * gw:] = dpb_ref[...]
        dh = _dot_nt(dpt_ref[...], w_ref[...]) + dhp_ref[...]
        xt = x_ref[...]
        dx, dg = _rms_bwd(dh, xt, _rms(xt), g_ref[...])
        gx_ref[...] = dx1_ref[...] + dx
        dg_ref[...] += dg

    row = lambda w: pl.BlockSpec((tm, w), lambda i: (i, 0))
    full = lambda a, b: pl.BlockSpec((a, b), lambda i: (0, 0))
    return pl.pallas_call(
        _with_dep(body, 9, dep), name="dx", grid=(t // tm,),
        in_specs=[row(gw), _res_spec(dp1, tm), _res_spec(dp2, tm), row(PBLK),
                  pl.BlockSpec((D_MODEL, P_WIDTH), lambda i: (0, 0), pipeline_mode=pl.Buffered(1)),
                  row(D_MODEL), row(D_MODEL), row(D_MODEL), full(1, D_MODEL)] + _dep_spec(dep),
        out_specs=[row(D_MODEL), full(1, D_MODEL)],
        out_shape=[jax.ShapeDtypeStruct((t, D_MODEL), F32), jax.ShapeDtypeStruct((1, D_MODEL), F32)],
        scratch_shapes=[pltpu.VMEM((tm, P_WIDTH), BF16), _lane_scratch(tm, gw)],
        compiler_params=_params(("arbitrary",)),
    )(dp0, dp1, dp2, dpb, w_p, dh_part, dx1, x, g_mix, *_dep_arg(dep))


MESH = pl.DeviceIdType.MESH
HBM_SPEC = pl.BlockSpec(memory_space=pltpu.HBM)
VMEM_SPEC = pl.BlockSpec(memory_space=pltpu.VMEM)


def _all_gather(xp, act, g, tm):
    t = act.shape[0]
    d1, d2 = DIL_GROUPS[1][1], DIL_GROUPS[2][1]

    def body(x_ref, act_ref, g_ref, out_ref, h_ref, h1_ref, h2_ref, send_sems, recv_sems, local_sem, hf_ref):
        x, y, c = lax.axis_index("x"), lax.axis_index("y"), lax.axis_index("c")
        me, sibling = (x, y, c), (x, y, 1 - c)
        chips = [(1 - x, y), (x, 1 - y), (1 - x, 1 - y)]

        def rows(px, py, pc):
            return out_ref.at[4 * px + 2 * py + pc]

        def copy(k, block, to, src=None):
            return pltpu.make_async_remote_copy(
                src_ref=rows(*block) if src is None else src, dst_ref=rows(*block),
                send_sem=send_sems.at[k], recv_sem=recv_sems.at[k], device_id=to, device_id_type=MESH)

        mine = pltpu.make_async_copy(x_ref, rows(*me), local_sem)
        mine.start()
        first = [copy(0, me, sibling, src=x_ref)]
        first += [copy(1 + j, me, (*chip, c), src=x_ref) for j, chip in enumerate(chips)]
        for cp in first:
            cp.start()

        def norm(a_blk, h_blk, h1_blk, h2_blk):
            xt = a_blk[...]
            hf = xt * _rms(xt) * g_ref[...]
            h_blk[...] = hf.astype(BF16)
            _deinterleave(hf, hf_ref, h1_blk, BF16)
            _deinterleave(hf, hf_ref, h2_blk, BF16)

        res = lambda d: pl.BlockSpec((d, tm // d, D_MODEL), lambda i: (0, i, 0))
        row = pl.BlockSpec((tm, D_MODEL), lambda i: (i, 0))
        pltpu.emit_pipeline(norm, grid=(t // tm,), in_specs=[row], out_specs=[row, res(d1), res(d2)])(
            act_ref, h_ref, h1_ref, h2_ref)

        passed = [copy(4 + j, (*chip, c), sibling) for j, chip in enumerate(chips)]
        for j, chip in enumerate(chips):
            copy(1 + j, (*chip, c), me).wait_recv()
            passed[j].start()
        copy(0, sibling, me).wait_recv()
        for j, chip in enumerate(chips):
            copy(4 + j, (*chip, 1 - c), me).wait_recv()
        for cp in first + passed:
            cp.wait_send()
        mine.wait()

    sds = jax.ShapeDtypeStruct
    return pl.pallas_call(
        body, name="all_gather",
        out_shape=[sds((N_DEV,) + xp.shape, xp.dtype), sds((t, D_MODEL), BF16),
                   sds((d1, t // d1, D_MODEL), BF16), sds((d2, t // d2, D_MODEL), BF16)],
        in_specs=[HBM_SPEC, HBM_SPEC, VMEM_SPEC], out_specs=[HBM_SPEC] * 4,
        scratch_shapes=[pltpu.SemaphoreType.DMA((7,)), pltpu.SemaphoreType.DMA((7,)), pltpu.SemaphoreType.DMA,
                        _lane_scratch(tm, D_MODEL)],
        compiler_params=pltpu.CompilerParams(vmem_limit_bytes=VMEM_LIMIT),
    )(xp, act, g)


def _peers():
    x, y, c = lax.axis_index("x"), lax.axis_index("y"), lax.axis_index("c")
    out = []
    for k in range(1, N_DEV):
        px = 1 - x if k & 4 else x
        py = 1 - y if k & 2 else y
        pc = 1 - c if k & 1 else c
        out.append((k, (px, py, pc), 4 * px + 2 * py + pc))
    return out


def _my_index():
    return 4 * lax.axis_index("x") + 2 * lax.axis_index("y") + lax.axis_index("c")


SEM_SPEC = pl.BlockSpec(memory_space=pltpu.SEMAPHORE)
ANY_SPEC = pl.BlockSpec(memory_space=pl.ANY)
_SPLIT_PARAMS = pltpu.CompilerParams(has_side_effects=pltpu.SideEffectType.DATAFLOW_SIDE_EFFECTING)


def _split_copies(gather, src_refs, land_refs, send_sems, recv_sems):
    me_idx = _my_index()
    out = []
    for a, (src_ref, land_ref) in enumerate(zip(src_refs, land_refs)):
        for k, peer, peer_idx in _peers():
            if gather:
                src, dst = src_ref, land_ref.at[me_idx]
            else:
                src, dst = src_ref.at[peer_idx], land_ref.at[k - 1]
            out.append(pltpu.make_async_remote_copy(
                src_ref=src, dst_ref=dst, send_sem=send_sems.at[7 * a + k - 1], recv_sem=recv_sems.at[7 * a + k - 1],
                device_id=peer, device_id_type=MESH))
    return out


def _split_start(name, gather, srcs):
    n = len(srcs)

    def body(*refs):
        send_sems, recv_sems = refs[n], refs[n + 1]
        for cp in _split_copies(gather, refs[:n], refs[2 * n + 2:3 * n + 2], send_sems, recv_sems):
            cp.start()
        token = refs[-1]
        token[...] = jnp.zeros_like(token)

    lands = [pltpu.HBM((N_DEV,) + a.shape if gather else (N_DEV - 1,) + a.shape[1:], a.dtype) for a in srcs]
    return pl.pallas_call(
        body, name=name,
        out_shape=(pltpu.SemaphoreType.DMA((7 * n,)), pltpu.SemaphoreType.DMA((7 * n,)),
                   *[pltpu.HBM(a.shape, a.dtype) for a in srcs], *lands, jax.ShapeDtypeStruct((8, 128), F32)),
        in_specs=(HBM_SPEC,) * n, out_specs=(SEM_SPEC, SEM_SPEC) + (HBM_SPEC,) * (2 * n) + (VMEM_SPEC,),
        input_output_aliases={i: 2 + i for i in range(n)}, compiler_params=_SPLIT_PARAMS,
    )(*[pltpu.with_memory_space_constraint(a, pltpu.HBM) for a in srcs])


def _split_wait(name, gather, started, after):
    send_sems, recv_sems, bufs = started[0], started[1], started[2:-1]
    n = len(bufs) // 2

    def body(*refs):
        for cp in _split_copies(gather, refs[:n], refs[n:2 * n], refs[2 * n], refs[2 * n + 1]):
            cp.wait_send()
            cp.wait_recv()

    out = pl.pallas_call(
        body, name=name, out_shape=tuple(pltpu.HBM(a.shape, a.dtype) for a in bufs),
        in_specs=(HBM_SPEC,) * (2 * n) + (SEM_SPEC, SEM_SPEC, ANY_SPEC), out_specs=(HBM_SPEC,) * (2 * n),
        input_output_aliases={i: i for i in range(2 * n)}, compiler_params=_SPLIT_PARAMS,
    )(*bufs, send_sems, recv_sems, after)
    return out[:n], out[n:]


def _adam_update(g, w, m, v):
    nm = ADAM_B1 * m + (1.0 - ADAM_B1) * g
    nv = ADAM_B2 * v + (1.0 - ADAM_B2) * (g * g)
    m_hat = nm / (1.0 - ADAM_B1 ** ADAM_STEP)
    v_hat = nv / (1.0 - ADAM_B2 ** ADAM_STEP)
    return -ADAM_LR * (m_hat / (jnp.sqrt(v_hat) + ADAM_EPS) + ADAM_WD * w), nm, nv


def _adamw(name, me, sent, got, w, m, v, tr, tc=None):
    r, c = w.shape
    tc = c if tc is None else tc

    def body(me_ref, own_ref, got_ref, w_ref, m_ref, v_ref, g_ref, d_ref, nm_ref, nv_ref):
        g = own_ref[...].astype(F32)
        for k in range(N_DEV - 1):
            g = g + got_ref[k].astype(F32)
        g_ref[...] = g
        d_ref[...], nm_ref[...], nv_ref[...] = _adam_update(g, w_ref[...], m_ref[...], v_ref[...])

    blk = pl.BlockSpec((tr, tc), lambda i, j, me_ref: (i, j))
    return pl.pallas_call(
        body, name=name,
        grid_spec=pltpu.PrefetchScalarGridSpec(
            num_scalar_prefetch=1, grid=(r // tr, c // tc),
            in_specs=[pl.BlockSpec((None, tr, tc), lambda i, j, me_ref: (me_ref[0], i, j)),
                      pl.BlockSpec((N_DEV - 1, tr, tc), lambda i, j, me_ref: (0, i, j)), blk, blk, blk],
            out_specs=[blk] * 4),
        out_shape=[jax.ShapeDtypeStruct((r, c), F32)] * 4,
        compiler_params=_params(("arbitrary", "arbitrary")),
    )(me, sent, got, w, m, v)


def _adamw_small(srecv, ws, ms, vs):
    nv_ = len(ws)

    def body(*refs):
        s_ref = refs[0]
        ins, outs = refs[1:1 + 3 * nv_], refs[1 + 3 * nv_:]
        g_all = s_ref[0]
        for k in range(1, N_DEV):
            g_all = g_all + s_ref[k]
        for i in range(nv_):
            n = ins[i].shape[1]
            g = g_all[i:i + 1, :n]
            d, nm, nv = _adam_update(g, ins[i][...], ins[nv_ + i][...], ins[2 * nv_ + i][...])
            outs[i][...], outs[nv_ + i][...], outs[2 * nv_ + i][...], outs[3 * nv_ + i][...] = g, d, nm, nv
        outs[-1][...] = g_all[nv_:nv_ + 1, :128]

    shapes = [jax.ShapeDtypeStruct(a.shape, F32) for a in ws]
    res = pl.pallas_call(body, name="adamw_small", out_shape=shapes * 4 + [jax.ShapeDtypeStruct((1, 128), F32)],
                         compiler_params=_params())(srecv, *ws, *ms, *vs)
    return [res[k * nv_:(k + 1) * nv_] for k in range(4)], res[-1]


def _cols_from_shards(a):
    return jnp.swapaxes(a, 0, 1).reshape(a.shape[1], a.shape[0] * a.shape[2])


def _shards_from_cols(a):
    return jnp.swapaxes(a.reshape(a.shape[0], N_DEV, a.shape[1] // N_DEV), 0, 1)


def _shards_from_rows(a):
    return a.reshape(N_DEV, a.shape[0] // N_DEV, a.shape[1])


def _pair_lanes(a):
    lead = a.shape[:-1]
    return a.reshape(lead + (2, 2, HEAD_DIM // 2)).swapaxes(-3, -2).reshape(lead + (128,))


def _split_w_in(w_in):
    rows = w_in.shape[0]
    dil = w_in[:, :3 * DIL_WIDTH].reshape(rows, 3, 3, 4, 128)
    dil = jnp.concatenate([_pair_lanes(dil[:, :2]), dil[:, 2:]], axis=1)
    dil = dil.transpose(0, 2, 3, 1, 4).reshape(rows, 3 * DIL_WIDTH)
    o = 3 * DIL_WIDTH
    qb = w_in[:, o:o + SWA_Q_WIDTH].reshape(rows, 2, 4, HEAD_DIM).transpose(0, 2, 1, 3).reshape(rows, 4, 128)
    qb = _pair_lanes(qb).reshape(rows, SWA_Q_WIDTH)
    kb = _pair_lanes(w_in[:, o + SWA_Q_WIDTH:o + SWA_Q_WIDTH + SWA_KV_WIDTH])
    vb = w_in[:, o + SWA_Q_WIDTH + SWA_KV_WIDTH:P_WIDTH]
    return jnp.concatenate([dil, qb, kb, vb], axis=1)


def _merge_w_in(dw_p, dw_g):
    rows = dw_p.shape[0]
    dil = dw_p[:, :3 * DIL_WIDTH].reshape(rows, 3, 4, 3, 128).transpose(0, 3, 1, 2, 4)
    dil = jnp.concatenate([_pair_lanes(dil[:, :2]), dil[:, 2:]], axis=1).reshape(rows, 3 * DIL_WIDTH)
    o = 3 * DIL_WIDTH
    qb = _pair_lanes(dw_p[:, o:o + SWA_Q_WIDTH].reshape(rows, 4, 128))
    qb = qb.reshape(rows, 4, 2, HEAD_DIM).transpose(0, 2, 1, 3).reshape(rows, SWA_Q_WIDTH)
    kb = _pair_lanes(dw_p[:, o + SWA_Q_WIDTH:o + SWA_Q_WIDTH + SWA_KV_WIDTH])
    vb = dw_p[:, o + SWA_Q_WIDTH + SWA_KV_WIDTH:]
    return jnp.concatenate([dil, qb, kb, vb, dw_g], axis=1)


def _swa_rows(w_b):
    return w_b.reshape(2, 4, HEAD_DIM, -1).transpose(1, 0, 2, 3).reshape(SWA_Q_WIDTH, -1)


def _swa_rows_inv(dw_b):
    return dw_b.reshape(4, 2, HEAD_DIM, -1).transpose(1, 0, 2, 3).reshape(SWA_Q_WIDTH, -1)


def _rope_tables(pos):
    half = HEAD_DIM // 2
    inv = ROPE_THETA ** (-jnp.arange(half, dtype=F32) / half)
    ang = pos.astype(F32)[:, None] * jnp.tile(inv, 4)
    sign = jnp.repeat(jnp.array([-1.0, 1.0], F32), 2 * half)
    return jnp.cos(ang), jnp.sin(ang) * sign


def _local_step(x, hs, mem, pos, target, w_in, dep, rest_weights, on_grads, g_mix, g_cross, g_mem, g_mlp, g_final, sink):
    t = x.shape[0]
    tm = min(512, t)
    tq = 1024
    tw = min(2048, t)
    w_p = _split_w_in(w_in)
    cos, sin = lax.optimization_barrier(_rope_tables(pos))
    sink_row = jnp.pad(sink.reshape(2, 4).T.reshape(1, 8), ((0, 0), (0, 120)))
    tabs = [(cos[None], sin[None])]
    for _, d in DIL_GROUPS[1:]:
        tabs.append(tuple(a.reshape(t // d, d, 128).swapaxes(0, 1) for a in (cos, sin)))
    tabs.append(tabs[0])

    h, h1, h2 = hs
    p0, p1, p2, pb = _inproj(h, h1, h2, w_p, [(cos, sin), tabs[1], tabs[2]], tm, dep)
    ps = [p0[None], p1, p2, pb[None]]
    outs, lses = [], []
    for gi, pv in enumerate(ps):
        res = _attn_fwd(f"attn_fwd{gi}", pv, gi == 3, sink_row[0, :8], min(tq, pv.shape[1]))
        outs.append(res[0])
        lses.append(res[1])
    o0, l0, ob, lb, ob32 = outs[0][0], lses[0][0], outs[3][0], lses[3][0], res[2][0]
    wts = rest_weights(lb)
    w_b = _swa_rows(wts["w_branch_b"])
    tf = 2048
    w_g = wts["w_g"]
    gts = _gates(h, w_g, wts["b_gate"].reshape(1, GATE_WIDTH), min(1024, t), 1024)
    oa, ya, yb, merged, x1, hc = _mix(o0, outs[1], outs[2], l0, lses[1], lses[2], ob, gts, x,
                                      wts["w_branch_a"], w_b, wts["w_out"], g_cross, tm)
    mn, kv = _memkv(mem, g_mem, wts["w_ckv"])
    q, o, x2, hm = _cross(hc, x1, kv, wts["w_cq"], wts["w_co"], g_mlp, tm)
    a, dx3, loss, dg_final = _mlp(hm, x2, wts["w_1"], wts["w_2"], g_final.reshape(1, D_MODEL), target, tm, tf)

    grads = {}
    dz, dx2, dg_mlp = _mlp_bwd(dx3, a, wts["w_1"], wts["w_2"], x2, g_mlp, tm, tf)
    grads["w_2"] = _shards_from_rows(_wgrad("dw_2", a, dx3, 1024, 1024, tw, square=True))
    grads["w_1"] = _wgrad("dw_1", hm, dz, 1024, 1024, tw, col_shards=True)
    dep = on_grads(GROUP_A, grads)
    dq, dx1, dkv, dg_cross = _cross_bwd(dx2, x1, q, kv, wts["w_cq"], wts["w_co"], g_cross, tm, dep)
    grads["w_co"] = _shards_from_rows(_wgrad("dw_co", o, dx2, 1024, 1024, tw))
    grads["w_cq"] = _shards_from_rows(_wgrad("dw_cq", hc, dq, 1024, 1024, tw))
    grads["w_ckv"], dg_mem = _memkv_bwd(dkv, mn, mem, wts["w_ckv"], g_mem)
    dgt, dh_part, dya, dyb, db_gate = _merge_bwd(dx1, ya, yb, gts, wts["w_out"], w_g, tm)
    do0, do1, do2, c0, c1, c2, dob, cb, dsink = _combine_bwd(
        dya, dyb, oa, ob32, l0, lses[1], lses[2], lb, sink_row, wts["w_branch_a"], w_b, tm)
    grads["w_out"] = _shards_from_rows(_wgrad("dw_out", merged, dx1, 1024, 1024, tw))
    grads["w_branch_a"] = _shards_from_cols(_wgrad("dw_a", oa, dya, 512, 1024, tw))
    grads["w_branch_b"] = _shards_from_cols(_swa_rows_inv(_wgrad("dw_b", ob, dyb, 512, 1024, tw)))
    grads["b_gate"] = _shards_from_cols(db_gate.reshape(2, D_MODEL)).astype(BF16)
    dep = on_grads(GROUP_B, grads)
    dw_g = _wgrad("dw_g", h, dgt, 1024, 1024, tw)
    dps = []
    for gi, (pv, do_g, c_g) in enumerate(zip(ps, (do0[None], do1, do2, dob[None]), (c0[None], c1, c2, cb[None]))):
        dps.append(_attn_bwd(f"attn_bwd{gi}", pv, do_g, lses[gi], c_g, tabs[gi][0], tabs[gi][1], gi == 3,
                             min(tq, pv.shape[1]),
                             dep if gi == 0 else None))
    dw_p = jnp.concatenate(
        [_wgrad(f"dw_p{gi}", hh.reshape(t, D_MODEL), dpg.reshape(t, -1), 1024, PBLK, tw)
         for gi, (hh, dpg) in enumerate(zip((h, h1, h2, h), dps))], axis=1)
    dw_in = _merge_w_in(dw_p, dw_g)
    grads["w_in"] = jnp.swapaxes(dw_in, 0, 1).reshape(N_DEV, dw_in.shape[1] // N_DEV, dw_in.shape[0])
    dep = on_grads(GROUP_C, grads)
    grad_x, dg_mix = _dx(dps[0][0], dps[1], dps[2], dps[3][0], w_p, dh_part, dx1, x, g_mix, tm, dep)
    dsink_heads = dsink[0, :8].reshape(4, 2).T.reshape(8)
    small = {"g_mix": dg_mix[0], "g_cross": dg_cross[0], "g_mem": dg_mem[0], "g_mlp": dg_mlp[0],
             "g_final": dg_final[0], "sink": dsink_heads}
    return loss[0, 0], grad_x, small


def kernel(x, mem, positions, g_mix, w_in, b_gate, sink, w_branch_a, w_branch_b, w_out, g_cross, g_mem, w_cq, w_ckv, w_co, g_mlp, w_1, w_2, g_final, loss_target, m_g_mix, m_w_in, m_b_gate, m_sink, m_w_branch_a, m_w_branch_b, m_w_out, m_g_cross, m_g_mem, m_w_cq, m_w_ckv, m_w_co, m_g_mlp, m_w_1, m_w_2, m_g_final, v_g_mix, v_w_in, v_b_gate, v_sink, v_w_branch_a, v_w_branch_b, v_w_out, v_g_cross, v_g_mem, v_w_cq, v_w_ckv, v_w_co, v_g_mlp, v_w_1, v_w_2, v_g_final):
    local = dict(locals())
    shard = {n: local[n][0] for n in GROUP_A + GROUP_B + GROUP_C}
    me = _my_index()
    me_arr = me.reshape(1).astype(jnp.int32)
    tags = {GROUP_A: "a", GROUP_B: "b", GROUP_C: "c"}

    transposed = lambda a: jnp.swapaxes(a, 0, 1)
    gathered_w_in, *hs = _all_gather(transposed(shard["w_in"]).astype(BF16), x[0], g_mix, min(512, x.shape[1]))
    w_in_full = transposed(gathered_w_in.reshape(-1, gathered_w_in.shape[2]))
    rest = GROUP_A + GROUP_B

    def gathered(name, started, after):
        srcs, lands = _split_wait(name, True, started, after)
        return [lax.dynamic_update_slice(land, src[None], (me,) + (0,) * src.ndim) for src, land in zip(srcs, lands)]

    gather = _split_start("gather_start", True,
                          [shard[n] if n == "b_gate" else shard[n].astype(BF16) for n in rest])

    def rest_weights(after):
        full = {"w_g": w_in_full[:, P_WIDTH:]}
        for name, a in zip(rest, gathered("gather_wait", gather, after)):
            if name in ("w_1", "w_ckv"):
                full[name] = a
            elif name in _COL_SHARDED:
                full[name] = _cols_from_shards(a)
            else:
                full[name] = a.reshape(N_DEV * a.shape[1], a.shape[2])
        return full

    scatters = {}

    def on_grads(names, grads):
        scatters[names] = _split_start("scatter_start_" + tags[names], False, [grads[n] for n in names])
        return scatters[names][-1]

    loss, grad_x, small = _local_step(
        x[0], hs, mem[0], positions[0], loss_target[0], w_in_full, gather[-1], rest_weights, on_grads,
        g_mix, g_cross, g_mem, g_mlp, g_final, sink[0])

    sp = jnp.stack([small[n] if n != "sink" else jnp.pad(small[n], (0, LANES - 8)) for n in SMALL]
                   + [jnp.pad(loss.reshape(1), (0, LANES - 1)), jnp.zeros((LANES,), F32)])
    small_gather = _split_start("small_start", True, [sp])

    after, updated = small_gather[-1], {}
    for names in (GROUP_A, GROUP_B, GROUP_C):
        sent, got = _split_wait("scatter_wait_" + tags[names], False, scatters[names], after)
        for i, name in enumerate(names):
            view = transposed if name == "w_in" else (lambda a: a)
            outs = _adamw("adamw_" + name, me_arr, sent[i], got[i], view(shard[name]),
                          view(local["m_" + name][0]), view(local["v_" + name][0]), ADAM_ROWS[name], ADAM_COLS.get(name))
            updated[name] = [view(a)[None] for a in outs]
            after = outs[3]

    flat = lambda prefix: [local[prefix + n].reshape(1, -1) for n in SMALL]
    outs, loss_row = _adamw_small(gathered("small_wait", small_gather, after)[0], flat(""), flat("m_"), flat("v_"))
    for i, name in enumerate(SMALL):
        updated[name] = [outs[which][i].reshape(local[name].shape) for which in range(4)]

    order = ["g_mix", "w_in", "b_gate", "sink", "w_branch_a", "w_branch_b", "w_out", "g_cross", "g_mem", "w_cq",
             "w_ckv", "w_co", "g_mlp", "w_1", "w_2", "g_final"]
    res = [loss_row[0, 0], grad_x[None]]
    for which in range(4):
        res += [updated[n][which] for n in order]
    return tuple(res)
```

```python
import functools
import math

import jax
import jax.numpy as jnp
import numpy as np
from jax import lax
from jax.experimental import pallas as pl
from jax.experimental.pallas import tpu as pltpu

F32 = jnp.float32
BF16 = jnp.bfloat16

D_MODEL = 1024
HEAD_DIM = 64
DIL_GROUPS = ((128, 1), (512, 4), (2048, 16))
ROPE_THETA = 10000.0
X_HEADS = 4
X_HEAD_DIM = D_MODEL // X_HEADS
D_FF = 4 * D_MODEL
EPS = 1e-6
DIL_WIDTH = 1536
SWA_Q_WIDTH = 512
SWA_KV_WIDTH = 128
P_WIDTH = 3 * DIL_WIDTH + SWA_Q_WIDTH + 2 * SWA_KV_WIDTH
GATE_WIDTH = 2 * D_MODEL
IN_WIDTH = P_WIDTH + GATE_WIDTH
BAND = 128
PBLK = 768
Q_SCALE = HEAD_DIM ** -0.5
X_SCALE = X_HEAD_DIM ** -0.5

ADAM_LR = 0.001
ADAM_B1 = 0.9
ADAM_B2 = 0.999
ADAM_EPS = 1e-08
ADAM_WD = 0.01
ADAM_STEP = 10

N_DEV = 8
LANES = 1024
VMEM_LIMIT = 52 * 1024 * 1024

NT = (((1,), (1,)), ((), ()))
TN = (((0,), (0,)), ((), ()))

GROUP_A = ("w_1", "w_2")
GROUP_B = ("w_branch_a", "w_branch_b", "w_out", "w_cq", "w_ckv", "w_co", "b_gate")
GROUP_C = ("w_in",)
_COL_SHARDED = ("w_in", "w_branch_a", "w_branch_b", "w_ckv", "w_1", "b_gate")
ADAM_ROWS = {"w_in": 464, "w_branch_a": 512, "w_branch_b": 512, "w_out": 128, "w_cq": 128, "w_ckv": 512,
             "w_co": 128, "w_1": 256, "w_2": 256, "b_gate": 2}
ADAM_COLS = {"w_in": 256}
SMALL = ("g_mix", "g_cross", "g_mem", "g_mlp", "g_final", "sink")


def _params(sem=None):
    return pltpu.CompilerParams(dimension_semantics=sem, vmem_limit_bytes=VMEM_LIMIT)


def _dot(a, b):
    return jnp.dot(a, b, preferred_element_type=F32)


def _dot_nt(a, b):
    return lax.dot_general(a, b, NT, preferred_element_type=F32)


def _dot_tn(a, b):
    return lax.dot_general(a, b, TN, preferred_element_type=F32)


def _rms(xt):
    return lax.rsqrt(jnp.mean(xt * xt, axis=-1, keepdims=True) + EPS)


def _rms_bwd(dh, xt, r, g):
    xn = xt * r
    dxn = dh * g
    dx = r * (dxn - xn * jnp.mean(dxn * xn, axis=-1, keepdims=True))
    return dx, jnp.sum(dh * xn, axis=0, keepdims=True)


def _rope(x, c, s, swa, sign):
    kinds = "qqqqkv" if swa else "qkvqkv"
    cq, sq = c * Q_SCALE, s * (sign * Q_SCALE)
    sk = s * sign if sign != 1 else s
    out = []
    for ci, kind in enumerate(kinds):
        xc = x[:, ci * 128:(ci + 1) * 128]
        if kind == "v":
            out.append(xc)
        elif kind == "q":
            out.append(xc * cq + pltpu.roll(xc, 64, 1) * sq)
        else:
            out.append(xc * c + pltpu.roll(xc, 64, 1) * sk)
    return jnp.concatenate(out, axis=1)


def _lane_scratch(rows, w):
    return pltpu.VMEM((w // 128, rows, 128), F32)


def _deinterleave(val, scr_ref, dst_ref, dtype):
    d, n = dst_ref.shape[0], dst_ref.shape[1]
    nc = val.shape[1] // 128
    for c in range(nc):
        scr_ref[c] = val[:, c * 128:(c + 1) * 128]
    for r in range(d):
        rows = [scr_ref.at[c][pl.ds(r, n, stride=d), :] for c in range(nc)]
        dst_ref[r] = jnp.concatenate(rows, axis=1).astype(dtype)


def _res_spec(a, tm):
    d, w = a.shape[0], a.shape[2]
    return pl.BlockSpec((d, tm // d, w), lambda i: (0, i, 0))


def _interleave(src_ref, scr_ref):
    d, n = src_ref.shape[0], src_ref.shape[1]
    nc = src_ref.shape[2] // 128
    for r in range(d):
        v = src_ref[r].astype(F32)
        for c in range(nc):
            scr_ref.at[c][pl.ds(r, n, stride=d), :] = v[:, c * 128:(c + 1) * 128]
    return jnp.concatenate([scr_ref[c] for c in range(nc)], axis=1)


def _with_dep(body, n_in, dep):
    if dep is None:
        return body
    return lambda *refs: body(*refs[:n_in], *refs[n_in + 1:])


def _dep_spec(dep):
    return [] if dep is None else [pl.BlockSpec(memory_space=pl.ANY)]


def _dep_arg(dep):
    return [] if dep is None else [dep]


def _inproj(h, h1, h2, w_p, tabs, tm, dep=None):
    t = h.shape[0]
    gw = 2 * PBLK
    (cos, sin), (cos1, sin1), (cos2, sin2) = tabs[0], tabs[1], tabs[2]

    def body(h_ref, h1_ref, h2_ref, w_ref, c_ref, s_ref, c1_ref, s1_ref, c2_ref, s2_ref,
             p0_ref, p1_ref, p2_ref, pb_ref):
        rows = lambda ref: ref[...].reshape(tm, ref.shape[-1])
        groups = ((h_ref, c_ref, s_ref, p0_ref), (h1_ref, c1_ref, s1_ref, p1_ref), (h2_ref, c2_ref, s2_ref, p2_ref))
        for gi, (lhs_ref, cc_ref, ss_ref, out_ref) in enumerate(groups):
            lhs, cc, ss = rows(lhs_ref), rows(cc_ref), rows(ss_ref)
            for half in range(2):
                col = gi * gw + half * PBLK
                val = _rope(_dot(lhs, w_ref[:, col:col + PBLK]), cc, ss, False, 1).astype(BF16)
                if out_ref.ndim == 3:
                    out_ref[:, :, half * PBLK:(half + 1) * PBLK] = val.reshape(out_ref.shape[:2] + (PBLK,))
                else:
                    out_ref[:, half * PBLK:(half + 1) * PBLK] = val
        pb_ref[...] = _rope(_dot(h_ref[...], w_ref[:, 3 * gw:]), c_ref[...], s_ref[...], True, 1).astype(BF16)

    d1, d2 = DIL_GROUPS[1][1], DIL_GROUPS[2][1]
    row = lambda w: pl.BlockSpec((tm, w), lambda i: (i, 0))
    res = lambda d, w: pl.BlockSpec((d, tm // d, w), lambda i: (0, i, 0))
    sds = jax.ShapeDtypeStruct
    return pl.pallas_call(
        _with_dep(body, 10, dep), name="inproj", grid=(t // tm,),
        in_specs=[row(D_MODEL), res(d1, D_MODEL), res(d2, D_MODEL),
                  pl.BlockSpec((D_MODEL, P_WIDTH), lambda i: (0, 0), pipeline_mode=pl.Buffered(1)),
                  row(128), row(128), res(d1, 128), res(d1, 128), res(d2, 128), res(d2, 128)] + _dep_spec(dep),
        out_specs=[row(gw), res(d1, gw), res(d2, gw), row(PBLK)],
        out_shape=[sds((t, gw), BF16), sds((d1, t // d1, gw), BF16), sds((d2, t // d2, gw), BF16),
                   sds((t, PBLK), BF16)],
        compiler_params=_params(("arbitrary",)),
    )(h, h1, h2, w_p, cos, sin, cos1, sin1, cos2, sin2, *_dep_arg(dep))


def _gates(h, w_g, b, tm, tn):
    t = h.shape[0]

    def body(h_ref, w_ref, b_ref, o_ref):
        z = _dot(h_ref[...], w_ref[...]) + b_ref[...]
        o_ref[...] = (0.5 * jnp.tanh(0.5 * z) + 0.5).astype(BF16)

    return pl.pallas_call(
        body, name="gates", grid=(t // tm, GATE_WIDTH // tn),
        in_specs=[pl.BlockSpec((tm, D_MODEL), lambda i, j: (i, 0)),
                  pl.BlockSpec((D_MODEL, tn), lambda i, j: (0, j)),
                  pl.BlockSpec((1, tn), lambda i, j: (0, j))],
        out_specs=pl.BlockSpec((tm, tn), lambda i, j: (i, j)),
        out_shape=jax.ShapeDtypeStruct((t, GATE_WIDTH), BF16),
        compiler_params=_params(("arbitrary", "arbitrary")),
    )(h, w_g, b)


def _band_mask(i, s):
    row = lax.broadcasted_iota(jnp.int32, (BAND, 2 * BAND), 0)
    col = lax.broadcasted_iota(jnp.int32, (BAND, 2 * BAND), 1)
    band = (col >= row) & (col <= row + BAND)
    if s == 0:
        band = band & ((col >= BAND) | (i > 0))
    return band


def _head_a_masks(rows):
    lane = lax.broadcasted_iota(jnp.int32, (rows, 128), 1)
    return (lane % HEAD_DIM) < HEAD_DIM // 2, lane < HEAD_DIM


def _stack_heads(x, head_a):
    zero = jnp.zeros_like(x)
    return jnp.concatenate([jnp.where(head_a, x, zero), jnp.where(head_a, zero, x)], axis=0)


def _stack_heads_t(xt, head_a_t):
    zero = jnp.zeros_like(xt)
    return jnp.concatenate([jnp.where(head_a_t, xt, zero), jnp.where(head_a_t, zero, xt)], axis=1)


def _kv_rows(cur_ref, tail_ref, s, off):
    if s == 0:
        return jnp.concatenate([tail_ref[:, off:off + 128], cur_ref[0:BAND, off:off + 128]], axis=0)
    return cur_ref[(s - 1) * BAND:(s + 1) * BAND, off:off + 128]


def _attn_layout(swa):
    if swa:
        return [(128 * j, 512, 640) for j in range(4)]
    return [(0, 128, 256), (384, 512, 640)]


def _attn_fwd(name, pv, swa, sinks, tq):
    d, ls = pv.shape[0], pv.shape[1]
    n, nsb = ls // tq, tq // BAND
    pairs = _attn_layout(swa)
    ncol = 1 if swa else 2
    ow = 128 * len(pairs)

    def body(cur_ref, tail_ref, *rest):
        sink_ref, o_ref, lse_ref, o32_ref = rest if swa else (None,) + rest + (None,)
        i = pl.program_id(2)
        lane = lax.broadcasted_iota(jnp.int32, (BAND, 128), 1)
        qk_a, v_a = _head_a_masks(BAND)
        first = lax.broadcasted_iota(jnp.int32, (2 * BAND, 1), 0) < BAND
        for s in range(nsb):
            mask = _band_mask(i, s)
            mask2 = jnp.concatenate([mask, mask], axis=0)
            rows = slice(s * BAND, (s + 1) * BAND)
            lse_tile = jnp.zeros((BAND, 128), F32)
            for j, (qo, ko, vo) in enumerate(pairs):
                q = cur_ref[rows, qo:qo + 128]
                kk = _kv_rows(cur_ref, tail_ref, s, ko)
                vv = _kv_rows(cur_ref, tail_ref, s, vo)
                sc = _dot_nt(_stack_heads(q, qk_a), kk)
                sc = jnp.where(mask2, sc, -jnp.inf)
                m = jnp.max(sc, axis=-1, keepdims=True)
                if swa:
                    sk = jnp.where(first, sink_ref[2 * j], sink_ref[2 * j + 1])
                    m = jnp.maximum(m, sk)
                p = jnp.exp(sc - m)
                den = jnp.sum(p, axis=-1, keepdims=True)
                if swa:
                    den = den + jnp.exp(sk - m)
                lse = m + jnp.log(den)
                lse_tile = jnp.where(lane == 2 * j, lse[:BAND], jnp.where(lane == 2 * j + 1, lse[BAND:], lse_tile))
                o2 = _dot(p.astype(BF16), vv) * (1.0 / den)
                o = jnp.where(v_a, o2[:BAND], o2[BAND:])
                o_ref[rows, j * 128:(j + 1) * 128] = o.astype(BF16)
                if swa:
                    o32_ref[rows, j * 128:(j + 1) * 128] = o
            lse_ref[rows, :] = lse_tile

    in_specs = [pl.BlockSpec((None, tq, PBLK), lambda r, cb, i: (r, i, cb)),
                pl.BlockSpec((None, BAND, PBLK), lambda r, cb, i: (r, jnp.maximum(i * nsb - 1, 0), cb))]
    args = [pv, pv]
    out_specs = [pl.BlockSpec((None, tq, ow), lambda r, cb, i: (r, i, cb)),
                 pl.BlockSpec((None, tq, 128), lambda r, cb, i: (r, i, cb))]
    out_shape = [jax.ShapeDtypeStruct((d, ls, 512), BF16), jax.ShapeDtypeStruct((d, ls, 128 * ncol), F32)]
    if swa:
        in_specs.append(pl.BlockSpec(memory_space=pltpu.SMEM))
        args.append(sinks)
        out_specs.append(out_specs[0])
        out_shape.append(jax.ShapeDtypeStruct((d, ls, 512), F32))
    return pl.pallas_call(
        body, name=name, grid=(d, ncol, n),
        in_specs=in_specs, out_specs=out_specs, out_shape=out_shape,
        compiler_params=_params(("arbitrary", "arbitrary", "arbitrary")),
    )(*args)


def _lse_lane(head):
    return (head // 4) * 128 + head % 4


def _dil_head_spread():
    lane = lax.broadcasted_iota(jnp.int32, (256, 512), 0)
    head = lax.broadcasted_iota(jnp.int32, (256, 512), 1) // HEAD_DIM
    return (lane == _lse_lane(head)).astype(BF16)


def _head_scale(x, tile, spread):
    return x * _dot(tile.astype(BF16), spread)


def _head_gather(width, dil):
    head = lax.broadcasted_iota(jnp.int32, (8 * HEAD_DIM, width), 0) // HEAD_DIM
    lane = lax.broadcasted_iota(jnp.int32, (8 * HEAD_DIM, width), 1)
    return (lane == (_lse_lane(head) if dil else head)).astype(BF16)


def _head_sums(x, gather):
    hi = x.astype(BF16)
    lo = (x - hi.astype(F32)).astype(BF16)
    return _dot(hi, gather) + _dot(lo, gather)


def _alphas(l0, l1, l2):
    m = jnp.maximum(jnp.maximum(l0, l1), l2)
    e0, e1, e2 = jnp.exp(l0 - m), jnp.exp(l1 - m), jnp.exp(l2 - m)
    den = e0 + e1 + e2
    return e0 / den, e1 / den, e2 / den


def _mix(o0, o1, o2, l0, l1, l2, ob, gts, x, w_a, w_b, w_out, g_cross, tm):
    t = x.shape[0]

    def body(o0_ref, o1_ref, o2_ref, l0_ref, l1_ref, l2_ref, ob_ref, g_ref, x_ref, wa_ref, wb_ref, wo_ref,
             gc_ref, oa_ref, ya_ref, yb_ref, mg_ref, x1_ref, hc_ref, so_ref, sl_ref):
        a0, a1, a2 = _alphas(l0_ref[...], _interleave(l1_ref, sl_ref), _interleave(l2_ref, sl_ref))
        spread = _dil_head_spread()
        oa = (_head_scale(o0_ref[...].astype(F32), a0, spread)
              + _head_scale(_interleave(o1_ref, so_ref), a1, spread)
              + _head_scale(_interleave(o2_ref, so_ref), a2, spread))
        oab = oa.astype(BF16)
        oa_ref[...] = oab
        ya = _dot(oab, wa_ref[...])
        yb = _dot(ob_ref[...], wb_ref[...])
        ya_ref[...] = ya.astype(BF16)
        yb_ref[...] = yb.astype(BF16)
        merged = (g_ref[:, :D_MODEL].astype(F32) * ya + g_ref[:, D_MODEL:].astype(F32) * yb).astype(BF16)
        mg_ref[...] = merged
        x1 = x_ref[...] + _dot(merged, wo_ref[...])
        x1_ref[...] = x1
        hc_ref[...] = (x1 * _rms(x1) * gc_ref[...]).astype(BF16)

    row = lambda w: pl.BlockSpec((tm, w), lambda i: (i, 0))
    full = lambda a, b: pl.BlockSpec((a, b), lambda i: (0, 0))
    return pl.pallas_call(
        body, name="mix", grid=(t // tm,),
        in_specs=[row(512), _res_spec(o1, tm), _res_spec(o2, tm), row(256), _res_spec(l1, tm), _res_spec(l2, tm),
                  row(512), row(GATE_WIDTH),
                  row(D_MODEL), full(512, D_MODEL), full(512, D_MODEL), full(D_MODEL, D_MODEL), full(1, D_MODEL)],
        out_specs=[row(512), row(D_MODEL), row(D_MODEL), row(D_MODEL), row(D_MODEL), row(D_MODEL)],
        out_shape=[jax.ShapeDtypeStruct((t, 512), BF16), jax.ShapeDtypeStruct((t, D_MODEL), BF16),
                   jax.ShapeDtypeStruct((t, D_MODEL), BF16), jax.ShapeDtypeStruct((t, D_MODEL), BF16),
                   jax.ShapeDtypeStruct((t, D_MODEL), F32), jax.ShapeDtypeStruct((t, D_MODEL), BF16)],
        scratch_shapes=[_lane_scratch(tm, 512), _lane_scratch(tm, 256)],
        compiler_params=_params(("arbitrary",)),
    )(o0, o1, o2, l0, l1, l2, ob, gts, x, w_a, w_b, w_out, g_cross)


def _memkv(mem, g_mem, w_ckv):
    m = mem.shape[0]
    ws = w_ckv.shape[2]

    def body(mem_ref, g_ref, w_ref, mn_ref, kv_ref):
        xt = mem_ref[...]
        mn = (xt * _rms(xt) * g_ref[...]).astype(BF16)
        mn_ref[...] = mn
        for j in range(N_DEV):
            kv_ref[:, j * ws:(j + 1) * ws] = _dot(mn, w_ref[j]).astype(BF16)

    return pl.pallas_call(
        body, name="memkv",
        out_shape=[jax.ShapeDtypeStruct((m, D_MODEL), BF16), jax.ShapeDtypeStruct((m, 2 * D_MODEL), BF16)],
        compiler_params=_params(),
    )(mem, g_mem, w_ckv)


def _cross_probs(q, kv_ref, h):
    k = kv_ref[:, h * X_HEAD_DIM:(h + 1) * X_HEAD_DIM]
    sc = _dot_nt(q[:, h * X_HEAD_DIM:(h + 1) * X_HEAD_DIM], k)
    m = jnp.max(sc, axis=-1, keepdims=True)
    p = jnp.exp(sc - m)
    return p / jnp.sum(p, axis=-1, keepdims=True)


def _cross(hc, x1, kv, w_cq, w_co, g_mlp, tm):
    t = x1.shape[0]
    m = kv.shape[0]

    def body(hc_ref, x1_ref, kv_ref, wq_ref, wo_ref, g_ref, q_ref, o_ref, x2_ref, hm_ref):
        q = (_dot(hc_ref[...], wq_ref[...]) * X_SCALE).astype(BF16)
        q_ref[...] = q
        outs = []
        for h in range(X_HEADS):
            p = _cross_probs(q, kv_ref, h)
            v = kv_ref[:, D_MODEL + h * X_HEAD_DIM:D_MODEL + (h + 1) * X_HEAD_DIM]
            outs.append(_dot(p.astype(BF16), v))
        o = jnp.concatenate(outs, axis=1).astype(BF16)
        o_ref[...] = o
        x2 = x1_ref[...] + _dot(o, wo_ref[...])
        x2_ref[...] = x2
        hm_ref[...] = (x2 * _rms(x2) * g_ref[...]).astype(BF16)

    row = lambda w: pl.BlockSpec((tm, w), lambda i: (i, 0))
    full = lambda a, b: pl.BlockSpec((a, b), lambda i: (0, 0))
    return pl.pallas_call(
        body, name="cross", grid=(t // tm,),
        in_specs=[row(D_MODEL), row(D_MODEL), full(m, 2 * D_MODEL), full(D_MODEL, D_MODEL),
                  full(D_MODEL, D_MODEL), full(1, D_MODEL)],
        out_specs=[row(D_MODEL)] * 4,
        out_shape=[jax.ShapeDtypeStruct((t, D_MODEL), BF16), jax.ShapeDtypeStruct((t, D_MODEL), BF16),
                   jax.ShapeDtypeStruct((t, D_MODEL), F32), jax.ShapeDtypeStruct((t, D_MODEL), BF16)],
        compiler_params=_params(("arbitrary",)),
    )(hc, x1, kv, w_cq, w_co, g_mlp)


def _mlp(hm, x2, w_1, w_2, g_final, target, tm, tf):
    t = x2.shape[0]
    nf = D_FF // tf

    def body(hm_ref, x2_ref, w1_ref, w2_ref, g_ref, tg_ref, a_ref, dx3_ref, loss_ref, dg_ref, acc_ref):
        i, f = pl.program_id(0), pl.program_id(1)
        hm_t = hm_ref[...]
        sw = w1_ref.shape[2]
        part = None
        for s in range(w1_ref.shape[0]):
            a = jnp.maximum(_dot(hm_t, w1_ref[s]), 0.0).astype(BF16)
            a_ref[:, s * sw:(s + 1) * sw] = a
            p_s = _dot(a * a, w2_ref[s * sw:(s + 1) * sw, :])
            part = p_s if part is None else part + p_s

        @pl.when(f == 0)
        def _():
            acc_ref[...] = part

        @pl.when(f > 0)
        def _():
            acc_ref[...] += part

        @pl.when((i == 0) & (f == 0))
        def _():
            loss_ref[...] = jnp.zeros_like(loss_ref)
            dg_ref[...] = jnp.zeros_like(dg_ref)

        @pl.when(f == nf - 1)
        def _():
            x3 = x2_ref[...] + acc_ref[...]
            r = _rms(x3)
            g = g_ref[...]
            diff = x3 * r * g - tg_ref[...]
            loss_ref[...] += 0.5 * jnp.sum(jnp.mean(diff * diff, axis=-1, keepdims=True))
            dx3, dg = _rms_bwd(diff / D_MODEL, x3, r, g)
            dx3_ref[...] = dx3
            dg_ref[...] += dg

    return pl.pallas_call(
        body, name="mlp", grid=(t // tm, nf),
        in_specs=[pl.BlockSpec((tm, D_MODEL), lambda i, f: (i, 0)),
                  pl.BlockSpec((tm, D_MODEL), lambda i, f: (i, 0)),
                  pl.BlockSpec((tf // w_1.shape[2], D_MODEL, w_1.shape[2]), lambda i, f: (f, 0, 0)),
                  pl.BlockSpec((tf, D_MODEL), lambda i, f: (f, 0)),
                  pl.BlockSpec((1, D_MODEL), lambda i, f: (0, 0)),
                  pl.BlockSpec((tm, D_MODEL), lambda i, f: (i, 0))],
        out_specs=[pl.BlockSpec((tm, tf), lambda i, f: (i, f)),
                   pl.BlockSpec((tm, D_MODEL), lambda i, f: (i, 0)),
                   pl.BlockSpec((1, 128), lambda i, f: (0, 0)),
                   pl.BlockSpec((1, D_MODEL), lambda i, f: (0, 0))],
        out_shape=[jax.ShapeDtypeStruct((t, D_FF), BF16), jax.ShapeDtypeStruct((t, D_MODEL), F32),
                   jax.ShapeDtypeStruct((1, 128), F32), jax.ShapeDtypeStruct((1, D_MODEL), F32)],
        scratch_shapes=[pltpu.VMEM((tm, D_MODEL), F32)],
        compiler_params=_params(("arbitrary", "arbitrary")),
    )(hm, x2, w_1, w_2, g_final, target)


def _mlp_bwd(dx3, a, w_1, w_2, x2, g_mlp, tm, tf):
    t = x2.shape[0]
    nf = D_FF // tf

    def body(dx3_ref, a_ref, w1_ref, w2_ref, x2_ref, g_ref, dz_ref, dx2_ref, dg_ref, acc_ref):
        i, f = pl.program_id(0), pl.program_id(1)
        dx3_b = dx3_ref[...].astype(BF16)
        sw = w1_ref.shape[2]
        part = None
        for s in range(w1_ref.shape[0]):
            cols = slice(s * sw, (s + 1) * sw)
            da2 = _dot_nt(dx3_b, w2_ref[cols, :])
            dz = (2.0 * a_ref[:, cols].astype(F32) * da2).astype(BF16)
            dz_ref[:, cols] = dz
            p_s = _dot_nt(dz, w1_ref[s])
            part = p_s if part is None else part + p_s

        @pl.when(f == 0)
        def _():
            acc_ref[...] = part

        @pl.when(f > 0)
        def _():
            acc_ref[...] += part

        @pl.when((i == 0) & (f == 0))
        def _():
            dg_ref[...] = jnp.zeros_like(dg_ref)

        @pl.when(f == nf - 1)
        def _():
            xt = x2_ref[...]
            dx, dg = _rms_bwd(acc_ref[...], xt, _rms(xt), g_ref[...])
            dx2_ref[...] = dx3_ref[...] + dx
            dg_ref[...] += dg

    return pl.pallas_call(
        body, name="mlp_bwd", grid=(t // tm, nf),
        in_specs=[pl.BlockSpec((tm, D_MODEL), lambda i, f: (i, 0)),
                  pl.BlockSpec((tm, tf), lambda i, f: (i, f)),
                  pl.BlockSpec((tf // w_1.shape[2], D_MODEL, w_1.shape[2]), lambda i, f: (f, 0, 0)),
                  pl.BlockSpec((tf, D_MODEL), lambda i, f: (f, 0)),
                  pl.BlockSpec((tm, D_MODEL), lambda i, f: (i, 0)),
                  pl.BlockSpec((1, D_MODEL), lambda i, f: (0, 0))],
        out_specs=[pl.BlockSpec((tm, tf), lambda i, f: (i, f)),
                   pl.BlockSpec((tm, D_MODEL), lambda i, f: (i, 0)),
                   pl.BlockSpec((1, D_MODEL), lambda i, f: (0, 0))],
        out_shape=[jax.ShapeDtypeStruct((t, D_FF), BF16), jax.ShapeDtypeStruct((t, D_MODEL), F32),
                   jax.ShapeDtypeStruct((1, D_MODEL), F32)],
        scratch_shapes=[pltpu.VMEM((tm, D_MODEL), F32)],
        compiler_params=_params(("arbitrary", "arbitrary")),
    )(dx3, a, w_1, w_2, x2, g_mlp)


def _wgrad(name, a, b, tka, tn, tm, square=False, col_shards=False):
    t, ka = a.shape
    n = b.shape[1]
    nk = t // tm

    def body(a_ref, b_ref, o_ref, acc_ref):
        at = a_ref[...].astype(BF16)
        if square:
            at = at * at
        part = _dot_tn(at, b_ref[...].astype(BF16))
        k = pl.program_id(2)

        @pl.when(k == 0)
        def _():
            acc_ref[...] = part

        @pl.when(k > 0)
        def _():
            acc_ref[...] += part

        @pl.when(k == nk - 1)
        def _():
            if col_shards:
                for s in range(tn // sw):
                    o_ref[s] = acc_ref[:, s * sw:(s + 1) * sw].astype(BF16)
            else:
                o_ref[...] = acc_ref[...].astype(BF16)

    if col_shards:
        sw = n // N_DEV
        out_spec = pl.BlockSpec((tn // sw, tka, sw), lambda p, q, k: (q, p, 0))
        out_shape = jax.ShapeDtypeStruct((N_DEV, ka, sw), BF16)
    else:
        out_spec = pl.BlockSpec((tka, tn), lambda p, q, k: (p, q))
        out_shape = jax.ShapeDtypeStruct((ka, n), BF16)
    return pl.pallas_call(
        body, name=name, grid=(ka // tka, n // tn, nk),
        in_specs=[pl.BlockSpec((tm, tka), lambda p, q, k: (k, p)),
                  pl.BlockSpec((tm, tn), lambda p, q, k: (k, q))],
        out_specs=out_spec, out_shape=out_shape,
        scratch_shapes=[pltpu.VMEM((tka, tn), F32)],
        compiler_params=_params(("arbitrary", "arbitrary", "arbitrary")),
    )(a, b)


def _cross_bwd(dx2, x1, q, kv, w_cq, w_co, g_cross, tm, dep=None):
    t = x1.shape[0]
    m = kv.shape[0]

    def body(dx2_ref, x1_ref, q_ref, kv_ref, wq_ref, wo_ref, g_ref, dq_ref, dx1_ref, dkv_ref, dg_ref):
        @pl.when(pl.program_id(0) == 0)
        def _():
            dkv_ref[...] = jnp.zeros_like(dkv_ref)
            dg_ref[...] = jnp.zeros_like(dg_ref)

        do = _dot_nt(dx2_ref[...].astype(BF16), wo_ref[...]).astype(BF16)
        q = q_ref[...]
        dqs = []
        for h in range(X_HEADS):
            hs = slice(h * X_HEAD_DIM, (h + 1) * X_HEAD_DIM)
            vs = slice(D_MODEL + h * X_HEAD_DIM, D_MODEL + (h + 1) * X_HEAD_DIM)
            p = _cross_probs(q, kv_ref, h)
            dp = _dot_nt(do[:, hs], kv_ref[:, vs])
            ds = (p * (dp - jnp.sum(dp * p, axis=-1, keepdims=True))).astype(BF16)
            dqs.append(_dot(ds, kv_ref[:, hs]))
            dkv_ref[:, hs] += _dot_tn(ds, q[:, hs])
            dkv_ref[:, vs] += _dot_tn(p.astype(BF16), do[:, hs])
        dq = (jnp.concatenate(dqs, axis=1) * X_SCALE).astype(BF16)
        dq_ref[...] = dq
        xt = x1_ref[...]
        dx, dg = _rms_bwd(_dot_nt(dq, wq_ref[...]), xt, _rms(xt), g_ref[...])
        dx1_ref[...] = dx2_ref[...] + dx
        dg_ref[...] += dg

    row = lambda w: pl.BlockSpec((tm, w), lambda i: (i, 0))
    full = lambda a, b: pl.BlockSpec((a, b), lambda i: (0, 0))
    return pl.pallas_call(
        _with_dep(body, 7, dep), name="cross_bwd", grid=(t // tm,),
        in_specs=[row(D_MODEL), row(D_MODEL), row(D_MODEL), full(m, 2 * D_MODEL), full(D_MODEL, D_MODEL),
                  full(D_MODEL, D_MODEL), full(1, D_MODEL)] + _dep_spec(dep),
        out_specs=[row(D_MODEL), row(D_MODEL), full(m, 2 * D_MODEL), full(1, D_MODEL)],
        out_shape=[jax.ShapeDtypeStruct((t, D_MODEL), BF16), jax.ShapeDtypeStruct((t, D_MODEL), F32),
                   jax.ShapeDtypeStruct((m, 2 * D_MODEL), F32), jax.ShapeDtypeStruct((1, D_MODEL), F32)],
        compiler_params=_params(("arbitrary",)),
    )(dx2, x1, q, kv, w_cq, w_co, g_cross, *_dep_arg(dep))


def _memkv_bwd(dkv, mn, mem, w_ckv, g_mem):
    ws = w_ckv.shape[2]

    def body(dkv_ref, mn_ref, mem_ref, w_ref, g_ref, dw_ref, dg_ref):
        mn = mn_ref[...]
        dmn = jnp.zeros(mn.shape, F32)
        for j in range(N_DEV):
            dkvb = dkv_ref[:, j * ws:(j + 1) * ws].astype(BF16)
            dw_ref[j] = _dot_tn(mn, dkvb).astype(BF16)
            dmn = dmn + _dot_nt(dkvb, w_ref[j])
        xt = mem_ref[...]
        dg_ref[...] = jnp.sum(dmn * xt * _rms(xt), axis=0, keepdims=True)

    return pl.pallas_call(
        body, name="memkv_bwd",
        out_shape=[jax.ShapeDtypeStruct(w_ckv.shape, BF16), jax.ShapeDtypeStruct((1, D_MODEL), F32)],
        compiler_params=_params(),
    )(dkv, mn, mem, w_ckv, g_mem)


def _merge_bwd(dx1, ya, yb, gts, w_out, w_g, tm):
    t = dx1.shape[0]

    def body(dx1_ref, ya_ref, yb_ref, g_ref, wo_ref, wg_ref, dg_ref, dhp_ref, dya_ref, dyb_ref, db_ref):
        @pl.when(pl.program_id(0) == 0)
        def _():
            db_ref[...] = jnp.zeros_like(db_ref)

        dm = _dot_nt(dx1_ref[...].astype(BF16), wo_ref[...])
        ga = g_ref[:, :D_MODEL].astype(F32)
        gb = g_ref[:, D_MODEL:].astype(F32)
        dya_ref[...] = (dm * ga).astype(BF16)
        dyb_ref[...] = (dm * gb).astype(BF16)
        dpa = dm * ya_ref[...].astype(F32) * ga * (1.0 - ga)
        dpb = dm * yb_ref[...].astype(F32) * gb * (1.0 - gb)
        dpre = jnp.concatenate([dpa, dpb], axis=1)
        db_ref[...] += jnp.sum(dpre, axis=0, keepdims=True)
        dpreb = dpre.astype(BF16)
        dg_ref[...] = dpreb
        dhp_ref[...] = _dot_nt(dpreb, wg_ref[...])

    row = lambda w: pl.BlockSpec((tm, w), lambda i: (i, 0))
    once = lambda a, b: pl.BlockSpec((a, b), lambda i: (0, 0), pipeline_mode=pl.Buffered(1))
    sds = jax.ShapeDtypeStruct
    return pl.pallas_call(
        body, name="merge_bwd", grid=(t // tm,),
        in_specs=[row(D_MODEL), row(D_MODEL), row(D_MODEL), row(GATE_WIDTH),
                  once(D_MODEL, D_MODEL), once(D_MODEL, GATE_WIDTH)],
        out_specs=[row(GATE_WIDTH), row(D_MODEL), row(D_MODEL), row(D_MODEL),
                   pl.BlockSpec((1, GATE_WIDTH), lambda i: (0, 0))],
        out_shape=[sds((t, GATE_WIDTH), BF16), sds((t, D_MODEL), F32), sds((t, D_MODEL), BF16),
                   sds((t, D_MODEL), BF16), sds((1, GATE_WIDTH), F32)],
        compiler_params=_params(("arbitrary",)),
    )(dx1, ya, yb, gts, w_out, w_g)


def _combine_bwd(dya, dyb, oa, ob, l0, l1, l2, lb, sink_row, w_a, w_b, tm):
    t = dya.shape[0]

    def body(dya_ref, dyb_ref, oa_ref, ob_ref, l0_ref, l1_ref, l2_ref, lb_ref, sk_ref, wa_ref, wb_ref,
             do0_ref, do1_ref, do2_ref, c0_ref, c1_ref, c2_ref, dob_ref, cb_ref, dsk_ref, so_ref, sl_ref):
        @pl.when(pl.program_id(0) == 0)
        def _():
            dsk_ref[...] = jnp.zeros_like(dsk_ref)

        doa = _dot_nt(dya_ref[...], wa_ref[...])
        dob = _dot_nt(dyb_ref[...], wb_ref[...])
        dsum = _head_sums(doa * oa_ref[...].astype(F32), _head_gather(256, True))
        a0, a1, a2 = _alphas(l0_ref[...], _interleave(l1_ref, sl_ref), _interleave(l2_ref, sl_ref))
        c0_ref[...] = a0 * dsum
        spread = _dil_head_spread()
        do0_ref[...] = _head_scale(doa, a0, spread).astype(BF16)
        for al, do_ref, c_ref in ((a1, do1_ref, c1_ref), (a2, do2_ref, c2_ref)):
            _deinterleave(al * dsum, sl_ref, c_ref, F32)
            _deinterleave(_head_scale(doa, al, spread), so_ref, do_ref, BF16)
        dob_ref[...] = dob.astype(BF16)
        cb = _head_sums(dob * ob_ref[...], _head_gather(128, False))
        cb_ref[...] = cb
        lane = lax.broadcasted_iota(jnp.int32, cb.shape, 1)
        psink = jnp.where(lane < 8, jnp.exp(sk_ref[...] - lb_ref[...]), 0.0)
        dsk_ref[...] += jnp.sum(-psink * cb, axis=0, keepdims=True)

    row = lambda w: pl.BlockSpec((tm, w), lambda i: (i, 0))
    full = lambda a, b: pl.BlockSpec((a, b), lambda i: (0, 0))
    sds = jax.ShapeDtypeStruct
    d1, d2 = l1.shape[0], l2.shape[0]
    res = lambda d, w: pl.BlockSpec((d, tm // d, w), lambda i: (0, i, 0))
    return pl.pallas_call(
        body, name="combine_bwd", grid=(t // tm,),
        in_specs=[row(D_MODEL), row(D_MODEL), row(512), row(512),
                  row(256), _res_spec(l1, tm), _res_spec(l2, tm), row(128), full(1, 128),
                  full(512, D_MODEL), full(512, D_MODEL)],
        out_specs=[row(512), res(d1, 512), res(d2, 512), row(256), res(d1, 256), res(d2, 256),
                   row(512), row(128), full(1, 128)],
        out_shape=[sds((t, 512), BF16), sds((d1, t // d1, 512), BF16),
                   sds((d2, t // d2, 512), BF16), sds((t, 256), F32), sds((d1, t // d1, 256), F32),
                   sds((d2, t // d2, 256), F32), sds((t, 512), BF16),
                   sds((t, 128), F32), sds((1, 128), F32)],
        scratch_shapes=[_lane_scratch(tm, 512), _lane_scratch(tm, 256)],
        compiler_params=_params(("arbitrary",)),
    )(dya, dyb, oa, ob, l0, l1, l2, lb, sink_row, w_a, w_b)


def _attn_bwd(name, pv, dov, lsev, cv, cosv, sinv, swa, tq, dep=None):
    d, ls = pv.shape[0], pv.shape[1]
    n, nsb = ls // tq, tq // BAND
    pairs = _attn_layout(swa)
    ncol = 1 if swa else 2
    ow = 128 * len(pairs)

    kv_slots = sorted({(ko, vo) for _, ko, vo in pairs})

    def body(cur_ref, tail_ref, do_ref, lse_ref, c_ref, cos_ref, sin_ref, out_ref, acc_ref, carry_ref, acct_ref):
        i = pl.program_id(2)
        blk_i = n - 1 - i
        acc_ref[...] = jnp.zeros_like(acc_ref)
        acct_ref[...] = jnp.zeros_like(acct_ref)

        @pl.when(i == 0)
        def _():
            carry_ref[...] = jnp.zeros_like(carry_ref)

        qk_a, v_a = _head_a_masks(BAND)
        dim = lax.broadcasted_iota(jnp.int32, (128, BAND), 0)
        qk_at, v_at = (dim % HEAD_DIM) < HEAD_DIM // 2, dim < HEAD_DIM
        for s in range(nsb):
            mask = _band_mask(blk_i, s)
            mask2 = jnp.concatenate([mask, mask], axis=0)
            rows = slice(s * BAND, (s + 1) * BAND)
            kcols = slice(s * BAND, (s + 2) * BAND)
            for j, (qo, ko, vo) in enumerate(pairs):
                slot = kv_slots.index((ko, vo))
                kk = _kv_rows(cur_ref, tail_ref, s, ko)
                vv = _kv_rows(cur_ref, tail_ref, s, vo)
                q, do = cur_ref[rows, qo:qo + 128], do_ref[rows, j * 128:(j + 1) * 128]
                q2, do2 = _stack_heads(q, qk_a), _stack_heads(do, v_a)
                col2 = lambda ref: jnp.concatenate([ref[rows, 2 * j:2 * j + 1], ref[rows, 2 * j + 1:2 * j + 2]], axis=0)
                sc = _dot_nt(q2, kk)
                p = jnp.exp(jnp.where(mask2, sc, -jnp.inf) - col2(lse_ref))
                dp = _dot_nt(do2, vv)
                ds = (p * (dp - col2(c_ref))).astype(BF16)
                dq2 = _dot(ds, kk)
                acc_ref[BAND + s * BAND:BAND + (s + 1) * BAND, qo:qo + 128] += jnp.where(qk_a, dq2[:BAND], dq2[BAND:])
                acct_ref[2 * slot, :, kcols] += _dot(_stack_heads_t(q.T, qk_at), ds)
                acct_ref[2 * slot + 1, :, kcols] += _dot(_stack_heads_t(do.T, v_at), p.astype(BF16))
        for slot, (ko, vo) in enumerate(kv_slots):
            acc_ref[:, ko:ko + 128] += acct_ref[2 * slot].T
            acc_ref[:, vo:vo + 128] += acct_ref[2 * slot + 1].T

        last = acc_ref[tq:, :] + carry_ref[...]
        fin = last if tq == BAND else jnp.concatenate([acc_ref[BAND:tq, :], last], axis=0)
        out_ref[...] = _rope(fin, cos_ref[...], sin_ref[...], swa, -1).astype(BF16)
        carry_ref[...] = acc_ref[0:BAND, :]

    rev = lambda i: n - 1 - i
    blk = lambda rows, w, row_of: pl.BlockSpec((None, rows, w), lambda r, cb, i: (r, row_of(i), cb))
    tab = pl.BlockSpec((None, tq, 128), lambda r, cb, i: (r, rev(i), 0))
    return pl.pallas_call(
        _with_dep(body, 7, dep), name=name, grid=(d, ncol, n),
        in_specs=[blk(tq, PBLK, rev), blk(BAND, PBLK, lambda i: jnp.maximum(rev(i) * nsb - 1, 0)),
                  blk(tq, ow, rev), blk(tq, 128, rev), blk(tq, 128, rev), tab, tab] + _dep_spec(dep),
        out_specs=blk(tq, PBLK, rev),
        out_shape=jax.ShapeDtypeStruct((d, ls, ncol * PBLK), BF16),
        scratch_shapes=[pltpu.VMEM((tq + BAND, PBLK), F32), pltpu.VMEM((BAND, PBLK), F32),
                        pltpu.VMEM((2 * len(kv_slots), 128, tq + BAND), F32)],
        compiler_params=_params(("arbitrary", "arbitrary", "arbitrary")),
    )(pv, pv, dov, lsev, cv, cosv, sinv, *_dep_arg(dep))


def _dx(dp0, dp1, dp2, dpb, w_p, dh_part, dx1, x, g_mix, tm, dep=None):
    t = x.shape[0]
    gw = 2 * PBLK

    def body(dp0_ref, dp1_ref, dp2_ref, dpb_ref, w_ref, dhp_ref, dx1_ref, x_ref, g_ref, gx_ref, dg_ref,
             dpt_ref, scr_ref):
        @pl.when(pl.program_id(0) == 0)
        def _():
            dg_ref[...] = jnp.zeros_like(dg_ref)

        dpt_ref[:, 0:gw] = dp0_ref[...]
        dpt_ref[:, gw:2 * gw] = _interleave(dp1_ref, scr_ref).astype(BF16)
        dpt_ref[:, 2 * gw:3 * gw] = _interleave(dp2_ref, scr_ref).astype(BF16)
        dpt_ref[:, 3 * gw:] = dpb_ref[...]
        dh = _dot_nt(dpt_ref[...], w_ref[...]) + dhp_ref[...]
        xt = x_ref[...]
        dx, dg = _rms_bwd(dh, xt, _rms(xt), g_ref[...])
        gx_ref[...] = dx1_ref[...] + dx
        dg_ref[...] += dg

    row = lambda w: pl.BlockSpec((tm, w), lambda i: (i, 0))
    full = lambda a, b: pl.BlockSpec((a, b), lambda i: (0, 0))
    return pl.pallas_call(
        _with_dep(body, 9, dep), name="dx", grid=(t // tm,),
        in_specs=[row(gw), _res_spec(dp1, tm), _res_spec(dp2, tm), row(PBLK),
                  pl.BlockSpec((D_MODEL, P_WIDTH), lambda i: (0, 0), pipeline_mode=pl.Buffered(1)),
                  row(D_MODEL), row(D_MODEL), row(D_MODEL), full(1, D_MODEL)] + _dep_spec(dep),
        out_specs=[row(D_MODEL), full(1, D_MODEL)],
        out_shape=[jax.ShapeDtypeStruct((t, D_MODEL), F32), jax.ShapeDtypeStruct((1, D_MODEL), F32)],
        scratch_shapes=[pltpu.VMEM((tm, P_WIDTH), BF16), _lane_scratch(tm, gw)],
        compiler_params=_params(("arbitrary",)),
    )(dp0, dp1, dp2, dpb, w_p, dh_part, dx1, x, g_mix, *_dep_arg(dep))


MESH = pl.DeviceIdType.MESH
HBM_SPEC = pl.BlockSpec(memory_space=pltpu.HBM)
VMEM_SPEC = pl.BlockSpec(memory_space=pltpu.VMEM)


def _all_gather(xp, act, g, tm):
    t = act.shape[0]
    d1, d2 = DIL_GROUPS[1][1], DIL_GROUPS[2][1]

    def body(x_ref, act_ref, g_ref, out_ref, h_ref, h1_ref, h2_ref, send_sems, recv_sems, local_sem, hf_ref):
        x, y, c = lax.axis_index("x"), lax.axis_index("y"), lax.axis_index("c")
        me, sibling = (x, y, c), (x, y, 1 - c)
        chips = [(1 - x, y), (x, 1 - y), (1 - x, 1 - y)]

        def rows(px, py, pc):
            return out_ref.at[4 * px + 2 * py + pc]

        def copy(k, block, to, src=None):
            return pltpu.make_async_remote_copy(
                src_ref=rows(*block) if src is None else src, dst_ref=rows(*block),
                send_sem=send_sems.at[k], recv_sem=recv_sems.at[k], device_id=to, device_id_type=MESH)

        mine = pltpu.make_async_copy(x_ref, rows(*me), local_sem)
        mine.start()
        first = [copy(0, me, sibling, src=x_ref)]
        first += [copy(1 + j, me, (*chip, c), src=x_ref) for j, chip in enumerate(chips)]
        for cp in first:
            cp.start()

        def norm(a_blk, h_blk, h1_blk, h2_blk):
            xt = a_blk[...]
            hf = xt * _rms(xt) * g_ref[...]
            h_blk[...] = hf.astype(BF16)
            _deinterleave(hf, hf_ref, h1_blk, BF16)
            _deinterleave(hf, hf_ref, h2_blk, BF16)

        res = lambda d: pl.BlockSpec((d, tm // d, D_MODEL), lambda i: (0, i, 0))
        row = pl.BlockSpec((tm, D_MODEL), lambda i: (i, 0))
        pltpu.emit_pipeline(norm, grid=(t // tm,), in_specs=[row], out_specs=[row, res(d1), res(d2)])(
            act_ref, h_ref, h1_ref, h2_ref)

        passed = [copy(4 + j, (*chip, c), sibling) for j, chip in enumerate(chips)]
        for j, chip in enumerate(chips):
            copy(1 + j, (*chip, c), me).wait_recv()
            passed[j].start()
        copy(0, sibling, me).wait_recv()
        for j, chip in enumerate(chips):
            copy(4 + j, (*chip, 1 - c), me).wait_recv()
        for cp in first + passed:
            cp.wait_send()
        mine.wait()

    sds = jax.ShapeDtypeStruct
    return pl.pallas_call(
        body, name="all_gather",
        out_shape=[sds((N_DEV,) + xp.shape, xp.dtype), sds((t, D_MODEL), BF16),
                   sds((d1, t // d1, D_MODEL), BF16), sds((d2, t // d2, D_MODEL), BF16)],
        in_specs=[HBM_SPEC, HBM_SPEC, VMEM_SPEC], out_specs=[HBM_SPEC] * 4,
        scratch_shapes=[pltpu.SemaphoreType.DMA((7,)), pltpu.SemaphoreType.DMA((7,)), pltpu.SemaphoreType.DMA,
                        _lane_scratch(tm, D_MODEL)],
        compiler_params=pltpu.CompilerParams(vmem_limit_bytes=VMEM_LIMIT),
    )(xp, act, g)


def _peers():
    x, y, c = lax.axis_index("x"), lax.axis_index("y"), lax.axis_index("c")
    out = []
    for k in range(1, N_DEV):
        px = 1 - x if k & 4 else x
        py = 1 - y if k & 2 else y
        pc = 1 - c if k & 1 else c
        out.append((k, (px, py, pc), 4 * px + 2 * py + pc))
    return out


def _my_index():
    return 4 * lax.axis_index("x") + 2 * lax.axis_index("y") + lax.axis_index("c")


SEM_SPEC = pl.BlockSpec(memory_space=pltpu.SEMAPHORE)
ANY_SPEC = pl.BlockSpec(memory_space=pl.ANY)
_SPLIT_PARAMS = pltpu.CompilerParams(has_side_effects=pltpu.SideEffectType.DATAFLOW_SIDE_EFFECTING)


def _split_copies(gather, src_refs, land_refs, send_sems, recv_sems):
    me_idx = _my_index()
    out = []
    for a, (src_ref, land_ref) in enumerate(zip(src_refs, land_refs)):
        for k, peer, peer_idx in _peers():
            if gather:
                src, dst = src_ref, land_ref.at[me_idx]
            else:
                src, dst = src_ref.at[peer_idx], land_ref.at[k - 1]
            out.append(pltpu.make_async_remote_copy(
                src_ref=src, dst_ref=dst, send_sem=send_sems.at[7 * a + k - 1], recv_sem=recv_sems.at[7 * a + k - 1],
                device_id=peer, device_id_type=MESH))
    return out


def _split_start(name, gather, srcs):
    n = len(srcs)

    def body(*refs):
        send_sems, recv_sems = refs[n], refs[n + 1]
        for cp in _split_copies(gather, refs[:n], refs[2 * n + 2:3 * n + 2], send_sems, recv_sems):
            cp.start()
        token = refs[-1]
        token[...] = jnp.zeros_like(token)

    lands = [pltpu.HBM((N_DEV,) + a.shape if gather else (N_DEV - 1,) + a.shape[1:], a.dtype) for a in srcs]
    return pl.pallas_call(
        body, name=name,
        out_shape=(pltpu.SemaphoreType.DMA((7 * n,)), pltpu.SemaphoreType.DMA((7 * n,)),
                   *[pltpu.HBM(a.shape, a.dtype) for a in srcs], *lands, jax.ShapeDtypeStruct((8, 128), F32)),
        in_specs=(HBM_SPEC,) * n, out_specs=(SEM_SPEC, SEM_SPEC) + (HBM_SPEC,) * (2 * n) + (VMEM_SPEC,),
        input_output_aliases={i: 2 + i for i in range(n)}, compiler_params=_SPLIT_PARAMS,
    )(*[pltpu.with_memory_space_constraint(a, pltpu.HBM) for a in srcs])


def _split_wait(name, gather, started, after):
    send_sems, recv_sems, bufs = started[0], started[1], started[2:-1]
    n = len(bufs) // 2

    def body(*refs):
        for cp in _split_copies(gather, refs[:n], refs[n:2 * n], refs[2 * n], refs[2 * n + 1]):
            cp.wait_send()
            cp.wait_recv()

    out = pl.pallas_call(
        body, name=name, out_shape=tuple(pltpu.HBM(a.shape, a.dtype) for a in bufs),
        in_specs=(HBM_SPEC,) * (2 * n) + (SEM_SPEC, SEM_SPEC, ANY_SPEC), out_specs=(HBM_SPEC,) * (2 * n),
        input_output_aliases={i: i for i in range(2 * n)}, compiler_params=_SPLIT_PARAMS,
    )(*bufs, send_sems, recv_sems, after)
    return out[:n], out[n:]


def _adam_update(g, w, m, v):
    nm = ADAM_B1 * m + (1.0 - ADAM_B1) * g
    nv = ADAM_B2 * v + (1.0 - ADAM_B2) * (g * g)
    m_hat = nm / (1.0 - ADAM_B1 ** ADAM_STEP)
    v_hat = nv / (1.0 - ADAM_B2 ** ADAM_STEP)
    return -ADAM_LR * (m_hat / (jnp.sqrt(v_hat) + ADAM_EPS) + ADAM_WD * w), nm, nv


def _adamw(name, me, sent, got, w, m, v, tr, tc=None):
    r, c = w.shape
    tc = c if tc is None else tc

    def body(me_ref, own_ref, got_ref, w_ref, m_ref, v_ref, g_ref, d_ref, nm_ref, nv_ref):
        g = own_ref[...].astype(F32)
        for k in range(N_DEV - 1):
            g = g + got_ref[k].astype(F32)
        g_ref[...] = g
        d_ref[...], nm_ref[...], nv_ref[...] = _adam_update(g, w_ref[...], m_ref[...], v_ref[...])

    blk = pl.BlockSpec((tr, tc), lambda i, j, me_ref: (i, j))
    return pl.pallas_call(
        body, name=name,
        grid_spec=pltpu.PrefetchScalarGridSpec(
            num_scalar_prefetch=1, grid=(r // tr, c // tc),
            in_specs=[pl.BlockSpec((None, tr, tc), lambda i, j, me_ref: (me_ref[0], i, j)),
                      pl.BlockSpec((N_DEV - 1, tr, tc), lambda i, j, me_ref: (0, i, j)), blk, blk, blk],
            out_specs=[blk] * 4),
        out_shape=[jax.ShapeDtypeStruct((r, c), F32)] * 4,
        compiler_params=_params(("arbitrary", "arbitrary")),
    )(me, sent, got, w, m, v)


def _adamw_small(srecv, ws, ms, vs):
    nv_ = len(ws)

    def body(*refs):
        s_ref = refs[0]
        ins, outs = refs[1:1 + 3 * nv_], refs[1 + 3 * nv_:]
        g_all = s_ref[0]
        for k in range(1, N_DEV):
            g_all = g_all + s_ref[k]
        for i in range(nv_):
            n = ins[i].shape[1]
            g = g_all[i:i + 1, :n]
            d, nm, nv = _adam_update(g, ins[i][...], ins[nv_ + i][...], ins[2 * nv_ + i][...])
            outs[i][...], outs[nv_ + i][...], outs[2 * nv_ + i][...], outs[3 * nv_ + i][...] = g, d, nm, nv
        outs[-1][...] = g_all[nv_:nv_ + 1, :128]

    shapes = [jax.ShapeDtypeStruct(a.shape, F32) for a in ws]
    res = pl.pallas_call(body, name="adamw_small", out_shape=shapes * 4 + [jax.ShapeDtypeStruct((1, 128), F32)],
                         compiler_params=_params())(srecv, *ws, *ms, *vs)
    return [res[k * nv_:(k + 1) * nv_] for k in range(4)], res[-1]


def _cols_from_shards(a):
    return jnp.swapaxes(a, 0, 1).reshape(a.shape[1], a.shape[0] * a.shape[2])


def _shards_from_cols(a):
    return jnp.swapaxes(a.reshape(a.shape[0], N_DEV, a.shape[1] // N_DEV), 0, 1)


def _shards_from_rows(a):
    return a.reshape(N_DEV, a.shape[0] // N_DEV, a.shape[1])


def _pair_lanes(a):
    lead = a.shape[:-1]
    return a.reshape(lead + (2, 2, HEAD_DIM // 2)).swapaxes(-3, -2).reshape(lead + (128,))


def _split_w_in(w_in):
    rows = w_in.shape[0]
    dil = w_in[:, :3 * DIL_WIDTH].reshape(rows, 3, 3, 4, 128)
    dil = np.concatenate([_pair_lanes(dil[:, :2]), dil[:, 2:]], axis=1)
    dil = dil.transpose(0, 2, 3, 1, 4).reshape(rows, 3 * DIL_WIDTH)
    o = 3 * DIL_WIDTH
    qb = w_in[:, o:o + SWA_Q_WIDTH].reshape(rows, 2, 4, HEAD_DIM).transpose(0, 2, 1, 3).reshape(rows, 4, 128)
    qb = _pair_lanes(qb).reshape(rows, SWA_Q_WIDTH)
    kb = _pair_lanes(w_in[:, o + SWA_Q_WIDTH:o + SWA_Q_WIDTH + SWA_KV_WIDTH])
    vb = w_in[:, o + SWA_Q_WIDTH + SWA_KV_WIDTH:P_WIDTH]
    return np.concatenate([dil, qb, kb, vb], axis=1)


ROW_GRANULE = HEAD_DIM // 2


def _w_p_runs(piece_rows):
    order = _split_w_in(np.arange(IN_WIDTH)[None])[0]
    assert sorted(order.tolist()) == list(range(P_WIDTH))
    starts = np.cumsum([0] + list(piece_rows))
    runs = []
    for i in range(0, P_WIDTH, ROW_GRANULE):
        o = int(order[i])
        assert o % ROW_GRANULE == 0 and (order[i:i + ROW_GRANULE] == o + np.arange(ROW_GRANULE)).all()
        piece = int(np.searchsorted(starts, i, side="right")) - 1
        at = i - int(starts[piece])
        last = runs[-1] if runs else None
        if last and last[0] == piece and last[1] + last[3] == at and last[2] + last[3] == o:
            last[3] += ROW_GRANULE
        else:
            runs.append([piece, at, o, ROW_GRANULE])
    return [tuple(r) for r in runs]


def _move_rows(name, srcs, runs, n_rows):
    n_src = len(srcs)
    assert sorted(r for _, _, d, n in runs for r in range(d, d + n)) == list(range(n_rows))

    def body(*refs):
        dst_ref, sems = refs[n_src], refs[n_src + 1]
        copies = [pltpu.make_async_copy(refs[s].at[pl.ds(a, n)], dst_ref.at[pl.ds(d, n)], sems.at[i])
                  for i, (s, a, d, n) in enumerate(runs)]
        for cp in copies:
            cp.start()
        for cp in copies:
            cp.wait()

    return pl.pallas_call(
        body, name=name, out_shape=jax.ShapeDtypeStruct((n_rows, srcs[0].shape[1]), srcs[0].dtype),
        in_specs=[HBM_SPEC] * n_src, out_specs=HBM_SPEC,
        scratch_shapes=[pltpu.SemaphoreType.DMA((len(runs),))],
    )(*srcs)


def _swa_rows(w_b):
    return w_b.reshape(2, 4, HEAD_DIM, -1).transpose(1, 0, 2, 3).reshape(SWA_Q_WIDTH, -1)


def _swa_rows_inv(dw_b):
    return dw_b.reshape(4, 2, HEAD_DIM, -1).transpose(1, 0, 2, 3).reshape(SWA_Q_WIDTH, -1)


def _rope_tables(pos):
    half = HEAD_DIM // 2
    inv = ROPE_THETA ** (-jnp.arange(half, dtype=F32) / half)
    ang = pos.astype(F32)[:, None] * jnp.tile(inv, 4)
    sign = jnp.repeat(jnp.array([-1.0, 1.0], F32), 2 * half)
    return jnp.cos(ang), jnp.sin(ang) * sign


def _local_step(x, hs, mem, pos, target, w_in_t, dep, rest_weights, on_grads, g_mix, g_cross, g_mem, g_mlp, g_final, sink):
    t = x.shape[0]
    tm = min(512, t)
    tq = 1024
    tw = min(2048, t)
    w_p = jnp.swapaxes(_move_rows("w_p_rows", [w_in_t], [(0, o, at, n) for _, at, o, n in _w_p_runs([P_WIDTH])],
                                  P_WIDTH), 0, 1)
    cos, sin = lax.optimization_barrier(_rope_tables(pos))
    sink_row = jnp.pad(sink.reshape(2, 4).T.reshape(1, 8), ((0, 0), (0, 120)))
    tabs = [(cos[None], sin[None])]
    for _, d in DIL_GROUPS[1:]:
        tabs.append(tuple(a.reshape(t // d, d, 128).swapaxes(0, 1) for a in (cos, sin)))
    tabs.append(tabs[0])

    h, h1, h2 = hs
    p0, p1, p2, pb = _inproj(h, h1, h2, w_p, [(cos, sin), tabs[1], tabs[2]], tm, dep)
    ps = [p0[None], p1, p2, pb[None]]
    outs, lses = [], []
    for gi, pv in enumerate(ps):
        res = _attn_fwd(f"attn_fwd{gi}", pv, gi == 3, sink_row[0, :8], min(tq, pv.shape[1]))
        outs.append(res[0])
        lses.append(res[1])
    o0, l0, ob, lb, ob32 = outs[0][0], lses[0][0], outs[3][0], lses[3][0], res[2][0]
    wts = rest_weights(lb)
    w_b = _swa_rows(wts["w_branch_b"])
    tf = 2048
    w_g = wts["w_g"]
    gts = _gates(h, w_g, wts["b_gate"].reshape(1, GATE_WIDTH), min(1024, t), 1024)
    oa, ya, yb, merged, x1, hc = _mix(o0, outs[1], outs[2], l0, lses[1], lses[2], ob, gts, x,
                                      wts["w_branch_a"], w_b, wts["w_out"], g_cross, tm)
    mn, kv = _memkv(mem, g_mem, wts["w_ckv"])
    q, o, x2, hm = _cross(hc, x1, kv, wts["w_cq"], wts["w_co"], g_mlp, tm)
    a, dx3, loss, dg_final = _mlp(hm, x2, wts["w_1"], wts["w_2"], g_final.reshape(1, D_MODEL), target, tm, tf)

    grads = {}
    dz, dx2, dg_mlp = _mlp_bwd(dx3, a, wts["w_1"], wts["w_2"], x2, g_mlp, tm, tf)
    grads["w_2"] = _shards_from_rows(_wgrad("dw_2", a, dx3, 1024, 1024, tw, square=True))
    grads["w_1"] = _wgrad("dw_1", hm, dz, 1024, 1024, tw, col_shards=True)
    dep = on_grads(GROUP_A, grads)
    dq, dx1, dkv, dg_cross = _cross_bwd(dx2, x1, q, kv, wts["w_cq"], wts["w_co"], g_cross, tm, dep)
    grads["w_co"] = _shards_from_rows(_wgrad("dw_co", o, dx2, 1024, 1024, tw))
    grads["w_cq"] = _shards_from_rows(_wgrad("dw_cq", hc, dq, 1024, 1024, tw))
    grads["w_ckv"], dg_mem = _memkv_bwd(dkv, mn, mem, wts["w_ckv"], g_mem)
    dgt, dh_part, dya, dyb, db_gate = _merge_bwd(dx1, ya, yb, gts, wts["w_out"], w_g, tm)
    do0, do1, do2, c0, c1, c2, dob, cb, dsink = _combine_bwd(
        dya, dyb, oa, ob32, l0, lses[1], lses[2], lb, sink_row, wts["w_branch_a"], w_b, tm)
    grads["w_out"] = _shards_from_rows(_wgrad("dw_out", merged, dx1, 1024, 1024, tw))
    grads["w_branch_a"] = _shards_from_cols(_wgrad("dw_a", oa, dya, 512, 1024, tw))
    grads["w_branch_b"] = _shards_from_cols(_swa_rows_inv(_wgrad("dw_b", ob, dyb, 512, 1024, tw)))
    grads["b_gate"] = _shards_from_cols(db_gate.reshape(2, D_MODEL)).astype(BF16)
    dep = on_grads(GROUP_B, grads)
    dw_g_t = _wgrad("dw_g", dgt, h, 1024, 1024, tw)
    dps = []
    for gi, (pv, do_g, c_g) in enumerate(zip(ps, (do0[None], do1, do2, dob[None]), (c0[None], c1, c2, cb[None]))):
        dps.append(_attn_bwd(f"attn_bwd{gi}", pv, do_g, lses[gi], c_g, tabs[gi][0], tabs[gi][1], gi == 3,
                             min(tq, pv.shape[1]),
                             dep if gi == 0 else None))
    dw_p_t = [_wgrad(f"dw_p{gi}", dpg.reshape(t, -1), hh.reshape(t, D_MODEL), PBLK, 1024, tw)
              for gi, (hh, dpg) in enumerate(zip((h, h1, h2, h), dps))]
    runs = _w_p_runs([a.shape[0] for a in dw_p_t]) + [(len(dw_p_t), 0, P_WIDTH, GATE_WIDTH)]
    dw_in_t = _move_rows("dw_in_rows", dw_p_t + [dw_g_t], runs, IN_WIDTH)
    grads["w_in"] = dw_in_t.reshape(N_DEV, IN_WIDTH // N_DEV, D_MODEL)
    dep = on_grads(GROUP_C, grads)
    grad_x, dg_mix = _dx(dps[0][0], dps[1], dps[2], dps[3][0], w_p, dh_part, dx1, x, g_mix, tm, dep)
    dsink_heads = dsink[0, :8].reshape(4, 2).T.reshape(8)
    small = {"g_mix": dg_mix[0], "g_cross": dg_cross[0], "g_mem": dg_mem[0], "g_mlp": dg_mlp[0],
             "g_final": dg_final[0], "sink": dsink_heads}
    return loss[0, 0], grad_x, small


def kernel(x, mem, positions, g_mix, w_in, b_gate, sink, w_branch_a, w_branch_b, w_out, g_cross, g_mem, w_cq, w_ckv, w_co, g_mlp, w_1, w_2, g_final, loss_target, m_g_mix, m_w_in, m_b_gate, m_sink, m_w_branch_a, m_w_branch_b, m_w_out, m_g_cross, m_g_mem, m_w_cq, m_w_ckv, m_w_co, m_g_mlp, m_w_1, m_w_2, m_g_final, v_g_mix, v_w_in, v_b_gate, v_sink, v_w_branch_a, v_w_branch_b, v_w_out, v_g_cross, v_g_mem, v_w_cq, v_w_ckv, v_w_co, v_g_mlp, v_w_1, v_w_2, v_g_final):
    local = dict(locals())
    shard = {n: local[n][0] for n in GROUP_A + GROUP_B + GROUP_C}
    me = _my_index()
    me_arr = me.reshape(1).astype(jnp.int32)
    tags = {GROUP_A: "a", GROUP_B: "b", GROUP_C: "c"}

    transposed = lambda a: jnp.swapaxes(a, 0, 1)
    gathered_w_in, *hs = _all_gather(transposed(shard["w_in"]).astype(BF16), x[0], g_mix, min(512, x.shape[1]))
    w_in_t = gathered_w_in.reshape(-1, gathered_w_in.shape[2])
    rest = GROUP_A + GROUP_B

    def gathered(name, started, after):
        srcs, lands = _split_wait(name, True, started, after)
        return [lax.dynamic_update_slice(land, src[None], (me,) + (0,) * src.ndim) for src, land in zip(srcs, lands)]

    gather = _split_start("gather_start", True,
                          [shard[n] if n == "b_gate" else shard[n].astype(BF16) for n in rest])

    def rest_weights(after):
        full = {"w_g": transposed(w_in_t[P_WIDTH:])}
        for name, a in zip(rest, gathered("gather_wait", gather, after)):
            if name in ("w_1", "w_ckv"):
                full[name] = a
            elif name in _COL_SHARDED:
                full[name] = _cols_from_shards(a)
            else:
                full[name] = a.reshape(N_DEV * a.shape[1], a.shape[2])
        return full

    scatters = {}

    def on_grads(names, grads):
        scatters[names] = _split_start("scatter_start_" + tags[names], False, [grads[n] for n in names])
        return scatters[names][-1]

    loss, grad_x, small = _local_step(
        x[0], hs, mem[0], positions[0], loss_target[0], w_in_t, gather[-1], rest_weights, on_grads,
        g_mix, g_cross, g_mem, g_mlp, g_final, sink[0])

    sp = jnp.stack([small[n] if n != "sink" else jnp.pad(small[n], (0, LANES - 8)) for n in SMALL]
                   + [jnp.pad(loss.reshape(1), (0, LANES - 1)), jnp.zeros((LANES,), F32)])
    small_gather = _split_start("small_start", True, [sp])

    after, updated = small_gather[-1], {}
    for names in (GROUP_A, GROUP_B, GROUP_C):
        sent, got = _split_wait("scatter_wait_" + tags[names], False, scatters[names], after)
        for i, name in enumerate(names):
            view = transposed if name == "w_in" else (lambda a: a)
            outs = _adamw("adamw_" + name, me_arr, sent[i], got[i], view(shard[name]),
                          view(local["m_" + name][0]), view(local["v_" + name][0]), ADAM_ROWS[name], ADAM_COLS.get(name))
            updated[name] = [view(a)[None] for a in outs]
            after = outs[3]

    flat = lambda prefix: [local[prefix + n].reshape(1, -1) for n in SMALL]
    outs, loss_row = _adamw_small(gathered("small_wait", small_gather, after)[0], flat(""), flat("m_"), flat("v_"))
    for i, name in enumerate(SMALL):
        updated[name] = [outs[which][i].reshape(local[name].shape) for which in range(4)]

    order = ["g_mix", "w_in", "b_gate", "sink", "w_branch_a", "w_branch_b", "w_out", "g_cross", "g_mem", "w_cq",
             "w_ckv", "w_co", "g_mlp", "w_1", "w_2", "g_final"]
    res = [loss_row[0, 0], grad_x[None]]
    for which in range(4):
        res += [updated[n][which] for n in order]
    return tuple(res)
```

```python
import functools
import math

import jax
import jax.numpy as jnp
import numpy as np
from jax import lax
from jax.experimental import pallas as pl
from jax.experimental.pallas import tpu as pltpu

F32 = jnp.float32
BF16 = jnp.bfloat16

D_MODEL = 1024
HEAD_DIM = 64
DIL_GROUPS = ((128, 1), (512, 4), (2048, 16))
ROPE_THETA = 10000.0
X_HEADS = 4
X_HEAD_DIM = D_MODEL // X_HEADS
D_FF = 4 * D_MODEL
EPS = 1e-6
DIL_WIDTH = 1536
SWA_Q_WIDTH = 512
SWA_KV_WIDTH = 128
P_WIDTH = 3 * DIL_WIDTH + SWA_Q_WIDTH + 2 * SWA_KV_WIDTH
GATE_WIDTH = 2 * D_MODEL
IN_WIDTH = P_WIDTH + GATE_WIDTH
BAND = 128
PBLK = 768
Q_SCALE = HEAD_DIM ** -0.5
X_SCALE = X_HEAD_DIM ** -0.5

ADAM_LR = 0.001
ADAM_B1 = 0.9
ADAM_B2 = 0.999
ADAM_EPS = 1e-08
ADAM_WD = 0.01
ADAM_STEP = 10

N_DEV = 8
LANES = 1024
VMEM_LIMIT = 52 * 1024 * 1024

NT = (((1,), (1,)), ((), ()))
TN = (((0,), (0,)), ((), ()))

GROUP_A = ("w_1", "w_2")
GROUP_B = ("w_branch_a", "w_branch_b", "w_out", "w_cq", "w_ckv", "w_co", "b_gate")
GROUP_C = ("w_in",)
_COL_SHARDED = ("w_in", "w_branch_a", "w_branch_b", "w_ckv", "w_1", "b_gate")
ADAM_ROWS = {"w_in": 464, "w_branch_a": 512, "w_branch_b": 512, "w_out": 128, "w_cq": 128, "w_ckv": 512,
             "w_co": 128, "w_1": 256, "w_2": 256, "b_gate": 2}
ADAM_COLS = {"w_in": 256}
SMALL = ("g_mix", "g_cross", "g_mem", "g_mlp", "g_final", "sink")


def _params(sem=None):
    return pltpu.CompilerParams(dimension_semantics=sem, vmem_limit_bytes=VMEM_LIMIT)


def _dot(a, b):
    return jnp.dot(a, b, preferred_element_type=F32)


def _dot_nt(a, b):
    return lax.dot_general(a, b, NT, preferred_element_type=F32)


def _dot_tn(a, b):
    return lax.dot_general(a, b, TN, preferred_element_type=F32)


def _rms(xt):
    return lax.rsqrt(jnp.mean(xt * xt, axis=-1, keepdims=True) + EPS)


def _rms_bwd(dh, xt, r, g):
    xn = xt * r
    dxn = dh * g
    dx = r * (dxn - xn * jnp.mean(dxn * xn, axis=-1, keepdims=True))
    return dx, jnp.sum(dh * xn, axis=0, keepdims=True)


def _rope(x, c, s, swa, sign):
    kinds = "qqqqkv" if swa else "qkvqkv"
    cq, sq = c * Q_SCALE, s * (sign * Q_SCALE)
    sk = s * sign if sign != 1 else s
    out = []
    for ci, kind in enumerate(kinds):
        xc = x[:, ci * 128:(ci + 1) * 128]
        if kind == "v":
            out.append(xc)
        elif kind == "q":
            out.append(xc * cq + pltpu.roll(xc, 64, 1) * sq)
        else:
            out.append(xc * c + pltpu.roll(xc, 64, 1) * sk)
    return jnp.concatenate(out, axis=1)


def _lane_scratch(rows, w):
    return pltpu.VMEM((w // 128, rows, 128), F32)


def _deinterleave(val, scr_ref, dst_ref, dtype):
    d, n = dst_ref.shape[0], dst_ref.shape[1]
    nc = val.shape[1] // 128
    for c in range(nc):
        scr_ref[c] = val[:, c * 128:(c + 1) * 128]
    for r in range(d):
        rows = [scr_ref.at[c][pl.ds(r, n, stride=d), :] for c in range(nc)]
        dst_ref[r] = jnp.concatenate(rows, axis=1).astype(dtype)


def _res_spec(a, tm):
    d, w = a.shape[0], a.shape[2]
    return pl.BlockSpec((d, tm // d, w), lambda i: (0, i, 0))


def _interleave(src_ref, scr_ref):
    d, n = src_ref.shape[0], src_ref.shape[1]
    nc = src_ref.shape[2] // 128
    for r in range(d):
        v = src_ref[r].astype(F32)
        for c in range(nc):
            scr_ref.at[c][pl.ds(r, n, stride=d), :] = v[:, c * 128:(c + 1) * 128]
    return jnp.concatenate([scr_ref[c] for c in range(nc)], axis=1)


def _with_dep(body, n_in, dep):
    if dep is None:
        return body
    return lambda *refs: body(*refs[:n_in], *refs[n_in + 1:])


def _dep_spec(dep):
    return [] if dep is None else [pl.BlockSpec(memory_space=pl.ANY)]


def _dep_arg(dep):
    return [] if dep is None else [dep]


def _inproj(h, h1, h2, w_p, tabs, tm, dep=None):
    t = h.shape[0]
    gw = 2 * PBLK
    (cos, sin), (cos1, sin1), (cos2, sin2) = tabs[0], tabs[1], tabs[2]

    def body(h_ref, h1_ref, h2_ref, w_ref, c_ref, s_ref, c1_ref, s1_ref, c2_ref, s2_ref,
             p0_ref, p1_ref, p2_ref, pb_ref):
        rows = lambda ref: ref[...].reshape(tm, ref.shape[-1])
        groups = ((h_ref, c_ref, s_ref, p0_ref), (h1_ref, c1_ref, s1_ref, p1_ref), (h2_ref, c2_ref, s2_ref, p2_ref))
        for gi, (lhs_ref, cc_ref, ss_ref, out_ref) in enumerate(groups):
            lhs, cc, ss = rows(lhs_ref), rows(cc_ref), rows(ss_ref)
            for half in range(2):
                col = gi * gw + half * PBLK
                val = _rope(_dot(lhs, w_ref[:, col:col + PBLK]), cc, ss, False, 1).astype(BF16)
                if out_ref.ndim == 3:
                    out_ref[:, :, half * PBLK:(half + 1) * PBLK] = val.reshape(out_ref.shape[:2] + (PBLK,))
                else:
                    out_ref[:, half * PBLK:(half + 1) * PBLK] = val
        pb_ref[...] = _rope(_dot(h_ref[...], w_ref[:, 3 * gw:]), c_ref[...], s_ref[...], True, 1).astype(BF16)

    d1, d2 = DIL_GROUPS[1][1], DIL_GROUPS[2][1]
    row = lambda w: pl.BlockSpec((tm, w), lambda i: (i, 0))
    res = lambda d, w: pl.BlockSpec((d, tm // d, w), lambda i: (0, i, 0))
    sds = jax.ShapeDtypeStruct
    return pl.pallas_call(
        _with_dep(body, 10, dep), name="inproj", grid=(t // tm,),
        in_specs=[row(D_MODEL), res(d1, D_MODEL), res(d2, D_MODEL),
                  pl.BlockSpec((D_MODEL, P_WIDTH), lambda i: (0, 0), pipeline_mode=pl.Buffered(1)),
                  row(128), row(128), res(d1, 128), res(d1, 128), res(d2, 128), res(d2, 128)] + _dep_spec(dep),
        out_specs=[row(gw), res(d1, gw), res(d2, gw), row(PBLK)],
        out_shape=[sds((t, gw), BF16), sds((d1, t // d1, gw), BF16), sds((d2, t // d2, gw), BF16),
                   sds((t, PBLK), BF16)],
        compiler_params=_params(("arbitrary",)),
    )(h, h1, h2, w_p, cos, sin, cos1, sin1, cos2, sin2, *_dep_arg(dep))


def _gates(h, w_g, b, tm, tn):
    t = h.shape[0]

    def body(h_ref, w_ref, b_ref, o_ref):
        z = _dot(h_ref[...], w_ref[...]) + b_ref[...]
        o_ref[...] = (0.5 * jnp.tanh(0.5 * z) + 0.5).astype(BF16)

    return pl.pallas_call(
        body, name="gates", grid=(t // tm, GATE_WIDTH // tn),
        in_specs=[pl.BlockSpec((tm, D_MODEL), lambda i, j: (i, 0)),
                  pl.BlockSpec((D_MODEL, tn), lambda i, j: (0, j)),
                  pl.BlockSpec((1, tn), lambda i, j: (0, j))],
        out_specs=pl.BlockSpec((tm, tn), lambda i, j: (i, j)),
        out_shape=jax.ShapeDtypeStruct((t, GATE_WIDTH), BF16),
        compiler_params=_params(("arbitrary", "arbitrary")),
    )(h, w_g, b)


def _band_mask(i, s):
    row = lax.broadcasted_iota(jnp.int32, (BAND, 2 * BAND), 0)
    col = lax.broadcasted_iota(jnp.int32, (BAND, 2 * BAND), 1)
    band = (col >= row) & (col <= row + BAND)
    if s == 0:
        band = band & ((col >= BAND) | (i > 0))
    return band


def _head_a_masks(rows):
    lane = lax.broadcasted_iota(jnp.int32, (rows, 128), 1)
    return (lane % HEAD_DIM) < HEAD_DIM // 2, lane < HEAD_DIM


def _stack_heads(x, head_a):
    zero = jnp.zeros_like(x)
    return jnp.concatenate([jnp.where(head_a, x, zero), jnp.where(head_a, zero, x)], axis=0)


def _stack_heads_t(xt, head_a_t):
    zero = jnp.zeros_like(xt)
    return jnp.concatenate([jnp.where(head_a_t, xt, zero), jnp.where(head_a_t, zero, xt)], axis=1)


def _kv_rows(cur_ref, tail_ref, s, off):
    if s == 0:
        return jnp.concatenate([tail_ref[:, off:off + 128], cur_ref[0:BAND, off:off + 128]], axis=0)
    return cur_ref[(s - 1) * BAND:(s + 1) * BAND, off:off + 128]


def _attn_layout(swa):
    if swa:
        return [(128 * j, 512, 640) for j in range(4)]
    return [(0, 128, 256), (384, 512, 640)]


def _attn_fwd(name, pv, swa, sinks, tq):
    d, ls = pv.shape[0], pv.shape[1]
    n, nsb = ls // tq, tq // BAND
    pairs = _attn_layout(swa)
    ncol = 1 if swa else 2
    ow = 128 * len(pairs)

    def body(cur_ref, tail_ref, *rest):
        sink_ref, o_ref, lse_ref, o32_ref = rest if swa else (None,) + rest + (None,)
        i = pl.program_id(2)
        lane = lax.broadcasted_iota(jnp.int32, (BAND, 128), 1)
        qk_a, v_a = _head_a_masks(BAND)
        first = lax.broadcasted_iota(jnp.int32, (2 * BAND, 1), 0) < BAND
        for s in range(nsb):
            mask = _band_mask(i, s)
            mask2 = jnp.concatenate([mask, mask], axis=0)
            rows = slice(s * BAND, (s + 1) * BAND)
            lse_tile = jnp.zeros((BAND, 128), F32)
            for j, (qo, ko, vo) in enumerate(pairs):
                q = cur_ref[rows, qo:qo + 128]
                kk = _kv_rows(cur_ref, tail_ref, s, ko)
                vv = _kv_rows(cur_ref, tail_ref, s, vo)
                sc = _dot_nt(_stack_heads(q, qk_a), kk)
                sc = jnp.where(mask2, sc, -jnp.inf)
                m = jnp.max(sc, axis=-1, keepdims=True)
                if swa:
                    sk = jnp.where(first, sink_ref[2 * j], sink_ref[2 * j + 1])
                    m = jnp.maximum(m, sk)
                p = jnp.exp(sc - m)
                den = jnp.sum(p, axis=-1, keepdims=True)
                if swa:
                    den = den + jnp.exp(sk - m)
                lse = m + jnp.log(den)
                lse_tile = jnp.where(lane == 2 * j, lse[:BAND], jnp.where(lane == 2 * j + 1, lse[BAND:], lse_tile))
                o2 = _dot(p.astype(BF16), vv) * (1.0 / den)
                o = jnp.where(v_a, o2[:BAND], o2[BAND:])
                o_ref[rows, j * 128:(j + 1) * 128] = o.astype(BF16)
                if swa:
                    o32_ref[rows, j * 128:(j + 1) * 128] = o
            lse_ref[rows, :] = lse_tile

    in_specs = [pl.BlockSpec((None, tq, PBLK), lambda r, cb, i: (r, i, cb)),
                pl.BlockSpec((None, BAND, PBLK), lambda r, cb, i: (r, jnp.maximum(i * nsb - 1, 0), cb))]
    args = [pv, pv]
    out_specs = [pl.BlockSpec((None, tq, ow), lambda r, cb, i: (r, i, cb)),
                 pl.BlockSpec((None, tq, 128), lambda r, cb, i: (r, i, cb))]
    out_shape = [jax.ShapeDtypeStruct((d, ls, 512), BF16), jax.ShapeDtypeStruct((d, ls, 128 * ncol), F32)]
    if swa:
        in_specs.append(pl.BlockSpec(memory_space=pltpu.SMEM))
        args.append(sinks)
        out_specs.append(out_specs[0])
        out_shape.append(jax.ShapeDtypeStruct((d, ls, 512), F32))
    return pl.pallas_call(
        body, name=name, grid=(d, ncol, n),
        in_specs=in_specs, out_specs=out_specs, out_shape=out_shape,
        compiler_params=_params(("arbitrary", "arbitrary", "arbitrary")),
    )(*args)


def _lse_lane(head):
    return (head // 4) * 128 + head % 4


def _dil_head_spread():
    lane = lax.broadcasted_iota(jnp.int32, (256, 512), 0)
    head = lax.broadcasted_iota(jnp.int32, (256, 512), 1) // HEAD_DIM
    return (lane == _lse_lane(head)).astype(BF16)


def _head_scale(x, tile, spread):
    return x * _dot(tile.astype(BF16), spread)


def _head_gather(width, dil):
    head = lax.broadcasted_iota(jnp.int32, (8 * HEAD_DIM, width), 0) // HEAD_DIM
    lane = lax.broadcasted_iota(jnp.int32, (8 * HEAD_DIM, width), 1)
    return (lane == (_lse_lane(head) if dil else head)).astype(BF16)


def _head_sums(x, gather):
    hi = x.astype(BF16)
    lo = (x - hi.astype(F32)).astype(BF16)
    return _dot(hi, gather) + _dot(lo, gather)


def _alphas(l0, l1, l2):
    m = jnp.maximum(jnp.maximum(l0, l1), l2)
    e0, e1, e2 = jnp.exp(l0 - m), jnp.exp(l1 - m), jnp.exp(l2 - m)
    den = e0 + e1 + e2
    return e0 / den, e1 / den, e2 / den


def _mix(o0, o1, o2, l0, l1, l2, ob, gts, x, w_a, w_b, w_out, g_cross, tm):
    t = x.shape[0]

    def body(o0_ref, o1_ref, o2_ref, l0_ref, l1_ref, l2_ref, ob_ref, g_ref, x_ref, wa_ref, wb_ref, wo_ref,
             gc_ref, oa_ref, ya_ref, yb_ref, mg_ref, x1_ref, hc_ref, so_ref, sl_ref):
        a0, a1, a2 = _alphas(l0_ref[...], _interleave(l1_ref, sl_ref), _interleave(l2_ref, sl_ref))
        spread = _dil_head_spread()
        oa = (_head_scale(o0_ref[...].astype(F32), a0, spread)
              + _head_scale(_interleave(o1_ref, so_ref), a1, spread)
              + _head_scale(_interleave(o2_ref, so_ref), a2, spread))
        oab = oa.astype(BF16)
        oa_ref[...] = oab
        ya = _dot(oab, wa_ref[...])
        yb = _dot(ob_ref[...], wb_ref[...])
        ya_ref[...] = ya.astype(BF16)
        yb_ref[...] = yb.astype(BF16)
        merged = (g_ref[:, :D_MODEL].astype(F32) * ya + g_ref[:, D_MODEL:].astype(F32) * yb).astype(BF16)
        mg_ref[...] = merged
        x1 = x_ref[...] + _dot(merged, wo_ref[...])
        x1_ref[...] = x1
        hc_ref[...] = (x1 * _rms(x1) * gc_ref[...]).astype(BF16)

    row = lambda w: pl.BlockSpec((tm, w), lambda i: (i, 0))
    full = lambda a, b: pl.BlockSpec((a, b), lambda i: (0, 0))
    return pl.pallas_call(
        body, name="mix", grid=(t // tm,),
        in_specs=[row(512), _res_spec(o1, tm), _res_spec(o2, tm), row(256), _res_spec(l1, tm), _res_spec(l2, tm),
                  row(512), row(GATE_WIDTH),
                  row(D_MODEL), full(512, D_MODEL), full(512, D_MODEL), full(D_MODEL, D_MODEL), full(1, D_MODEL)],
        out_specs=[row(512), row(D_MODEL), row(D_MODEL), row(D_MODEL), row(D_MODEL), row(D_MODEL)],
        out_shape=[jax.ShapeDtypeStruct((t, 512), BF16), jax.ShapeDtypeStruct((t, D_MODEL), BF16),
                   jax.ShapeDtypeStruct((t, D_MODEL), BF16), jax.ShapeDtypeStruct((t, D_MODEL), BF16),
                   jax.ShapeDtypeStruct((t, D_MODEL), F32), jax.ShapeDtypeStruct((t, D_MODEL), BF16)],
        scratch_shapes=[_lane_scratch(tm, 512), _lane_scratch(tm, 256)],
        compiler_params=_params(("arbitrary",)),
    )(o0, o1, o2, l0, l1, l2, ob, gts, x, w_a, w_b, w_out, g_cross)


def _memkv(mem, g_mem, w_ckv):
    m = mem.shape[0]
    ws = w_ckv.shape[2]

    def body(mem_ref, g_ref, w_ref, mn_ref, kv_ref):
        xt = mem_ref[...]
        mn = (xt * _rms(xt) * g_ref[...]).astype(BF16)
        mn_ref[...] = mn
        for j in range(N_DEV):
            kv_ref[:, j * ws:(j + 1) * ws] = _dot(mn, w_ref[j]).astype(BF16)

    return pl.pallas_call(
        body, name="memkv",
        out_shape=[jax.ShapeDtypeStruct((m, D_MODEL), BF16), jax.ShapeDtypeStruct((m, 2 * D_MODEL), BF16)],
        compiler_params=_params(),
    )(mem, g_mem, w_ckv)


def _cross_probs(q, kv_ref, h):
    k = kv_ref[:, h * X_HEAD_DIM:(h + 1) * X_HEAD_DIM]
    sc = _dot_nt(q[:, h * X_HEAD_DIM:(h + 1) * X_HEAD_DIM], k)
    m = jnp.max(sc, axis=-1, keepdims=True)
    p = jnp.exp(sc - m)
    return p / jnp.sum(p, axis=-1, keepdims=True)


def _cross(hc, x1, kv, w_cq, w_co, g_mlp, tm):
    t = x1.shape[0]
    m = kv.shape[0]

    def body(hc_ref, x1_ref, kv_ref, wq_ref, wo_ref, g_ref, q_ref, o_ref, x2_ref, hm_ref):
        q = (_dot(hc_ref[...], wq_ref[...]) * X_SCALE).astype(BF16)
        q_ref[...] = q
        outs = []
        for h in range(X_HEADS):
            p = _cross_probs(q, kv_ref, h)
            v = kv_ref[:, D_MODEL + h * X_HEAD_DIM:D_MODEL + (h + 1) * X_HEAD_DIM]
            outs.append(_dot(p.astype(BF16), v))
        o = jnp.concatenate(outs, axis=1).astype(BF16)
        o_ref[...] = o
        x2 = x1_ref[...] + _dot(o, wo_ref[...])
        x2_ref[...] = x2
        hm_ref[...] = (x2 * _rms(x2) * g_ref[...]).astype(BF16)

    row = lambda w: pl.BlockSpec((tm, w), lambda i: (i, 0))
    full = lambda a, b: pl.BlockSpec((a, b), lambda i: (0, 0))
    return pl.pallas_call(
        body, name="cross", grid=(t // tm,),
        in_specs=[row(D_MODEL), row(D_MODEL), full(m, 2 * D_MODEL), full(D_MODEL, D_MODEL),
                  full(D_MODEL, D_MODEL), full(1, D_MODEL)],
        out_specs=[row(D_MODEL)] * 4,
        out_shape=[jax.ShapeDtypeStruct((t, D_MODEL), BF16), jax.ShapeDtypeStruct((t, D_MODEL), BF16),
                   jax.ShapeDtypeStruct((t, D_MODEL), F32), jax.ShapeDtypeStruct((t, D_MODEL), BF16)],
        compiler_params=_params(("arbitrary",)),
    )(hc, x1, kv, w_cq, w_co, g_mlp)


def _mlp(hm, x2, w_1, w_2, g_final, target, tm, tf):
    t = x2.shape[0]
    nf = D_FF // tf

    def body(hm_ref, x2_ref, w1_ref, w2_ref, g_ref, tg_ref, a_ref, dx3_ref, loss_ref, dg_ref, acc_ref):
        i, f = pl.program_id(0), pl.program_id(1)
        hm_t = hm_ref[...]
        sw = w1_ref.shape[2]
        part = None
        for s in range(w1_ref.shape[0]):
            a = jnp.maximum(_dot(hm_t, w1_ref[s]), 0.0).astype(BF16)
            a_ref[:, s * sw:(s + 1) * sw] = a
            p_s = _dot(a * a, w2_ref[s * sw:(s + 1) * sw, :])
            part = p_s if part is None else part + p_s

        @pl.when(f == 0)
        def _():
            acc_ref[...] = part

        @pl.when(f > 0)
        def _():
            acc_ref[...] += part

        @pl.when((i == 0) & (f == 0))
        def _():
            loss_ref[...] = jnp.zeros_like(loss_ref)
            dg_ref[...] = jnp.zeros_like(dg_ref)

        @pl.when(f == nf - 1)
        def _():
            x3 = x2_ref[...] + acc_ref[...]
            r = _rms(x3)
            g = g_ref[...]
            diff = x3 * r * g - tg_ref[...]
            loss_ref[...] += 0.5 * jnp.sum(jnp.mean(diff * diff, axis=-1, keepdims=True))
            dx3, dg = _rms_bwd(diff / D_MODEL, x3, r, g)
            dx3_ref[...] = dx3
            dg_ref[...] += dg

    return pl.pallas_call(
        body, name="mlp", grid=(t // tm, nf),
        in_specs=[pl.BlockSpec((tm, D_MODEL), lambda i, f: (i, 0)),
                  pl.BlockSpec((tm, D_MODEL), lambda i, f: (i, 0)),
                  pl.BlockSpec((tf // w_1.shape[2], D_MODEL, w_1.shape[2]), lambda i, f: (f, 0, 0)),
                  pl.BlockSpec((tf, D_MODEL), lambda i, f: (f, 0)),
                  pl.BlockSpec((1, D_MODEL), lambda i, f: (0, 0)),
                  pl.BlockSpec((tm, D_MODEL), lambda i, f: (i, 0))],
        out_specs=[pl.BlockSpec((tm, tf), lambda i, f: (i, f)),
                   pl.BlockSpec((tm, D_MODEL), lambda i, f: (i, 0)),
                   pl.BlockSpec((1, 128), lambda i, f: (0, 0)),
                   pl.BlockSpec((1, D_MODEL), lambda i, f: (0, 0))],
        out_shape=[jax.ShapeDtypeStruct((t, D_FF), BF16), jax.ShapeDtypeStruct((t, D_MODEL), F32),
                   jax.ShapeDtypeStruct((1, 128), F32), jax.ShapeDtypeStruct((1, D_MODEL), F32)],
        scratch_shapes=[pltpu.VMEM((tm, D_MODEL), F32)],
        compiler_params=_params(("arbitrary", "arbitrary")),
    )(hm, x2, w_1, w_2, g_final, target)


def _mlp_bwd(dx3, a, w_1, w_2, x2, g_mlp, tm, tf):
    t = x2.shape[0]
    nf = D_FF // tf

    def body(dx3_ref, a_ref, w1_ref, w2_ref, x2_ref, g_ref, dz_ref, dx2_ref, dg_ref, acc_ref):
        i, f = pl.program_id(0), pl.program_id(1)
        dx3_b = dx3_ref[...].astype(BF16)
        sw = w1_ref.shape[2]
        part = None
        for s in range(w1_ref.shape[0]):
            cols = slice(s * sw, (s + 1) * sw)
            da2 = _dot_nt(dx3_b, w2_ref[cols, :])
            dz = (2.0 * a_ref[:, cols].astype(F32) * da2).astype(BF16)
            dz_ref[:, cols] = dz
            p_s = _dot_nt(dz, w1_ref[s])
            part = p_s if part is None else part + p_s

        @pl.when(f == 0)
        def _():
            acc_ref[...] = part

        @pl.when(f > 0)
        def _():
            acc_ref[...] += part

        @pl.when((i == 0) & (f == 0))
        def _():
            dg_ref[...] = jnp.zeros_like(dg_ref)

        @pl.when(f == nf - 1)
        def _():
            xt = x2_ref[...]
            dx, dg = _rms_bwd(acc_ref[...], xt, _rms(xt), g_ref[...])
            dx2_ref[...] = dx3_ref[...] + dx
            dg_ref[...] += dg

    return pl.pallas_call(
        body, name="mlp_bwd", grid=(t // tm, nf),
        in_specs=[pl.BlockSpec((tm, D_MODEL), lambda i, f: (i, 0)),
                  pl.BlockSpec((tm, tf), lambda i, f: (i, f)),
                  pl.BlockSpec((tf // w_1.shape[2], D_MODEL, w_1.shape[2]), lambda i, f: (f, 0, 0)),
                  pl.BlockSpec((tf, D_MODEL), lambda i, f: (f, 0)),
                  pl.BlockSpec((tm, D_MODEL), lambda i, f: (i, 0)),
                  pl.BlockSpec((1, D_MODEL), lambda i, f: (0, 0))],
        out_specs=[pl.BlockSpec((tm, tf), lambda i, f: (i, f)),
                   pl.BlockSpec((tm, D_MODEL), lambda i, f: (i, 0)),
                   pl.BlockSpec((1, D_MODEL), lambda i, f: (0, 0))],
        out_shape=[jax.ShapeDtypeStruct((t, D_FF), BF16), jax.ShapeDtypeStruct((t, D_MODEL), F32),
                   jax.ShapeDtypeStruct((1, D_MODEL), F32)],
        scratch_shapes=[pltpu.VMEM((tm, D_MODEL), F32)],
        compiler_params=_params(("arbitrary", "arbitrary")),
    )(dx3, a, w_1, w_2, x2, g_mlp)


def _wgrad(name, a, b, tka, tn, tm, square=False, col_shards=False):
    t, ka = a.shape
    n = b.shape[1]
    nk = t // tm

    def body(a_ref, b_ref, o_ref, acc_ref):
        at = a_ref[...].astype(BF16)
        if square:
            at = at * at
        part = _dot_tn(at, b_ref[...].astype(BF16))
        k = pl.program_id(2)

        @pl.when(k == 0)
        def _():
            acc_ref[...] = part

        @pl.when(k > 0)
        def _():
            acc_ref[...] += part

        @pl.when(k == nk - 1)
        def _():
            if col_shards:
                for s in range(tn // sw):
                    o_ref[s] = acc_ref[:, s * sw:(s + 1) * sw].astype(BF16)
            else:
                o_ref[...] = acc_ref[...].astype(BF16)

    if col_shards:
        sw = n // N_DEV
        out_spec = pl.BlockSpec((tn // sw, tka, sw), lambda p, q, k: (q, p, 0))
        out_shape = jax.ShapeDtypeStruct((N_DEV, ka, sw), BF16)
    else:
        out_spec = pl.BlockSpec((tka, tn), lambda p, q, k: (p, q))
        out_shape = jax.ShapeDtypeStruct((ka, n), BF16)
    return pl.pallas_call(
        body, name=name, grid=(ka // tka, n // tn, nk),
        in_specs=[pl.BlockSpec((tm, tka), lambda p, q, k: (k, p)),
                  pl.BlockSpec((tm, tn), lambda p, q, k: (k, q))],
        out_specs=out_spec, out_shape=out_shape,
        scratch_shapes=[pltpu.VMEM((tka, tn), F32)],
        compiler_params=_params(("arbitrary", "arbitrary", "arbitrary")),
    )(a, b)


def _cross_bwd(dx2, x1, q, kv, w_cq, w_co, g_cross, tm, dep=None):
    t = x1.shape[0]
    m = kv.shape[0]

    def body(dx2_ref, x1_ref, q_ref, kv_ref, wq_ref, wo_ref, g_ref, dq_ref, dx1_ref, dkv_ref, dg_ref):
        @pl.when(pl.program_id(0) == 0)
        def _():
            dkv_ref[...] = jnp.zeros_like(dkv_ref)
            dg_ref[...] = jnp.zeros_like(dg_ref)

        do = _dot_nt(dx2_ref[...].astype(BF16), wo_ref[...]).astype(BF16)
        q = q_ref[...]
        dqs = []
        for h in range(X_HEADS):
            hs = slice(h * X_HEAD_DIM, (h + 1) * X_HEAD_DIM)
            vs = slice(D_MODEL + h * X_HEAD_DIM, D_MODEL + (h + 1) * X_HEAD_DIM)
            p = _cross_probs(q, kv_ref, h)
            dp = _dot_nt(do[:, hs], kv_ref[:, vs])
            ds = (p * (dp - jnp.sum(dp * p, axis=-1, keepdims=True))).astype(BF16)
            dqs.append(_dot(ds, kv_ref[:, hs]))
            dkv_ref[:, hs] += _dot_tn(ds, q[:, hs])
            dkv_ref[:, vs] += _dot_tn(p.astype(BF16), do[:, hs])
        dq = (jnp.concatenate(dqs, axis=1) * X_SCALE).astype(BF16)
        dq_ref[...] = dq
        xt = x1_ref[...]
        dx, dg = _rms_bwd(_dot_nt(dq, wq_ref[...]), xt, _rms(xt), g_ref[...])
        dx1_ref[...] = dx2_ref[...] + dx
        dg_ref[...] += dg

    row = lambda w: pl.BlockSpec((tm, w), lambda i: (i, 0))
    full = lambda a, b: pl.BlockSpec((a, b), lambda i: (0, 0))
    return pl.pallas_call(
        _with_dep(body, 7, dep), name="cross_bwd", grid=(t // tm,),
        in_specs=[row(D_MODEL), row(D_MODEL), row(D_MODEL), full(m, 2 * D_MODEL), full(D_MODEL, D_MODEL),
                  full(D_MODEL, D_MODEL), full(1, D_MODEL)] + _dep_spec(dep),
        out_specs=[row(D_MODEL), row(D_MODEL), full(m, 2 * D_MODEL), full(1, D_MODEL)],
        out_shape=[jax.ShapeDtypeStruct((t, D_MODEL), BF16), jax.ShapeDtypeStruct((t, D_MODEL), F32),
                   jax.ShapeDtypeStruct((m, 2 * D_MODEL), F32), jax.ShapeDtypeStruct((1, D_MODEL), F32)],
        compiler_params=_params(("arbitrary",)),
    )(dx2, x1, q, kv, w_cq, w_co, g_cross, *_dep_arg(dep))


def _memkv_bwd(dkv, mn, mem, w_ckv, g_mem):
    ws = w_ckv.shape[2]

    def body(dkv_ref, mn_ref, mem_ref, w_ref, g_ref, dw_ref, dg_ref):
        mn = mn_ref[...]
        dmn = jnp.zeros(mn.shape, F32)
        for j in range(N_DEV):
            dkvb = dkv_ref[:, j * ws:(j + 1) * ws].astype(BF16)
            dw_ref[j] = _dot_tn(mn, dkvb).astype(BF16)
            dmn = dmn + _dot_nt(dkvb, w_ref[j])
        xt = mem_ref[...]
        dg_ref[...] = jnp.sum(dmn * xt * _rms(xt), axis=0, keepdims=True)

    return pl.pallas_call(
        body, name="memkv_bwd",
        out_shape=[jax.ShapeDtypeStruct(w_ckv.shape, BF16), jax.ShapeDtypeStruct((1, D_MODEL), F32)],
        compiler_params=_params(),
    )(dkv, mn, mem, w_ckv, g_mem)


def _merge_bwd(dx1, ya, yb, gts, w_out, w_g, tm):
    t = dx1.shape[0]

    def body(dx1_ref, ya_ref, yb_ref, g_ref, wo_ref, wg_ref, dg_ref, dhp_ref, dya_ref, dyb_ref, db_ref):
        @pl.when(pl.program_id(0) == 0)
        def _():
            db_ref[...] = jnp.zeros_like(db_ref)

        dm = _dot_nt(dx1_ref[...].astype(BF16), wo_ref[...])
        ga = g_ref[:, :D_MODEL].astype(F32)
        gb = g_ref[:, D_MODEL:].astype(F32)
        dya_ref[...] = (dm * ga).astype(BF16)
        dyb_ref[...] = (dm * gb).astype(BF16)
        dpa = dm * ya_ref[...].astype(F32) * ga * (1.0 - ga)
        dpb = dm * yb_ref[...].astype(F32) * gb * (1.0 - gb)
        dpre = jnp.concatenate([dpa, dpb], axis=1)
        db_ref[...] += jnp.sum(dpre, axis=0, keepdims=True)
        dpreb = dpre.astype(BF16)
        dg_ref[...] = dpreb
        dhp_ref[...] = _dot_nt(dpreb, wg_ref[...])

    row = lambda w: pl.BlockSpec((tm, w), lambda i: (i, 0))
    once = lambda a, b: pl.BlockSpec((a, b), lambda i: (0, 0), pipeline_mode=pl.Buffered(1))
    sds = jax.ShapeDtypeStruct
    return pl.pallas_call(
        body, name="merge_bwd", grid=(t // tm,),
        in_specs=[row(D_MODEL), row(D_MODEL), row(D_MODEL), row(GATE_WIDTH),
                  once(D_MODEL, D_MODEL), once(D_MODEL, GATE_WIDTH)],
        out_specs=[row(GATE_WIDTH), row(D_MODEL), row(D_MODEL), row(D_MODEL),
                   pl.BlockSpec((1, GATE_WIDTH), lambda i: (0, 0))],
        out_shape=[sds((t, GATE_WIDTH), BF16), sds((t, D_MODEL), F32), sds((t, D_MODEL), BF16),
                   sds((t, D_MODEL), BF16), sds((1, GATE_WIDTH), F32)],
        compiler_params=_params(("arbitrary",)),
    )(dx1, ya, yb, gts, w_out, w_g)


def _combine_bwd(dya, dyb, oa, ob, l0, l1, l2, lb, sink_row, w_a, w_b, tm):
    t = dya.shape[0]

    def body(dya_ref, dyb_ref, oa_ref, ob_ref, l0_ref, l1_ref, l2_ref, lb_ref, sk_ref, wa_ref, wb_ref,
             do0_ref, do1_ref, do2_ref, c0_ref, c1_ref, c2_ref, dob_ref, cb_ref, dsk_ref, so_ref, sl_ref):
        @pl.when(pl.program_id(0) == 0)
        def _():
            dsk_ref[...] = jnp.zeros_like(dsk_ref)

        doa = _dot_nt(dya_ref[...], wa_ref[...])
        dob = _dot_nt(dyb_ref[...], wb_ref[...])
        dsum = _head_sums(doa * oa_ref[...].astype(F32), _head_gather(256, True))
        a0, a1, a2 = _alphas(l0_ref[...], _interleave(l1_ref, sl_ref), _interleave(l2_ref, sl_ref))
        c0_ref[...] = a0 * dsum
        spread = _dil_head_spread()
        do0_ref[...] = _head_scale(doa, a0, spread).astype(BF16)
        for al, do_ref, c_ref in ((a1, do1_ref, c1_ref), (a2, do2_ref, c2_ref)):
            _deinterleave(al * dsum, sl_ref, c_ref, F32)
            _deinterleave(_head_scale(doa, al, spread), so_ref, do_ref, BF16)
        dob_ref[...] = dob.astype(BF16)
        cb = _head_sums(dob * ob_ref[...], _head_gather(128, False))
        cb_ref[...] = cb
        lane = lax.broadcasted_iota(jnp.int32, cb.shape, 1)
        psink = jnp.where(lane < 8, jnp.exp(sk_ref[...] - lb_ref[...]), 0.0)
        dsk_ref[...] += jnp.sum(-psink * cb, axis=0, keepdims=True)

    row = lambda w: pl.BlockSpec((tm, w), lambda i: (i, 0))
    full = lambda a, b: pl.BlockSpec((a, b), lambda i: (0, 0))
    sds = jax.ShapeDtypeStruct
    d1, d2 = l1.shape[0], l2.shape[0]
    res = lambda d, w: pl.BlockSpec((d, tm // d, w), lambda i: (0, i, 0))
    return pl.pallas_call(
        body, name="combine_bwd", grid=(t // tm,),
        in_specs=[row(D_MODEL), row(D_MODEL), row(512), row(512),
                  row(256), _res_spec(l1, tm), _res_spec(l2, tm), row(128), full(1, 128),
                  full(512, D_MODEL), full(512, D_MODEL)],
        out_specs=[row(512), res(d1, 512), res(d2, 512), row(256), res(d1, 256), res(d2, 256),
                   row(512), row(128), full(1, 128)],
        out_shape=[sds((t, 512), BF16), sds((d1, t // d1, 512), BF16),
                   sds((d2, t // d2, 512), BF16), sds((t, 256), F32), sds((d1, t // d1, 256), F32),
                   sds((d2, t // d2, 256), F32), sds((t, 512), BF16),
                   sds((t, 128), F32), sds((1, 128), F32)],
        scratch_shapes=[_lane_scratch(tm, 512), _lane_scratch(tm, 256)],
        compiler_params=_params(("arbitrary",)),
    )(dya, dyb, oa, ob, l0, l1, l2, lb, sink_row, w_a, w_b)


def _attn_bwd(name, pv, dov, lsev, cv, cosv, sinv, swa, tq, dep=None):
    d, ls = pv.shape[0], pv.shape[1]
    n, nsb = ls // tq, tq // BAND
    pairs = _attn_layout(swa)
    ncol = 1 if swa else 2
    ow = 128 * len(pairs)

    kv_slots = sorted({(ko, vo) for _, ko, vo in pairs})

    def body(cur_ref, tail_ref, do_ref, lse_ref, c_ref, cos_ref, sin_ref, out_ref, acc_ref, carry_ref, acct_ref):
        i = pl.program_id(2)
        blk_i = n - 1 - i
        acc_ref[...] = jnp.zeros_like(acc_ref)
        acct_ref[...] = jnp.zeros_like(acct_ref)

        @pl.when(i == 0)
        def _():
            carry_ref[...] = jnp.zeros_like(carry_ref)

        qk_a, v_a = _head_a_masks(BAND)
        dim = lax.broadcasted_iota(jnp.int32, (128, BAND), 0)
        qk_at, v_at = (dim % HEAD_DIM) < HEAD_DIM // 2, dim < HEAD_DIM
        for s in range(nsb):
            mask = _band_mask(blk_i, s)
            mask2 = jnp.concatenate([mask, mask], axis=0)
            rows = slice(s * BAND, (s + 1) * BAND)
            kcols = slice(s * BAND, (s + 2) * BAND)
            for j, (qo, ko, vo) in enumerate(pairs):
                slot = kv_slots.index((ko, vo))
                kk = _kv_rows(cur_ref, tail_ref, s, ko)
                vv = _kv_rows(cur_ref, tail_ref, s, vo)
                q, do = cur_ref[rows, qo:qo + 128], do_ref[rows, j * 128:(j + 1) * 128]
                q2, do2 = _stack_heads(q, qk_a), _stack_heads(do, v_a)
                col2 = lambda ref: jnp.concatenate([ref[rows, 2 * j:2 * j + 1], ref[rows, 2 * j + 1:2 * j + 2]], axis=0)
                sc = _dot_nt(q2, kk)
                p = jnp.exp(jnp.where(mask2, sc, -jnp.inf) - col2(lse_ref))
                dp = _dot_nt(do2, vv)
                ds = (p * (dp - col2(c_ref))).astype(BF16)
                dq2 = _dot(ds, kk)
                acc_ref[BAND + s * BAND:BAND + (s + 1) * BAND, qo:qo + 128] += jnp.where(qk_a, dq2[:BAND], dq2[BAND:])
                acct_ref[2 * slot, :, kcols] += _dot(_stack_heads_t(q.T, qk_at), ds)
                acct_ref[2 * slot + 1, :, kcols] += _dot(_stack_heads_t(do.T, v_at), p.astype(BF16))
        for slot, (ko, vo) in enumerate(kv_slots):
            acc_ref[:, ko:ko + 128] += acct_ref[2 * slot].T
            acc_ref[:, vo:vo + 128] += acct_ref[2 * slot + 1].T

        last = acc_ref[tq:, :] + carry_ref[...]
        fin = last if tq == BAND else jnp.concatenate([acc_ref[BAND:tq, :], last], axis=0)
        out_ref[...] = _rope(fin, cos_ref[...], sin_ref[...], swa, -1).astype(BF16)
        carry_ref[...] = acc_ref[0:BAND, :]

    rev = lambda i: n - 1 - i
    blk = lambda rows, w, row_of: pl.BlockSpec((None, rows, w), lambda r, cb, i: (r, row_of(i), cb))
    tab = pl.BlockSpec((None, tq, 128), lambda r, cb, i: (r, rev(i), 0))
    return pl.pallas_call(
        _with_dep(body, 7, dep), name=name, grid=(d, ncol, n),
        in_specs=[blk(tq, PBLK, rev), blk(BAND, PBLK, lambda i: jnp.maximum(rev(i) * nsb - 1, 0)),
                  blk(tq, ow, rev), blk(tq, 128, rev), blk(tq, 128, rev), tab, tab] + _dep_spec(dep),
        out_specs=blk(tq, PBLK, rev),
        out_shape=jax.ShapeDtypeStruct((d, ls, ncol * PBLK), BF16),
        scratch_shapes=[pltpu.VMEM((tq + BAND, PBLK), F32), pltpu.VMEM((BAND, PBLK), F32),
                        pltpu.VMEM((2 * len(kv_slots), 128, tq + BAND), F32)],
        compiler_params=_params(("arbitrary", "arbitrary", "arbitrary")),
    )(pv, pv, dov, lsev, cv, cosv, sinv, *_dep_arg(dep))


def _dx(dp0, dp1, dp2, dpb, w_p, dh_part, dx1, x, g_mix, tm, dep=None):
    t = x.shape[0]
    gw = 2 * PBLK

    def body(dp0_ref, dp1_ref, dp2_ref, dpb_ref, w_ref, dhp_ref, dx1_ref, x_ref, g_ref, gx_ref, dg_ref,
             dpt_ref, scr_ref):
        @pl.when(pl.program_id(0) == 0)
        def _():
            dg_ref[...] = jnp.zeros_like(dg_ref)

        dpt_ref[:, 0:gw] = dp0_ref[...]
        dpt_ref[:, gw:2 * gw] = _interleave(dp1_ref, scr_ref).astype(BF16)
        dpt_ref[:, 2 * gw:3 * gw] = _interleave(dp2_ref, scr_ref).astype(BF16)
        dpt_ref[:, 3 * gw:] = dpb_ref[...]
        dh = _dot_nt(dpt_ref[...], w_ref[...]) + dhp_ref[...]
        xt = x_ref[...]
        dx, dg = _rms_bwd(dh, xt, _rms(xt), g_ref[...])
        gx_ref[...] = dx1_ref[...] + dx
        dg_ref[...] += dg

    row = lambda w: pl.BlockSpec((tm, w), lambda i: (i, 0))
    full = lambda a, b: pl.BlockSpec((a, b), lambda i: (0, 0))
    return pl.pallas_call(
        _with_dep(body, 9, dep), name="dx", grid=(t // tm,),
        in_specs=[row(gw), _res_spec(dp1, tm), _res_spec(dp2, tm), row(PBLK),
                  pl.BlockSpec((D_MODEL, P_WIDTH), lambda i: (0, 0), pipeline_mode=pl.Buffered(1)),
                  row(D_MODEL), row(D_MODEL), row(D_MODEL), full(1, D_MODEL)] + _dep_spec(dep),
        out_specs=[row(D_MODEL), full(1, D_MODEL)],
        out_shape=[jax.ShapeDtypeStruct((t, D_MODEL), F32), jax.ShapeDtypeStruct((1, D_MODEL), F32)],
        scratch_shapes=[pltpu.VMEM((tm, P_WIDTH), BF16), _lane_scratch(tm, gw)],
        compiler_params=_params(("arbitrary",)),
    )(dp0, dp1, dp2, dpb, w_p, dh_part, dx1, x, g_mix, *_dep_arg(dep))


MESH = pl.DeviceIdType.MESH
HBM_SPEC = pl.BlockSpec(memory_space=pltpu.HBM)
VMEM_SPEC = pl.BlockSpec(memory_space=pltpu.VMEM)


def _all_gather(xp, act, g, tm):
    t = act.shape[0]
    d1, d2 = DIL_GROUPS[1][1], DIL_GROUPS[2][1]

    def body(x_ref, act_ref, g_ref, out_ref, h_ref, h1_ref, h2_ref, send_sems, recv_sems, local_sem, hf_ref):
        x, y, c = lax.axis_index("x"), lax.axis_index("y"), lax.axis_index("c")
        me, sibling = (x, y, c), (x, y, 1 - c)
        chips = [(1 - x, y), (x, 1 - y), (1 - x, 1 - y)]

        def rows(px, py, pc):
            return out_ref.at[4 * px + 2 * py + pc]

        def copy(k, block, to, src=None):
            return pltpu.make_async_remote_copy(
                src_ref=rows(*block) if src is None else src, dst_ref=rows(*block),
                send_sem=send_sems.at[k], recv_sem=recv_sems.at[k], device_id=to, device_id_type=MESH)

        mine = pltpu.make_async_copy(x_ref, rows(*me), local_sem)
        mine.start()
        first = [copy(0, me, sibling, src=x_ref)]
        first += [copy(1 + j, me, (*chip, c), src=x_ref) for j, chip in enumerate(chips)]
        for cp in first:
            cp.start()

        def norm(a_blk, h_blk, h1_blk, h2_blk):
            xt = a_blk[...]
            hf = xt * _rms(xt) * g_ref[...]
            h_blk[...] = hf.astype(BF16)
            _deinterleave(hf, hf_ref, h1_blk, BF16)
            _deinterleave(hf, hf_ref, h2_blk, BF16)

        res = lambda d: pl.BlockSpec((d, tm // d, D_MODEL), lambda i: (0, i, 0))
        row = pl.BlockSpec((tm, D_MODEL), lambda i: (i, 0))
        pltpu.emit_pipeline(norm, grid=(t // tm,), in_specs=[row], out_specs=[row, res(d1), res(d2)])(
            act_ref, h_ref, h1_ref, h2_ref)

        passed = [copy(4 + j, (*chip, c), sibling) for j, chip in enumerate(chips)]
        for j, chip in enumerate(chips):
            copy(1 + j, (*chip, c), me).wait_recv()
            passed[j].start()
        copy(0, sibling, me).wait_recv()
        for j, chip in enumerate(chips):
            copy(4 + j, (*chip, 1 - c), me).wait_recv()
        for cp in first + passed:
            cp.wait_send()
        mine.wait()

    sds = jax.ShapeDtypeStruct
    return pl.pallas_call(
        body, name="all_gather",
        out_shape=[sds((N_DEV,) + xp.shape, xp.dtype), sds((t, D_MODEL), BF16),
                   sds((d1, t // d1, D_MODEL), BF16), sds((d2, t // d2, D_MODEL), BF16)],
        in_specs=[HBM_SPEC, HBM_SPEC, VMEM_SPEC], out_specs=[HBM_SPEC] * 4,
        scratch_shapes=[pltpu.SemaphoreType.DMA((7,)), pltpu.SemaphoreType.DMA((7,)), pltpu.SemaphoreType.DMA,
                        _lane_scratch(tm, D_MODEL)],
        compiler_params=pltpu.CompilerParams(vmem_limit_bytes=VMEM_LIMIT),
    )(xp, act, g)


def _peers():
    x, y, c = lax.axis_index("x"), lax.axis_index("y"), lax.axis_index("c")
    out = []
    for k in range(1, N_DEV):
        px = 1 - x if k & 4 else x
        py = 1 - y if k & 2 else y
        pc = 1 - c if k & 1 else c
        out.append((k, (px, py, pc), 4 * px + 2 * py + pc))
    return out


def _my_index():
    return 4 * lax.axis_index("x") + 2 * lax.axis_index("y") + lax.axis_index("c")


SEM_SPEC = pl.BlockSpec(memory_space=pltpu.SEMAPHORE)
ANY_SPEC = pl.BlockSpec(memory_space=pl.ANY)
_SPLIT_PARAMS = pltpu.CompilerParams(has_side_effects=pltpu.SideEffectType.DATAFLOW_SIDE_EFFECTING)


def _split_copies(gather, src_refs, land_refs, send_sems, recv_sems):
    me_idx = _my_index()
    out = []
    for a, (src_ref, land_ref) in enumerate(zip(src_refs, land_refs)):
        for k, peer, peer_idx in _peers():
            if gather:
                src, dst = src_ref, land_ref.at[me_idx]
            else:
                src, dst = src_ref.at[peer_idx], land_ref.at[k - 1]
            out.append(pltpu.make_async_remote_copy(
                src_ref=src, dst_ref=dst, send_sem=send_sems.at[7 * a + k - 1], recv_sem=recv_sems.at[7 * a + k - 1],
                device_id=peer, device_id_type=MESH))
    return out


def _split_start(name, gather, srcs):
    n = len(srcs)

    def body(*refs):
        send_sems, recv_sems = refs[n], refs[n + 1]
        for cp in _split_copies(gather, refs[:n], refs[2 * n + 2:3 * n + 2], send_sems, recv_sems):
            cp.start()
        token = refs[-1]
        token[...] = jnp.zeros_like(token)

    lands = [pltpu.HBM((N_DEV,) + a.shape if gather else (N_DEV - 1,) + a.shape[1:], a.dtype) for a in srcs]
    return pl.pallas_call(
        body, name=name,
        out_shape=(pltpu.SemaphoreType.DMA((7 * n,)), pltpu.SemaphoreType.DMA((7 * n,)),
                   *[pltpu.HBM(a.shape, a.dtype) for a in srcs], *lands, jax.ShapeDtypeStruct((8, 128), F32)),
        in_specs=(HBM_SPEC,) * n, out_specs=(SEM_SPEC, SEM_SPEC) + (HBM_SPEC,) * (2 * n) + (VMEM_SPEC,),
        input_output_aliases={i: 2 + i for i in range(n)}, compiler_params=_SPLIT_PARAMS,
    )(*[pltpu.with_memory_space_constraint(a, pltpu.HBM) for a in srcs])


def _split_wait(name, gather, started, after):
    send_sems, recv_sems, bufs = started[0], started[1], started[2:-1]
    n = len(bufs) // 2

    def body(*refs):
        for cp in _split_copies(gather, refs[:n], refs[n:2 * n], refs[2 * n], refs[2 * n + 1]):
            cp.wait_send()
            cp.wait_recv()

    out = pl.pallas_call(
        body, name=name, out_shape=tuple(pltpu.HBM(a.shape, a.dtype) for a in bufs),
        in_specs=(HBM_SPEC,) * (2 * n) + (SEM_SPEC, SEM_SPEC, ANY_SPEC), out_specs=(HBM_SPEC,) * (2 * n),
        input_output_aliases={i: i for i in range(2 * n)}, compiler_params=_SPLIT_PARAMS,
    )(*bufs, send_sems, recv_sems, after)
    return out[:n], out[n:]


def _adam_update(g, w, m, v):
    nm = ADAM_B1 * m + (1.0 - ADAM_B1) * g
    nv = ADAM_B2 * v + (1.0 - ADAM_B2) * (g * g)
    m_hat = nm / (1.0 - ADAM_B1 ** ADAM_STEP)
    v_hat = nv / (1.0 - ADAM_B2 ** ADAM_STEP)
    return -ADAM_LR * (m_hat / (jnp.sqrt(v_hat) + ADAM_EPS) + ADAM_WD * w), nm, nv


def _adamw(name, me, sent, got, w, m, v, tr, tc=None):
    r, c = w.shape
    tc = c if tc is None else tc

    def body(me_ref, own_ref, got_ref, w_ref, m_ref, v_ref, g_ref, d_ref, nm_ref, nv_ref):
        g = own_ref[...].astype(F32)
        for k in range(N_DEV - 1):
            g = g + got_ref[k].astype(F32)
        g_ref[...] = g
        d_ref[...], nm_ref[...], nv_ref[...] = _adam_update(g, w_ref[...], m_ref[...], v_ref[...])

    blk = pl.BlockSpec((tr, tc), lambda i, j, me_ref: (i, j))
    return pl.pallas_call(
        body, name=name,
        grid_spec=pltpu.PrefetchScalarGridSpec(
            num_scalar_prefetch=1, grid=(r // tr, c // tc),
            in_specs=[pl.BlockSpec((None, tr, tc), lambda i, j, me_ref: (me_ref[0], i, j)),
                      pl.BlockSpec((N_DEV - 1, tr, tc), lambda i, j, me_ref: (0, i, j)), blk, blk, blk],
            out_specs=[blk] * 4),
        out_shape=[jax.ShapeDtypeStruct((r, c), F32)] * 4,
        compiler_params=_params(("arbitrary", "arbitrary")),
    )(me, sent, got, w, m, v)


def _adamw_small(srecv, ws, ms, vs):
    nv_ = len(ws)

    def body(*refs):
        s_ref = refs[0]
        ins, outs = refs[1:1 + 3 * nv_], refs[1 + 3 * nv_:]
        g_all = s_ref[0]
        for k in range(1, N_DEV):
            g_all = g_all + s_ref[k]
        for i in range(nv_):
            n = ins[i].shape[1]
            g = g_all[i:i + 1, :n]
            d, nm, nv = _adam_update(g, ins[i][...], ins[nv_ + i][...], ins[2 * nv_ + i][...])
            outs[i][...], outs[nv_ + i][...], outs[2 * nv_ + i][...], outs[3 * nv_ + i][...] = g, d, nm, nv
        outs[-1][...] = g_all[nv_:nv_ + 1, :128]

    shapes = [jax.ShapeDtypeStruct(a.shape, F32) for a in ws]
    res = pl.pallas_call(body, name="adamw_small", out_shape=shapes * 4 + [jax.ShapeDtypeStruct((1, 128), F32)],
                         compiler_params=_params())(srecv, *ws, *ms, *vs)
    return [res[k * nv_:(k + 1) * nv_] for k in range(4)], res[-1]


def _cols_from_shards(a):
    return jnp.swapaxes(a, 0, 1).reshape(a.shape[1], a.shape[0] * a.shape[2])


def _shards_from_cols(a):
    return jnp.swapaxes(a.reshape(a.shape[0], N_DEV, a.shape[1] // N_DEV), 0, 1)


def _shards_from_rows(a):
    return a.reshape(N_DEV, a.shape[0] // N_DEV, a.shape[1])


def _pair_lanes(a):
    lead = a.shape[:-1]
    return a.reshape(lead + (2, 2, HEAD_DIM // 2)).swapaxes(-3, -2).reshape(lead + (128,))


def _split_w_in(w_in):
    rows = w_in.shape[0]
    dil = w_in[:, :3 * DIL_WIDTH].reshape(rows, 3, 3, 4, 128)
    dil = np.concatenate([_pair_lanes(dil[:, :2]), dil[:, 2:]], axis=1)
    dil = dil.transpose(0, 2, 3, 1, 4).reshape(rows, 3 * DIL_WIDTH)
    o = 3 * DIL_WIDTH
    qb = w_in[:, o:o + SWA_Q_WIDTH].reshape(rows, 2, 4, HEAD_DIM).transpose(0, 2, 1, 3).reshape(rows, 4, 128)
    qb = _pair_lanes(qb).reshape(rows, SWA_Q_WIDTH)
    kb = _pair_lanes(w_in[:, o + SWA_Q_WIDTH:o + SWA_Q_WIDTH + SWA_KV_WIDTH])
    vb = w_in[:, o + SWA_Q_WIDTH + SWA_KV_WIDTH:P_WIDTH]
    return np.concatenate([dil, qb, kb, vb], axis=1)


ROW_GRANULE = HEAD_DIM // 2


def _w_p_granules():
    order = _split_w_in(np.arange(IN_WIDTH)[None])[0]
    assert sorted(order.tolist()) == list(range(P_WIDTH))
    first = order[::ROW_GRANULE]
    assert (first % ROW_GRANULE == 0).all() and (order.reshape(-1, ROW_GRANULE) == first[:, None] + np.arange(ROW_GRANULE)).all()
    return first // ROW_GRANULE


def _gather_rows(name, src, granules, per_step=8):
    n, cols = len(granules), src.shape[1]
    assert n % per_step == 0

    def body(tab_ref, *refs):
        out_ref = refs[per_step]
        for j in range(per_step):
            out_ref[j * ROW_GRANULE:(j + 1) * ROW_GRANULE, :] = refs[j][...]

    def pick(j):
        return pl.BlockSpec((ROW_GRANULE, cols), lambda i, tab_ref: (tab_ref[per_step * i + j], 0))

    return pl.pallas_call(
        body, name=name,
        grid_spec=pltpu.PrefetchScalarGridSpec(
            num_scalar_prefetch=1, grid=(n // per_step,),
            in_specs=[pick(j) for j in range(per_step)],
            out_specs=pl.BlockSpec((per_step * ROW_GRANULE, cols), lambda i, tab_ref: (i, 0))),
        out_shape=jax.ShapeDtypeStruct((n * ROW_GRANULE, cols), src.dtype),
        compiler_params=_params(("arbitrary",)),
    )(jnp.asarray(granules, jnp.int32), *([src] * per_step))


def _swa_rows(w_b):
    return w_b.reshape(2, 4, HEAD_DIM, -1).transpose(1, 0, 2, 3).reshape(SWA_Q_WIDTH, -1)


def _swa_rows_inv(dw_b):
    return dw_b.reshape(4, 2, HEAD_DIM, -1).transpose(1, 0, 2, 3).reshape(SWA_Q_WIDTH, -1)


def _rope_tables(pos):
    half = HEAD_DIM // 2
    inv = ROPE_THETA ** (-jnp.arange(half, dtype=F32) / half)
    ang = pos.astype(F32)[:, None] * jnp.tile(inv, 4)
    sign = jnp.repeat(jnp.array([-1.0, 1.0], F32), 2 * half)
    return jnp.cos(ang), jnp.sin(ang) * sign


def _local_step(x, hs, mem, pos, target, w_in_t, dep, rest_weights, on_grads, g_mix, g_cross, g_mem, g_mlp, g_final, sink):
    t = x.shape[0]
    tm = min(512, t)
    tq = 1024
    tw = min(2048, t)
    w_p = jnp.swapaxes(_gather_rows("w_p_rows", w_in_t, _w_p_granules()), 0, 1)
    cos, sin = lax.optimization_barrier(_rope_tables(pos))
    sink_row = jnp.pad(sink.reshape(2, 4).T.reshape(1, 8), ((0, 0), (0, 120)))
    tabs = [(cos[None], sin[None])]
    for _, d in DIL_GROUPS[1:]:
        tabs.append(tuple(a.reshape(t // d, d, 128).swapaxes(0, 1) for a in (cos, sin)))
    tabs.append(tabs[0])

    h, h1, h2 = hs
    p0, p1, p2, pb = _inproj(h, h1, h2, w_p, [(cos, sin), tabs[1], tabs[2]], tm, dep)
    ps = [p0[None], p1, p2, pb[None]]
    outs, lses = [], []
    for gi, pv in enumerate(ps):
        res = _attn_fwd(f"attn_fwd{gi}", pv, gi == 3, sink_row[0, :8], min(tq, pv.shape[1]))
        outs.append(res[0])
        lses.append(res[1])
    o0, l0, ob, lb, ob32 = outs[0][0], lses[0][0], outs[3][0], lses[3][0], res[2][0]
    wts = rest_weights(lb)
    w_b = _swa_rows(wts["w_branch_b"])
    tf = 2048
    w_g = wts["w_g"]
    gts = _gates(h, w_g, wts["b_gate"].reshape(1, GATE_WIDTH), min(1024, t), 1024)
    oa, ya, yb, merged, x1, hc = _mix(o0, outs[1], outs[2], l0, lses[1], lses[2], ob, gts, x,
                                      wts["w_branch_a"], w_b, wts["w_out"], g_cross, tm)
    mn, kv = _memkv(mem, g_mem, wts["w_ckv"])
    q, o, x2, hm = _cross(hc, x1, kv, wts["w_cq"], wts["w_co"], g_mlp, tm)
    a, dx3, loss, dg_final = _mlp(hm, x2, wts["w_1"], wts["w_2"], g_final.reshape(1, D_MODEL), target, tm, tf)

    grads = {}
    dz, dx2, dg_mlp = _mlp_bwd(dx3, a, wts["w_1"], wts["w_2"], x2, g_mlp, tm, tf)
    grads["w_2"] = _shards_from_rows(_wgrad("dw_2", a, dx3, 1024, 1024, tw, square=True))
    grads["w_1"] = _wgrad("dw_1", hm, dz, 1024, 1024, tw, col_shards=True)
    dep = on_grads(GROUP_A, grads)
    dq, dx1, dkv, dg_cross = _cross_bwd(dx2, x1, q, kv, wts["w_cq"], wts["w_co"], g_cross, tm, dep)
    grads["w_co"] = _shards_from_rows(_wgrad("dw_co", o, dx2, 1024, 1024, tw))
    grads["w_cq"] = _shards_from_rows(_wgrad("dw_cq", hc, dq, 1024, 1024, tw))
    grads["w_ckv"], dg_mem = _memkv_bwd(dkv, mn, mem, wts["w_ckv"], g_mem)
    dgt, dh_part, dya, dyb, db_gate = _merge_bwd(dx1, ya, yb, gts, wts["w_out"], w_g, tm)
    do0, do1, do2, c0, c1, c2, dob, cb, dsink = _combine_bwd(
        dya, dyb, oa, ob32, l0, lses[1], lses[2], lb, sink_row, wts["w_branch_a"], w_b, tm)
    grads["w_out"] = _shards_from_rows(_wgrad("dw_out", merged, dx1, 1024, 1024, tw))
    grads["w_branch_a"] = _shards_from_cols(_wgrad("dw_a", oa, dya, 512, 1024, tw))
    grads["w_branch_b"] = _shards_from_cols(_swa_rows_inv(_wgrad("dw_b", ob, dyb, 512, 1024, tw)))
    grads["b_gate"] = _shards_from_cols(db_gate.reshape(2, D_MODEL)).astype(BF16)
    dep = on_grads(GROUP_B, grads)
    dw_g_t = _wgrad("dw_g", dgt, h, 1024, 1024, tw)
    dps = []
    for gi, (pv, do_g, c_g) in enumerate(zip(ps, (do0[None], do1, do2, dob[None]), (c0[None], c1, c2, cb[None]))):
        dps.append(_attn_bwd(f"attn_bwd{gi}", pv, do_g, lses[gi], c_g, tabs[gi][0], tabs[gi][1], gi == 3,
                             min(tq, pv.shape[1]),
                             dep if gi == 0 else None))
    dw_p_t = [_wgrad(f"dw_p{gi}", dpg.reshape(t, -1), hh.reshape(t, D_MODEL), PBLK, 1024, tw)
              for gi, (hh, dpg) in enumerate(zip((h, h1, h2, h), dps))]
    back = np.concatenate([np.argsort(_w_p_granules()),
                           np.arange(P_WIDTH // ROW_GRANULE, IN_WIDTH // ROW_GRANULE)])
    dw_in_t = _gather_rows("dw_in_rows", jnp.concatenate(dw_p_t + [dw_g_t], axis=0), back)
    grads["w_in"] = dw_in_t.reshape(N_DEV, IN_WIDTH // N_DEV, D_MODEL)
    dep = on_grads(GROUP_C, grads)
    grad_x, dg_mix = _dx(dps[0][0], dps[1], dps[2], dps[3][0], w_p, dh_part, dx1, x, g_mix, tm, dep)
    dsink_heads = dsink[0, :8].reshape(4, 2).T.reshape(8)
    small = {"g_mix": dg_mix[0], "g_cross": dg_cross[0], "g_mem": dg_mem[0], "g_mlp": dg_mlp[0],
             "g_final": dg_final[0], "sink": dsink_heads}
    return loss[0, 0], grad_x, small


def kernel(x, mem, positions, g_mix, w_in, b_gate, sink, w_branch_a, w_branch_b, w_out, g_cross, g_mem, w_cq, w_ckv, w_co, g_mlp, w_1, w_2, g_final, loss_target, m_g_mix, m_w_in, m_b_gate, m_sink, m_w_branch_a, m_w_branch_b, m_w_out, m_g_cross, m_g_mem, m_w_cq, m_w_ckv, m_w_co, m_g_mlp, m_w_1, m_w_2, m_g_final, v_g_mix, v_w_in, v_b_gate, v_sink, v_w_branch_a, v_w_branch_b, v_w_out, v_g_cross, v_g_mem, v_w_cq, v_w_ckv, v_w_co, v_g_mlp, v_w_1, v_w_2, v_g_final):
    local = dict(locals())
    shard = {n: local[n][0] for n in GROUP_A + GROUP_B + GROUP_C}
    me = _my_index()
    me_arr = me.reshape(1).astype(jnp.int32)
    tags = {GROUP_A: "a", GROUP_B: "b", GROUP_C: "c"}

    transposed = lambda a: jnp.swapaxes(a, 0, 1)
    gathered_w_in, *hs = _all_gather(transposed(shard["w_in"]).astype(BF16), x[0], g_mix, min(512, x.shape[1]))
    w_in_t = gathered_w_in.reshape(-1, gathered_w_in.shape[2])
    rest = GROUP_A + GROUP_B

    def gathered(name, started, after):
        srcs, lands = _split_wait(name, True, started, after)
        return [lax.dynamic_update_slice(land, src[None], (me,) + (0,) * src.ndim) for src, land in zip(srcs, lands)]

    gather = _split_start("gather_start", True,
                          [shard[n] if n == "b_gate" else shard[n].astype(BF16) for n in rest])

    def rest_weights(after):
        full = {"w_g": transposed(w_in_t[P_WIDTH:])}
        for name, a in zip(rest, gathered("gather_wait", gather, after)):
            if name in ("w_1", "w_ckv"):
                full[name] = a
            elif name in _COL_SHARDED:
                full[name] = _cols_from_shards(a)
            else:
                full[name] = a.reshape(N_DEV * a.shape[1], a.shape[2])
        return full

    scatters = {}

    def on_grads(names, grads):
        scatters[names] = _split_start("scatter_start_" + tags[names], False, [grads[n] for n in names])
        return scatters[names][-1]

    loss, grad_x, small = _local_step(
        x[0], hs, mem[0], positions[0], loss_target[0], w_in_t, gather[-1], rest_weights, on_grads,
        g_mix, g_cross, g_mem, g_mlp, g_final, sink[0])

    sp = jnp.stack([small[n] if n != "sink" else jnp.pad(small[n], (0, LANES - 8)) for n in SMALL]
                   + [jnp.pad(loss.reshape(1), (0, LANES - 1)), jnp.zeros((LANES,), F32)])
    small_gather = _split_start("small_start", True, [sp])

    after, updated = small_gather[-1], {}
    for names in (GROUP_A, GROUP_B, GROUP_C):
        sent, got = _split_wait("scatter_wait_" + tags[names], False, scatters[names], after)
        for i, name in enumerate(names):
            view = transposed if name == "w_in" else (lambda a: a)
            outs = _adamw("adamw_" + name, me_arr, sent[i], got[i], view(shard[name]),
                          view(local["m_" + name][0]), view(local["v_" + name][0]), ADAM_ROWS[name], ADAM_COLS.get(name))
            updated[name] = [view(a)[None] for a in outs]
            after = outs[3]

    flat = lambda prefix: [local[prefix + n].reshape(1, -1) for n in SMALL]
    outs, loss_row = _adamw_small(gathered("small_wait", small_gather, after)[0], flat(""), flat("m_"), flat("v_"))
    for i, name in enumerate(SMALL):
        updated[name] = [outs[which][i].reshape(local[name].shape) for which in range(4)]

    order = ["g_mix", "w_in", "b_gate", "sink", "w_branch_a", "w_branch_b", "w_out", "g_cross", "g_mem", "w_cq",
             "w_ckv", "w_co", "g_mlp", "w_1", "w_2", "g_final"]
    res = [loss_row[0, 0], grad_x[None]]
    for which in range(4):
        res += [updated[n][which] for n in order]
    return tuple(res)
```

```python
import functools
import math

import jax
import jax.numpy as jnp
import numpy as np
from jax import lax
from jax.experimental import pallas as pl
from jax.experimental.pallas import tpu as pltpu

F32 = jnp.float32
BF16 = jnp.bfloat16

D_MODEL = 1024
HEAD_DIM = 64
DIL_GROUPS = ((128, 1), (512, 4), (2048, 16))
ROPE_THETA = 10000.0
X_HEADS = 4
X_HEAD_DIM = D_MODEL // X_HEADS
D_FF = 4 * D_MODEL
EPS = 1e-6
DIL_WIDTH = 1536
SWA_Q_WIDTH = 512
SWA_KV_WIDTH = 128
P_WIDTH = 3 * DIL_WIDTH + SWA_Q_WIDTH + 2 * SWA_KV_WIDTH
GATE_WIDTH = 2 * D_MODEL
IN_WIDTH = P_WIDTH + GATE_WIDTH
BAND = 128
PBLK = 768
Q_SCALE = HEAD_DIM ** -0.5
X_SCALE = X_HEAD_DIM ** -0.5

ADAM_LR = 0.001
ADAM_B1 = 0.9
ADAM_B2 = 0.999
ADAM_EPS = 1e-08
ADAM_WD = 0.01
ADAM_STEP = 10

N_DEV = 8
LANES = 1024
VMEM_LIMIT = 52 * 1024 * 1024

NT = (((1,), (1,)), ((), ()))
TN = (((0,), (0,)), ((), ()))

GROUP_A = ("w_1", "w_2")
GROUP_B = ("w_branch_a", "w_branch_b", "w_out", "w_cq", "w_ckv", "w_co", "b_gate")
GROUP_C = ("w_in",)
_COL_SHARDED = ("w_in", "w_branch_a", "w_branch_b", "w_ckv", "w_1", "b_gate")
ADAM_ROWS = {"w_in": 464, "w_branch_a": 512, "w_branch_b": 512, "w_out": 128, "w_cq": 128, "w_ckv": 512,
             "w_co": 128, "w_1": 256, "w_2": 256, "b_gate": 2}
ADAM_COLS = {"w_in": 256}
SMALL = ("g_mix", "g_cross", "g_mem", "g_mlp", "g_final", "sink")


def _params(sem=None):
    return pltpu.CompilerParams(dimension_semantics=sem, vmem_limit_bytes=VMEM_LIMIT)


def _dot(a, b):
    return jnp.dot(a, b, preferred_element_type=F32)


def _dot_nt(a, b):
    return lax.dot_general(a, b, NT, preferred_element_type=F32)


def _dot_tn(a, b):
    return lax.dot_general(a, b, TN, preferred_element_type=F32)


def _rms(xt):
    return lax.rsqrt(jnp.mean(xt * xt, axis=-1, keepdims=True) + EPS)


def _rms_bwd(dh, xt, r, g):
    xn = xt * r
    dxn = dh * g
    dx = r * (dxn - xn * jnp.mean(dxn * xn, axis=-1, keepdims=True))
    return dx, jnp.sum(dh * xn, axis=0, keepdims=True)


def _rope(x, c, s, swa, sign):
    kinds = "qqqqkv" if swa else "qkvqkv"
    cq, sq = c * Q_SCALE, s * (sign * Q_SCALE)
    sk = s * sign if sign != 1 else s
    out = []
    for ci, kind in enumerate(kinds):
        xc = x[:, ci * 128:(ci + 1) * 128]
        if kind == "v":
            out.append(xc)
        elif kind == "q":
            out.append(xc * cq + pltpu.roll(xc, 64, 1) * sq)
        else:
            out.append(xc * c + pltpu.roll(xc, 64, 1) * sk)
    return jnp.concatenate(out, axis=1)


def _lane_scratch(rows, w):
    return pltpu.VMEM((w // 128, rows, 128), F32)


def _deinterleave(val, scr_ref, dst_ref, dtype):
    d, n = dst_ref.shape[0], dst_ref.shape[1]
    nc = val.shape[1] // 128
    for c in range(nc):
        scr_ref[c] = val[:, c * 128:(c + 1) * 128]
    for r in range(d):
        rows = [scr_ref.at[c][pl.ds(r, n, stride=d), :] for c in range(nc)]
        dst_ref[r] = jnp.concatenate(rows, axis=1).astype(dtype)


def _res_spec(a, tm):
    d, w = a.shape[0], a.shape[2]
    return pl.BlockSpec((d, tm // d, w), lambda i: (0, i, 0))


def _interleave(src_ref, scr_ref):
    d, n = src_ref.shape[0], src_ref.shape[1]
    nc = src_ref.shape[2] // 128
    for r in range(d):
        v = src_ref[r].astype(F32)
        for c in range(nc):
            scr_ref.at[c][pl.ds(r, n, stride=d), :] = v[:, c * 128:(c + 1) * 128]
    return jnp.concatenate([scr_ref[c] for c in range(nc)], axis=1)


def _with_dep(body, n_in, dep):
    if dep is None:
        return body
    return lambda *refs: body(*refs[:n_in], *refs[n_in + 1:])


def _dep_spec(dep):
    return [] if dep is None else [pl.BlockSpec(memory_space=pl.ANY)]


def _dep_arg(dep):
    return [] if dep is None else [dep]


def _inproj(h, h1, h2, w_p, tabs, tm, dep=None):
    t = h.shape[0]
    gw = 2 * PBLK
    (cos, sin), (cos1, sin1), (cos2, sin2) = tabs[0], tabs[1], tabs[2]

    def body(h_ref, h1_ref, h2_ref, w_ref, c_ref, s_ref, c1_ref, s1_ref, c2_ref, s2_ref,
             p0_ref, p1_ref, p2_ref, pb_ref):
        rows = lambda ref: ref[...].reshape(tm, ref.shape[-1])
        groups = ((h_ref, c_ref, s_ref, p0_ref), (h1_ref, c1_ref, s1_ref, p1_ref), (h2_ref, c2_ref, s2_ref, p2_ref))
        for gi, (lhs_ref, cc_ref, ss_ref, out_ref) in enumerate(groups):
            lhs, cc, ss = rows(lhs_ref), rows(cc_ref), rows(ss_ref)
            for half in range(2):
                col = gi * gw + half * PBLK
                val = _rope(_dot(lhs, w_ref[:, col:col + PBLK]), cc, ss, False, 1).astype(BF16)
                if out_ref.ndim == 3:
                    out_ref[:, :, half * PBLK:(half + 1) * PBLK] = val.reshape(out_ref.shape[:2] + (PBLK,))
                else:
                    out_ref[:, half * PBLK:(half + 1) * PBLK] = val
        pb_ref[...] = _rope(_dot(h_ref[...], w_ref[:, 3 * gw:]), c_ref[...], s_ref[...], True, 1).astype(BF16)

    d1, d2 = DIL_GROUPS[1][1], DIL_GROUPS[2][1]
    row = lambda w: pl.BlockSpec((tm, w), lambda i: (i, 0))
    res = lambda d, w: pl.BlockSpec((d, tm // d, w), lambda i: (0, i, 0))
    sds = jax.ShapeDtypeStruct
    return pl.pallas_call(
        _with_dep(body, 10, dep), name="inproj", grid=(t // tm,),
        in_specs=[row(D_MODEL), res(d1, D_MODEL), res(d2, D_MODEL),
                  pl.BlockSpec((D_MODEL, P_WIDTH), lambda i: (0, 0), pipeline_mode=pl.Buffered(1)),
                  row(128), row(128), res(d1, 128), res(d1, 128), res(d2, 128), res(d2, 128)] + _dep_spec(dep),
        out_specs=[row(gw), res(d1, gw), res(d2, gw), row(PBLK)],
        out_shape=[sds((t, gw), BF16), sds((d1, t // d1, gw), BF16), sds((d2, t // d2, gw), BF16),
                   sds((t, PBLK), BF16)],
        compiler_params=_params(("arbitrary",)),
    )(h, h1, h2, w_p, cos, sin, cos1, sin1, cos2, sin2, *_dep_arg(dep))


def _gates(h, w_g, b, tm, tn):
    t = h.shape[0]

    def body(h_ref, w_ref, b_ref, o_ref):
        z = _dot(h_ref[...], w_ref[...]) + b_ref[...]
        o_ref[...] = (0.5 * jnp.tanh(0.5 * z) + 0.5).astype(BF16)

    return pl.pallas_call(
        body, name="gates", grid=(t // tm, GATE_WIDTH // tn),
        in_specs=[pl.BlockSpec((tm, D_MODEL), lambda i, j: (i, 0)),
                  pl.BlockSpec((D_MODEL, tn), lambda i, j: (0, j)),
                  pl.BlockSpec((1, tn), lambda i, j: (0, j))],
        out_specs=pl.BlockSpec((tm, tn), lambda i, j: (i, j)),
        out_shape=jax.ShapeDtypeStruct((t, GATE_WIDTH), BF16),
        compiler_params=_params(("arbitrary", "arbitrary")),
    )(h, w_g, b)


def _band_mask(i, s):
    row = lax.broadcasted_iota(jnp.int32, (BAND, 2 * BAND), 0)
    col = lax.broadcasted_iota(jnp.int32, (BAND, 2 * BAND), 1)
    band = (col >= row) & (col <= row + BAND)
    if s == 0:
        band = band & ((col >= BAND) | (i > 0))
    return band


def _head_a_masks(rows):
    lane = lax.broadcasted_iota(jnp.int32, (rows, 128), 1)
    return (lane % HEAD_DIM) < HEAD_DIM // 2, lane < HEAD_DIM


def _stack_heads(x, head_a):
    zero = jnp.zeros_like(x)
    return jnp.concatenate([jnp.where(head_a, x, zero), jnp.where(head_a, zero, x)], axis=0)


def _stack_heads_t(xt, head_a_t):
    zero = jnp.zeros_like(xt)
    return jnp.concatenate([jnp.where(head_a_t, xt, zero), jnp.where(head_a_t, zero, xt)], axis=1)


def _kv_rows(cur_ref, tail_ref, s, off):
    if s == 0:
        return jnp.concatenate([tail_ref[:, off:off + 128], cur_ref[0:BAND, off:off + 128]], axis=0)
    return cur_ref[(s - 1) * BAND:(s + 1) * BAND, off:off + 128]


def _attn_layout(swa):
    if swa:
        return [(128 * j, 512, 640) for j in range(4)]
    return [(0, 128, 256), (384, 512, 640)]


def _attn_fwd(name, pv, swa, sinks, tq):
    d, ls = pv.shape[0], pv.shape[1]
    n, nsb = ls // tq, tq // BAND
    pairs = _attn_layout(swa)
    ncol = 1 if swa else 2
    ow = 128 * len(pairs)

    def body(cur_ref, tail_ref, *rest):
        sink_ref, o_ref, lse_ref, o32_ref = rest if swa else (None,) + rest + (None,)
        i = pl.program_id(2)
        lane = lax.broadcasted_iota(jnp.int32, (BAND, 128), 1)
        qk_a, v_a = _head_a_masks(BAND)
        first = lax.broadcasted_iota(jnp.int32, (2 * BAND, 1), 0) < BAND
        for s in range(nsb):
            mask = _band_mask(i, s)
            mask2 = jnp.concatenate([mask, mask], axis=0)
            rows = slice(s * BAND, (s + 1) * BAND)
            lse_tile = jnp.zeros((BAND, 128), F32)
            for j, (qo, ko, vo) in enumerate(pairs):
                q = cur_ref[rows, qo:qo + 128]
                kk = _kv_rows(cur_ref, tail_ref, s, ko)
                vv = _kv_rows(cur_ref, tail_ref, s, vo)
                sc = _dot_nt(_stack_heads(q, qk_a), kk)
                sc = jnp.where(mask2, sc, -jnp.inf)
                m = jnp.max(sc, axis=-1, keepdims=True)
                if swa:
                    sk = jnp.where(first, sink_ref[2 * j], sink_ref[2 * j + 1])
                    m = jnp.maximum(m, sk)
                p = jnp.exp(sc - m)
                den = jnp.sum(p, axis=-1, keepdims=True)
                if swa:
                    den = den + jnp.exp(sk - m)
                lse = m + jnp.log(den)
                lse_tile = jnp.where(lane == 2 * j, lse[:BAND], jnp.where(lane == 2 * j + 1, lse[BAND:], lse_tile))
                o2 = _dot(p.astype(BF16), vv) * (1.0 / den)
                o = jnp.where(v_a, o2[:BAND], o2[BAND:])
                o_ref[rows, j * 128:(j + 1) * 128] = o.astype(BF16)
                if swa:
                    o32_ref[rows, j * 128:(j + 1) * 128] = o
            lse_ref[rows, :] = lse_tile

    in_specs = [pl.BlockSpec((None, tq, PBLK), lambda r, cb, i: (r, i, cb)),
                pl.BlockSpec((None, BAND, PBLK), lambda r, cb, i: (r, jnp.maximum(i * nsb - 1, 0), cb))]
    args = [pv, pv]
    out_specs = [pl.BlockSpec((None, tq, ow), lambda r, cb, i: (r, i, cb)),
                 pl.BlockSpec((None, tq, 128), lambda r, cb, i: (r, i, cb))]
    out_shape = [jax.ShapeDtypeStruct((d, ls, 512), BF16), jax.ShapeDtypeStruct((d, ls, 128 * ncol), F32)]
    if swa:
        in_specs.append(pl.BlockSpec(memory_space=pltpu.SMEM))
        args.append(sinks)
        out_specs.append(out_specs[0])
        out_shape.append(jax.ShapeDtypeStruct((d, ls, 512), F32))
    return pl.pallas_call(
        body, name=name, grid=(d, ncol, n),
        in_specs=in_specs, out_specs=out_specs, out_shape=out_shape,
        compiler_params=_params(("arbitrary", "arbitrary", "arbitrary")),
    )(*args)


def _lse_lane(head):
    return (head // 4) * 128 + head % 4


def _dil_head_spread():
    lane = lax.broadcasted_iota(jnp.int32, (256, 512), 0)
    head = lax.broadcasted_iota(jnp.int32, (256, 512), 1) // HEAD_DIM
    return (lane == _lse_lane(head)).astype(BF16)


def _head_scale(x, tile, spread):
    return x * _dot(tile.astype(BF16), spread)


def _head_gather(width, dil):
    head = lax.broadcasted_iota(jnp.int32, (8 * HEAD_DIM, width), 0) // HEAD_DIM
    lane = lax.broadcasted_iota(jnp.int32, (8 * HEAD_DIM, width), 1)
    return (lane == (_lse_lane(head) if dil else head)).astype(BF16)


def _head_sums(x, gather):
    hi = x.astype(BF16)
    lo = (x - hi.astype(F32)).astype(BF16)
    return _dot(hi, gather) + _dot(lo, gather)


def _alphas(l0, l1, l2):
    m = jnp.maximum(jnp.maximum(l0, l1), l2)
    e0, e1, e2 = jnp.exp(l0 - m), jnp.exp(l1 - m), jnp.exp(l2 - m)
    den = e0 + e1 + e2
    return e0 / den, e1 / den, e2 / den


def _mix(o0, o1, o2, l0, l1, l2, ob, gts, x, w_a, w_b, w_out, g_cross, tm):
    t = x.shape[0]

    def body(o0_ref, o1_ref, o2_ref, l0_ref, l1_ref, l2_ref, ob_ref, g_ref, x_ref, wa_ref, wb_ref, wo_ref,
             gc_ref, oa_ref, ya_ref, yb_ref, mg_ref, x1_ref, hc_ref, so_ref, sl_ref):
        a0, a1, a2 = _alphas(l0_ref[...], _interleave(l1_ref, sl_ref), _interleave(l2_ref, sl_ref))
        spread = _dil_head_spread()
        oa = (_head_scale(o0_ref[...].astype(F32), a0, spread)
              + _head_scale(_interleave(o1_ref, so_ref), a1, spread)
              + _head_scale(_interleave(o2_ref, so_ref), a2, spread))
        oab = oa.astype(BF16)
        oa_ref[...] = oab
        ya = _dot(oab, wa_ref[...])
        yb = _dot(ob_ref[...], wb_ref[...])
        ya_ref[...] = ya.astype(BF16)
        yb_ref[...] = yb.astype(BF16)
        merged = (g_ref[:, :D_MODEL].astype(F32) * ya + g_ref[:, D_MODEL:].astype(F32) * yb).astype(BF16)
        mg_ref[...] = merged
        x1 = x_ref[...] + _dot(merged, wo_ref[...])
        x1_ref[...] = x1
        hc_ref[...] = (x1 * _rms(x1) * gc_ref[...]).astype(BF16)

    row = lambda w: pl.BlockSpec((tm, w), lambda i: (i, 0))
    full = lambda a, b: pl.BlockSpec((a, b), lambda i: (0, 0))
    return pl.pallas_call(
        body, name="mix", grid=(t // tm,),
        in_specs=[row(512), _res_spec(o1, tm), _res_spec(o2, tm), row(256), _res_spec(l1, tm), _res_spec(l2, tm),
                  row(512), row(GATE_WIDTH),
                  row(D_MODEL), full(512, D_MODEL), full(512, D_MODEL), full(D_MODEL, D_MODEL), full(1, D_MODEL)],
        out_specs=[row(512), row(D_MODEL), row(D_MODEL), row(D_MODEL), row(D_MODEL), row(D_MODEL)],
        out_shape=[jax.ShapeDtypeStruct((t, 512), BF16), jax.ShapeDtypeStruct((t, D_MODEL), BF16),
                   jax.ShapeDtypeStruct((t, D_MODEL), BF16), jax.ShapeDtypeStruct((t, D_MODEL), BF16),
                   jax.ShapeDtypeStruct((t, D_MODEL), F32), jax.ShapeDtypeStruct((t, D_MODEL), BF16)],
        scratch_shapes=[_lane_scratch(tm, 512), _lane_scratch(tm, 256)],
        compiler_params=_params(("arbitrary",)),
    )(o0, o1, o2, l0, l1, l2, ob, gts, x, w_a, w_b, w_out, g_cross)


def _memkv(mem, g_mem, w_ckv):
    m = mem.shape[0]
    ws = w_ckv.shape[2]

    def body(mem_ref, g_ref, w_ref, mn_ref, kv_ref):
        xt = mem_ref[...]
        mn = (xt * _rms(xt) * g_ref[...]).astype(BF16)
        mn_ref[...] = mn
        for j in range(N_DEV):
            kv_ref[:, j * ws:(j + 1) * ws] = _dot(mn, w_ref[j]).astype(BF16)

    return pl.pallas_call(
        body, name="memkv",
        out_shape=[jax.ShapeDtypeStruct((m, D_MODEL), BF16), jax.ShapeDtypeStruct((m, 2 * D_MODEL), BF16)],
        compiler_params=_params(),
    )(mem, g_mem, w_ckv)


def _cross_probs(q, kv_ref, h):
    k = kv_ref[:, h * X_HEAD_DIM:(h + 1) * X_HEAD_DIM]
    sc = _dot_nt(q[:, h * X_HEAD_DIM:(h + 1) * X_HEAD_DIM], k)
    m = jnp.max(sc, axis=-1, keepdims=True)
    p = jnp.exp(sc - m)
    return p / jnp.sum(p, axis=-1, keepdims=True)


def _cross(hc, x1, kv, w_cq, w_co, g_mlp, tm):
    t = x1.shape[0]
    m = kv.shape[0]

    def body(hc_ref, x1_ref, kv_ref, wq_ref, wo_ref, g_ref, q_ref, o_ref, x2_ref, hm_ref):
        q = (_dot(hc_ref[...], wq_ref[...]) * X_SCALE).astype(BF16)
        q_ref[...] = q
        outs = []
        for h in range(X_HEADS):
            p = _cross_probs(q, kv_ref, h)
            v = kv_ref[:, D_MODEL + h * X_HEAD_DIM:D_MODEL + (h + 1) * X_HEAD_DIM]
            outs.append(_dot(p.astype(BF16), v))
        o = jnp.concatenate(outs, axis=1).astype(BF16)
        o_ref[...] = o
        x2 = x1_ref[...] + _dot(o, wo_ref[...])
        x2_ref[...] = x2
        hm_ref[...] = (x2 * _rms(x2) * g_ref[...]).astype(BF16)

    row = lambda w: pl.BlockSpec((tm, w), lambda i: (i, 0))
    full = lambda a, b: pl.BlockSpec((a, b), lambda i: (0, 0))
    return pl.pallas_call(
        body, name="cross", grid=(t // tm,),
        in_specs=[row(D_MODEL), row(D_MODEL), full(m, 2 * D_MODEL), full(D_MODEL, D_MODEL),
                  full(D_MODEL, D_MODEL), full(1, D_MODEL)],
        out_specs=[row(D_MODEL)] * 4,
        out_shape=[jax.ShapeDtypeStruct((t, D_MODEL), BF16), jax.ShapeDtypeStruct((t, D_MODEL), BF16),
                   jax.ShapeDtypeStruct((t, D_MODEL), F32), jax.ShapeDtypeStruct((t, D_MODEL), BF16)],
        compiler_params=_params(("arbitrary",)),
    )(hc, x1, kv, w_cq, w_co, g_mlp)


def _mlp(hm, x2, w_1, w_2, g_final, target, tm, tf):
    t = x2.shape[0]
    nf = D_FF // tf

    def body(hm_ref, x2_ref, w1_ref, w2_ref, g_ref, tg_ref, a_ref, dx3_ref, loss_ref, dg_ref, acc_ref):
        i, f = pl.program_id(0), pl.program_id(1)
        hm_t = hm_ref[...]
        sw = w1_ref.shape[2]
        part = None
        for s in range(w1_ref.shape[0]):
            a = jnp.maximum(_dot(hm_t, w1_ref[s]), 0.0).astype(BF16)
            a_ref[:, s * sw:(s + 1) * sw] = a
            p_s = _dot(a * a, w2_ref[s * sw:(s + 1) * sw, :])
            part = p_s if part is None else part + p_s

        @pl.when(f == 0)
        def _():
            acc_ref[...] = part

        @pl.when(f > 0)
        def _():
            acc_ref[...] += part

        @pl.when((i == 0) & (f == 0))
        def _():
            loss_ref[...] = jnp.zeros_like(loss_ref)
            dg_ref[...] = jnp.zeros_like(dg_ref)

        @pl.when(f == nf - 1)
        def _():
            x3 = x2_ref[...] + acc_ref[...]
            r = _rms(x3)
            g = g_ref[...]
            diff = x3 * r * g - tg_ref[...]
            loss_ref[...] += 0.5 * jnp.sum(jnp.mean(diff * diff, axis=-1, keepdims=True))
            dx3, dg = _rms_bwd(diff / D_MODEL, x3, r, g)
            dx3_ref[...] = dx3
            dg_ref[...] += dg

    return pl.pallas_call(
        body, name="mlp", grid=(t // tm, nf),
        in_specs=[pl.BlockSpec((tm, D_MODEL), lambda i, f: (i, 0)),
                  pl.BlockSpec((tm, D_MODEL), lambda i, f: (i, 0)),
                  pl.BlockSpec((tf // w_1.shape[2], D_MODEL, w_1.shape[2]), lambda i, f: (f, 0, 0)),
                  pl.BlockSpec((tf, D_MODEL), lambda i, f: (f, 0)),
                  pl.BlockSpec((1, D_MODEL), lambda i, f: (0, 0)),
                  pl.BlockSpec((tm, D_MODEL), lambda i, f: (i, 0))],
        out_specs=[pl.BlockSpec((tm, tf), lambda i, f: (i, f)),
                   pl.BlockSpec((tm, D_MODEL), lambda i, f: (i, 0)),
                   pl.BlockSpec((1, 128), lambda i, f: (0, 0)),
                   pl.BlockSpec((1, D_MODEL), lambda i, f: (0, 0))],
        out_shape=[jax.ShapeDtypeStruct((t, D_FF), BF16), jax.ShapeDtypeStruct((t, D_MODEL), F32),
                   jax.ShapeDtypeStruct((1, 128), F32), jax.ShapeDtypeStruct((1, D_MODEL), F32)],
        scratch_shapes=[pltpu.VMEM((tm, D_MODEL), F32)],
        compiler_params=_params(("arbitrary", "arbitrary")),
    )(hm, x2, w_1, w_2, g_final, target)


def _mlp_bwd(dx3, a, w_1, w_2, x2, g_mlp, tm, tf):
    t = x2.shape[0]
    nf = D_FF // tf

    def body(dx3_ref, a_ref, w1_ref, w2_ref, x2_ref, g_ref, dz_ref, dx2_ref, dg_ref, acc_ref):
        i, f = pl.program_id(0), pl.program_id(1)
        dx3_b = dx3_ref[...].astype(BF16)
        sw = w1_ref.shape[2]
        part = None
        for s in range(w1_ref.shape[0]):
            cols = slice(s * sw, (s + 1) * sw)
            da2 = _dot_nt(dx3_b, w2_ref[cols, :])
            dz = (2.0 * a_ref[:, cols].astype(F32) * da2).astype(BF16)
            dz_ref[:, cols] = dz
            p_s = _dot_nt(dz, w1_ref[s])
            part = p_s if part is None else part + p_s

        @pl.when(f == 0)
        def _():
            acc_ref[...] = part

        @pl.when(f > 0)
        def _():
            acc_ref[...] += part

        @pl.when((i == 0) & (f == 0))
        def _():
            dg_ref[...] = jnp.zeros_like(dg_ref)

        @pl.when(f == nf - 1)
        def _():
            xt = x2_ref[...]
            dx, dg = _rms_bwd(acc_ref[...], xt, _rms(xt), g_ref[...])
            dx2_ref[...] = dx3_ref[...] + dx
            dg_ref[...] += dg

    return pl.pallas_call(
        body, name="mlp_bwd", grid=(t // tm, nf),
        in_specs=[pl.BlockSpec((tm, D_MODEL), lambda i, f: (i, 0)),
                  pl.BlockSpec((tm, tf), lambda i, f: (i, f)),
                  pl.BlockSpec((tf // w_1.shape[2], D_MODEL, w_1.shape[2]), lambda i, f: (f, 0, 0)),
                  pl.BlockSpec((tf, D_MODEL), lambda i, f: (f, 0)),
                  pl.BlockSpec((tm, D_MODEL), lambda i, f: (i, 0)),
                  pl.BlockSpec((1, D_MODEL), lambda i, f: (0, 0))],
        out_specs=[pl.BlockSpec((tm, tf), lambda i, f: (i, f)),
                   pl.BlockSpec((tm, D_MODEL), lambda i, f: (i, 0)),
                   pl.BlockSpec((1, D_MODEL), lambda i, f: (0, 0))],
        out_shape=[jax.ShapeDtypeStruct((t, D_FF), BF16), jax.ShapeDtypeStruct((t, D_MODEL), F32),
                   jax.ShapeDtypeStruct((1, D_MODEL), F32)],
        scratch_shapes=[pltpu.VMEM((tm, D_MODEL), F32)],
        compiler_params=_params(("arbitrary", "arbitrary")),
    )(dx3, a, w_1, w_2, x2, g_mlp)


def _wgrad(name, a, b, tka, tn, tm, square=False, col_shards=False):
    t, ka = a.shape
    n = b.shape[1]
    nk = t // tm

    def body(a_ref, b_ref, o_ref, acc_ref):
        at = a_ref[...].astype(BF16)
        if square:
            at = at * at
        part = _dot_tn(at, b_ref[...].astype(BF16))
        k = pl.program_id(2)

        @pl.when(k == 0)
        def _():
            acc_ref[...] = part

        @pl.when(k > 0)
        def _():
            acc_ref[...] += part

        @pl.when(k == nk - 1)
        def _():
            if col_shards:
                for s in range(tn // sw):
                    o_ref[s] = acc_ref[:, s * sw:(s + 1) * sw].astype(BF16)
            else:
                o_ref[...] = acc_ref[...].astype(BF16)

    if col_shards:
        sw = n // N_DEV
        out_spec = pl.BlockSpec((tn // sw, tka, sw), lambda p, q, k: (q, p, 0))
        out_shape = jax.ShapeDtypeStruct((N_DEV, ka, sw), BF16)
    else:
        out_spec = pl.BlockSpec((tka, tn), lambda p, q, k: (p, q))
        out_shape = jax.ShapeDtypeStruct((ka, n), BF16)
    return pl.pallas_call(
        body, name=name, grid=(ka // tka, n // tn, nk),
        in_specs=[pl.BlockSpec((tm, tka), lambda p, q, k: (k, p)),
                  pl.BlockSpec((tm, tn), lambda p, q, k: (k, q))],
        out_specs=out_spec, out_shape=out_shape,
        scratch_shapes=[pltpu.VMEM((tka, tn), F32)],
        compiler_params=_params(("arbitrary", "arbitrary", "arbitrary")),
    )(a, b)


def _cross_bwd(dx2, x1, q, kv, w_cq, w_co, g_cross, tm, dep=None):
    t = x1.shape[0]
    m = kv.shape[0]

    def body(dx2_ref, x1_ref, q_ref, kv_ref, wq_ref, wo_ref, g_ref, dq_ref, dx1_ref, dkv_ref, dg_ref):
        @pl.when(pl.program_id(0) == 0)
        def _():
            dkv_ref[...] = jnp.zeros_like(dkv_ref)
            dg_ref[...] = jnp.zeros_like(dg_ref)

        do = _dot_nt(dx2_ref[...].astype(BF16), wo_ref[...]).astype(BF16)
        q = q_ref[...]
        dqs = []
        for h in range(X_HEADS):
            hs = slice(h * X_HEAD_DIM, (h + 1) * X_HEAD_DIM)
            vs = slice(D_MODEL + h * X_HEAD_DIM, D_MODEL + (h + 1) * X_HEAD_DIM)
            p = _cross_probs(q, kv_ref, h)
            dp = _dot_nt(do[:, hs], kv_ref[:, vs])
            ds = (p * (dp - jnp.sum(dp * p, axis=-1, keepdims=True))).astype(BF16)
            dqs.append(_dot(ds, kv_ref[:, hs]))
            dkv_ref[:, hs] += _dot_tn(ds, q[:, hs])
            dkv_ref[:, vs] += _dot_tn(p.astype(BF16), do[:, hs])
        dq = (jnp.concatenate(dqs, axis=1) * X_SCALE).astype(BF16)
        dq_ref[...] = dq
        xt = x1_ref[...]
        dx, dg = _rms_bwd(_dot_nt(dq, wq_ref[...]), xt, _rms(xt), g_ref[...])
        dx1_ref[...] = dx2_ref[...] + dx
        dg_ref[...] += dg

    row = lambda w: pl.BlockSpec((tm, w), lambda i: (i, 0))
    full = lambda a, b: pl.BlockSpec((a, b), lambda i: (0, 0))
    return pl.pallas_call(
        _with_dep(body, 7, dep), name="cross_bwd", grid=(t // tm,),
        in_specs=[row(D_MODEL), row(D_MODEL), row(D_MODEL), full(m, 2 * D_MODEL), full(D_MODEL, D_MODEL),
                  full(D_MODEL, D_MODEL), full(1, D_MODEL)] + _dep_spec(dep),
        out_specs=[row(D_MODEL), row(D_MODEL), full(m, 2 * D_MODEL), full(1, D_MODEL)],
        out_shape=[jax.ShapeDtypeStruct((t, D_MODEL), BF16), jax.ShapeDtypeStruct((t, D_MODEL), F32),
                   jax.ShapeDtypeStruct((m, 2 * D_MODEL), F32), jax.ShapeDtypeStruct((1, D_MODEL), F32)],
        compiler_params=_params(("arbitrary",)),
    )(dx2, x1, q, kv, w_cq, w_co, g_cross, *_dep_arg(dep))


def _memkv_bwd(dkv, mn, mem, w_ckv, g_mem):
    ws = w_ckv.shape[2]

    def body(dkv_ref, mn_ref, mem_ref, w_ref, g_ref, dw_ref, dg_ref):
        mn = mn_ref[...]
        dmn = jnp.zeros(mn.shape, F32)
        for j in range(N_DEV):
            dkvb = dkv_ref[:, j * ws:(j + 1) * ws].astype(BF16)
            dw_ref[j] = _dot_tn(mn, dkvb).astype(BF16)
            dmn = dmn + _dot_nt(dkvb, w_ref[j])
        xt = mem_ref[...]
        dg_ref[...] = jnp.sum(dmn * xt * _rms(xt), axis=0, keepdims=True)

    return pl.pallas_call(
        body, name="memkv_bwd",
        out_shape=[jax.ShapeDtypeStruct(w_ckv.shape, BF16), jax.ShapeDtypeStruct((1, D_MODEL), F32)],
        compiler_params=_params(),
    )(dkv, mn, mem, w_ckv, g_mem)


def _merge_bwd(dx1, ya, yb, gts, w_out, w_g, tm):
    t = dx1.shape[0]

    def body(dx1_ref, ya_ref, yb_ref, g_ref, wo_ref, wg_ref, dg_ref, dhp_ref, dya_ref, dyb_ref, db_ref):
        @pl.when(pl.program_id(0) == 0)
        def _():
            db_ref[...] = jnp.zeros_like(db_ref)

        dm = _dot_nt(dx1_ref[...].astype(BF16), wo_ref[...])
        ga = g_ref[:, :D_MODEL].astype(F32)
        gb = g_ref[:, D_MODEL:].astype(F32)
        dya_ref[...] = (dm * ga).astype(BF16)
        dyb_ref[...] = (dm * gb).astype(BF16)
        dpa = dm * ya_ref[...].astype(F32) * ga * (1.0 - ga)
        dpb = dm * yb_ref[...].astype(F32) * gb * (1.0 - gb)
        dpre = jnp.concatenate([dpa, dpb], axis=1)
        db_ref[...] += jnp.sum(dpre, axis=0, keepdims=True)
        dpreb = dpre.astype(BF16)
        dg_ref[...] = dpreb
        dhp_ref[...] = _dot_nt(dpreb, wg_ref[...])

    row = lambda w: pl.BlockSpec((tm, w), lambda i: (i, 0))
    once = lambda a, b: pl.BlockSpec((a, b), lambda i: (0, 0), pipeline_mode=pl.Buffered(1))
    sds = jax.ShapeDtypeStruct
    return pl.pallas_call(
        body, name="merge_bwd", grid=(t // tm,),
        in_specs=[row(D_MODEL), row(D_MODEL), row(D_MODEL), row(GATE_WIDTH),
                  once(D_MODEL, D_MODEL), once(D_MODEL, GATE_WIDTH)],
        out_specs=[row(GATE_WIDTH), row(D_MODEL), row(D_MODEL), row(D_MODEL),
                   pl.BlockSpec((1, GATE_WIDTH), lambda i: (0, 0))],
        out_shape=[sds((t, GATE_WIDTH), BF16), sds((t, D_MODEL), F32), sds((t, D_MODEL), BF16),
                   sds((t, D_MODEL), BF16), sds((1, GATE_WIDTH), F32)],
        compiler_params=_params(("arbitrary",)),
    )(dx1, ya, yb, gts, w_out, w_g)


def _combine_bwd(dya, dyb, oa, ob, l0, l1, l2, lb, sink_row, w_a, w_b, tm):
    t = dya.shape[0]

    def body(dya_ref, dyb_ref, oa_ref, ob_ref, l0_ref, l1_ref, l2_ref, lb_ref, sk_ref, wa_ref, wb_ref,
             do0_ref, do1_ref, do2_ref, c0_ref, c1_ref, c2_ref, dob_ref, cb_ref, dsk_ref, so_ref, sl_ref):
        @pl.when(pl.program_id(0) == 0)
        def _():
            dsk_ref[...] = jnp.zeros_like(dsk_ref)

        doa = _dot_nt(dya_ref[...], wa_ref[...])
        dob = _dot_nt(dyb_ref[...], wb_ref[...])
        dsum = _head_sums(doa * oa_ref[...].astype(F32), _head_gather(256, True))
        a0, a1, a2 = _alphas(l0_ref[...], _interleave(l1_ref, sl_ref), _interleave(l2_ref, sl_ref))
        c0_ref[...] = a0 * dsum
        spread = _dil_head_spread()
        do0_ref[...] = _head_scale(doa, a0, spread).astype(BF16)
        for al, do_ref, c_ref in ((a1, do1_ref, c1_ref), (a2, do2_ref, c2_ref)):
            _deinterleave(al * dsum, sl_ref, c_ref, F32)
            _deinterleave(_head_scale(doa, al, spread), so_ref, do_ref, BF16)
        dob_ref[...] = dob.astype(BF16)
        cb = _head_sums(dob * ob_ref[...], _head_gather(128, False))
        cb_ref[...] = cb
        lane = lax.broadcasted_iota(jnp.int32, cb.shape, 1)
        psink = jnp.where(lane < 8, jnp.exp(sk_ref[...] - lb_ref[...]), 0.0)
        dsk_ref[...] += jnp.sum(-psink * cb, axis=0, keepdims=True)

    row = lambda w: pl.BlockSpec((tm, w), lambda i: (i, 0))
    full = lambda a, b: pl.BlockSpec((a, b), lambda i: (0, 0))
    sds = jax.ShapeDtypeStruct
    d1, d2 = l1.shape[0], l2.shape[0]
    res = lambda d, w: pl.BlockSpec((d, tm // d, w), lambda i: (0, i, 0))
    return pl.pallas_call(
        body, name="combine_bwd", grid=(t // tm,),
        in_specs=[row(D_MODEL), row(D_MODEL), row(512), row(512),
                  row(256), _res_spec(l1, tm), _res_spec(l2, tm), row(128), full(1, 128),
                  full(512, D_MODEL), full(512, D_MODEL)],
        out_specs=[row(512), res(d1, 512), res(d2, 512), row(256), res(d1, 256), res(d2, 256),
                   row(512), row(128), full(1, 128)],
        out_shape=[sds((t, 512), BF16), sds((d1, t // d1, 512), BF16),
                   sds((d2, t // d2, 512), BF16), sds((t, 256), F32), sds((d1, t // d1, 256), F32),
                   sds((d2, t // d2, 256), F32), sds((t, 512), BF16),
                   sds((t, 128), F32), sds((1, 128), F32)],
        scratch_shapes=[_lane_scratch(tm, 512), _lane_scratch(tm, 256)],
        compiler_params=_params(("arbitrary",)),
    )(dya, dyb, oa, ob, l0, l1, l2, lb, sink_row, w_a, w_b)


def _attn_bwd(name, pv, dov, lsev, cv, cosv, sinv, swa, tq, dep=None):
    d, ls = pv.shape[0], pv.shape[1]
    n, nsb = ls // tq, tq // BAND
    pairs = _attn_layout(swa)
    ncol = 1 if swa else 2
    ow = 128 * len(pairs)

    kv_slots = sorted({(ko, vo) for _, ko, vo in pairs})

    def body(cur_ref, tail_ref, do_ref, lse_ref, c_ref, cos_ref, sin_ref, out_ref, acc_ref, carry_ref, acct_ref):
        i = pl.program_id(2)
        blk_i = n - 1 - i
        acc_ref[...] = jnp.zeros_like(acc_ref)
        acct_ref[...] = jnp.zeros_like(acct_ref)

        @pl.when(i == 0)
        def _():
            carry_ref[...] = jnp.zeros_like(carry_ref)

        qk_a, v_a = _head_a_masks(BAND)
        dim = lax.broadcasted_iota(jnp.int32, (128, BAND), 0)
        qk_at, v_at = (dim % HEAD_DIM) < HEAD_DIM // 2, dim < HEAD_DIM
        for s in range(nsb):
            mask = _band_mask(blk_i, s)
            mask2 = jnp.concatenate([mask, mask], axis=0)
            rows = slice(s * BAND, (s + 1) * BAND)
            kcols = slice(s * BAND, (s + 2) * BAND)
            for j, (qo, ko, vo) in enumerate(pairs):
                slot = kv_slots.index((ko, vo))
                kk = _kv_rows(cur_ref, tail_ref, s, ko)
                vv = _kv_rows(cur_ref, tail_ref, s, vo)
                q, do = cur_ref[rows, qo:qo + 128], do_ref[rows, j * 128:(j + 1) * 128]
                q2, do2 = _stack_heads(q, qk_a), _stack_heads(do, v_a)
                col2 = lambda ref: jnp.concatenate([ref[rows, 2 * j:2 * j + 1], ref[rows, 2 * j + 1:2 * j + 2]], axis=0)
                sc = _dot_nt(q2, kk)
                p = jnp.exp(jnp.where(mask2, sc, -jnp.inf) - col2(lse_ref))
                dp = _dot_nt(do2, vv)
                ds = (p * (dp - col2(c_ref))).astype(BF16)
                dq2 = _dot(ds, kk)
                acc_ref[BAND + s * BAND:BAND + (s + 1) * BAND, qo:qo + 128] += jnp.where(qk_a, dq2[:BAND], dq2[BAND:])
                acct_ref[2 * slot, :, kcols] += _dot(_stack_heads_t(q.T, qk_at), ds)
                acct_ref[2 * slot + 1, :, kcols] += _dot(_stack_heads_t(do.T, v_at), p.astype(BF16))
        for slot, (ko, vo) in enumerate(kv_slots):
            acc_ref[:, ko:ko + 128] += acct_ref[2 * slot].T
            acc_ref[:, vo:vo + 128] += acct_ref[2 * slot + 1].T

        last = acc_ref[tq:, :] + carry_ref[...]
        fin = last if tq == BAND else jnp.concatenate([acc_ref[BAND:tq, :], last], axis=0)
        out_ref[...] = _rope(fin, cos_ref[...], sin_ref[...], swa, -1).astype(BF16)
        carry_ref[...] = acc_ref[0:BAND, :]

    rev = lambda i: n - 1 - i
    blk = lambda rows, w, row_of: pl.BlockSpec((None, rows, w), lambda r, cb, i: (r, row_of(i), cb))
    tab = pl.BlockSpec((None, tq, 128), lambda r, cb, i: (r, rev(i), 0))
    return pl.pallas_call(
        _with_dep(body, 7, dep), name=name, grid=(d, ncol, n),
        in_specs=[blk(tq, PBLK, rev), blk(BAND, PBLK, lambda i: jnp.maximum(rev(i) * nsb - 1, 0)),
                  blk(tq, ow, rev), blk(tq, 128, rev), blk(tq, 128, rev), tab, tab] + _dep_spec(dep),
        out_specs=blk(tq, PBLK, rev),
        out_shape=jax.ShapeDtypeStruct((d, ls, ncol * PBLK), BF16),
        scratch_shapes=[pltpu.VMEM((tq + BAND, PBLK), F32), pltpu.VMEM((BAND, PBLK), F32),
                        pltpu.VMEM((2 * len(kv_slots), 128, tq + BAND), F32)],
        compiler_params=_params(("arbitrary", "arbitrary", "arbitrary")),
    )(pv, pv, dov, lsev, cv, cosv, sinv, *_dep_arg(dep))


def _dx(dp0, dp1, dp2, dpb, w_p_t, dh_part, dx1, x, g_mix, tm, dep=None):
    t = x.shape[0]
    gw = 2 * PBLK

    def body(dp0_ref, dp1_ref, dp2_ref, dpb_ref, w_ref, dhp_ref, dx1_ref, x_ref, g_ref, gx_ref, dg_ref,
             dpt_ref, scr_ref):
        @pl.when(pl.program_id(0) == 0)
        def _():
            dg_ref[...] = jnp.zeros_like(dg_ref)

        dpt_ref[:, 0:gw] = dp0_ref[...]
        dpt_ref[:, gw:2 * gw] = _interleave(dp1_ref, scr_ref).astype(BF16)
        dpt_ref[:, 2 * gw:3 * gw] = _interleave(dp2_ref, scr_ref).astype(BF16)
        dpt_ref[:, 3 * gw:] = dpb_ref[...]
        dh = _dot(dpt_ref[...], w_ref[...]) + dhp_ref[...]
        xt = x_ref[...]
        dx, dg = _rms_bwd(dh, xt, _rms(xt), g_ref[...])
        gx_ref[...] = dx1_ref[...] + dx
        dg_ref[...] += dg

    row = lambda w: pl.BlockSpec((tm, w), lambda i: (i, 0))
    full = lambda a, b: pl.BlockSpec((a, b), lambda i: (0, 0))
    return pl.pallas_call(
        _with_dep(body, 9, dep), name="dx", grid=(t // tm,),
        in_specs=[row(gw), _res_spec(dp1, tm), _res_spec(dp2, tm), row(PBLK),
                  pl.BlockSpec((P_WIDTH, D_MODEL), lambda i: (0, 0), pipeline_mode=pl.Buffered(1)),
                  row(D_MODEL), row(D_MODEL), row(D_MODEL), full(1, D_MODEL)] + _dep_spec(dep),
        out_specs=[row(D_MODEL), full(1, D_MODEL)],
        out_shape=[jax.ShapeDtypeStruct((t, D_MODEL), F32), jax.ShapeDtypeStruct((1, D_MODEL), F32)],
        scratch_shapes=[pltpu.VMEM((tm, P_WIDTH), BF16), _lane_scratch(tm, gw)],
        compiler_params=_params(("arbitrary",)),
    )(dp0, dp1, dp2, dpb, w_p_t, dh_part, dx1, x, g_mix, *_dep_arg(dep))


MESH = pl.DeviceIdType.MESH
HBM_SPEC = pl.BlockSpec(memory_space=pltpu.HBM)
VMEM_SPEC = pl.BlockSpec(memory_space=pltpu.VMEM)


def _all_gather(xp, act, g, tm):
    t = act.shape[0]
    d1, d2 = DIL_GROUPS[1][1], DIL_GROUPS[2][1]

    def body(x_ref, act_ref, g_ref, out_ref, h_ref, h1_ref, h2_ref, send_sems, recv_sems, local_sem, hf_ref):
        x, y, c = lax.axis_index("x"), lax.axis_index("y"), lax.axis_index("c")
        me, sibling = (x, y, c), (x, y, 1 - c)
        chips = [(1 - x, y), (x, 1 - y), (1 - x, 1 - y)]

        def rows(px, py, pc):
            return out_ref.at[4 * px + 2 * py + pc]

        def copy(k, block, to, src=None):
            return pltpu.make_async_remote_copy(
                src_ref=rows(*block) if src is None else src, dst_ref=rows(*block),
                send_sem=send_sems.at[k], recv_sem=recv_sems.at[k], device_id=to, device_id_type=MESH)

        mine = pltpu.make_async_copy(x_ref, rows(*me), local_sem)
        mine.start()
        first = [copy(0, me, sibling, src=x_ref)]
        first += [copy(1 + j, me, (*chip, c), src=x_ref) for j, chip in enumerate(chips)]
        for cp in first:
            cp.start()

        def norm(a_blk, h_blk, h1_blk, h2_blk):
            xt = a_blk[...]
            hf = xt * _rms(xt) * g_ref[...]
            h_blk[...] = hf.astype(BF16)
            _deinterleave(hf, hf_ref, h1_blk, BF16)
            _deinterleave(hf, hf_ref, h2_blk, BF16)

        res = lambda d: pl.BlockSpec((d, tm // d, D_MODEL), lambda i: (0, i, 0))
        row = pl.BlockSpec((tm, D_MODEL), lambda i: (i, 0))
        pltpu.emit_pipeline(norm, grid=(t // tm,), in_specs=[row], out_specs=[row, res(d1), res(d2)])(
            act_ref, h_ref, h1_ref, h2_ref)

        passed = [copy(4 + j, (*chip, c), sibling) for j, chip in enumerate(chips)]
        for j, chip in enumerate(chips):
            copy(1 + j, (*chip, c), me).wait_recv()
            passed[j].start()
        copy(0, sibling, me).wait_recv()
        for j, chip in enumerate(chips):
            copy(4 + j, (*chip, 1 - c), me).wait_recv()
        for cp in first + passed:
            cp.wait_send()
        mine.wait()

    sds = jax.ShapeDtypeStruct
    return pl.pallas_call(
        body, name="all_gather",
        out_shape=[sds((N_DEV,) + xp.shape, xp.dtype), sds((t, D_MODEL), BF16),
                   sds((d1, t // d1, D_MODEL), BF16), sds((d2, t // d2, D_MODEL), BF16)],
        in_specs=[HBM_SPEC, HBM_SPEC, VMEM_SPEC], out_specs=[HBM_SPEC] * 4,
        scratch_shapes=[pltpu.SemaphoreType.DMA((7,)), pltpu.SemaphoreType.DMA((7,)), pltpu.SemaphoreType.DMA,
                        _lane_scratch(tm, D_MODEL)],
        compiler_params=pltpu.CompilerParams(vmem_limit_bytes=VMEM_LIMIT),
    )(xp, act, g)


def _peers():
    x, y, c = lax.axis_index("x"), lax.axis_index("y"), lax.axis_index("c")
    out = []
    for k in range(1, N_DEV):
        px = 1 - x if k & 4 else x
        py = 1 - y if k & 2 else y
        pc = 1 - c if k & 1 else c
        out.append((k, (px, py, pc), 4 * px + 2 * py + pc))
    return out


def _my_index():
    return 4 * lax.axis_index("x") + 2 * lax.axis_index("y") + lax.axis_index("c")


SEM_SPEC = pl.BlockSpec(memory_space=pltpu.SEMAPHORE)
ANY_SPEC = pl.BlockSpec(memory_space=pl.ANY)
_SPLIT_PARAMS = pltpu.CompilerParams(has_side_effects=pltpu.SideEffectType.DATAFLOW_SIDE_EFFECTING)


def _split_copies(gather, src_refs, land_refs, send_sems, recv_sems):
    me_idx = _my_index()
    out = []
    for a, (src_ref, land_ref) in enumerate(zip(src_refs, land_refs)):
        for k, peer, peer_idx in _peers():
            if gather:
                src, dst = src_ref, land_ref.at[me_idx]
            else:
                src, dst = src_ref.at[peer_idx], land_ref.at[k - 1]
            out.append(pltpu.make_async_remote_copy(
                src_ref=src, dst_ref=dst, send_sem=send_sems.at[7 * a + k - 1], recv_sem=recv_sems.at[7 * a + k - 1],
                device_id=peer, device_id_type=MESH))
    return out


def _split_start(name, gather, srcs, after=None):
    n = len(srcs)
    extra = [] if after is None else [after]
    first_out = n + len(extra)

    def body(*refs):
        send_sems, recv_sems = refs[first_out], refs[first_out + 1]
        lands = refs[first_out + 2 + n:first_out + 2 + 2 * n]
        for cp in _split_copies(gather, refs[:n], lands, send_sems, recv_sems):
            cp.start()
        token = refs[-1]
        token[...] = jnp.zeros_like(token)

    lands = [pltpu.HBM((N_DEV,) + a.shape if gather else (N_DEV - 1,) + a.shape[1:], a.dtype) for a in srcs]
    return pl.pallas_call(
        body, name=name,
        out_shape=(pltpu.SemaphoreType.DMA((7 * n,)), pltpu.SemaphoreType.DMA((7 * n,)),
                   *[pltpu.HBM(a.shape, a.dtype) for a in srcs], *lands, jax.ShapeDtypeStruct((8, 128), F32)),
        in_specs=(HBM_SPEC,) * n + (ANY_SPEC,) * len(extra),
        out_specs=(SEM_SPEC, SEM_SPEC) + (HBM_SPEC,) * (2 * n) + (VMEM_SPEC,),
        input_output_aliases={i: 2 + i for i in range(n)}, compiler_params=_SPLIT_PARAMS,
    )(*[pltpu.with_memory_space_constraint(a, pltpu.HBM) for a in srcs], *extra)


def _split_wait(name, gather, started, after):
    send_sems, recv_sems, bufs = started[0], started[1], started[2:-1]
    n = len(bufs) // 2

    def body(*refs):
        for cp in _split_copies(gather, refs[:n], refs[n:2 * n], refs[2 * n], refs[2 * n + 1]):
            cp.wait_send()
            cp.wait_recv()

    out = pl.pallas_call(
        body, name=name, out_shape=tuple(pltpu.HBM(a.shape, a.dtype) for a in bufs),
        in_specs=(HBM_SPEC,) * (2 * n) + (SEM_SPEC, SEM_SPEC, ANY_SPEC), out_specs=(HBM_SPEC,) * (2 * n),
        input_output_aliases={i: i for i in range(2 * n)}, compiler_params=_SPLIT_PARAMS,
    )(*bufs, send_sems, recv_sems, after)
    return out[:n], out[n:]


def _adam_update(g, w, m, v):
    nm = ADAM_B1 * m + (1.0 - ADAM_B1) * g
    nv = ADAM_B2 * v + (1.0 - ADAM_B2) * (g * g)
    m_hat = nm / (1.0 - ADAM_B1 ** ADAM_STEP)
    v_hat = nv / (1.0 - ADAM_B2 ** ADAM_STEP)
    return -ADAM_LR * (m_hat / (jnp.sqrt(v_hat) + ADAM_EPS) + ADAM_WD * w), nm, nv


def _adamw(name, me, sent, got, w, m, v, tr, tc=None):
    r, c = w.shape
    tc = c if tc is None else tc

    def body(me_ref, own_ref, got_ref, w_ref, m_ref, v_ref, g_ref, d_ref, nm_ref, nv_ref):
        g = own_ref[...].astype(F32)
        for k in range(N_DEV - 1):
            g = g + got_ref[k].astype(F32)
        g_ref[...] = g
        d_ref[...], nm_ref[...], nv_ref[...] = _adam_update(g, w_ref[...], m_ref[...], v_ref[...])

    blk = pl.BlockSpec((tr, tc), lambda i, j, me_ref: (i, j))
    return pl.pallas_call(
        body, name=name,
        grid_spec=pltpu.PrefetchScalarGridSpec(
            num_scalar_prefetch=1, grid=(r // tr, c // tc),
            in_specs=[pl.BlockSpec((None, tr, tc), lambda i, j, me_ref: (me_ref[0], i, j)),
                      pl.BlockSpec((N_DEV - 1, tr, tc), lambda i, j, me_ref: (0, i, j)), blk, blk, blk],
            out_specs=[blk] * 4),
        out_shape=[jax.ShapeDtypeStruct((r, c), F32)] * 4,
        compiler_params=_params(("arbitrary", "arbitrary")),
    )(me, sent, got, w, m, v)


def _adamw_small(srecv, ws, ms, vs):
    nv_ = len(ws)

    def body(*refs):
        s_ref = refs[0]
        ins, outs = refs[1:1 + 3 * nv_], refs[1 + 3 * nv_:]
        g_all = s_ref[0]
        for k in range(1, N_DEV):
            g_all = g_all + s_ref[k]
        for i in range(nv_):
            n = ins[i].shape[1]
            g = g_all[i:i + 1, :n]
            d, nm, nv = _adam_update(g, ins[i][...], ins[nv_ + i][...], ins[2 * nv_ + i][...])
            outs[i][...], outs[nv_ + i][...], outs[2 * nv_ + i][...], outs[3 * nv_ + i][...] = g, d, nm, nv
        outs[-1][...] = g_all[nv_:nv_ + 1, :128]

    shapes = [jax.ShapeDtypeStruct(a.shape, F32) for a in ws]
    res = pl.pallas_call(body, name="adamw_small", out_shape=shapes * 4 + [jax.ShapeDtypeStruct((1, 128), F32)],
                         compiler_params=_params())(srecv, *ws, *ms, *vs)
    return [res[k * nv_:(k + 1) * nv_] for k in range(4)], res[-1]


def _cols_from_shards(a):
    return jnp.swapaxes(a, 0, 1).reshape(a.shape[1], a.shape[0] * a.shape[2])


def _shards_from_cols(a):
    return jnp.swapaxes(a.reshape(a.shape[0], N_DEV, a.shape[1] // N_DEV), 0, 1)


def _shards_from_rows(a):
    return a.reshape(N_DEV, a.shape[0] // N_DEV, a.shape[1])


def _pair_lanes(a):
    lead = a.shape[:-1]
    return a.reshape(lead + (2, 2, HEAD_DIM // 2)).swapaxes(-3, -2).reshape(lead + (128,))


def _split_w_in(w_in):
    rows = w_in.shape[0]
    dil = w_in[:, :3 * DIL_WIDTH].reshape(rows, 3, 3, 4, 128)
    dil = np.concatenate([_pair_lanes(dil[:, :2]), dil[:, 2:]], axis=1)
    dil = dil.transpose(0, 2, 3, 1, 4).reshape(rows, 3 * DIL_WIDTH)
    o = 3 * DIL_WIDTH
    qb = w_in[:, o:o + SWA_Q_WIDTH].reshape(rows, 2, 4, HEAD_DIM).transpose(0, 2, 1, 3).reshape(rows, 4, 128)
    qb = _pair_lanes(qb).reshape(rows, SWA_Q_WIDTH)
    kb = _pair_lanes(w_in[:, o + SWA_Q_WIDTH:o + SWA_Q_WIDTH + SWA_KV_WIDTH])
    vb = w_in[:, o + SWA_Q_WIDTH + SWA_KV_WIDTH:P_WIDTH]
    return np.concatenate([dil, qb, kb, vb], axis=1)


ROW_GRANULE = HEAD_DIM // 2


def _w_p_granules():
    order = _split_w_in(np.arange(IN_WIDTH)[None])[0]
    assert sorted(order.tolist()) == list(range(P_WIDTH))
    first = order[::ROW_GRANULE]
    assert (first % ROW_GRANULE == 0).all() and (order.reshape(-1, ROW_GRANULE) == first[:, None] + np.arange(ROW_GRANULE)).all()
    return first // ROW_GRANULE


def _gather_rows(name, src, granules, per_step=8):
    n, cols = len(granules), src.shape[1]
    assert n % per_step == 0

    def body(tab_ref, *refs):
        out_ref = refs[per_step]
        for j in range(per_step):
            out_ref[j * ROW_GRANULE:(j + 1) * ROW_GRANULE, :] = refs[j][...]

    def pick(j):
        return pl.BlockSpec((ROW_GRANULE, cols), lambda i, tab_ref: (tab_ref[per_step * i + j], 0))

    return pl.pallas_call(
        body, name=name,
        grid_spec=pltpu.PrefetchScalarGridSpec(
            num_scalar_prefetch=1, grid=(n // per_step,),
            in_specs=[pick(j) for j in range(per_step)],
            out_specs=pl.BlockSpec((per_step * ROW_GRANULE, cols), lambda i, tab_ref: (i, 0))),
        out_shape=jax.ShapeDtypeStruct((n * ROW_GRANULE, cols), src.dtype),
        compiler_params=_params(("arbitrary",)),
    )(jnp.asarray(granules, jnp.int32), *([src] * per_step))


def _swa_rows(w_b):
    return w_b.reshape(2, 4, HEAD_DIM, -1).transpose(1, 0, 2, 3).reshape(SWA_Q_WIDTH, -1)


def _swa_rows_inv(dw_b):
    return dw_b.reshape(4, 2, HEAD_DIM, -1).transpose(1, 0, 2, 3).reshape(SWA_Q_WIDTH, -1)


def _rope_tables(pos):
    half = HEAD_DIM // 2
    inv = ROPE_THETA ** (-jnp.arange(half, dtype=F32) / half)
    ang = pos.astype(F32)[:, None] * jnp.tile(inv, 4)
    sign = jnp.repeat(jnp.array([-1.0, 1.0], F32), 2 * half)
    return jnp.cos(ang), jnp.sin(ang) * sign


def _local_step(x, hs, mem, pos, target, w_in_t, dep, rest_weights, on_grads, g_mix, g_cross, g_mem, g_mlp, g_final, sink):
    t = x.shape[0]
    tm = min(512, t)
    tq = 1024
    tw = min(2048, t)
    w_p_t = _gather_rows("w_p_rows", w_in_t, _w_p_granules())
    w_p = jnp.swapaxes(w_p_t, 0, 1)
    cos, sin = lax.optimization_barrier(_rope_tables(pos))
    sink_row = jnp.pad(sink.reshape(2, 4).T.reshape(1, 8), ((0, 0), (0, 120)))
    tabs = [(cos[None], sin[None])]
    for _, d in DIL_GROUPS[1:]:
        tabs.append(tuple(a.reshape(t // d, d, 128).swapaxes(0, 1) for a in (cos, sin)))
    tabs.append(tabs[0])

    h, h1, h2 = hs
    p0, p1, p2, pb = _inproj(h, h1, h2, w_p, [(cos, sin), tabs[1], tabs[2]], tm, dep)
    ps = [p0[None], p1, p2, pb[None]]
    outs, lses = [], []
    for gi, pv in enumerate(ps):
        res = _attn_fwd(f"attn_fwd{gi}", pv, gi == 3, sink_row[0, :8], min(tq, pv.shape[1]))
        outs.append(res[0])
        lses.append(res[1])
    o0, l0, ob, lb, ob32 = outs[0][0], lses[0][0], outs[3][0], lses[3][0], res[2][0]
    wts = rest_weights(GROUP_B, lb)
    w_b = _swa_rows(wts["w_branch_b"])
    tf = 2048
    w_g = wts["w_g"]
    gts = _gates(h, w_g, wts["b_gate"].reshape(1, GATE_WIDTH), min(1024, t), 1024)
    oa, ya, yb, merged, x1, hc = _mix(o0, outs[1], outs[2], l0, lses[1], lses[2], ob, gts, x,
                                      wts["w_branch_a"], w_b, wts["w_out"], g_cross, tm)
    mn, kv = _memkv(mem, g_mem, wts["w_ckv"])
    q, o, x2, hm = _cross(hc, x1, kv, wts["w_cq"], wts["w_co"], g_mlp, tm)
    wts.update(rest_weights(GROUP_A, hm))
    a, dx3, loss, dg_final = _mlp(hm, x2, wts["w_1"], wts["w_2"], g_final.reshape(1, D_MODEL), target, tm, tf)

    grads = {}
    dz, dx2, dg_mlp = _mlp_bwd(dx3, a, wts["w_1"], wts["w_2"], x2, g_mlp, tm, tf)
    grads["w_2"] = _shards_from_rows(_wgrad("dw_2", a, dx3, 1024, 1024, tw, square=True))
    grads["w_1"] = _wgrad("dw_1", hm, dz, 1024, 1024, tw, col_shards=True)
    dep = on_grads(GROUP_A, grads)
    dq, dx1, dkv, dg_cross = _cross_bwd(dx2, x1, q, kv, wts["w_cq"], wts["w_co"], g_cross, tm, dep)
    grads["w_co"] = _shards_from_rows(_wgrad("dw_co", o, dx2, 1024, 1024, tw))
    grads["w_cq"] = _shards_from_rows(_wgrad("dw_cq", hc, dq, 1024, 1024, tw))
    grads["w_ckv"], dg_mem = _memkv_bwd(dkv, mn, mem, wts["w_ckv"], g_mem)
    dgt, dh_part, dya, dyb, db_gate = _merge_bwd(dx1, ya, yb, gts, wts["w_out"], w_g, tm)
    do0, do1, do2, c0, c1, c2, dob, cb, dsink = _combine_bwd(
        dya, dyb, oa, ob32, l0, lses[1], lses[2], lb, sink_row, wts["w_branch_a"], w_b, tm)
    grads["w_out"] = _shards_from_rows(_wgrad("dw_out", merged, dx1, 1024, 1024, tw))
    grads["w_branch_a"] = _shards_from_cols(_wgrad("dw_a", oa, dya, 512, 1024, tw))
    grads["w_branch_b"] = _shards_from_cols(_swa_rows_inv(_wgrad("dw_b", ob, dyb, 512, 1024, tw)))
    grads["b_gate"] = _shards_from_cols(db_gate.reshape(2, D_MODEL)).astype(BF16)
    dep = on_grads(GROUP_B, grads)
    dw_g_t = _wgrad("dw_g", dgt, h, 1024, 1024, tw)
    dps = []
    for gi, (pv, do_g, c_g) in enumerate(zip(ps, (do0[None], do1, do2, dob[None]), (c0[None], c1, c2, cb[None]))):
        dps.append(_attn_bwd(f"attn_bwd{gi}", pv, do_g, lses[gi], c_g, tabs[gi][0], tabs[gi][1], gi == 3,
                             min(tq, pv.shape[1]),
                             dep if gi == 0 else None))
    dw_p_t = [_wgrad(f"dw_p{gi}", dpg.reshape(t, -1), hh.reshape(t, D_MODEL), PBLK, 1024, tw)
              for gi, (hh, dpg) in enumerate(zip((h, h1, h2, h), dps))]
    back = np.concatenate([np.argsort(_w_p_granules()),
                           np.arange(P_WIDTH // ROW_GRANULE, IN_WIDTH // ROW_GRANULE)])
    dw_in_t = _gather_rows("dw_in_rows", jnp.concatenate(dw_p_t + [dw_g_t], axis=0), back)
    grads["w_in"] = dw_in_t.reshape(N_DEV, IN_WIDTH // N_DEV, D_MODEL)
    dep = on_grads(GROUP_C, grads)
    grad_x, dg_mix = _dx(dps[0][0], dps[1], dps[2], dps[3][0], w_p_t, dh_part, dx1, x, g_mix, tm, dep)
    dsink_heads = dsink[0, :8].reshape(4, 2).T.reshape(8)
    small = {"g_mix": dg_mix[0], "g_cross": dg_cross[0], "g_mem": dg_mem[0], "g_mlp": dg_mlp[0],
             "g_final": dg_final[0], "sink": dsink_heads}
    return loss[0, 0], grad_x, small


def kernel(x, mem, positions, g_mix, w_in, b_gate, sink, w_branch_a, w_branch_b, w_out, g_cross, g_mem, w_cq, w_ckv, w_co, g_mlp, w_1, w_2, g_final, loss_target, m_g_mix, m_w_in, m_b_gate, m_sink, m_w_branch_a, m_w_branch_b, m_w_out, m_g_cross, m_g_mem, m_w_cq, m_w_ckv, m_w_co, m_g_mlp, m_w_1, m_w_2, m_g_final, v_g_mix, v_w_in, v_b_gate, v_sink, v_w_branch_a, v_w_branch_b, v_w_out, v_g_cross, v_g_mem, v_w_cq, v_w_ckv, v_w_co, v_g_mlp, v_w_1, v_w_2, v_g_final):
    local = dict(locals())
    shard = {n: local[n][0] for n in GROUP_A + GROUP_B + GROUP_C}
    me = _my_index()
    me_arr = me.reshape(1).astype(jnp.int32)
    tags = {GROUP_A: "a", GROUP_B: "b", GROUP_C: "c"}

    transposed = lambda a: jnp.swapaxes(a, 0, 1)
    gathered_w_in, *hs = _all_gather(transposed(shard["w_in"]).astype(BF16), x[0], g_mix, min(512, x.shape[1]))
    w_in_t = gathered_w_in.reshape(-1, gathered_w_in.shape[2])

    def gathered(name, started, after):
        srcs, lands = _split_wait(name, True, started, after)
        return [lax.dynamic_update_slice(land, src[None], (me,) + (0,) * src.ndim) for src, land in zip(srcs, lands)]

    def start_gather(names, after=None):
        return _split_start("gather_start_" + tags[names], True,
                            [shard[n] if n == "b_gate" else shard[n].astype(BF16) for n in names], after)

    gathers = {GROUP_B: start_gather(GROUP_B)}
    gathers[GROUP_A] = start_gather(GROUP_A, gathers[GROUP_B][-1])

    def rest_weights(names, after):
        full = {"w_g": transposed(w_in_t[P_WIDTH:])} if names == GROUP_B else {}
        for name, a in zip(names, gathered("gather_wait_" + tags[names], gathers[names], after)):
            if name in ("w_1", "w_ckv"):
                full[name] = a
            elif name in _COL_SHARDED:
                full[name] = _cols_from_shards(a)
            else:
                full[name] = a.reshape(N_DEV * a.shape[1], a.shape[2])
        return full

    scatters = {}

    def on_grads(names, grads):
        scatters[names] = _split_start("scatter_start_" + tags[names], False, [grads[n] for n in names])
        return scatters[names][-1]

    loss, grad_x, small = _local_step(
        x[0], hs, mem[0], positions[0], loss_target[0], w_in_t, gathers[GROUP_A][-1], rest_weights, on_grads,
        g_mix, g_cross, g_mem, g_mlp, g_final, sink[0])

    sp = jnp.stack([small[n] if n != "sink" else jnp.pad(small[n], (0, LANES - 8)) for n in SMALL]
                   + [jnp.pad(loss.reshape(1), (0, LANES - 1)), jnp.zeros((LANES,), F32)])
    small_gather = _split_start("small_start", True, [sp])

    after, updated = small_gather[-1], {}
    for names in (GROUP_A, GROUP_B, GROUP_C):
        sent, got = _split_wait("scatter_wait_" + tags[names], False, scatters[names], after)
        for i, name in enumerate(names):
            view = transposed if name == "w_in" else (lambda a: a)
            outs = _adamw("adamw_" + name, me_arr, sent[i], got[i], view(shard[name]),
                          view(local["m_" + name][0]), view(local["v_" + name][0]), ADAM_ROWS[name], ADAM_COLS.get(name))
            updated[name] = [view(a)[None] for a in outs]
            after = outs[3]

    flat = lambda prefix: [local[prefix + n].reshape(1, -1) for n in SMALL]
    outs, loss_row = _adamw_small(gathered("small_wait", small_gather, after)[0], flat(""), flat("m_"), flat("v_"))
    for i, name in enumerate(SMALL):
        updated[name] = [outs[which][i].reshape(local[name].shape) for which in range(4)]

    order = ["g_mix", "w_in", "b_gate", "sink", "w_branch_a", "w_branch_b", "w_out", "g_cross", "g_mem", "w_cq",
             "w_ckv", "w_co", "g_mlp", "w_1", "w_2", "g_final"]
    res = [loss_row[0, 0], grad_x[None]]
    for which in range(4):
        res += [updated[n][which] for n in order]
    return tuple(res)
```

```python
import functools
import math

import jax
import jax.numpy as jnp
import numpy as np
from jax import lax
from jax.experimental import pallas as pl
from jax.experimental.pallas import tpu as pltpu

F32 = jnp.float32
BF16 = jnp.bfloat16

D_MODEL = 1024
HEAD_DIM = 64
DIL_GROUPS = ((128, 1), (512, 4), (2048, 16))
ROPE_THETA = 10000.0
X_HEADS = 4
X_HEAD_DIM = D_MODEL // X_HEADS
D_FF = 4 * D_MODEL
EPS = 1e-6
DIL_WIDTH = 1536
SWA_Q_WIDTH = 512
SWA_KV_WIDTH = 128
P_WIDTH = 3 * DIL_WIDTH + SWA_Q_WIDTH + 2 * SWA_KV_WIDTH
GATE_WIDTH = 2 * D_MODEL
IN_WIDTH = P_WIDTH + GATE_WIDTH
BAND = 128
PBLK = 768
Q_SCALE = HEAD_DIM ** -0.5
X_SCALE = X_HEAD_DIM ** -0.5

ADAM_LR = 0.001
ADAM_B1 = 0.9
ADAM_B2 = 0.999
ADAM_EPS = 1e-08
ADAM_WD = 0.01
ADAM_STEP = 10

N_DEV = 8
LANES = 1024
VMEM_LIMIT = 52 * 1024 * 1024

NT = (((1,), (1,)), ((), ()))
TN = (((0,), (0,)), ((), ()))

GROUP_A = ("w_1", "w_2")
GROUP_B = ("w_branch_a", "w_branch_b", "w_out", "w_cq", "w_ckv", "w_co", "b_gate")
GROUP_C = ("w_in",)
_COL_SHARDED = ("w_in", "w_branch_a", "w_branch_b", "w_ckv", "w_1", "b_gate")
ADAM_ROWS = {"w_in": 464, "w_branch_a": 512, "w_branch_b": 512, "w_out": 128, "w_cq": 128, "w_ckv": 512,
             "w_co": 128, "w_1": 256, "w_2": 256, "b_gate": 2}
ADAM_COLS = {"w_in": 256}
SMALL = ("g_mix", "g_cross", "g_mem", "g_mlp", "g_final", "sink")


def _params(sem=None):
    return pltpu.CompilerParams(dimension_semantics=sem, vmem_limit_bytes=VMEM_LIMIT)


def _dot(a, b):
    return jnp.dot(a, b, preferred_element_type=F32)


def _dot_nt(a, b):
    return lax.dot_general(a, b, NT, preferred_element_type=F32)


def _dot_tn(a, b):
    return lax.dot_general(a, b, TN, preferred_element_type=F32)


def _rms(xt):
    return lax.rsqrt(jnp.mean(xt * xt, axis=-1, keepdims=True) + EPS)


def _rms_bwd(dh, xt, r, g):
    xn = xt * r
    dxn = dh * g
    dx = r * (dxn - xn * jnp.mean(dxn * xn, axis=-1, keepdims=True))
    return dx, jnp.sum(dh * xn, axis=0, keepdims=True)


def _rope(x, c, s, swa, sign):
    kinds = "qqqqkv" if swa else "qkvqkv"
    cq, sq = c * Q_SCALE, s * (sign * Q_SCALE)
    sk = s * sign if sign != 1 else s
    out = []
    for ci, kind in enumerate(kinds):
        xc = x[:, ci * 128:(ci + 1) * 128]
        if kind == "v":
            out.append(xc)
        elif kind == "q":
            out.append(xc * cq + pltpu.roll(xc, 64, 1) * sq)
        else:
            out.append(xc * c + pltpu.roll(xc, 64, 1) * sk)
    return jnp.concatenate(out, axis=1)


def _lane_scratch(rows, w):
    return pltpu.VMEM((w // 128, rows, 128), F32)


def _deinterleave(val, scr_ref, dst_ref, dtype):
    d, n = dst_ref.shape[0], dst_ref.shape[1]
    nc = val.shape[1] // 128
    for c in range(nc):
        scr_ref[c] = val[:, c * 128:(c + 1) * 128]
    for r in range(d):
        rows = [scr_ref.at[c][pl.ds(r, n, stride=d), :] for c in range(nc)]
        dst_ref[r] = jnp.concatenate(rows, axis=1).astype(dtype)


def _res_spec(a, tm):
    d, w = a.shape[0], a.shape[2]
    return pl.BlockSpec((d, tm // d, w), lambda i: (0, i, 0))


def _interleave(src_ref, scr_ref):
    d, n = src_ref.shape[0], src_ref.shape[1]
    nc = src_ref.shape[2] // 128
    for r in range(d):
        v = src_ref[r].astype(F32)
        for c in range(nc):
            scr_ref.at[c][pl.ds(r, n, stride=d), :] = v[:, c * 128:(c + 1) * 128]
    return jnp.concatenate([scr_ref[c] for c in range(nc)], axis=1)


def _with_dep(body, n_in, dep):
    if dep is None:
        return body
    return lambda *refs: body(*refs[:n_in], *refs[n_in + 1:])


def _dep_spec(dep):
    return [] if dep is None else [pl.BlockSpec(memory_space=pl.ANY)]


def _dep_arg(dep):
    return [] if dep is None else [dep]


def _inproj(h, h1, h2, w_p, tabs, tm, dep=None):
    t = h.shape[0]
    gw = 2 * PBLK
    (cos, sin), (cos1, sin1), (cos2, sin2) = tabs[0], tabs[1], tabs[2]

    def body(h_ref, h1_ref, h2_ref, w_ref, c_ref, s_ref, c1_ref, s1_ref, c2_ref, s2_ref,
             p0_ref, p1_ref, p2_ref, pb_ref):
        rows = lambda ref: ref[...].reshape(tm, ref.shape[-1])
        groups = ((h_ref, c_ref, s_ref, p0_ref), (h1_ref, c1_ref, s1_ref, p1_ref), (h2_ref, c2_ref, s2_ref, p2_ref))
        for gi, (lhs_ref, cc_ref, ss_ref, out_ref) in enumerate(groups):
            lhs, cc, ss = rows(lhs_ref), rows(cc_ref), rows(ss_ref)
            for half in range(2):
                col = gi * gw + half * PBLK
                val = _rope(_dot(lhs, w_ref[:, col:col + PBLK]), cc, ss, False, 1).astype(BF16)
                if out_ref.ndim == 3:
                    out_ref[:, :, half * PBLK:(half + 1) * PBLK] = val.reshape(out_ref.shape[:2] + (PBLK,))
                else:
                    out_ref[:, half * PBLK:(half + 1) * PBLK] = val
        pb_ref[...] = _rope(_dot(h_ref[...], w_ref[:, 3 * gw:]), c_ref[...], s_ref[...], True, 1).astype(BF16)

    d1, d2 = DIL_GROUPS[1][1], DIL_GROUPS[2][1]
    row = lambda w: pl.BlockSpec((tm, w), lambda i: (i, 0))
    res = lambda d, w: pl.BlockSpec((d, tm // d, w), lambda i: (0, i, 0))
    sds = jax.ShapeDtypeStruct
    return pl.pallas_call(
        _with_dep(body, 10, dep), name="inproj", grid=(t // tm,),
        in_specs=[row(D_MODEL), res(d1, D_MODEL), res(d2, D_MODEL),
                  pl.BlockSpec((D_MODEL, P_WIDTH), lambda i: (0, 0), pipeline_mode=pl.Buffered(1)),
                  row(128), row(128), res(d1, 128), res(d1, 128), res(d2, 128), res(d2, 128)] + _dep_spec(dep),
        out_specs=[row(gw), res(d1, gw), res(d2, gw), row(PBLK)],
        out_shape=[sds((t, gw), BF16), sds((d1, t // d1, gw), BF16), sds((d2, t // d2, gw), BF16),
                   sds((t, PBLK), BF16)],
        compiler_params=_params(("arbitrary",)),
    )(h, h1, h2, w_p, cos, sin, cos1, sin1, cos2, sin2, *_dep_arg(dep))


def _gates(h, w_g, b, tm, tn):
    t = h.shape[0]

    def body(h_ref, w_ref, b_ref, o_ref):
        z = _dot(h_ref[...], w_ref[...]) + b_ref[...]
        o_ref[...] = (0.5 * jnp.tanh(0.5 * z) + 0.5).astype(BF16)

    return pl.pallas_call(
        body, name="gates", grid=(t // tm, GATE_WIDTH // tn),
        in_specs=[pl.BlockSpec((tm, D_MODEL), lambda i, j: (i, 0)),
                  pl.BlockSpec((D_MODEL, tn), lambda i, j: (0, j)),
                  pl.BlockSpec((1, tn), lambda i, j: (0, j))],
        out_specs=pl.BlockSpec((tm, tn), lambda i, j: (i, j)),
        out_shape=jax.ShapeDtypeStruct((t, GATE_WIDTH), BF16),
        compiler_params=_params(("arbitrary", "arbitrary")),
    )(h, w_g, b)


def _band_mask(i, s):
    row = lax.broadcasted_iota(jnp.int32, (BAND, 2 * BAND), 0)
    col = lax.broadcasted_iota(jnp.int32, (BAND, 2 * BAND), 1)
    band = (col >= row) & (col <= row + BAND)
    if s == 0:
        band = band & ((col >= BAND) | (i > 0))
    return band


def _head_a_masks(rows):
    lane = lax.broadcasted_iota(jnp.int32, (rows, 128), 1)
    return (lane % HEAD_DIM) < HEAD_DIM // 2, lane < HEAD_DIM


def _stack_heads(x, head_a):
    zero = jnp.zeros_like(x)
    return jnp.concatenate([jnp.where(head_a, x, zero), jnp.where(head_a, zero, x)], axis=0)


def _stack_heads_t(xt, head_a_t):
    zero = jnp.zeros_like(xt)
    return jnp.concatenate([jnp.where(head_a_t, xt, zero), jnp.where(head_a_t, zero, xt)], axis=1)


def _kv_rows(cur_ref, tail_ref, s, off):
    if s == 0:
        return jnp.concatenate([tail_ref[:, off:off + 128], cur_ref[0:BAND, off:off + 128]], axis=0)
    return cur_ref[(s - 1) * BAND:(s + 1) * BAND, off:off + 128]


def _attn_layout(swa):
    if swa:
        return [(128 * j, 512, 640) for j in range(4)]
    return [(0, 128, 256), (384, 512, 640)]


def _attn_fwd(name, pv, swa, sinks, tq):
    d, ls = pv.shape[0], pv.shape[1]
    n, nsb = ls // tq, tq // BAND
    pairs = _attn_layout(swa)
    ncol = 1 if swa else 2
    ow = 128 * len(pairs)

    def body(cur_ref, tail_ref, *rest):
        sink_ref, o_ref, lse_ref, o32_ref = rest if swa else (None,) + rest + (None,)
        i = pl.program_id(2)
        lane = lax.broadcasted_iota(jnp.int32, (BAND, 128), 1)
        qk_a, v_a = _head_a_masks(BAND)
        first = lax.broadcasted_iota(jnp.int32, (2 * BAND, 1), 0) < BAND
        for s in range(nsb):
            mask = _band_mask(i, s)
            mask2 = jnp.concatenate([mask, mask], axis=0)
            rows = slice(s * BAND, (s + 1) * BAND)
            lse_tile = jnp.zeros((BAND, 128), F32)
            for j, (qo, ko, vo) in enumerate(pairs):
                q = cur_ref[rows, qo:qo + 128]
                kk = _kv_rows(cur_ref, tail_ref, s, ko)
                vv = _kv_rows(cur_ref, tail_ref, s, vo)
                sc = _dot_nt(_stack_heads(q, qk_a), kk)
                sc = jnp.where(mask2, sc, -jnp.inf)
                m = jnp.max(sc, axis=-1, keepdims=True)
                if swa:
                    sk = jnp.where(first, sink_ref[2 * j], sink_ref[2 * j + 1])
                    m = jnp.maximum(m, sk)
                p = jnp.exp(sc - m)
                den = jnp.sum(p, axis=-1, keepdims=True)
                if swa:
                    den = den + jnp.exp(sk - m)
                lse = m + jnp.log(den)
                lse_tile = jnp.where(lane == 2 * j, lse[:BAND], jnp.where(lane == 2 * j + 1, lse[BAND:], lse_tile))
                o2 = _dot(p.astype(BF16), vv) * (1.0 / den)
                o = jnp.where(v_a, o2[:BAND], o2[BAND:])
                o_ref[rows, j * 128:(j + 1) * 128] = o.astype(BF16)
                if swa:
                    o32_ref[rows, j * 128:(j + 1) * 128] = o
            lse_ref[rows, :] = lse_tile

    in_specs = [pl.BlockSpec((None, tq, PBLK), lambda r, cb, i: (r, i, cb)),
                pl.BlockSpec((None, BAND, PBLK), lambda r, cb, i: (r, jnp.maximum(i * nsb - 1, 0), cb))]
    args = [pv, pv]
    out_specs = [pl.BlockSpec((None, tq, ow), lambda r, cb, i: (r, i, cb)),
                 pl.BlockSpec((None, tq, 128), lambda r, cb, i: (r, i, cb))]
    out_shape = [jax.ShapeDtypeStruct((d, ls, 512), BF16), jax.ShapeDtypeStruct((d, ls, 128 * ncol), F32)]
    if swa:
        in_specs.append(pl.BlockSpec(memory_space=pltpu.SMEM))
        args.append(sinks)
        out_specs.append(out_specs[0])
        out_shape.append(jax.ShapeDtypeStruct((d, ls, 512), F32))
    return pl.pallas_call(
        body, name=name, grid=(d, ncol, n),
        in_specs=in_specs, out_specs=out_specs, out_shape=out_shape,
        compiler_params=_params(("arbitrary", "arbitrary", "arbitrary")),
    )(*args)


def _lse_lane(head):
    return (head // 4) * 128 + head % 4


def _dil_head_spread():
    lane = lax.broadcasted_iota(jnp.int32, (256, 512), 0)
    head = lax.broadcasted_iota(jnp.int32, (256, 512), 1) // HEAD_DIM
    return (lane == _lse_lane(head)).astype(BF16)


def _head_scale(x, tile, spread):
    return x * _dot(tile.astype(BF16), spread)


def _head_gather(width, dil):
    head = lax.broadcasted_iota(jnp.int32, (8 * HEAD_DIM, width), 0) // HEAD_DIM
    lane = lax.broadcasted_iota(jnp.int32, (8 * HEAD_DIM, width), 1)
    return (lane == (_lse_lane(head) if dil else head)).astype(BF16)


def _head_sums(x, gather):
    hi = x.astype(BF16)
    lo = (x - hi.astype(F32)).astype(BF16)
    return _dot(hi, gather) + _dot(lo, gather)


def _alphas(l0, l1, l2):
    m = jnp.maximum(jnp.maximum(l0, l1), l2)
    e0, e1, e2 = jnp.exp(l0 - m), jnp.exp(l1 - m), jnp.exp(l2 - m)
    den = e0 + e1 + e2
    return e0 / den, e1 / den, e2 / den


def _mix(o0, o1, o2, l0, l1, l2, ob, gts, x, w_a, w_b, w_out, g_cross, tm):
    t = x.shape[0]

    def body(o0_ref, o1_ref, o2_ref, l0_ref, l1_ref, l2_ref, ob_ref, g_ref, x_ref, wa_ref, wb_ref, wo_ref,
             gc_ref, oa_ref, ya_ref, yb_ref, mg_ref, x1_ref, hc_ref, so_ref, sl_ref):
        a0, a1, a2 = _alphas(l0_ref[...], _interleave(l1_ref, sl_ref), _interleave(l2_ref, sl_ref))
        spread = _dil_head_spread()
        oa = (_head_scale(o0_ref[...].astype(F32), a0, spread)
              + _head_scale(_interleave(o1_ref, so_ref), a1, spread)
              + _head_scale(_interleave(o2_ref, so_ref), a2, spread))
        oab = oa.astype(BF16)
        oa_ref[...] = oab
        ya = _dot(oab, wa_ref[...])
        yb = _dot(ob_ref[...], wb_ref[...])
        ya_ref[...] = ya.astype(BF16)
        yb_ref[...] = yb.astype(BF16)
        merged = (g_ref[:, :D_MODEL].astype(F32) * ya + g_ref[:, D_MODEL:].astype(F32) * yb).astype(BF16)
        mg_ref[...] = merged
        x1 = x_ref[...] + _dot(merged, wo_ref[...])
        x1_ref[...] = x1
        hc_ref[...] = (x1 * _rms(x1) * gc_ref[...]).astype(BF16)

    row = lambda w: pl.BlockSpec((tm, w), lambda i: (i, 0))
    full = lambda a, b: pl.BlockSpec((a, b), lambda i: (0, 0))
    return pl.pallas_call(
        body, name="mix", grid=(t // tm,),
        in_specs=[row(512), _res_spec(o1, tm), _res_spec(o2, tm), row(256), _res_spec(l1, tm), _res_spec(l2, tm),
                  row(512), row(GATE_WIDTH),
                  row(D_MODEL), full(512, D_MODEL), full(512, D_MODEL), full(D_MODEL, D_MODEL), full(1, D_MODEL)],
        out_specs=[row(512), row(D_MODEL), row(D_MODEL), row(D_MODEL), row(D_MODEL), row(D_MODEL)],
        out_shape=[jax.ShapeDtypeStruct((t, 512), BF16), jax.ShapeDtypeStruct((t, D_MODEL), BF16),
                   jax.ShapeDtypeStruct((t, D_MODEL), BF16), jax.ShapeDtypeStruct((t, D_MODEL), BF16),
                   jax.ShapeDtypeStruct((t, D_MODEL), F32), jax.ShapeDtypeStruct((t, D_MODEL), BF16)],
        scratch_shapes=[_lane_scratch(tm, 512), _lane_scratch(tm, 256)],
        compiler_params=_params(("arbitrary",)),
    )(o0, o1, o2, l0, l1, l2, ob, gts, x, w_a, w_b, w_out, g_cross)


def _memkv(mem, g_mem, w_ckv):
    m = mem.shape[0]
    ws = w_ckv.shape[2]

    def body(mem_ref, g_ref, w_ref, mn_ref, kv_ref):
        xt = mem_ref[...]
        mn = (xt * _rms(xt) * g_ref[...]).astype(BF16)
        mn_ref[...] = mn
        for j in range(N_DEV):
            kv_ref[:, j * ws:(j + 1) * ws] = _dot(mn, w_ref[j]).astype(BF16)

    return pl.pallas_call(
        body, name="memkv",
        out_shape=[jax.ShapeDtypeStruct((m, D_MODEL), BF16), jax.ShapeDtypeStruct((m, 2 * D_MODEL), BF16)],
        compiler_params=_params(),
    )(mem, g_mem, w_ckv)


def _cross_probs(q, kv_ref, h):
    k = kv_ref[:, h * X_HEAD_DIM:(h + 1) * X_HEAD_DIM]
    sc = _dot_nt(q[:, h * X_HEAD_DIM:(h + 1) * X_HEAD_DIM], k)
    m = jnp.max(sc, axis=-1, keepdims=True)
    p = jnp.exp(sc - m)
    return p / jnp.sum(p, axis=-1, keepdims=True)


def _cross(hc, x1, kv, w_cq, w_co, g_mlp, tm):
    t = x1.shape[0]
    m = kv.shape[0]

    def body(hc_ref, x1_ref, kv_ref, wq_ref, wo_ref, g_ref, q_ref, o_ref, x2_ref, hm_ref):
        q = (_dot(hc_ref[...], wq_ref[...]) * X_SCALE).astype(BF16)
        q_ref[...] = q
        outs = []
        for h in range(X_HEADS):
            p = _cross_probs(q, kv_ref, h)
            v = kv_ref[:, D_MODEL + h * X_HEAD_DIM:D_MODEL + (h + 1) * X_HEAD_DIM]
            outs.append(_dot(p.astype(BF16), v))
        o = jnp.concatenate(outs, axis=1).astype(BF16)
        o_ref[...] = o
        x2 = x1_ref[...] + _dot(o, wo_ref[...])
        x2_ref[...] = x2
        hm_ref[...] = (x2 * _rms(x2) * g_ref[...]).astype(BF16)

    row = lambda w: pl.BlockSpec((tm, w), lambda i: (i, 0))
    full = lambda a, b: pl.BlockSpec((a, b), lambda i: (0, 0))
    return pl.pallas_call(
        body, name="cross", grid=(t // tm,),
        in_specs=[row(D_MODEL), row(D_MODEL), full(m, 2 * D_MODEL), full(D_MODEL, D_MODEL),
                  full(D_MODEL, D_MODEL), full(1, D_MODEL)],
        out_specs=[row(D_MODEL)] * 4,
        out_shape=[jax.ShapeDtypeStruct((t, D_MODEL), BF16), jax.ShapeDtypeStruct((t, D_MODEL), BF16),
                   jax.ShapeDtypeStruct((t, D_MODEL), F32), jax.ShapeDtypeStruct((t, D_MODEL), BF16)],
        compiler_params=_params(("arbitrary",)),
    )(hc, x1, kv, w_cq, w_co, g_mlp)


def _mlp(hm, x2, w_1, w_2, g_final, target, tm, tf):
    t = x2.shape[0]
    nf = D_FF // tf

    def body(hm_ref, x2_ref, w1_ref, w2_ref, g_ref, tg_ref, a_ref, dx3_ref, loss_ref, dg_ref, acc_ref):
        i, f = pl.program_id(0), pl.program_id(1)
        hm_t = hm_ref[...]
        sw = w1_ref.shape[2]
        part = None
        for s in range(w1_ref.shape[0]):
            a = jnp.maximum(_dot(hm_t, w1_ref[s]), 0.0).astype(BF16)
            a_ref[:, s * sw:(s + 1) * sw] = a
            p_s = _dot(a * a, w2_ref[s * sw:(s + 1) * sw, :])
            part = p_s if part is None else part + p_s

        @pl.when(f == 0)
        def _():
            acc_ref[...] = part

        @pl.when(f > 0)
        def _():
            acc_ref[...] += part

        @pl.when((i == 0) & (f == 0))
        def _():
            loss_ref[...] = jnp.zeros_like(loss_ref)
            dg_ref[...] = jnp.zeros_like(dg_ref)

        @pl.when(f == nf - 1)
        def _():
            x3 = x2_ref[...] + acc_ref[...]
            r = _rms(x3)
            g = g_ref[...]
            diff = x3 * r * g - tg_ref[...]
            loss_ref[...] += 0.5 * jnp.sum(jnp.mean(diff * diff, axis=-1, keepdims=True))
            dx3, dg = _rms_bwd(diff / D_MODEL, x3, r, g)
            dx3_ref[...] = dx3
            dg_ref[...] += dg

    return pl.pallas_call(
        body, name="mlp", grid=(t // tm, nf),
        in_specs=[pl.BlockSpec((tm, D_MODEL), lambda i, f: (i, 0)),
                  pl.BlockSpec((tm, D_MODEL), lambda i, f: (i, 0)),
                  pl.BlockSpec((tf // w_1.shape[2], D_MODEL, w_1.shape[2]), lambda i, f: (f, 0, 0)),
                  pl.BlockSpec((tf, D_MODEL), lambda i, f: (f, 0)),
                  pl.BlockSpec((1, D_MODEL), lambda i, f: (0, 0)),
                  pl.BlockSpec((tm, D_MODEL), lambda i, f: (i, 0))],
        out_specs=[pl.BlockSpec((tm, tf), lambda i, f: (i, f)),
                   pl.BlockSpec((tm, D_MODEL), lambda i, f: (i, 0)),
                   pl.BlockSpec((1, 128), lambda i, f: (0, 0)),
                   pl.BlockSpec((1, D_MODEL), lambda i, f: (0, 0))],
        out_shape=[jax.ShapeDtypeStruct((t, D_FF), BF16), jax.ShapeDtypeStruct((t, D_MODEL), F32),
                   jax.ShapeDtypeStruct((1, 128), F32), jax.ShapeDtypeStruct((1, D_MODEL), F32)],
        scratch_shapes=[pltpu.VMEM((tm, D_MODEL), F32)],
        compiler_params=_params(("arbitrary", "arbitrary")),
    )(hm, x2, w_1, w_2, g_final, target)


def _mlp_bwd(dx3, a, w_1, w_2, x2, g_mlp, tm, tf):
    t = x2.shape[0]
    nf = D_FF // tf

    def body(dx3_ref, a_ref, w1_ref, w2_ref, x2_ref, g_ref, dz_ref, dx2_ref, dg_ref, acc_ref):
        i, f = pl.program_id(0), pl.program_id(1)
        dx3_b = dx3_ref[...].astype(BF16)
        sw = w1_ref.shape[2]
        part = None
        for s in range(w1_ref.shape[0]):
            cols = slice(s * sw, (s + 1) * sw)
            da2 = _dot_nt(dx3_b, w2_ref[cols, :])
            dz = (2.0 * a_ref[:, cols].astype(F32) * da2).astype(BF16)
            dz_ref[:, cols] = dz
            p_s = _dot_nt(dz, w1_ref[s])
            part = p_s if part is None else part + p_s

        @pl.when(f == 0)
        def _():
            acc_ref[...] = part

        @pl.when(f > 0)
        def _():
            acc_ref[...] += part

        @pl.when((i == 0) & (f == 0))
        def _():
            dg_ref[...] = jnp.zeros_like(dg_ref)

        @pl.when(f == nf - 1)
        def _():
            xt = x2_ref[...]
            dx, dg = _rms_bwd(acc_ref[...], xt, _rms(xt), g_ref[...])
            dx2_ref[...] = dx3_ref[...] + dx
            dg_ref[...] += dg

    return pl.pallas_call(
        body, name="mlp_bwd", grid=(t // tm, nf),
        in_specs=[pl.BlockSpec((tm, D_MODEL), lambda i, f: (i, 0)),
                  pl.BlockSpec((tm, tf), lambda i, f: (i, f)),
                  pl.BlockSpec((tf // w_1.shape[2], D_MODEL, w_1.shape[2]), lambda i, f: (f, 0, 0)),
                  pl.BlockSpec((tf, D_MODEL), lambda i, f: (f, 0)),
                  pl.BlockSpec((tm, D_MODEL), lambda i, f: (i, 0)),
                  pl.BlockSpec((1, D_MODEL), lambda i, f: (0, 0))],
        out_specs=[pl.BlockSpec((tm, tf), lambda i, f: (i, f)),
                   pl.BlockSpec((tm, D_MODEL), lambda i, f: (i, 0)),
                   pl.BlockSpec((1, D_MODEL), lambda i, f: (0, 0))],
        out_shape=[jax.ShapeDtypeStruct((t, D_FF), BF16), jax.ShapeDtypeStruct((t, D_MODEL), F32),
                   jax.ShapeDtypeStruct((1, D_MODEL), F32)],
        scratch_shapes=[pltpu.VMEM((tm, D_MODEL), F32)],
        compiler_params=_params(("arbitrary", "arbitrary")),
    )(dx3, a, w_1, w_2, x2, g_mlp)


def _wgrad(name, a, b, tka, tn, tm, square=False, col_shards=False):
    t, ka = a.shape
    n = b.shape[1]
    nk = t // tm

    def body(a_ref, b_ref, o_ref, acc_ref):
        at = a_ref[...].astype(BF16)
        if square:
            at = at * at
        part = _dot_tn(at, b_ref[...].astype(BF16))
        k = pl.program_id(2)

        @pl.when(k == 0)
        def _():
            acc_ref[...] = part

        @pl.when(k > 0)
        def _():
            acc_ref[...] += part

        @pl.when(k == nk - 1)
        def _():
            if col_shards:
                for s in range(tn // sw):
                    o_ref[s] = acc_ref[:, s * sw:(s + 1) * sw].astype(BF16)
            else:
                o_ref[...] = acc_ref[...].astype(BF16)

    if col_shards:
        sw = n // N_DEV
        out_spec = pl.BlockSpec((tn // sw, tka, sw), lambda p, q, k: (q, p, 0))
        out_shape = jax.ShapeDtypeStruct((N_DEV, ka, sw), BF16)
    else:
        out_spec = pl.BlockSpec((tka, tn), lambda p, q, k: (p, q))
        out_shape = jax.ShapeDtypeStruct((ka, n), BF16)
    return pl.pallas_call(
        body, name=name, grid=(ka // tka, n // tn, nk),
        in_specs=[pl.BlockSpec((tm, tka), lambda p, q, k: (k, p)),
                  pl.BlockSpec((tm, tn), lambda p, q, k: (k, q))],
        out_specs=out_spec, out_shape=out_shape,
        scratch_shapes=[pltpu.VMEM((tka, tn), F32)],
        compiler_params=_params(("arbitrary", "arbitrary", "arbitrary")),
    )(a, b)


def _cross_bwd(dx2, x1, q, kv, w_cq, w_co, g_cross, tm, dep=None):
    t = x1.shape[0]
    m = kv.shape[0]

    def body(dx2_ref, x1_ref, q_ref, kv_ref, wq_ref, wo_ref, g_ref, dq_ref, dx1_ref, dkv_ref, dg_ref):
        @pl.when(pl.program_id(0) == 0)
        def _():
            dkv_ref[...] = jnp.zeros_like(dkv_ref)
            dg_ref[...] = jnp.zeros_like(dg_ref)

        do = _dot_nt(dx2_ref[...].astype(BF16), wo_ref[...]).astype(BF16)
        q = q_ref[...]
        dqs = []
        for h in range(X_HEADS):
            hs = slice(h * X_HEAD_DIM, (h + 1) * X_HEAD_DIM)
            vs = slice(D_MODEL + h * X_HEAD_DIM, D_MODEL + (h + 1) * X_HEAD_DIM)
            p = _cross_probs(q, kv_ref, h)
            dp = _dot_nt(do[:, hs], kv_ref[:, vs])
            ds = (p * (dp - jnp.sum(dp * p, axis=-1, keepdims=True))).astype(BF16)
            dqs.append(_dot(ds, kv_ref[:, hs]))
            dkv_ref[:, hs] += _dot_tn(ds, q[:, hs])
            dkv_ref[:, vs] += _dot_tn(p.astype(BF16), do[:, hs])
        dq = (jnp.concatenate(dqs, axis=1) * X_SCALE).astype(BF16)
        dq_ref[...] = dq
        xt = x1_ref[...]
        dx, dg = _rms_bwd(_dot_nt(dq, wq_ref[...]), xt, _rms(xt), g_ref[...])
        dx1_ref[...] = dx2_ref[...] + dx
        dg_ref[...] += dg

    row = lambda w: pl.BlockSpec((tm, w), lambda i: (i, 0))
    full = lambda a, b: pl.BlockSpec((a, b), lambda i: (0, 0))
    return pl.pallas_call(
        _with_dep(body, 7, dep), name="cross_bwd", grid=(t // tm,),
        in_specs=[row(D_MODEL), row(D_MODEL), row(D_MODEL), full(m, 2 * D_MODEL), full(D_MODEL, D_MODEL),
                  full(D_MODEL, D_MODEL), full(1, D_MODEL)] + _dep_spec(dep),
        out_specs=[row(D_MODEL), row(D_MODEL), full(m, 2 * D_MODEL), full(1, D_MODEL)],
        out_shape=[jax.ShapeDtypeStruct((t, D_MODEL), BF16), jax.ShapeDtypeStruct((t, D_MODEL), F32),
                   jax.ShapeDtypeStruct((m, 2 * D_MODEL), F32), jax.ShapeDtypeStruct((1, D_MODEL), F32)],
        compiler_params=_params(("arbitrary",)),
    )(dx2, x1, q, kv, w_cq, w_co, g_cross, *_dep_arg(dep))


def _memkv_bwd(dkv, mn, mem, w_ckv, g_mem):
    ws = w_ckv.shape[2]

    def body(dkv_ref, mn_ref, mem_ref, w_ref, g_ref, dw_ref, dg_ref):
        mn = mn_ref[...]
        dmn = jnp.zeros(mn.shape, F32)
        for j in range(N_DEV):
            dkvb = dkv_ref[:, j * ws:(j + 1) * ws].astype(BF16)
            dw_ref[j] = _dot_tn(mn, dkvb).astype(BF16)
            dmn = dmn + _dot_nt(dkvb, w_ref[j])
        xt = mem_ref[...]
        dg_ref[...] = jnp.sum(dmn * xt * _rms(xt), axis=0, keepdims=True)

    return pl.pallas_call(
        body, name="memkv_bwd",
        out_shape=[jax.ShapeDtypeStruct(w_ckv.shape, BF16), jax.ShapeDtypeStruct((1, D_MODEL), F32)],
        compiler_params=_params(),
    )(dkv, mn, mem, w_ckv, g_mem)


def _merge_bwd(dx1, ya, yb, gts, w_out, w_g, tm):
    t = dx1.shape[0]

    def body(dx1_ref, ya_ref, yb_ref, g_ref, wo_ref, wg_ref, dg_ref, dhp_ref, dya_ref, dyb_ref, db_ref):
        @pl.when(pl.program_id(0) == 0)
        def _():
            db_ref[...] = jnp.zeros_like(db_ref)

        dm = _dot_nt(dx1_ref[...].astype(BF16), wo_ref[...])
        ga = g_ref[:, :D_MODEL].astype(F32)
        gb = g_ref[:, D_MODEL:].astype(F32)
        dya_ref[...] = (dm * ga).astype(BF16)
        dyb_ref[...] = (dm * gb).astype(BF16)
        dpa = dm * ya_ref[...].astype(F32) * ga * (1.0 - ga)
        dpb = dm * yb_ref[...].astype(F32) * gb * (1.0 - gb)
        dpre = jnp.concatenate([dpa, dpb], axis=1)
        db_ref[...] += jnp.sum(dpre, axis=0, keepdims=True)
        dpreb = dpre.astype(BF16)
        dg_ref[...] = dpreb
        dhp_ref[...] = _dot_nt(dpreb, wg_ref[...])

    row = lambda w: pl.BlockSpec((tm, w), lambda i: (i, 0))
    once = lambda a, b: pl.BlockSpec((a, b), lambda i: (0, 0), pipeline_mode=pl.Buffered(1))
    sds = jax.ShapeDtypeStruct
    return pl.pallas_call(
        body, name="merge_bwd", grid=(t // tm,),
        in_specs=[row(D_MODEL), row(D_MODEL), row(D_MODEL), row(GATE_WIDTH),
                  once(D_MODEL, D_MODEL), once(D_MODEL, GATE_WIDTH)],
        out_specs=[row(GATE_WIDTH), row(D_MODEL), row(D_MODEL), row(D_MODEL),
                   pl.BlockSpec((1, GATE_WIDTH), lambda i: (0, 0))],
        out_shape=[sds((t, GATE_WIDTH), BF16), sds((t, D_MODEL), F32), sds((t, D_MODEL), BF16),
                   sds((t, D_MODEL), BF16), sds((1, GATE_WIDTH), F32)],
        compiler_params=_params(("arbitrary",)),
    )(dx1, ya, yb, gts, w_out, w_g)


def _combine_bwd(dya, dyb, oa, ob, l0, l1, l2, lb, sink_row, w_a, w_b, tm):
    t = dya.shape[0]

    def body(dya_ref, dyb_ref, oa_ref, ob_ref, l0_ref, l1_ref, l2_ref, lb_ref, sk_ref, wa_ref, wb_ref,
             do0_ref, do1_ref, do2_ref, c0_ref, c1_ref, c2_ref, dob_ref, cb_ref, dsk_ref, so_ref, sl_ref):
        @pl.when(pl.program_id(0) == 0)
        def _():
            dsk_ref[...] = jnp.zeros_like(dsk_ref)

        doa = _dot_nt(dya_ref[...], wa_ref[...])
        dob = _dot_nt(dyb_ref[...], wb_ref[...])
        dsum = _head_sums(doa * oa_ref[...].astype(F32), _head_gather(256, True))
        a0, a1, a2 = _alphas(l0_ref[...], _interleave(l1_ref, sl_ref), _interleave(l2_ref, sl_ref))
        c0_ref[...] = a0 * dsum
        spread = _dil_head_spread()
        do0_ref[...] = _head_scale(doa, a0, spread).astype(BF16)
        for al, do_ref, c_ref in ((a1, do1_ref, c1_ref), (a2, do2_ref, c2_ref)):
            _deinterleave(al * dsum, sl_ref, c_ref, F32)
            _deinterleave(_head_scale(doa, al, spread), so_ref, do_ref, BF16)
        dob_ref[...] = dob.astype(BF16)
        cb = _head_sums(dob * ob_ref[...], _head_gather(128, False))
        cb_ref[...] = cb
        lane = lax.broadcasted_iota(jnp.int32, cb.shape, 1)
        psink = jnp.where(lane < 8, jnp.exp(sk_ref[...] - lb_ref[...]), 0.0)
        dsk_ref[...] += jnp.sum(-psink * cb, axis=0, keepdims=True)

    row = lambda w: pl.BlockSpec((tm, w), lambda i: (i, 0))
    full = lambda a, b: pl.BlockSpec((a, b), lambda i: (0, 0))
    sds = jax.ShapeDtypeStruct
    d1, d2 = l1.shape[0], l2.shape[0]
    res = lambda d, w: pl.BlockSpec((d, tm // d, w), lambda i: (0, i, 0))
    return pl.pallas_call(
        body, name="combine_bwd", grid=(t // tm,),
        in_specs=[row(D_MODEL), row(D_MODEL), row(512), row(512),
                  row(256), _res_spec(l1, tm), _res_spec(l2, tm), row(128), full(1, 128),
                  full(512, D_MODEL), full(512, D_MODEL)],
        out_specs=[row(512), res(d1, 512), res(d2, 512), row(256), res(d1, 256), res(d2, 256),
                   row(512), row(128), full(1, 128)],
        out_shape=[sds((t, 512), BF16), sds((d1, t // d1, 512), BF16),
                   sds((d2, t // d2, 512), BF16), sds((t, 256), F32), sds((d1, t // d1, 256), F32),
                   sds((d2, t // d2, 256), F32), sds((t, 512), BF16),
                   sds((t, 128), F32), sds((1, 128), F32)],
        scratch_shapes=[_lane_scratch(tm, 512), _lane_scratch(tm, 256)],
        compiler_params=_params(("arbitrary",)),
    )(dya, dyb, oa, ob, l0, l1, l2, lb, sink_row, w_a, w_b)


def _attn_bwd(name, pv, dov, lsev, cv, cosv, sinv, swa, tq, dep=None):
    d, ls = pv.shape[0], pv.shape[1]
    n, nsb = ls // tq, tq // BAND
    pairs = _attn_layout(swa)
    ncol = 1 if swa else 2
    ow = 128 * len(pairs)

    kv_slots = sorted({(ko, vo) for _, ko, vo in pairs})

    def body(cur_ref, tail_ref, do_ref, lse_ref, c_ref, cos_ref, sin_ref, out_ref, acc_ref, carry_ref, acct_ref):
        i = pl.program_id(2)
        blk_i = n - 1 - i
        acc_ref[...] = jnp.zeros_like(acc_ref)
        acct_ref[...] = jnp.zeros_like(acct_ref)

        @pl.when(i == 0)
        def _():
            carry_ref[...] = jnp.zeros_like(carry_ref)

        qk_a, v_a = _head_a_masks(BAND)
        dim = lax.broadcasted_iota(jnp.int32, (128, BAND), 0)
        qk_at, v_at = (dim % HEAD_DIM) < HEAD_DIM // 2, dim < HEAD_DIM
        for s in range(nsb):
            mask = _band_mask(blk_i, s)
            mask2 = jnp.concatenate([mask, mask], axis=0)
            rows = slice(s * BAND, (s + 1) * BAND)
            kcols = slice(s * BAND, (s + 2) * BAND)
            for j, (qo, ko, vo) in enumerate(pairs):
                slot = kv_slots.index((ko, vo))
                kk = _kv_rows(cur_ref, tail_ref, s, ko)
                vv = _kv_rows(cur_ref, tail_ref, s, vo)
                q, do = cur_ref[rows, qo:qo + 128], do_ref[rows, j * 128:(j + 1) * 128]
                q2, do2 = _stack_heads(q, qk_a), _stack_heads(do, v_a)
                col2 = lambda ref: jnp.concatenate([ref[rows, 2 * j:2 * j + 1], ref[rows, 2 * j + 1:2 * j + 2]], axis=0)
                sc = _dot_nt(q2, kk)
                p = jnp.exp(jnp.where(mask2, sc, -jnp.inf) - col2(lse_ref))
                dp = _dot_nt(do2, vv)
                ds = (p * (dp - col2(c_ref))).astype(BF16)
                dq2 = _dot(ds, kk)
                acc_ref[BAND + s * BAND:BAND + (s + 1) * BAND, qo:qo + 128] += jnp.where(qk_a, dq2[:BAND], dq2[BAND:])
                acct_ref[2 * slot, :, kcols] += _dot(_stack_heads_t(q.T, qk_at), ds)
                acct_ref[2 * slot + 1, :, kcols] += _dot(_stack_heads_t(do.T, v_at), p.astype(BF16))
        for slot, (ko, vo) in enumerate(kv_slots):
            acc_ref[:, ko:ko + 128] += acct_ref[2 * slot].T
            acc_ref[:, vo:vo + 128] += acct_ref[2 * slot + 1].T

        last = acc_ref[tq:, :] + carry_ref[...]
        fin = last if tq == BAND else jnp.concatenate([acc_ref[BAND:tq, :], last], axis=0)
        out_ref[...] = _rope(fin, cos_ref[...], sin_ref[...], swa, -1).astype(BF16)
        carry_ref[...] = acc_ref[0:BAND, :]

    rev = lambda i: n - 1 - i
    blk = lambda rows, w, row_of: pl.BlockSpec((None, rows, w), lambda r, cb, i: (r, row_of(i), cb))
    tab = pl.BlockSpec((None, tq, 128), lambda r, cb, i: (r, rev(i), 0))
    return pl.pallas_call(
        _with_dep(body, 7, dep), name=name, grid=(d, ncol, n),
        in_specs=[blk(tq, PBLK, rev), blk(BAND, PBLK, lambda i: jnp.maximum(rev(i) * nsb - 1, 0)),
                  blk(tq, ow, rev), blk(tq, 128, rev), blk(tq, 128, rev), tab, tab] + _dep_spec(dep),
        out_specs=blk(tq, PBLK, rev),
        out_shape=jax.ShapeDtypeStruct((d, ls, ncol * PBLK), BF16),
        scratch_shapes=[pltpu.VMEM((tq + BAND, PBLK), F32), pltpu.VMEM((BAND, PBLK), F32),
                        pltpu.VMEM((2 * len(kv_slots), 128, tq + BAND), F32)],
        compiler_params=_params(("arbitrary", "arbitrary", "arbitrary")),
    )(pv, pv, dov, lsev, cv, cosv, sinv, *_dep_arg(dep))


def _dx(dp0, dp1, dp2, dpb, w_p_t, dh_part, dx1, x, g_mix, tm, dep=None):
    t = x.shape[0]
    gw = 2 * PBLK

    def body(dp0_ref, dp1_ref, dp2_ref, dpb_ref, w_ref, dhp_ref, dx1_ref, x_ref, g_ref, gx_ref, dg_ref,
             dpt_ref, scr_ref):
        @pl.when(pl.program_id(0) == 0)
        def _():
            dg_ref[...] = jnp.zeros_like(dg_ref)

        dpt_ref[:, 0:gw] = dp0_ref[...]
        dpt_ref[:, gw:2 * gw] = _interleave(dp1_ref, scr_ref).astype(BF16)
        dpt_ref[:, 2 * gw:3 * gw] = _interleave(dp2_ref, scr_ref).astype(BF16)
        dpt_ref[:, 3 * gw:] = dpb_ref[...]
        dh = _dot(dpt_ref[...], w_ref[...]) + dhp_ref[...]
        xt = x_ref[...]
        dx, dg = _rms_bwd(dh, xt, _rms(xt), g_ref[...])
        gx_ref[...] = dx1_ref[...] + dx
        dg_ref[...] += dg

    row = lambda w: pl.BlockSpec((tm, w), lambda i: (i, 0))
    full = lambda a, b: pl.BlockSpec((a, b), lambda i: (0, 0))
    return pl.pallas_call(
        _with_dep(body, 9, dep), name="dx", grid=(t // tm,),
        in_specs=[row(gw), _res_spec(dp1, tm), _res_spec(dp2, tm), row(PBLK),
                  pl.BlockSpec((P_WIDTH, D_MODEL), lambda i: (0, 0), pipeline_mode=pl.Buffered(1)),
                  row(D_MODEL), row(D_MODEL), row(D_MODEL), full(1, D_MODEL)] + _dep_spec(dep),
        out_specs=[row(D_MODEL), full(1, D_MODEL)],
        out_shape=[jax.ShapeDtypeStruct((t, D_MODEL), F32), jax.ShapeDtypeStruct((1, D_MODEL), F32)],
        scratch_shapes=[pltpu.VMEM((tm, P_WIDTH), BF16), _lane_scratch(tm, gw)],
        compiler_params=_params(("arbitrary",)),
    )(dp0, dp1, dp2, dpb, w_p_t, dh_part, dx1, x, g_mix, *_dep_arg(dep))


MESH = pl.DeviceIdType.MESH
HBM_SPEC = pl.BlockSpec(memory_space=pltpu.HBM)
VMEM_SPEC = pl.BlockSpec(memory_space=pltpu.VMEM)


def _all_gather(xp, act, g, tm):
    t = act.shape[0]
    d1, d2 = DIL_GROUPS[1][1], DIL_GROUPS[2][1]

    def body(x_ref, act_ref, g_ref, out_ref, h_ref, h1_ref, h2_ref, send_sems, recv_sems, local_sem, hf_ref):
        x, y, c = lax.axis_index("x"), lax.axis_index("y"), lax.axis_index("c")
        me, sibling = (x, y, c), (x, y, 1 - c)
        chips = [(1 - x, y), (x, 1 - y), (1 - x, 1 - y)]

        def rows(px, py, pc):
            return out_ref.at[4 * px + 2 * py + pc]

        def copy(k, block, to, src=None):
            return pltpu.make_async_remote_copy(
                src_ref=rows(*block) if src is None else src, dst_ref=rows(*block),
                send_sem=send_sems.at[k], recv_sem=recv_sems.at[k], device_id=to, device_id_type=MESH)

        mine = pltpu.make_async_copy(x_ref, rows(*me), local_sem)
        mine.start()
        first = [copy(0, me, sibling, src=x_ref)]
        first += [copy(1 + j, me, (*chip, c), src=x_ref) for j, chip in enumerate(chips)]
        for cp in first:
            cp.start()

        def norm(a_blk, h_blk, h1_blk, h2_blk):
            xt = a_blk[...]
            hf = xt * _rms(xt) * g_ref[...]
            h_blk[...] = hf.astype(BF16)
            _deinterleave(hf, hf_ref, h1_blk, BF16)
            _deinterleave(hf, hf_ref, h2_blk, BF16)

        res = lambda d: pl.BlockSpec((d, tm // d, D_MODEL), lambda i: (0, i, 0))
        row = pl.BlockSpec((tm, D_MODEL), lambda i: (i, 0))
        pltpu.emit_pipeline(norm, grid=(t // tm,), in_specs=[row], out_specs=[row, res(d1), res(d2)])(
            act_ref, h_ref, h1_ref, h2_ref)

        passed = [copy(4 + j, (*chip, c), sibling) for j, chip in enumerate(chips)]
        for j, chip in enumerate(chips):
            copy(1 + j, (*chip, c), me).wait_recv()
            passed[j].start()
        copy(0, sibling, me).wait_recv()
        for j, chip in enumerate(chips):
            copy(4 + j, (*chip, 1 - c), me).wait_recv()
        for cp in first + passed:
            cp.wait_send()
        mine.wait()

    sds = jax.ShapeDtypeStruct
    return pl.pallas_call(
        body, name="all_gather",
        out_shape=[sds((N_DEV,) + xp.shape, xp.dtype), sds((t, D_MODEL), BF16),
                   sds((d1, t // d1, D_MODEL), BF16), sds((d2, t // d2, D_MODEL), BF16)],
        in_specs=[HBM_SPEC, HBM_SPEC, VMEM_SPEC], out_specs=[HBM_SPEC] * 4,
        scratch_shapes=[pltpu.SemaphoreType.DMA((7,)), pltpu.SemaphoreType.DMA((7,)), pltpu.SemaphoreType.DMA,
                        _lane_scratch(tm, D_MODEL)],
        compiler_params=pltpu.CompilerParams(vmem_limit_bytes=VMEM_LIMIT),
    )(xp, act, g)


def _peers():
    x, y, c = lax.axis_index("x"), lax.axis_index("y"), lax.axis_index("c")
    out = []
    for k in range(1, N_DEV):
        px = 1 - x if k & 4 else x
        py = 1 - y if k & 2 else y
        pc = 1 - c if k & 1 else c
        out.append((k, (px, py, pc), 4 * px + 2 * py + pc))
    return out


def _my_index():
    return 4 * lax.axis_index("x") + 2 * lax.axis_index("y") + lax.axis_index("c")


SEM_SPEC = pl.BlockSpec(memory_space=pltpu.SEMAPHORE)
ANY_SPEC = pl.BlockSpec(memory_space=pl.ANY)
_SPLIT_PARAMS = pltpu.CompilerParams(has_side_effects=pltpu.SideEffectType.DATAFLOW_SIDE_EFFECTING)


def _split_copies(gather, src_refs, land_refs, send_sems, recv_sems):
    me_idx = _my_index()
    out = []
    for a, (src_ref, land_ref) in enumerate(zip(src_refs, land_refs)):
        for k, peer, peer_idx in _peers():
            if gather:
                src, dst = src_ref, land_ref.at[me_idx]
            else:
                src, dst = src_ref.at[peer_idx], land_ref.at[k - 1]
            out.append(pltpu.make_async_remote_copy(
                src_ref=src, dst_ref=dst, send_sem=send_sems.at[7 * a + k - 1], recv_sem=recv_sems.at[7 * a + k - 1],
                device_id=peer, device_id_type=MESH))
    return out


def _split_start(name, gather, srcs, after=None):
    n = len(srcs)
    extra = [] if after is None else [after]
    first_out = n + len(extra)

    def body(*refs):
        send_sems, recv_sems = refs[first_out], refs[first_out + 1]
        lands = refs[first_out + 2 + n:first_out + 2 + 2 * n]
        for cp in _split_copies(gather, refs[:n], lands, send_sems, recv_sems):
            cp.start()
        token = refs[-1]
        token[...] = jnp.zeros_like(token)

    lands = [pltpu.HBM((N_DEV,) + a.shape if gather else (N_DEV - 1,) + a.shape[1:], a.dtype) for a in srcs]
    return pl.pallas_call(
        body, name=name,
        out_shape=(pltpu.SemaphoreType.DMA((7 * n,)), pltpu.SemaphoreType.DMA((7 * n,)),
                   *[pltpu.HBM(a.shape, a.dtype) for a in srcs], *lands, jax.ShapeDtypeStruct((8, 128), F32)),
        in_specs=(HBM_SPEC,) * n + (ANY_SPEC,) * len(extra),
        out_specs=(SEM_SPEC, SEM_SPEC) + (HBM_SPEC,) * (2 * n) + (VMEM_SPEC,),
        input_output_aliases={i: 2 + i for i in range(n)}, compiler_params=_SPLIT_PARAMS,
    )(*[pltpu.with_memory_space_constraint(a, pltpu.HBM) for a in srcs], *extra)


def _split_wait(name, gather, started, after):
    send_sems, recv_sems, bufs = started[0], started[1], started[2:-1]
    n = len(bufs) // 2

    def body(*refs):
        for cp in _split_copies(gather, refs[:n], refs[n:2 * n], refs[2 * n], refs[2 * n + 1]):
            cp.wait_send()
            cp.wait_recv()

    out = pl.pallas_call(
        body, name=name, out_shape=tuple(pltpu.HBM(a.shape, a.dtype) for a in bufs),
        in_specs=(HBM_SPEC,) * (2 * n) + (SEM_SPEC, SEM_SPEC, ANY_SPEC), out_specs=(HBM_SPEC,) * (2 * n),
        input_output_aliases={i: i for i in range(2 * n)}, compiler_params=_SPLIT_PARAMS,
    )(*bufs, send_sems, recv_sems, after)
    return out[:n], out[n:]


def _adam_update(g, w, m, v):
    nm = ADAM_B1 * m + (1.0 - ADAM_B1) * g
    nv = ADAM_B2 * v + (1.0 - ADAM_B2) * (g * g)
    m_hat = nm / (1.0 - ADAM_B1 ** ADAM_STEP)
    v_hat = nv / (1.0 - ADAM_B2 ** ADAM_STEP)
    return -ADAM_LR * (m_hat / (jnp.sqrt(v_hat) + ADAM_EPS) + ADAM_WD * w), nm, nv


def _adamw(name, me, sent, got, w, m, v, tr, tc=None):
    r, c = w.shape
    tc = c if tc is None else tc

    def body(me_ref, own_ref, got_ref, w_ref, m_ref, v_ref, g_ref, d_ref, nm_ref, nv_ref):
        g = own_ref[...].astype(F32)
        for k in range(N_DEV - 1):
            g = g + got_ref[k].astype(F32)
        g_ref[...] = g
        d_ref[...], nm_ref[...], nv_ref[...] = _adam_update(g, w_ref[...], m_ref[...], v_ref[...])

    blk = pl.BlockSpec((tr, tc), lambda i, j, me_ref: (i, j))
    return pl.pallas_call(
        body, name=name,
        grid_spec=pltpu.PrefetchScalarGridSpec(
            num_scalar_prefetch=1, grid=(r // tr, c // tc),
            in_specs=[pl.BlockSpec((None, tr, tc), lambda i, j, me_ref: (me_ref[0], i, j)),
                      pl.BlockSpec((N_DEV - 1, tr, tc), lambda i, j, me_ref: (0, i, j)), blk, blk, blk],
            out_specs=[blk] * 4),
        out_shape=[jax.ShapeDtypeStruct((r, c), F32)] * 4,
        compiler_params=_params(("arbitrary", "arbitrary")),
    )(me, sent, got, w, m, v)


def _adamw_small(srecv, ws, ms, vs):
    nv_ = len(ws)

    def body(*refs):
        s_ref = refs[0]
        ins, outs = refs[1:1 + 3 * nv_], refs[1 + 3 * nv_:]
        g_all = s_ref[0]
        for k in range(1, N_DEV):
            g_all = g_all + s_ref[k]
        for i in range(nv_):
            n = ins[i].shape[1]
            g = g_all[i:i + 1, :n]
            d, nm, nv = _adam_update(g, ins[i][...], ins[nv_ + i][...], ins[2 * nv_ + i][...])
            outs[i][...], outs[nv_ + i][...], outs[2 * nv_ + i][...], outs[3 * nv_ + i][...] = g, d, nm, nv
        outs[-1][...] = g_all[nv_:nv_ + 1, :128]

    shapes = [jax.ShapeDtypeStruct(a.shape, F32) for a in ws]
    res = pl.pallas_call(body, name="adamw_small", out_shape=shapes * 4 + [jax.ShapeDtypeStruct((1, 128), F32)],
                         compiler_params=_params())(srecv, *ws, *ms, *vs)
    return [res[k * nv_:(k + 1) * nv_] for k in range(4)], res[-1]


def _cols_from_shards(a):
    return jnp.swapaxes(a, 0, 1).reshape(a.shape[1], a.shape[0] * a.shape[2])


def _shards_from_cols(a):
    return jnp.swapaxes(a.reshape(a.shape[0], N_DEV, a.shape[1] // N_DEV), 0, 1)


def _shards_from_rows(a):
    return a.reshape(N_DEV, a.shape[0] // N_DEV, a.shape[1])


def _pair_lanes(a):
    lead = a.shape[:-1]
    return a.reshape(lead + (2, 2, HEAD_DIM // 2)).swapaxes(-3, -2).reshape(lead + (128,))


def _split_w_in(w_in):
    rows = w_in.shape[0]
    dil = w_in[:, :3 * DIL_WIDTH].reshape(rows, 3, 3, 4, 128)
    dil = np.concatenate([_pair_lanes(dil[:, :2]), dil[:, 2:]], axis=1)
    dil = dil.transpose(0, 2, 3, 1, 4).reshape(rows, 3 * DIL_WIDTH)
    o = 3 * DIL_WIDTH
    qb = w_in[:, o:o + SWA_Q_WIDTH].reshape(rows, 2, 4, HEAD_DIM).transpose(0, 2, 1, 3).reshape(rows, 4, 128)
    qb = _pair_lanes(qb).reshape(rows, SWA_Q_WIDTH)
    kb = _pair_lanes(w_in[:, o + SWA_Q_WIDTH:o + SWA_Q_WIDTH + SWA_KV_WIDTH])
    vb = w_in[:, o + SWA_Q_WIDTH + SWA_KV_WIDTH:P_WIDTH]
    return np.concatenate([dil, qb, kb, vb], axis=1)


ROW_GRANULE = HEAD_DIM // 2


def _w_p_granules():
    order = _split_w_in(np.arange(IN_WIDTH)[None])[0]
    assert sorted(order.tolist()) == list(range(P_WIDTH))
    first = order[::ROW_GRANULE]
    assert (first % ROW_GRANULE == 0).all() and (order.reshape(-1, ROW_GRANULE) == first[:, None] + np.arange(ROW_GRANULE)).all()
    return first // ROW_GRANULE


def _gather_rows(name, src, granules, per_step):
    n, cols = len(granules), src.shape[1]
    assert n % per_step == 0

    def body(tab_ref, *refs):
        out_ref = refs[per_step]
        for j in range(per_step):
            out_ref[j * ROW_GRANULE:(j + 1) * ROW_GRANULE, :] = refs[j][...]

    def pick(j):
        return pl.BlockSpec((ROW_GRANULE, cols), lambda i, tab_ref: (tab_ref[per_step * i + j], 0))

    return pl.pallas_call(
        body, name=name,
        grid_spec=pltpu.PrefetchScalarGridSpec(
            num_scalar_prefetch=1, grid=(n // per_step,),
            in_specs=[pick(j) for j in range(per_step)],
            out_specs=pl.BlockSpec((per_step * ROW_GRANULE, cols), lambda i, tab_ref: (i, 0))),
        out_shape=jax.ShapeDtypeStruct((n * ROW_GRANULE, cols), src.dtype),
        compiler_params=_params(("arbitrary",)),
    )(jnp.asarray(granules, jnp.int32), *([src] * per_step))


def _swa_rows(w_b):
    return w_b.reshape(2, 4, HEAD_DIM, -1).transpose(1, 0, 2, 3).reshape(SWA_Q_WIDTH, -1)


def _swa_rows_inv(dw_b):
    return dw_b.reshape(4, 2, HEAD_DIM, -1).transpose(1, 0, 2, 3).reshape(SWA_Q_WIDTH, -1)


def _rope_tables(pos):
    half = HEAD_DIM // 2
    inv = ROPE_THETA ** (-jnp.arange(half, dtype=F32) / half)
    ang = pos.astype(F32)[:, None] * jnp.tile(inv, 4)
    sign = jnp.repeat(jnp.array([-1.0, 1.0], F32), 2 * half)
    return jnp.cos(ang), jnp.sin(ang) * sign


def _local_step(x, hs, mem, pos, target, w_in_t, dep, rest_weights, on_grads, g_mix, g_cross, g_mem, g_mlp, g_final, sink):
    t = x.shape[0]
    tm = min(512, t)
    tq = 1024
    tw = min(2048, t)
    w_p_t = _gather_rows("w_p_rows", w_in_t, _w_p_granules(), 24)
    w_p = jnp.swapaxes(w_p_t, 0, 1)
    cos, sin = lax.optimization_barrier(_rope_tables(pos))
    sink_row = jnp.pad(sink.reshape(2, 4).T.reshape(1, 8), ((0, 0), (0, 120)))
    tabs = [(cos[None], sin[None])]
    for _, d in DIL_GROUPS[1:]:
        tabs.append(tuple(a.reshape(t // d, d, 128).swapaxes(0, 1) for a in (cos, sin)))
    tabs.append(tabs[0])

    h, h1, h2 = hs
    p0, p1, p2, pb = _inproj(h, h1, h2, w_p, [(cos, sin), tabs[1], tabs[2]], tm, dep)
    ps = [p0[None], p1, p2, pb[None]]
    outs, lses = [], []
    for gi, pv in enumerate(ps):
        res = _attn_fwd(f"attn_fwd{gi}", pv, gi == 3, sink_row[0, :8], min(tq, pv.shape[1]))
        outs.append(res[0])
        lses.append(res[1])
    o0, l0, ob, lb, ob32 = outs[0][0], lses[0][0], outs[3][0], lses[3][0], res[2][0]
    wts = rest_weights(GROUP_B, lb)
    w_b = _swa_rows(wts["w_branch_b"])
    tf = 2048
    w_g = wts["w_g"]
    gts = _gates(h, w_g, wts["b_gate"].reshape(1, GATE_WIDTH), min(1024, t), 1024)
    oa, ya, yb, merged, x1, hc = _mix(o0, outs[1], outs[2], l0, lses[1], lses[2], ob, gts, x,
                                      wts["w_branch_a"], w_b, wts["w_out"], g_cross, tm)
    mn, kv = _memkv(mem, g_mem, wts["w_ckv"])
    q, o, x2, hm = _cross(hc, x1, kv, wts["w_cq"], wts["w_co"], g_mlp, tm)
    wts.update(rest_weights(GROUP_A, hm))
    a, dx3, loss, dg_final = _mlp(hm, x2, wts["w_1"], wts["w_2"], g_final.reshape(1, D_MODEL), target, tm, tf)

    grads = {}
    dz, dx2, dg_mlp = _mlp_bwd(dx3, a, wts["w_1"], wts["w_2"], x2, g_mlp, tm, tf)
    grads["w_2"] = _shards_from_rows(_wgrad("dw_2", a, dx3, 1024, 1024, tw, square=True))
    grads["w_1"] = _wgrad("dw_1", hm, dz, 1024, 1024, tw, col_shards=True)
    dep = on_grads(GROUP_A, grads)
    dq, dx1, dkv, dg_cross = _cross_bwd(dx2, x1, q, kv, wts["w_cq"], wts["w_co"], g_cross, tm, dep)
    grads["w_co"] = _shards_from_rows(_wgrad("dw_co", o, dx2, 1024, 1024, tw))
    grads["w_cq"] = _shards_from_rows(_wgrad("dw_cq", hc, dq, 1024, 1024, tw))
    grads["w_ckv"], dg_mem = _memkv_bwd(dkv, mn, mem, wts["w_ckv"], g_mem)
    dgt, dh_part, dya, dyb, db_gate = _merge_bwd(dx1, ya, yb, gts, wts["w_out"], w_g, tm)
    do0, do1, do2, c0, c1, c2, dob, cb, dsink = _combine_bwd(
        dya, dyb, oa, ob32, l0, lses[1], lses[2], lb, sink_row, wts["w_branch_a"], w_b, tm)
    grads["w_out"] = _shards_from_rows(_wgrad("dw_out", merged, dx1, 1024, 1024, tw))
    grads["w_branch_a"] = _shards_from_cols(_wgrad("dw_a", oa, dya, 512, 1024, tw))
    grads["w_branch_b"] = _shards_from_cols(_swa_rows_inv(_wgrad("dw_b", ob, dyb, 512, 1024, tw)))
    grads["b_gate"] = _shards_from_cols(db_gate.reshape(2, D_MODEL)).astype(BF16)
    dep = on_grads(GROUP_B, grads)
    dw_g_t = _wgrad("dw_g", dgt, h, 1024, 1024, tw)
    dps = []
    for gi, (pv, do_g, c_g) in enumerate(zip(ps, (do0[None], do1, do2, dob[None]), (c0[None], c1, c2, cb[None]))):
        dps.append(_attn_bwd(f"attn_bwd{gi}", pv, do_g, lses[gi], c_g, tabs[gi][0], tabs[gi][1], gi == 3,
                             min(tq, pv.shape[1]),
                             dep if gi == 0 else None))
    dw_p_t = [_wgrad(f"dw_p{gi}", dpg.reshape(t, -1), hh.reshape(t, D_MODEL), PBLK, 1024, tw)
              for gi, (hh, dpg) in enumerate(zip((h, h1, h2, h), dps))]
    back = np.concatenate([np.argsort(_w_p_granules()),
                           np.arange(P_WIDTH // ROW_GRANULE, IN_WIDTH // ROW_GRANULE)])
    dw_in_t = _gather_rows("dw_in_rows", jnp.concatenate(dw_p_t + [dw_g_t], axis=0), back, 29)
    grads["w_in"] = dw_in_t.reshape(N_DEV, IN_WIDTH // N_DEV, D_MODEL)
    dep = on_grads(GROUP_C, grads)
    grad_x, dg_mix = _dx(dps[0][0], dps[1], dps[2], dps[3][0], w_p_t, dh_part, dx1, x, g_mix, tm, dep)
    dsink_heads = dsink[0, :8].reshape(4, 2).T.reshape(8)
    small = {"g_mix": dg_mix[0], "g_cross": dg_cross[0], "g_mem": dg_mem[0], "g_mlp": dg_mlp[0],
             "g_final": dg_final[0], "sink": dsink_heads}
    return loss[0, 0], grad_x, small


def kernel(x, mem, positions, g_mix, w_in, b_gate, sink, w_branch_a, w_branch_b, w_out, g_cross, g_mem, w_cq, w_ckv, w_co, g_mlp, w_1, w_2, g_final, loss_target, m_g_mix, m_w_in, m_b_gate, m_sink, m_w_branch_a, m_w_branch_b, m_w_out, m_g_cross, m_g_mem, m_w_cq, m_w_ckv, m_w_co, m_g_mlp, m_w_1, m_w_2, m_g_final, v_g_mix, v_w_in, v_b_gate, v_sink, v_w_branch_a, v_w_branch_b, v_w_out, v_g_cross, v_g_mem, v_w_cq, v_w_ckv, v_w_co, v_g_mlp, v_w_1, v_w_2, v_g_final):
    local = dict(locals())
    shard = {n: local[n][0] for n in GROUP_A + GROUP_B + GROUP_C}
    me = _my_index()
    me_arr = me.reshape(1).astype(jnp.int32)
    tags = {GROUP_A: "a", GROUP_B: "b", GROUP_C: "c"}

    transposed = lambda a: jnp.swapaxes(a, 0, 1)
    gathered_w_in, *hs = _all_gather(transposed(shard["w_in"]).astype(BF16), x[0], g_mix, min(512, x.shape[1]))
    w_in_t = gathered_w_in.reshape(-1, gathered_w_in.shape[2])

    def gathered(name, started, after):
        srcs, lands = _split_wait(name, True, started, after)
        return [lax.dynamic_update_slice(land, src[None], (me,) + (0,) * src.ndim) for src, land in zip(srcs, lands)]

    def start_gather(names, after=None):
        return _split_start("gather_start_" + tags[names], True,
                            [shard[n] if n == "b_gate" else shard[n].astype(BF16) for n in names], after)

    gathers = {GROUP_B: start_gather(GROUP_B)}
    gathers[GROUP_A] = start_gather(GROUP_A, gathers[GROUP_B][-1])

    def rest_weights(names, after):
        full = {"w_g": transposed(w_in_t[P_WIDTH:])} if names == GROUP_B else {}
        for name, a in zip(names, gathered("gather_wait_" + tags[names], gathers[names], after)):
            if name in ("w_1", "w_ckv"):
                full[name] = a
            elif name in _COL_SHARDED:
                full[name] = _cols_from_shards(a)
            else:
                full[name] = a.reshape(N_DEV * a.shape[1], a.shape[2])
        return full

    scatters = {}

    def on_grads(names, grads):
        scatters[names] = _split_start("scatter_start_" + tags[names], False, [grads[n] for n in names])
        return scatters[names][-1]

    loss, grad_x, small = _local_step(
        x[0], hs, mem[0], positions[0], loss_target[0], w_in_t, gathers[GROUP_A][-1], rest_weights, on_grads,
        g_mix, g_cross, g_mem, g_mlp, g_final, sink[0])

    sp = jnp.stack([small[n] if n != "sink" else jnp.pad(small[n], (0, LANES - 8)) for n in SMALL]
                   + [jnp.pad(loss.reshape(1), (0, LANES - 1)), jnp.zeros((LANES,), F32)])
    small_gather = _split_start("small_start", True, [sp])

    after, updated = small_gather[-1], {}
    for names in (GROUP_A, GROUP_B, GROUP_C):
        sent, got = _split_wait("scatter_wait_" + tags[names], False, scatters[names], after)
        for i, name in enumerate(names):
            view = transposed if name == "w_in" else (lambda a: a)
            outs = _adamw("adamw_" + name, me_arr, sent[i], got[i], view(shard[name]),
                          view(local["m_" + name][0]), view(local["v_" + name][0]), ADAM_ROWS[name], ADAM_COLS.get(name))
            updated[name] = [view(a)[None] for a in outs]
            after = outs[3]

    flat = lambda prefix: [local[prefix + n].reshape(1, -1) for n in SMALL]
    outs, loss_row = _adamw_small(gathered("small_wait", small_gather, after)[0], flat(""), flat("m_"), flat("v_"))
    for i, name in enumerate(SMALL):
        updated[name] = [outs[which][i].reshape(local[name].shape) for which in range(4)]

    order = ["g_mix", "w_in", "b_gate", "sink", "w_branch_a", "w_branch_b", "w_out", "g_cross", "g_mem", "w_cq",
             "w_ckv", "w_co", "g_mlp", "w_1", "w_2", "g_final"]
    res = [loss_row[0, 0], grad_x[None]]
    for which in range(4):
        res += [updated[n][which] for n in order]
    return tuple(res)
```

```python
import functools
import math

import jax
import jax.numpy as jnp
import numpy as np
from jax import lax
from jax.experimental import pallas as pl
from jax.experimental.pallas import tpu as pltpu

F32 = jnp.float32
BF16 = jnp.bfloat16

D_MODEL = 1024
HEAD_DIM = 64
DIL_GROUPS = ((128, 1), (512, 4), (2048, 16))
ROPE_THETA = 10000.0
X_HEADS = 4
X_HEAD_DIM = D_MODEL // X_HEADS
D_FF = 4 * D_MODEL
EPS = 1e-6
DIL_WIDTH = 1536
SWA_Q_WIDTH = 512
SWA_KV_WIDTH = 128
P_WIDTH = 3 * DIL_WIDTH + SWA_Q_WIDTH + 2 * SWA_KV_WIDTH
GATE_WIDTH = 2 * D_MODEL
IN_WIDTH = P_WIDTH + GATE_WIDTH
BAND = 128
PBLK = 768
Q_SCALE = HEAD_DIM ** -0.5
X_SCALE = X_HEAD_DIM ** -0.5

ADAM_LR = 0.001
ADAM_B1 = 0.9
ADAM_B2 = 0.999
ADAM_EPS = 1e-08
ADAM_WD = 0.01
ADAM_STEP = 10

N_DEV = 8
LANES = 1024
VMEM_LIMIT = 52 * 1024 * 1024

NT = (((1,), (1,)), ((), ()))
TN = (((0,), (0,)), ((), ()))

GROUP_A = ("w_1", "w_2")
GROUP_B = ("w_branch_a", "w_branch_b", "w_out", "w_cq", "w_ckv", "w_co", "b_gate")
GROUP_C = ("w_in",)
_COL_SHARDED = ("w_in", "w_branch_a", "w_branch_b", "w_ckv", "w_1", "b_gate")
ADAM_ROWS = {"w_in": 464, "w_branch_a": 512, "w_branch_b": 512, "w_out": 128, "w_cq": 128, "w_ckv": 512,
             "w_co": 128, "w_1": 256, "w_2": 256, "b_gate": 2}
ADAM_COLS = {"w_in": 256}
SMALL = ("g_mix", "g_cross", "g_mem", "g_mlp", "g_final", "sink")


def _params(sem=None):
    return pltpu.CompilerParams(dimension_semantics=sem, vmem_limit_bytes=VMEM_LIMIT)


def _dot(a, b):
    return jnp.dot(a, b, preferred_element_type=F32)


def _dot_nt(a, b):
    return lax.dot_general(a, b, NT, preferred_element_type=F32)


def _dot_tn(a, b):
    return lax.dot_general(a, b, TN, preferred_element_type=F32)


def _rms(xt):
    return lax.rsqrt(jnp.mean(xt * xt, axis=-1, keepdims=True) + EPS)


def _rms_bwd(dh, xt, r, g):
    xn = xt * r
    dxn = dh * g
    dx = r * (dxn - xn * jnp.mean(dxn * xn, axis=-1, keepdims=True))
    return dx, jnp.sum(dh * xn, axis=0, keepdims=True)


def _rope(x, c, s, swa, sign):
    kinds = "qqqqkv" if swa else "qkvqkv"
    cq, sq = c * Q_SCALE, s * (sign * Q_SCALE)
    sk = s * sign if sign != 1 else s
    out = []
    for ci, kind in enumerate(kinds):
        xc = x[:, ci * 128:(ci + 1) * 128]
        if kind == "v":
            out.append(xc)
        elif kind == "q":
            out.append(xc * cq + pltpu.roll(xc, 64, 1) * sq)
        else:
            out.append(xc * c + pltpu.roll(xc, 64, 1) * sk)
    return jnp.concatenate(out, axis=1)


def _lane_scratch(rows, w):
    return pltpu.VMEM((w // 128, rows, 128), F32)


def _deinterleave(val, scr_ref, dst_ref, dtype):
    d, n = dst_ref.shape[0], dst_ref.shape[1]
    nc = val.shape[1] // 128
    for c in range(nc):
        scr_ref[c] = val[:, c * 128:(c + 1) * 128]
    for r in range(d):
        rows = [scr_ref.at[c][pl.ds(r, n, stride=d), :] for c in range(nc)]
        dst_ref[r] = jnp.concatenate(rows, axis=1).astype(dtype)


def _res_spec(a, tm):
    d, w = a.shape[0], a.shape[2]
    return pl.BlockSpec((d, tm // d, w), lambda i: (0, i, 0))


def _interleave(src_ref, scr_ref):
    d, n = src_ref.shape[0], src_ref.shape[1]
    nc = src_ref.shape[2] // 128
    for r in range(d):
        v = src_ref[r].astype(F32)
        for c in range(nc):
            scr_ref.at[c][pl.ds(r, n, stride=d), :] = v[:, c * 128:(c + 1) * 128]
    return jnp.concatenate([scr_ref[c] for c in range(nc)], axis=1)


def _with_dep(body, n_in, dep):
    if dep is None:
        return body
    return lambda *refs: body(*refs[:n_in], *refs[n_in + 1:])


def _dep_spec(dep):
    return [] if dep is None else [pl.BlockSpec(memory_space=pl.ANY)]


def _dep_arg(dep):
    return [] if dep is None else [dep]


def _inproj(h, h1, h2, w_p_t, tabs, tm, dep=None):
    t = h.shape[0]
    gw = 2 * PBLK
    (cos, sin), (cos1, sin1), (cos2, sin2) = tabs[0], tabs[1], tabs[2]

    def body(h_ref, h1_ref, h2_ref, w_ref, c_ref, s_ref, c1_ref, s1_ref, c2_ref, s2_ref,
             p0_ref, p1_ref, p2_ref, pb_ref):
        rows = lambda ref: ref[...].reshape(tm, ref.shape[-1])
        groups = ((h_ref, c_ref, s_ref, p0_ref), (h1_ref, c1_ref, s1_ref, p1_ref), (h2_ref, c2_ref, s2_ref, p2_ref))
        for gi, (lhs_ref, cc_ref, ss_ref, out_ref) in enumerate(groups):
            lhs, cc, ss = rows(lhs_ref), rows(cc_ref), rows(ss_ref)
            for half in range(2):
                col = gi * gw + half * PBLK
                val = _rope(_dot_nt(lhs, w_ref[col:col + PBLK, :]), cc, ss, False, 1).astype(BF16)
                if out_ref.ndim == 3:
                    out_ref[:, :, half * PBLK:(half + 1) * PBLK] = val.reshape(out_ref.shape[:2] + (PBLK,))
                else:
                    out_ref[:, half * PBLK:(half + 1) * PBLK] = val
        pb_ref[...] = _rope(_dot_nt(h_ref[...], w_ref[3 * gw:, :]), c_ref[...], s_ref[...], True, 1).astype(BF16)

    d1, d2 = DIL_GROUPS[1][1], DIL_GROUPS[2][1]
    row = lambda w: pl.BlockSpec((tm, w), lambda i: (i, 0))
    res = lambda d, w: pl.BlockSpec((d, tm // d, w), lambda i: (0, i, 0))
    sds = jax.ShapeDtypeStruct
    return pl.pallas_call(
        _with_dep(body, 10, dep), name="inproj", grid=(t // tm,),
        in_specs=[row(D_MODEL), res(d1, D_MODEL), res(d2, D_MODEL),
                  pl.BlockSpec((P_WIDTH, D_MODEL), lambda i: (0, 0), pipeline_mode=pl.Buffered(1)),
                  row(128), row(128), res(d1, 128), res(d1, 128), res(d2, 128), res(d2, 128)] + _dep_spec(dep),
        out_specs=[row(gw), res(d1, gw), res(d2, gw), row(PBLK)],
        out_shape=[sds((t, gw), BF16), sds((d1, t // d1, gw), BF16), sds((d2, t // d2, gw), BF16),
                   sds((t, PBLK), BF16)],
        compiler_params=_params(("arbitrary",)),
    )(h, h1, h2, w_p_t, cos, sin, cos1, sin1, cos2, sin2, *_dep_arg(dep))


def _gates(h, w_g, b, tm, tn):
    t = h.shape[0]

    def body(h_ref, w_ref, b_ref, o_ref):
        z = _dot(h_ref[...], w_ref[...]) + b_ref[...]
        o_ref[...] = (0.5 * jnp.tanh(0.5 * z) + 0.5).astype(BF16)

    return pl.pallas_call(
        body, name="gates", grid=(t // tm, GATE_WIDTH // tn),
        in_specs=[pl.BlockSpec((tm, D_MODEL), lambda i, j: (i, 0)),
                  pl.BlockSpec((D_MODEL, tn), lambda i, j: (0, j)),
                  pl.BlockSpec((1, tn), lambda i, j: (0, j))],
        out_specs=pl.BlockSpec((tm, tn), lambda i, j: (i, j)),
        out_shape=jax.ShapeDtypeStruct((t, GATE_WIDTH), BF16),
        compiler_params=_params(("arbitrary", "arbitrary")),
    )(h, w_g, b)


def _band_mask(i, s):
    row = lax.broadcasted_iota(jnp.int32, (BAND, 2 * BAND), 0)
    col = lax.broadcasted_iota(jnp.int32, (BAND, 2 * BAND), 1)
    band = (col >= row) & (col <= row + BAND)
    if s == 0:
        band = band & ((col >= BAND) | (i > 0))
    return band


def _head_a_masks(rows):
    lane = lax.broadcasted_iota(jnp.int32, (rows, 128), 1)
    return (lane % HEAD_DIM) < HEAD_DIM // 2, lane < HEAD_DIM


def _stack_heads(x, head_a):
    zero = jnp.zeros_like(x)
    return jnp.concatenate([jnp.where(head_a, x, zero), jnp.where(head_a, zero, x)], axis=0)


def _stack_heads_t(xt, head_a_t):
    zero = jnp.zeros_like(xt)
    return jnp.concatenate([jnp.where(head_a_t, xt, zero), jnp.where(head_a_t, zero, xt)], axis=1)


def _kv_rows(cur_ref, tail_ref, s, off):
    if s == 0:
        return jnp.concatenate([tail_ref[:, off:off + 128], cur_ref[0:BAND, off:off + 128]], axis=0)
    return cur_ref[(s - 1) * BAND:(s + 1) * BAND, off:off + 128]


def _attn_layout(swa):
    if swa:
        return [(128 * j, 512, 640) for j in range(4)]
    return [(0, 128, 256), (384, 512, 640)]


def _attn_fwd(name, pv, swa, sinks, tq):
    d, ls = pv.shape[0], pv.shape[1]
    n, nsb = ls // tq, tq // BAND
    pairs = _attn_layout(swa)
    ncol = 1 if swa else 2
    ow = 128 * len(pairs)

    def body(cur_ref, tail_ref, *rest):
        sink_ref, o_ref, lse_ref, o32_ref = rest if swa else (None,) + rest + (None,)
        i = pl.program_id(2)
        lane = lax.broadcasted_iota(jnp.int32, (BAND, 128), 1)
        qk_a, v_a = _head_a_masks(BAND)
        first = lax.broadcasted_iota(jnp.int32, (2 * BAND, 1), 0) < BAND
        for s in range(nsb):
            mask = _band_mask(i, s)
            mask2 = jnp.concatenate([mask, mask], axis=0)
            rows = slice(s * BAND, (s + 1) * BAND)
            lse_tile = jnp.zeros((BAND, 128), F32)
            for j, (qo, ko, vo) in enumerate(pairs):
                q = cur_ref[rows, qo:qo + 128]
                kk = _kv_rows(cur_ref, tail_ref, s, ko)
                vv = _kv_rows(cur_ref, tail_ref, s, vo)
                sc = _dot_nt(_stack_heads(q, qk_a), kk)
                sc = jnp.where(mask2, sc, -jnp.inf)
                m = jnp.max(sc, axis=-1, keepdims=True)
                if swa:
                    sk = jnp.where(first, sink_ref[2 * j], sink_ref[2 * j + 1])
                    m = jnp.maximum(m, sk)
                p = jnp.exp(sc - m)
                den = jnp.sum(p, axis=-1, keepdims=True)
                if swa:
                    den = den + jnp.exp(sk - m)
                lse = m + jnp.log(den)
                lse_tile = jnp.where(lane == 2 * j, lse[:BAND], jnp.where(lane == 2 * j + 1, lse[BAND:], lse_tile))
                o2 = _dot(p.astype(BF16), vv) * (1.0 / den)
                o = jnp.where(v_a, o2[:BAND], o2[BAND:])
                o_ref[rows, j * 128:(j + 1) * 128] = o.astype(BF16)
                if swa:
                    o32_ref[rows, j * 128:(j + 1) * 128] = o
            lse_ref[rows, :] = lse_tile

    in_specs = [pl.BlockSpec((None, tq, PBLK), lambda r, cb, i: (r, i, cb)),
                pl.BlockSpec((None, BAND, PBLK), lambda r, cb, i: (r, jnp.maximum(i * nsb - 1, 0), cb))]
    args = [pv, pv]
    out_specs = [pl.BlockSpec((None, tq, ow), lambda r, cb, i: (r, i, cb)),
                 pl.BlockSpec((None, tq, 128), lambda r, cb, i: (r, i, cb))]
    out_shape = [jax.ShapeDtypeStruct((d, ls, 512), BF16), jax.ShapeDtypeStruct((d, ls, 128 * ncol), F32)]
    if swa:
        in_specs.append(pl.BlockSpec(memory_space=pltpu.SMEM))
        args.append(sinks)
        out_specs.append(out_specs[0])
        out_shape.append(jax.ShapeDtypeStruct((d, ls, 512), F32))
    return pl.pallas_call(
        body, name=name, grid=(d, ncol, n),
        in_specs=in_specs, out_specs=out_specs, out_shape=out_shape,
        compiler_params=_params(("arbitrary", "arbitrary", "arbitrary")),
    )(*args)


def _lse_lane(head):
    return (head // 4) * 128 + head % 4


def _dil_head_spread():
    lane = lax.broadcasted_iota(jnp.int32, (256, 512), 0)
    head = lax.broadcasted_iota(jnp.int32, (256, 512), 1) // HEAD_DIM
    return (lane == _lse_lane(head)).astype(BF16)


def _head_scale(x, tile, spread):
    return x * _dot(tile.astype(BF16), spread)


def _head_gather(width, dil):
    head = lax.broadcasted_iota(jnp.int32, (8 * HEAD_DIM, width), 0) // HEAD_DIM
    lane = lax.broadcasted_iota(jnp.int32, (8 * HEAD_DIM, width), 1)
    return (lane == (_lse_lane(head) if dil else head)).astype(BF16)


def _head_sums(x, gather):
    hi = x.astype(BF16)
    lo = (x - hi.astype(F32)).astype(BF16)
    return _dot(hi, gather) + _dot(lo, gather)


def _alphas(l0, l1, l2):
    m = jnp.maximum(jnp.maximum(l0, l1), l2)
    e0, e1, e2 = jnp.exp(l0 - m), jnp.exp(l1 - m), jnp.exp(l2 - m)
    den = e0 + e1 + e2
    return e0 / den, e1 / den, e2 / den


def _mix(o0, o1, o2, l0, l1, l2, ob, gts, x, w_a, w_b, w_out, g_cross, tm):
    t = x.shape[0]

    def body(o0_ref, o1_ref, o2_ref, l0_ref, l1_ref, l2_ref, ob_ref, g_ref, x_ref, wa_ref, wb_ref, wo_ref,
             gc_ref, oa_ref, ya_ref, yb_ref, mg_ref, x1_ref, hc_ref, so_ref, sl_ref):
        a0, a1, a2 = _alphas(l0_ref[...], _interleave(l1_ref, sl_ref), _interleave(l2_ref, sl_ref))
        spread = _dil_head_spread()
        oa = (_head_scale(o0_ref[...].astype(F32), a0, spread)
              + _head_scale(_interleave(o1_ref, so_ref), a1, spread)
              + _head_scale(_interleave(o2_ref, so_ref), a2, spread))
        oab = oa.astype(BF16)
        oa_ref[...] = oab
        ya = _dot(oab, wa_ref[...])
        yb = _dot(ob_ref[...], wb_ref[...])
        ya_ref[...] = ya.astype(BF16)
        yb_ref[...] = yb.astype(BF16)
        merged = (g_ref[:, :D_MODEL].astype(F32) * ya + g_ref[:, D_MODEL:].astype(F32) * yb).astype(BF16)
        mg_ref[...] = merged
        x1 = x_ref[...] + _dot(merged, wo_ref[...])
        x1_ref[...] = x1
        hc_ref[...] = (x1 * _rms(x1) * gc_ref[...]).astype(BF16)

    row = lambda w: pl.BlockSpec((tm, w), lambda i: (i, 0))
    full = lambda a, b: pl.BlockSpec((a, b), lambda i: (0, 0))
    return pl.pallas_call(
        body, name="mix", grid=(t // tm,),
        in_specs=[row(512), _res_spec(o1, tm), _res_spec(o2, tm), row(256), _res_spec(l1, tm), _res_spec(l2, tm),
                  row(512), row(GATE_WIDTH),
                  row(D_MODEL), full(512, D_MODEL), full(512, D_MODEL), full(D_MODEL, D_MODEL), full(1, D_MODEL)],
        out_specs=[row(512), row(D_MODEL), row(D_MODEL), row(D_MODEL), row(D_MODEL), row(D_MODEL)],
        out_shape=[jax.ShapeDtypeStruct((t, 512), BF16), jax.ShapeDtypeStruct((t, D_MODEL), BF16),
                   jax.ShapeDtypeStruct((t, D_MODEL), BF16), jax.ShapeDtypeStruct((t, D_MODEL), BF16),
                   jax.ShapeDtypeStruct((t, D_MODEL), F32), jax.ShapeDtypeStruct((t, D_MODEL), BF16)],
        scratch_shapes=[_lane_scratch(tm, 512), _lane_scratch(tm, 256)],
        compiler_params=_params(("arbitrary",)),
    )(o0, o1, o2, l0, l1, l2, ob, gts, x, w_a, w_b, w_out, g_cross)


def _memkv(mem, g_mem, w_ckv):
    m = mem.shape[0]
    ws = w_ckv.shape[2]

    def body(mem_ref, g_ref, w_ref, mn_ref, kv_ref):
        xt = mem_ref[...]
        mn = (xt * _rms(xt) * g_ref[...]).astype(BF16)
        mn_ref[...] = mn
        for j in range(N_DEV):
            kv_ref[:, j * ws:(j + 1) * ws] = _dot(mn, w_ref[j]).astype(BF16)

    return pl.pallas_call(
        body, name="memkv",
        out_shape=[jax.ShapeDtypeStruct((m, D_MODEL), BF16), jax.ShapeDtypeStruct((m, 2 * D_MODEL), BF16)],
        compiler_params=_params(),
    )(mem, g_mem, w_ckv)


def _cross_probs(q, kv_ref, h):
    k = kv_ref[:, h * X_HEAD_DIM:(h + 1) * X_HEAD_DIM]
    sc = _dot_nt(q[:, h * X_HEAD_DIM:(h + 1) * X_HEAD_DIM], k)
    m = jnp.max(sc, axis=-1, keepdims=True)
    p = jnp.exp(sc - m)
    return p / jnp.sum(p, axis=-1, keepdims=True)


def _cross(hc, x1, kv, w_cq, w_co, g_mlp, tm):
    t = x1.shape[0]
    m = kv.shape[0]

    def body(hc_ref, x1_ref, kv_ref, wq_ref, wo_ref, g_ref, q_ref, o_ref, x2_ref, hm_ref):
        q = (_dot(hc_ref[...], wq_ref[...]) * X_SCALE).astype(BF16)
        q_ref[...] = q
        outs = []
        for h in range(X_HEADS):
            p = _cross_probs(q, kv_ref, h)
            v = kv_ref[:, D_MODEL + h * X_HEAD_DIM:D_MODEL + (h + 1) * X_HEAD_DIM]
            outs.append(_dot(p.astype(BF16), v))
        o = jnp.concatenate(outs, axis=1).astype(BF16)
        o_ref[...] = o
        x2 = x1_ref[...] + _dot(o, wo_ref[...])
        x2_ref[...] = x2
        hm_ref[...] = (x2 * _rms(x2) * g_ref[...]).astype(BF16)

    row = lambda w: pl.BlockSpec((tm, w), lambda i: (i, 0))
    full = lambda a, b: pl.BlockSpec((a, b), lambda i: (0, 0))
    return pl.pallas_call(
        body, name="cross", grid=(t // tm,),
        in_specs=[row(D_MODEL), row(D_MODEL), full(m, 2 * D_MODEL), full(D_MODEL, D_MODEL),
                  full(D_MODEL, D_MODEL), full(1, D_MODEL)],
        out_specs=[row(D_MODEL)] * 4,
        out_shape=[jax.ShapeDtypeStruct((t, D_MODEL), BF16), jax.ShapeDtypeStruct((t, D_MODEL), BF16),
                   jax.ShapeDtypeStruct((t, D_MODEL), F32), jax.ShapeDtypeStruct((t, D_MODEL), BF16)],
        compiler_params=_params(("arbitrary",)),
    )(hc, x1, kv, w_cq, w_co, g_mlp)


def _mlp(hm, x2, w_1, w_2, g_final, target, tm, tf):
    t = x2.shape[0]
    nf = D_FF // tf

    def body(hm_ref, x2_ref, w1_ref, w2_ref, g_ref, tg_ref, a_ref, dx3_ref, loss_ref, dg_ref, acc_ref):
        i, f = pl.program_id(0), pl.program_id(1)
        hm_t = hm_ref[...]
        sw = w1_ref.shape[2]
        part = None
        for s in range(w1_ref.shape[0]):
            a = jnp.maximum(_dot(hm_t, w1_ref[s]), 0.0).astype(BF16)
            a_ref[:, s * sw:(s + 1) * sw] = a
            p_s = _dot(a * a, w2_ref[s * sw:(s + 1) * sw, :])
            part = p_s if part is None else part + p_s

        @pl.when(f == 0)
        def _():
            acc_ref[...] = part

        @pl.when(f > 0)
        def _():
            acc_ref[...] += part

        @pl.when((i == 0) & (f == 0))
        def _():
            loss_ref[...] = jnp.zeros_like(loss_ref)
            dg_ref[...] = jnp.zeros_like(dg_ref)

        @pl.when(f == nf - 1)
        def _():
            x3 = x2_ref[...] + acc_ref[...]
            r = _rms(x3)
            g = g_ref[...]
            diff = x3 * r * g - tg_ref[...]
            loss_ref[...] += 0.5 * jnp.sum(jnp.mean(diff * diff, axis=-1, keepdims=True))
            dx3, dg = _rms_bwd(diff / D_MODEL, x3, r, g)
            dx3_ref[...] = dx3
            dg_ref[...] += dg

    return pl.pallas_call(
        body, name="mlp", grid=(t // tm, nf),
        in_specs=[pl.BlockSpec((tm, D_MODEL), lambda i, f: (i, 0)),
                  pl.BlockSpec((tm, D_MODEL), lambda i, f: (i, 0)),
                  pl.BlockSpec((tf // w_1.shape[2], D_MODEL, w_1.shape[2]), lambda i, f: (f, 0, 0)),
                  pl.BlockSpec((tf, D_MODEL), lambda i, f: (f, 0)),
                  pl.BlockSpec((1, D_MODEL), lambda i, f: (0, 0)),
                  pl.BlockSpec((tm, D_MODEL), lambda i, f: (i, 0))],
        out_specs=[pl.BlockSpec((tm, tf), lambda i, f: (i, f)),
                   pl.BlockSpec((tm, D_MODEL), lambda i, f: (i, 0)),
                   pl.BlockSpec((1, 128), lambda i, f: (0, 0)),
                   pl.BlockSpec((1, D_MODEL), lambda i, f: (0, 0))],
        out_shape=[jax.ShapeDtypeStruct((t, D_FF), BF16), jax.ShapeDtypeStruct((t, D_MODEL), F32),
                   jax.ShapeDtypeStruct((1, 128), F32), jax.ShapeDtypeStruct((1, D_MODEL), F32)],
        scratch_shapes=[pltpu.VMEM((tm, D_MODEL), F32)],
        compiler_params=_params(("arbitrary", "arbitrary")),
    )(hm, x2, w_1, w_2, g_final, target)


def _mlp_bwd(dx3, a, w_1, w_2, x2, g_mlp, tm, tf):
    t = x2.shape[0]
    nf = D_FF // tf

    def body(dx3_ref, a_ref, w1_ref, w2_ref, x2_ref, g_ref, dz_ref, dx2_ref, dg_ref, acc_ref):
        i, f = pl.program_id(0), pl.program_id(1)
        dx3_b = dx3_ref[...].astype(BF16)
        sw = w1_ref.shape[2]
        part = None
        for s in range(w1_ref.shape[0]):
            cols = slice(s * sw, (s + 1) * sw)
            da2 = _dot_nt(dx3_b, w2_ref[cols, :])
            dz = (2.0 * a_ref[:, cols].astype(F32) * da2).astype(BF16)
            dz_ref[:, cols] = dz
            p_s = _dot_nt(dz, w1_ref[s])
            part = p_s if part is None else part + p_s

        @pl.when(f == 0)
        def _():
            acc_ref[...] = part

        @pl.when(f > 0)
        def _():
            acc_ref[...] += part

        @pl.when((i == 0) & (f == 0))
        def _():
            dg_ref[...] = jnp.zeros_like(dg_ref)

        @pl.when(f == nf - 1)
        def _():
            xt = x2_ref[...]
            dx, dg = _rms_bwd(acc_ref[...], xt, _rms(xt), g_ref[...])
            dx2_ref[...] = dx3_ref[...] + dx
            dg_ref[...] += dg

    return pl.pallas_call(
        body, name="mlp_bwd", grid=(t // tm, nf),
        in_specs=[pl.BlockSpec((tm, D_MODEL), lambda i, f: (i, 0)),
                  pl.BlockSpec((tm, tf), lambda i, f: (i, f)),
                  pl.BlockSpec((tf // w_1.shape[2], D_MODEL, w_1.shape[2]), lambda i, f: (f, 0, 0)),
                  pl.BlockSpec((tf, D_MODEL), lambda i, f: (f, 0)),
                  pl.BlockSpec((tm, D_MODEL), lambda i, f: (i, 0)),
                  pl.BlockSpec((1, D_MODEL), lambda i, f: (0, 0))],
        out_specs=[pl.BlockSpec((tm, tf), lambda i, f: (i, f)),
                   pl.BlockSpec((tm, D_MODEL), lambda i, f: (i, 0)),
                   pl.BlockSpec((1, D_MODEL), lambda i, f: (0, 0))],
        out_shape=[jax.ShapeDtypeStruct((t, D_FF), BF16), jax.ShapeDtypeStruct((t, D_MODEL), F32),
                   jax.ShapeDtypeStruct((1, D_MODEL), F32)],
        scratch_shapes=[pltpu.VMEM((tm, D_MODEL), F32)],
        compiler_params=_params(("arbitrary", "arbitrary")),
    )(dx3, a, w_1, w_2, x2, g_mlp)


def _wgrad(name, a, b, tka, tn, tm, square=False, col_shards=False):
    t, ka = a.shape
    n = b.shape[1]
    nk = t // tm

    def body(a_ref, b_ref, o_ref, acc_ref):
        at = a_ref[...].astype(BF16)
        if square:
            at = at * at
        part = _dot_tn(at, b_ref[...].astype(BF16))
        k = pl.program_id(2)

        @pl.when(k == 0)
        def _():
            acc_ref[...] = part

        @pl.when(k > 0)
        def _():
            acc_ref[...] += part

        @pl.when(k == nk - 1)
        def _():
            if col_shards:
                for s in range(tn // sw):
                    o_ref[s] = acc_ref[:, s * sw:(s + 1) * sw].astype(BF16)
            else:
                o_ref[...] = acc_ref[...].astype(BF16)

    if col_shards:
        sw = n // N_DEV
        out_spec = pl.BlockSpec((tn // sw, tka, sw), lambda p, q, k: (q, p, 0))
        out_shape = jax.ShapeDtypeStruct((N_DEV, ka, sw), BF16)
    else:
        out_spec = pl.BlockSpec((tka, tn), lambda p, q, k: (p, q))
        out_shape = jax.ShapeDtypeStruct((ka, n), BF16)
    return pl.pallas_call(
        body, name=name, grid=(ka // tka, n // tn, nk),
        in_specs=[pl.BlockSpec((tm, tka), lambda p, q, k: (k, p)),
                  pl.BlockSpec((tm, tn), lambda p, q, k: (k, q))],
        out_specs=out_spec, out_shape=out_shape,
        scratch_shapes=[pltpu.VMEM((tka, tn), F32)],
        compiler_params=_params(("arbitrary", "arbitrary", "arbitrary")),
    )(a, b)


def _cross_bwd(dx2, x1, q, kv, w_cq, w_co, g_cross, tm, dep=None):
    t = x1.shape[0]
    m = kv.shape[0]

    def body(dx2_ref, x1_ref, q_ref, kv_ref, wq_ref, wo_ref, g_ref, dq_ref, dx1_ref, dkv_ref, dg_ref):
        @pl.when(pl.program_id(0) == 0)
        def _():
            dkv_ref[...] = jnp.zeros_like(dkv_ref)
            dg_ref[...] = jnp.zeros_like(dg_ref)

        do = _dot_nt(dx2_ref[...].astype(BF16), wo_ref[...]).astype(BF16)
        q = q_ref[...]
        dqs = []
        for h in range(X_HEADS):
            hs = slice(h * X_HEAD_DIM, (h + 1) * X_HEAD_DIM)
            vs = slice(D_MODEL + h * X_HEAD_DIM, D_MODEL + (h + 1) * X_HEAD_DIM)
            p = _cross_probs(q, kv_ref, h)
            dp = _dot_nt(do[:, hs], kv_ref[:, vs])
            ds = (p * (dp - jnp.sum(dp * p, axis=-1, keepdims=True))).astype(BF16)
            dqs.append(_dot(ds, kv_ref[:, hs]))
            dkv_ref[:, hs] += _dot_tn(ds, q[:, hs])
            dkv_ref[:, vs] += _dot_tn(p.astype(BF16), do[:, hs])
        dq = (jnp.concatenate(dqs, axis=1) * X_SCALE).astype(BF16)
        dq_ref[...] = dq
        xt = x1_ref[...]
        dx, dg = _rms_bwd(_dot_nt(dq, wq_ref[...]), xt, _rms(xt), g_ref[...])
        dx1_ref[...] = dx2_ref[...] + dx
        dg_ref[...] += dg

    row = lambda w: pl.BlockSpec((tm, w), lambda i: (i, 0))
    full = lambda a, b: pl.BlockSpec((a, b), lambda i: (0, 0))
    return pl.pallas_call(
        _with_dep(body, 7, dep), name="cross_bwd", grid=(t // tm,),
        in_specs=[row(D_MODEL), row(D_MODEL), row(D_MODEL), full(m, 2 * D_MODEL), full(D_MODEL, D_MODEL),
                  full(D_MODEL, D_MODEL), full(1, D_MODEL)] + _dep_spec(dep),
        out_specs=[row(D_MODEL), row(D_MODEL), full(m, 2 * D_MODEL), full(1, D_MODEL)],
        out_shape=[jax.ShapeDtypeStruct((t, D_MODEL), BF16), jax.ShapeDtypeStruct((t, D_MODEL), F32),
                   jax.ShapeDtypeStruct((m, 2 * D_MODEL), F32), jax.ShapeDtypeStruct((1, D_MODEL), F32)],
        compiler_params=_params(("arbitrary",)),
    )(dx2, x1, q, kv, w_cq, w_co, g_cross, *_dep_arg(dep))


def _memkv_bwd(dkv, mn, mem, w_ckv, g_mem):
    ws = w_ckv.shape[2]

    def body(dkv_ref, mn_ref, mem_ref, w_ref, g_ref, dw_ref, dg_ref):
        mn = mn_ref[...]
        dmn = jnp.zeros(mn.shape, F32)
        for j in range(N_DEV):
            dkvb = dkv_ref[:, j * ws:(j + 1) * ws].astype(BF16)
            dw_ref[j] = _dot_tn(mn, dkvb).astype(BF16)
            dmn = dmn + _dot_nt(dkvb, w_ref[j])
        xt = mem_ref[...]
        dg_ref[...] = jnp.sum(dmn * xt * _rms(xt), axis=0, keepdims=True)

    return pl.pallas_call(
        body, name="memkv_bwd",
        out_shape=[jax.ShapeDtypeStruct(w_ckv.shape, BF16), jax.ShapeDtypeStruct((1, D_MODEL), F32)],
        compiler_params=_params(),
    )(dkv, mn, mem, w_ckv, g_mem)


def _merge_bwd(dx1, ya, yb, gts, w_out, w_g, tm):
    t = dx1.shape[0]

    def body(dx1_ref, ya_ref, yb_ref, g_ref, wo_ref, wg_ref, dg_ref, dhp_ref, dya_ref, dyb_ref, db_ref):
        @pl.when(pl.program_id(0) == 0)
        def _():
            db_ref[...] = jnp.zeros_like(db_ref)

        dm = _dot_nt(dx1_ref[...].astype(BF16), wo_ref[...])
        ga = g_ref[:, :D_MODEL].astype(F32)
        gb = g_ref[:, D_MODEL:].astype(F32)
        dya_ref[...] = (dm * ga).astype(BF16)
        dyb_ref[...] = (dm * gb).astype(BF16)
        dpa = dm * ya_ref[...].astype(F32) * ga * (1.0 - ga)
        dpb = dm * yb_ref[...].astype(F32) * gb * (1.0 - gb)
        dpre = jnp.concatenate([dpa, dpb], axis=1)
        db_ref[...] += jnp.sum(dpre, axis=0, keepdims=True)
        dpreb = dpre.astype(BF16)
        dg_ref[...] = dpreb
        dhp_ref[...] = _dot_nt(dpreb, wg_ref[...])

    row = lambda w: pl.BlockSpec((tm, w), lambda i: (i, 0))
    once = lambda a, b: pl.BlockSpec((a, b), lambda i: (0, 0), pipeline_mode=pl.Buffered(1))
    sds = jax.ShapeDtypeStruct
    return pl.pallas_call(
        body, name="merge_bwd", grid=(t // tm,),
        in_specs=[row(D_MODEL), row(D_MODEL), row(D_MODEL), row(GATE_WIDTH),
                  once(D_MODEL, D_MODEL), once(D_MODEL, GATE_WIDTH)],
        out_specs=[row(GATE_WIDTH), row(D_MODEL), row(D_MODEL), row(D_MODEL),
                   pl.BlockSpec((1, GATE_WIDTH), lambda i: (0, 0))],
        out_shape=[sds((t, GATE_WIDTH), BF16), sds((t, D_MODEL), F32), sds((t, D_MODEL), BF16),
                   sds((t, D_MODEL), BF16), sds((1, GATE_WIDTH), F32)],
        compiler_params=_params(("arbitrary",)),
    )(dx1, ya, yb, gts, w_out, w_g)


def _combine_bwd(dya, dyb, oa, ob, l0, l1, l2, lb, sink_row, w_a, w_b, tm):
    t = dya.shape[0]

    def body(dya_ref, dyb_ref, oa_ref, ob_ref, l0_ref, l1_ref, l2_ref, lb_ref, sk_ref, wa_ref, wb_ref,
             do0_ref, do1_ref, do2_ref, c0_ref, c1_ref, c2_ref, dob_ref, cb_ref, dsk_ref, so_ref, sl_ref):
        @pl.when(pl.program_id(0) == 0)
        def _():
            dsk_ref[...] = jnp.zeros_like(dsk_ref)

        doa = _dot_nt(dya_ref[...], wa_ref[...])
        dob = _dot_nt(dyb_ref[...], wb_ref[...])
        dsum = _head_sums(doa * oa_ref[...].astype(F32), _head_gather(256, True))
        a0, a1, a2 = _alphas(l0_ref[...], _interleave(l1_ref, sl_ref), _interleave(l2_ref, sl_ref))
        c0_ref[...] = a0 * dsum
        spread = _dil_head_spread()
        do0_ref[...] = _head_scale(doa, a0, spread).astype(BF16)
        for al, do_ref, c_ref in ((a1, do1_ref, c1_ref), (a2, do2_ref, c2_ref)):
            _deinterleave(al * dsum, sl_ref, c_ref, F32)
            _deinterleave(_head_scale(doa, al, spread), so_ref, do_ref, BF16)
        dob_ref[...] = dob.astype(BF16)
        cb = _head_sums(dob * ob_ref[...], _head_gather(128, False))
        cb_ref[...] = cb
        lane = lax.broadcasted_iota(jnp.int32, cb.shape, 1)
        psink = jnp.where(lane < 8, jnp.exp(sk_ref[...] - lb_ref[...]), 0.0)
        dsk_ref[...] += jnp.sum(-psink * cb, axis=0, keepdims=True)

    row = lambda w: pl.BlockSpec((tm, w), lambda i: (i, 0))
    full = lambda a, b: pl.BlockSpec((a, b), lambda i: (0, 0))
    sds = jax.ShapeDtypeStruct
    d1, d2 = l1.shape[0], l2.shape[0]
    res = lambda d, w: pl.BlockSpec((d, tm // d, w), lambda i: (0, i, 0))
    return pl.pallas_call(
        body, name="combine_bwd", grid=(t // tm,),
        in_specs=[row(D_MODEL), row(D_MODEL), row(512), row(512),
                  row(256), _res_spec(l1, tm), _res_spec(l2, tm), row(128), full(1, 128),
                  full(512, D_MODEL), full(512, D_MODEL)],
        out_specs=[row(512), res(d1, 512), res(d2, 512), row(256), res(d1, 256), res(d2, 256),
                   row(512), row(128), full(1, 128)],
        out_shape=[sds((t, 512), BF16), sds((d1, t // d1, 512), BF16),
                   sds((d2, t // d2, 512), BF16), sds((t, 256), F32), sds((d1, t // d1, 256), F32),
                   sds((d2, t // d2, 256), F32), sds((t, 512), BF16),
                   sds((t, 128), F32), sds((1, 128), F32)],
        scratch_shapes=[_lane_scratch(tm, 512), _lane_scratch(tm, 256)],
        compiler_params=_params(("arbitrary",)),
    )(dya, dyb, oa, ob, l0, l1, l2, lb, sink_row, w_a, w_b)


def _attn_bwd(name, pv, dov, lsev, cv, cosv, sinv, swa, tq, dep=None):
    d, ls = pv.shape[0], pv.shape[1]
    n, nsb = ls // tq, tq // BAND
    pairs = _attn_layout(swa)
    ncol = 1 if swa else 2
    ow = 128 * len(pairs)

    kv_slots = sorted({(ko, vo) for _, ko, vo in pairs})

    def body(cur_ref, tail_ref, do_ref, lse_ref, c_ref, cos_ref, sin_ref, out_ref, acc_ref, carry_ref, acct_ref):
        i = pl.program_id(2)
        blk_i = n - 1 - i
        acc_ref[...] = jnp.zeros_like(acc_ref)
        acct_ref[...] = jnp.zeros_like(acct_ref)

        @pl.when(i == 0)
        def _():
            carry_ref[...] = jnp.zeros_like(carry_ref)

        qk_a, v_a = _head_a_masks(BAND)
        dim = lax.broadcasted_iota(jnp.int32, (128, BAND), 0)
        qk_at, v_at = (dim % HEAD_DIM) < HEAD_DIM // 2, dim < HEAD_DIM
        for s in range(nsb):
            mask = _band_mask(blk_i, s)
            mask2 = jnp.concatenate([mask, mask], axis=0)
            rows = slice(s * BAND, (s + 1) * BAND)
            kcols = slice(s * BAND, (s + 2) * BAND)
            for j, (qo, ko, vo) in enumerate(pairs):
                slot = kv_slots.index((ko, vo))
                kk = _kv_rows(cur_ref, tail_ref, s, ko)
                vv = _kv_rows(cur_ref, tail_ref, s, vo)
                q, do = cur_ref[rows, qo:qo + 128], do_ref[rows, j * 128:(j + 1) * 128]
                q2, do2 = _stack_heads(q, qk_a), _stack_heads(do, v_a)
                col2 = lambda ref: jnp.concatenate([ref[rows, 2 * j:2 * j + 1], ref[rows, 2 * j + 1:2 * j + 2]], axis=0)
                sc = _dot_nt(q2, kk)
                p = jnp.exp(jnp.where(mask2, sc, -jnp.inf) - col2(lse_ref))
                dp = _dot_nt(do2, vv)
                ds = (p * (dp - col2(c_ref))).astype(BF16)
                dq2 = _dot(ds, kk)
                acc_ref[BAND + s * BAND:BAND + (s + 1) * BAND, qo:qo + 128] += jnp.where(qk_a, dq2[:BAND], dq2[BAND:])
                acct_ref[2 * slot, :, kcols] += _dot(_stack_heads_t(q.T, qk_at), ds)
                acct_ref[2 * slot + 1, :, kcols] += _dot(_stack_heads_t(do.T, v_at), p.astype(BF16))
        for slot, (ko, vo) in enumerate(kv_slots):
            acc_ref[:, ko:ko + 128] += acct_ref[2 * slot].T
            acc_ref[:, vo:vo + 128] += acct_ref[2 * slot + 1].T

        last = acc_ref[tq:, :] + carry_ref[...]
        fin = last if tq == BAND else jnp.concatenate([acc_ref[BAND:tq, :], last], axis=0)
        out_ref[...] = _rope(fin, cos_ref[...], sin_ref[...], swa, -1).astype(BF16)
        carry_ref[...] = acc_ref[0:BAND, :]

    rev = lambda i: n - 1 - i
    blk = lambda rows, w, row_of: pl.BlockSpec((None, rows, w), lambda r, cb, i: (r, row_of(i), cb))
    tab = pl.BlockSpec((None, tq, 128), lambda r, cb, i: (r, rev(i), 0))
    return pl.pallas_call(
        _with_dep(body, 7, dep), name=name, grid=(d, ncol, n),
        in_specs=[blk(tq, PBLK, rev), blk(BAND, PBLK, lambda i: jnp.maximum(rev(i) * nsb - 1, 0)),
                  blk(tq, ow, rev), blk(tq, 128, rev), blk(tq, 128, rev), tab, tab] + _dep_spec(dep),
        out_specs=blk(tq, PBLK, rev),
        out_shape=jax.ShapeDtypeStruct((d, ls, ncol * PBLK), BF16),
        scratch_shapes=[pltpu.VMEM((tq + BAND, PBLK), F32), pltpu.VMEM((BAND, PBLK), F32),
                        pltpu.VMEM((2 * len(kv_slots), 128, tq + BAND), F32)],
        compiler_params=_params(("arbitrary", "arbitrary", "arbitrary")),
    )(pv, pv, dov, lsev, cv, cosv, sinv, *_dep_arg(dep))


def _dx(dp0, dp1, dp2, dpb, w_p_t, dh_part, dx1, x, g_mix, tm, dep=None):
    t = x.shape[0]
    gw = 2 * PBLK

    def body(dp0_ref, dp1_ref, dp2_ref, dpb_ref, w_ref, dhp_ref, dx1_ref, x_ref, g_ref, gx_ref, dg_ref,
             dpt_ref, scr_ref):
        @pl.when(pl.program_id(0) == 0)
        def _():
            dg_ref[...] = jnp.zeros_like(dg_ref)

        dpt_ref[:, 0:gw] = dp0_ref[...]
        dpt_ref[:, gw:2 * gw] = _interleave(dp1_ref, scr_ref).astype(BF16)
        dpt_ref[:, 2 * gw:3 * gw] = _interleave(dp2_ref, scr_ref).astype(BF16)
        dpt_ref[:, 3 * gw:] = dpb_ref[...]
        dh = _dot(dpt_ref[...], w_ref[...]) + dhp_ref[...]
        xt = x_ref[...]
        dx, dg = _rms_bwd(dh, xt, _rms(xt), g_ref[...])
        gx_ref[...] = dx1_ref[...] + dx
        dg_ref[...] += dg

    row = lambda w: pl.BlockSpec((tm, w), lambda i: (i, 0))
    full = lambda a, b: pl.BlockSpec((a, b), lambda i: (0, 0))
    return pl.pallas_call(
        _with_dep(body, 9, dep), name="dx", grid=(t // tm,),
        in_specs=[row(gw), _res_spec(dp1, tm), _res_spec(dp2, tm), row(PBLK),
                  pl.BlockSpec((P_WIDTH, D_MODEL), lambda i: (0, 0), pipeline_mode=pl.Buffered(1)),
                  row(D_MODEL), row(D_MODEL), row(D_MODEL), full(1, D_MODEL)] + _dep_spec(dep),
        out_specs=[row(D_MODEL), full(1, D_MODEL)],
        out_shape=[jax.ShapeDtypeStruct((t, D_MODEL), F32), jax.ShapeDtypeStruct((1, D_MODEL), F32)],
        scratch_shapes=[pltpu.VMEM((tm, P_WIDTH), BF16), _lane_scratch(tm, gw)],
        compiler_params=_params(("arbitrary",)),
    )(dp0, dp1, dp2, dpb, w_p_t, dh_part, dx1, x, g_mix, *_dep_arg(dep))


MESH = pl.DeviceIdType.MESH
HBM_SPEC = pl.BlockSpec(memory_space=pltpu.HBM)
VMEM_SPEC = pl.BlockSpec(memory_space=pltpu.VMEM)


def _all_gather(xp, act, g, tm):
    t = act.shape[0]
    d1, d2 = DIL_GROUPS[1][1], DIL_GROUPS[2][1]

    def body(x_ref, act_ref, g_ref, out_ref, h_ref, h1_ref, h2_ref, send_sems, recv_sems, local_sem, hf_ref):
        x, y, c = lax.axis_index("x"), lax.axis_index("y"), lax.axis_index("c")
        me, sibling = (x, y, c), (x, y, 1 - c)
        chips = [(1 - x, y), (x, 1 - y), (1 - x, 1 - y)]

        def rows(px, py, pc):
            return out_ref.at[4 * px + 2 * py + pc]

        def copy(k, block, to, src=None):
            return pltpu.make_async_remote_copy(
                src_ref=rows(*block) if src is None else src, dst_ref=rows(*block),
                send_sem=send_sems.at[k], recv_sem=recv_sems.at[k], device_id=to, device_id_type=MESH)

        mine = pltpu.make_async_copy(x_ref, rows(*me), local_sem)
        mine.start()
        first = [copy(0, me, sibling, src=x_ref)]
        first += [copy(1 + j, me, (*chip, c), src=x_ref) for j, chip in enumerate(chips)]
        for cp in first:
            cp.start()

        def norm(a_blk, h_blk, h1_blk, h2_blk):
            xt = a_blk[...]
            hf = xt * _rms(xt) * g_ref[...]
            h_blk[...] = hf.astype(BF16)
            _deinterleave(hf, hf_ref, h1_blk, BF16)
            _deinterleave(hf, hf_ref, h2_blk, BF16)

        res = lambda d: pl.BlockSpec((d, tm // d, D_MODEL), lambda i: (0, i, 0))
        row = pl.BlockSpec((tm, D_MODEL), lambda i: (i, 0))
        pltpu.emit_pipeline(norm, grid=(t // tm,), in_specs=[row], out_specs=[row, res(d1), res(d2)])(
            act_ref, h_ref, h1_ref, h2_ref)

        passed = [copy(4 + j, (*chip, c), sibling) for j, chip in enumerate(chips)]
        for j, chip in enumerate(chips):
            copy(1 + j, (*chip, c), me).wait_recv()
            passed[j].start()
        copy(0, sibling, me).wait_recv()
        for j, chip in enumerate(chips):
            copy(4 + j, (*chip, 1 - c), me).wait_recv()
        for cp in first + passed:
            cp.wait_send()
        mine.wait()

    sds = jax.ShapeDtypeStruct
    return pl.pallas_call(
        body, name="all_gather",
        out_shape=[sds((N_DEV,) + xp.shape, xp.dtype), sds((t, D_MODEL), BF16),
                   sds((d1, t // d1, D_MODEL), BF16), sds((d2, t // d2, D_MODEL), BF16)],
        in_specs=[HBM_SPEC, HBM_SPEC, VMEM_SPEC], out_specs=[HBM_SPEC] * 4,
        scratch_shapes=[pltpu.SemaphoreType.DMA((7,)), pltpu.SemaphoreType.DMA((7,)), pltpu.SemaphoreType.DMA,
                        _lane_scratch(tm, D_MODEL)],
        compiler_params=pltpu.CompilerParams(vmem_limit_bytes=VMEM_LIMIT),
    )(xp, act, g)


def _peers():
    x, y, c = lax.axis_index("x"), lax.axis_index("y"), lax.axis_index("c")
    out = []
    for k in range(1, N_DEV):
        px = 1 - x if k & 4 else x
        py = 1 - y if k & 2 else y
        pc = 1 - c if k & 1 else c
        out.append((k, (px, py, pc), 4 * px + 2 * py + pc))
    return out


def _my_index():
    return 4 * lax.axis_index("x") + 2 * lax.axis_index("y") + lax.axis_index("c")


SEM_SPEC = pl.BlockSpec(memory_space=pltpu.SEMAPHORE)
ANY_SPEC = pl.BlockSpec(memory_space=pl.ANY)
_SPLIT_PARAMS = pltpu.CompilerParams(has_side_effects=pltpu.SideEffectType.DATAFLOW_SIDE_EFFECTING)


def _split_copies(gather, src_refs, land_refs, send_sems, recv_sems):
    me_idx = _my_index()
    out = []
    for a, (src_ref, land_ref) in enumerate(zip(src_refs, land_refs)):
        for k, peer, peer_idx in _peers():
            if gather:
                src, dst = src_ref, land_ref.at[me_idx]
            else:
                src, dst = src_ref.at[peer_idx], land_ref.at[k - 1]
            out.append(pltpu.make_async_remote_copy(
                src_ref=src, dst_ref=dst, send_sem=send_sems.at[7 * a + k - 1], recv_sem=recv_sems.at[7 * a + k - 1],
                device_id=peer, device_id_type=MESH))
    return out


def _split_start(name, gather, srcs, after=None):
    n = len(srcs)
    extra = [] if after is None else [after]
    first_out = n + len(extra)

    def body(*refs):
        send_sems, recv_sems = refs[first_out], refs[first_out + 1]
        lands = refs[first_out + 2 + n:first_out + 2 + 2 * n]
        for cp in _split_copies(gather, refs[:n], lands, send_sems, recv_sems):
            cp.start()
        token = refs[-1]
        token[...] = jnp.zeros_like(token)

    lands = [pltpu.HBM((N_DEV,) + a.shape if gather else (N_DEV - 1,) + a.shape[1:], a.dtype) for a in srcs]
    return pl.pallas_call(
        body, name=name,
        out_shape=(pltpu.SemaphoreType.DMA((7 * n,)), pltpu.SemaphoreType.DMA((7 * n,)),
                   *[pltpu.HBM(a.shape, a.dtype) for a in srcs], *lands, jax.ShapeDtypeStruct((8, 128), F32)),
        in_specs=(HBM_SPEC,) * n + (ANY_SPEC,) * len(extra),
        out_specs=(SEM_SPEC, SEM_SPEC) + (HBM_SPEC,) * (2 * n) + (VMEM_SPEC,),
        input_output_aliases={i: 2 + i for i in range(n)}, compiler_params=_SPLIT_PARAMS,
    )(*[pltpu.with_memory_space_constraint(a, pltpu.HBM) for a in srcs], *extra)


def _split_wait(name, gather, started, after):
    send_sems, recv_sems, bufs = started[0], started[1], started[2:-1]
    n = len(bufs) // 2

    def body(*refs):
        for cp in _split_copies(gather, refs[:n], refs[n:2 * n], refs[2 * n], refs[2 * n + 1]):
            cp.wait_send()
            cp.wait_recv()

    out = pl.pallas_call(
        body, name=name, out_shape=tuple(pltpu.HBM(a.shape, a.dtype) for a in bufs),
        in_specs=(HBM_SPEC,) * (2 * n) + (SEM_SPEC, SEM_SPEC, ANY_SPEC), out_specs=(HBM_SPEC,) * (2 * n),
        input_output_aliases={i: i for i in range(2 * n)}, compiler_params=_SPLIT_PARAMS,
    )(*bufs, send_sems, recv_sems, after)
    return out[:n], out[n:]


def _adam_update(g, w, m, v):
    nm = ADAM_B1 * m + (1.0 - ADAM_B1) * g
    nv = ADAM_B2 * v + (1.0 - ADAM_B2) * (g * g)
    m_hat = nm / (1.0 - ADAM_B1 ** ADAM_STEP)
    v_hat = nv / (1.0 - ADAM_B2 ** ADAM_STEP)
    return -ADAM_LR * (m_hat / (jnp.sqrt(v_hat) + ADAM_EPS) + ADAM_WD * w), nm, nv


def _adamw(name, me, sent, got, w, m, v, tr, tc=None):
    r, c = w.shape
    tc = c if tc is None else tc

    def body(me_ref, own_ref, got_ref, w_ref, m_ref, v_ref, g_ref, d_ref, nm_ref, nv_ref):
        g = own_ref[...].astype(F32)
        for k in range(N_DEV - 1):
            g = g + got_ref[k].astype(F32)
        g_ref[...] = g
        d_ref[...], nm_ref[...], nv_ref[...] = _adam_update(g, w_ref[...], m_ref[...], v_ref[...])

    blk = pl.BlockSpec((tr, tc), lambda i, j, me_ref: (i, j))
    return pl.pallas_call(
        body, name=name,
        grid_spec=pltpu.PrefetchScalarGridSpec(
            num_scalar_prefetch=1, grid=(r // tr, c // tc),
            in_specs=[pl.BlockSpec((None, tr, tc), lambda i, j, me_ref: (me_ref[0], i, j)),
                      pl.BlockSpec((N_DEV - 1, tr, tc), lambda i, j, me_ref: (0, i, j)), blk, blk, blk],
            out_specs=[blk] * 4),
        out_shape=[jax.ShapeDtypeStruct((r, c), F32)] * 4,
        compiler_params=_params(("arbitrary", "arbitrary")),
    )(me, sent, got, w, m, v)


def _adamw_small(srecv, ws, ms, vs):
    nv_ = len(ws)

    def body(*refs):
        s_ref = refs[0]
        ins, outs = refs[1:1 + 3 * nv_], refs[1 + 3 * nv_:]
        g_all = s_ref[0]
        for k in range(1, N_DEV):
            g_all = g_all + s_ref[k]
        for i in range(nv_):
            n = ins[i].shape[1]
            g = g_all[i:i + 1, :n]
            d, nm, nv = _adam_update(g, ins[i][...], ins[nv_ + i][...], ins[2 * nv_ + i][...])
            outs[i][...], outs[nv_ + i][...], outs[2 * nv_ + i][...], outs[3 * nv_ + i][...] = g, d, nm, nv
        outs[-1][...] = g_all[nv_:nv_ + 1, :128]

    shapes = [jax.ShapeDtypeStruct(a.shape, F32) for a in ws]
    res = pl.pallas_call(body, name="adamw_small", out_shape=shapes * 4 + [jax.ShapeDtypeStruct((1, 128), F32)],
                         compiler_params=_params())(srecv, *ws, *ms, *vs)
    return [res[k * nv_:(k + 1) * nv_] for k in range(4)], res[-1]


def _cols_from_shards(a):
    return jnp.swapaxes(a, 0, 1).reshape(a.shape[1], a.shape[0] * a.shape[2])


def _shards_from_cols(a):
    return jnp.swapaxes(a.reshape(a.shape[0], N_DEV, a.shape[1] // N_DEV), 0, 1)


def _shards_from_rows(a):
    return a.reshape(N_DEV, a.shape[0] // N_DEV, a.shape[1])


def _pair_lanes(a):
    lead = a.shape[:-1]
    return a.reshape(lead + (2, 2, HEAD_DIM // 2)).swapaxes(-3, -2).reshape(lead + (128,))


def _split_w_in(w_in):
    rows = w_in.shape[0]
    dil = w_in[:, :3 * DIL_WIDTH].reshape(rows, 3, 3, 4, 128)
    dil = np.concatenate([_pair_lanes(dil[:, :2]), dil[:, 2:]], axis=1)
    dil = dil.transpose(0, 2, 3, 1, 4).reshape(rows, 3 * DIL_WIDTH)
    o = 3 * DIL_WIDTH
    qb = w_in[:, o:o + SWA_Q_WIDTH].reshape(rows, 2, 4, HEAD_DIM).transpose(0, 2, 1, 3).reshape(rows, 4, 128)
    qb = _pair_lanes(qb).reshape(rows, SWA_Q_WIDTH)
    kb = _pair_lanes(w_in[:, o + SWA_Q_WIDTH:o + SWA_Q_WIDTH + SWA_KV_WIDTH])
    vb = w_in[:, o + SWA_Q_WIDTH + SWA_KV_WIDTH:P_WIDTH]
    return np.concatenate([dil, qb, kb, vb], axis=1)


ROW_GRANULE = HEAD_DIM // 2


def _w_p_granules():
    order = _split_w_in(np.arange(IN_WIDTH)[None])[0]
    assert sorted(order.tolist()) == list(range(P_WIDTH))
    first = order[::ROW_GRANULE]
    assert (first % ROW_GRANULE == 0).all() and (order.reshape(-1, ROW_GRANULE) == first[:, None] + np.arange(ROW_GRANULE)).all()
    return first // ROW_GRANULE


def _gather_rows(name, src, granules, per_step):
    n, cols = len(granules), src.shape[1]
    assert n % per_step == 0

    def body(tab_ref, *refs):
        out_ref = refs[per_step]
        for j in range(per_step):
            out_ref[j * ROW_GRANULE:(j + 1) * ROW_GRANULE, :] = refs[j][...]

    def pick(j):
        return pl.BlockSpec((ROW_GRANULE, cols), lambda i, tab_ref: (tab_ref[per_step * i + j], 0))

    return pl.pallas_call(
        body, name=name,
        grid_spec=pltpu.PrefetchScalarGridSpec(
            num_scalar_prefetch=1, grid=(n // per_step,),
            in_specs=[pick(j) for j in range(per_step)],
            out_specs=pl.BlockSpec((per_step * ROW_GRANULE, cols), lambda i, tab_ref: (i, 0))),
        out_shape=jax.ShapeDtypeStruct((n * ROW_GRANULE, cols), src.dtype),
        compiler_params=_params(("arbitrary",)),
    )(jnp.asarray(granules, jnp.int32), *([src] * per_step))


def _swa_rows(w_b):
    return w_b.reshape(2, 4, HEAD_DIM, -1).transpose(1, 0, 2, 3).reshape(SWA_Q_WIDTH, -1)


def _swa_rows_inv(dw_b):
    return dw_b.reshape(4, 2, HEAD_DIM, -1).transpose(1, 0, 2, 3).reshape(SWA_Q_WIDTH, -1)


def _rope_tables(pos):
    half = HEAD_DIM // 2
    inv = ROPE_THETA ** (-jnp.arange(half, dtype=F32) / half)
    ang = pos.astype(F32)[:, None] * jnp.tile(inv, 4)
    sign = jnp.repeat(jnp.array([-1.0, 1.0], F32), 2 * half)
    return jnp.cos(ang), jnp.sin(ang) * sign


def _local_step(x, hs, mem, pos, target, w_in_t, dep, rest_weights, on_grads, g_mix, g_cross, g_mem, g_mlp, g_final, sink):
    t = x.shape[0]
    tm = min(512, t)
    tq = 1024
    tw = min(2048, t)
    w_p_t = _gather_rows("w_p_rows", w_in_t, _w_p_granules(), 24)
    cos, sin = lax.optimization_barrier(_rope_tables(pos))
    sink_row = jnp.pad(sink.reshape(2, 4).T.reshape(1, 8), ((0, 0), (0, 120)))
    tabs = [(cos[None], sin[None])]
    for _, d in DIL_GROUPS[1:]:
        tabs.append(tuple(a.reshape(t // d, d, 128).swapaxes(0, 1) for a in (cos, sin)))
    tabs.append(tabs[0])

    h, h1, h2 = hs
    p0, p1, p2, pb = _inproj(h, h1, h2, w_p_t, [(cos, sin), tabs[1], tabs[2]], tm, dep)
    ps = [p0[None], p1, p2, pb[None]]
    outs, lses = [], []
    for gi, pv in enumerate(ps):
        res = _attn_fwd(f"attn_fwd{gi}", pv, gi == 3, sink_row[0, :8], min(tq, pv.shape[1]))
        outs.append(res[0])
        lses.append(res[1])
    o0, l0, ob, lb, ob32 = outs[0][0], lses[0][0], outs[3][0], lses[3][0], res[2][0]
    wts = rest_weights(GROUP_B, lb)
    w_b = _swa_rows(wts["w_branch_b"])
    tf = 2048
    w_g = wts["w_g"]
    gts = _gates(h, w_g, wts["b_gate"].reshape(1, GATE_WIDTH), min(1024, t), 1024)
    oa, ya, yb, merged, x1, hc = _mix(o0, outs[1], outs[2], l0, lses[1], lses[2], ob, gts, x,
                                      wts["w_branch_a"], w_b, wts["w_out"], g_cross, tm)
    mn, kv = _memkv(mem, g_mem, wts["w_ckv"])
    q, o, x2, hm = _cross(hc, x1, kv, wts["w_cq"], wts["w_co"], g_mlp, tm)
    wts.update(rest_weights(GROUP_A, hm))
    a, dx3, loss, dg_final = _mlp(hm, x2, wts["w_1"], wts["w_2"], g_final.reshape(1, D_MODEL), target, tm, tf)

    grads = {}
    dz, dx2, dg_mlp = _mlp_bwd(dx3, a, wts["w_1"], wts["w_2"], x2, g_mlp, tm, tf)
    grads["w_2"] = _shards_from_rows(_wgrad("dw_2", a, dx3, 1024, 1024, tw, square=True))
    grads["w_1"] = _wgrad("dw_1", hm, dz, 1024, 1024, tw, col_shards=True)
    dep = on_grads(GROUP_A, grads)
    dq, dx1, dkv, dg_cross = _cross_bwd(dx2, x1, q, kv, wts["w_cq"], wts["w_co"], g_cross, tm, dep)
    grads["w_co"] = _shards_from_rows(_wgrad("dw_co", o, dx2, 1024, 1024, tw))
    grads["w_cq"] = _shards_from_rows(_wgrad("dw_cq", hc, dq, 1024, 1024, tw))
    grads["w_ckv"], dg_mem = _memkv_bwd(dkv, mn, mem, wts["w_ckv"], g_mem)
    dgt, dh_part, dya, dyb, db_gate = _merge_bwd(dx1, ya, yb, gts, wts["w_out"], w_g, tm)
    do0, do1, do2, c0, c1, c2, dob, cb, dsink = _combine_bwd(
        dya, dyb, oa, ob32, l0, lses[1], lses[2], lb, sink_row, wts["w_branch_a"], w_b, tm)
    grads["w_out"] = _shards_from_rows(_wgrad("dw_out", merged, dx1, 1024, 1024, tw))
    grads["w_branch_a"] = _shards_from_cols(_wgrad("dw_a", oa, dya, 512, 1024, tw))
    grads["w_branch_b"] = _shards_from_cols(_swa_rows_inv(_wgrad("dw_b", ob, dyb, 512, 1024, tw)))
    grads["b_gate"] = _shards_from_cols(db_gate.reshape(2, D_MODEL)).astype(BF16)
    dep = on_grads(GROUP_B, grads)
    dw_g_t = _wgrad("dw_g", dgt, h, 1024, 1024, tw)
    dps = []
    for gi, (pv, do_g, c_g) in enumerate(zip(ps, (do0[None], do1, do2, dob[None]), (c0[None], c1, c2, cb[None]))):
        dps.append(_attn_bwd(f"attn_bwd{gi}", pv, do_g, lses[gi], c_g, tabs[gi][0], tabs[gi][1], gi == 3,
                             min(tq, pv.shape[1]),
                             dep if gi == 0 else None))
    dw_p_t = [_wgrad(f"dw_p{gi}", dpg.reshape(t, -1), hh.reshape(t, D_MODEL), PBLK, 1024, tw)
              for gi, (hh, dpg) in enumerate(zip((h, h1, h2, h), dps))]
    back = np.concatenate([np.argsort(_w_p_granules()),
                           np.arange(P_WIDTH // ROW_GRANULE, IN_WIDTH // ROW_GRANULE)])
    dw_in_t = _gather_rows("dw_in_rows", jnp.concatenate(dw_p_t + [dw_g_t], axis=0), back, 29)
    grads["w_in"] = dw_in_t.reshape(N_DEV, IN_WIDTH // N_DEV, D_MODEL)
    dep = on_grads(GROUP_C, grads)
    grad_x, dg_mix = _dx(dps[0][0], dps[1], dps[2], dps[3][0], w_p_t, dh_part, dx1, x, g_mix, tm, dep)
    dsink_heads = dsink[0, :8].reshape(4, 2).T.reshape(8)
    small = {"g_mix": dg_mix[0], "g_cross": dg_cross[0], "g_mem": dg_mem[0], "g_mlp": dg_mlp[0],
             "g_final": dg_final[0], "sink": dsink_heads}
    return loss[0, 0], grad_x, small


def kernel(x, mem, positions, g_mix, w_in, b_gate, sink, w_branch_a, w_branch_b, w_out, g_cross, g_mem, w_cq, w_ckv, w_co, g_mlp, w_1, w_2, g_final, loss_target, m_g_mix, m_w_in, m_b_gate, m_sink, m_w_branch_a, m_w_branch_b, m_w_out, m_g_cross, m_g_mem, m_w_cq, m_w_ckv, m_w_co, m_g_mlp, m_w_1, m_w_2, m_g_final, v_g_mix, v_w_in, v_b_gate, v_sink, v_w_branch_a, v_w_branch_b, v_w_out, v_g_cross, v_g_mem, v_w_cq, v_w_ckv, v_w_co, v_g_mlp, v_w_1, v_w_2, v_g_final):
    local = dict(locals())
    shard = {n: local[n][0] for n in GROUP_A + GROUP_B + GROUP_C}
    me = _my_index()
    me_arr = me.reshape(1).astype(jnp.int32)
    tags = {GROUP_A: "a", GROUP_B: "b", GROUP_C: "c"}

    transposed = lambda a: jnp.swapaxes(a, 0, 1)
    gathered_w_in, *hs = _all_gather(transposed(shard["w_in"]).astype(BF16), x[0], g_mix, min(512, x.shape[1]))
    w_in_t = gathered_w_in.reshape(-1, gathered_w_in.shape[2])

    def gathered(name, started, after):
        srcs, lands = _split_wait(name, True, started, after)
        return [lax.dynamic_update_slice(land, src[None], (me,) + (0,) * src.ndim) for src, land in zip(srcs, lands)]

    def start_gather(names, after=None):
        return _split_start("gather_start_" + tags[names], True,
                            [shard[n] if n == "b_gate" else shard[n].astype(BF16) for n in names], after)

    gathers = {GROUP_B: start_gather(GROUP_B)}
    gathers[GROUP_A] = start_gather(GROUP_A, gathers[GROUP_B][-1])

    def rest_weights(names, after):
        full = {"w_g": transposed(w_in_t[P_WIDTH:])} if names == GROUP_B else {}
        for name, a in zip(names, gathered("gather_wait_" + tags[names], gathers[names], after)):
            if name in ("w_1", "w_ckv"):
                full[name] = a
            elif name in _COL_SHARDED:
                full[name] = _cols_from_shards(a)
            else:
                full[name] = a.reshape(N_DEV * a.shape[1], a.shape[2])
        return full

    scatters = {}

    def on_grads(names, grads):
        scatters[names] = _split_start("scatter_start_" + tags[names], False, [grads[n] for n in names])
        return scatters[names][-1]

    loss, grad_x, small = _local_step(
        x[0], hs, mem[0], positions[0], loss_target[0], w_in_t, gathers[GROUP_A][-1], rest_weights, on_grads,
        g_mix, g_cross, g_mem, g_mlp, g_final, sink[0])

    sp = jnp.stack([small[n] if n != "sink" else jnp.pad(small[n], (0, LANES - 8)) for n in SMALL]
                   + [jnp.pad(loss.reshape(1), (0, LANES - 1)), jnp.zeros((LANES,), F32)])
    small_gather = _split_start("small_start", True, [sp])

    after, updated = small_gather[-1], {}
    for names in (GROUP_A, GROUP_B, GROUP_C):
        sent, got = _split_wait("scatter_wait_" + tags[names], False, scatters[names], after)
        for i, name in enumerate(names):
            view = transposed if name == "w_in" else (lambda a: a)
            outs = _adamw("adamw_" + name, me_arr, sent[i], got[i], view(shard[name]),
                          view(local["m_" + name][0]), view(local["v_" + name][0]), ADAM_ROWS[name], ADAM_COLS.get(name))
            updated[name] = [view(a)[None] for a in outs]
            after = outs[3]

    flat = lambda prefix: [local[prefix + n].reshape(1, -1) for n in SMALL]
    outs, loss_row = _adamw_small(gathered("small_wait", small_gather, after)[0], flat(""), flat("m_"), flat("v_"))
    for i, name in enumerate(SMALL):
        updated[name] = [outs[which][i].reshape(local[name].shape) for which in range(4)]

    order = ["g_mix", "w_in", "b_gate", "sink", "w_branch_a", "w_branch_b", "w_out", "g_cross", "g_mem", "w_cq",
             "w_ckv", "w_co", "g_mlp", "w_1", "w_2", "g_final"]
    res = [loss_row[0, 0], grad_x[None]]
    for which in range(4):
        res += [updated[n][which] for n in order]
    return tuple(res)
```

```python
import functools
import math

import jax
import jax.numpy as jnp
import numpy as np
from jax import lax
from jax.experimental import pallas as pl
from jax.experimental.pallas import tpu as pltpu

F32 = jnp.float32
BF16 = jnp.bfloat16

D_MODEL = 1024
HEAD_DIM = 64
DIL_GROUPS = ((128, 1), (512, 4), (2048, 16))
ROPE_THETA = 10000.0
X_HEADS = 4
X_HEAD_DIM = D_MODEL // X_HEADS
D_FF = 4 * D_MODEL
EPS = 1e-6
DIL_WIDTH = 1536
SWA_Q_WIDTH = 512
SWA_KV_WIDTH = 128
P_WIDTH = 3 * DIL_WIDTH + SWA_Q_WIDTH + 2 * SWA_KV_WIDTH
GATE_WIDTH = 2 * D_MODEL
IN_WIDTH = P_WIDTH + GATE_WIDTH
BAND = 128
PBLK = 768
Q_SCALE = HEAD_DIM ** -0.5
X_SCALE = X_HEAD_DIM ** -0.5

ADAM_LR = 0.001
ADAM_B1 = 0.9
ADAM_B2 = 0.999
ADAM_EPS = 1e-08
ADAM_WD = 0.01
ADAM_STEP = 10

N_DEV = 8
LANES = 1024
VMEM_LIMIT = 52 * 1024 * 1024

NT = (((1,), (1,)), ((), ()))
TN = (((0,), (0,)), ((), ()))

GROUP_A = ("w_1", "w_2")
GROUP_B = ("w_branch_a", "w_branch_b", "w_out", "w_cq", "w_ckv", "w_co", "b_gate")
GROUP_C = ("w_in",)
_COL_SHARDED = ("w_in", "w_branch_a", "w_branch_b", "w_ckv", "w_1", "b_gate")
ADAM_ROWS = {"w_in": 464, "w_branch_a": 512, "w_branch_b": 512, "w_out": 128, "w_cq": 128, "w_ckv": 512,
             "w_co": 128, "w_1": 256, "w_2": 256, "b_gate": 2}
ADAM_COLS = {"w_in": 256}
SMALL = ("g_mix", "g_cross", "g_mem", "g_mlp", "g_final", "sink")


def _params(sem=None):
    return pltpu.CompilerParams(dimension_semantics=sem, vmem_limit_bytes=VMEM_LIMIT)


def _dot(a, b):
    return jnp.dot(a, b, preferred_element_type=F32)


def _dot_nt(a, b):
    return lax.dot_general(a, b, NT, preferred_element_type=F32)


def _dot_tn(a, b):
    return lax.dot_general(a, b, TN, preferred_element_type=F32)


def _rms(xt):
    return lax.rsqrt(jnp.mean(xt * xt, axis=-1, keepdims=True) + EPS)


def _rms_bwd(dh, xt, r, g):
    xn = xt * r
    dxn = dh * g
    dx = r * (dxn - xn * jnp.mean(dxn * xn, axis=-1, keepdims=True))
    return dx, jnp.sum(dh * xn, axis=0, keepdims=True)


def _rope(x, c, s, swa, sign):
    kinds = "qqqqkv" if swa else "qkvqkv"
    cq, sq = c * Q_SCALE, s * (sign * Q_SCALE)
    sk = s * sign if sign != 1 else s
    out = []
    for ci, kind in enumerate(kinds):
        xc = x[:, ci * 128:(ci + 1) * 128]
        if kind == "v":
            out.append(xc)
        elif kind == "q":
            out.append(xc * cq + pltpu.roll(xc, 64, 1) * sq)
        else:
            out.append(xc * c + pltpu.roll(xc, 64, 1) * sk)
    return jnp.concatenate(out, axis=1)


def _lane_scratch(rows, w):
    return pltpu.VMEM((w // 128, rows, 128), F32)


def _deinterleave(val, scr_ref, dst_ref, dtype):
    d, n = dst_ref.shape[0], dst_ref.shape[1]
    nc = val.shape[1] // 128
    for c in range(nc):
        scr_ref[c] = val[:, c * 128:(c + 1) * 128]
    for r in range(d):
        rows = [scr_ref.at[c][pl.ds(r, n, stride=d), :] for c in range(nc)]
        dst_ref[r] = jnp.concatenate(rows, axis=1).astype(dtype)


def _res_spec(a, tm):
    d, w = a.shape[0], a.shape[2]
    return pl.BlockSpec((d, tm // d, w), lambda i: (0, i, 0))


def _interleave(src_ref, scr_ref):
    d, n = src_ref.shape[0], src_ref.shape[1]
    nc = src_ref.shape[2] // 128
    for r in range(d):
        v = src_ref[r].astype(F32)
        for c in range(nc):
            scr_ref.at[c][pl.ds(r, n, stride=d), :] = v[:, c * 128:(c + 1) * 128]
    return jnp.concatenate([scr_ref[c] for c in range(nc)], axis=1)


def _with_dep(body, n_in, dep):
    if dep is None:
        return body
    return lambda *refs: body(*refs[:n_in], *refs[n_in + 1:])


def _dep_spec(dep):
    return [] if dep is None else [pl.BlockSpec(memory_space=pl.ANY)]


def _dep_arg(dep):
    return [] if dep is None else [dep]


def _inproj(h, h1, h2, w_p_t, tabs, tm, dep=None):
    t = h.shape[0]
    gw = 2 * PBLK
    (cos, sin), (cos1, sin1), (cos2, sin2) = tabs[0], tabs[1], tabs[2]

    def body(h_ref, h1_ref, h2_ref, w_ref, c_ref, s_ref, c1_ref, s1_ref, c2_ref, s2_ref,
             p0_ref, p1_ref, p2_ref, pb_ref):
        rows = lambda ref: ref[...].reshape(tm, ref.shape[-1])
        groups = ((h_ref, c_ref, s_ref, p0_ref), (h1_ref, c1_ref, s1_ref, p1_ref), (h2_ref, c2_ref, s2_ref, p2_ref))
        for gi, (lhs_ref, cc_ref, ss_ref, out_ref) in enumerate(groups):
            lhs, cc, ss = rows(lhs_ref), rows(cc_ref), rows(ss_ref)
            for half in range(2):
                col = gi * gw + half * PBLK
                val = _rope(_dot_nt(lhs, w_ref[col:col + PBLK, :]), cc, ss, False, 1).astype(BF16)
                if out_ref.ndim == 3:
                    out_ref[:, :, half * PBLK:(half + 1) * PBLK] = val.reshape(out_ref.shape[:2] + (PBLK,))
                else:
                    out_ref[:, half * PBLK:(half + 1) * PBLK] = val
        pb_ref[...] = _rope(_dot_nt(h_ref[...], w_ref[3 * gw:, :]), c_ref[...], s_ref[...], True, 1).astype(BF16)

    d1, d2 = DIL_GROUPS[1][1], DIL_GROUPS[2][1]
    row = lambda w: pl.BlockSpec((tm, w), lambda i: (i, 0))
    res = lambda d, w: pl.BlockSpec((d, tm // d, w), lambda i: (0, i, 0))
    sds = jax.ShapeDtypeStruct
    return pl.pallas_call(
        _with_dep(body, 10, dep), name="inproj", grid=(t // tm,),
        in_specs=[row(D_MODEL), res(d1, D_MODEL), res(d2, D_MODEL),
                  pl.BlockSpec((P_WIDTH, D_MODEL), lambda i: (0, 0), pipeline_mode=pl.Buffered(1)),
                  row(128), row(128), res(d1, 128), res(d1, 128), res(d2, 128), res(d2, 128)] + _dep_spec(dep),
        out_specs=[row(gw), res(d1, gw), res(d2, gw), row(PBLK)],
        out_shape=[sds((t, gw), BF16), sds((d1, t // d1, gw), BF16), sds((d2, t // d2, gw), BF16),
                   sds((t, PBLK), BF16)],
        compiler_params=_params(("arbitrary",)),
    )(h, h1, h2, w_p_t, cos, sin, cos1, sin1, cos2, sin2, *_dep_arg(dep))


def _gates(h, w_g_t, b, tm, tn):
    t = h.shape[0]

    def body(h_ref, w_ref, b_ref, o_ref):
        z = _dot_nt(h_ref[...], w_ref[...]) + b_ref[...]
        o_ref[...] = (0.5 * jnp.tanh(0.5 * z) + 0.5).astype(BF16)

    return pl.pallas_call(
        body, name="gates", grid=(t // tm, GATE_WIDTH // tn),
        in_specs=[pl.BlockSpec((tm, D_MODEL), lambda i, j: (i, 0)),
                  pl.BlockSpec((tn, D_MODEL), lambda i, j: (j, 0)),
                  pl.BlockSpec((1, tn), lambda i, j: (0, j))],
        out_specs=pl.BlockSpec((tm, tn), lambda i, j: (i, j)),
        out_shape=jax.ShapeDtypeStruct((t, GATE_WIDTH), BF16),
        compiler_params=_params(("arbitrary", "arbitrary")),
    )(h, w_g_t, b)


def _band_mask(i, s):
    row = lax.broadcasted_iota(jnp.int32, (BAND, 2 * BAND), 0)
    col = lax.broadcasted_iota(jnp.int32, (BAND, 2 * BAND), 1)
    band = (col >= row) & (col <= row + BAND)
    if s == 0:
        band = band & ((col >= BAND) | (i > 0))
    return band


def _head_a_masks(rows):
    lane = lax.broadcasted_iota(jnp.int32, (rows, 128), 1)
    return (lane % HEAD_DIM) < HEAD_DIM // 2, lane < HEAD_DIM


def _stack_heads(x, head_a):
    zero = jnp.zeros_like(x)
    return jnp.concatenate([jnp.where(head_a, x, zero), jnp.where(head_a, zero, x)], axis=0)


def _stack_heads_t(xt, head_a_t):
    zero = jnp.zeros_like(xt)
    return jnp.concatenate([jnp.where(head_a_t, xt, zero), jnp.where(head_a_t, zero, xt)], axis=1)


def _kv_rows(cur_ref, tail_ref, s, off):
    if s == 0:
        return jnp.concatenate([tail_ref[:, off:off + 128], cur_ref[0:BAND, off:off + 128]], axis=0)
    return cur_ref[(s - 1) * BAND:(s + 1) * BAND, off:off + 128]


def _attn_layout(swa):
    if swa:
        return [(128 * j, 512, 640) for j in range(4)]
    return [(0, 128, 256), (384, 512, 640)]


def _attn_fwd(name, pv, swa, sinks, tq):
    d, ls = pv.shape[0], pv.shape[1]
    n, nsb = ls // tq, tq // BAND
    pairs = _attn_layout(swa)
    ncol = 1 if swa else 2
    ow = 128 * len(pairs)

    def body(cur_ref, tail_ref, *rest):
        sink_ref, o_ref, lse_ref, o32_ref = rest if swa else (None,) + rest + (None,)
        i = pl.program_id(2)
        lane = lax.broadcasted_iota(jnp.int32, (BAND, 128), 1)
        qk_a, v_a = _head_a_masks(BAND)
        first = lax.broadcasted_iota(jnp.int32, (2 * BAND, 1), 0) < BAND
        for s in range(nsb):
            mask = _band_mask(i, s)
            mask2 = jnp.concatenate([mask, mask], axis=0)
            rows = slice(s * BAND, (s + 1) * BAND)
            lse_tile = jnp.zeros((BAND, 128), F32)
            for j, (qo, ko, vo) in enumerate(pairs):
                q = cur_ref[rows, qo:qo + 128]
                kk = _kv_rows(cur_ref, tail_ref, s, ko)
                vv = _kv_rows(cur_ref, tail_ref, s, vo)
                sc = _dot_nt(_stack_heads(q, qk_a), kk)
                sc = jnp.where(mask2, sc, -jnp.inf)
                m = jnp.max(sc, axis=-1, keepdims=True)
                if swa:
                    sk = jnp.where(first, sink_ref[2 * j], sink_ref[2 * j + 1])
                    m = jnp.maximum(m, sk)
                p = jnp.exp(sc - m)
                den = jnp.sum(p, axis=-1, keepdims=True)
                if swa:
                    den = den + jnp.exp(sk - m)
                lse = m + jnp.log(den)
                lse_tile = jnp.where(lane == 2 * j, lse[:BAND], jnp.where(lane == 2 * j + 1, lse[BAND:], lse_tile))
                o2 = _dot(p.astype(BF16), vv) * (1.0 / den)
                o = jnp.where(v_a, o2[:BAND], o2[BAND:])
                o_ref[rows, j * 128:(j + 1) * 128] = o.astype(BF16)
                if swa:
                    o32_ref[rows, j * 128:(j + 1) * 128] = o
            lse_ref[rows, :] = lse_tile

    in_specs = [pl.BlockSpec((None, tq, PBLK), lambda r, cb, i: (r, i, cb)),
                pl.BlockSpec((None, BAND, PBLK), lambda r, cb, i: (r, jnp.maximum(i * nsb - 1, 0), cb))]
    args = [pv, pv]
    out_specs = [pl.BlockSpec((None, tq, ow), lambda r, cb, i: (r, i, cb)),
                 pl.BlockSpec((None, tq, 128), lambda r, cb, i: (r, i, cb))]
    out_shape = [jax.ShapeDtypeStruct((d, ls, 512), BF16), jax.ShapeDtypeStruct((d, ls, 128 * ncol), F32)]
    if swa:
        in_specs.append(pl.BlockSpec(memory_space=pltpu.SMEM))
        args.append(sinks)
        out_specs.append(out_specs[0])
        out_shape.append(jax.ShapeDtypeStruct((d, ls, 512), F32))
    return pl.pallas_call(
        body, name=name, grid=(d, ncol, n),
        in_specs=in_specs, out_specs=out_specs, out_shape=out_shape,
        compiler_params=_params(("arbitrary", "arbitrary", "arbitrary")),
    )(*args)


def _lse_lane(head):
    return (head // 4) * 128 + head % 4


def _dil_head_spread():
    lane = lax.broadcasted_iota(jnp.int32, (256, 512), 0)
    head = lax.broadcasted_iota(jnp.int32, (256, 512), 1) // HEAD_DIM
    return (lane == _lse_lane(head)).astype(BF16)


def _head_scale(x, tile, spread):
    return x * _dot(tile.astype(BF16), spread)


def _head_gather(width, dil):
    head = lax.broadcasted_iota(jnp.int32, (8 * HEAD_DIM, width), 0) // HEAD_DIM
    lane = lax.broadcasted_iota(jnp.int32, (8 * HEAD_DIM, width), 1)
    return (lane == (_lse_lane(head) if dil else head)).astype(BF16)


def _head_sums(x, gather):
    hi = x.astype(BF16)
    lo = (x - hi.astype(F32)).astype(BF16)
    return _dot(hi, gather) + _dot(lo, gather)


def _alphas(l0, l1, l2):
    m = jnp.maximum(jnp.maximum(l0, l1), l2)
    e0, e1, e2 = jnp.exp(l0 - m), jnp.exp(l1 - m), jnp.exp(l2 - m)
    den = e0 + e1 + e2
    return e0 / den, e1 / den, e2 / den


def _mix(o0, o1, o2, l0, l1, l2, ob, gts, x, w_a, w_b, w_out, g_cross, tm):
    t = x.shape[0]

    def body(o0_ref, o1_ref, o2_ref, l0_ref, l1_ref, l2_ref, ob_ref, g_ref, x_ref, wa_ref, wb_ref, wo_ref,
             gc_ref, oa_ref, ya_ref, yb_ref, mg_ref, x1_ref, hc_ref, so_ref, sl_ref):
        a0, a1, a2 = _alphas(l0_ref[...], _interleave(l1_ref, sl_ref), _interleave(l2_ref, sl_ref))
        spread = _dil_head_spread()
        oa = (_head_scale(o0_ref[...].astype(F32), a0, spread)
              + _head_scale(_interleave(o1_ref, so_ref), a1, spread)
              + _head_scale(_interleave(o2_ref, so_ref), a2, spread))
        oab = oa.astype(BF16)
        oa_ref[...] = oab
        ya = _dot(oab, wa_ref[...])
        yb = _dot(ob_ref[...], wb_ref[...])
        ya_ref[...] = ya.astype(BF16)
        yb_ref[...] = yb.astype(BF16)
        merged = (g_ref[:, :D_MODEL].astype(F32) * ya + g_ref[:, D_MODEL:].astype(F32) * yb).astype(BF16)
        mg_ref[...] = merged
        x1 = x_ref[...] + _dot(merged, wo_ref[...])
        x1_ref[...] = x1
        hc_ref[...] = (x1 * _rms(x1) * gc_ref[...]).astype(BF16)

    row = lambda w: pl.BlockSpec((tm, w), lambda i: (i, 0))
    full = lambda a, b: pl.BlockSpec((a, b), lambda i: (0, 0))
    return pl.pallas_call(
        body, name="mix", grid=(t // tm,),
        in_specs=[row(512), _res_spec(o1, tm), _res_spec(o2, tm), row(256), _res_spec(l1, tm), _res_spec(l2, tm),
                  row(512), row(GATE_WIDTH),
                  row(D_MODEL), full(512, D_MODEL), full(512, D_MODEL), full(D_MODEL, D_MODEL), full(1, D_MODEL)],
        out_specs=[row(512), row(D_MODEL), row(D_MODEL), row(D_MODEL), row(D_MODEL), row(D_MODEL)],
        out_shape=[jax.ShapeDtypeStruct((t, 512), BF16), jax.ShapeDtypeStruct((t, D_MODEL), BF16),
                   jax.ShapeDtypeStruct((t, D_MODEL), BF16), jax.ShapeDtypeStruct((t, D_MODEL), BF16),
                   jax.ShapeDtypeStruct((t, D_MODEL), F32), jax.ShapeDtypeStruct((t, D_MODEL), BF16)],
        scratch_shapes=[_lane_scratch(tm, 512), _lane_scratch(tm, 256)],
        compiler_params=_params(("arbitrary",)),
    )(o0, o1, o2, l0, l1, l2, ob, gts, x, w_a, w_b, w_out, g_cross)


def _memkv(mem, g_mem, w_ckv):
    m = mem.shape[0]
    ws = w_ckv.shape[2]

    def body(mem_ref, g_ref, w_ref, mn_ref, kv_ref):
        xt = mem_ref[...]
        mn = (xt * _rms(xt) * g_ref[...]).astype(BF16)
        mn_ref[...] = mn
        for j in range(N_DEV):
            kv_ref[:, j * ws:(j + 1) * ws] = _dot(mn, w_ref[j]).astype(BF16)

    return pl.pallas_call(
        body, name="memkv",
        out_shape=[jax.ShapeDtypeStruct((m, D_MODEL), BF16), jax.ShapeDtypeStruct((m, 2 * D_MODEL), BF16)],
        compiler_params=_params(),
    )(mem, g_mem, w_ckv)


def _cross_probs(q, kv_ref, h):
    k = kv_ref[:, h * X_HEAD_DIM:(h + 1) * X_HEAD_DIM]
    sc = _dot_nt(q[:, h * X_HEAD_DIM:(h + 1) * X_HEAD_DIM], k)
    m = jnp.max(sc, axis=-1, keepdims=True)
    p = jnp.exp(sc - m)
    return p / jnp.sum(p, axis=-1, keepdims=True)


def _cross(hc, x1, kv, w_cq, w_co, g_mlp, tm):
    t = x1.shape[0]
    m = kv.shape[0]

    def body(hc_ref, x1_ref, kv_ref, wq_ref, wo_ref, g_ref, q_ref, o_ref, x2_ref, hm_ref):
        q = (_dot(hc_ref[...], wq_ref[...]) * X_SCALE).astype(BF16)
        q_ref[...] = q
        outs = []
        for h in range(X_HEADS):
            p = _cross_probs(q, kv_ref, h)
            v = kv_ref[:, D_MODEL + h * X_HEAD_DIM:D_MODEL + (h + 1) * X_HEAD_DIM]
            outs.append(_dot(p.astype(BF16), v))
        o = jnp.concatenate(outs, axis=1).astype(BF16)
        o_ref[...] = o
        x2 = x1_ref[...] + _dot(o, wo_ref[...])
        x2_ref[...] = x2
        hm_ref[...] = (x2 * _rms(x2) * g_ref[...]).astype(BF16)

    row = lambda w: pl.BlockSpec((tm, w), lambda i: (i, 0))
    full = lambda a, b: pl.BlockSpec((a, b), lambda i: (0, 0))
    return pl.pallas_call(
        body, name="cross", grid=(t // tm,),
        in_specs=[row(D_MODEL), row(D_MODEL), full(m, 2 * D_MODEL), full(D_MODEL, D_MODEL),
                  full(D_MODEL, D_MODEL), full(1, D_MODEL)],
        out_specs=[row(D_MODEL)] * 4,
        out_shape=[jax.ShapeDtypeStruct((t, D_MODEL), BF16), jax.ShapeDtypeStruct((t, D_MODEL), BF16),
                   jax.ShapeDtypeStruct((t, D_MODEL), F32), jax.ShapeDtypeStruct((t, D_MODEL), BF16)],
        compiler_params=_params(("arbitrary",)),
    )(hc, x1, kv, w_cq, w_co, g_mlp)


def _mlp(hm, x2, w_1, w_2, g_final, target, tm, tf):
    t = x2.shape[0]
    nf = D_FF // tf

    def body(hm_ref, x2_ref, w1_ref, w2_ref, g_ref, tg_ref, a_ref, dx3_ref, loss_ref, dg_ref, acc_ref):
        i, f = pl.program_id(0), pl.program_id(1)
        hm_t = hm_ref[...]
        sw = w1_ref.shape[2]
        part = None
        for s in range(w1_ref.shape[0]):
            a = jnp.maximum(_dot(hm_t, w1_ref[s]), 0.0).astype(BF16)
            a_ref[:, s * sw:(s + 1) * sw] = a
            p_s = _dot(a * a, w2_ref[s * sw:(s + 1) * sw, :])
            part = p_s if part is None else part + p_s

        @pl.when(f == 0)
        def _():
            acc_ref[...] = part

        @pl.when(f > 0)
        def _():
            acc_ref[...] += part

        @pl.when((i == 0) & (f == 0))
        def _():
            loss_ref[...] = jnp.zeros_like(loss_ref)
            dg_ref[...] = jnp.zeros_like(dg_ref)

        @pl.when(f == nf - 1)
        def _():
            x3 = x2_ref[...] + acc_ref[...]
            r = _rms(x3)
            g = g_ref[...]
            diff = x3 * r * g - tg_ref[...]
            loss_ref[...] += 0.5 * jnp.sum(jnp.mean(diff * diff, axis=-1, keepdims=True))
            dx3, dg = _rms_bwd(diff / D_MODEL, x3, r, g)
            dx3_ref[...] = dx3
            dg_ref[...] += dg

    return pl.pallas_call(
        body, name="mlp", grid=(t // tm, nf),
        in_specs=[pl.BlockSpec((tm, D_MODEL), lambda i, f: (i, 0)),
                  pl.BlockSpec((tm, D_MODEL), lambda i, f: (i, 0)),
                  pl.BlockSpec((tf // w_1.shape[2], D_MODEL, w_1.shape[2]), lambda i, f: (f, 0, 0)),
                  pl.BlockSpec((tf, D_MODEL), lambda i, f: (f, 0)),
                  pl.BlockSpec((1, D_MODEL), lambda i, f: (0, 0)),
                  pl.BlockSpec((tm, D_MODEL), lambda i, f: (i, 0))],
        out_specs=[pl.BlockSpec((tm, tf), lambda i, f: (i, f)),
                   pl.BlockSpec((tm, D_MODEL), lambda i, f: (i, 0)),
                   pl.BlockSpec((1, 128), lambda i, f: (0, 0)),
                   pl.BlockSpec((1, D_MODEL), lambda i, f: (0, 0))],
        out_shape=[jax.ShapeDtypeStruct((t, D_FF), BF16), jax.ShapeDtypeStruct((t, D_MODEL), F32),
                   jax.ShapeDtypeStruct((1, 128), F32), jax.ShapeDtypeStruct((1, D_MODEL), F32)],
        scratch_shapes=[pltpu.VMEM((tm, D_MODEL), F32)],
        compiler_params=_params(("arbitrary", "arbitrary")),
    )(hm, x2, w_1, w_2, g_final, target)


def _mlp_bwd(dx3, a, w_1, w_2, x2, g_mlp, tm, tf):
    t = x2.shape[0]
    nf = D_FF // tf

    def body(dx3_ref, a_ref, w1_ref, w2_ref, x2_ref, g_ref, dz_ref, dx2_ref, dg_ref, acc_ref):
        i, f = pl.program_id(0), pl.program_id(1)
        dx3_b = dx3_ref[...].astype(BF16)
        sw = w1_ref.shape[2]
        part = None
        for s in range(w1_ref.shape[0]):
            cols = slice(s * sw, (s + 1) * sw)
            da2 = _dot_nt(dx3_b, w2_ref[cols, :])
            dz = (2.0 * a_ref[:, cols].astype(F32) * da2).astype(BF16)
            dz_ref[:, cols] = dz
            p_s = _dot_nt(dz, w1_ref[s])
            part = p_s if part is None else part + p_s

        @pl.when(f == 0)
        def _():
            acc_ref[...] = part

        @pl.when(f > 0)
        def _():
            acc_ref[...] += part

        @pl.when((i == 0) & (f == 0))
        def _():
            dg_ref[...] = jnp.zeros_like(dg_ref)

        @pl.when(f == nf - 1)
        def _():
            xt = x2_ref[...]
            dx, dg = _rms_bwd(acc_ref[...], xt, _rms(xt), g_ref[...])
            dx2_ref[...] = dx3_ref[...] + dx
            dg_ref[...] += dg

    return pl.pallas_call(
        body, name="mlp_bwd", grid=(t // tm, nf),
        in_specs=[pl.BlockSpec((tm, D_MODEL), lambda i, f: (i, 0)),
                  pl.BlockSpec((tm, tf), lambda i, f: (i, f)),
                  pl.BlockSpec((tf // w_1.shape[2], D_MODEL, w_1.shape[2]), lambda i, f: (f, 0, 0)),
                  pl.BlockSpec((tf, D_MODEL), lambda i, f: (f, 0)),
                  pl.BlockSpec((tm, D_MODEL), lambda i, f: (i, 0)),
                  pl.BlockSpec((1, D_MODEL), lambda i, f: (0, 0))],
        out_specs=[pl.BlockSpec((tm, tf), lambda i, f: (i, f)),
                   pl.BlockSpec((tm, D_MODEL), lambda i, f: (i, 0)),
                   pl.BlockSpec((1, D_MODEL), lambda i, f: (0, 0))],
        out_shape=[jax.ShapeDtypeStruct((t, D_FF), BF16), jax.ShapeDtypeStruct((t, D_MODEL), F32),
                   jax.ShapeDtypeStruct((1, D_MODEL), F32)],
        scratch_shapes=[pltpu.VMEM((tm, D_MODEL), F32)],
        compiler_params=_params(("arbitrary", "arbitrary")),
    )(dx3, a, w_1, w_2, x2, g_mlp)


def _wgrad(name, a, b, tka, tn, tm, square=False, col_shards=False):
    t, ka = a.shape
    n = b.shape[1]
    nk = t // tm

    def body(a_ref, b_ref, o_ref, acc_ref):
        at = a_ref[...].astype(BF16)
        if square:
            at = at * at
        part = _dot_tn(at, b_ref[...].astype(BF16))
        k = pl.program_id(2)

        @pl.when(k == 0)
        def _():
            acc_ref[...] = part

        @pl.when(k > 0)
        def _():
            acc_ref[...] += part

        @pl.when(k == nk - 1)
        def _():
            if col_shards:
                for s in range(tn // sw):
                    o_ref[s] = acc_ref[:, s * sw:(s + 1) * sw].astype(BF16)
            else:
                o_ref[...] = acc_ref[...].astype(BF16)

    if col_shards:
        sw = n // N_DEV
        out_spec = pl.BlockSpec((tn // sw, tka, sw), lambda p, q, k: (q, p, 0))
        out_shape = jax.ShapeDtypeStruct((N_DEV, ka, sw), BF16)
    else:
        out_spec = pl.BlockSpec((tka, tn), lambda p, q, k: (p, q))
        out_shape = jax.ShapeDtypeStruct((ka, n), BF16)
    return pl.pallas_call(
        body, name=name, grid=(ka // tka, n // tn, nk),
        in_specs=[pl.BlockSpec((tm, tka), lambda p, q, k: (k, p)),
                  pl.BlockSpec((tm, tn), lambda p, q, k: (k, q))],
        out_specs=out_spec, out_shape=out_shape,
        scratch_shapes=[pltpu.VMEM((tka, tn), F32)],
        compiler_params=_params(("arbitrary", "arbitrary", "arbitrary")),
    )(a, b)


def _cross_bwd(dx2, x1, q, kv, w_cq, w_co, g_cross, tm, dep=None):
    t = x1.shape[0]
    m = kv.shape[0]

    def body(dx2_ref, x1_ref, q_ref, kv_ref, wq_ref, wo_ref, g_ref, dq_ref, dx1_ref, dkv_ref, dg_ref):
        @pl.when(pl.program_id(0) == 0)
        def _():
            dkv_ref[...] = jnp.zeros_like(dkv_ref)
            dg_ref[...] = jnp.zeros_like(dg_ref)

        do = _dot_nt(dx2_ref[...].astype(BF16), wo_ref[...]).astype(BF16)
        q = q_ref[...]
        dqs = []
        for h in range(X_HEADS):
            hs = slice(h * X_HEAD_DIM, (h + 1) * X_HEAD_DIM)
            vs = slice(D_MODEL + h * X_HEAD_DIM, D_MODEL + (h + 1) * X_HEAD_DIM)
            p = _cross_probs(q, kv_ref, h)
            dp = _dot_nt(do[:, hs], kv_ref[:, vs])
            ds = (p * (dp - jnp.sum(dp * p, axis=-1, keepdims=True))).astype(BF16)
            dqs.append(_dot(ds, kv_ref[:, hs]))
            dkv_ref[:, hs] += _dot_tn(ds, q[:, hs])
            dkv_ref[:, vs] += _dot_tn(p.astype(BF16), do[:, hs])
        dq = (jnp.concatenate(dqs, axis=1) * X_SCALE).astype(BF16)
        dq_ref[...] = dq
        xt = x1_ref[...]
        dx, dg = _rms_bwd(_dot_nt(dq, wq_ref[...]), xt, _rms(xt), g_ref[...])
        dx1_ref[...] = dx2_ref[...] + dx
        dg_ref[...] += dg

    row = lambda w: pl.BlockSpec((tm, w), lambda i: (i, 0))
    full = lambda a, b: pl.BlockSpec((a, b), lambda i: (0, 0))
    return pl.pallas_call(
        _with_dep(body, 7, dep), name="cross_bwd", grid=(t // tm,),
        in_specs=[row(D_MODEL), row(D_MODEL), row(D_MODEL), full(m, 2 * D_MODEL), full(D_MODEL, D_MODEL),
                  full(D_MODEL, D_MODEL), full(1, D_MODEL)] + _dep_spec(dep),
        out_specs=[row(D_MODEL), row(D_MODEL), full(m, 2 * D_MODEL), full(1, D_MODEL)],
        out_shape=[jax.ShapeDtypeStruct((t, D_MODEL), BF16), jax.ShapeDtypeStruct((t, D_MODEL), F32),
                   jax.ShapeDtypeStruct((m, 2 * D_MODEL), F32), jax.ShapeDtypeStruct((1, D_MODEL), F32)],
        compiler_params=_params(("arbitrary",)),
    )(dx2, x1, q, kv, w_cq, w_co, g_cross, *_dep_arg(dep))


def _memkv_bwd(dkv, mn, mem, w_ckv, g_mem):
    ws = w_ckv.shape[2]

    def body(dkv_ref, mn_ref, mem_ref, w_ref, g_ref, dw_ref, dg_ref):
        mn = mn_ref[...]
        dmn = jnp.zeros(mn.shape, F32)
        for j in range(N_DEV):
            dkvb = dkv_ref[:, j * ws:(j + 1) * ws].astype(BF16)
            dw_ref[j] = _dot_tn(mn, dkvb).astype(BF16)
            dmn = dmn + _dot_nt(dkvb, w_ref[j])
        xt = mem_ref[...]
        dg_ref[...] = jnp.sum(dmn * xt * _rms(xt), axis=0, keepdims=True)

    return pl.pallas_call(
        body, name="memkv_bwd",
        out_shape=[jax.ShapeDtypeStruct(w_ckv.shape, BF16), jax.ShapeDtypeStruct((1, D_MODEL), F32)],
        compiler_params=_params(),
    )(dkv, mn, mem, w_ckv, g_mem)


def _merge_bwd(dx1, ya, yb, gts, w_out, w_g_t, tm):
    t = dx1.shape[0]

    def body(dx1_ref, ya_ref, yb_ref, g_ref, wo_ref, wg_ref, dg_ref, dhp_ref, dya_ref, dyb_ref, db_ref):
        @pl.when(pl.program_id(0) == 0)
        def _():
            db_ref[...] = jnp.zeros_like(db_ref)

        dm = _dot_nt(dx1_ref[...].astype(BF16), wo_ref[...])
        ga = g_ref[:, :D_MODEL].astype(F32)
        gb = g_ref[:, D_MODEL:].astype(F32)
        dya_ref[...] = (dm * ga).astype(BF16)
        dyb_ref[...] = (dm * gb).astype(BF16)
        dpa = dm * ya_ref[...].astype(F32) * ga * (1.0 - ga)
        dpb = dm * yb_ref[...].astype(F32) * gb * (1.0 - gb)
        dpre = jnp.concatenate([dpa, dpb], axis=1)
        db_ref[...] += jnp.sum(dpre, axis=0, keepdims=True)
        dpreb = dpre.astype(BF16)
        dg_ref[...] = dpreb
        dhp_ref[...] = _dot(dpreb, wg_ref[...])

    row = lambda w: pl.BlockSpec((tm, w), lambda i: (i, 0))
    once = lambda a, b: pl.BlockSpec((a, b), lambda i: (0, 0), pipeline_mode=pl.Buffered(1))
    sds = jax.ShapeDtypeStruct
    return pl.pallas_call(
        body, name="merge_bwd", grid=(t // tm,),
        in_specs=[row(D_MODEL), row(D_MODEL), row(D_MODEL), row(GATE_WIDTH),
                  once(D_MODEL, D_MODEL), once(GATE_WIDTH, D_MODEL)],
        out_specs=[row(GATE_WIDTH), row(D_MODEL), row(D_MODEL), row(D_MODEL),
                   pl.BlockSpec((1, GATE_WIDTH), lambda i: (0, 0))],
        out_shape=[sds((t, GATE_WIDTH), BF16), sds((t, D_MODEL), F32), sds((t, D_MODEL), BF16),
                   sds((t, D_MODEL), BF16), sds((1, GATE_WIDTH), F32)],
        compiler_params=_params(("arbitrary",)),
    )(dx1, ya, yb, gts, w_out, w_g_t)


def _combine_bwd(dya, dyb, oa, ob, l0, l1, l2, lb, sink_row, w_a, w_b, tm):
    t = dya.shape[0]

    def body(dya_ref, dyb_ref, oa_ref, ob_ref, l0_ref, l1_ref, l2_ref, lb_ref, sk_ref, wa_ref, wb_ref,
             do0_ref, do1_ref, do2_ref, c0_ref, c1_ref, c2_ref, dob_ref, cb_ref, dsk_ref, so_ref, sl_ref):
        @pl.when(pl.program_id(0) == 0)
        def _():
            dsk_ref[...] = jnp.zeros_like(dsk_ref)

        doa = _dot_nt(dya_ref[...], wa_ref[...])
        dob = _dot_nt(dyb_ref[...], wb_ref[...])
        dsum = _head_sums(doa * oa_ref[...].astype(F32), _head_gather(256, True))
        a0, a1, a2 = _alphas(l0_ref[...], _interleave(l1_ref, sl_ref), _interleave(l2_ref, sl_ref))
        c0_ref[...] = a0 * dsum
        spread = _dil_head_spread()
        do0_ref[...] = _head_scale(doa, a0, spread).astype(BF16)
        for al, do_ref, c_ref in ((a1, do1_ref, c1_ref), (a2, do2_ref, c2_ref)):
            _deinterleave(al * dsum, sl_ref, c_ref, F32)
            _deinterleave(_head_scale(doa, al, spread), so_ref, do_ref, BF16)
        dob_ref[...] = dob.astype(BF16)
        cb = _head_sums(dob * ob_ref[...], _head_gather(128, False))
        cb_ref[...] = cb
        lane = lax.broadcasted_iota(jnp.int32, cb.shape, 1)
        psink = jnp.where(lane < 8, jnp.exp(sk_ref[...] - lb_ref[...]), 0.0)
        dsk_ref[...] += jnp.sum(-psink * cb, axis=0, keepdims=True)

    row = lambda w: pl.BlockSpec((tm, w), lambda i: (i, 0))
    full = lambda a, b: pl.BlockSpec((a, b), lambda i: (0, 0))
    sds = jax.ShapeDtypeStruct
    d1, d2 = l1.shape[0], l2.shape[0]
    res = lambda d, w: pl.BlockSpec((d, tm // d, w), lambda i: (0, i, 0))
    return pl.pallas_call(
        body, name="combine_bwd", grid=(t // tm,),
        in_specs=[row(D_MODEL), row(D_MODEL), row(512), row(512),
                  row(256), _res_spec(l1, tm), _res_spec(l2, tm), row(128), full(1, 128),
                  full(512, D_MODEL), full(512, D_MODEL)],
        out_specs=[row(512), res(d1, 512), res(d2, 512), row(256), res(d1, 256), res(d2, 256),
                   row(512), row(128), full(1, 128)],
        out_shape=[sds((t, 512), BF16), sds((d1, t // d1, 512), BF16),
                   sds((d2, t // d2, 512), BF16), sds((t, 256), F32), sds((d1, t // d1, 256), F32),
                   sds((d2, t // d2, 256), F32), sds((t, 512), BF16),
                   sds((t, 128), F32), sds((1, 128), F32)],
        scratch_shapes=[_lane_scratch(tm, 512), _lane_scratch(tm, 256)],
        compiler_params=_params(("arbitrary",)),
    )(dya, dyb, oa, ob, l0, l1, l2, lb, sink_row, w_a, w_b)


def _attn_bwd(name, pv, dov, lsev, cv, cosv, sinv, swa, tq, dep=None):
    d, ls = pv.shape[0], pv.shape[1]
    n, nsb = ls // tq, tq // BAND
    pairs = _attn_layout(swa)
    ncol = 1 if swa else 2
    ow = 128 * len(pairs)

    kv_slots = sorted({(ko, vo) for _, ko, vo in pairs})

    def body(cur_ref, tail_ref, do_ref, lse_ref, c_ref, cos_ref, sin_ref, out_ref, acc_ref, carry_ref, acct_ref):
        i = pl.program_id(2)
        blk_i = n - 1 - i
        acc_ref[...] = jnp.zeros_like(acc_ref)
        acct_ref[...] = jnp.zeros_like(acct_ref)

        @pl.when(i == 0)
        def _():
            carry_ref[...] = jnp.zeros_like(carry_ref)

        qk_a, v_a = _head_a_masks(BAND)
        dim = lax.broadcasted_iota(jnp.int32, (128, BAND), 0)
        qk_at, v_at = (dim % HEAD_DIM) < HEAD_DIM // 2, dim < HEAD_DIM
        for s in range(nsb):
            mask = _band_mask(blk_i, s)
            mask2 = jnp.concatenate([mask, mask], axis=0)
            rows = slice(s * BAND, (s + 1) * BAND)
            kcols = slice(s * BAND, (s + 2) * BAND)
            for j, (qo, ko, vo) in enumerate(pairs):
                slot = kv_slots.index((ko, vo))
                kk = _kv_rows(cur_ref, tail_ref, s, ko)
                vv = _kv_rows(cur_ref, tail_ref, s, vo)
                q, do = cur_ref[rows, qo:qo + 128], do_ref[rows, j * 128:(j + 1) * 128]
                q2, do2 = _stack_heads(q, qk_a), _stack_heads(do, v_a)
                col2 = lambda ref: jnp.concatenate([ref[rows, 2 * j:2 * j + 1], ref[rows, 2 * j + 1:2 * j + 2]], axis=0)
                sc = _dot_nt(q2, kk)
                p = jnp.exp(jnp.where(mask2, sc, -jnp.inf) - col2(lse_ref))
                dp = _dot_nt(do2, vv)
                ds = (p * (dp - col2(c_ref))).astype(BF16)
                dq2 = _dot(ds, kk)
                acc_ref[BAND + s * BAND:BAND + (s + 1) * BAND, qo:qo + 128] += jnp.where(qk_a, dq2[:BAND], dq2[BAND:])
                acct_ref[2 * slot, :, kcols] += _dot(_stack_heads_t(q.T, qk_at), ds)
                acct_ref[2 * slot + 1, :, kcols] += _dot(_stack_heads_t(do.T, v_at), p.astype(BF16))
        for slot, (ko, vo) in enumerate(kv_slots):
            acc_ref[:, ko:ko + 128] += acct_ref[2 * slot].T
            acc_ref[:, vo:vo + 128] += acct_ref[2 * slot + 1].T

        last = acc_ref[tq:, :] + carry_ref[...]
        fin = last if tq == BAND else jnp.concatenate([acc_ref[BAND:tq, :], last], axis=0)
        out_ref[...] = _rope(fin, cos_ref[...], sin_ref[...], swa, -1).astype(BF16)
        carry_ref[...] = acc_ref[0:BAND, :]

    rev = lambda i: n - 1 - i
    blk = lambda rows, w, row_of: pl.BlockSpec((None, rows, w), lambda r, cb, i: (r, row_of(i), cb))
    tab = pl.BlockSpec((None, tq, 128), lambda r, cb, i: (r, rev(i), 0))
    return pl.pallas_call(
        _with_dep(body, 7, dep), name=name, grid=(d, ncol, n),
        in_specs=[blk(tq, PBLK, rev), blk(BAND, PBLK, lambda i: jnp.maximum(rev(i) * nsb - 1, 0)),
                  blk(tq, ow, rev), blk(tq, 128, rev), blk(tq, 128, rev), tab, tab] + _dep_spec(dep),
        out_specs=blk(tq, PBLK, rev),
        out_shape=jax.ShapeDtypeStruct((d, ls, ncol * PBLK), BF16),
        scratch_shapes=[pltpu.VMEM((tq + BAND, PBLK), F32), pltpu.VMEM((BAND, PBLK), F32),
                        pltpu.VMEM((2 * len(kv_slots), 128, tq + BAND), F32)],
        compiler_params=_params(("arbitrary", "arbitrary", "arbitrary")),
    )(pv, pv, dov, lsev, cv, cosv, sinv, *_dep_arg(dep))


def _dx(dp0, dp1, dp2, dpb, w_p_t, dh_part, dx1, x, g_mix, tm, dep=None):
    t = x.shape[0]
    gw = 2 * PBLK

    def body(dp0_ref, dp1_ref, dp2_ref, dpb_ref, w_ref, dhp_ref, dx1_ref, x_ref, g_ref, gx_ref, dg_ref,
             dpt_ref, scr_ref):
        @pl.when(pl.program_id(0) == 0)
        def _():
            dg_ref[...] = jnp.zeros_like(dg_ref)

        dpt_ref[:, 0:gw] = dp0_ref[...]
        dpt_ref[:, gw:2 * gw] = _interleave(dp1_ref, scr_ref).astype(BF16)
        dpt_ref[:, 2 * gw:3 * gw] = _interleave(dp2_ref, scr_ref).astype(BF16)
        dpt_ref[:, 3 * gw:] = dpb_ref[...]
        dh = _dot(dpt_ref[...], w_ref[...]) + dhp_ref[...]
        xt = x_ref[...]
        dx, dg = _rms_bwd(dh, xt, _rms(xt), g_ref[...])
        gx_ref[...] = dx1_ref[...] + dx
        dg_ref[...] += dg

    row = lambda w: pl.BlockSpec((tm, w), lambda i: (i, 0))
    full = lambda a, b: pl.BlockSpec((a, b), lambda i: (0, 0))
    return pl.pallas_call(
        _with_dep(body, 9, dep), name="dx", grid=(t // tm,),
        in_specs=[row(gw), _res_spec(dp1, tm), _res_spec(dp2, tm), row(PBLK),
                  pl.BlockSpec((P_WIDTH, D_MODEL), lambda i: (0, 0), pipeline_mode=pl.Buffered(1)),
                  row(D_MODEL), row(D_MODEL), row(D_MODEL), full(1, D_MODEL)] + _dep_spec(dep),
        out_specs=[row(D_MODEL), full(1, D_MODEL)],
        out_shape=[jax.ShapeDtypeStruct((t, D_MODEL), F32), jax.ShapeDtypeStruct((1, D_MODEL), F32)],
        scratch_shapes=[pltpu.VMEM((tm, P_WIDTH), BF16), _lane_scratch(tm, gw)],
        compiler_params=_params(("arbitrary",)),
    )(dp0, dp1, dp2, dpb, w_p_t, dh_part, dx1, x, g_mix, *_dep_arg(dep))


MESH = pl.DeviceIdType.MESH
HBM_SPEC = pl.BlockSpec(memory_space=pltpu.HBM)
VMEM_SPEC = pl.BlockSpec(memory_space=pltpu.VMEM)


def _all_gather(xp, act, g, tm):
    t = act.shape[0]
    d1, d2 = DIL_GROUPS[1][1], DIL_GROUPS[2][1]

    def body(x_ref, act_ref, g_ref, out_ref, h_ref, h1_ref, h2_ref, send_sems, recv_sems, local_sem, hf_ref):
        x, y, c = lax.axis_index("x"), lax.axis_index("y"), lax.axis_index("c")
        me, sibling = (x, y, c), (x, y, 1 - c)
        chips = [(1 - x, y), (x, 1 - y), (1 - x, 1 - y)]

        def rows(px, py, pc):
            return out_ref.at[4 * px + 2 * py + pc]

        def copy(k, block, to, src=None):
            return pltpu.make_async_remote_copy(
                src_ref=rows(*block) if src is None else src, dst_ref=rows(*block),
                send_sem=send_sems.at[k], recv_sem=recv_sems.at[k], device_id=to, device_id_type=MESH)

        mine = pltpu.make_async_copy(x_ref, rows(*me), local_sem)
        mine.start()
        first = [copy(0, me, sibling, src=x_ref)]
        first += [copy(1 + j, me, (*chip, c), src=x_ref) for j, chip in enumerate(chips)]
        for cp in first:
            cp.start()

        def norm(a_blk, h_blk, h1_blk, h2_blk):
            xt = a_blk[...]
            hf = xt * _rms(xt) * g_ref[...]
            h_blk[...] = hf.astype(BF16)
            _deinterleave(hf, hf_ref, h1_blk, BF16)
            _deinterleave(hf, hf_ref, h2_blk, BF16)

        res = lambda d: pl.BlockSpec((d, tm // d, D_MODEL), lambda i: (0, i, 0))
        row = pl.BlockSpec((tm, D_MODEL), lambda i: (i, 0))
        pltpu.emit_pipeline(norm, grid=(t // tm,), in_specs=[row], out_specs=[row, res(d1), res(d2)])(
            act_ref, h_ref, h1_ref, h2_ref)

        passed = [copy(4 + j, (*chip, c), sibling) for j, chip in enumerate(chips)]
        for j, chip in enumerate(chips):
            copy(1 + j, (*chip, c), me).wait_recv()
            passed[j].start()
        copy(0, sibling, me).wait_recv()
        for j, chip in enumerate(chips):
            copy(4 + j, (*chip, 1 - c), me).wait_recv()
        for cp in first + passed:
            cp.wait_send()
        mine.wait()

    sds = jax.ShapeDtypeStruct
    return pl.pallas_call(
        body, name="all_gather",
        out_shape=[sds((N_DEV,) + xp.shape, xp.dtype), sds((t, D_MODEL), BF16),
                   sds((d1, t // d1, D_MODEL), BF16), sds((d2, t // d2, D_MODEL), BF16)],
        in_specs=[HBM_SPEC, HBM_SPEC, VMEM_SPEC], out_specs=[HBM_SPEC] * 4,
        scratch_shapes=[pltpu.SemaphoreType.DMA((7,)), pltpu.SemaphoreType.DMA((7,)), pltpu.SemaphoreType.DMA,
                        _lane_scratch(tm, D_MODEL)],
        compiler_params=pltpu.CompilerParams(vmem_limit_bytes=VMEM_LIMIT),
    )(xp, act, g)


def _peers():
    x, y, c = lax.axis_index("x"), lax.axis_index("y"), lax.axis_index("c")
    out = []
    for k in range(1, N_DEV):
        px = 1 - x if k & 4 else x
        py = 1 - y if k & 2 else y
        pc = 1 - c if k & 1 else c
        out.append((k, (px, py, pc), 4 * px + 2 * py + pc))
    return out


def _my_index():
    return 4 * lax.axis_index("x") + 2 * lax.axis_index("y") + lax.axis_index("c")


SEM_SPEC = pl.BlockSpec(memory_space=pltpu.SEMAPHORE)
ANY_SPEC = pl.BlockSpec(memory_space=pl.ANY)
_SPLIT_PARAMS = pltpu.CompilerParams(has_side_effects=pltpu.SideEffectType.DATAFLOW_SIDE_EFFECTING)


def _split_copies(gather, src_refs, land_refs, send_sems, recv_sems):
    me_idx = _my_index()
    out = []
    for a, (src_ref, land_ref) in enumerate(zip(src_refs, land_refs)):
        for k, peer, peer_idx in _peers():
            if gather:
                src, dst = src_ref, land_ref.at[me_idx]
            else:
                src, dst = src_ref.at[peer_idx], land_ref.at[k - 1]
            out.append(pltpu.make_async_remote_copy(
                src_ref=src, dst_ref=dst, send_sem=send_sems.at[7 * a + k - 1], recv_sem=recv_sems.at[7 * a + k - 1],
                device_id=peer, device_id_type=MESH))
    return out


def _split_start(name, gather, srcs, after=None):
    n = len(srcs)
    extra = [] if after is None else [after]
    first_out = n + len(extra)

    def body(*refs):
        send_sems, recv_sems = refs[first_out], refs[first_out + 1]
        lands = refs[first_out + 2 + n:first_out + 2 + 2 * n]
        for cp in _split_copies(gather, refs[:n], lands, send_sems, recv_sems):
            cp.start()
        token = refs[-1]
        token[...] = jnp.zeros_like(token)

    lands = [pltpu.HBM((N_DEV,) + a.shape if gather else (N_DEV - 1,) + a.shape[1:], a.dtype) for a in srcs]
    return pl.pallas_call(
        body, name=name,
        out_shape=(pltpu.SemaphoreType.DMA((7 * n,)), pltpu.SemaphoreType.DMA((7 * n,)),
                   *[pltpu.HBM(a.shape, a.dtype) for a in srcs], *lands, jax.ShapeDtypeStruct((8, 128), F32)),
        in_specs=(HBM_SPEC,) * n + (ANY_SPEC,) * len(extra),
        out_specs=(SEM_SPEC, SEM_SPEC) + (HBM_SPEC,) * (2 * n) + (VMEM_SPEC,),
        input_output_aliases={i: 2 + i for i in range(n)}, compiler_params=_SPLIT_PARAMS,
    )(*[pltpu.with_memory_space_constraint(a, pltpu.HBM) for a in srcs], *extra)


def _split_wait(name, gather, started, after):
    send_sems, recv_sems, bufs = started[0], started[1], started[2:-1]
    n = len(bufs) // 2

    def body(*refs):
        for cp in _split_copies(gather, refs[:n], refs[n:2 * n], refs[2 * n], refs[2 * n + 1]):
            cp.wait_send()
            cp.wait_recv()

    out = pl.pallas_call(
        body, name=name, out_shape=tuple(pltpu.HBM(a.shape, a.dtype) for a in bufs),
        in_specs=(HBM_SPEC,) * (2 * n) + (SEM_SPEC, SEM_SPEC, ANY_SPEC), out_specs=(HBM_SPEC,) * (2 * n),
        input_output_aliases={i: i for i in range(2 * n)}, compiler_params=_SPLIT_PARAMS,
    )(*bufs, send_sems, recv_sems, after)
    return out[:n], out[n:]


def _adam_update(g, w, m, v):
    nm = ADAM_B1 * m + (1.0 - ADAM_B1) * g
    nv = ADAM_B2 * v + (1.0 - ADAM_B2) * (g * g)
    m_hat = nm / (1.0 - ADAM_B1 ** ADAM_STEP)
    v_hat = nv / (1.0 - ADAM_B2 ** ADAM_STEP)
    return -ADAM_LR * (m_hat / (jnp.sqrt(v_hat) + ADAM_EPS) + ADAM_WD * w), nm, nv


def _adamw(name, me, sent, got, w, m, v, tr, tc=None):
    r, c = w.shape
    tc = c if tc is None else tc

    def body(me_ref, own_ref, got_ref, w_ref, m_ref, v_ref, g_ref, d_ref, nm_ref, nv_ref):
        g = own_ref[...].astype(F32)
        for k in range(N_DEV - 1):
            g = g + got_ref[k].astype(F32)
        g_ref[...] = g
        d_ref[...], nm_ref[...], nv_ref[...] = _adam_update(g, w_ref[...], m_ref[...], v_ref[...])

    blk = pl.BlockSpec((tr, tc), lambda i, j, me_ref: (i, j))
    return pl.pallas_call(
        body, name=name,
        grid_spec=pltpu.PrefetchScalarGridSpec(
            num_scalar_prefetch=1, grid=(r // tr, c // tc),
            in_specs=[pl.BlockSpec((None, tr, tc), lambda i, j, me_ref: (me_ref[0], i, j)),
                      pl.BlockSpec((N_DEV - 1, tr, tc), lambda i, j, me_ref: (0, i, j)), blk, blk, blk],
            out_specs=[blk] * 4),
        out_shape=[jax.ShapeDtypeStruct((r, c), F32)] * 4,
        compiler_params=_params(("arbitrary", "arbitrary")),
    )(me, sent, got, w, m, v)


def _adamw_small(srecv, ws, ms, vs):
    nv_ = len(ws)

    def body(*refs):
        s_ref = refs[0]
        ins, outs = refs[1:1 + 3 * nv_], refs[1 + 3 * nv_:]
        g_all = s_ref[0]
        for k in range(1, N_DEV):
            g_all = g_all + s_ref[k]
        for i in range(nv_):
            n = ins[i].shape[1]
            g = g_all[i:i + 1, :n]
            d, nm, nv = _adam_update(g, ins[i][...], ins[nv_ + i][...], ins[2 * nv_ + i][...])
            outs[i][...], outs[nv_ + i][...], outs[2 * nv_ + i][...], outs[3 * nv_ + i][...] = g, d, nm, nv
        outs[-1][...] = g_all[nv_:nv_ + 1, :128]

    shapes = [jax.ShapeDtypeStruct(a.shape, F32) for a in ws]
    res = pl.pallas_call(body, name="adamw_small", out_shape=shapes * 4 + [jax.ShapeDtypeStruct((1, 128), F32)],
                         compiler_params=_params())(srecv, *ws, *ms, *vs)
    return [res[k * nv_:(k + 1) * nv_] for k in range(4)], res[-1]


def _cols_from_shards(a):
    return jnp.swapaxes(a, 0, 1).reshape(a.shape[1], a.shape[0] * a.shape[2])


def _shards_from_cols(a):
    return jnp.swapaxes(a.reshape(a.shape[0], N_DEV, a.shape[1] // N_DEV), 0, 1)


def _shards_from_rows(a):
    return a.reshape(N_DEV, a.shape[0] // N_DEV, a.shape[1])


def _pair_lanes(a):
    lead = a.shape[:-1]
    return a.reshape(lead + (2, 2, HEAD_DIM // 2)).swapaxes(-3, -2).reshape(lead + (128,))


def _split_w_in(w_in):
    rows = w_in.shape[0]
    dil = w_in[:, :3 * DIL_WIDTH].reshape(rows, 3, 3, 4, 128)
    dil = np.concatenate([_pair_lanes(dil[:, :2]), dil[:, 2:]], axis=1)
    dil = dil.transpose(0, 2, 3, 1, 4).reshape(rows, 3 * DIL_WIDTH)
    o = 3 * DIL_WIDTH
    qb = w_in[:, o:o + SWA_Q_WIDTH].reshape(rows, 2, 4, HEAD_DIM).transpose(0, 2, 1, 3).reshape(rows, 4, 128)
    qb = _pair_lanes(qb).reshape(rows, SWA_Q_WIDTH)
    kb = _pair_lanes(w_in[:, o + SWA_Q_WIDTH:o + SWA_Q_WIDTH + SWA_KV_WIDTH])
    vb = w_in[:, o + SWA_Q_WIDTH + SWA_KV_WIDTH:P_WIDTH]
    return np.concatenate([dil, qb, kb, vb], axis=1)


ROW_GRANULE = HEAD_DIM // 2


def _w_p_granules():
    order = _split_w_in(np.arange(IN_WIDTH)[None])[0]
    assert sorted(order.tolist()) == list(range(P_WIDTH))
    first = order[::ROW_GRANULE]
    assert (first % ROW_GRANULE == 0).all() and (order.reshape(-1, ROW_GRANULE) == first[:, None] + np.arange(ROW_GRANULE)).all()
    return first // ROW_GRANULE


def _gather_rows(name, src, granules, per_step):
    n, cols = len(granules), src.shape[1]
    assert n % per_step == 0

    def body(tab_ref, *refs):
        out_ref = refs[per_step]
        for j in range(per_step):
            out_ref[j * ROW_GRANULE:(j + 1) * ROW_GRANULE, :] = refs[j][...]

    def pick(j):
        return pl.BlockSpec((ROW_GRANULE, cols), lambda i, tab_ref: (tab_ref[per_step * i + j], 0))

    return pl.pallas_call(
        body, name=name,
        grid_spec=pltpu.PrefetchScalarGridSpec(
            num_scalar_prefetch=1, grid=(n // per_step,),
            in_specs=[pick(j) for j in range(per_step)],
            out_specs=pl.BlockSpec((per_step * ROW_GRANULE, cols), lambda i, tab_ref: (i, 0))),
        out_shape=jax.ShapeDtypeStruct((n * ROW_GRANULE, cols), src.dtype),
        compiler_params=_params(("arbitrary",)),
    )(jnp.asarray(granules, jnp.int32), *([src] * per_step))


def _swa_rows(w_b):
    return w_b.reshape(2, 4, HEAD_DIM, -1).transpose(1, 0, 2, 3).reshape(SWA_Q_WIDTH, -1)


def _swa_rows_inv(dw_b):
    return dw_b.reshape(4, 2, HEAD_DIM, -1).transpose(1, 0, 2, 3).reshape(SWA_Q_WIDTH, -1)


def _rope_tables(pos):
    half = HEAD_DIM // 2
    inv = ROPE_THETA ** (-jnp.arange(half, dtype=F32) / half)
    ang = pos.astype(F32)[:, None] * jnp.tile(inv, 4)
    sign = jnp.repeat(jnp.array([-1.0, 1.0], F32), 2 * half)
    return jnp.cos(ang), jnp.sin(ang) * sign


def _local_step(x, hs, mem, pos, target, w_in_t, dep, rest_weights, on_grads, g_mix, g_cross, g_mem, g_mlp, g_final, sink):
    t = x.shape[0]
    tm = min(512, t)
    tq = 1024
    tw = min(2048, t)
    w_p_t = _gather_rows("w_p_rows", w_in_t, _w_p_granules(), 24)
    cos, sin = lax.optimization_barrier(_rope_tables(pos))
    sink_row = jnp.pad(sink.reshape(2, 4).T.reshape(1, 8), ((0, 0), (0, 120)))
    tabs = [(cos[None], sin[None])]
    for _, d in DIL_GROUPS[1:]:
        tabs.append(tuple(a.reshape(t // d, d, 128).swapaxes(0, 1) for a in (cos, sin)))
    tabs.append(tabs[0])

    h, h1, h2 = hs
    p0, p1, p2, pb = _inproj(h, h1, h2, w_p_t, [(cos, sin), tabs[1], tabs[2]], tm, dep)
    ps = [p0[None], p1, p2, pb[None]]
    outs, lses = [], []
    for gi, pv in enumerate(ps):
        res = _attn_fwd(f"attn_fwd{gi}", pv, gi == 3, sink_row[0, :8], min(tq, pv.shape[1]))
        outs.append(res[0])
        lses.append(res[1])
    o0, l0, ob, lb, ob32 = outs[0][0], lses[0][0], outs[3][0], lses[3][0], res[2][0]
    wts = rest_weights(GROUP_B, lb)
    w_b = _swa_rows(wts["w_branch_b"])
    tf = 2048
    w_g_t = wts["w_g_t"]
    gts = _gates(h, w_g_t, wts["b_gate"].reshape(1, GATE_WIDTH), min(1024, t), 1024)
    oa, ya, yb, merged, x1, hc = _mix(o0, outs[1], outs[2], l0, lses[1], lses[2], ob, gts, x,
                                      wts["w_branch_a"], w_b, wts["w_out"], g_cross, tm)
    mn, kv = _memkv(mem, g_mem, wts["w_ckv"])
    q, o, x2, hm = _cross(hc, x1, kv, wts["w_cq"], wts["w_co"], g_mlp, tm)
    wts.update(rest_weights(GROUP_A, hm))
    a, dx3, loss, dg_final = _mlp(hm, x2, wts["w_1"], wts["w_2"], g_final.reshape(1, D_MODEL), target, tm, tf)

    grads = {}
    dz, dx2, dg_mlp = _mlp_bwd(dx3, a, wts["w_1"], wts["w_2"], x2, g_mlp, tm, tf)
    grads["w_2"] = _shards_from_rows(_wgrad("dw_2", a, dx3, 1024, 1024, tw, square=True))
    grads["w_1"] = _wgrad("dw_1", hm, dz, 1024, 1024, tw, col_shards=True)
    dep = on_grads(GROUP_A, grads)
    dq, dx1, dkv, dg_cross = _cross_bwd(dx2, x1, q, kv, wts["w_cq"], wts["w_co"], g_cross, tm, dep)
    grads["w_co"] = _shards_from_rows(_wgrad("dw_co", o, dx2, 1024, 1024, tw))
    grads["w_cq"] = _shards_from_rows(_wgrad("dw_cq", hc, dq, 1024, 1024, tw))
    grads["w_ckv"], dg_mem = _memkv_bwd(dkv, mn, mem, wts["w_ckv"], g_mem)
    dgt, dh_part, dya, dyb, db_gate = _merge_bwd(dx1, ya, yb, gts, wts["w_out"], w_g_t, tm)
    do0, do1, do2, c0, c1, c2, dob, cb, dsink = _combine_bwd(
        dya, dyb, oa, ob32, l0, lses[1], lses[2], lb, sink_row, wts["w_branch_a"], w_b, tm)
    grads["w_out"] = _shards_from_rows(_wgrad("dw_out", merged, dx1, 1024, 1024, tw))
    grads["w_branch_a"] = _shards_from_cols(_wgrad("dw_a", oa, dya, 512, 1024, tw))
    grads["w_branch_b"] = _shards_from_cols(_swa_rows_inv(_wgrad("dw_b", ob, dyb, 512, 1024, tw)))
    grads["b_gate"] = _shards_from_cols(db_gate.reshape(2, D_MODEL)).astype(BF16)
    dep = on_grads(GROUP_B, grads)
    dw_g_t = _wgrad("dw_g", dgt, h, 1024, 1024, tw)
    dps = []
    for gi, (pv, do_g, c_g) in enumerate(zip(ps, (do0[None], do1, do2, dob[None]), (c0[None], c1, c2, cb[None]))):
        dps.append(_attn_bwd(f"attn_bwd{gi}", pv, do_g, lses[gi], c_g, tabs[gi][0], tabs[gi][1], gi == 3,
                             min(tq, pv.shape[1]),
                             dep if gi == 0 else None))
    dw_p_t = [_wgrad(f"dw_p{gi}", dpg.reshape(t, -1), hh.reshape(t, D_MODEL), PBLK, 1024, tw)
              for gi, (hh, dpg) in enumerate(zip((h, h1, h2, h), dps))]
    back = np.concatenate([np.argsort(_w_p_granules()),
                           np.arange(P_WIDTH // ROW_GRANULE, IN_WIDTH // ROW_GRANULE)])
    dw_in_t = _gather_rows("dw_in_rows", jnp.concatenate(dw_p_t + [dw_g_t], axis=0), back, 29)
    grads["w_in"] = dw_in_t.reshape(N_DEV, IN_WIDTH // N_DEV, D_MODEL)
    dep = on_grads(GROUP_C, grads)
    grad_x, dg_mix = _dx(dps[0][0], dps[1], dps[2], dps[3][0], w_p_t, dh_part, dx1, x, g_mix, tm, dep)
    dsink_heads = dsink[0, :8].reshape(4, 2).T.reshape(8)
    small = {"g_mix": dg_mix[0], "g_cross": dg_cross[0], "g_mem": dg_mem[0], "g_mlp": dg_mlp[0],
             "g_final": dg_final[0], "sink": dsink_heads}
    return loss[0, 0], grad_x, small


def kernel(x, mem, positions, g_mix, w_in, b_gate, sink, w_branch_a, w_branch_b, w_out, g_cross, g_mem, w_cq, w_ckv, w_co, g_mlp, w_1, w_2, g_final, loss_target, m_g_mix, m_w_in, m_b_gate, m_sink, m_w_branch_a, m_w_branch_b, m_w_out, m_g_cross, m_g_mem, m_w_cq, m_w_ckv, m_w_co, m_g_mlp, m_w_1, m_w_2, m_g_final, v_g_mix, v_w_in, v_b_gate, v_sink, v_w_branch_a, v_w_branch_b, v_w_out, v_g_cross, v_g_mem, v_w_cq, v_w_ckv, v_w_co, v_g_mlp, v_w_1, v_w_2, v_g_final):
    local = dict(locals())
    shard = {n: local[n][0] for n in GROUP_A + GROUP_B + GROUP_C}
    me = _my_index()
    me_arr = me.reshape(1).astype(jnp.int32)
    tags = {GROUP_A: "a", GROUP_B: "b", GROUP_C: "c"}

    transposed = lambda a: jnp.swapaxes(a, 0, 1)
    gathered_w_in, *hs = _all_gather(transposed(shard["w_in"]).astype(BF16), x[0], g_mix, min(512, x.shape[1]))
    w_in_t = gathered_w_in.reshape(-1, gathered_w_in.shape[2])

    def gathered(name, started, after):
        srcs, lands = _split_wait(name, True, started, after)
        return [lax.dynamic_update_slice(land, src[None], (me,) + (0,) * src.ndim) for src, land in zip(srcs, lands)]

    def start_gather(names, after=None):
        return _split_start("gather_start_" + tags[names], True,
                            [shard[n] if n == "b_gate" else shard[n].astype(BF16) for n in names], after)

    gathers = {GROUP_B: start_gather(GROUP_B)}
    gathers[GROUP_A] = start_gather(GROUP_A, gathers[GROUP_B][-1])

    def rest_weights(names, after):
        full = {"w_g_t": w_in_t[P_WIDTH:]} if names == GROUP_B else {}
        for name, a in zip(names, gathered("gather_wait_" + tags[names], gathers[names], after)):
            if name in ("w_1", "w_ckv"):
                full[name] = a
            elif name in _COL_SHARDED:
                full[name] = _cols_from_shards(a)
            else:
                full[name] = a.reshape(N_DEV * a.shape[1], a.shape[2])
        return full

    scatters = {}

    def on_grads(names, grads):
        scatters[names] = _split_start("scatter_start_" + tags[names], False, [grads[n] for n in names])
        return scatters[names][-1]

    loss, grad_x, small = _local_step(
        x[0], hs, mem[0], positions[0], loss_target[0], w_in_t, gathers[GROUP_A][-1], rest_weights, on_grads,
        g_mix, g_cross, g_mem, g_mlp, g_final, sink[0])

    sp = jnp.stack([small[n] if n != "sink" else jnp.pad(small[n], (0, LANES - 8)) for n in SMALL]
                   + [jnp.pad(loss.reshape(1), (0, LANES - 1)), jnp.zeros((LANES,), F32)])
    small_gather = _split_start("small_start", True, [sp])

    after, updated = small_gather[-1], {}
    for names in (GROUP_A, GROUP_B, GROUP_C):
        sent, got = _split_wait("scatter_wait_" + tags[names], False, scatters[names], after)
        for i, name in enumerate(names):
            view = transposed if name == "w_in" else (lambda a: a)
            outs = _adamw("adamw_" + name, me_arr, sent[i], got[i], view(shard[name]),
                          view(local["m_" + name][0]), view(local["v_" + name][0]), ADAM_ROWS[name], ADAM_COLS.get(name))
            updated[name] = [view(a)[None] for a in outs]
            after = outs[3]

    flat = lambda prefix: [local[prefix + n].reshape(1, -1) for n in SMALL]
    outs, loss_row = _adamw_small(gathered("small_wait", small_gather, after)[0], flat(""), flat("m_"), flat("v_"))
    for i, name in enumerate(SMALL):
        updated[name] = [outs[which][i].reshape(local[name].shape) for which in range(4)]

    order = ["g_mix", "w_in", "b_gate", "sink", "w_branch_a", "w_branch_b", "w_out", "g_cross", "g_mem", "w_cq",
             "w_ckv", "w_co", "g_mlp", "w_1", "w_2", "g_final"]
    res = [loss_row[0, 0], grad_x[None]]
    for which in range(4):
        res += [updated[n][which] for n in order]
    return tuple(res)
```
